```python
import math
import jax, jax.numpy as jnp
from jax import lax
import numpy as np

D_MODEL = 1024
BATCH = 8
SEQ = 8192
DEPTH = 2

N_META = 16
BLOCK = 128
PAD = BLOCK - N_META

D_RNN = D_MODEL
LRU_BLOCKS = 8
LRU_BS = D_RNN // LRU_BLOCKS
LRU_C = 8.0
CONV_A = 4

N_Q_HEADS = 16
N_KV_HEADS = 2
HEAD_DIM = 64
Q_PER_KV = N_Q_HEADS // N_KV_HEADS
WINDOW = 128
Q_DIM = N_Q_HEADS * HEAD_DIM
KV_DIM = N_KV_HEADS * HEAD_DIM

EVEN_IN = 2 * D_RNN + Q_DIM + 2 * KV_DIM
EVEN_MIX = D_RNN + Q_DIM

D_SSM = 2 * D_MODEL
SSD_HEADDIM = 64
SSD_HEADS = D_SSM // SSD_HEADDIM
SSD_GROUPS = 8
SSD_HPG = SSD_HEADS // SSD_GROUPS
SSD_STATE = 128
CONV_C = 4
SSD_CONV_DIM = D_SSM + 2 * SSD_GROUPS * SSD_STATE
ODD_IN = D_SSM + SSD_CONV_DIM + SSD_HEADS

D_FF = 2816
CONV_F = 3

EPS = 1e-6

kernel_name = "hybrid_rglru_swa_sink_ssd_convffn"


def rms_norm(x, w):
    x32 = x.astype(jnp.float32)
    y = x32 * lax.rsqrt(jnp.mean(x32 * x32, axis=-1, keepdims=True) + EPS)
    return (y * w.astype(jnp.float32)).astype(x.dtype)


def causal_dwconv(x, w, b):
    k = w.shape[0]
    y = lax.conv_general_dilated(
        x, w[:, None, :].astype(x.dtype), window_strides=(1,), padding=[(k - 1, 0)],
        dimension_numbers=("NWC", "WIO", "NWC"), feature_group_count=x.shape[-1])
    return y + b.astype(x.dtype)


def alibi_slopes(n_heads):
    return 2.0 ** (-8.0 * jnp.arange(1, n_heads + 1, dtype=jnp.float32) / n_heads)


def rg_lru(x, w_a, b_a, w_x, b_x, lam):
    bsz, L, _ = x.shape
    x32 = x.astype(jnp.float32)
    xb = x32.reshape(bsz, L, LRU_BLOCKS, LRU_BS)
    r = jax.nn.sigmoid(jnp.einsum("blni,nij->blnj", xb, w_a.astype(jnp.float32)).reshape(bsz, L, D_RNN) + b_a)
    i = jax.nn.sigmoid(jnp.einsum("blni,nij->blnj", xb, w_x.astype(jnp.float32)).reshape(bsz, L, D_RNN) + b_x)
    log_a = -LRU_C * r * jax.nn.softplus(-lam.astype(jnp.float32))
    a = jnp.exp(log_a)
    u = jnp.sqrt(-jnp.expm1(2.0 * log_a)) * (i * x32)

    def combine(c1, c2):
        a1, b1 = c1
        a2, b2 = c2
        return a1 * a2, a2 * b1 + b2

    _, h = lax.associative_scan(combine, (a, u), axis=1)
    return h.astype(x.dtype)


def swa_sink_alibi(q, k, v, sinks):
    bsz, L = q.shape[:2]
    Lp = L + PAD
    nblk = Lp // BLOCK
    q = q.astype(jnp.float32)
    k = k.astype(jnp.float32)
    v = v.astype(jnp.float32)
    padw = ((0, 0), (PAD, 0), (0, 0), (0, 0))
    qb = jnp.pad(q, padw).reshape(bsz, nblk, BLOCK, N_KV_HEADS, Q_PER_KV, HEAD_DIM)
    kb = jnp.pad(k, padw).reshape(bsz, nblk, BLOCK, N_KV_HEADS, HEAD_DIM)
    vb = jnp.pad(v, padw).reshape(bsz, nblk, BLOCK, N_KV_HEADS, HEAD_DIM)
    shift = ((0, 0), (1, 0), (0, 0), (0, 0), (0, 0))
    k_band = jnp.concatenate([jnp.pad(kb, shift)[:, :-1], kb], axis=2)
    v_band = jnp.concatenate([jnp.pad(vb, shift)[:, :-1], vb], axis=2)
    k_meta = k[:, :N_META]
    v_meta = v[:, :N_META]
    scale = HEAD_DIM ** -0.5
    s_band = jnp.einsum("bnqkgd,bnskd->bnkgqs", qb, k_band) * scale
    s_meta = jnp.einsum("bnqkgd,bmkd->bnkgqm", qb, k_meta) * scale

    blk = jnp.arange(nblk)
    t = blk[:, None] * BLOCK + jnp.arange(BLOCK)[None, :] - PAD
    s = (blk[:, None] - 1) * BLOCK + jnp.arange(2 * BLOCK)[None, :] - PAD
    dist_band = t[:, :, None] - s[:, None, :]
    band_ok = (s[:, None, :] >= N_META) & (dist_band >= 0) & (dist_band < WINDOW)
    dist_meta = t[:, :, None] - jnp.arange(N_META)[None, None, :]
    meta_ok = dist_meta >= 0

    slopes = alibi_slopes(N_Q_HEADS).reshape(N_KV_HEADS, Q_PER_KV)[:, :, None, None]
    pen_band = slopes * dist_band[:, None, None].astype(jnp.float32)
    pen_meta = slopes * jnp.minimum(dist_meta, WINDOW)[:, None, None].astype(jnp.float32)
    s_band = jnp.where(band_ok[:, None, None], s_band - pen_band, -jnp.inf)
    s_meta = jnp.where(meta_ok[:, None, None], s_meta - pen_meta, -jnp.inf)

    sink = sinks.astype(jnp.float32).reshape(N_KV_HEADS, Q_PER_KV)[:, :, None, None]
    mx = jnp.maximum(jnp.maximum(s_band.max(-1, keepdims=True), s_meta.max(-1, keepdims=True)), sink)
    p_band = jnp.exp(s_band - mx)
    p_meta = jnp.exp(s_meta - mx)
    denom = p_band.sum(-1, keepdims=True) + p_meta.sum(-1, keepdims=True) + jnp.exp(sink - mx)
    p_band = p_band / denom
    p_meta = p_meta / denom
    o = (jnp.einsum("bnkgqs,bnskd->bnqkgd", p_band, v_band)
         + jnp.einsum("bnkgqm,bmkd->bnqkgd", p_meta, v_meta))
    return o.reshape(bsz, Lp, Q_DIM)[:, PAD:]


def griffin_swa_mixer(u, w_in, conv_w, conv_b, w_a, b_a, w_x, b_x, lam, sinks, w_out):
    bsz, L, _ = u.shape
    proj = u @ w_in
    gate, xr, q, k, v = jnp.split(
        proj, [D_RNN, 2 * D_RNN, 2 * D_RNN + Q_DIM, 2 * D_RNN + Q_DIM + KV_DIM], axis=-1)
    xr = causal_dwconv(xr, conv_w, conv_b)
    y_a = jax.nn.gelu(gate, approximate=True) * rg_lru(xr, w_a, b_a, w_x, b_x, lam)
    y_b = swa_sink_alibi(q.reshape(bsz, L, N_Q_HEADS, HEAD_DIM),
                         k.reshape(bsz, L, N_KV_HEADS, HEAD_DIM),
                         v.reshape(bsz, L, N_KV_HEADS, HEAD_DIM), sinks).astype(u.dtype)
    return jnp.concatenate([y_a, y_b], axis=-1) @ w_out


def ssd_chunked(x, dt, a, b_in, c_in):
    bsz, Lp = x.shape[:2]
    nc = Lp // BLOCK
    x = x.reshape(bsz, nc, BLOCK, SSD_GROUPS, SSD_HPG, SSD_HEADDIM)
    dt = dt.reshape(bsz, nc, BLOCK, SSD_GROUPS, SSD_HPG)
    bc = b_in.reshape(bsz, nc, BLOCK, SSD_GROUPS, SSD_STATE)
    cc = c_in.reshape(bsz, nc, BLOCK, SSD_GROUPS, SSD_STATE)
    cs = jnp.cumsum(dt * a.reshape(SSD_GROUPS, SSD_HPG), axis=2)
    xdt = x * dt[..., None]
    cs_t = jnp.moveaxis(cs, 2, -1)
    seg = cs_t[..., :, None] - cs_t[..., None, :]
    tril = jnp.tril(jnp.ones((BLOCK, BLOCK), dtype=bool))
    decay_in = jnp.exp(jnp.where(tril, seg, -jnp.inf))
    cb = jnp.einsum("bclgn,bcsgn->bcgls", cc, bc)
    y_diag = jnp.einsum("bcgls,bcghls,bcsghp->bclghp", cb, decay_in, xdt)
    cs_last = cs[:, :, -1:]
    chunk_states = jnp.einsum("bclgn,bclgh,bclghp->bcghpn", bc, jnp.exp(cs_last - cs), xdt)
    chunk_decay = jnp.exp(cs_last[:, :, 0])

    def step(state, inp):
        dec, st = inp
        return state * dec[..., None, None] + st, state

    init = jnp.zeros_like(chunk_states[:, 0])
    _, prev = lax.scan(step, init, (jnp.moveaxis(chunk_decay, 1, 0), jnp.moveaxis(chunk_states, 1, 0)))
    prev = jnp.moveaxis(prev, 0, 1)
    y_off = jnp.einsum("bclgn,bcghpn,bclgh->bclghp", cc, prev, jnp.exp(cs))
    return (y_diag + y_off).reshape(bsz, Lp, SSD_HEADS, SSD_HEADDIM)


def mamba2_mixer(u, w_in, conv_w, conv_b, dt_bias, a_log, d_skip, gate_norm, w_out):
    bsz, L, _ = u.shape
    proj = u @ w_in
    z, xbc, dt = jnp.split(proj, [D_SSM, D_SSM + SSD_CONV_DIM], axis=-1)
    xbc = jax.nn.silu(causal_dwconv(xbc, conv_w, conv_b))
    xs, bs, cs = jnp.split(xbc, [D_SSM, D_SSM + SSD_GROUPS * SSD_STATE], axis=-1)
    xs = xs.reshape(bsz, L, SSD_HEADS, SSD_HEADDIM).astype(jnp.float32)
    bs = bs.reshape(bsz, L, SSD_GROUPS, SSD_STATE).astype(jnp.float32)
    cs = cs.reshape(bsz, L, SSD_GROUPS, SSD_STATE).astype(jnp.float32)
    dt = jax.nn.softplus(dt.astype(jnp.float32) + dt_bias.astype(jnp.float32))
    a = -jnp.exp(a_log.astype(jnp.float32))

    def front_pad(t):
        return jnp.pad(t, ((0, 0), (PAD, 0)) + ((0, 0),) * (t.ndim - 2))

    y = ssd_chunked(front_pad(xs), front_pad(dt), a, front_pad(bs), front_pad(cs))[:, PAD:]
    y = y + d_skip.astype(jnp.float32)[:, None] * xs
    y = y.reshape(bsz, L, D_SSM) * jax.nn.silu(z.astype(jnp.float32))
    yg = y.reshape(bsz, L, SSD_GROUPS, D_SSM // SSD_GROUPS)
    yg = yg * lax.rsqrt(jnp.mean(yg * yg, axis=-1, keepdims=True) + EPS)
    y = yg.reshape(bsz, L, D_SSM) * gate_norm.astype(jnp.float32)
    return y.astype(u.dtype) @ w_out


def conv_ffn(u, w_up, conv_w, conv_b, w_down):
    h = causal_dwconv(u @ w_up, conv_w, conv_b)
    g, up = jnp.split(h, 2, axis=-1)
    return (jax.nn.gelu(g, approximate=True) * up) @ w_down


def _fwd_setup_inputs(seed: int = 0) -> dict:
    key = jax.random.key(seed)
    k = jax.random.split(key, 36)

    def normal(kk, shape, scale):
        return jax.random.normal(kk, shape, jnp.float32) * scale

    def gain(kk, n):
        return 1.0 + 0.05 * jax.random.normal(kk, (n,), jnp.float32)

    u = jax.random.uniform(k[11], (D_RNN,), jnp.float32, 0.9, 0.999)
    a0 = u ** (1.0 / LRU_C)
    lru_lambda = jnp.log(a0) - jnp.log1p(-a0)
    dt0 = jnp.exp(jax.random.uniform(k[25], (SSD_HEADS,), jnp.float32, math.log(1e-3), math.log(1e-1)))
    dt_bias = dt0 + jnp.log(-jnp.expm1(-dt0))
    a_log = jnp.log(jax.random.uniform(k[26], (SSD_HEADS,), jnp.float32, 1.0, 16.0))

    return {
        "x": normal(k[0], (BATCH, SEQ, D_MODEL), 1.0),
        "meta_tokens": normal(k[1], (N_META, D_MODEL), 1.0),
        "l0_mix_pre_norm": gain(k[2], D_MODEL),
        "l0_mix_post_norm": gain(k[3], D_MODEL),
        "l0_w_in": normal(k[4], (D_MODEL, EVEN_IN), D_MODEL ** -0.5),
        "l0_lru_conv_w": normal(k[5], (CONV_A, D_RNN), CONV_A ** -0.5),
        "l0_lru_conv_b": normal(k[6], (D_RNN,), 0.01),
        "l0_lru_w_a": normal(k[7], (LRU_BLOCKS, LRU_BS, LRU_BS), LRU_BS ** -0.5),
        "l0_lru_b_a": normal(k[8], (D_RNN,), 0.01),
        "l0_lru_w_x": normal(k[9], (LRU_BLOCKS, LRU_BS, LRU_BS), LRU_BS ** -0.5),
        "l0_lru_b_x": normal(k[10], (D_RNN,), 0.01),
        "l0_lru_lambda": lru_lambda,
        "l0_attn_sinks": normal(k[12], (N_Q_HEADS,), 0.5),
        "l0_w_out": normal(k[13], (EVEN_MIX, D_MODEL), EVEN_MIX ** -0.5),
        "l0_ffn_pre_norm": gain(k[14], D_MODEL),
        "l0_ffn_post_norm": gain(k[15], D_MODEL),
        "l0_ffn_w_up": normal(k[16], (D_MODEL, 2 * D_FF), D_MODEL ** -0.5),
        "l0_ffn_conv_w": normal(k[17], (CONV_F, 2 * D_FF), CONV_F ** -0.5),
        "l0_ffn_conv_b": normal(k[18], (2 * D_FF,), 0.01),
        "l0_ffn_w_down": normal(k[19], (D_FF, D_MODEL), D_FF ** -0.5),
        "l1_mix_pre_norm": gain(k[20], D_MODEL),
        "l1_mix_post_norm": gain(k[21], D_MODEL),
        "l1_w_in": normal(k[22], (D_MODEL, ODD_IN), D_MODEL ** -0.5),
        "l1_ssm_conv_w": normal(k[23], (CONV_C, SSD_CONV_DIM), CONV_C ** -0.5),
        "l1_ssm_conv_b": normal(k[24], (SSD_CONV_DIM,), 0.01),
        "l1_dt_bias": dt_bias,
        "l1_a_log": a_log,
        "l1_d_skip": 1.0 + 0.1 * jax.random.normal(k[27], (SSD_HEADS,), jnp.float32),
        "l1_gate_norm": gain(k[28], D_SSM),
        "l1_w_out": normal(k[29], (D_SSM, D_MODEL), D_SSM ** -0.5),
        "l1_ffn_pre_norm": gain(k[30], D_MODEL),
        "l1_ffn_post_norm": gain(k[31], D_MODEL),
        "l1_ffn_w_up": normal(k[32], (D_MODEL, 2 * D_FF), D_MODEL ** -0.5),
        "l1_ffn_conv_w": normal(k[33], (CONV_F, 2 * D_FF), CONV_F ** -0.5),
        "l1_ffn_conv_b": normal(k[34], (2 * D_FF,), 0.01),
        "l1_ffn_w_down": normal(k[35], (D_FF, D_MODEL), D_FF ** -0.5),
    }


def _fwd_reference(x, meta_tokens,
              l0_mix_pre_norm, l0_mix_post_norm, l0_w_in, l0_lru_conv_w, l0_lru_conv_b,
              l0_lru_w_a, l0_lru_b_a, l0_lru_w_x, l0_lru_b_x, l0_lru_lambda, l0_attn_sinks, l0_w_out,
              l0_ffn_pre_norm, l0_ffn_post_norm, l0_ffn_w_up, l0_ffn_conv_w, l0_ffn_conv_b, l0_ffn_w_down,
              l1_mix_pre_norm, l1_mix_post_norm, l1_w_in, l1_ssm_conv_w, l1_ssm_conv_b,
              l1_dt_bias, l1_a_log, l1_d_skip, l1_gate_norm, l1_w_out,
              l1_ffn_pre_norm, l1_ffn_post_norm, l1_ffn_w_up, l1_ffn_conv_w, l1_ffn_conv_b, l1_ffn_w_down):
    bsz = x.shape[0]
    meta = jnp.broadcast_to(meta_tokens.astype(x.dtype)[None], (bsz, N_META, D_MODEL))
    h = jnp.concatenate([meta, x], axis=1)
    layers = [
        (l0_mix_pre_norm, l0_mix_post_norm,
         (l0_w_in, l0_lru_conv_w, l0_lru_conv_b, l0_lru_w_a, l0_lru_b_a, l0_lru_w_x, l0_lru_b_x,
          l0_lru_lambda, l0_attn_sinks, l0_w_out),
         (l0_ffn_pre_norm, l0_ffn_post_norm, l0_ffn_w_up, l0_ffn_conv_w, l0_ffn_conv_b, l0_ffn_w_down)),
        (l1_mix_pre_norm, l1_mix_post_norm,
         (l1_w_in, l1_ssm_conv_w, l1_ssm_conv_b, l1_dt_bias, l1_a_log, l1_d_skip, l1_gate_norm, l1_w_out),
         (l1_ffn_pre_norm, l1_ffn_post_norm, l1_ffn_w_up, l1_ffn_conv_w, l1_ffn_conv_b, l1_ffn_w_down)),
    ]
    for i in range(DEPTH):
        pre, post, mix, ffn = layers[i]
        mixer = griffin_swa_mixer if i % 2 == 0 else mamba2_mixer
        h = h + rms_norm(mixer(rms_norm(h, pre), *mix), post)
        f_pre, f_post, w_up, c_w, c_b, w_down = ffn
        h = h + rms_norm(conv_ffn(rms_norm(h, f_pre), w_up, c_w, c_b, w_down), f_post)
    return h[:, N_META:]


import jax as _jax
import jax.numpy as _jnp

TWIN_FORMAT = 'train_step'
FWD_PARAMS = ['x', 'meta_tokens', 'l0_mix_pre_norm', 'l0_mix_post_norm', 'l0_w_in', 'l0_lru_conv_w', 'l0_lru_conv_b', 'l0_lru_w_a', 'l0_lru_b_a', 'l0_lru_w_x', 'l0_lru_b_x', 'l0_lru_lambda', 'l0_attn_sinks', 'l0_w_out', 'l0_ffn_pre_norm', 'l0_ffn_post_norm', 'l0_ffn_w_up', 'l0_ffn_conv_w', 'l0_ffn_conv_b', 'l0_ffn_w_down', 'l1_mix_pre_norm', 'l1_mix_post_norm', 'l1_w_in', 'l1_ssm_conv_w', 'l1_ssm_conv_b', 'l1_dt_bias', 'l1_a_log', 'l1_d_skip', 'l1_gate_norm', 'l1_w_out', 'l1_ffn_pre_norm', 'l1_ffn_post_norm', 'l1_ffn_w_up', 'l1_ffn_conv_w', 'l1_ffn_conv_b', 'l1_ffn_w_down']
TWIN_WEIGHTS = ['meta_tokens', 'l0_mix_pre_norm', 'l0_mix_post_norm', 'l0_w_in', 'l0_lru_conv_w', 'l0_lru_conv_b', 'l0_lru_w_a', 'l0_lru_b_a', 'l0_lru_w_x', 'l0_lru_b_x', 'l0_lru_lambda', 'l0_attn_sinks', 'l0_w_out', 'l0_ffn_pre_norm', 'l0_ffn_post_norm', 'l0_ffn_w_up', 'l0_ffn_conv_w', 'l0_ffn_conv_b', 'l0_ffn_w_down', 'l1_mix_pre_norm', 'l1_mix_post_norm', 'l1_w_in', 'l1_ssm_conv_w', 'l1_ssm_conv_b', 'l1_dt_bias', 'l1_a_log', 'l1_d_skip', 'l1_gate_norm', 'l1_w_out', 'l1_ffn_pre_norm', 'l1_ffn_post_norm', 'l1_ffn_w_up', 'l1_ffn_conv_w', 'l1_ffn_conv_b', 'l1_ffn_w_down']
TWIN_DIFF_INPUT = 'x'
TWIN_INPUTS = ['x', 'meta_tokens', 'l0_mix_pre_norm', 'l0_mix_post_norm', 'l0_w_in', 'l0_lru_conv_w', 'l0_lru_conv_b', 'l0_lru_w_a', 'l0_lru_b_a', 'l0_lru_w_x', 'l0_lru_b_x', 'l0_lru_lambda', 'l0_attn_sinks', 'l0_w_out', 'l0_ffn_pre_norm', 'l0_ffn_post_norm', 'l0_ffn_w_up', 'l0_ffn_conv_w', 'l0_ffn_conv_b', 'l0_ffn_w_down', 'l1_mix_pre_norm', 'l1_mix_post_norm', 'l1_w_in', 'l1_ssm_conv_w', 'l1_ssm_conv_b', 'l1_dt_bias', 'l1_a_log', 'l1_d_skip', 'l1_gate_norm', 'l1_w_out', 'l1_ffn_pre_norm', 'l1_ffn_post_norm', 'l1_ffn_w_up', 'l1_ffn_conv_w', 'l1_ffn_conv_b', 'l1_ffn_w_down', 'loss_target', 'm_meta_tokens', 'm_l0_mix_pre_norm', 'm_l0_mix_post_norm', 'm_l0_w_in', 'm_l0_lru_conv_w', 'm_l0_lru_conv_b', 'm_l0_lru_w_a', 'm_l0_lru_b_a', 'm_l0_lru_w_x', 'm_l0_lru_b_x', 'm_l0_lru_lambda', 'm_l0_attn_sinks', 'm_l0_w_out', 'm_l0_ffn_pre_norm', 'm_l0_ffn_post_norm', 'm_l0_ffn_w_up', 'm_l0_ffn_conv_w', 'm_l0_ffn_conv_b', 'm_l0_ffn_w_down', 'm_l1_mix_pre_norm', 'm_l1_mix_post_norm', 'm_l1_w_in', 'm_l1_ssm_conv_w', 'm_l1_ssm_conv_b', 'm_l1_dt_bias', 'm_l1_a_log', 'm_l1_d_skip', 'm_l1_gate_norm', 'm_l1_w_out', 'm_l1_ffn_pre_norm', 'm_l1_ffn_post_norm', 'm_l1_ffn_w_up', 'm_l1_ffn_conv_w', 'm_l1_ffn_conv_b', 'm_l1_ffn_w_down', 'v_meta_tokens', 'v_l0_mix_pre_norm', 'v_l0_mix_post_norm', 'v_l0_w_in', 'v_l0_lru_conv_w', 'v_l0_lru_conv_b', 'v_l0_lru_w_a', 'v_l0_lru_b_a', 'v_l0_lru_w_x', 'v_l0_lru_b_x', 'v_l0_lru_lambda', 'v_l0_attn_sinks', 'v_l0_w_out', 'v_l0_ffn_pre_norm', 'v_l0_ffn_post_norm', 'v_l0_ffn_w_up', 'v_l0_ffn_conv_w', 'v_l0_ffn_conv_b', 'v_l0_ffn_w_down', 'v_l1_mix_pre_norm', 'v_l1_mix_post_norm', 'v_l1_w_in', 'v_l1_ssm_conv_w', 'v_l1_ssm_conv_b', 'v_l1_dt_bias', 'v_l1_a_log', 'v_l1_d_skip', 'v_l1_gate_norm', 'v_l1_w_out', 'v_l1_ffn_pre_norm', 'v_l1_ffn_post_norm', 'v_l1_ffn_w_up', 'v_l1_ffn_conv_w', 'v_l1_ffn_conv_b', 'v_l1_ffn_w_down']
TWIN_OUTPUTS = ['loss', 'grad_x', 'grad_meta_tokens', 'grad_l0_mix_pre_norm', 'grad_l0_mix_post_norm', 'grad_l0_w_in', 'grad_l0_lru_conv_w', 'grad_l0_lru_conv_b', 'grad_l0_lru_w_a', 'grad_l0_lru_b_a', 'grad_l0_lru_w_x', 'grad_l0_lru_b_x', 'grad_l0_lru_lambda', 'grad_l0_attn_sinks', 'grad_l0_w_out', 'grad_l0_ffn_pre_norm', 'grad_l0_ffn_post_norm', 'grad_l0_ffn_w_up', 'grad_l0_ffn_conv_w', 'grad_l0_ffn_conv_b', 'grad_l0_ffn_w_down', 'grad_l1_mix_pre_norm', 'grad_l1_mix_post_norm', 'grad_l1_w_in', 'grad_l1_ssm_conv_w', 'grad_l1_ssm_conv_b', 'grad_l1_dt_bias', 'grad_l1_a_log', 'grad_l1_d_skip', 'grad_l1_gate_norm', 'grad_l1_w_out', 'grad_l1_ffn_pre_norm', 'grad_l1_ffn_post_norm', 'grad_l1_ffn_w_up', 'grad_l1_ffn_conv_w', 'grad_l1_ffn_conv_b', 'grad_l1_ffn_w_down', 'delta_meta_tokens', 'delta_l0_mix_pre_norm', 'delta_l0_mix_post_norm', 'delta_l0_w_in', 'delta_l0_lru_conv_w', 'delta_l0_lru_conv_b', 'delta_l0_lru_w_a', 'delta_l0_lru_b_a', 'delta_l0_lru_w_x', 'delta_l0_lru_b_x', 'delta_l0_lru_lambda', 'delta_l0_attn_sinks', 'delta_l0_w_out', 'delta_l0_ffn_pre_norm', 'delta_l0_ffn_post_norm', 'delta_l0_ffn_w_up', 'delta_l0_ffn_conv_w', 'delta_l0_ffn_conv_b', 'delta_l0_ffn_w_down', 'delta_l1_mix_pre_norm', 'delta_l1_mix_post_norm', 'delta_l1_w_in', 'delta_l1_ssm_conv_w', 'delta_l1_ssm_conv_b', 'delta_l1_dt_bias', 'delta_l1_a_log', 'delta_l1_d_skip', 'delta_l1_gate_norm', 'delta_l1_w_out', 'delta_l1_ffn_pre_norm', 'delta_l1_ffn_post_norm', 'delta_l1_ffn_w_up', 'delta_l1_ffn_conv_w', 'delta_l1_ffn_conv_b', 'delta_l1_ffn_w_down', 'new_m_meta_tokens', 'new_m_l0_mix_pre_norm', 'new_m_l0_mix_post_norm', 'new_m_l0_w_in', 'new_m_l0_lru_conv_w', 'new_m_l0_lru_conv_b', 'new_m_l0_lru_w_a', 'new_m_l0_lru_b_a', 'new_m_l0_lru_w_x', 'new_m_l0_lru_b_x', 'new_m_l0_lru_lambda', 'new_m_l0_attn_sinks', 'new_m_l0_w_out', 'new_m_l0_ffn_pre_norm', 'new_m_l0_ffn_post_norm', 'new_m_l0_ffn_w_up', 'new_m_l0_ffn_conv_w', 'new_m_l0_ffn_conv_b', 'new_m_l0_ffn_w_down', 'new_m_l1_mix_pre_norm', 'new_m_l1_mix_post_norm', 'new_m_l1_w_in', 'new_m_l1_ssm_conv_w', 'new_m_l1_ssm_conv_b', 'new_m_l1_dt_bias', 'new_m_l1_a_log', 'new_m_l1_d_skip', 'new_m_l1_gate_norm', 'new_m_l1_w_out', 'new_m_l1_ffn_pre_norm', 'new_m_l1_ffn_post_norm', 'new_m_l1_ffn_w_up', 'new_m_l1_ffn_conv_w', 'new_m_l1_ffn_conv_b', 'new_m_l1_ffn_w_down', 'new_v_meta_tokens', 'new_v_l0_mix_pre_norm', 'new_v_l0_mix_post_norm', 'new_v_l0_w_in', 'new_v_l0_lru_conv_w', 'new_v_l0_lru_conv_b', 'new_v_l0_lru_w_a', 'new_v_l0_lru_b_a', 'new_v_l0_lru_w_x', 'new_v_l0_lru_b_x', 'new_v_l0_lru_lambda', 'new_v_l0_attn_sinks', 'new_v_l0_w_out', 'new_v_l0_ffn_pre_norm', 'new_v_l0_ffn_post_norm', 'new_v_l0_ffn_w_up', 'new_v_l0_ffn_conv_w', 'new_v_l0_ffn_conv_b', 'new_v_l0_ffn_w_down', 'new_v_l1_mix_pre_norm', 'new_v_l1_mix_post_norm', 'new_v_l1_w_in', 'new_v_l1_ssm_conv_w', 'new_v_l1_ssm_conv_b', 'new_v_l1_dt_bias', 'new_v_l1_a_log', 'new_v_l1_d_skip', 'new_v_l1_gate_norm', 'new_v_l1_w_out', 'new_v_l1_ffn_pre_norm', 'new_v_l1_ffn_post_norm', 'new_v_l1_ffn_w_up', 'new_v_l1_ffn_conv_w', 'new_v_l1_ffn_conv_b', 'new_v_l1_ffn_w_down']
TWIN_LEAF_KINDS = {'loss': 'loss', 'grad_x': 'grad_x', 'grad_meta_tokens': 'grad_w', 'grad_l0_mix_pre_norm': 'grad_w', 'grad_l0_mix_post_norm': 'grad_w', 'grad_l0_w_in': 'grad_w', 'grad_l0_lru_conv_w': 'grad_w', 'grad_l0_lru_conv_b': 'grad_w', 'grad_l0_lru_w_a': 'grad_w', 'grad_l0_lru_b_a': 'grad_w', 'grad_l0_lru_w_x': 'grad_w', 'grad_l0_lru_b_x': 'grad_w', 'grad_l0_lru_lambda': 'grad_w', 'grad_l0_attn_sinks': 'grad_w', 'grad_l0_w_out': 'grad_w', 'grad_l0_ffn_pre_norm': 'grad_w', 'grad_l0_ffn_post_norm': 'grad_w', 'grad_l0_ffn_w_up': 'grad_w', 'grad_l0_ffn_conv_w': 'grad_w', 'grad_l0_ffn_conv_b': 'grad_w', 'grad_l0_ffn_w_down': 'grad_w', 'grad_l1_mix_pre_norm': 'grad_w', 'grad_l1_mix_post_norm': 'grad_w', 'grad_l1_w_in': 'grad_w', 'grad_l1_ssm_conv_w': 'grad_w', 'grad_l1_ssm_conv_b': 'grad_w', 'grad_l1_dt_bias': 'grad_w', 'grad_l1_a_log': 'grad_w', 'grad_l1_d_skip': 'grad_w', 'grad_l1_gate_norm': 'grad_w', 'grad_l1_w_out': 'grad_w', 'grad_l1_ffn_pre_norm': 'grad_w', 'grad_l1_ffn_post_norm': 'grad_w', 'grad_l1_ffn_w_up': 'grad_w', 'grad_l1_ffn_conv_w': 'grad_w', 'grad_l1_ffn_conv_b': 'grad_w', 'grad_l1_ffn_w_down': 'grad_w', 'delta_meta_tokens': 'delta_w', 'delta_l0_mix_pre_norm': 'delta_w', 'delta_l0_mix_post_norm': 'delta_w', 'delta_l0_w_in': 'delta_w', 'delta_l0_lru_conv_w': 'delta_w', 'delta_l0_lru_conv_b': 'delta_w', 'delta_l0_lru_w_a': 'delta_w', 'delta_l0_lru_b_a': 'delta_w', 'delta_l0_lru_w_x': 'delta_w', 'delta_l0_lru_b_x': 'delta_w', 'delta_l0_lru_lambda': 'delta_w', 'delta_l0_attn_sinks': 'delta_w', 'delta_l0_w_out': 'delta_w', 'delta_l0_ffn_pre_norm': 'delta_w', 'delta_l0_ffn_post_norm': 'delta_w', 'delta_l0_ffn_w_up': 'delta_w', 'delta_l0_ffn_conv_w': 'delta_w', 'delta_l0_ffn_conv_b': 'delta_w', 'delta_l0_ffn_w_down': 'delta_w', 'delta_l1_mix_pre_norm': 'delta_w', 'delta_l1_mix_post_norm': 'delta_w', 'delta_l1_w_in': 'delta_w', 'delta_l1_ssm_conv_w': 'delta_w', 'delta_l1_ssm_conv_b': 'delta_w', 'delta_l1_dt_bias': 'delta_w', 'delta_l1_a_log': 'delta_w', 'delta_l1_d_skip': 'delta_w', 'delta_l1_gate_norm': 'delta_w', 'delta_l1_w_out': 'delta_w', 'delta_l1_ffn_pre_norm': 'delta_w', 'delta_l1_ffn_post_norm': 'delta_w', 'delta_l1_ffn_w_up': 'delta_w', 'delta_l1_ffn_conv_w': 'delta_w', 'delta_l1_ffn_conv_b': 'delta_w', 'delta_l1_ffn_w_down': 'delta_w', 'new_m_meta_tokens': 'new_m', 'new_m_l0_mix_pre_norm': 'new_m', 'new_m_l0_mix_post_norm': 'new_m', 'new_m_l0_w_in': 'new_m', 'new_m_l0_lru_conv_w': 'new_m', 'new_m_l0_lru_conv_b': 'new_m', 'new_m_l0_lru_w_a': 'new_m', 'new_m_l0_lru_b_a': 'new_m', 'new_m_l0_lru_w_x': 'new_m', 'new_m_l0_lru_b_x': 'new_m', 'new_m_l0_lru_lambda': 'new_m', 'new_m_l0_attn_sinks': 'new_m', 'new_m_l0_w_out': 'new_m', 'new_m_l0_ffn_pre_norm': 'new_m', 'new_m_l0_ffn_post_norm': 'new_m', 'new_m_l0_ffn_w_up': 'new_m', 'new_m_l0_ffn_conv_w': 'new_m', 'new_m_l0_ffn_conv_b': 'new_m', 'new_m_l0_ffn_w_down': 'new_m', 'new_m_l1_mix_pre_norm': 'new_m', 'new_m_l1_mix_post_norm': 'new_m', 'new_m_l1_w_in': 'new_m', 'new_m_l1_ssm_conv_w': 'new_m', 'new_m_l1_ssm_conv_b': 'new_m', 'new_m_l1_dt_bias': 'new_m', 'new_m_l1_a_log': 'new_m', 'new_m_l1_d_skip': 'new_m', 'new_m_l1_gate_norm': 'new_m', 'new_m_l1_w_out': 'new_m', 'new_m_l1_ffn_pre_norm': 'new_m', 'new_m_l1_ffn_post_norm': 'new_m', 'new_m_l1_ffn_w_up': 'new_m', 'new_m_l1_ffn_conv_w': 'new_m', 'new_m_l1_ffn_conv_b': 'new_m', 'new_m_l1_ffn_w_down': 'new_m', 'new_v_meta_tokens': 'new_v', 'new_v_l0_mix_pre_norm': 'new_v', 'new_v_l0_mix_post_norm': 'new_v', 'new_v_l0_w_in': 'new_v', 'new_v_l0_lru_conv_w': 'new_v', 'new_v_l0_lru_conv_b': 'new_v', 'new_v_l0_lru_w_a': 'new_v', 'new_v_l0_lru_b_a': 'new_v', 'new_v_l0_lru_w_x': 'new_v', 'new_v_l0_lru_b_x': 'new_v', 'new_v_l0_lru_lambda': 'new_v', 'new_v_l0_attn_sinks': 'new_v', 'new_v_l0_w_out': 'new_v', 'new_v_l0_ffn_pre_norm': 'new_v', 'new_v_l0_ffn_post_norm': 'new_v', 'new_v_l0_ffn_w_up': 'new_v', 'new_v_l0_ffn_conv_w': 'new_v', 'new_v_l0_ffn_conv_b': 'new_v', 'new_v_l0_ffn_w_down': 'new_v', 'new_v_l1_mix_pre_norm': 'new_v', 'new_v_l1_mix_post_norm': 'new_v', 'new_v_l1_w_in': 'new_v', 'new_v_l1_ssm_conv_w': 'new_v', 'new_v_l1_ssm_conv_b': 'new_v', 'new_v_l1_dt_bias': 'new_v', 'new_v_l1_a_log': 'new_v', 'new_v_l1_d_skip': 'new_v', 'new_v_l1_gate_norm': 'new_v', 'new_v_l1_w_out': 'new_v', 'new_v_l1_ffn_pre_norm': 'new_v', 'new_v_l1_ffn_post_norm': 'new_v', 'new_v_l1_ffn_w_up': 'new_v', 'new_v_l1_ffn_conv_w': 'new_v', 'new_v_l1_ffn_conv_b': 'new_v', 'new_v_l1_ffn_w_down': 'new_v'}


def _forward(args):
    return _fwd_reference(*[args[k] for k in FWD_PARAMS])


def _output_shape():
    def fwd():
        inp = _fwd_setup_inputs(0)
        return _fwd_reference(*[inp[k] for k in FWD_PARAMS])
    out = _jax.eval_shape(fwd)
    return out.shape, out.dtype

N_MICROBATCH = 1
ADAM_LR = 0.001
ADAM_B1 = 0.9
ADAM_B2 = 0.999
ADAM_EPS = 1e-08
ADAM_WD = 0.01
ADAM_STEP = 10
PER_EXAMPLE_BATCH_AXIS = {'x': 0, 'loss_target': 0}
SHARED_INPUTS = []
_WEIGHT_DTYPES = {'meta_tokens': _jnp.float32, 'l0_mix_pre_norm': _jnp.float32, 'l0_mix_post_norm': _jnp.float32, 'l0_w_in': _jnp.float32, 'l0_lru_conv_w': _jnp.float32, 'l0_lru_conv_b': _jnp.float32, 'l0_lru_w_a': _jnp.float32, 'l0_lru_b_a': _jnp.float32, 'l0_lru_w_x': _jnp.float32, 'l0_lru_b_x': _jnp.float32, 'l0_lru_lambda': _jnp.float32, 'l0_attn_sinks': _jnp.float32, 'l0_w_out': _jnp.float32, 'l0_ffn_pre_norm': _jnp.float32, 'l0_ffn_post_norm': _jnp.float32, 'l0_ffn_w_up': _jnp.float32, 'l0_ffn_conv_w': _jnp.float32, 'l0_ffn_conv_b': _jnp.float32, 'l0_ffn_w_down': _jnp.float32, 'l1_mix_pre_norm': _jnp.float32, 'l1_mix_post_norm': _jnp.float32, 'l1_w_in': _jnp.float32, 'l1_ssm_conv_w': _jnp.float32, 'l1_ssm_conv_b': _jnp.float32, 'l1_dt_bias': _jnp.float32, 'l1_a_log': _jnp.float32, 'l1_d_skip': _jnp.float32, 'l1_gate_norm': _jnp.float32, 'l1_w_out': _jnp.float32, 'l1_ffn_pre_norm': _jnp.float32, 'l1_ffn_post_norm': _jnp.float32, 'l1_ffn_w_up': _jnp.float32, 'l1_ffn_conv_w': _jnp.float32, 'l1_ffn_conv_b': _jnp.float32, 'l1_ffn_w_down': _jnp.float32}
MOMENT_SCALE = {'meta_tokens': 1.221874e-01, 'l0_mix_pre_norm': 2.051896e+00, 'l0_mix_post_norm': 6.364628e+01, 'l0_w_in': 1.113446e+00, 'l0_lru_conv_w': 2.104356e+00, 'l0_lru_conv_b': 5.812575e+01, 'l0_lru_w_a': 1.258406e+00, 'l0_lru_b_a': 6.216973e-01, 'l0_lru_w_x': 2.308551e+00, 'l0_lru_b_x': 6.430986e-01, 'l0_lru_lambda': 9.495997e-01, 'l0_attn_sinks': 1.450360e+00, 'l0_w_out': 2.600747e+00, 'l0_ffn_pre_norm': 1.740444e+00, 'l0_ffn_post_norm': 6.384878e+01, 'l0_ffn_w_up': 6.936578e-01, 'l0_ffn_conv_w': 7.980646e-01, 'l0_ffn_conv_b': 3.672250e+00, 'l0_ffn_w_down': 1.454455e+00, 'l1_mix_pre_norm': 1.772059e+00, 'l1_mix_post_norm': 6.457534e+01, 'l1_w_in': 6.927851e-01, 'l1_ssm_conv_w': 8.237722e-01, 'l1_ssm_conv_b': 2.981076e+00, 'l1_dt_bias': 9.812302e-01, 'l1_a_log': 3.227137e+00, 'l1_d_skip': 2.940914e+00, 'l1_gate_norm': 1.658835e+00, 'l1_w_out': 2.639670e+00, 'l1_ffn_pre_norm': 1.480671e+00, 'l1_ffn_post_norm': 6.429417e+01, 'l1_ffn_w_up': 5.930196e-01, 'l1_ffn_conv_w': 7.127024e-01, 'l1_ffn_conv_b': 3.709661e+00, 'l1_ffn_w_down': 1.295223e+00}


def _to_microbatches(a, axis):
    t = _jnp.moveaxis(a, axis, 0)
    t = t.reshape((N_MICROBATCH, t.shape[0] // N_MICROBATCH) + t.shape[1:])
    return _jnp.moveaxis(t, 1, axis + 1)


def setup_inputs(seed: int = 0) -> dict:
    inp = _fwd_setup_inputs(seed)
    key = _jax.random.fold_in(_jax.random.key(seed), 7919)
    shape, _ = _output_shape()
    out = dict(inp)
    out["loss_target"] = _jax.random.normal(_jax.random.fold_in(key, 0), shape, _jnp.float32)
    for i, name in enumerate(TWIN_WEIGHTS):
        w = inp[name].astype(_jnp.float32)
        if MOMENT_SCALE is None:
            s = _jnp.sqrt(_jnp.mean(_jnp.square(w)) + 1e-30)
        else:
            s = MOMENT_SCALE[name]
        km, kv = _jax.random.split(_jax.random.fold_in(key, i + 1))
        out[name] = w
        out["m_" + name] = s * _jax.random.normal(km, w.shape, _jnp.float32)
        out["v_" + name] = (s * s) * _jax.random.uniform(kv, w.shape, _jnp.float32, 0.5, 1.5)
    if N_MICROBATCH > 1:
        for name, axis in PER_EXAMPLE_BATCH_AXIS.items():
            out[name] = _to_microbatches(out[name], axis)
    return {'x': out['x'], 'meta_tokens': out['meta_tokens'], 'l0_mix_pre_norm': out['l0_mix_pre_norm'], 'l0_mix_post_norm': out['l0_mix_post_norm'], 'l0_w_in': out['l0_w_in'], 'l0_lru_conv_w': out['l0_lru_conv_w'], 'l0_lru_conv_b': out['l0_lru_conv_b'], 'l0_lru_w_a': out['l0_lru_w_a'], 'l0_lru_b_a': out['l0_lru_b_a'], 'l0_lru_w_x': out['l0_lru_w_x'], 'l0_lru_b_x': out['l0_lru_b_x'], 'l0_lru_lambda': out['l0_lru_lambda'], 'l0_attn_sinks': out['l0_attn_sinks'], 'l0_w_out': out['l0_w_out'], 'l0_ffn_pre_norm': out['l0_ffn_pre_norm'], 'l0_ffn_post_norm': out['l0_ffn_post_norm'], 'l0_ffn_w_up': out['l0_ffn_w_up'], 'l0_ffn_conv_w': out['l0_ffn_conv_w'], 'l0_ffn_conv_b': out['l0_ffn_conv_b'], 'l0_ffn_w_down': out['l0_ffn_w_down'], 'l1_mix_pre_norm': out['l1_mix_pre_norm'], 'l1_mix_post_norm': out['l1_mix_post_norm'], 'l1_w_in': out['l1_w_in'], 'l1_ssm_conv_w': out['l1_ssm_conv_w'], 'l1_ssm_conv_b': out['l1_ssm_conv_b'], 'l1_dt_bias': out['l1_dt_bias'], 'l1_a_log': out['l1_a_log'], 'l1_d_skip': out['l1_d_skip'], 'l1_gate_norm': out['l1_gate_norm'], 'l1_w_out': out['l1_w_out'], 'l1_ffn_pre_norm': out['l1_ffn_pre_norm'], 'l1_ffn_post_norm': out['l1_ffn_post_norm'], 'l1_ffn_w_up': out['l1_ffn_w_up'], 'l1_ffn_conv_w': out['l1_ffn_conv_w'], 'l1_ffn_conv_b': out['l1_ffn_conv_b'], 'l1_ffn_w_down': out['l1_ffn_w_down'], 'loss_target': out['loss_target'], 'm_meta_tokens': out['m_meta_tokens'], 'm_l0_mix_pre_norm': out['m_l0_mix_pre_norm'], 'm_l0_mix_post_norm': out['m_l0_mix_post_norm'], 'm_l0_w_in': out['m_l0_w_in'], 'm_l0_lru_conv_w': out['m_l0_lru_conv_w'], 'm_l0_lru_conv_b': out['m_l0_lru_conv_b'], 'm_l0_lru_w_a': out['m_l0_lru_w_a'], 'm_l0_lru_b_a': out['m_l0_lru_b_a'], 'm_l0_lru_w_x': out['m_l0_lru_w_x'], 'm_l0_lru_b_x': out['m_l0_lru_b_x'], 'm_l0_lru_lambda': out['m_l0_lru_lambda'], 'm_l0_attn_sinks': out['m_l0_attn_sinks'], 'm_l0_w_out': out['m_l0_w_out'], 'm_l0_ffn_pre_norm': out['m_l0_ffn_pre_norm'], 'm_l0_ffn_post_norm': out['m_l0_ffn_post_norm'], 'm_l0_ffn_w_up': out['m_l0_ffn_w_up'], 'm_l0_ffn_conv_w': out['m_l0_ffn_conv_w'], 'm_l0_ffn_conv_b': out['m_l0_ffn_conv_b'], 'm_l0_ffn_w_down': out['m_l0_ffn_w_down'], 'm_l1_mix_pre_norm': out['m_l1_mix_pre_norm'], 'm_l1_mix_post_norm': out['m_l1_mix_post_norm'], 'm_l1_w_in': out['m_l1_w_in'], 'm_l1_ssm_conv_w': out['m_l1_ssm_conv_w'], 'm_l1_ssm_conv_b': out['m_l1_ssm_conv_b'], 'm_l1_dt_bias': out['m_l1_dt_bias'], 'm_l1_a_log': out['m_l1_a_log'], 'm_l1_d_skip': out['m_l1_d_skip'], 'm_l1_gate_norm': out['m_l1_gate_norm'], 'm_l1_w_out': out['m_l1_w_out'], 'm_l1_ffn_pre_norm': out['m_l1_ffn_pre_norm'], 'm_l1_ffn_post_norm': out['m_l1_ffn_post_norm'], 'm_l1_ffn_w_up': out['m_l1_ffn_w_up'], 'm_l1_ffn_conv_w': out['m_l1_ffn_conv_w'], 'm_l1_ffn_conv_b': out['m_l1_ffn_conv_b'], 'm_l1_ffn_w_down': out['m_l1_ffn_w_down'], 'v_meta_tokens': out['v_meta_tokens'], 'v_l0_mix_pre_norm': out['v_l0_mix_pre_norm'], 'v_l0_mix_post_norm': out['v_l0_mix_post_norm'], 'v_l0_w_in': out['v_l0_w_in'], 'v_l0_lru_conv_w': out['v_l0_lru_conv_w'], 'v_l0_lru_conv_b': out['v_l0_lru_conv_b'], 'v_l0_lru_w_a': out['v_l0_lru_w_a'], 'v_l0_lru_b_a': out['v_l0_lru_b_a'], 'v_l0_lru_w_x': out['v_l0_lru_w_x'], 'v_l0_lru_b_x': out['v_l0_lru_b_x'], 'v_l0_lru_lambda': out['v_l0_lru_lambda'], 'v_l0_attn_sinks': out['v_l0_attn_sinks'], 'v_l0_w_out': out['v_l0_w_out'], 'v_l0_ffn_pre_norm': out['v_l0_ffn_pre_norm'], 'v_l0_ffn_post_norm': out['v_l0_ffn_post_norm'], 'v_l0_ffn_w_up': out['v_l0_ffn_w_up'], 'v_l0_ffn_conv_w': out['v_l0_ffn_conv_w'], 'v_l0_ffn_conv_b': out['v_l0_ffn_conv_b'], 'v_l0_ffn_w_down': out['v_l0_ffn_w_down'], 'v_l1_mix_pre_norm': out['v_l1_mix_pre_norm'], 'v_l1_mix_post_norm': out['v_l1_mix_post_norm'], 'v_l1_w_in': out['v_l1_w_in'], 'v_l1_ssm_conv_w': out['v_l1_ssm_conv_w'], 'v_l1_ssm_conv_b': out['v_l1_ssm_conv_b'], 'v_l1_dt_bias': out['v_l1_dt_bias'], 'v_l1_a_log': out['v_l1_a_log'], 'v_l1_d_skip': out['v_l1_d_skip'], 'v_l1_gate_norm': out['v_l1_gate_norm'], 'v_l1_w_out': out['v_l1_w_out'], 'v_l1_ffn_pre_norm': out['v_l1_ffn_pre_norm'], 'v_l1_ffn_post_norm': out['v_l1_ffn_post_norm'], 'v_l1_ffn_w_up': out['v_l1_ffn_w_up'], 'v_l1_ffn_conv_w': out['v_l1_ffn_conv_w'], 'v_l1_ffn_conv_b': out['v_l1_ffn_conv_b'], 'v_l1_ffn_w_down': out['v_l1_ffn_w_down']}


def _loss(weights, diff, rest, loss_target):
    with _jax.named_scope("forward"):
        args = {**rest, TWIN_DIFF_INPUT: diff, **{k: w.astype(_WEIGHT_DTYPES[k]) for k, w in weights.items()}}
        y = _forward(args)
    with _jax.named_scope("loss_head"):
        err = _jnp.square(y.astype(_jnp.float32) - loss_target)
        return 0.5 * _jnp.sum(_jnp.mean(err, axis=-1)) if err.ndim else 0.5 * err


def _adamw(w, g, m, v):
    m = ADAM_B1 * m + (1.0 - ADAM_B1) * g
    v = ADAM_B2 * v + (1.0 - ADAM_B2) * _jnp.square(g)
    m_hat = m / (1.0 - ADAM_B1 ** ADAM_STEP)
    v_hat = v / (1.0 - ADAM_B2 ** ADAM_STEP)
    delta = -ADAM_LR * (m_hat / (_jnp.sqrt(v_hat) + ADAM_EPS) + ADAM_WD * w)
    return delta, m, v


def reference(x, meta_tokens, l0_mix_pre_norm, l0_mix_post_norm, l0_w_in, l0_lru_conv_w, l0_lru_conv_b, l0_lru_w_a, l0_lru_b_a, l0_lru_w_x, l0_lru_b_x, l0_lru_lambda, l0_attn_sinks, l0_w_out, l0_ffn_pre_norm, l0_ffn_post_norm, l0_ffn_w_up, l0_ffn_conv_w, l0_ffn_conv_b, l0_ffn_w_down, l1_mix_pre_norm, l1_mix_post_norm, l1_w_in, l1_ssm_conv_w, l1_ssm_conv_b, l1_dt_bias, l1_a_log, l1_d_skip, l1_gate_norm, l1_w_out, l1_ffn_pre_norm, l1_ffn_post_norm, l1_ffn_w_up, l1_ffn_conv_w, l1_ffn_conv_b, l1_ffn_w_down, loss_target, m_meta_tokens, m_l0_mix_pre_norm, m_l0_mix_post_norm, m_l0_w_in, m_l0_lru_conv_w, m_l0_lru_conv_b, m_l0_lru_w_a, m_l0_lru_b_a, m_l0_lru_w_x, m_l0_lru_b_x, m_l0_lru_lambda, m_l0_attn_sinks, m_l0_w_out, m_l0_ffn_pre_norm, m_l0_ffn_post_norm, m_l0_ffn_w_up, m_l0_ffn_conv_w, m_l0_ffn_conv_b, m_l0_ffn_w_down, m_l1_mix_pre_norm, m_l1_mix_post_norm, m_l1_w_in, m_l1_ssm_conv_w, m_l1_ssm_conv_b, m_l1_dt_bias, m_l1_a_log, m_l1_d_skip, m_l1_gate_norm, m_l1_w_out, m_l1_ffn_pre_norm, m_l1_ffn_post_norm, m_l1_ffn_w_up, m_l1_ffn_conv_w, m_l1_ffn_conv_b, m_l1_ffn_w_down, v_meta_tokens, v_l0_mix_pre_norm, v_l0_mix_post_norm, v_l0_w_in, v_l0_lru_conv_w, v_l0_lru_conv_b, v_l0_lru_w_a, v_l0_lru_b_a, v_l0_lru_w_x, v_l0_lru_b_x, v_l0_lru_lambda, v_l0_attn_sinks, v_l0_w_out, v_l0_ffn_pre_norm, v_l0_ffn_post_norm, v_l0_ffn_w_up, v_l0_ffn_conv_w, v_l0_ffn_conv_b, v_l0_ffn_w_down, v_l1_mix_pre_norm, v_l1_mix_post_norm, v_l1_w_in, v_l1_ssm_conv_w, v_l1_ssm_conv_b, v_l1_dt_bias, v_l1_a_log, v_l1_d_skip, v_l1_gate_norm, v_l1_w_out, v_l1_ffn_pre_norm, v_l1_ffn_post_norm, v_l1_ffn_w_up, v_l1_ffn_conv_w, v_l1_ffn_conv_b, v_l1_ffn_w_down):
    given = dict(x=x, meta_tokens=meta_tokens, l0_mix_pre_norm=l0_mix_pre_norm, l0_mix_post_norm=l0_mix_post_norm, l0_w_in=l0_w_in, l0_lru_conv_w=l0_lru_conv_w, l0_lru_conv_b=l0_lru_conv_b, l0_lru_w_a=l0_lru_w_a, l0_lru_b_a=l0_lru_b_a, l0_lru_w_x=l0_lru_w_x, l0_lru_b_x=l0_lru_b_x, l0_lru_lambda=l0_lru_lambda, l0_attn_sinks=l0_attn_sinks, l0_w_out=l0_w_out, l0_ffn_pre_norm=l0_ffn_pre_norm, l0_ffn_post_norm=l0_ffn_post_norm, l0_ffn_w_up=l0_ffn_w_up, l0_ffn_conv_w=l0_ffn_conv_w, l0_ffn_conv_b=l0_ffn_conv_b, l0_ffn_w_down=l0_ffn_w_down, l1_mix_pre_norm=l1_mix_pre_norm, l1_mix_post_norm=l1_mix_post_norm, l1_w_in=l1_w_in, l1_ssm_conv_w=l1_ssm_conv_w, l1_ssm_conv_b=l1_ssm_conv_b, l1_dt_bias=l1_dt_bias, l1_a_log=l1_a_log, l1_d_skip=l1_d_skip, l1_gate_norm=l1_gate_norm, l1_w_out=l1_w_out, l1_ffn_pre_norm=l1_ffn_pre_norm, l1_ffn_post_norm=l1_ffn_post_norm, l1_ffn_w_up=l1_ffn_w_up, l1_ffn_conv_w=l1_ffn_conv_w, l1_ffn_conv_b=l1_ffn_conv_b, l1_ffn_w_down=l1_ffn_w_down, loss_target=loss_target, m_meta_tokens=m_meta_tokens, m_l0_mix_pre_norm=m_l0_mix_pre_norm, m_l0_mix_post_norm=m_l0_mix_post_norm, m_l0_w_in=m_l0_w_in, m_l0_lru_conv_w=m_l0_lru_conv_w, m_l0_lru_conv_b=m_l0_lru_conv_b, m_l0_lru_w_a=m_l0_lru_w_a, m_l0_lru_b_a=m_l0_lru_b_a, m_l0_lru_w_x=m_l0_lru_w_x, m_l0_lru_b_x=m_l0_lru_b_x, m_l0_lru_lambda=m_l0_lru_lambda, m_l0_attn_sinks=m_l0_attn_sinks, m_l0_w_out=m_l0_w_out, m_l0_ffn_pre_norm=m_l0_ffn_pre_norm, m_l0_ffn_post_norm=m_l0_ffn_post_norm, m_l0_ffn_w_up=m_l0_ffn_w_up, m_l0_ffn_conv_w=m_l0_ffn_conv_w, m_l0_ffn_conv_b=m_l0_ffn_conv_b, m_l0_ffn_w_down=m_l0_ffn_w_down, m_l1_mix_pre_norm=m_l1_mix_pre_norm, m_l1_mix_post_norm=m_l1_mix_post_norm, m_l1_w_in=m_l1_w_in, m_l1_ssm_conv_w=m_l1_ssm_conv_w, m_l1_ssm_conv_b=m_l1_ssm_conv_b, m_l1_dt_bias=m_l1_dt_bias, m_l1_a_log=m_l1_a_log, m_l1_d_skip=m_l1_d_skip, m_l1_gate_norm=m_l1_gate_norm, m_l1_w_out=m_l1_w_out, m_l1_ffn_pre_norm=m_l1_ffn_pre_norm, m_l1_ffn_post_norm=m_l1_ffn_post_norm, m_l1_ffn_w_up=m_l1_ffn_w_up, m_l1_ffn_conv_w=m_l1_ffn_conv_w, m_l1_ffn_conv_b=m_l1_ffn_conv_b, m_l1_ffn_w_down=m_l1_ffn_w_down, v_meta_tokens=v_meta_tokens, v_l0_mix_pre_norm=v_l0_mix_pre_norm, v_l0_mix_post_norm=v_l0_mix_post_norm, v_l0_w_in=v_l0_w_in, v_l0_lru_conv_w=v_l0_lru_conv_w, v_l0_lru_conv_b=v_l0_lru_conv_b, v_l0_lru_w_a=v_l0_lru_w_a, v_l0_lru_b_a=v_l0_lru_b_a, v_l0_lru_w_x=v_l0_lru_w_x, v_l0_lru_b_x=v_l0_lru_b_x, v_l0_lru_lambda=v_l0_lru_lambda, v_l0_attn_sinks=v_l0_attn_sinks, v_l0_w_out=v_l0_w_out, v_l0_ffn_pre_norm=v_l0_ffn_pre_norm, v_l0_ffn_post_norm=v_l0_ffn_post_norm, v_l0_ffn_w_up=v_l0_ffn_w_up, v_l0_ffn_conv_w=v_l0_ffn_conv_w, v_l0_ffn_conv_b=v_l0_ffn_conv_b, v_l0_ffn_w_down=v_l0_ffn_w_down, v_l1_mix_pre_norm=v_l1_mix_pre_norm, v_l1_mix_post_norm=v_l1_mix_post_norm, v_l1_w_in=v_l1_w_in, v_l1_ssm_conv_w=v_l1_ssm_conv_w, v_l1_ssm_conv_b=v_l1_ssm_conv_b, v_l1_dt_bias=v_l1_dt_bias, v_l1_a_log=v_l1_a_log, v_l1_d_skip=v_l1_d_skip, v_l1_gate_norm=v_l1_gate_norm, v_l1_w_out=v_l1_w_out, v_l1_ffn_pre_norm=v_l1_ffn_pre_norm, v_l1_ffn_post_norm=v_l1_ffn_post_norm, v_l1_ffn_w_up=v_l1_ffn_w_up, v_l1_ffn_conv_w=v_l1_ffn_conv_w, v_l1_ffn_conv_b=v_l1_ffn_conv_b, v_l1_ffn_w_down=v_l1_ffn_w_down)
    weights = {n: given[n] for n in TWIN_WEIGHTS}
    shared = {n: given[n] for n in SHARED_INPUTS}
    per_example = {n: given[n] for n in ['x']}
    grad_fn = _jax.value_and_grad(_loss, argnums=(0, 1))

    def one_microbatch(ex, loss_target):
        ex = dict(ex)
        diff = ex.pop(TWIN_DIFF_INPUT)
        return grad_fn(weights, diff, {**shared, **ex}, loss_target)

    if N_MICROBATCH == 1:
        loss, (grad_w, grad_x) = one_microbatch(per_example, given["loss_target"])
    else:
        def body(carry, xs):
            loss_sum, grad_sum = carry
            l_k, (gw_k, gx_k) = one_microbatch(xs[0], xs[1])
            with _jax.named_scope("update"):
                return (loss_sum + l_k, _jax.tree.map(_jnp.add, grad_sum, gw_k)), gx_k

        init = (_jnp.zeros((), _jnp.float32), _jax.tree.map(_jnp.zeros_like, weights))
        (loss, grad_w), grad_x = _jax.lax.scan(body, init, (per_example, given["loss_target"]))
    with _jax.named_scope("update"):
        delta_w, new_m, new_v = {}, {}, {}
        for n in TWIN_WEIGHTS:
            delta_w[n], new_m[n], new_v[n] = _adamw(weights[n], grad_w[n], given["m_" + n], given["v_" + n])
    return (loss, grad_x, *[grad_w[n] for n in TWIN_WEIGHTS], *[delta_w[n] for n in TWIN_WEIGHTS],
            *[new_m[n] for n in TWIN_WEIGHTS], *[new_v[n] for n in TWIN_WEIGHTS])
```

```python
import functools
import math

import jax
import jax.numpy as jnp
from jax import lax
from jax.experimental import pallas as pl
from jax.experimental.pallas import tpu as pltpu

F32 = jnp.float32
BF16 = jnp.bfloat16
MXU = jnp.bfloat16

D_MODEL = 1024
N_META = 16
BLK = 128
PAD = BLK - N_META
D_RNN = 1024
LRU_C = 8.0
N_Q_HEADS = 16
HEAD_DIM = 64
D_SSM = 2048
SSD_HEADS = 32
SSD_GROUPS = 8
D_FF = 2816
EPS = 1e-6
NEG = -1e30
N_DEV = 8

ADAM_LR = 0.001
ADAM_B1 = 0.9
ADAM_B2 = 0.999
ADAM_EPS = 1e-08
ADAM_WD = 0.01
ADAM_STEP = 10

VMEM_LIMIT = 56 * 1024 * 1024
MESH = pl.DeviceIdType.MESH
ANY = pl.BlockSpec(memory_space=pl.ANY)


def _cp(*sem):
    return pltpu.CompilerParams(dimension_semantics=sem, vmem_limit_bytes=VMEM_LIMIT)


def _pick(n, cands):
    for c in cands:
        if n % c == 0:
            return c
    return n


def _dot(a, b):
    return jnp.dot(a.astype(MXU), b.astype(MXU), preferred_element_type=F32)


def _dot_nt(a, b):
    return lax.dot_general(a.astype(MXU), b.astype(MXU), (((1,), (1,)), ((), ())),
                           preferred_element_type=F32)


def _dot_tn(a, b):
    return jnp.dot(a.T.astype(MXU), b.astype(MXU), preferred_element_type=F32)


def _dot_split(v, e):
    hi = v.astype(BF16)
    lo = (v - hi.astype(F32)).astype(BF16)
    return (jnp.dot(hi, e, preferred_element_type=F32)
            + jnp.dot(lo, e, preferred_element_type=F32))


def _sigmoid(x):
    return 1.0 / (1.0 + jnp.exp(-x))


def _log1p(x):
    u = 1.0 + x
    return jnp.where(u == 1.0, x, jnp.log(u) * (x / jnp.where(u == 1.0, 1.0, u - 1.0)))


def _expm1(x):
    u = jnp.exp(x)
    um1 = u - 1.0
    lg = jnp.log(jnp.where(u > 0.0, u, 1.0))
    safe = (um1 != 0.0) & (um1 != -1.0)
    return jnp.where(um1 == 0.0, x, jnp.where(um1 == -1.0, -1.0,
                                               um1 * (x / jnp.where(safe, lg, 1.0))))


def _softplus(x):
    return jnp.maximum(x, 0.0) + _log1p(jnp.exp(-jnp.abs(x)))


_GC = math.sqrt(2.0 / math.pi)


def _gelu(x):
    t = jnp.tanh(_GC * (x + 0.044715 * x * x * x))
    return 0.5 * x * (1.0 + t)


def _gelu_grad(x):
    t = jnp.tanh(_GC * (x + 0.044715 * x * x * x))
    return 0.5 * (1.0 + t) + 0.5 * x * (1.0 - t * t) * (_GC * (1.0 + 3.0 * 0.044715 * x * x))


def _silu(x):
    return x * _sigmoid(x)


def _silu_grad(x):
    s = _sigmoid(x)
    return s * (1.0 + x * (1.0 - s))


def _rows(shape):
    return lax.broadcasted_iota(jnp.int32, shape, 0)


def _lanes(shape):
    return lax.broadcasted_iota(jnp.int32, shape, 1)


def _shift_down(x, tail, d):
    if d == 0:
        return x
    n = x.shape[0]
    xr = pltpu.roll(x, d, 0)
    tr = pltpu.roll(tail, d, 0)
    first = jnp.where(_rows(tr.shape) < d, tr, xr[0:8])
    return jnp.concatenate([first, xr[8:n]], axis=0)


def _shift_up(x, head, d):
    if d == 0:
        return x
    n = x.shape[0]
    xr = pltpu.roll(x, n - d, 0)
    hr = pltpu.roll(head, 8 - d, 0)
    last = jnp.where(_rows(hr.shape) >= 8 - d, hr, xr[n - 8:n])
    return jnp.concatenate([xr[0:n - 8], last], axis=0)


def _row_at(x, i):
    return jnp.sum(jnp.where(_rows(x.shape) == i, x, 0.0), axis=0, keepdims=True)


def _scan_fwd(a, u):
    n = a.shape[0]
    ri = _rows(a.shape)
    d = 1
    while d < n:
        m = ri >= d
        us = jnp.where(m, pltpu.roll(u, d, 0), 0.0)
        as_ = jnp.where(m, pltpu.roll(a, d, 0), 1.0)
        u = u + a * us
        a = a * as_
        d *= 2
    return a, u


def _scan_rev(c, u):
    n = c.shape[0]
    ri = _rows(c.shape)
    d = 1
    while d < n:
        m = ri < n - d
        us = jnp.where(m, pltpu.roll(u, n - d, 0), 0.0)
        cs = jnp.where(m, pltpu.roll(c, n - d, 0), 1.0)
        u = u + c * us
        c = c * cs
        d *= 2
    return c, u


def _cumsum_fwd(x):
    n = x.shape[0]
    ri = _rows(x.shape)
    d = 1
    while d < n:
        x = x + jnp.where(ri >= d, pltpu.roll(x, d, 0), 0.0)
        d *= 2
    return x


def _cumsum_rev(x):
    n = x.shape[0]
    ri = _rows(x.shape)
    d = 1
    while d < n:
        x = x + jnp.where(ri < n - d, pltpu.roll(x, n - d, 0), 0.0)
        d *= 2
    return x


def matmul(a, b, *, trans_b=False, out_dtype=F32, name):
    m, k = a.shape
    n = b.shape[0] if trans_b else b.shape[1]
    tm = _pick(m, (1664, 1408, 1040, 832, 640, 512, 384, 256, 128))
    tn = _pick(n, (512, 896, 640, 384, 256, 128))
    tk = k if k <= 2048 else _pick(k, (1664, 1408, 1024, 896, 512, 256, 128))
    nk = k // tk

    def kern(a_ref, b_ref, o_ref, acc_ref):
        kk = pl.program_id(2)

        @pl.when(kk == 0)
        def _():
            acc_ref[...] = jnp.zeros_like(acc_ref)

        if trans_b:
            acc_ref[...] += _dot_nt(a_ref[...], b_ref[...])
        else:
            acc_ref[...] += _dot(a_ref[...], b_ref[...])

        @pl.when(kk == nk - 1)
        def _():
            o_ref[...] = acc_ref[...].astype(o_ref.dtype)

    b_spec = (pl.BlockSpec((tn, tk), lambda i, j, kk: (j, kk)) if trans_b
              else pl.BlockSpec((tk, tn), lambda i, j, kk: (kk, j)))
    return pl.pallas_call(
        kern, name=name,
        grid=(m // tm, n // tn, nk),
        in_specs=[pl.BlockSpec((tm, tk), lambda i, j, kk: (i, kk)), b_spec],
        out_specs=pl.BlockSpec((tm, tn), lambda i, j, kk: (i, j)),
        out_shape=jax.ShapeDtypeStruct((m, n), out_dtype),
        scratch_shapes=[pltpu.VMEM((tm, tn), F32)],
        compiler_params=_cp("parallel", "parallel", "arbitrary"),
    )(a, b)


def _row_tile(t):
    return _pick(t, (832, 640, 512, 384, 256, 128))


def rmsnorm_fwd(x, w, res=None, *, out_dtype, name):
    t, d = x.shape
    tr = _row_tile(t)

    def kern(*refs):
        if res is None:
            x_ref, w_ref, o_ref = refs
        else:
            x_ref, w_ref, r_ref, o_ref = refs
        xv = x_ref[...]
        r = lax.rsqrt(jnp.mean(xv * xv, axis=-1, keepdims=True) + EPS)
        y = xv * r * w_ref[...]
        if res is not None:
            y = r_ref[...] + y
        o_ref[...] = y.astype(o_ref.dtype)

    row = pl.BlockSpec((tr, d), lambda i: (i, 0))
    vec = pl.BlockSpec((1, d), lambda i: (0, 0))
    ins = [x, w.reshape(1, d)] + ([] if res is None else [res])
    specs = [row, vec] + ([] if res is None else [row])
    return pl.pallas_call(
        kern, name=name, grid=(t // tr,), in_specs=specs, out_specs=row,
        out_shape=jax.ShapeDtypeStruct((t, d), out_dtype),
        compiler_params=_cp("parallel"),
    )(*ins)


def rmsnorm_bwd(x, w, dy, res=None, *, out_dtype, name):
    t, d = x.shape
    tr = _row_tile(t)

    def kern(*refs):
        if res is None:
            x_ref, w_ref, dy_ref, dx_ref, dw_ref = refs
        else:
            x_ref, w_ref, dy_ref, r_ref, dx_ref, dw_ref = refs
        i = pl.program_id(0)
        xv = x_ref[...]
        dyv = dy_ref[...].astype(F32)
        r = lax.rsqrt(jnp.mean(xv * xv, axis=-1, keepdims=True) + EPS)
        xh = xv * r
        g = dyv * w_ref[...]
        dx = r * (g - xh * jnp.mean(g * xh, axis=-1, keepdims=True))
        if res is not None:
            dx = r_ref[...] + dx
        dx_ref[...] = dx.astype(dx_ref.dtype)
        part = jnp.sum(dyv * xh, axis=0, keepdims=True)

        @pl.when(i == 0)
        def _():
            dw_ref[...] = part

        @pl.when(i > 0)
        def _():
            dw_ref[...] += part

    row = pl.BlockSpec((tr, d), lambda i: (i, 0))
    vec = pl.BlockSpec((1, d), lambda i: (0, 0))
    ins = [x, w.reshape(1, d), dy] + ([] if res is None else [res])
    specs = [row, vec, row] + ([] if res is None else [row])
    return pl.pallas_call(
        kern, name=name, grid=(t // tr,), in_specs=specs, out_specs=[row, vec],
        out_shape=[jax.ShapeDtypeStruct((t, d), out_dtype), jax.ShapeDtypeStruct((1, d), F32)],
        compiler_params=_cp("arbitrary"),
    )(*ins)


def _conv_tile(t):
    return _pick(t, (640, 384, 256, 128))


def _conv_apply(x, tail, cw, cb, ksz):
    y = cb
    for k in range(ksz):
        y = y + cw[k:k + 1, :] * _shift_down(x, tail, ksz - 1 - k)
    return y


def dwconv_fwd(x, cw, cb, *, mode, x_off, c_out, cblk, out_dtype, name):
    t = x.shape[0]
    ksz = cw.shape[0]
    tb = _conv_tile(t)
    nb, ncb, t8 = t // tb, c_out // cblk, tb // 8
    xo = x_off // cblk
    nin = 2 if mode == "geglu" else 1

    def kern(*refs):
        o_ref = refs[-1]
        n = pl.program_id(1)
        valid = (n * tb + _rows((tb, cblk))) >= PAD
        hs = []
        for q in range(nin):
            x_ref, t_ref, w_ref, b_ref = refs[4 * q:4 * q + 4]
            tail = jnp.where(n > 0, t_ref[...], 0.0)
            hs.append(_conv_apply(x_ref[...], tail, w_ref[...], b_ref[...], ksz))
        if mode == "geglu":
            y = _gelu(hs[0]) * hs[1]
        else:
            y = _silu(hs[0])
        o_ref[...] = jnp.where(valid, y, 0.0).astype(o_ref.dtype)

    ins, specs = [], []
    for q in range(nin):
        co = xo + q * ncb
        wo = q * ncb
        ins += [x, x, cw, cb.reshape(1, -1)]
        specs += [
            pl.BlockSpec((tb, cblk), lambda j, n, co=co: (n, co + j)),
            pl.BlockSpec((8, cblk), lambda j, n, co=co: (jnp.maximum(n * t8 - 1, 0), co + j)),
            pl.BlockSpec((ksz, cblk), lambda j, n, wo=wo: (0, wo + j)),
            pl.BlockSpec((1, cblk), lambda j, n, wo=wo: (0, wo + j)),
        ]
    return pl.pallas_call(
        kern, name=name, grid=(ncb, nb), in_specs=specs,
        out_specs=pl.BlockSpec((tb, cblk), lambda j, n: (n, j)),
        out_shape=jax.ShapeDtypeStruct((t, c_out), out_dtype),
        compiler_params=_cp("parallel", "parallel"),
    )(*ins)


def dwconv_bwd(x, cw, cb, dy, *, mode, x_off, c_out, cblk, name):
    t = x.shape[0]
    ksz = cw.shape[0]
    tb = _conv_tile(t)
    nb, ncb, t8 = t // tb, c_out // cblk, tb // 8
    xo = x_off // cblk
    nin = 2 if mode == "geglu" else 1
    ctot = nin * c_out

    def kern(*refs):
        dy_ref = refs[4 * nin]
        outs = refs[4 * nin + 1:4 * nin + 1 + 3 * nin]
        heads = refs[4 * nin + 1 + 3 * nin:]
        n = pl.program_id(1)
        blk = nb - 1 - n
        valid = (blk * tb + _rows((tb, cblk))) >= PAD

        @pl.when(n == 0)
        def _():
            for q in range(nin):
                heads[q][...] = jnp.zeros_like(heads[q])
                outs[3 * q + 1][...] = jnp.zeros_like(outs[3 * q + 1])
                outs[3 * q + 2][...] = jnp.zeros_like(outs[3 * q + 2])

        xs, tails, hs = [], [], []
        for q in range(nin):
            x_ref, t_ref, w_ref, b_ref = refs[4 * q:4 * q + 4]
            tail = jnp.where(blk > 0, t_ref[...], 0.0)
            xs.append(x_ref[...])
            tails.append(tail)
            hs.append(_conv_apply(x_ref[...], tail, w_ref[...], b_ref[...], ksz))
        dyv = dy_ref[...].astype(F32)
        if mode == "geglu":
            dhs = [dyv * hs[1] * _gelu_grad(hs[0]), dyv * _gelu(hs[0])]
        else:
            dhs = [dyv * _silu_grad(hs[0])]
        for q in range(nin):
            w_ref = refs[4 * q + 2]
            dx_ref, dw_ref, db_ref = outs[3 * q:3 * q + 3]
            dh = jnp.where(valid, dhs[q], 0.0)
            head = heads[q][...]
            dx = jnp.zeros_like(dh)
            dws = []
            for k in range(ksz):
                sh = ksz - 1 - k
                dx = dx + w_ref[k:k + 1, :] * _shift_up(dh, head, sh)
                dws.append(jnp.sum(dh * _shift_down(xs[q], tails[q], sh), axis=0, keepdims=True))
            dx_ref[...] = jnp.where(valid, dx, 0.0).astype(dx_ref.dtype)
            dw_ref[...] += jnp.concatenate(dws, axis=0)
            db_ref[...] += jnp.sum(dh, axis=0, keepdims=True)
            heads[q][...] = dh[0:8]

    ins, specs, out_specs, out_shape, scratch = [], [], [], [], []
    for q in range(nin):
        co = xo + q * ncb
        wo = q * ncb
        ins += [x, x, cw, cb.reshape(1, -1)]
        specs += [
            pl.BlockSpec((tb, cblk), lambda j, n, co=co: (nb - 1 - n, co + j)),
            pl.BlockSpec((8, cblk), lambda j, n, co=co: (jnp.maximum((nb - 1 - n) * t8 - 1, 0), co + j)),
            pl.BlockSpec((ksz, cblk), lambda j, n, wo=wo: (0, wo + j)),
            pl.BlockSpec((1, cblk), lambda j, n, wo=wo: (0, wo + j)),
        ]
        out_specs += [
            pl.BlockSpec((tb, cblk), lambda j, n: (nb - 1 - n, j)),
            pl.BlockSpec((ksz, cblk), lambda j, n: (0, j)),
            pl.BlockSpec((1, cblk), lambda j, n: (0, j)),
        ]
        out_shape += [jax.ShapeDtypeStruct((t, c_out), MXU),
                      jax.ShapeDtypeStruct((ksz, c_out), F32),
                      jax.ShapeDtypeStruct((1, c_out), F32)]
        scratch.append(pltpu.VMEM((8, cblk), F32))
    ins.append(dy)
    specs.append(pl.BlockSpec((tb, cblk), lambda j, n: (nb - 1 - n, j)))
    res = pl.pallas_call(
        kern, name=name, grid=(ncb, nb), in_specs=specs, out_specs=out_specs,
        out_shape=out_shape, scratch_shapes=scratch,
        compiler_params=_cp("parallel", "arbitrary"),
    )(*ins)
    dxs = [res[3 * q] for q in range(nin)]
    dcw = jnp.concatenate([res[3 * q + 1] for q in range(nin)], axis=1)
    dcb = jnp.concatenate([res[3 * q + 2] for q in range(nin)], axis=1)
    return dxs, dcw, dcb.reshape(ctot)


def _lru_tile(t):
    return _pick(t, (640, 384, 256, 128))


def _lru_gates(xc, wa, ba, wx, bx, sp):
    r = _sigmoid(_dot(xc, wa) + ba)
    i = _sigmoid(_dot(xc, wx) + bx)
    log_a = -LRU_C * r * sp
    a = jnp.exp(log_a)
    mult = jnp.sqrt(-_expm1(2.0 * log_a))
    return r, i, a, mult


def lru_fwd(proj, cw, cb, wa, ba, wx, bx, lam, *, gate_off, xr_off, name):
    t = proj.shape[0]
    tb = _lru_tile(t)
    nb, ns, t8 = t // tb, tb // BLK, tb // 8
    go, xo = gate_off // BLK, xr_off // BLK

    def kern(g_ref, x_ref, xt_ref, cw_ref, cb_ref, wa_ref, ba_ref, wx_ref, bx_ref, lam_ref,
             y_ref, h_ref, hc_ref):
        n = pl.program_id(1)

        @pl.when(n == 0)
        def _():
            hc_ref[...] = jnp.zeros_like(hc_ref)

        sp = _softplus(-lam_ref[...])
        hprev = hc_ref[0:1, :]
        for s in range(ns):
            sl = slice(s * BLK, (s + 1) * BLK)
            xv = x_ref[sl, :]
            tail = jnp.where(n > 0, xt_ref[...], 0.0) if s == 0 else x_ref[s * BLK - 8:s * BLK, :]
            valid = (n * tb + s * BLK + _rows((BLK, BLK))) >= PAD
            xc = jnp.where(valid, _conv_apply(xv, tail, cw_ref[...], cb_ref[...], 4), 0.0)
            _, i, a, mult = _lru_gates(xc, wa_ref[0], ba_ref[...], wx_ref[0], bx_ref[...], sp)
            u = mult * (i * xc)
            ca, cu = _scan_fwd(a, u)
            h = cu + ca * hprev
            hprev = _row_at(h, BLK - 1)
            h_ref[sl, :] = h
            y_ref[sl, :] = (_gelu(g_ref[sl, :]) * h).astype(y_ref.dtype)
        hc_ref[...] = jnp.broadcast_to(hprev, hc_ref.shape)

    vec = pl.BlockSpec((1, BLK), lambda j, n: (0, j))
    mat = pl.BlockSpec((1, BLK, BLK), lambda j, n: (j, 0, 0))
    return pl.pallas_call(
        kern, name=name, grid=(D_RNN // BLK, nb),
        in_specs=[
            pl.BlockSpec((tb, BLK), lambda j, n: (n, go + j)),
            pl.BlockSpec((tb, BLK), lambda j, n: (n, xo + j)),
            pl.BlockSpec((8, BLK), lambda j, n: (jnp.maximum(n * t8 - 1, 0), xo + j)),
            pl.BlockSpec((4, BLK), lambda j, n: (0, j)), vec, mat, vec, mat, vec, vec,
        ],
        out_specs=[pl.BlockSpec((tb, BLK), lambda j, n: (n, j)),
                   pl.BlockSpec((tb, BLK), lambda j, n: (n, j))],
        out_shape=[jax.ShapeDtypeStruct((t, D_RNN), MXU), jax.ShapeDtypeStruct((t, D_RNN), F32)],
        scratch_shapes=[pltpu.VMEM((8, BLK), F32)],
        compiler_params=_cp("parallel", "arbitrary"),
    )(proj, proj, proj, cw, cb.reshape(1, -1), wa, ba.reshape(1, -1), wx, bx.reshape(1, -1),
      lam.reshape(1, -1))


def lru_bwd(proj, h, dy, cw, cb, wa, ba, wx, bx, lam, *, gate_off, xr_off, dy_off, name):
    t = proj.shape[0]
    tb = _lru_tile(t)
    nb, ns, t8 = t // tb, tb // BLK, tb // 8
    go, xo, do = gate_off // BLK, xr_off // BLK, dy_off // BLK

    def kern(g_ref, x_ref, xt_ref, h_ref, ht_ref, dy_ref, cw_ref, cb_ref, wa_ref, ba_ref,
             wx_ref, bx_ref, lam_ref,
             dg_ref, dx_ref, dcw_ref, dcb_ref, dwa_ref, dba_ref, dwx_ref, dbx_ref, dlam_ref,
             gin_ref, head_ref):
        n = pl.program_id(1)
        blk = nb - 1 - n

        @pl.when(n == 0)
        def _():
            gin_ref[...] = jnp.zeros_like(gin_ref)
            head_ref[...] = jnp.zeros_like(head_ref)
            for r_ in (dcw_ref, dcb_ref, dwa_ref, dba_ref, dwx_ref, dbx_ref, dlam_ref):
                r_[...] = jnp.zeros_like(r_)

        lamv = lam_ref[...]
        sp = _softplus(-lamv)
        dsp_dlam = -_sigmoid(-lamv)
        g_in = gin_ref[0:1, :]
        head = head_ref[...]
        ones8 = jnp.ones((8, BLK), F32)
        for s in reversed(range(ns)):
            sl = slice(s * BLK, (s + 1) * BLK)
            xv = x_ref[sl, :]
            if s == 0:
                tail = jnp.where(blk > 0, xt_ref[...], 0.0)
                htail = jnp.where(blk > 0, ht_ref[...], 0.0)
            else:
                tail = x_ref[s * BLK - 8:s * BLK, :]
                htail = h_ref[s * BLK - 8:s * BLK, :]
            valid = (blk * tb + s * BLK + _rows((BLK, BLK))) >= PAD
            xc = jnp.where(valid, _conv_apply(xv, tail, cw_ref[...], cb_ref[...], 4), 0.0)
            wav, wxv = wa_ref[0], wx_ref[0]
            r, i, a, mult = _lru_gates(xc, wav, ba_ref[...], wxv, bx_ref[...], sp)
            hv = h_ref[sl, :]
            hprev = _shift_down(hv, htail, 1)
            gv = g_ref[sl, :]
            dyv = dy_ref[sl, :].astype(F32)
            dh = dyv * _gelu(gv)
            dg_ref[sl, :] = (dyv * hv * _gelu_grad(gv)).astype(dg_ref.dtype)
            c = _shift_up(a, ones8, 1)
            cc, cu = _scan_rev(c, dh)
            gg = cu + cc * g_in
            g_in = _row_at(a * gg, 0)
            da = gg * hprev
            di = gg * mult * xc
            dxc = gg * mult * i
            dmult = gg * i * xc
            dlog_a = da * a - dmult * (a * a) / mult
            dr = dlog_a * (-LRU_C * sp)
            dlam_ref[...] += jnp.sum(dlog_a * (-LRU_C) * r, axis=0, keepdims=True) * dsp_dlam
            dpr = dr * r * (1.0 - r)
            dpi = di * i * (1.0 - i)
            dxc = dxc + _dot_nt(dpr, wav) + _dot_nt(dpi, wxv)
            dxc = jnp.where(valid, dxc, 0.0)
            dpr = jnp.where(valid, dpr, 0.0)
            dpi = jnp.where(valid, dpi, 0.0)
            dwa_ref[0] += _dot_tn(xc, dpr)
            dwx_ref[0] += _dot_tn(xc, dpi)
            dba_ref[...] += jnp.sum(dpr, axis=0, keepdims=True)
            dbx_ref[...] += jnp.sum(dpi, axis=0, keepdims=True)
            dx = jnp.zeros_like(dxc)
            dws = []
            for k in range(4):
                dx = dx + cw_ref[k:k + 1, :] * _shift_up(dxc, head, 3 - k)
                dws.append(jnp.sum(dxc * _shift_down(xv, tail, 3 - k), axis=0, keepdims=True))
            dx_ref[sl, :] = jnp.where(valid, dx, 0.0).astype(dx_ref.dtype)
            dcw_ref[...] += jnp.concatenate(dws, axis=0)
            dcb_ref[...] += jnp.sum(dxc, axis=0, keepdims=True)
            head = dxc[0:8]
        gin_ref[...] = jnp.broadcast_to(g_in, gin_ref.shape)
        head_ref[...] = head

    vec = pl.BlockSpec((1, BLK), lambda j, n: (0, j))
    mat = pl.BlockSpec((1, BLK, BLK), lambda j, n: (j, 0, 0))
    cws = pl.BlockSpec((4, BLK), lambda j, n: (0, j))

    def rb(off):
        return pl.BlockSpec((tb, BLK), lambda j, n: (nb - 1 - n, off + j))

    def tl(off):
        return pl.BlockSpec((8, BLK), lambda j, n: (jnp.maximum((nb - 1 - n) * t8 - 1, 0), off + j))

    return pl.pallas_call(
        kern, name=name, grid=(D_RNN // BLK, nb),
        in_specs=[rb(go), rb(xo), tl(xo), rb(0), tl(0), rb(do), cws, vec, mat, vec, mat, vec, vec],
        out_specs=[rb(0), rb(0), cws, vec, mat, vec, mat, vec, vec],
        out_shape=[jax.ShapeDtypeStruct((t, D_RNN), MXU), jax.ShapeDtypeStruct((t, D_RNN), MXU),
                   jax.ShapeDtypeStruct((4, D_RNN), F32), jax.ShapeDtypeStruct((1, D_RNN), F32),
                   jax.ShapeDtypeStruct((8, BLK, BLK), F32), jax.ShapeDtypeStruct((1, D_RNN), F32),
                   jax.ShapeDtypeStruct((8, BLK, BLK), F32), jax.ShapeDtypeStruct((1, D_RNN), F32),
                   jax.ShapeDtypeStruct((1, D_RNN), F32)],
        scratch_shapes=[pltpu.VMEM((8, BLK), F32), pltpu.VMEM((8, BLK), F32)],
        compiler_params=_cp("parallel", "arbitrary"),
    )(proj, proj, proj, h, h, dy, cw, cb.reshape(1, -1), wa, ba.reshape(1, -1), wx,
      bx.reshape(1, -1), lam.reshape(1, -1))


_SCALE = HEAD_DIM ** -0.5


def _attn_masks(n):
    qi = _rows((BLK, 3 * BLK))
    c = _lanes((BLK, 3 * BLK))
    tq = n * BLK + qi - PAD
    s_band = (n - 1) * BLK + c - PAD
    d_band = tq - s_band
    ok_band = (s_band >= N_META) & (d_band >= 0) & (d_band < BLK)
    jm = c - 2 * BLK
    d_meta = tq - (jm - PAD)
    ok_meta = (jm >= PAD) & (d_meta >= 0)
    is_band = c < 2 * BLK
    ok = (is_band & ok_band) | (jnp.logical_not(is_band) & ok_meta)
    dist = jnp.where(is_band, d_band, jnp.minimum(d_meta, BLK)).astype(F32)
    return ok, dist


def _attn_probs(qm, kk, ok, dist, sk, h):
    sink = jnp.sum(jnp.where(_lanes(sk.shape) == h, sk, 0.0), axis=1, keepdims=True)
    slope = 2.0 ** (-8.0 * (h + 1) / N_Q_HEADS)
    s = _dot_nt(qm, kk) * _SCALE - slope * dist
    s = jnp.where(ok, s, NEG)
    mx = jnp.maximum(jnp.max(s, axis=-1, keepdims=True), sink)
    p = jnp.exp(s - mx)
    es = jnp.exp(sink - mx)
    inv = 1.0 / (jnp.sum(p, axis=-1, keepdims=True) + es)
    return p * inv, es * inv


def _attn_specs(t, q_off, k_off, v_off, rev):
    nb = t // BLK
    qo, ko, vo = q_off // 1024, k_off // BLK, v_off // BLK

    def b(n):
        return nb - 1 - n if rev else n

    return [
        pl.BlockSpec((BLK, 1024), lambda n: (b(n), qo)),
        pl.BlockSpec((BLK, BLK), lambda n: (b(n), ko)),
        pl.BlockSpec((BLK, BLK), lambda n: (b(n), vo)),
        pl.BlockSpec((BLK, BLK), lambda n: (jnp.maximum(b(n) - 1, 0), ko)),
        pl.BlockSpec((BLK, BLK), lambda n: (jnp.maximum(b(n) - 1, 0), vo)),
        pl.BlockSpec((BLK, BLK), lambda n: (0, ko)),
        pl.BlockSpec((BLK, BLK), lambda n: (0, vo)),
        pl.BlockSpec((1, BLK), lambda n: (0, 0)),
    ]


def attn_fwd(proj, sinks, *, q_off, k_off, v_off, name):
    t = proj.shape[0]
    nb = t // BLK

    def kern(q_ref, kc_ref, vc_ref, kp_ref, vp_ref, km_ref, vm_ref, sk_ref, o_ref):
        n = pl.program_id(0)
        ok, dist = _attn_masks(n)
        k_all = jnp.concatenate([kp_ref[...], kc_ref[...], km_ref[...]], axis=0)
        v_all = jnp.concatenate([vp_ref[...], vc_ref[...], vm_ref[...]], axis=0)
        k_alt = pltpu.roll(k_all, HEAD_DIM, 1)
        v_alt = pltpu.roll(v_all, HEAD_DIM, 1)
        low = _lanes((BLK, BLK)) < HEAD_DIM
        for hp in range(N_Q_HEADS // 2):
            qp = q_ref[:, hp * BLK:(hp + 1) * BLK]
            halves = []
            for e in range(2):
                h = 2 * hp + e
                g = h // 8
                qm = jnp.where(low == (e == 0), qp, 0.0)
                kk = k_all if g == e else k_alt
                vv = v_all if g == e else v_alt
                p, _ = _attn_probs(qm, kk, ok, dist, sk_ref[...], h)
                halves.append(_dot(p, vv))
            o_ref[:, hp * BLK:(hp + 1) * BLK] = jnp.where(low, halves[0], halves[1]).astype(o_ref.dtype)

    sk = jnp.zeros((1, BLK), F32).at[0, :N_Q_HEADS].set(sinks)
    return pl.pallas_call(
        kern, name=name, grid=(nb,),
        in_specs=_attn_specs(t, q_off, k_off, v_off, False),
        out_specs=pl.BlockSpec((BLK, 1024), lambda n: (n, 0)),
        out_shape=jax.ShapeDtypeStruct((t, 1024), MXU),
        compiler_params=_cp("parallel"),
    )(proj, proj, proj, proj, proj, proj, proj, sk)


def attn_bwd(proj, sinks, dy, *, q_off, k_off, v_off, dy_off, name):
    t = proj.shape[0]
    nb = t // BLK
    do = dy_off // 1024

    def kern(q_ref, kc_ref, vc_ref, kp_ref, vp_ref, km_ref, vm_ref, sk_ref, do_ref,
             dq_ref, dk_ref, dv_ref, dsk_ref, ck_ref, cv_ref, mk_ref, mv_ref):
        n = pl.program_id(0)
        blk = nb - 1 - n

        @pl.when(n == 0)
        def _():
            for r_ in (ck_ref, cv_ref, mk_ref, mv_ref, dsk_ref):
                r_[...] = jnp.zeros_like(r_)

        ok, dist = _attn_masks(blk)
        k_all = jnp.concatenate([kp_ref[...], kc_ref[...], km_ref[...]], axis=0)
        v_all = jnp.concatenate([vp_ref[...], vc_ref[...], vm_ref[...]], axis=0)
        k_alt = pltpu.roll(k_all, HEAD_DIM, 1)
        v_alt = pltpu.roll(v_all, HEAD_DIM, 1)
        low = _lanes((BLK, BLK)) < HEAD_DIM
        lane1 = _lanes((1, BLK))
        dk_all = jnp.zeros((3 * BLK, BLK), F32)
        dv_all = jnp.zeros((3 * BLK, BLK), F32)
        dsk = jnp.zeros((1, BLK), F32)
        for hp in range(N_Q_HEADS // 2):
            qp = q_ref[:, hp * BLK:(hp + 1) * BLK]
            dop = do_ref[:, hp * BLK:(hp + 1) * BLK].astype(F32)
            halves = []
            for e in range(2):
                h = 2 * hp + e
                g = h // 8
                sel = low == (e == 0)
                qm = jnp.where(sel, qp, 0.0)
                dom = jnp.where(sel, dop, 0.0)
                kk = k_all if g == e else k_alt
                vv = v_all if g == e else v_alt
                p, psink = _attn_probs(qm, kk, ok, dist, sk_ref[...], h)
                dp = _dot_nt(dom, vv)
                delta = jnp.sum(p * dp, axis=-1, keepdims=True)
                ds = p * (dp - delta) * _SCALE
                dsk = dsk + jnp.where(lane1 == h, -jnp.sum(psink * delta, axis=0, keepdims=True), 0.0)
                halves.append(_dot(ds, kk))
                dkh = _dot_tn(ds, qm)
                dvh = _dot_tn(p, dom)
                if g != e:
                    dkh = pltpu.roll(dkh, HEAD_DIM, 1)
                    dvh = pltpu.roll(dvh, HEAD_DIM, 1)
                dk_all = dk_all + dkh
                dv_all = dv_all + dvh
            dq_ref[:, hp * BLK:(hp + 1) * BLK] = jnp.where(low, halves[0], halves[1]).astype(dq_ref.dtype)
        dsk_ref[...] += dsk
        mk_ref[...] += dk_all[2 * BLK:3 * BLK]
        mv_ref[...] += dv_all[2 * BLK:3 * BLK]
        is0 = blk == 0
        dk_ref[...] = (dk_all[BLK:2 * BLK] + ck_ref[...] + jnp.where(is0, mk_ref[...], 0.0)).astype(dk_ref.dtype)
        dv_ref[...] = (dv_all[BLK:2 * BLK] + cv_ref[...] + jnp.where(is0, mv_ref[...], 0.0)).astype(dv_ref.dtype)
        ck_ref[...] = dk_all[0:BLK]
        cv_ref[...] = dv_all[0:BLK]

    sk = jnp.zeros((1, BLK), F32).at[0, :N_Q_HEADS].set(sinks)
    kv = pl.BlockSpec((BLK, BLK), lambda n: (nb - 1 - n, 0))
    res = pl.pallas_call(
        kern, name=name, grid=(nb,),
        in_specs=_attn_specs(t, q_off, k_off, v_off, True)
        + [pl.BlockSpec((BLK, 1024), lambda n: (nb - 1 - n, do))],
        out_specs=[pl.BlockSpec((BLK, 1024), lambda n: (nb - 1 - n, 0)), kv, kv,
                   pl.BlockSpec((1, BLK), lambda n: (0, 0))],
        out_shape=[jax.ShapeDtypeStruct((t, 1024), MXU), jax.ShapeDtypeStruct((t, BLK), MXU),
                   jax.ShapeDtypeStruct((t, BLK), MXU), jax.ShapeDtypeStruct((1, BLK), F32)],
        scratch_shapes=[pltpu.VMEM((BLK, BLK), F32)] * 4,
        compiler_params=_cp("arbitrary"),
    )(proj, proj, proj, proj, proj, proj, proj, sk, dy)
    return res[0], res[1], res[2], res[3][0, :N_Q_HEADS]


GW = D_SSM // SSD_GROUPS
EXP_ROWS = 3 * BLK + 8
RED_ROWS = EXP_ROWS + 8


def _head_expand():
    ch = jnp.arange(D_SSM) // HEAD_DIM
    return (jnp.arange(BLK)[:, None] == ch[None, :]).astype(BF16)


def _ssd_decay(raw, dtb, alog, rowv):
    valid = rowv & (_lanes((BLK, BLK)) < SSD_HEADS)
    pre = raw + dtb
    dtp = jnp.where(valid, _softplus(pre), 0.0)
    av = -jnp.exp(alog)
    cs = _cumsum_fwd(dtp * av)
    cs_last = _row_at(cs, BLK - 1)
    return valid, pre, dtp, av, cs, jnp.exp(cs), jnp.exp(cs_last - cs), jnp.exp(cs_last)


def _head_col(x, h):
    return jnp.sum(jnp.where(_lanes(x.shape) == h, x, 0.0), axis=1, keepdims=True)


def _ssd_group_fwd(g, xdt, cs, cst, bg, cg, tril, low):
    cb = _dot_nt(cg, bg)
    ys, lm = [], []
    for j in range(2):
        xp = xdt[:, g * GW + j * BLK:g * GW + (j + 1) * BLK]
        hv = []
        for e in range(2):
            h = 4 * g + 2 * j + e
            seg = _head_col(cs, h) - _row_at(cst, h)
            lmat = jnp.where(tril, jnp.exp(jnp.minimum(seg, 0.0)), 0.0)
            mmat = cb * lmat
            lm.append((lmat, mmat))
            hv.append(_dot(mmat, xp))
        ys.append(jnp.where(low, hv[0], hv[1]))
    return jnp.concatenate(ys, axis=1), lm


def _ssd_specs(t, z_off, dt_off, rev):
    nb = t // BLK
    zo, dto = z_off // D_SSM, dt_off // BLK

    def b(n):
        return nb - 1 - n if rev else n

    vec = lambda w: pl.BlockSpec((1, w), lambda n: (0, 0))
    return [
        pl.BlockSpec((BLK, D_SSM), lambda n: (b(n), 0)),
        pl.BlockSpec((BLK, 1024), lambda n: (b(n), 2)),
        pl.BlockSpec((BLK, 1024), lambda n: (b(n), 3)),
        pl.BlockSpec((BLK, D_SSM), lambda n: (b(n), zo)),
        pl.BlockSpec((BLK, BLK), lambda n: (b(n), dto)),
        vec(BLK), vec(BLK), vec(D_SSM), vec(D_SSM),
        pl.BlockSpec((BLK, D_SSM), lambda n: (0, 0)),
    ]


def _pad128(v):
    return jnp.zeros((1, BLK), F32).at[0, :v.shape[0]].set(v)


def ssd_fwd(xbc, proj, dt_bias, a_log, d_skip, gate_norm, *, z_off, dt_off, name):
    t = xbc.shape[0]
    nb = t // BLK

    def kern(x_ref, b_ref, c_ref, z_ref, dt_ref, dtb_ref, alog_ref, dsk_ref, gn_ref, e_ref,
             yn_ref, st_ref, p_ref):
        n = pl.program_id(0)

        @pl.when(n == 0)
        def _():
            p_ref[...] = jnp.zeros_like(p_ref)

        rowv = (n * BLK + _rows((BLK, BLK))) >= PAD
        _, _, dtp, _, cs, ecs, w, dec = _ssd_decay(dt_ref[...], dtb_ref[...], alog_ref[...], rowv)
        ex = _dot_split(jnp.concatenate([dtp, ecs, w, jnp.broadcast_to(dec, (8, BLK))], axis=0),
                        e_ref[...])
        dtp_c, ecs_c, w_c = ex[0:BLK], ex[BLK:2 * BLK], ex[2 * BLK:3 * BLK]
        dec_c = jnp.max(ex[3 * BLK:EXP_ROWS], axis=0, keepdims=True)
        xv = x_ref[...]
        xdt = xv * dtp_c
        wx = w_c * xdt
        cst = cs.T
        tril = _rows((BLK, BLK)) >= _lanes((BLK, BLK))
        low = _lanes((BLK, BLK)) < HEAD_DIM
        st_ref[0] = p_ref[...]
        for g in range(SSD_GROUPS):
            gs = slice(g * GW, (g + 1) * GW)
            bg = b_ref[:, g * BLK:(g + 1) * BLK]
            cg = c_ref[:, g * BLK:(g + 1) * BLK]
            pg = p_ref[g]
            ydiag, _ = _ssd_group_fwd(g, xdt, cs, cst, bg, cg, tril, low)
            y = ydiag + _dot(cg, pg) * ecs_c[:, gs] + dsk_ref[:, gs] * xv[:, gs]
            p_ref[g] = pg * dec_c[:, gs] + _dot_tn(bg, wx[:, gs])
            yz = y * _silu(z_ref[:, gs])
            r = lax.rsqrt(jnp.mean(yz * yz, axis=-1, keepdims=True) + EPS)
            yn_ref[:, gs] = (yz * r * gn_ref[:, gs]).astype(yn_ref.dtype)

    return pl.pallas_call(
        kern, name=name, grid=(nb,),
        in_specs=_ssd_specs(t, z_off, dt_off, False),
        out_specs=[pl.BlockSpec((BLK, D_SSM), lambda n: (n, 0)),
                   pl.BlockSpec((1, SSD_GROUPS, BLK, GW), lambda n: (n, 0, 0, 0))],
        out_shape=[jax.ShapeDtypeStruct((t, D_SSM), MXU),
                   jax.ShapeDtypeStruct((nb, SSD_GROUPS, BLK, GW), F32)],
        scratch_shapes=[pltpu.VMEM((SSD_GROUPS, BLK, GW), F32)],
        compiler_params=_cp("arbitrary"),
    )(xbc, xbc, xbc, proj, proj, _pad128(dt_bias), _pad128(a_log),
      jnp.repeat(d_skip, HEAD_DIM).reshape(1, D_SSM), gate_norm.reshape(1, D_SSM), _head_expand())


def ssd_bwd(xbc, proj, st, dyn, dt_bias, a_log, d_skip, gate_norm, *, z_off, dt_off, name):
    t = xbc.shape[0]
    nb = t // BLK

    def kern(x_ref, b_ref, c_ref, z_ref, dt_ref, dtb_ref, alog_ref, dsk_ref, gn_ref, e_ref,
             et_ref, st_ref, dyn_ref,
             dxbc_ref, dz_ref, draw_ref, dgn_ref, ddsk_ref, ddtb_ref, dalog_ref,
             dp_ref, tr_ref):
        n = pl.program_id(0)
        blk = nb - 1 - n

        @pl.when(n == 0)
        def _():
            for r_ in (dp_ref, dgn_ref, ddsk_ref, ddtb_ref, dalog_ref):
                r_[...] = jnp.zeros_like(r_)

        rowv = (blk * BLK + _rows((BLK, BLK))) >= PAD
        valid, pre, dtp, av, cs, ecs, w, dec = _ssd_decay(dt_ref[...], dtb_ref[...], alog_ref[...], rowv)
        ex = _dot_split(jnp.concatenate([dtp, ecs, w, jnp.broadcast_to(dec, (8, BLK))], axis=0),
                        e_ref[...])
        dtp_c, ecs_c, w_c = ex[0:BLK], ex[BLK:2 * BLK], ex[2 * BLK:3 * BLK]
        dec_c = jnp.max(ex[3 * BLK:EXP_ROWS], axis=0, keepdims=True)
        xv = x_ref[...]
        xdt = xv * dtp_c
        wx = w_c * xdt
        cst = cs.T
        tril = _rows((BLK, BLK)) >= _lanes((BLK, BLK))
        lane = _lanes((BLK, BLK))
        rowi = _rows((BLK, BLK))
        low = lane < HEAD_DIM
        dcs = jnp.zeros((BLK, BLK), F32)
        dcst = jnp.zeros((BLK, BLK), F32)
        for g in range(SSD_GROUPS):
            gs = slice(g * GW, (g + 1) * GW)
            bg = b_ref[:, g * BLK:(g + 1) * BLK]
            cg = c_ref[:, g * BLK:(g + 1) * BLK]
            pg = st_ref[0, g]
            dpn = dp_ref[g]
            xg = xv[:, gs]
            ydiag, lm = _ssd_group_fwd(g, xdt, cs, cst, bg, cg, tril, low)
            yoff = _dot(cg, pg) * ecs_c[:, gs]
            y = ydiag + yoff + dsk_ref[:, gs] * xg
            zz = z_ref[:, gs]
            sz = _silu(zz)
            yz = y * sz
            r = lax.rsqrt(jnp.mean(yz * yz, axis=-1, keepdims=True) + EPS)
            yhat = yz * r
            dynv = dyn_ref[:, gs].astype(F32)
            gy = dynv * gn_ref[:, gs]
            dgn_ref[:, gs] += jnp.sum(dynv * yhat, axis=0, keepdims=True)
            dyz = r * (gy - yhat * jnp.mean(gy * yhat, axis=-1, keepdims=True))
            dy = dyz * sz
            dz_ref[:, gs] = (dyz * y * _silu_grad(zz)).astype(dz_ref.dtype)
            tr_ref[EXP_ROWS:RED_ROWS, gs] = jnp.broadcast_to(
                jnp.sum(dy * xg, axis=0, keepdims=True), (8, GW))
            dx = dsk_ref[:, gs] * dy
            dwx = _dot(bg, dpn)
            dxdt = w_c[:, gs] * dwx
            tr_ref[0:BLK, gs] = dwx * wx[:, gs]
            dbg = _dot_nt(wx[:, gs], dpn)
            dzo = ecs_c[:, gs] * dy
            tr_ref[BLK:2 * BLK, gs] = dy * yoff
            dcg = _dot_nt(dzo, pg)
            dp_ref[g] = dec_c[:, gs] * dpn + _dot_tn(cg, dzo)
            tr_ref[3 * BLK:EXP_ROWS, gs] = jnp.broadcast_to(
                jnp.sum(dpn * pg, axis=0, keepdims=True), (8, GW))
            dcb = jnp.zeros((BLK, BLK), F32)
            pairs = []
            for j in range(2):
                ps = slice(g * GW + j * BLK, g * GW + (j + 1) * BLK)
                xp = xdt[:, ps]
                dyp = dy[:, j * BLK:(j + 1) * BLK]
                acc = jnp.zeros((BLK, BLK), F32)
                for e in range(2):
                    h = 4 * g + 2 * j + e
                    lmat, mmat = lm[2 * j + e]
                    dyh = jnp.where(low == (e == 0), dyp, 0.0)
                    dm = jnp.where(tril, _dot_nt(dyh, xp), 0.0)
                    nh = dm * mmat
                    dcs = dcs + jnp.where(lane == h, jnp.sum(nh, axis=1, keepdims=True), 0.0)
                    dcst = dcst - jnp.where(rowi == h, jnp.sum(nh, axis=0, keepdims=True), 0.0)
                    dcb = dcb + dm * lmat
                    acc = acc + _dot_tn(mmat, dyh)
                pairs.append(acc)
            dxdt = dxdt + jnp.concatenate(pairs, axis=1)
            dcg = dcg + _dot(dcb, bg)
            dbg = dbg + _dot_tn(dcb, cg)
            tr_ref[2 * BLK:3 * BLK, gs] = dxdt * xg
            dxbc_ref[:, gs] = dx + dxdt * dtp_c[:, gs]
            dxbc_ref[:, D_SSM + g * BLK:D_SSM + (g + 1) * BLK] = dbg
            dxbc_ref[:, D_SSM + 1024 + g * BLK:D_SSM + 1024 + (g + 1) * BLK] = dcg
        red = _dot_split(tr_ref[...], et_ref[...])
        r1, r2, r3 = red[0:BLK], red[BLK:2 * BLK], red[2 * BLK:3 * BLK]
        ddec = jnp.max(red[3 * BLK:EXP_ROWS], axis=0, keepdims=True)
        ddsk_ref[...] += jnp.max(red[EXP_ROWS:RED_ROWS], axis=0, keepdims=True)
        dcs = dcs + dcst.T - r1 + r2
        dcs_last = jnp.sum(r1, axis=0, keepdims=True) + ddec * dec
        dcs = dcs + jnp.where(rowi == BLK - 1, dcs_last, 0.0)
        dda = _cumsum_rev(dcs)
        ddtp = r3 + dda * av
        dalog_ref[...] += jnp.sum(dda * dtp, axis=0, keepdims=True) * av
        draw = jnp.where(valid, ddtp * _sigmoid(pre), 0.0)
        ddtb_ref[...] += jnp.sum(draw, axis=0, keepdims=True)
        draw_ref[...] = draw.astype(draw_ref.dtype)

    vec = lambda w_: pl.BlockSpec((1, w_), lambda n: (0, 0))
    rb = lambda w_: pl.BlockSpec((BLK, w_), lambda n: (nb - 1 - n, 0))
    e = _head_expand()
    res = pl.pallas_call(
        kern, name=name, grid=(nb,),
        in_specs=_ssd_specs(t, z_off, dt_off, True)
        + [pl.BlockSpec((D_SSM, BLK), lambda n: (0, 0)),
           pl.BlockSpec((1, SSD_GROUPS, BLK, GW), lambda n: (nb - 1 - n, 0, 0, 0)),
           rb(D_SSM)],
        out_specs=[rb(2 * D_SSM), rb(D_SSM), rb(BLK), vec(D_SSM), vec(BLK), vec(BLK), vec(BLK)],
        out_shape=[jax.ShapeDtypeStruct((t, 2 * D_SSM), F32), jax.ShapeDtypeStruct((t, D_SSM), MXU),
                   jax.ShapeDtypeStruct((t, BLK), MXU), jax.ShapeDtypeStruct((1, D_SSM), F32),
                   jax.ShapeDtypeStruct((1, BLK), F32), jax.ShapeDtypeStruct((1, BLK), F32),
                   jax.ShapeDtypeStruct((1, BLK), F32)],
        scratch_shapes=[pltpu.VMEM((SSD_GROUPS, BLK, GW), F32), pltpu.VMEM((RED_ROWS, D_SSM), F32)],
        compiler_params=_cp("arbitrary"),
    )(xbc, xbc, xbc, proj, proj, _pad128(dt_bias), _pad128(a_log),
      jnp.repeat(d_skip, HEAD_DIM).reshape(1, D_SSM), gate_norm.reshape(1, D_SSM), e, e.T, st, dyn)
    dxbc, dz, draw, dgn, ddsk, ddtb, dalog = res
    return dxbc, dz, draw, dgn[0], ddsk[0, :SSD_HEADS], ddtb[0, :SSD_HEADS], dalog[0, :SSD_HEADS]


def loss_fwd_bwd(h, target, *, name):
    t, d = h.shape
    nb = t // BLK

    def kern(h_ref, t_ref, loss_ref, dh_ref):
        n = pl.program_id(0)
        err = jnp.where(n > 0, h_ref[...] - t_ref[...], 0.0)
        dh_ref[...] = err * (1.0 / d)
        part = (0.5 / d) * jnp.sum(jnp.sum(err * err, axis=1, keepdims=True), axis=0, keepdims=True)

        @pl.when(n == 0)
        def _():
            loss_ref[...] = part

        @pl.when(n > 0)
        def _():
            loss_ref[...] += part

    return pl.pallas_call(
        kern, name=name, grid=(nb,),
        in_specs=[pl.BlockSpec((BLK, d), lambda n: (n, 0)),
                  pl.BlockSpec((BLK, d), lambda n: (jnp.maximum(n - 1, 0), 0))],
        out_specs=[pl.BlockSpec((1, 1), lambda n: (0, 0)), pl.BlockSpec((BLK, d), lambda n: (n, 0))],
        out_shape=[jax.ShapeDtypeStruct((1, 1), F32), jax.ShapeDtypeStruct((t, d), F32)],
        compiler_params=_cp("arbitrary"),
    )(h, target)


def adamw(parts, w, m, v, *, name):
    npart, r, c = parts.shape
    tr = _pick(r, (256, 128)) if c >= PACK_W else _pick(r, (768, 640, 512, 384, 256, 128))

    def kern(p_ref, w_ref, m_ref, v_ref, g_ref, d_ref, m2_ref, v2_ref):
        g = p_ref[0]
        for k in range(1, npart):
            g = g + p_ref[k]
        m2 = ADAM_B1 * m_ref[...] + (1.0 - ADAM_B1) * g
        v2 = ADAM_B2 * v_ref[...] + (1.0 - ADAM_B2) * (g * g)
        m_hat = m2 / (1.0 - ADAM_B1 ** ADAM_STEP)
        v_hat = v2 / (1.0 - ADAM_B2 ** ADAM_STEP)
        g_ref[...] = g
        d_ref[...] = -ADAM_LR * (m_hat / (jnp.sqrt(v_hat) + ADAM_EPS) + ADAM_WD * w_ref[...])
        m2_ref[...] = m2
        v2_ref[...] = v2

    row = pl.BlockSpec((tr, c), lambda i: (i, 0))
    sds = jax.ShapeDtypeStruct((r, c), F32)
    return pl.pallas_call(
        kern, name=name, grid=(r // tr,),
        in_specs=[pl.BlockSpec((npart, tr, c), lambda i: (0, i, 0)), row, row, row],
        out_specs=[row, row, row, row], out_shape=[sds, sds, sds, sds],
        compiler_params=_cp("parallel"),
    )(parts, w, m, v)


def pair_add(p, land, *, name):
    _, r, c = p.shape
    tr = _pick(r, (768, 640, 512, 384, 256, 128))
    core = lax.axis_index("c").astype(jnp.int32).reshape(1)

    def kern(c_ref, p_ref, l_ref, o_ref):
        o_ref[...] = p_ref[...] + l_ref[...]

    return pl.pallas_call(
        kern, name=name,
        grid_spec=pltpu.PrefetchScalarGridSpec(
            num_scalar_prefetch=1, grid=(4, r // tr),
            in_specs=[pl.BlockSpec((1, tr, c), lambda k, i, c_ref: (2 * k + c_ref[0], i, 0)),
                      pl.BlockSpec((1, tr, c), lambda k, i, c_ref: (k, i, 0))],
            out_specs=pl.BlockSpec((1, tr, c), lambda k, i, c_ref: (k, i, 0))),
        out_shape=jax.ShapeDtypeStruct((4, r, c), F32),
        compiler_params=_cp("parallel", "parallel"),
    )(core, p, land)


def _me():
    return lax.axis_index("x"), lax.axis_index("y"), lax.axis_index("c")


def all_gather(x, *, name):
    def body(x_ref, out_ref, send_sems, recv_sems, local_sem):
        mx, my, mc = _me()
        me, sib = (mx, my, mc), (mx, my, 1 - mc)
        chips = [(1 - mx, my), (mx, 1 - my), (1 - mx, 1 - my)]

        def rows(px, py, pc):
            return out_ref.at[4 * px + 2 * py + pc]

        def copy(k, block, to, src=None):
            return pltpu.make_async_remote_copy(
                src_ref=rows(*block) if src is None else src, dst_ref=rows(*block),
                send_sem=send_sems.at[k], recv_sem=recv_sems.at[k],
                device_id=to, device_id_type=MESH)

        mine = pltpu.make_async_copy(x_ref, rows(*me), local_sem)
        mine.start()
        first = [copy(0, me, sib, src=x_ref)]
        first += [copy(1 + j, me, (*chip, mc), src=x_ref) for j, chip in enumerate(chips)]
        for cp in first:
            cp.start()
        passed = [copy(4 + j, (*chip, mc), sib) for j, chip in enumerate(chips)]
        for j, chip in enumerate(chips):
            copy(1 + j, (*chip, mc), me).wait_recv()
            passed[j].start()
        copy(0, sib, me).wait_recv()
        for j, chip in enumerate(chips):
            copy(4 + j, (*chip, 1 - mc), me).wait_recv()
        for cp in first + passed:
            cp.wait_send()
        mine.wait()

    return pl.pallas_call(
        body, name=name,
        out_shape=jax.ShapeDtypeStruct((N_DEV,) + x.shape, x.dtype),
        in_specs=[ANY], out_specs=ANY,
        scratch_shapes=[pltpu.SemaphoreType.DMA((7,)), pltpu.SemaphoreType.DMA((7,)),
                        pltpu.SemaphoreType.DMA(())],
    )(x)


def pair_exchange(p, *, name):
    _, r, c = p.shape

    def body(p_ref, out_ref, send_sems, recv_sems):
        mx, my, mc = _me()
        cps = [pltpu.make_async_remote_copy(
            src_ref=p_ref.at[2 * k + (1 - mc)], dst_ref=out_ref.at[k],
            send_sem=send_sems.at[k], recv_sem=recv_sems.at[k],
            device_id=(mx, my, 1 - mc), device_id_type=MESH) for k in range(4)]
        for cp in cps:
            cp.start()
        for cp in cps:
            cp.wait_recv()
        for cp in cps:
            cp.wait_send()

    return pl.pallas_call(
        body, name=name, out_shape=jax.ShapeDtypeStruct((4, r, c), p.dtype),
        in_specs=[ANY], out_specs=ANY,
        scratch_shapes=[pltpu.SemaphoreType.DMA((4,)), pltpu.SemaphoreType.DMA((4,))],
    )(p)


def chip_exchange(q, *, name):
    _, r, c = q.shape

    def body(q_ref, out_ref, send_sems, recv_sems, local_sem):
        mx, my, mc = _me()
        mine = 2 * mx + my
        local = pltpu.make_async_copy(q_ref.at[mine], out_ref.at[mine], local_sem)
        local.start()
        chips = [(1 - mx, my), (mx, 1 - my), (1 - mx, 1 - my)]
        sends = [pltpu.make_async_remote_copy(
            src_ref=q_ref.at[2 * px + py], dst_ref=out_ref.at[mine],
            send_sem=send_sems.at[k], recv_sem=recv_sems.at[k],
            device_id=(px, py, mc), device_id_type=MESH) for k, (px, py) in enumerate(chips)]
        recvs = [pltpu.make_async_remote_copy(
            src_ref=q_ref.at[mine], dst_ref=out_ref.at[2 * px + py],
            send_sem=send_sems.at[k], recv_sem=recv_sems.at[k],
            device_id=(px, py, mc), device_id_type=MESH) for k, (px, py) in enumerate(chips)]
        for cp in sends:
            cp.start()
        for cp in recvs:
            cp.wait_recv()
        for cp in sends:
            cp.wait_send()
        local.wait()

    return pl.pallas_call(
        body, name=name, out_shape=jax.ShapeDtypeStruct((4, r, c), q.dtype),
        in_specs=[ANY], out_specs=ANY,
        scratch_shapes=[pltpu.SemaphoreType.DMA((3,)), pltpu.SemaphoreType.DMA((3,)),
                        pltpu.SemaphoreType.DMA(())],
    )(q)


WEIGHTS = [
    "meta_tokens", "l0_mix_pre_norm", "l0_mix_post_norm", "l0_w_in", "l0_lru_conv_w", "l0_lru_conv_b",
    "l0_lru_w_a", "l0_lru_b_a", "l0_lru_w_x", "l0_lru_b_x", "l0_lru_lambda", "l0_attn_sinks", "l0_w_out",
    "l0_ffn_pre_norm", "l0_ffn_post_norm", "l0_ffn_w_up", "l0_ffn_conv_w", "l0_ffn_conv_b", "l0_ffn_w_down",
    "l1_mix_pre_norm", "l1_mix_post_norm", "l1_w_in", "l1_ssm_conv_w", "l1_ssm_conv_b", "l1_dt_bias",
    "l1_a_log", "l1_d_skip", "l1_gate_norm", "l1_w_out", "l1_ffn_pre_norm", "l1_ffn_post_norm",
    "l1_ffn_w_up", "l1_ffn_conv_w", "l1_ffn_conv_b", "l1_ffn_w_down",
]
INPUTS = (["x"] + WEIGHTS + ["loss_target"] + ["m_" + n for n in WEIGHTS] + ["v_" + n for n in WEIGHTS])

MATS = {"l0_w_in": ("col", (1024, 3328)), "l0_w_out": ("row", (2048, 1024)),
        "l0_ffn_w_up": ("col", (1024, 5632)), "l0_ffn_w_down": ("row", (2816, 1024)),
        "l1_w_in": ("col", (1024, 6176)), "l1_w_out": ("row", (2048, 1024)),
        "l1_ffn_w_up": ("col", (1024, 5632)), "l1_ffn_w_down": ("row", (2816, 1024))}
SMALL_SHARDED = {"meta_tokens": ("col", (16, 1024)), "l0_lru_conv_w": ("col", (4, 1024)),
                 "l0_ffn_conv_w": ("col", (3, 5632)), "l1_ssm_conv_w": ("col", (4, 4096)),
                 "l1_ffn_conv_w": ("col", (3, 5632))}
SHARDED = {**MATS, **SMALL_SHARDED}
REPLICATED = [n for n in WEIGHTS if n not in SHARDED]
PACK_W = 1024
SMALL_W = 128


def _shard_shape(name):
    kind, (r, c) = SHARDED[name]
    return (r, c // N_DEV) if kind == "col" else (r // N_DEV, c)


def _rows_of(numel, width):
    return -(-numel // width)


def _to_rows(a, width):
    flat = a.reshape(-1)
    rows = _rows_of(flat.shape[0], width)
    return jnp.pad(flat, (0, rows * width - flat.shape[0])).reshape(rows, width)


def _pack(arrs, width, total_rows):
    slab = jnp.concatenate([_to_rows(a, width) for a in arrs], axis=0)
    return jnp.pad(slab, ((0, total_rows - slab.shape[0]), (0, 0)))


def _unpack(slab, shapes, width):
    out, off = [], 0
    for shp in shapes:
        numel = math.prod(shp)
        rows = _rows_of(numel, width)
        out.append(slab[off:off + rows].reshape(-1)[:numel].reshape(shp))
        off += rows
    return out


def _round_up(n, m):
    return -(-n // m) * m


def _split_by_dest(name, g):
    kind, (r, c) = SHARDED[name]
    if kind == "col":
        blocks = g.reshape(r, N_DEV, c // N_DEV).transpose(1, 0, 2)
    else:
        blocks = g.reshape(N_DEV, r // N_DEV, c)
    flat = blocks.reshape(N_DEV, -1)
    rows = _rows_of(flat.shape[1], PACK_W)
    return jnp.pad(flat, ((0, 0), (0, rows * PACK_W - flat.shape[1]))).reshape(N_DEV, rows, PACK_W)


def _gather_weights(a):
    names = list(MATS)
    rows = _round_up(sum(_rows_of(math.prod(_shard_shape(n)), PACK_W) for n in names), 16)
    slab = _pack([a[n].astype(MXU) for n in names], PACK_W, rows)
    got = all_gather(slab, name="gather_weights")
    full, off = {}, 0
    for n in names:
        kind, (r, c) = MATS[n]
        sr, sc = _shard_shape(n)
        nrows = _rows_of(sr * sc, PACK_W)
        blk = got[:, off:off + nrows].reshape(N_DEV, -1)[:, :sr * sc].reshape(N_DEV, sr, sc)
        full[n] = blk.transpose(1, 0, 2).reshape(r, c) if kind == "col" else blk.reshape(r, c)
        off += nrows
    return full


def _gather_small(a):
    names = list(SMALL_SHARDED)
    shapes = [_shard_shape(n) for n in names]
    rows = _round_up(sum(_rows_of(math.prod(s), SMALL_W) for s in shapes), 8)
    got = all_gather(_pack([a[n] for n in names], SMALL_W, rows), name="gather_small")
    full, off = {}, 0
    for n, (sr, sc) in zip(names, shapes):
        nrows = _rows_of(sr * sc, SMALL_W)
        blk = got[:, off:off + nrows].reshape(N_DEV, -1)[:, :sr * sc].reshape(N_DEV, sr, sc)
        full[n] = blk.transpose(1, 0, 2).reshape(SMALL_SHARDED[n][1])
        off += nrows
    return full


L1_IN_PAD = 6272


def _ffn_fwd(h, a, w, pfx):
    u = rmsnorm_fwd(h, a[pfx + "ffn_pre_norm"], out_dtype=MXU, name=pfx + "ffn_pre")
    up = matmul(u, w[pfx + "ffn_w_up"], name=pfx + "ffn_up")
    act = dwconv_fwd(up, a[pfx + "ffn_conv_w"], a[pfx + "ffn_conv_b"], mode="geglu", x_off=0,
                     c_out=D_FF, cblk=256, out_dtype=MXU, name=pfx + "ffn_act")
    down = matmul(act, w[pfx + "ffn_w_down"], name=pfx + "ffn_down")
    out = rmsnorm_fwd(down, a[pfx + "ffn_post_norm"], res=h, out_dtype=F32, name=pfx + "ffn_post")
    return out, (h, u, up, act, down)


def _ffn_bwd(dh, saved, a, w, pfx, g):
    h, u, up, act, down = saved
    dd, g[pfx + "ffn_post_norm"] = rmsnorm_bwd(down, a[pfx + "ffn_post_norm"], dh, out_dtype=MXU,
                                               name=pfx + "ffn_post_bwd")
    dact = matmul(dd, w[pfx + "ffn_w_down"], trans_b=True, name=pfx + "ffn_down_dx")
    g[pfx + "ffn_w_down"] = matmul(act.T, dd, name=pfx + "ffn_down_dw")
    dups, g[pfx + "ffn_conv_w"], g[pfx + "ffn_conv_b"] = dwconv_bwd(
        up, a[pfx + "ffn_conv_w"], a[pfx + "ffn_conv_b"], dact, mode="geglu", x_off=0, c_out=D_FF,
        cblk=256, name=pfx + "ffn_act_bwd")
    dup = jnp.concatenate(dups, axis=1)
    g[pfx + "ffn_w_up"] = matmul(u.T, dup, name=pfx + "ffn_up_dw")
    du = matmul(dup, w[pfx + "ffn_w_up"], trans_b=True, name=pfx + "ffn_up_dx")
    dh_in, g[pfx + "ffn_pre_norm"] = rmsnorm_bwd(h, a[pfx + "ffn_pre_norm"], du, res=dh, out_dtype=F32,
                                                 name=pfx + "ffn_pre_bwd")
    return dh_in


def _local_step(a, w):
    x = a["x"][0]
    seq = x.shape[0]
    h0 = jnp.concatenate([jnp.zeros((PAD, D_MODEL), F32), a["meta_tokens"], x], axis=0)
    g = {}

    u0 = rmsnorm_fwd(h0, a["l0_mix_pre_norm"], out_dtype=MXU, name="l0_mix_pre")
    proj0 = matmul(u0, w["l0_w_in"], name="l0_in")
    lru = (a["l0_lru_conv_w"], a["l0_lru_conv_b"], a["l0_lru_w_a"], a["l0_lru_b_a"], a["l0_lru_w_x"],
           a["l0_lru_b_x"], a["l0_lru_lambda"])
    ya, hl = lru_fwd(proj0, *lru, gate_off=0, xr_off=1024, name="l0_lru")
    yb = attn_fwd(proj0, a["l0_attn_sinks"], q_off=2048, k_off=3072, v_off=3200, name="l0_attn")
    ycat = jnp.concatenate([ya, yb], axis=1)
    o0 = matmul(ycat, w["l0_w_out"], name="l0_out")
    h1 = rmsnorm_fwd(o0, a["l0_mix_post_norm"], res=h0, out_dtype=F32, name="l0_mix_post")
    h2, ffn0 = _ffn_fwd(h1, a, w, "l0_")

    u2 = rmsnorm_fwd(h2, a["l1_mix_pre_norm"], out_dtype=MXU, name="l1_mix_pre")
    proj1 = matmul(u2, w["l1_w_in"], name="l1_in")
    xbc = dwconv_fwd(proj1, a["l1_ssm_conv_w"], a["l1_ssm_conv_b"], mode="silu", x_off=D_SSM,
                     c_out=2 * D_SSM, cblk=512, out_dtype=F32, name="l1_ssm_conv")
    ssd = (a["l1_dt_bias"], a["l1_a_log"], a["l1_d_skip"], a["l1_gate_norm"])
    yn, st = ssd_fwd(xbc, proj1, *ssd, z_off=0, dt_off=3 * D_SSM, name="l1_ssd")
    o1 = matmul(yn, w["l1_w_out"], name="l1_out")
    h3 = rmsnorm_fwd(o1, a["l1_mix_post_norm"], res=h2, out_dtype=F32, name="l1_mix_post")
    h4, ffn1 = _ffn_fwd(h3, a, w, "l1_")

    loss, dh4 = loss_fwd_bwd(h4, a["loss_target"][0], name="loss")

    dh3 = _ffn_bwd(dh4, ffn1, a, w, "l1_", g)
    do1, g["l1_mix_post_norm"] = rmsnorm_bwd(o1, a["l1_mix_post_norm"], dh3, out_dtype=MXU,
                                             name="l1_mix_post_bwd")
    dyn = matmul(do1, w["l1_w_out"], trans_b=True, name="l1_out_dx")
    g["l1_w_out"] = matmul(yn.T, do1, name="l1_out_dw")
    (dxbc, dz, draw, g["l1_gate_norm"], g["l1_d_skip"], g["l1_dt_bias"], g["l1_a_log"]) = ssd_bwd(
        xbc, proj1, st, dyn, *ssd, z_off=0, dt_off=3 * D_SSM, name="l1_ssd_bwd")
    (dxin,), g["l1_ssm_conv_w"], g["l1_ssm_conv_b"] = dwconv_bwd(
        proj1, a["l1_ssm_conv_w"], a["l1_ssm_conv_b"], dxbc, mode="silu", x_off=D_SSM,
        c_out=2 * D_SSM, cblk=512, name="l1_ssm_conv_bwd")
    dproj1 = jnp.concatenate([dz, dxin, draw], axis=1)
    g["l1_w_in"] = matmul(u2.T, dproj1, name="l1_in_dw")[:, :MATS["l1_w_in"][1][1]]
    du2 = matmul(dproj1, w["l1_w_in"], trans_b=True, name="l1_in_dx")
    dh2, g["l1_mix_pre_norm"] = rmsnorm_bwd(h2, a["l1_mix_pre_norm"], du2, res=dh3, out_dtype=F32,
                                            name="l1_mix_pre_bwd")

    dh1 = _ffn_bwd(dh2, ffn0, a, w, "l0_", g)
    do0, g["l0_mix_post_norm"] = rmsnorm_bwd(o0, a["l0_mix_post_norm"], dh1, out_dtype=MXU,
                                             name="l0_mix_post_bwd")
    dy = matmul(do0, w["l0_w_out"], trans_b=True, name="l0_out_dx")
    g["l0_w_out"] = matmul(ycat.T, do0, name="l0_out_dw")
    (dgate, dxr, g["l0_lru_conv_w"], dcb, g["l0_lru_w_a"], dba, g["l0_lru_w_x"], dbx, dlam) = lru_bwd(
        proj0, hl, dy, *lru, gate_off=0, xr_off=1024, dy_off=0, name="l0_lru_bwd")
    g["l0_lru_conv_b"], g["l0_lru_b_a"], g["l0_lru_b_x"], g["l0_lru_lambda"] = dcb[0], dba[0], dbx[0], dlam[0]
    dq, dk, dv, g["l0_attn_sinks"] = attn_bwd(proj0, a["l0_attn_sinks"], dy, q_off=2048, k_off=3072,
                                              v_off=3200, dy_off=1024, name="l0_attn_bwd")
    dproj0 = jnp.concatenate([dgate, dxr, dq, dk, dv], axis=1)
    g["l0_w_in"] = matmul(u0.T, dproj0, name="l0_in_dw")
    du0 = matmul(dproj0, w["l0_w_in"], trans_b=True, name="l0_in_dx")
    dh0, g["l0_mix_pre_norm"] = rmsnorm_bwd(h0, a["l0_mix_pre_norm"], du0, res=dh1, out_dtype=F32,
                                            name="l0_mix_pre_bwd")
    g["meta_tokens"] = dh0[PAD:BLK]
    for n in REPLICATED:
        g[n] = g[n].reshape(a[n].shape)
    return loss[0, 0], dh0[BLK:].reshape(1, seq, D_MODEL), g


def kernel(*args):
    a = dict(zip(INPUTS, args))
    w = _gather_weights(a)
    w["l1_w_in"] = jnp.pad(w["l1_w_in"], ((0, 0), (0, L1_IN_PAD - w["l1_w_in"].shape[1])))
    loss_part, grad_x, g = _local_step({**a, **_gather_small(a)}, w)
    loss = lax.psum(loss_part, ("x", "y", "c"))

    sh_names = list(SHARDED)
    sh_shapes = [_shard_shape(n) for n in sh_names]
    rows = _round_up(sum(_rows_of(math.prod(s), PACK_W) for s in sh_shapes), 128)
    parts = jnp.concatenate([_split_by_dest(n, g[n]) for n in sh_names], axis=1)
    parts = jnp.pad(parts, ((0, 0), (0, rows - parts.shape[1]), (0, 0)))
    pair = pair_add(parts, pair_exchange(parts, name="rs_pair"), name="rs_pair_add")
    landed = chip_exchange(pair, name="rs_chip")
    sh_out = adamw(landed, *[_pack([a[p + n] for n in sh_names], PACK_W, rows) for p in ("", "m_", "v_")],
                   name="adamw_sharded")
    sh_out = [dict(zip(sh_names, _unpack(s, sh_shapes, PACK_W))) for s in sh_out]

    rp_shapes = [a[n].shape for n in REPLICATED]
    rrows = _round_up(sum(_rows_of(math.prod(s), SMALL_W) for s in rp_shapes), 128)
    gathered = all_gather(_pack([g[n] for n in REPLICATED], SMALL_W, rrows), name="gather_small_grads")
    rp_out = adamw(gathered, *[_pack([a[p + n] for n in REPLICATED], SMALL_W, rrows) for p in ("", "m_", "v_")],
                   name="adamw_replicated")
    rp_out = [dict(zip(REPLICATED, _unpack(s, rp_shapes, SMALL_W))) for s in rp_out]

    outs = [loss, grad_x]
    for k in range(4):
        outs += [sh_out[k][n] if n in SHARDED else rp_out[k][n] for n in WEIGHTS]
    return tuple(outs)
```

```python
import functools
import math

import jax
import jax.numpy as jnp
from jax import lax
from jax.experimental import pallas as pl
from jax.experimental.pallas import tpu as pltpu

F32 = jnp.float32
BF16 = jnp.bfloat16
MXU = jnp.bfloat16

D_MODEL = 1024
N_META = 16
BLK = 128
PAD = BLK - N_META
D_RNN = 1024
LRU_C = 8.0
N_Q_HEADS = 16
HEAD_DIM = 64
D_SSM = 2048
SSD_HEADS = 32
SSD_GROUPS = 8
D_FF = 2816
EPS = 1e-6
NEG = -1e30
N_DEV = 8

ADAM_LR = 0.001
ADAM_B1 = 0.9
ADAM_B2 = 0.999
ADAM_EPS = 1e-08
ADAM_WD = 0.01
ADAM_STEP = 10

VMEM_LIMIT = 56 * 1024 * 1024
MESH = pl.DeviceIdType.MESH
ANY = pl.BlockSpec(memory_space=pl.ANY)


def _cp(*sem):
    return pltpu.CompilerParams(dimension_semantics=sem, vmem_limit_bytes=VMEM_LIMIT)


def _pick(n, cands):
    for c in cands:
        if n % c == 0:
            return c
    return n


def _dot(a, b):
    return jnp.dot(a.astype(MXU), b.astype(MXU), preferred_element_type=F32)


def _dot_nt(a, b):
    return lax.dot_general(a.astype(MXU), b.astype(MXU), (((1,), (1,)), ((), ())),
                           preferred_element_type=F32)


def _dot_tn(a, b):
    return jnp.dot(a.T.astype(MXU), b.astype(MXU), preferred_element_type=F32)


def _dot_split(v, e):
    hi = v.astype(BF16)
    lo = (v - hi.astype(F32)).astype(BF16)
    return (jnp.dot(hi, e, preferred_element_type=F32)
            + jnp.dot(lo, e, preferred_element_type=F32))


def _sigmoid(x):
    return 1.0 / (1.0 + jnp.exp(-x))


def _log1p(x):
    u = 1.0 + x
    return jnp.where(u == 1.0, x, jnp.log(u) * (x / jnp.where(u == 1.0, 1.0, u - 1.0)))


def _expm1(x):
    u = jnp.exp(x)
    um1 = u - 1.0
    lg = jnp.log(jnp.where(u > 0.0, u, 1.0))
    safe = (um1 != 0.0) & (um1 != -1.0)
    return jnp.where(um1 == 0.0, x, jnp.where(um1 == -1.0, -1.0,
                                               um1 * (x / jnp.where(safe, lg, 1.0))))


def _softplus(x):
    return jnp.maximum(x, 0.0) + _log1p(jnp.exp(-jnp.abs(x)))


_GC = math.sqrt(2.0 / math.pi)


def _gelu(x):
    t = jnp.tanh(_GC * (x + 0.044715 * x * x * x))
    return 0.5 * x * (1.0 + t)


def _gelu_grad(x):
    t = jnp.tanh(_GC * (x + 0.044715 * x * x * x))
    return 0.5 * (1.0 + t) + 0.5 * x * (1.0 - t * t) * (_GC * (1.0 + 3.0 * 0.044715 * x * x))


def _silu(x):
    return x * _sigmoid(x)


def _silu_grad(x):
    s = _sigmoid(x)
    return s * (1.0 + x * (1.0 - s))


def _rows(shape):
    return lax.broadcasted_iota(jnp.int32, shape, 0)


def _lanes(shape):
    return lax.broadcasted_iota(jnp.int32, shape, 1)


def _shift_down(x, tail, d):
    if d == 0:
        return x
    n = x.shape[0]
    xr = pltpu.roll(x, d, 0)
    tr = pltpu.roll(tail, d, 0)
    first = jnp.where(_rows(tr.shape) < d, tr, xr[0:8])
    return jnp.concatenate([first, xr[8:n]], axis=0)


def _shift_up(x, head, d):
    if d == 0:
        return x
    n = x.shape[0]
    xr = pltpu.roll(x, n - d, 0)
    hr = pltpu.roll(head, 8 - d, 0)
    last = jnp.where(_rows(hr.shape) >= 8 - d, hr, xr[n - 8:n])
    return jnp.concatenate([xr[0:n - 8], last], axis=0)


def _row_at(x, i):
    return jnp.sum(jnp.where(_rows(x.shape) == i, x, 0.0), axis=0, keepdims=True)


def _scan_fwd(a, u):
    n = a.shape[0]
    ri = _rows(a.shape)
    d = 1
    while d < n:
        m = ri >= d
        us = jnp.where(m, pltpu.roll(u, d, 0), 0.0)
        as_ = jnp.where(m, pltpu.roll(a, d, 0), 1.0)
        u = u + a * us
        a = a * as_
        d *= 2
    return a, u


def _scan_rev(c, u):
    n = c.shape[0]
    ri = _rows(c.shape)
    d = 1
    while d < n:
        m = ri < n - d
        us = jnp.where(m, pltpu.roll(u, n - d, 0), 0.0)
        cs = jnp.where(m, pltpu.roll(c, n - d, 0), 1.0)
        u = u + c * us
        c = c * cs
        d *= 2
    return c, u


def _cumsum_fwd(x):
    n = x.shape[0]
    ri = _rows(x.shape)
    d = 1
    while d < n:
        x = x + jnp.where(ri >= d, pltpu.roll(x, d, 0), 0.0)
        d *= 2
    return x


def _cumsum_rev(x):
    n = x.shape[0]
    ri = _rows(x.shape)
    d = 1
    while d < n:
        x = x + jnp.where(ri < n - d, pltpu.roll(x, n - d, 0), 0.0)
        d *= 2
    return x


def matmul(a, b, *, trans_b=False, out_dtype=F32, name):
    m, k = a.shape
    n = b.shape[0] if trans_b else b.shape[1]
    tm = _pick(m, (1664, 1408, 1040, 1024, 832, 640, 512, 384, 256, 128))
    tn = _pick(n, (512, 896, 640, 384, 256, 128))
    tk = k if k <= 2048 else _pick(k, (1664, 1408, 1024, 896, 512, 256, 128))
    nk = k // tk

    def product(a_ref, b_ref):
        return _dot_nt(a_ref[...], b_ref[...]) if trans_b else _dot(a_ref[...], b_ref[...])

    def kern_once(a_ref, b_ref, o_ref):
        o_ref[...] = product(a_ref, b_ref).astype(o_ref.dtype)

    def kern_acc(a_ref, b_ref, o_ref, acc_ref):
        kk = pl.program_id(2)

        @pl.when(kk == 0)
        def _():
            acc_ref[...] = product(a_ref, b_ref)

        @pl.when(kk > 0)
        def _():
            acc_ref[...] += product(a_ref, b_ref)

        @pl.when(kk == nk - 1)
        def _():
            o_ref[...] = acc_ref[...].astype(o_ref.dtype)

    b_spec = (pl.BlockSpec((tn, tk), lambda i, j, kk: (j, kk)) if trans_b
              else pl.BlockSpec((tk, tn), lambda i, j, kk: (kk, j)))
    return pl.pallas_call(
        kern_once if nk == 1 else kern_acc, name=name,
        grid=(m // tm, n // tn, nk),
        in_specs=[pl.BlockSpec((tm, tk), lambda i, j, kk: (i, kk)), b_spec],
        out_specs=pl.BlockSpec((tm, tn), lambda i, j, kk: (i, j)),
        out_shape=jax.ShapeDtypeStruct((m, n), out_dtype),
        scratch_shapes=[] if nk == 1 else [pltpu.VMEM((tm, tn), F32)],
        compiler_params=_cp("parallel", "parallel", "arbitrary"),
    )(a, b)


def matmul_cat(a_list, b, *, trans_b=False, out_dtype=F32, name):
    m = a_list[0].shape[0]
    ks = [x.shape[1] for x in a_list]
    ktot = sum(ks)
    n = b.shape[0] if trans_b else b.shape[1]
    tn = _pick(n, (512, 256, 128))
    tm = next((c for c in (1664, 1040, 832, 640, 512, 384, 256, 128)
               if m % c == 0 and c * ktot * 2 <= 8 * 1024 * 1024), m)
    na = len(a_list)

    def kern(*refs):
        b_ref, o_ref = refs[na], refs[na + 1]
        acc, off = None, 0
        for a_ref, k in zip(refs[:na], ks):
            if trans_b:
                part = _dot_nt(a_ref[...], b_ref[:, off:off + k])
            else:
                part = _dot(a_ref[...], b_ref[off:off + k, :])
            acc = part if acc is None else acc + part
            off += k
        o_ref[...] = acc.astype(o_ref.dtype)

    b_spec = (pl.BlockSpec((tn, ktot), lambda i, j: (j, 0)) if trans_b
              else pl.BlockSpec((ktot, tn), lambda i, j: (0, j)))
    return pl.pallas_call(
        kern, name=name, grid=(m // tm, n // tn),
        in_specs=[pl.BlockSpec((tm, k), lambda i, j: (i, 0)) for k in ks] + [b_spec],
        out_specs=pl.BlockSpec((tm, tn), lambda i, j: (i, j)),
        out_shape=jax.ShapeDtypeStruct((m, n), out_dtype),
        compiler_params=_cp("parallel", "parallel"),
    )(*a_list, b)


def _row_tile(t):
    return _pick(t, (832, 640, 512, 384, 256, 128))


def rmsnorm_fwd(x, w, res=None, *, out_dtype, name):
    t, d = x.shape
    tr = _row_tile(t)

    def kern(*refs):
        if res is None:
            x_ref, w_ref, o_ref = refs
        else:
            x_ref, w_ref, r_ref, o_ref = refs
        xv = x_ref[...]
        r = lax.rsqrt(jnp.mean(xv * xv, axis=-1, keepdims=True) + EPS)
        y = xv * r * w_ref[...]
        if res is not None:
            y = r_ref[...] + y
        o_ref[...] = y.astype(o_ref.dtype)

    row = pl.BlockSpec((tr, d), lambda i: (i, 0))
    vec = pl.BlockSpec((1, d), lambda i: (0, 0))
    ins = [x, w.reshape(1, d)] + ([] if res is None else [res])
    specs = [row, vec] + ([] if res is None else [row])
    return pl.pallas_call(
        kern, name=name, grid=(t // tr,), in_specs=specs, out_specs=row,
        out_shape=jax.ShapeDtypeStruct((t, d), out_dtype),
        compiler_params=_cp("parallel"),
    )(*ins)


def rmsnorm_bwd(x, w, dy, res=None, *, out_dtype, name):
    t, d = x.shape
    tr = _row_tile(t)

    def kern(*refs):
        if res is None:
            x_ref, w_ref, dy_ref, dx_ref, dw_ref = refs
        else:
            x_ref, w_ref, dy_ref, r_ref, dx_ref, dw_ref = refs
        i = pl.program_id(0)
        xv = x_ref[...]
        dyv = dy_ref[...].astype(F32)
        r = lax.rsqrt(jnp.mean(xv * xv, axis=-1, keepdims=True) + EPS)
        xh = xv * r
        g = dyv * w_ref[...]
        dx = r * (g - xh * jnp.mean(g * xh, axis=-1, keepdims=True))
        if res is not None:
            dx = r_ref[...] + dx
        dx_ref[...] = dx.astype(dx_ref.dtype)
        part = jnp.sum(dyv * xh, axis=0, keepdims=True)

        @pl.when(i == 0)
        def _():
            dw_ref[...] = part

        @pl.when(i > 0)
        def _():
            dw_ref[...] += part

    row = pl.BlockSpec((tr, d), lambda i: (i, 0))
    vec = pl.BlockSpec((1, d), lambda i: (0, 0))
    ins = [x, w.reshape(1, d), dy] + ([] if res is None else [res])
    specs = [row, vec, row] + ([] if res is None else [row])
    return pl.pallas_call(
        kern, name=name, grid=(t // tr,), in_specs=specs, out_specs=[row, vec],
        out_shape=[jax.ShapeDtypeStruct((t, d), out_dtype), jax.ShapeDtypeStruct((1, d), F32)],
        compiler_params=_cp("arbitrary"),
    )(*ins)


def _conv_tile(t):
    return _pick(t, (640, 384, 256, 128))


def _conv_apply(x, tail, cw, cb, ksz):
    y = cb
    for k in range(ksz):
        y = y + cw[k:k + 1, :] * _shift_down(x, tail, ksz - 1 - k)
    return y


def dwconv_fwd(x, cw, cb, *, mode, x_off, c_out, cblk, out_dtype, name):
    t = x.shape[0]
    ksz = cw.shape[0]
    tb = _conv_tile(t)
    nb, ncb, t8 = t // tb, c_out // cblk, tb // 8
    xo = x_off // cblk
    nin = 2 if mode == "geglu" else 1

    def kern(*refs):
        o_ref = refs[-1]
        n = pl.program_id(1)
        valid = (n * tb + _rows((tb, cblk))) >= PAD
        hs = []
        for q in range(nin):
            x_ref, t_ref, w_ref, b_ref = refs[4 * q:4 * q + 4]
            tail = jnp.where(n > 0, t_ref[...], 0.0)
            hs.append(_conv_apply(x_ref[...], tail, w_ref[...], b_ref[...], ksz))
        if mode == "geglu":
            y = _gelu(hs[0]) * hs[1]
        else:
            y = _silu(hs[0])
        o_ref[...] = jnp.where(valid, y, 0.0).astype(o_ref.dtype)

    ins, specs = [], []
    for q in range(nin):
        co = xo + q * ncb
        wo = q * ncb
        ins += [x, x, cw, cb.reshape(1, -1)]
        specs += [
            pl.BlockSpec((tb, cblk), lambda j, n, co=co: (n, co + j)),
            pl.BlockSpec((8, cblk), lambda j, n, co=co: (jnp.maximum(n * t8 - 1, 0), co + j)),
            pl.BlockSpec((ksz, cblk), lambda j, n, wo=wo: (0, wo + j)),
            pl.BlockSpec((1, cblk), lambda j, n, wo=wo: (0, wo + j)),
        ]
    return pl.pallas_call(
        kern, name=name, grid=(ncb, nb), in_specs=specs,
        out_specs=pl.BlockSpec((tb, cblk), lambda j, n: (n, j)),
        out_shape=jax.ShapeDtypeStruct((t, c_out), out_dtype),
        compiler_params=_cp("parallel", "parallel"),
    )(*ins)


def dwconv_bwd(x, cw, cb, dy, *, mode, x_off, c_out, cblk, name):
    t = x.shape[0]
    ksz = cw.shape[0]
    tb = _conv_tile(t)
    nb, ncb, t8 = t // tb, c_out // cblk, tb // 8
    xo = x_off // cblk
    nin = 2 if mode == "geglu" else 1
    ctot = nin * c_out

    def kern(*refs):
        dy_ref = refs[4 * nin]
        outs = refs[4 * nin + 1:4 * nin + 1 + 3 * nin]
        heads = refs[4 * nin + 1 + 3 * nin:]
        n = pl.program_id(1)
        blk = nb - 1 - n
        valid = (blk * tb + _rows((tb, cblk))) >= PAD

        @pl.when(n == 0)
        def _():
            for q in range(nin):
                heads[q][...] = jnp.zeros_like(heads[q])
                outs[3 * q + 1][...] = jnp.zeros_like(outs[3 * q + 1])
                outs[3 * q + 2][...] = jnp.zeros_like(outs[3 * q + 2])

        xs, tails, hs = [], [], []
        for q in range(nin):
            x_ref, t_ref, w_ref, b_ref = refs[4 * q:4 * q + 4]
            tail = jnp.where(blk > 0, t_ref[...], 0.0)
            xs.append(x_ref[...])
            tails.append(tail)
            hs.append(_conv_apply(x_ref[...], tail, w_ref[...], b_ref[...], ksz))
        dyv = dy_ref[...].astype(F32)
        if mode == "geglu":
            dhs = [dyv * hs[1] * _gelu_grad(hs[0]), dyv * _gelu(hs[0])]
        else:
            dhs = [dyv * _silu_grad(hs[0])]
        for q in range(nin):
            w_ref = refs[4 * q + 2]
            dx_ref, dw_ref, db_ref = outs[3 * q:3 * q + 3]
            dh = jnp.where(valid, dhs[q], 0.0)
            head = heads[q][...]
            dx = jnp.zeros_like(dh)
            dws = []
            for k in range(ksz):
                sh = ksz - 1 - k
                dx = dx + w_ref[k:k + 1, :] * _shift_up(dh, head, sh)
                dws.append(jnp.sum(dh * _shift_down(xs[q], tails[q], sh), axis=0, keepdims=True))
            dx_ref[...] = jnp.where(valid, dx, 0.0).astype(dx_ref.dtype)
            dw_ref[...] += jnp.concatenate(dws, axis=0)
            db_ref[...] += jnp.sum(dh, axis=0, keepdims=True)
            heads[q][...] = dh[0:8]

    ins, specs, out_specs, out_shape, scratch = [], [], [], [], []
    for q in range(nin):
        co = xo + q * ncb
        wo = q * ncb
        ins += [x, x, cw, cb.reshape(1, -1)]
        specs += [
            pl.BlockSpec((tb, cblk), lambda j, n, co=co: (nb - 1 - n, co + j)),
            pl.BlockSpec((8, cblk), lambda j, n, co=co: (jnp.maximum((nb - 1 - n) * t8 - 1, 0), co + j)),
            pl.BlockSpec((ksz, cblk), lambda j, n, wo=wo: (0, wo + j)),
            pl.BlockSpec((1, cblk), lambda j, n, wo=wo: (0, wo + j)),
        ]
        out_specs += [
            pl.BlockSpec((tb, cblk), lambda j, n: (nb - 1 - n, j)),
            pl.BlockSpec((ksz, cblk), lambda j, n: (0, j)),
            pl.BlockSpec((1, cblk), lambda j, n: (0, j)),
        ]
        out_shape += [jax.ShapeDtypeStruct((t, c_out), MXU),
                      jax.ShapeDtypeStruct((ksz, c_out), F32),
                      jax.ShapeDtypeStruct((1, c_out), F32)]
        scratch.append(pltpu.VMEM((8, cblk), F32))
    ins.append(dy)
    specs.append(pl.BlockSpec((tb, cblk), lambda j, n: (nb - 1 - n, j)))
    res = pl.pallas_call(
        kern, name=name, grid=(ncb, nb), in_specs=specs, out_specs=out_specs,
        out_shape=out_shape, scratch_shapes=scratch,
        compiler_params=_cp("parallel", "arbitrary"),
    )(*ins)
    dxs = [res[3 * q] for q in range(nin)]
    dcw = jnp.concatenate([res[3 * q + 1] for q in range(nin)], axis=1)
    dcb = jnp.concatenate([res[3 * q + 2] for q in range(nin)], axis=1)
    return dxs, dcw, dcb.reshape(ctot)


def _lru_tile(t):
    return _pick(t, (640, 384, 256, 128))


def _lru_gates(xc, wa, ba, wx, bx, sp):
    r = _sigmoid(_dot(xc, wa) + ba)
    i = _sigmoid(_dot(xc, wx) + bx)
    log_a = -LRU_C * r * sp
    a = jnp.exp(log_a)
    mult = jnp.sqrt(-_expm1(2.0 * log_a))
    return r, i, a, mult


def lru_fwd(proj, cw, cb, wa, ba, wx, bx, lam, *, gate_off, xr_off, name):
    t = proj.shape[0]
    tb = _lru_tile(t)
    nb, ns, t8 = t // tb, tb // BLK, tb // 8
    go, xo = gate_off // BLK, xr_off // BLK

    def kern(g_ref, x_ref, xt_ref, cw_ref, cb_ref, wa_ref, ba_ref, wx_ref, bx_ref, lam_ref,
             y_ref, h_ref, hc_ref):
        n = pl.program_id(1)

        @pl.when(n == 0)
        def _():
            hc_ref[...] = jnp.zeros_like(hc_ref)

        sp = _softplus(-lam_ref[...])
        hprev = hc_ref[0:1, :]
        for s in range(ns):
            sl = slice(s * BLK, (s + 1) * BLK)
            xv = x_ref[sl, :]
            tail = jnp.where(n > 0, xt_ref[...], 0.0) if s == 0 else x_ref[s * BLK - 8:s * BLK, :]
            valid = (n * tb + s * BLK + _rows((BLK, BLK))) >= PAD
            xc = jnp.where(valid, _conv_apply(xv, tail, cw_ref[...], cb_ref[...], 4), 0.0)
            _, i, a, mult = _lru_gates(xc, wa_ref[0], ba_ref[...], wx_ref[0], bx_ref[...], sp)
            u = mult * (i * xc)
            ca, cu = _scan_fwd(a, u)
            h = cu + ca * hprev
            hprev = _row_at(h, BLK - 1)
            h_ref[sl, :] = h
            y_ref[sl, :] = (_gelu(g_ref[sl, :]) * h).astype(y_ref.dtype)
        hc_ref[...] = jnp.broadcast_to(hprev, hc_ref.shape)

    vec = pl.BlockSpec((1, BLK), lambda j, n: (0, j))
    mat = pl.BlockSpec((1, BLK, BLK), lambda j, n: (j, 0, 0))
    return pl.pallas_call(
        kern, name=name, grid=(D_RNN // BLK, nb),
        in_specs=[
            pl.BlockSpec((tb, BLK), lambda j, n: (n, go + j)),
            pl.BlockSpec((tb, BLK), lambda j, n: (n, xo + j)),
            pl.BlockSpec((8, BLK), lambda j, n: (jnp.maximum(n * t8 - 1, 0), xo + j)),
            pl.BlockSpec((4, BLK), lambda j, n: (0, j)), vec, mat, vec, mat, vec, vec,
        ],
        out_specs=[pl.BlockSpec((tb, BLK), lambda j, n: (n, j)),
                   pl.BlockSpec((tb, BLK), lambda j, n: (n, j))],
        out_shape=[jax.ShapeDtypeStruct((t, D_RNN), MXU), jax.ShapeDtypeStruct((t, D_RNN), F32)],
        scratch_shapes=[pltpu.VMEM((8, BLK), F32)],
        compiler_params=_cp("parallel", "arbitrary"),
    )(proj, proj, proj, cw, cb.reshape(1, -1), wa, ba.reshape(1, -1), wx, bx.reshape(1, -1),
      lam.reshape(1, -1))


def lru_bwd(proj, h, dy, cw, cb, wa, ba, wx, bx, lam, *, gate_off, xr_off, dy_off, name):
    t = proj.shape[0]
    tb = _lru_tile(t)
    nb, ns, t8 = t // tb, tb // BLK, tb // 8
    go, xo, do = gate_off // BLK, xr_off // BLK, dy_off // BLK

    def kern(g_ref, x_ref, xt_ref, h_ref, ht_ref, dy_ref, cw_ref, cb_ref, wa_ref, ba_ref,
             wx_ref, bx_ref, lam_ref,
             dg_ref, dx_ref, dcw_ref, dcb_ref, dwa_ref, dba_ref, dwx_ref, dbx_ref, dlam_ref,
             gin_ref, head_ref):
        n = pl.program_id(1)
        blk = nb - 1 - n

        @pl.when(n == 0)
        def _():
            gin_ref[...] = jnp.zeros_like(gin_ref)
            head_ref[...] = jnp.zeros_like(head_ref)
            for r_ in (dcw_ref, dcb_ref, dwa_ref, dba_ref, dwx_ref, dbx_ref, dlam_ref):
                r_[...] = jnp.zeros_like(r_)

        lamv = lam_ref[...]
        sp = _softplus(-lamv)
        dsp_dlam = -_sigmoid(-lamv)
        g_in = gin_ref[0:1, :]
        head = head_ref[...]
        ones8 = jnp.ones((8, BLK), F32)
        for s in reversed(range(ns)):
            sl = slice(s * BLK, (s + 1) * BLK)
            xv = x_ref[sl, :]
            if s == 0:
                tail = jnp.where(blk > 0, xt_ref[...], 0.0)
                htail = jnp.where(blk > 0, ht_ref[...], 0.0)
            else:
                tail = x_ref[s * BLK - 8:s * BLK, :]
                htail = h_ref[s * BLK - 8:s * BLK, :]
            valid = (blk * tb + s * BLK + _rows((BLK, BLK))) >= PAD
            xc = jnp.where(valid, _conv_apply(xv, tail, cw_ref[...], cb_ref[...], 4), 0.0)
            wav, wxv = wa_ref[0], wx_ref[0]
            r, i, a, mult = _lru_gates(xc, wav, ba_ref[...], wxv, bx_ref[...], sp)
            hv = h_ref[sl, :]
            hprev = _shift_down(hv, htail, 1)
            gv = g_ref[sl, :]
            dyv = dy_ref[sl, :].astype(F32)
            dh = dyv * _gelu(gv)
            dg_ref[sl, :] = (dyv * hv * _gelu_grad(gv)).astype(dg_ref.dtype)
            c = _shift_up(a, ones8, 1)
            cc, cu = _scan_rev(c, dh)
            gg = cu + cc * g_in
            g_in = _row_at(a * gg, 0)
            da = gg * hprev
            di = gg * mult * xc
            dxc = gg * mult * i
            dmult = gg * i * xc
            dlog_a = da * a - dmult * (a * a) / mult
            dr = dlog_a * (-LRU_C * sp)
            dlam_ref[...] += jnp.sum(dlog_a * (-LRU_C) * r, axis=0, keepdims=True) * dsp_dlam
            dpr = dr * r * (1.0 - r)
            dpi = di * i * (1.0 - i)
            dxc = dxc + _dot_nt(dpr, wav) + _dot_nt(dpi, wxv)
            dxc = jnp.where(valid, dxc, 0.0)
            dpr = jnp.where(valid, dpr, 0.0)
            dpi = jnp.where(valid, dpi, 0.0)
            dwa_ref[0] += _dot_tn(xc, dpr)
            dwx_ref[0] += _dot_tn(xc, dpi)
            dba_ref[...] += jnp.sum(dpr, axis=0, keepdims=True)
            dbx_ref[...] += jnp.sum(dpi, axis=0, keepdims=True)
            dx = jnp.zeros_like(dxc)
            dws = []
            for k in range(4):
                dx = dx + cw_ref[k:k + 1, :] * _shift_up(dxc, head, 3 - k)
                dws.append(jnp.sum(dxc * _shift_down(xv, tail, 3 - k), axis=0, keepdims=True))
            dx_ref[sl, :] = jnp.where(valid, dx, 0.0).astype(dx_ref.dtype)
            dcw_ref[...] += jnp.concatenate(dws, axis=0)
            dcb_ref[...] += jnp.sum(dxc, axis=0, keepdims=True)
            head = dxc[0:8]
        gin_ref[...] = jnp.broadcast_to(g_in, gin_ref.shape)
        head_ref[...] = head

    vec = pl.BlockSpec((1, BLK), lambda j, n: (0, j))
    mat = pl.BlockSpec((1, BLK, BLK), lambda j, n: (j, 0, 0))
    cws = pl.BlockSpec((4, BLK), lambda j, n: (0, j))

    def rb(off):
        return pl.BlockSpec((tb, BLK), lambda j, n: (nb - 1 - n, off + j))

    def tl(off):
        return pl.BlockSpec((8, BLK), lambda j, n: (jnp.maximum((nb - 1 - n) * t8 - 1, 0), off + j))

    return pl.pallas_call(
        kern, name=name, grid=(D_RNN // BLK, nb),
        in_specs=[rb(go), rb(xo), tl(xo), rb(0), tl(0), rb(do), cws, vec, mat, vec, mat, vec, vec],
        out_specs=[rb(0), rb(0), cws, vec, mat, vec, mat, vec, vec],
        out_shape=[jax.ShapeDtypeStruct((t, D_RNN), MXU), jax.ShapeDtypeStruct((t, D_RNN), MXU),
                   jax.ShapeDtypeStruct((4, D_RNN), F32), jax.ShapeDtypeStruct((1, D_RNN), F32),
                   jax.ShapeDtypeStruct((8, BLK, BLK), F32), jax.ShapeDtypeStruct((1, D_RNN), F32),
                   jax.ShapeDtypeStruct((8, BLK, BLK), F32), jax.ShapeDtypeStruct((1, D_RNN), F32),
                   jax.ShapeDtypeStruct((1, D_RNN), F32)],
        scratch_shapes=[pltpu.VMEM((8, BLK), F32), pltpu.VMEM((8, BLK), F32)],
        compiler_params=_cp("parallel", "arbitrary"),
    )(proj, proj, proj, h, h, dy, cw, cb.reshape(1, -1), wa, ba.reshape(1, -1), wx,
      bx.reshape(1, -1), lam.reshape(1, -1))


_SCALE = HEAD_DIM ** -0.5


STK = 4


def _attn_masks(n):
    qi = _rows((STK * BLK, 3 * BLK)) & (BLK - 1)
    c = _lanes((STK * BLK, 3 * BLK))
    tq = n * BLK + qi - PAD
    s_band = (n - 1) * BLK + c - PAD
    d_band = tq - s_band
    ok_band = (s_band >= N_META) & (d_band >= 0) & (d_band < BLK)
    jm = c - 2 * BLK
    d_meta = tq - (jm - PAD)
    ok_meta = (jm >= PAD) & (d_meta >= 0)
    is_band = c < 2 * BLK
    ok = (is_band & ok_band) | (jnp.logical_not(is_band) & ok_meta)
    dist = jnp.where(is_band, d_band, jnp.minimum(d_meta, BLK)).astype(F32)
    return ok, dist


def _stack_heads(g, e):
    return [8 * g + 2 * i + e for i in range(STK)]


def _stack_cols(heads, sk):
    slope = jnp.concatenate(
        [jnp.full((BLK, 1), 2.0 ** (-8.0 * (h + 1) / N_Q_HEADS), F32) for h in heads], axis=0)
    sink = jnp.concatenate(
        [jnp.broadcast_to(jnp.sum(jnp.where(_lanes(sk.shape) == h, sk, 0.0), axis=1, keepdims=True),
                          (BLK, 1)) for h in heads], axis=0)
    return slope, sink


def _stack_tiles(ref, g, sel):
    return jnp.concatenate(
        [jnp.where(sel, ref[:, (4 * g + i) * BLK:(4 * g + i + 1) * BLK].astype(F32), 0.0)
         for i in range(STK)], axis=0)


def _attn_probs(qm, kk, ok, dist, slope, sink):
    s = _dot_nt(qm, kk) * _SCALE - slope * dist
    s = jnp.where(ok, s, NEG)
    mx = jnp.maximum(jnp.max(s, axis=-1, keepdims=True), sink)
    p = jnp.exp(s - mx)
    es = jnp.exp(sink - mx)
    inv = 1.0 / (jnp.sum(p, axis=-1, keepdims=True) + es)
    return p * inv, es * inv


def _attn_specs(t, q_off, k_off, v_off, rev):
    nb = t // BLK
    qo, ko, vo = q_off // 1024, k_off // BLK, v_off // BLK

    def b(n):
        return nb - 1 - n if rev else n

    return [
        pl.BlockSpec((BLK, 1024), lambda n: (b(n), qo)),
        pl.BlockSpec((BLK, BLK), lambda n: (b(n), ko)),
        pl.BlockSpec((BLK, BLK), lambda n: (b(n), vo)),
        pl.BlockSpec((BLK, BLK), lambda n: (jnp.maximum(b(n) - 1, 0), ko)),
        pl.BlockSpec((BLK, BLK), lambda n: (jnp.maximum(b(n) - 1, 0), vo)),
        pl.BlockSpec((BLK, BLK), lambda n: (0, ko)),
        pl.BlockSpec((BLK, BLK), lambda n: (0, vo)),
        pl.BlockSpec((1, BLK), lambda n: (0, 0)),
    ]


def attn_fwd(proj, sinks, *, q_off, k_off, v_off, name):
    t = proj.shape[0]
    nb = t // BLK

    def kern(q_ref, kc_ref, vc_ref, kp_ref, vp_ref, km_ref, vm_ref, sk_ref, o_ref):
        n = pl.program_id(0)
        ok, dist = _attn_masks(n)
        k_all = jnp.concatenate([kp_ref[...], kc_ref[...], km_ref[...]], axis=0)
        v_all = jnp.concatenate([vp_ref[...], vc_ref[...], vm_ref[...]], axis=0)
        k_alt = pltpu.roll(k_all, HEAD_DIM, 1)
        v_alt = pltpu.roll(v_all, HEAD_DIM, 1)
        low = _lanes((BLK, BLK)) < HEAD_DIM
        outs = {}
        for g in range(2):
            for e in range(2):
                qm = _stack_tiles(q_ref, g, low == (e == 0))
                kk = k_all if g == e else k_alt
                vv = v_all if g == e else v_alt
                slope, sink = _stack_cols(_stack_heads(g, e), sk_ref[...])
                p, _ = _attn_probs(qm, kk, ok, dist, slope, sink)
                outs[g, e] = _dot(p, vv)
        for hp in range(N_Q_HEADS // 2):
            g, rs = hp // STK, slice((hp % STK) * BLK, (hp % STK + 1) * BLK)
            o_ref[:, hp * BLK:(hp + 1) * BLK] = jnp.where(low, outs[g, 0][rs], outs[g, 1][rs]).astype(o_ref.dtype)

    sk = jnp.zeros((1, BLK), F32).at[0, :N_Q_HEADS].set(sinks)
    return pl.pallas_call(
        kern, name=name, grid=(nb,),
        in_specs=_attn_specs(t, q_off, k_off, v_off, False),
        out_specs=pl.BlockSpec((BLK, 1024), lambda n: (n, 0)),
        out_shape=jax.ShapeDtypeStruct((t, 1024), MXU),
        compiler_params=_cp("parallel"),
    )(proj, proj, proj, proj, proj, proj, proj, sk)


def attn_bwd(proj, sinks, dy, *, q_off, k_off, v_off, dy_off, name):
    t = proj.shape[0]
    nb = t // BLK
    do = dy_off // 1024

    def kern(q_ref, kc_ref, vc_ref, kp_ref, vp_ref, km_ref, vm_ref, sk_ref, do_ref,
             dq_ref, dk_ref, dv_ref, dsk_ref, ck_ref, cv_ref, mk_ref, mv_ref):
        n = pl.program_id(0)
        blk = nb - 1 - n

        @pl.when(n == 0)
        def _():
            for r_ in (ck_ref, cv_ref, mk_ref, mv_ref, dsk_ref):
                r_[...] = jnp.zeros_like(r_)

        ok, dist = _attn_masks(blk)
        k_all = jnp.concatenate([kp_ref[...], kc_ref[...], km_ref[...]], axis=0)
        v_all = jnp.concatenate([vp_ref[...], vc_ref[...], vm_ref[...]], axis=0)
        k_alt = pltpu.roll(k_all, HEAD_DIM, 1)
        v_alt = pltpu.roll(v_all, HEAD_DIM, 1)
        low = _lanes((BLK, BLK)) < HEAD_DIM
        lane1 = _lanes((1, BLK))
        dk_all = jnp.zeros((3 * BLK, BLK), F32)
        dv_all = jnp.zeros((3 * BLK, BLK), F32)
        dsk = jnp.zeros((1, BLK), F32)
        dqs = {}
        for g in range(2):
            for e in range(2):
                sel = low == (e == 0)
                heads = _stack_heads(g, e)
                qm = _stack_tiles(q_ref, g, sel)
                dom = _stack_tiles(do_ref, g, sel)
                kk = k_all if g == e else k_alt
                vv = v_all if g == e else v_alt
                slope, sink = _stack_cols(heads, sk_ref[...])
                p, psink = _attn_probs(qm, kk, ok, dist, slope, sink)
                dp = _dot_nt(dom, vv)
                delta = jnp.sum(p * dp, axis=-1, keepdims=True)
                ds = p * (dp - delta) * _SCALE
                psd = psink * delta
                for i, h in enumerate(heads):
                    dsk = dsk + jnp.where(lane1 == h, -jnp.sum(psd[i * BLK:(i + 1) * BLK], axis=0, keepdims=True), 0.0)
                dqs[g, e] = _dot(ds, kk)
                dkh = _dot_tn(ds, qm)
                dvh = _dot_tn(p, dom)
                if g != e:
                    dkh = pltpu.roll(dkh, HEAD_DIM, 1)
                    dvh = pltpu.roll(dvh, HEAD_DIM, 1)
                dk_all = dk_all + dkh
                dv_all = dv_all + dvh
        for hp in range(N_Q_HEADS // 2):
            g, rs = hp // STK, slice((hp % STK) * BLK, (hp % STK + 1) * BLK)
            dq_ref[:, hp * BLK:(hp + 1) * BLK] = jnp.where(low, dqs[g, 0][rs], dqs[g, 1][rs]).astype(dq_ref.dtype)
        dsk_ref[...] += dsk
        mk_ref[...] += dk_all[2 * BLK:3 * BLK]
        mv_ref[...] += dv_all[2 * BLK:3 * BLK]
        is0 = blk == 0
        dk_ref[...] = (dk_all[BLK:2 * BLK] + ck_ref[...] + jnp.where(is0, mk_ref[...], 0.0)).astype(dk_ref.dtype)
        dv_ref[...] = (dv_all[BLK:2 * BLK] + cv_ref[...] + jnp.where(is0, mv_ref[...], 0.0)).astype(dv_ref.dtype)
        ck_ref[...] = dk_all[0:BLK]
        cv_ref[...] = dv_all[0:BLK]

    sk = jnp.zeros((1, BLK), F32).at[0, :N_Q_HEADS].set(sinks)
    kv = pl.BlockSpec((BLK, BLK), lambda n: (nb - 1 - n, 0))
    res = pl.pallas_call(
        kern, name=name, grid=(nb,),
        in_specs=_attn_specs(t, q_off, k_off, v_off, True)
        + [pl.BlockSpec((BLK, 1024), lambda n: (nb - 1 - n, do))],
        out_specs=[pl.BlockSpec((BLK, 1024), lambda n: (nb - 1 - n, 0)), kv, kv,
                   pl.BlockSpec((1, BLK), lambda n: (0, 0))],
        out_shape=[jax.ShapeDtypeStruct((t, 1024), MXU), jax.ShapeDtypeStruct((t, BLK), MXU),
                   jax.ShapeDtypeStruct((t, BLK), MXU), jax.ShapeDtypeStruct((1, BLK), F32)],
        scratch_shapes=[pltpu.VMEM((BLK, BLK), F32)] * 4,
        compiler_params=_cp("arbitrary"),
    )(proj, proj, proj, proj, proj, proj, proj, sk, dy)
    return res[0], res[1], res[2], res[3][0, :N_Q_HEADS]


GW = D_SSM // SSD_GROUPS
EXP_ROWS = 3 * BLK + 8
RED_ROWS = EXP_ROWS + 8


def _head_expand():
    ch = jnp.arange(D_SSM) // HEAD_DIM
    return (jnp.arange(BLK)[:, None] == ch[None, :]).astype(BF16)


def _ssd_decay(raw, dtb, alog, rowv):
    valid = rowv & (_lanes((BLK, BLK)) < SSD_HEADS)
    pre = raw + dtb
    dtp = jnp.where(valid, _softplus(pre), 0.0)
    av = -jnp.exp(alog)
    cs = _cumsum_fwd(dtp * av)
    cs_last = _row_at(cs, BLK - 1)
    return valid, pre, dtp, av, cs, jnp.exp(cs), jnp.exp(cs_last - cs), jnp.exp(cs_last)


def _head_col(x, h):
    return jnp.sum(jnp.where(_lanes(x.shape) == h, x, 0.0), axis=1, keepdims=True)


def _ssd_group_fwd(g, xdt, cs, cst, bg, cg, tril, low):
    cb = _dot_nt(cg, bg)
    ys, lm = [], []
    for j in range(2):
        xp = xdt[:, g * GW + j * BLK:g * GW + (j + 1) * BLK]
        hv = []
        for e in range(2):
            h = 4 * g + 2 * j + e
            seg = _head_col(cs, h) - _row_at(cst, h)
            lmat = jnp.where(tril, jnp.exp(jnp.minimum(seg, 0.0)), 0.0)
            mmat = cb * lmat
            lm.append((lmat, mmat))
            hv.append(_dot(mmat, xp))
        ys.append(jnp.where(low, hv[0], hv[1]))
    return jnp.concatenate(ys, axis=1), lm


def _ssd_specs(t, z_off, dt_off, rev):
    nb = t // BLK
    zo, dto = z_off // D_SSM, dt_off // BLK

    def b(n):
        return nb - 1 - n if rev else n

    vec = lambda w: pl.BlockSpec((1, w), lambda n: (0, 0))
    return [
        pl.BlockSpec((BLK, D_SSM), lambda n: (b(n), 0)),
        pl.BlockSpec((BLK, 1024), lambda n: (b(n), 2)),
        pl.BlockSpec((BLK, 1024), lambda n: (b(n), 3)),
        pl.BlockSpec((BLK, D_SSM), lambda n: (b(n), zo)),
        pl.BlockSpec((BLK, BLK), lambda n: (b(n), dto)),
        vec(BLK), vec(BLK), vec(D_SSM), vec(D_SSM),
        pl.BlockSpec((BLK, D_SSM), lambda n: (0, 0)),
    ]


def _pad128(v):
    return jnp.zeros((1, BLK), F32).at[0, :v.shape[0]].set(v)


def ssd_fwd(xbc, proj, dt_bias, a_log, d_skip, gate_norm, *, z_off, dt_off, name):
    t = xbc.shape[0]
    nb = t // BLK

    def kern(x_ref, b_ref, c_ref, z_ref, dt_ref, dtb_ref, alog_ref, dsk_ref, gn_ref, e_ref,
             yn_ref, st_ref, p_ref):
        n = pl.program_id(0)

        @pl.when(n == 0)
        def _():
            p_ref[...] = jnp.zeros_like(p_ref)

        rowv = (n * BLK + _rows((BLK, BLK))) >= PAD
        _, _, dtp, _, cs, ecs, w, dec = _ssd_decay(dt_ref[...], dtb_ref[...], alog_ref[...], rowv)
        ex = _dot_split(jnp.concatenate([dtp, ecs, w, jnp.broadcast_to(dec, (8, BLK))], axis=0),
                        e_ref[...])
        dtp_c, ecs_c, w_c = ex[0:BLK], ex[BLK:2 * BLK], ex[2 * BLK:3 * BLK]
        dec_c = jnp.max(ex[3 * BLK:EXP_ROWS], axis=0, keepdims=True)
        xv = x_ref[...]
        xdt = xv * dtp_c
        wx = w_c * xdt
        cst = cs.T
        tril = _rows((BLK, BLK)) >= _lanes((BLK, BLK))
        low = _lanes((BLK, BLK)) < HEAD_DIM
        st_ref[0] = p_ref[...]
        for g in range(SSD_GROUPS):
            gs = slice(g * GW, (g + 1) * GW)
            bg = b_ref[:, g * BLK:(g + 1) * BLK]
            cg = c_ref[:, g * BLK:(g + 1) * BLK]
            pg = p_ref[g]
            ydiag, _ = _ssd_group_fwd(g, xdt, cs, cst, bg, cg, tril, low)
            y = ydiag + _dot(cg, pg) * ecs_c[:, gs] + dsk_ref[:, gs] * xv[:, gs]
            p_ref[g] = pg * dec_c[:, gs] + _dot_tn(bg, wx[:, gs])
            yz = y * _silu(z_ref[:, gs])
            r = lax.rsqrt(jnp.mean(yz * yz, axis=-1, keepdims=True) + EPS)
            yn_ref[:, gs] = (yz * r * gn_ref[:, gs]).astype(yn_ref.dtype)

    return pl.pallas_call(
        kern, name=name, grid=(nb,),
        in_specs=_ssd_specs(t, z_off, dt_off, False),
        out_specs=[pl.BlockSpec((BLK, D_SSM), lambda n: (n, 0)),
                   pl.BlockSpec((1, SSD_GROUPS, BLK, GW), lambda n: (n, 0, 0, 0))],
        out_shape=[jax.ShapeDtypeStruct((t, D_SSM), MXU),
                   jax.ShapeDtypeStruct((nb, SSD_GROUPS, BLK, GW), F32)],
        scratch_shapes=[pltpu.VMEM((SSD_GROUPS, BLK, GW), F32)],
        compiler_params=_cp("arbitrary"),
    )(xbc, xbc, xbc, proj, proj, _pad128(dt_bias), _pad128(a_log),
      jnp.repeat(d_skip, HEAD_DIM).reshape(1, D_SSM), gate_norm.reshape(1, D_SSM), _head_expand())


def ssd_bwd(xbc, proj, st, dyn, dt_bias, a_log, d_skip, gate_norm, *, z_off, dt_off, name):
    t = xbc.shape[0]
    nb = t // BLK

    def kern(x_ref, b_ref, c_ref, z_ref, dt_ref, dtb_ref, alog_ref, dsk_ref, gn_ref, e_ref,
             et_ref, st_ref, dyn_ref,
             dxbc_ref, dz_ref, draw_ref, dgn_ref, ddsk_ref, ddtb_ref, dalog_ref,
             dp_ref, tr_ref):
        n = pl.program_id(0)
        blk = nb - 1 - n

        @pl.when(n == 0)
        def _():
            for r_ in (dp_ref, dgn_ref, ddsk_ref, ddtb_ref, dalog_ref):
                r_[...] = jnp.zeros_like(r_)

        rowv = (blk * BLK + _rows((BLK, BLK))) >= PAD
        valid, pre, dtp, av, cs, ecs, w, dec = _ssd_decay(dt_ref[...], dtb_ref[...], alog_ref[...], rowv)
        ex = _dot_split(jnp.concatenate([dtp, ecs, w, jnp.broadcast_to(dec, (8, BLK))], axis=0),
                        e_ref[...])
        dtp_c, ecs_c, w_c = ex[0:BLK], ex[BLK:2 * BLK], ex[2 * BLK:3 * BLK]
        dec_c = jnp.max(ex[3 * BLK:EXP_ROWS], axis=0, keepdims=True)
        xv = x_ref[...]
        xdt = xv * dtp_c
        wx = w_c * xdt
        cst = cs.T
        tril = _rows((BLK, BLK)) >= _lanes((BLK, BLK))
        lane = _lanes((BLK, BLK))
        rowi = _rows((BLK, BLK))
        low = lane < HEAD_DIM
        dcs = jnp.zeros((BLK, BLK), F32)
        dcst = jnp.zeros((BLK, BLK), F32)
        for g in range(SSD_GROUPS):
            gs = slice(g * GW, (g + 1) * GW)
            bg = b_ref[:, g * BLK:(g + 1) * BLK]
            cg = c_ref[:, g * BLK:(g + 1) * BLK]
            pg = st_ref[0, g]
            dpn = dp_ref[g]
            xg = xv[:, gs]
            ydiag, lm = _ssd_group_fwd(g, xdt, cs, cst, bg, cg, tril, low)
            yoff = _dot(cg, pg) * ecs_c[:, gs]
            y = ydiag + yoff + dsk_ref[:, gs] * xg
            zz = z_ref[:, gs]
            sz = _silu(zz)
            yz = y * sz
            r = lax.rsqrt(jnp.mean(yz * yz, axis=-1, keepdims=True) + EPS)
            yhat = yz * r
            dynv = dyn_ref[:, gs].astype(F32)
            gy = dynv * gn_ref[:, gs]
            dgn_ref[:, gs] += jnp.sum(dynv * yhat, axis=0, keepdims=True)
            dyz = r * (gy - yhat * jnp.mean(gy * yhat, axis=-1, keepdims=True))
            dy = dyz * sz
            dz_ref[:, gs] = (dyz * y * _silu_grad(zz)).astype(dz_ref.dtype)
            tr_ref[EXP_ROWS:RED_ROWS, gs] = jnp.broadcast_to(
                jnp.sum(dy * xg, axis=0, keepdims=True), (8, GW))
            dx = dsk_ref[:, gs] * dy
            dwx = _dot(bg, dpn)
            dxdt = w_c[:, gs] * dwx
            tr_ref[0:BLK, gs] = dwx * wx[:, gs]
            dbg = _dot_nt(wx[:, gs], dpn)
            dzo = ecs_c[:, gs] * dy
            tr_ref[BLK:2 * BLK, gs] = dy * yoff
            dcg = _dot_nt(dzo, pg)
            dp_ref[g] = dec_c[:, gs] * dpn + _dot_tn(cg, dzo)
            tr_ref[3 * BLK:EXP_ROWS, gs] = jnp.broadcast_to(
                jnp.sum(dpn * pg, axis=0, keepdims=True), (8, GW))
            dcb = jnp.zeros((BLK, BLK), F32)
            pairs = []
            for j in range(2):
                ps = slice(g * GW + j * BLK, g * GW + (j + 1) * BLK)
                xp = xdt[:, ps]
                dyp = dy[:, j * BLK:(j + 1) * BLK]
                acc = jnp.zeros((BLK, BLK), F32)
                for e in range(2):
                    h = 4 * g + 2 * j + e
                    lmat, mmat = lm[2 * j + e]
                    dyh = jnp.where(low == (e == 0), dyp, 0.0)
                    dm = jnp.where(tril, _dot_nt(dyh, xp), 0.0)
                    nh = dm * mmat
                    dcs = dcs + jnp.where(lane == h, jnp.sum(nh, axis=1, keepdims=True), 0.0)
                    dcst = dcst - jnp.where(rowi == h, jnp.sum(nh, axis=0, keepdims=True), 0.0)
                    dcb = dcb + dm * lmat
                    acc = acc + _dot_tn(mmat, dyh)
                pairs.append(acc)
            dxdt = dxdt + jnp.concatenate(pairs, axis=1)
            dcg = dcg + _dot(dcb, bg)
            dbg = dbg + _dot_tn(dcb, cg)
            tr_ref[2 * BLK:3 * BLK, gs] = dxdt * xg
            dxbc_ref[:, gs] = dx + dxdt * dtp_c[:, gs]
            dxbc_ref[:, D_SSM + g * BLK:D_SSM + (g + 1) * BLK] = dbg
            dxbc_ref[:, D_SSM + 1024 + g * BLK:D_SSM + 1024 + (g + 1) * BLK] = dcg
        red = _dot(tr_ref[...], et_ref[...])
        r1, r2, r3 = red[0:BLK], red[BLK:2 * BLK], red[2 * BLK:3 * BLK]
        ddec = jnp.max(red[3 * BLK:EXP_ROWS], axis=0, keepdims=True)
        ddsk_ref[...] += jnp.max(red[EXP_ROWS:RED_ROWS], axis=0, keepdims=True)
        dcs = dcs + dcst.T - r1 + r2
        dcs_last = jnp.sum(r1, axis=0, keepdims=True) + ddec * dec
        dcs = dcs + jnp.where(rowi == BLK - 1, dcs_last, 0.0)
        dda = _cumsum_rev(dcs)
        ddtp = r3 + dda * av
        dalog_ref[...] += jnp.sum(dda * dtp, axis=0, keepdims=True) * av
        draw = jnp.where(valid, ddtp * _sigmoid(pre), 0.0)
        ddtb_ref[...] += jnp.sum(draw, axis=0, keepdims=True)
        draw_ref[...] = draw.astype(draw_ref.dtype)

    vec = lambda w_: pl.BlockSpec((1, w_), lambda n: (0, 0))
    rb = lambda w_: pl.BlockSpec((BLK, w_), lambda n: (nb - 1 - n, 0))
    e = _head_expand()
    res = pl.pallas_call(
        kern, name=name, grid=(nb,),
        in_specs=_ssd_specs(t, z_off, dt_off, True)
        + [pl.BlockSpec((D_SSM, BLK), lambda n: (0, 0)),
           pl.BlockSpec((1, SSD_GROUPS, BLK, GW), lambda n: (nb - 1 - n, 0, 0, 0)),
           rb(D_SSM)],
        out_specs=[rb(2 * D_SSM), rb(D_SSM), rb(BLK), vec(D_SSM), vec(BLK), vec(BLK), vec(BLK)],
        out_shape=[jax.ShapeDtypeStruct((t, 2 * D_SSM), F32), jax.ShapeDtypeStruct((t, D_SSM), MXU),
                   jax.ShapeDtypeStruct((t, BLK), MXU), jax.ShapeDtypeStruct((1, D_SSM), F32),
                   jax.ShapeDtypeStruct((1, BLK), F32), jax.ShapeDtypeStruct((1, BLK), F32),
                   jax.ShapeDtypeStruct((1, BLK), F32)],
        scratch_shapes=[pltpu.VMEM((SSD_GROUPS, BLK, GW), F32), pltpu.VMEM((RED_ROWS, D_SSM), F32)],
        compiler_params=_cp("arbitrary"),
    )(xbc, xbc, xbc, proj, proj, _pad128(dt_bias), _pad128(a_log),
      jnp.repeat(d_skip, HEAD_DIM).reshape(1, D_SSM), gate_norm.reshape(1, D_SSM), e, e.T, st, dyn)
    dxbc, dz, draw, dgn, ddsk, ddtb, dalog = res
    return dxbc, dz, draw, dgn[0], ddsk[0, :SSD_HEADS], ddtb[0, :SSD_HEADS], dalog[0, :SSD_HEADS]


def loss_fwd_bwd(h, target, *, name):
    t, d = h.shape
    nb = t // BLK

    def kern(h_ref, t_ref, loss_ref, dh_ref):
        n = pl.program_id(0)
        err = jnp.where(n > 0, h_ref[...] - t_ref[...], 0.0)
        dh_ref[...] = err * (1.0 / d)
        part = (0.5 / d) * jnp.sum(jnp.sum(err * err, axis=1, keepdims=True), axis=0, keepdims=True)

        @pl.when(n == 0)
        def _():
            loss_ref[...] = part

        @pl.when(n > 0)
        def _():
            loss_ref[...] += part

    return pl.pallas_call(
        kern, name=name, grid=(nb,),
        in_specs=[pl.BlockSpec((BLK, d), lambda n: (n, 0)),
                  pl.BlockSpec((BLK, d), lambda n: (jnp.maximum(n - 1, 0), 0))],
        out_specs=[pl.BlockSpec((1, 1), lambda n: (0, 0)), pl.BlockSpec((BLK, d), lambda n: (n, 0))],
        out_shape=[jax.ShapeDtypeStruct((1, 1), F32), jax.ShapeDtypeStruct((t, d), F32)],
        compiler_params=_cp("arbitrary"),
    )(h, target)


def _ew_tile(r, c):
    cap = max(16, (256 * 1024) // c)
    best = None
    for tr in range(16, min(r, cap) + 1, 16):
        if r % tr == 0:
            best = tr
    return best if best is not None else r


def adamw(parts, w, m, v, *, name):
    npart, r, c = parts.shape
    tr = _ew_tile(r, c)

    def kern(p_ref, w_ref, m_ref, v_ref, g_ref, d_ref, m2_ref, v2_ref):
        g = p_ref[0].astype(F32)
        for k in range(1, npart):
            g = g + p_ref[k].astype(F32)
        m2 = ADAM_B1 * m_ref[...] + (1.0 - ADAM_B1) * g
        v2 = ADAM_B2 * v_ref[...] + (1.0 - ADAM_B2) * (g * g)
        m_hat = m2 / (1.0 - ADAM_B1 ** ADAM_STEP)
        v_hat = v2 / (1.0 - ADAM_B2 ** ADAM_STEP)
        g_ref[...] = g
        d_ref[...] = -ADAM_LR * (m_hat / (jnp.sqrt(v_hat) + ADAM_EPS) + ADAM_WD * w_ref[...])
        m2_ref[...] = m2
        v2_ref[...] = v2

    row = pl.BlockSpec((tr, c), lambda i: (i, 0))
    sds = jax.ShapeDtypeStruct((r, c), F32)
    return pl.pallas_call(
        kern, name=name, grid=(r // tr,),
        in_specs=[pl.BlockSpec((npart, tr, c), lambda i: (0, i, 0)), row, row, row],
        out_specs=[row, row, row, row], out_shape=[sds, sds, sds, sds],
        compiler_params=_cp("parallel"),
    )(parts, w, m, v)


def pair_add(p, land, *, name):
    _, r, c = p.shape
    tr = _ew_tile(r, c)
    core = lax.axis_index("c").astype(jnp.int32).reshape(1)

    def kern(c_ref, p_ref, l_ref, o_ref):
        o_ref[...] = (p_ref[...] + l_ref[...]).astype(o_ref.dtype)

    return pl.pallas_call(
        kern, name=name,
        grid_spec=pltpu.PrefetchScalarGridSpec(
            num_scalar_prefetch=1, grid=(4, r // tr),
            in_specs=[pl.BlockSpec((1, tr, c), lambda k, i, c_ref: (2 * k + c_ref[0], i, 0)),
                      pl.BlockSpec((1, tr, c), lambda k, i, c_ref: (k, i, 0))],
            out_specs=pl.BlockSpec((1, tr, c), lambda k, i, c_ref: (k, i, 0))),
        out_shape=jax.ShapeDtypeStruct((4, r, c), BF16),
        compiler_params=_cp("parallel", "parallel"),
    )(core, p, land)


def _me():
    return lax.axis_index("x"), lax.axis_index("y"), lax.axis_index("c")


def all_gather(xs, *, name):
    n = len(xs)

    def body(*refs):
        x_refs, out_refs = refs[:n], refs[n:2 * n]
        send_sems, recv_sems, local_sems = refs[2 * n:]
        mx, my, mc = _me()
        me, sib = (mx, my, mc), (mx, my, 1 - mc)
        chips = [(1 - mx, my), (mx, 1 - my), (1 - mx, 1 - my)]

        def rows(i, px, py, pc):
            return out_refs[i].at[4 * px + 2 * py + pc]

        def copy(i, k, block, to, src=None):
            return pltpu.make_async_remote_copy(
                src_ref=rows(i, *block) if src is None else src, dst_ref=rows(i, *block),
                send_sem=send_sems.at[7 * i + k], recv_sem=recv_sems.at[7 * i + k],
                device_id=to, device_id_type=MESH)

        mine = [pltpu.make_async_copy(x_refs[i], rows(i, *me), local_sems.at[i]) for i in range(n)]
        first = []
        for i in range(n):
            mine[i].start()
            first.append(copy(i, 0, me, sib, src=x_refs[i]))
            first += [copy(i, 1 + j, me, (*chip, mc), src=x_refs[i]) for j, chip in enumerate(chips)]
        for cp in first:
            cp.start()
        passed = []
        for i in range(n):
            for j, chip in enumerate(chips):
                copy(i, 1 + j, (*chip, mc), me).wait_recv()
                passed.append(copy(i, 4 + j, (*chip, mc), sib))
                passed[-1].start()
        for i in range(n):
            copy(i, 0, sib, me).wait_recv()
            for j, chip in enumerate(chips):
                copy(i, 4 + j, (*chip, 1 - mc), me).wait_recv()
        for cp in first + passed:
            cp.wait_send()
        for cp in mine:
            cp.wait()

    return pl.pallas_call(
        body, name=name,
        out_shape=[jax.ShapeDtypeStruct((N_DEV,) + x.shape, x.dtype) for x in xs],
        in_specs=[ANY] * n, out_specs=[ANY] * n,
        scratch_shapes=[pltpu.SemaphoreType.DMA((7 * n,)), pltpu.SemaphoreType.DMA((7 * n,)),
                        pltpu.SemaphoreType.DMA((n,))],
    )(*xs)


def pair_exchange(ps, *, name):
    n = len(ps)

    def body(*refs):
        p_refs, out_refs = refs[:n], refs[n:2 * n]
        send_sems, recv_sems = refs[2 * n:]
        mx, my, mc = _me()
        cps = [pltpu.make_async_remote_copy(
            src_ref=p_refs[i].at[2 * k + (1 - mc)], dst_ref=out_refs[i].at[k],
            send_sem=send_sems.at[4 * i + k], recv_sem=recv_sems.at[4 * i + k],
            device_id=(mx, my, 1 - mc), device_id_type=MESH) for i in range(n) for k in range(4)]
        for cp in cps:
            cp.start()
        for cp in cps:
            cp.wait_recv()
        for cp in cps:
            cp.wait_send()

    return pl.pallas_call(
        body, name=name,
        out_shape=[jax.ShapeDtypeStruct((4,) + p.shape[1:], p.dtype) for p in ps],
        in_specs=[ANY] * n, out_specs=[ANY] * n,
        scratch_shapes=[pltpu.SemaphoreType.DMA((4 * n,)), pltpu.SemaphoreType.DMA((4 * n,))],
    )(*ps)


def chip_exchange(qs, *, name):
    n = len(qs)

    def body(*refs):
        q_refs, out_refs = refs[:n], refs[n:2 * n]
        send_sems, recv_sems, local_sems = refs[2 * n:]
        mx, my, mc = _me()
        mine = 2 * mx + my
        chips = [(1 - mx, my), (mx, 1 - my), (1 - mx, 1 - my)]
        local, sends, recvs = [], [], []
        for i in range(n):
            local.append(pltpu.make_async_copy(q_refs[i].at[mine], out_refs[i].at[mine], local_sems.at[i]))
            for k, (px, py) in enumerate(chips):
                sems = dict(send_sem=send_sems.at[3 * i + k], recv_sem=recv_sems.at[3 * i + k],
                            device_id=(px, py, mc), device_id_type=MESH)
                sends.append(pltpu.make_async_remote_copy(
                    src_ref=q_refs[i].at[2 * px + py], dst_ref=out_refs[i].at[mine], **sems))
                recvs.append(pltpu.make_async_remote_copy(
                    src_ref=q_refs[i].at[mine], dst_ref=out_refs[i].at[2 * px + py], **sems))
        for cp in local + sends:
            cp.start()
        for cp in recvs:
            cp.wait_recv()
        for cp in sends:
            cp.wait_send()
        for cp in local:
            cp.wait()

    return pl.pallas_call(
        body, name=name,
        out_shape=[jax.ShapeDtypeStruct(q.shape, q.dtype) for q in qs],
        in_specs=[ANY] * n, out_specs=[ANY] * n,
        scratch_shapes=[pltpu.SemaphoreType.DMA((3 * n,)), pltpu.SemaphoreType.DMA((3 * n,)),
                        pltpu.SemaphoreType.DMA((n,))],
    )(*qs)


WEIGHTS = [
    "meta_tokens", "l0_mix_pre_norm", "l0_mix_post_norm", "l0_w_in", "l0_lru_conv_w", "l0_lru_conv_b",
    "l0_lru_w_a", "l0_lru_b_a", "l0_lru_w_x", "l0_lru_b_x", "l0_lru_lambda", "l0_attn_sinks", "l0_w_out",
    "l0_ffn_pre_norm", "l0_ffn_post_norm", "l0_ffn_w_up", "l0_ffn_conv_w", "l0_ffn_conv_b", "l0_ffn_w_down",
    "l1_mix_pre_norm", "l1_mix_post_norm", "l1_w_in", "l1_ssm_conv_w", "l1_ssm_conv_b", "l1_dt_bias",
    "l1_a_log", "l1_d_skip", "l1_gate_norm", "l1_w_out", "l1_ffn_pre_norm", "l1_ffn_post_norm",
    "l1_ffn_w_up", "l1_ffn_conv_w", "l1_ffn_conv_b", "l1_ffn_w_down",
]
INPUTS = (["x"] + WEIGHTS + ["loss_target"] + ["m_" + n for n in WEIGHTS] + ["v_" + n for n in WEIGHTS])

MATS = {"l0_w_in": ("col", (1024, 3328)), "l0_w_out": ("row", (2048, 1024)),
        "l0_ffn_w_up": ("col", (1024, 5632)), "l0_ffn_w_down": ("row", (2816, 1024)),
        "l1_w_in": ("col", (1024, 6176)), "l1_w_out": ("row", (2048, 1024)),
        "l1_ffn_w_up": ("col", (1024, 5632)), "l1_ffn_w_down": ("row", (2816, 1024))}
SMALL_SHARDED = {"meta_tokens": ("col", (16, 1024)), "l0_lru_conv_w": ("col", (4, 1024)),
                 "l0_ffn_conv_w": ("col", (3, 5632)), "l1_ssm_conv_w": ("col", (4, 4096)),
                 "l1_ffn_conv_w": ("col", (3, 5632))}
SHARDED = {**MATS, **SMALL_SHARDED}
REPLICATED = [n for n in WEIGHTS if n not in SHARDED]
PACK_W = 1024
SMALL_W = 128


def _shard_shape(name):
    kind, (r, c) = SHARDED[name]
    return (r, c // N_DEV) if kind == "col" else (r // N_DEV, c)


def _rows_of(numel, width):
    return -(-numel // width)


def _to_rows(a, width):
    flat = a.reshape(-1)
    rows = _rows_of(flat.shape[0], width)
    return jnp.pad(flat, (0, rows * width - flat.shape[0])).reshape(rows, width)


def _pack(arrs, width, total_rows):
    slab = jnp.concatenate([_to_rows(a, width) for a in arrs], axis=0)
    return jnp.pad(slab, ((0, total_rows - slab.shape[0]), (0, 0)))


def _unpack(slab, shapes, width):
    out, off = [], 0
    for shp in shapes:
        numel = math.prod(shp)
        rows = _rows_of(numel, width)
        out.append(slab[off:off + rows].reshape(-1)[:numel].reshape(shp))
        off += rows
    return out


def _round_up(n, m):
    return -(-n // m) * m


def _by_dest(name, g):
    kind, (r, c) = SHARDED[name]
    if kind == "col":
        return g.reshape(r, N_DEV, c // N_DEV).transpose(1, 0, 2)
    return g.reshape(N_DEV, r // N_DEV, c)


def _from_shards(name, blocks):
    kind, (r, c) = SHARDED[name]
    return blocks.transpose(1, 0, 2).reshape(r, c) if kind == "col" else blocks.reshape(r, c)


def _gather_params(a):
    names = list(SHARDED)
    got = all_gather([a[n].astype(MXU) if n in MATS else a[n] for n in names], name="gather_params")
    return {n: _from_shards(n, blocks) for n, blocks in zip(names, got)}


L1_IN_PAD = 6272


def _ffn_fwd(h, a, w, pfx):
    u = rmsnorm_fwd(h, a[pfx + "ffn_pre_norm"], out_dtype=MXU, name=pfx + "ffn_pre")
    up = matmul(u, w[pfx + "ffn_w_up"], name=pfx + "ffn_up")
    act = dwconv_fwd(up, a[pfx + "ffn_conv_w"], a[pfx + "ffn_conv_b"], mode="geglu", x_off=0,
                     c_out=D_FF, cblk=256, out_dtype=MXU, name=pfx + "ffn_act")
    down = matmul(act, w[pfx + "ffn_w_down"], name=pfx + "ffn_down")
    out = rmsnorm_fwd(down, a[pfx + "ffn_post_norm"], res=h, out_dtype=F32, name=pfx + "ffn_post")
    return out, (h, u, up, act, down)


def _ffn_bwd(dh, saved, a, w, pfx, g):
    h, u, up, act, down = saved
    dd, g[pfx + "ffn_post_norm"] = rmsnorm_bwd(down, a[pfx + "ffn_post_norm"], dh, out_dtype=MXU,
                                               name=pfx + "ffn_post_bwd")
    dact = matmul(dd, w[pfx + "ffn_w_down"], trans_b=True, name=pfx + "ffn_down_dx")
    g[pfx + "ffn_w_down"] = matmul(act.T, dd, name=pfx + "ffn_down_dw")
    dups, g[pfx + "ffn_conv_w"], g[pfx + "ffn_conv_b"] = dwconv_bwd(
        up, a[pfx + "ffn_conv_w"], a[pfx + "ffn_conv_b"], dact, mode="geglu", x_off=0, c_out=D_FF,
        cblk=256, name=pfx + "ffn_act_bwd")
    ut = u.T
    g[pfx + "ffn_w_up"] = jnp.concatenate(
        [matmul(ut, d, name=pfx + "ffn_up_dw%d" % i) for i, d in enumerate(dups)], axis=1)
    du = matmul_cat(dups, w[pfx + "ffn_w_up"], trans_b=True, name=pfx + "ffn_up_dx")
    dh_in, g[pfx + "ffn_pre_norm"] = rmsnorm_bwd(h, a[pfx + "ffn_pre_norm"], du, res=dh, out_dtype=F32,
                                                 name=pfx + "ffn_pre_bwd")
    return dh_in


def _local_step(a, w):
    x = a["x"][0]
    seq = x.shape[0]
    h0 = jnp.concatenate([jnp.zeros((PAD, D_MODEL), F32), a["meta_tokens"], x], axis=0)
    g = {}

    u0 = rmsnorm_fwd(h0, a["l0_mix_pre_norm"], out_dtype=MXU, name="l0_mix_pre")
    proj0 = matmul(u0, w["l0_w_in"], name="l0_in")
    lru = (a["l0_lru_conv_w"], a["l0_lru_conv_b"], a["l0_lru_w_a"], a["l0_lru_b_a"], a["l0_lru_w_x"],
           a["l0_lru_b_x"], a["l0_lru_lambda"])
    ya, hl = lru_fwd(proj0, *lru, gate_off=0, xr_off=1024, name="l0_lru")
    yb = attn_fwd(proj0, a["l0_attn_sinks"], q_off=2048, k_off=3072, v_off=3200, name="l0_attn")
    o0 = matmul_cat([ya, yb], w["l0_w_out"], name="l0_out")
    h1 = rmsnorm_fwd(o0, a["l0_mix_post_norm"], res=h0, out_dtype=F32, name="l0_mix_post")
    h2, ffn0 = _ffn_fwd(h1, a, w, "l0_")

    u2 = rmsnorm_fwd(h2, a["l1_mix_pre_norm"], out_dtype=MXU, name="l1_mix_pre")
    proj1 = matmul(u2, w["l1_w_in"], name="l1_in")
    xbc = dwconv_fwd(proj1, a["l1_ssm_conv_w"], a["l1_ssm_conv_b"], mode="silu", x_off=D_SSM,
                     c_out=2 * D_SSM, cblk=512, out_dtype=F32, name="l1_ssm_conv")
    ssd = (a["l1_dt_bias"], a["l1_a_log"], a["l1_d_skip"], a["l1_gate_norm"])
    yn, st = ssd_fwd(xbc, proj1, *ssd, z_off=0, dt_off=3 * D_SSM, name="l1_ssd")
    o1 = matmul(yn, w["l1_w_out"], name="l1_out")
    h3 = rmsnorm_fwd(o1, a["l1_mix_post_norm"], res=h2, out_dtype=F32, name="l1_mix_post")
    h4, ffn1 = _ffn_fwd(h3, a, w, "l1_")

    loss, dh4 = loss_fwd_bwd(h4, a["loss_target"][0], name="loss")

    dh3 = _ffn_bwd(dh4, ffn1, a, w, "l1_", g)
    do1, g["l1_mix_post_norm"] = rmsnorm_bwd(o1, a["l1_mix_post_norm"], dh3, out_dtype=MXU,
                                             name="l1_mix_post_bwd")
    dyn = matmul(do1, w["l1_w_out"], trans_b=True, name="l1_out_dx")
    g["l1_w_out"] = matmul(yn.T, do1, name="l1_out_dw")
    (dxbc, dz, draw, g["l1_gate_norm"], g["l1_d_skip"], g["l1_dt_bias"], g["l1_a_log"]) = ssd_bwd(
        xbc, proj1, st, dyn, *ssd, z_off=0, dt_off=3 * D_SSM, name="l1_ssd_bwd")
    (dxin,), g["l1_ssm_conv_w"], g["l1_ssm_conv_b"] = dwconv_bwd(
        proj1, a["l1_ssm_conv_w"], a["l1_ssm_conv_b"], dxbc, mode="silu", x_off=D_SSM,
        c_out=2 * D_SSM, cblk=512, name="l1_ssm_conv_bwd")
    u2t = u2.T
    g["l1_w_in"] = jnp.concatenate(
        [matmul(u2t, dz, name="l1_in_dw_z"), matmul(u2t, dxin, name="l1_in_dw_x"),
         matmul(u2t, draw, name="l1_in_dw_dt")[:, :SSD_HEADS]], axis=1)
    du2 = matmul_cat([dz, dxin, draw], w["l1_w_in"], trans_b=True, name="l1_in_dx")
    dh2, g["l1_mix_pre_norm"] = rmsnorm_bwd(h2, a["l1_mix_pre_norm"], du2, res=dh3, out_dtype=F32,
                                            name="l1_mix_pre_bwd")

    dh1 = _ffn_bwd(dh2, ffn0, a, w, "l0_", g)
    do0, g["l0_mix_post_norm"] = rmsnorm_bwd(o0, a["l0_mix_post_norm"], dh1, out_dtype=MXU,
                                             name="l0_mix_post_bwd")
    dy = matmul(do0, w["l0_w_out"], trans_b=True, name="l0_out_dx")
    g["l0_w_out"] = jnp.concatenate([matmul(ya.T, do0, name="l0_out_dw_a"),
                                     matmul(yb.T, do0, name="l0_out_dw_b")], axis=0)
    (dgate, dxr, g["l0_lru_conv_w"], dcb, g["l0_lru_w_a"], dba, g["l0_lru_w_x"], dbx, dlam) = lru_bwd(
        proj0, hl, dy, *lru, gate_off=0, xr_off=1024, dy_off=0, name="l0_lru_bwd")
    g["l0_lru_conv_b"], g["l0_lru_b_a"], g["l0_lru_b_x"], g["l0_lru_lambda"] = dcb[0], dba[0], dbx[0], dlam[0]
    dq, dk, dv, g["l0_attn_sinks"] = attn_bwd(proj0, a["l0_attn_sinks"], dy, q_off=2048, k_off=3072,
                                              v_off=3200, dy_off=1024, name="l0_attn_bwd")
    dproj0 = [dgate, dxr, dq, dk, dv]
    u0t = u0.T
    g["l0_w_in"] = jnp.concatenate(
        [matmul(u0t, d, name="l0_in_dw%d" % i) for i, d in enumerate(dproj0)], axis=1)
    du0 = matmul_cat(dproj0, w["l0_w_in"], trans_b=True, name="l0_in_dx")
    dh0, g["l0_mix_pre_norm"] = rmsnorm_bwd(h0, a["l0_mix_pre_norm"], du0, res=dh1, out_dtype=F32,
                                            name="l0_mix_pre_bwd")
    g["meta_tokens"] = dh0[PAD:BLK]
    for n in REPLICATED:
        g[n] = g[n].reshape(a[n].shape)
    return loss[0, 0], dh0[BLK:].reshape(1, seq, D_MODEL), g


def kernel(*args):
    a = dict(zip(INPUTS, args))
    full = _gather_params(a)
    w = {n: full[n] for n in MATS}
    w["l1_w_in"] = jnp.pad(w["l1_w_in"], ((0, 0), (0, L1_IN_PAD - w["l1_w_in"].shape[1])))
    loss_part, grad_x, g = _local_step({**a, **{n: full[n] for n in SMALL_SHARDED}}, w)
    loss = lax.psum(loss_part, ("x", "y", "c"))

    sh_names = list(SHARDED)
    parts = [_by_dest(n, g[n]) for n in sh_names]
    from_sibling = pair_exchange(parts, name="rs_pair")
    pairs = [pair_add(p, l, name="rs_pair_add_" + n) for n, p, l in zip(sh_names, parts, from_sibling)]
    landed = chip_exchange(pairs, name="rs_chip")
    sh_out = {n: adamw(l, a[n], a["m_" + n], a["v_" + n], name="adamw_" + n)
              for n, l in zip(sh_names, landed)}

    rp_shapes = [a[n].shape for n in REPLICATED]
    rrows = _round_up(sum(_rows_of(math.prod(s), SMALL_W) for s in rp_shapes), 128)
    gathered = all_gather([_pack([g[n] for n in REPLICATED], SMALL_W, rrows)], name="gather_small_grads")[0]
    rp_out = adamw(gathered, *[_pack([a[p + n] for n in REPLICATED], SMALL_W, rrows) for p in ("", "m_", "v_")],
                   name="adamw_replicated")
    rp_out = [dict(zip(REPLICATED, _unpack(s, rp_shapes, SMALL_W))) for s in rp_out]

    outs = [loss, grad_x]
    for k in range(4):
        outs += [sh_out[n][k] if n in SHARDED else rp_out[k][n] for n in WEIGHTS]
    return tuple(outs)
```

```python
import functools
import math

import jax
import jax.numpy as jnp
from jax import lax
from jax.experimental import pallas as pl
from jax.experimental.pallas import tpu as pltpu

F32 = jnp.float32
BF16 = jnp.bfloat16
MXU = jnp.bfloat16

D_MODEL = 1024
N_META = 16
BLK = 128
PAD = BLK - N_META
D_RNN = 1024
LRU_C = 8.0
N_Q_HEADS = 16
HEAD_DIM = 64
D_SSM = 2048
SSD_HEADS = 32
SSD_GROUPS = 8
D_FF = 2816
EPS = 1e-6
NEG = -1e30
N_DEV = 8

ADAM_LR = 0.001
ADAM_B1 = 0.9
ADAM_B2 = 0.999
ADAM_EPS = 1e-08
ADAM_WD = 0.01
ADAM_STEP = 10

VMEM_LIMIT = 56 * 1024 * 1024
MESH = pl.DeviceIdType.MESH
ANY = pl.BlockSpec(memory_space=pl.ANY)


def _cp(*sem):
    return pltpu.CompilerParams(dimension_semantics=sem, vmem_limit_bytes=VMEM_LIMIT)


def _pick(n, cands):
    for c in cands:
        if n % c == 0:
            return c
    return n


def _dot(a, b):
    return jnp.dot(a.astype(MXU), b.astype(MXU), preferred_element_type=F32)


def _dot_nt(a, b):
    return lax.dot_general(a.astype(MXU), b.astype(MXU), (((1,), (1,)), ((), ())),
                           preferred_element_type=F32)


def _dot_tn(a, b):
    return jnp.dot(a.T.astype(MXU), b.astype(MXU), preferred_element_type=F32)


def _dot_split(v, e):
    hi = v.astype(BF16)
    lo = (v - hi.astype(F32)).astype(BF16)
    return (jnp.dot(hi, e, preferred_element_type=F32)
            + jnp.dot(lo, e, preferred_element_type=F32))


def _sigmoid(x):
    return 1.0 / (1.0 + jnp.exp(-x))


def _log1p(x):
    u = 1.0 + x
    return jnp.where(u == 1.0, x, jnp.log(u) * (x / jnp.where(u == 1.0, 1.0, u - 1.0)))


def _expm1(x):
    u = jnp.exp(x)
    um1 = u - 1.0
    lg = jnp.log(jnp.where(u > 0.0, u, 1.0))
    safe = (um1 != 0.0) & (um1 != -1.0)
    return jnp.where(um1 == 0.0, x, jnp.where(um1 == -1.0, -1.0,
                                               um1 * (x / jnp.where(safe, lg, 1.0))))


def _softplus(x):
    return jnp.maximum(x, 0.0) + _log1p(jnp.exp(-jnp.abs(x)))


_GC = math.sqrt(2.0 / math.pi)


def _gelu(x):
    t = jnp.tanh(_GC * (x + 0.044715 * x * x * x))
    return 0.5 * x * (1.0 + t)


def _gelu_grad(x):
    t = jnp.tanh(_GC * (x + 0.044715 * x * x * x))
    return 0.5 * (1.0 + t) + 0.5 * x * (1.0 - t * t) * (_GC * (1.0 + 3.0 * 0.044715 * x * x))


def _silu(x):
    return x * _sigmoid(x)


def _silu_grad(x):
    s = _sigmoid(x)
    return s * (1.0 + x * (1.0 - s))


def _rows(shape):
    return lax.broadcasted_iota(jnp.int32, shape, 0)


def _lanes(shape):
    return lax.broadcasted_iota(jnp.int32, shape, 1)


def _shift_down(x, tail, d):
    if d == 0:
        return x
    n = x.shape[0]
    xr = pltpu.roll(x, d, 0)
    tr = pltpu.roll(tail, d, 0)
    first = jnp.where(_rows(tr.shape) < d, tr, xr[0:8])
    return jnp.concatenate([first, xr[8:n]], axis=0)


def _shift_up(x, head, d):
    if d == 0:
        return x
    n = x.shape[0]
    xr = pltpu.roll(x, n - d, 0)
    hr = pltpu.roll(head, 8 - d, 0)
    last = jnp.where(_rows(hr.shape) >= 8 - d, hr, xr[n - 8:n])
    return jnp.concatenate([xr[0:n - 8], last], axis=0)


def _row_at(x, i):
    return jnp.sum(jnp.where(_rows(x.shape) == i, x, 0.0), axis=0, keepdims=True)


def _scan_fwd(a, u):
    n = a.shape[0]
    ri = _rows(a.shape)
    d = 1
    while d < n:
        m = ri >= d
        us = jnp.where(m, pltpu.roll(u, d, 0), 0.0)
        as_ = jnp.where(m, pltpu.roll(a, d, 0), 1.0)
        u = u + a * us
        a = a * as_
        d *= 2
    return a, u


def _scan_rev(c, u):
    n = c.shape[0]
    ri = _rows(c.shape)
    d = 1
    while d < n:
        m = ri < n - d
        us = jnp.where(m, pltpu.roll(u, n - d, 0), 0.0)
        cs = jnp.where(m, pltpu.roll(c, n - d, 0), 1.0)
        u = u + c * us
        c = c * cs
        d *= 2
    return c, u


def _cumsum_fwd(x):
    n = x.shape[0]
    ri = _rows(x.shape)
    d = 1
    while d < n:
        x = x + jnp.where(ri >= d, pltpu.roll(x, d, 0), 0.0)
        d *= 2
    return x


def _cumsum_rev(x):
    n = x.shape[0]
    ri = _rows(x.shape)
    d = 1
    while d < n:
        x = x + jnp.where(ri < n - d, pltpu.roll(x, n - d, 0), 0.0)
        d *= 2
    return x


MATMUL_VMEM = 40 * 1024 * 1024


def _matmul_tiles(m, n, k, tk, out_bytes):
    best = None
    for tm in (1664, 1408, 1040, 1024, 832, 640, 512, 384, 256, 128):
        if m % tm:
            continue
        for tn in (2048, 1408, 1024, 896, 640, 512, 384, 256, 128):
            if n % tn:
                continue
            vmem = 2 * (tm * tk * 2 + tk * tn * 2 + tm * tn * out_bytes) + (tm * tn * 4 if k > tk else 0)
            if vmem > MATMUL_VMEM:
                continue
            traffic = (n // tn) * m * k * 2 + (m // tm) * k * n * 2
            if best is None or traffic < best[0]:
                best = (traffic, tm, tn)
    return (best[1], best[2]) if best else (m, n)


def matmul(a, b, *, trans_b=False, out_dtype=F32, name):
    m, k = a.shape
    n = b.shape[0] if trans_b else b.shape[1]
    tk = k if k <= 2048 else _pick(k, (1664, 1408, 1024, 896, 512, 256, 128))
    nk = k // tk
    tm, tn = _matmul_tiles(m, n, k, tk, jnp.dtype(out_dtype).itemsize)

    def product(a_ref, b_ref):
        return _dot_nt(a_ref[...], b_ref[...]) if trans_b else _dot(a_ref[...], b_ref[...])

    def kern_once(a_ref, b_ref, o_ref):
        o_ref[...] = product(a_ref, b_ref).astype(o_ref.dtype)

    def kern_acc(a_ref, b_ref, o_ref, acc_ref):
        kk = pl.program_id(2)

        @pl.when(kk == 0)
        def _():
            acc_ref[...] = product(a_ref, b_ref)

        @pl.when(kk > 0)
        def _():
            acc_ref[...] += product(a_ref, b_ref)

        @pl.when(kk == nk - 1)
        def _():
            o_ref[...] = acc_ref[...].astype(o_ref.dtype)

    b_spec = (pl.BlockSpec((tn, tk), lambda i, j, kk: (j, kk)) if trans_b
              else pl.BlockSpec((tk, tn), lambda i, j, kk: (kk, j)))
    return pl.pallas_call(
        kern_once if nk == 1 else kern_acc, name=name,
        grid=(m // tm, n // tn, nk),
        in_specs=[pl.BlockSpec((tm, tk), lambda i, j, kk: (i, kk)), b_spec],
        out_specs=pl.BlockSpec((tm, tn), lambda i, j, kk: (i, j)),
        out_shape=jax.ShapeDtypeStruct((m, n), out_dtype),
        scratch_shapes=[] if nk == 1 else [pltpu.VMEM((tm, tn), F32)],
        compiler_params=_cp("parallel", "parallel", "arbitrary"),
    )(a, b)


def matmul_cat(a_list, b, *, trans_b=False, out_dtype=F32, name):
    m = a_list[0].shape[0]
    ks = [x.shape[1] for x in a_list]
    ktot = sum(ks)
    n = b.shape[0] if trans_b else b.shape[1]
    tn = _pick(n, (512, 256, 128))
    tm = next((c for c in (1664, 1040, 832, 640, 512, 384, 256, 128)
               if m % c == 0 and c * ktot * 2 <= 8 * 1024 * 1024), m)
    na = len(a_list)

    def kern(*refs):
        b_ref, o_ref = refs[na], refs[na + 1]
        acc, off = None, 0
        for a_ref, k in zip(refs[:na], ks):
            if trans_b:
                part = _dot_nt(a_ref[...], b_ref[:, off:off + k])
            else:
                part = _dot(a_ref[...], b_ref[off:off + k, :])
            acc = part if acc is None else acc + part
            off += k
        o_ref[...] = acc.astype(o_ref.dtype)

    b_spec = (pl.BlockSpec((tn, ktot), lambda i, j: (j, 0)) if trans_b
              else pl.BlockSpec((ktot, tn), lambda i, j: (0, j)))
    return pl.pallas_call(
        kern, name=name, grid=(m // tm, n // tn),
        in_specs=[pl.BlockSpec((tm, k), lambda i, j: (i, 0)) for k in ks] + [b_spec],
        out_specs=pl.BlockSpec((tm, tn), lambda i, j: (i, j)),
        out_shape=jax.ShapeDtypeStruct((m, n), out_dtype),
        compiler_params=_cp("parallel", "parallel"),
    )(*a_list, b)


def _row_tile(t):
    return _pick(t, (832, 640, 512, 384, 256, 128))


def rmsnorm_fwd(x, w, res=None, *, out_dtype, name, with_t=False):
    t, d = x.shape
    tr = _conv_tile(t) if with_t else _row_tile(t)

    def kern(*refs):
        x_ref, w_ref = refs[0], refs[1]
        o_ref = refs[-2] if with_t else refs[-1]
        xv = x_ref[...]
        r = lax.rsqrt(jnp.mean(xv * xv, axis=-1, keepdims=True) + EPS)
        y = xv * r * w_ref[...]
        if res is not None:
            y = refs[2][...] + y
        o_ref[...] = y.astype(o_ref.dtype)
        if with_t:
            refs[-1][...] = y.T.astype(o_ref.dtype)

    row = pl.BlockSpec((tr, d), lambda i: (i, 0))
    vec = pl.BlockSpec((1, d), lambda i: (0, 0))
    ins = [x, w.reshape(1, d)] + ([] if res is None else [res])
    specs = [row, vec] + ([] if res is None else [row])
    out_specs, out_shape = row, jax.ShapeDtypeStruct((t, d), out_dtype)
    if with_t:
        out_specs = [row, pl.BlockSpec((d, tr), lambda i: (0, i))]
        out_shape = [out_shape, jax.ShapeDtypeStruct((d, t), out_dtype)]
    return pl.pallas_call(
        kern, name=name, grid=(t // tr,), in_specs=specs, out_specs=out_specs, out_shape=out_shape,
        compiler_params=_cp("parallel"),
    )(*ins)


def rmsnorm_bwd(x, w, dy, res=None, *, out_dtype, name):
    t, d = x.shape
    tr = _row_tile(t)

    def kern(*refs):
        if res is None:
            x_ref, w_ref, dy_ref, dx_ref, dw_ref = refs
        else:
            x_ref, w_ref, dy_ref, r_ref, dx_ref, dw_ref = refs
        i = pl.program_id(0)
        xv = x_ref[...]
        dyv = dy_ref[...].astype(F32)
        r = lax.rsqrt(jnp.mean(xv * xv, axis=-1, keepdims=True) + EPS)
        xh = xv * r
        g = dyv * w_ref[...]
        dx = r * (g - xh * jnp.mean(g * xh, axis=-1, keepdims=True))
        if res is not None:
            dx = r_ref[...] + dx
        dx_ref[...] = dx.astype(dx_ref.dtype)
        part = jnp.sum(dyv * xh, axis=0, keepdims=True)

        @pl.when(i == 0)
        def _():
            dw_ref[...] = part

        @pl.when(i > 0)
        def _():
            dw_ref[...] += part

    row = pl.BlockSpec((tr, d), lambda i: (i, 0))
    vec = pl.BlockSpec((1, d), lambda i: (0, 0))
    ins = [x, w.reshape(1, d), dy] + ([] if res is None else [res])
    specs = [row, vec, row] + ([] if res is None else [row])
    return pl.pallas_call(
        kern, name=name, grid=(t // tr,), in_specs=specs, out_specs=[row, vec],
        out_shape=[jax.ShapeDtypeStruct((t, d), out_dtype), jax.ShapeDtypeStruct((1, d), F32)],
        compiler_params=_cp("arbitrary"),
    )(*ins)


def _conv_tile(t):
    return _pick(t, (640, 384, 256, 128))


def _conv_apply(x, tail, cw, cb, ksz):
    y = cb
    for k in range(ksz):
        y = y + cw[k:k + 1, :] * _shift_down(x, tail, ksz - 1 - k)
    return y


def dwconv_fwd(x, cw, cb, *, mode, x_off, c_out, cblk, out_dtype, name, with_t=False):
    t = x.shape[0]
    ksz = cw.shape[0]
    tb = _conv_tile(t)
    nb, ncb, t8 = t // tb, c_out // cblk, tb // 8
    xo = x_off // cblk
    nin = 2 if mode == "geglu" else 1

    def kern(*refs):
        o_ref = refs[-2] if with_t else refs[-1]
        n = pl.program_id(1)
        for c in range(cblk // BLK):
            ls = slice(c * BLK, (c + 1) * BLK)
            for s in range(tb // BLK):
                rs = slice(s * BLK, (s + 1) * BLK)
                valid = (n * tb + s * BLK + _rows((BLK, BLK))) >= PAD
                hs = []
                for q in range(nin):
                    x_ref, t_ref, w_ref, b_ref = refs[4 * q:4 * q + 4]
                    tail = (jnp.where(n > 0, t_ref[:, ls], 0.0) if s == 0
                            else x_ref[s * BLK - 8:s * BLK, ls])
                    hs.append(_conv_apply(x_ref[rs, ls], tail, w_ref[:, ls], b_ref[:, ls], ksz))
                y = _gelu(hs[0]) * hs[1] if mode == "geglu" else _silu(hs[0])
                y = jnp.where(valid, y, 0.0)
                o_ref[rs, ls] = y.astype(o_ref.dtype)
                if with_t:
                    refs[-1][ls, rs] = y.T.astype(o_ref.dtype)

    ins, specs = [], []
    for q in range(nin):
        co = xo + q * ncb
        wo = q * ncb
        ins += [x, x, cw, cb.reshape(1, -1)]
        specs += [
            pl.BlockSpec((tb, cblk), lambda j, n, co=co: (n, co + j)),
            pl.BlockSpec((8, cblk), lambda j, n, co=co: (jnp.maximum(n * t8 - 1, 0), co + j)),
            pl.BlockSpec((ksz, cblk), lambda j, n, wo=wo: (0, wo + j)),
            pl.BlockSpec((1, cblk), lambda j, n, wo=wo: (0, wo + j)),
        ]
    out_specs = pl.BlockSpec((tb, cblk), lambda j, n: (n, j))
    out_shape = jax.ShapeDtypeStruct((t, c_out), out_dtype)
    if with_t:
        out_specs = [out_specs, pl.BlockSpec((cblk, tb), lambda j, n: (j, n))]
        out_shape = [out_shape, jax.ShapeDtypeStruct((c_out, t), out_dtype)]
    return pl.pallas_call(
        kern, name=name, grid=(ncb, nb), in_specs=specs, out_specs=out_specs, out_shape=out_shape,
        compiler_params=_cp("parallel", "parallel"),
    )(*ins)


def dwconv_bwd(x, cw, cb, dy, *, mode, x_off, c_out, cblk, name):
    t = x.shape[0]
    ksz = cw.shape[0]
    tb = _conv_tile(t)
    nb, ncb, t8 = t // tb, c_out // cblk, tb // 8
    xo = x_off // cblk
    nin = 2 if mode == "geglu" else 1
    ctot = nin * c_out

    def kern(*refs):
        dy_ref = refs[4 * nin]
        outs = refs[4 * nin + 1:4 * nin + 1 + 3 * nin]
        heads = refs[4 * nin + 1 + 3 * nin:]
        n = pl.program_id(1)
        blk = nb - 1 - n

        @pl.when(n == 0)
        def _():
            for q in range(nin):
                heads[q][...] = jnp.zeros_like(heads[q])
                outs[3 * q + 1][...] = jnp.zeros_like(outs[3 * q + 1])
                outs[3 * q + 2][...] = jnp.zeros_like(outs[3 * q + 2])

        for c in range(cblk // BLK):
            ls = slice(c * BLK, (c + 1) * BLK)
            head = [heads[q][:, ls] for q in range(nin)]
            dwa = [[None] * ksz for _ in range(nin)]
            dba = [None] * nin
            for s in reversed(range(tb // BLK)):
                rs = slice(s * BLK, (s + 1) * BLK)
                valid = (blk * tb + s * BLK + _rows((BLK, BLK))) >= PAD
                xs, tails, hs = [], [], []
                for q in range(nin):
                    x_ref, t_ref, w_ref, b_ref = refs[4 * q:4 * q + 4]
                    tail = (jnp.where(blk > 0, t_ref[:, ls], 0.0) if s == 0
                            else x_ref[s * BLK - 8:s * BLK, ls])
                    xs.append(x_ref[rs, ls])
                    tails.append(tail)
                    hs.append(_conv_apply(xs[q], tail, w_ref[:, ls], b_ref[:, ls], ksz))
                dyv = dy_ref[rs, ls].astype(F32)
                if mode == "geglu":
                    dhs = [dyv * hs[1] * _gelu_grad(hs[0]), dyv * _gelu(hs[0])]
                else:
                    dhs = [dyv * _silu_grad(hs[0])]
                for q in range(nin):
                    w_ref = refs[4 * q + 2]
                    dh = jnp.where(valid, dhs[q], 0.0)
                    dx = jnp.zeros_like(dh)
                    for k in range(ksz):
                        sh = ksz - 1 - k
                        dx = dx + w_ref[k:k + 1, ls] * _shift_up(dh, head[q], sh)
                        part = jnp.sum(dh * _shift_down(xs[q], tails[q], sh), axis=0, keepdims=True)
                        dwa[q][k] = part if dwa[q][k] is None else dwa[q][k] + part
                    outs[3 * q][rs, ls] = jnp.where(valid, dx, 0.0).astype(outs[3 * q].dtype)
                    part = jnp.sum(dh, axis=0, keepdims=True)
                    dba[q] = part if dba[q] is None else dba[q] + part
                    head[q] = dh[0:8]
            for q in range(nin):
                outs[3 * q + 1][:, ls] += jnp.concatenate(dwa[q], axis=0)
                outs[3 * q + 2][:, ls] += dba[q]
                heads[q][:, ls] = head[q]

    ins, specs, out_specs, out_shape, scratch = [], [], [], [], []
    for q in range(nin):
        co = xo + q * ncb
        wo = q * ncb
        ins += [x, x, cw, cb.reshape(1, -1)]
        specs += [
            pl.BlockSpec((tb, cblk), lambda j, n, co=co: (nb - 1 - n, co + j)),
            pl.BlockSpec((8, cblk), lambda j, n, co=co: (jnp.maximum((nb - 1 - n) * t8 - 1, 0), co + j)),
            pl.BlockSpec((ksz, cblk), lambda j, n, wo=wo: (0, wo + j)),
            pl.BlockSpec((1, cblk), lambda j, n, wo=wo: (0, wo + j)),
        ]
        out_specs += [
            pl.BlockSpec((tb, cblk), lambda j, n: (nb - 1 - n, j)),
            pl.BlockSpec((ksz, cblk), lambda j, n: (0, j)),
            pl.BlockSpec((1, cblk), lambda j, n: (0, j)),
        ]
        out_shape += [jax.ShapeDtypeStruct((t, c_out), MXU),
                      jax.ShapeDtypeStruct((ksz, c_out), F32),
                      jax.ShapeDtypeStruct((1, c_out), F32)]
        scratch.append(pltpu.VMEM((8, cblk), F32))
    ins.append(dy)
    specs.append(pl.BlockSpec((tb, cblk), lambda j, n: (nb - 1 - n, j)))
    res = pl.pallas_call(
        kern, name=name, grid=(ncb, nb), in_specs=specs, out_specs=out_specs,
        out_shape=out_shape, scratch_shapes=scratch,
        compiler_params=_cp("parallel", "arbitrary"),
    )(*ins)
    dxs = [res[3 * q] for q in range(nin)]
    dcw = jnp.concatenate([res[3 * q + 1] for q in range(nin)], axis=1)
    dcb = jnp.concatenate([res[3 * q + 2] for q in range(nin)], axis=1)
    return dxs, dcw, dcb.reshape(ctot)


def _lru_tile(t):
    return _pick(t, (640, 384, 256, 128))


def _lru_gates(xc, wa, ba, wx, bx, sp):
    r = _sigmoid(_dot(xc, wa) + ba)
    i = _sigmoid(_dot(xc, wx) + bx)
    log_a = -LRU_C * r * sp
    a = jnp.exp(log_a)
    mult = jnp.sqrt(-_expm1(2.0 * log_a))
    return r, i, a, mult


def lru_fwd(proj, cw, cb, wa, ba, wx, bx, lam, *, gate_off, xr_off, name):
    t = proj.shape[0]
    tb = _lru_tile(t)
    nb, ns, t8 = t // tb, tb // BLK, tb // 8
    go, xo = gate_off // BLK, xr_off // BLK

    def kern(g_ref, x_ref, xt_ref, cw_ref, cb_ref, wa_ref, ba_ref, wx_ref, bx_ref, lam_ref,
             y_ref, yt_ref, h_ref, hc_ref):
        n = pl.program_id(1)

        @pl.when(n == 0)
        def _():
            hc_ref[...] = jnp.zeros_like(hc_ref)

        sp = _softplus(-lam_ref[...])
        hprev = hc_ref[0:1, :]
        for s in range(ns):
            sl = slice(s * BLK, (s + 1) * BLK)
            xv = x_ref[sl, :]
            tail = jnp.where(n > 0, xt_ref[...], 0.0) if s == 0 else x_ref[s * BLK - 8:s * BLK, :]
            valid = (n * tb + s * BLK + _rows((BLK, BLK))) >= PAD
            xc = jnp.where(valid, _conv_apply(xv, tail, cw_ref[...], cb_ref[...], 4), 0.0)
            _, i, a, mult = _lru_gates(xc, wa_ref[0], ba_ref[...], wx_ref[0], bx_ref[...], sp)
            u = mult * (i * xc)
            ca, cu = _scan_fwd(a, u)
            h = cu + ca * hprev
            hprev = _row_at(h, BLK - 1)
            h_ref[sl, :] = h
            y = _gelu(g_ref[sl, :]) * h
            y_ref[sl, :] = y.astype(y_ref.dtype)
            yt_ref[:, sl] = y.T.astype(yt_ref.dtype)
        hc_ref[...] = jnp.broadcast_to(hprev, hc_ref.shape)

    vec = pl.BlockSpec((1, BLK), lambda j, n: (0, j))
    mat = pl.BlockSpec((1, BLK, BLK), lambda j, n: (j, 0, 0))
    return pl.pallas_call(
        kern, name=name, grid=(D_RNN // BLK, nb),
        in_specs=[
            pl.BlockSpec((tb, BLK), lambda j, n: (n, go + j)),
            pl.BlockSpec((tb, BLK), lambda j, n: (n, xo + j)),
            pl.BlockSpec((8, BLK), lambda j, n: (jnp.maximum(n * t8 - 1, 0), xo + j)),
            pl.BlockSpec((4, BLK), lambda j, n: (0, j)), vec, mat, vec, mat, vec, vec,
        ],
        out_specs=[pl.BlockSpec((tb, BLK), lambda j, n: (n, j)),
                   pl.BlockSpec((BLK, tb), lambda j, n: (j, n)),
                   pl.BlockSpec((tb, BLK), lambda j, n: (n, j))],
        out_shape=[jax.ShapeDtypeStruct((t, D_RNN), MXU), jax.ShapeDtypeStruct((D_RNN, t), MXU),
                   jax.ShapeDtypeStruct((t, D_RNN), F32)],
        scratch_shapes=[pltpu.VMEM((8, BLK), F32)],
        compiler_params=_cp("parallel", "arbitrary"),
    )(proj, proj, proj, cw, cb.reshape(1, -1), wa, ba.reshape(1, -1), wx, bx.reshape(1, -1),
      lam.reshape(1, -1))


def lru_bwd(proj, h, dy, cw, cb, wa, ba, wx, bx, lam, *, gate_off, xr_off, dy_off, name):
    t = proj.shape[0]
    tb = _lru_tile(t)
    nb, ns, t8 = t // tb, tb // BLK, tb // 8
    go, xo, do = gate_off // BLK, xr_off // BLK, dy_off // BLK

    def kern(g_ref, x_ref, xt_ref, h_ref, ht_ref, dy_ref, cw_ref, cb_ref, wa_ref, ba_ref,
             wx_ref, bx_ref, lam_ref,
             dg_ref, dx_ref, dcw_ref, dcb_ref, dwa_ref, dba_ref, dwx_ref, dbx_ref, dlam_ref,
             gin_ref, head_ref):
        n = pl.program_id(1)
        blk = nb - 1 - n

        @pl.when(n == 0)
        def _():
            gin_ref[...] = jnp.zeros_like(gin_ref)
            head_ref[...] = jnp.zeros_like(head_ref)
            for r_ in (dcw_ref, dcb_ref, dwa_ref, dba_ref, dwx_ref, dbx_ref, dlam_ref):
                r_[...] = jnp.zeros_like(r_)

        lamv = lam_ref[...]
        sp = _softplus(-lamv)
        dsp_dlam = -_sigmoid(-lamv)
        g_in = gin_ref[0:1, :]
        head = head_ref[...]
        ones8 = jnp.ones((8, BLK), F32)
        for s in reversed(range(ns)):
            sl = slice(s * BLK, (s + 1) * BLK)
            xv = x_ref[sl, :]
            if s == 0:
                tail = jnp.where(blk > 0, xt_ref[...], 0.0)
                htail = jnp.where(blk > 0, ht_ref[...], 0.0)
            else:
                tail = x_ref[s * BLK - 8:s * BLK, :]
                htail = h_ref[s * BLK - 8:s * BLK, :]
            valid = (blk * tb + s * BLK + _rows((BLK, BLK))) >= PAD
            xc = jnp.where(valid, _conv_apply(xv, tail, cw_ref[...], cb_ref[...], 4), 0.0)
            wav, wxv = wa_ref[0], wx_ref[0]
            r, i, a, mult = _lru_gates(xc, wav, ba_ref[...], wxv, bx_ref[...], sp)
            hv = h_ref[sl, :]
            hprev = _shift_down(hv, htail, 1)
            gv = g_ref[sl, :]
            dyv = dy_ref[sl, :].astype(F32)
            dh = dyv * _gelu(gv)
            dg_ref[sl, :] = (dyv * hv * _gelu_grad(gv)).astype(dg_ref.dtype)
            c = _shift_up(a, ones8, 1)
            cc, cu = _scan_rev(c, dh)
            gg = cu + cc * g_in
            g_in = _row_at(a * gg, 0)
            da = gg * hprev
            di = gg * mult * xc
            dxc = gg * mult * i
            dmult = gg * i * xc
            dlog_a = da * a - dmult * (a * a) / mult
            dr = dlog_a * (-LRU_C * sp)
            dlam_ref[...] += jnp.sum(dlog_a * (-LRU_C) * r, axis=0, keepdims=True) * dsp_dlam
            dpr = dr * r * (1.0 - r)
            dpi = di * i * (1.0 - i)
            dxc = dxc + _dot_nt(dpr, wav) + _dot_nt(dpi, wxv)
            dxc = jnp.where(valid, dxc, 0.0)
            dpr = jnp.where(valid, dpr, 0.0)
            dpi = jnp.where(valid, dpi, 0.0)
            dwa_ref[0] += _dot_tn(xc, dpr)
            dwx_ref[0] += _dot_tn(xc, dpi)
            dba_ref[...] += jnp.sum(dpr, axis=0, keepdims=True)
            dbx_ref[...] += jnp.sum(dpi, axis=0, keepdims=True)
            dx = jnp.zeros_like(dxc)
            dws = []
            for k in range(4):
                dx = dx + cw_ref[k:k + 1, :] * _shift_up(dxc, head, 3 - k)
                dws.append(jnp.sum(dxc * _shift_down(xv, tail, 3 - k), axis=0, keepdims=True))
            dx_ref[sl, :] = jnp.where(valid, dx, 0.0).astype(dx_ref.dtype)
            dcw_ref[...] += jnp.concatenate(dws, axis=0)
            dcb_ref[...] += jnp.sum(dxc, axis=0, keepdims=True)
            head = dxc[0:8]
        gin_ref[...] = jnp.broadcast_to(g_in, gin_ref.shape)
        head_ref[...] = head

    vec = pl.BlockSpec((1, BLK), lambda j, n: (0, j))
    mat = pl.BlockSpec((1, BLK, BLK), lambda j, n: (j, 0, 0))
    cws = pl.BlockSpec((4, BLK), lambda j, n: (0, j))

    def rb(off):
        return pl.BlockSpec((tb, BLK), lambda j, n: (nb - 1 - n, off + j))

    def tl(off):
        return pl.BlockSpec((8, BLK), lambda j, n: (jnp.maximum((nb - 1 - n) * t8 - 1, 0), off + j))

    return pl.pallas_call(
        kern, name=name, grid=(D_RNN // BLK, nb),
        in_specs=[rb(go), rb(xo), tl(xo), rb(0), tl(0), rb(do), cws, vec, mat, vec, mat, vec, vec],
        out_specs=[rb(0), rb(0), cws, vec, mat, vec, mat, vec, vec],
        out_shape=[jax.ShapeDtypeStruct((t, D_RNN), MXU), jax.ShapeDtypeStruct((t, D_RNN), MXU),
                   jax.ShapeDtypeStruct((4, D_RNN), F32), jax.ShapeDtypeStruct((1, D_RNN), F32),
                   jax.ShapeDtypeStruct((8, BLK, BLK), F32), jax.ShapeDtypeStruct((1, D_RNN), F32),
                   jax.ShapeDtypeStruct((8, BLK, BLK), F32), jax.ShapeDtypeStruct((1, D_RNN), F32),
                   jax.ShapeDtypeStruct((1, D_RNN), F32)],
        scratch_shapes=[pltpu.VMEM((8, BLK), F32), pltpu.VMEM((8, BLK), F32)],
        compiler_params=_cp("parallel", "arbitrary"),
    )(proj, proj, proj, h, h, dy, cw, cb.reshape(1, -1), wa, ba.reshape(1, -1), wx,
      bx.reshape(1, -1), lam.reshape(1, -1))


_SCALE = HEAD_DIM ** -0.5


STK = 4


def _attn_masks(n):
    qi = _rows((STK * BLK, 3 * BLK)) & (BLK - 1)
    c = _lanes((STK * BLK, 3 * BLK))
    tq = n * BLK + qi - PAD
    s_band = (n - 1) * BLK + c - PAD
    d_band = tq - s_band
    ok_band = (s_band >= N_META) & (d_band >= 0) & (d_band < BLK)
    jm = c - 2 * BLK
    d_meta = tq - (jm - PAD)
    ok_meta = (jm >= PAD) & (d_meta >= 0)
    is_band = c < 2 * BLK
    ok = (is_band & ok_band) | (jnp.logical_not(is_band) & ok_meta)
    dist = jnp.where(is_band, d_band, jnp.minimum(d_meta, BLK)).astype(F32)
    return ok, dist


def _stack_heads(g, e):
    return [8 * g + 2 * i + e for i in range(STK)]


def _stack_cols(heads, sk):
    slope = jnp.concatenate(
        [jnp.full((BLK, 1), 2.0 ** (-8.0 * (h + 1) / N_Q_HEADS), F32) for h in heads], axis=0)
    sink = jnp.concatenate(
        [jnp.broadcast_to(jnp.sum(jnp.where(_lanes(sk.shape) == h, sk, 0.0), axis=1, keepdims=True),
                          (BLK, 1)) for h in heads], axis=0)
    return slope, sink


def _stack_tiles(ref, g, sel):
    return jnp.concatenate(
        [jnp.where(sel, ref[:, (4 * g + i) * BLK:(4 * g + i + 1) * BLK].astype(F32), 0.0)
         for i in range(STK)], axis=0)


def _attn_probs(qm, kk, ok, dist, slope, sink):
    s = _dot_nt(qm, kk) * _SCALE - slope * dist
    s = jnp.where(ok, s, NEG)
    mx = jnp.maximum(jnp.max(s, axis=-1, keepdims=True), sink)
    p = jnp.exp(s - mx)
    es = jnp.exp(sink - mx)
    inv = 1.0 / (jnp.sum(p, axis=-1, keepdims=True) + es)
    return p * inv, es * inv


def _attn_specs(t, q_off, k_off, v_off, rev):
    nb = t // BLK
    qo, ko, vo = q_off // 1024, k_off // BLK, v_off // BLK

    def b(n):
        return nb - 1 - n if rev else n

    return [
        pl.BlockSpec((BLK, 1024), lambda n: (b(n), qo)),
        pl.BlockSpec((BLK, BLK), lambda n: (b(n), ko)),
        pl.BlockSpec((BLK, BLK), lambda n: (b(n), vo)),
        pl.BlockSpec((BLK, BLK), lambda n: (jnp.maximum(b(n) - 1, 0), ko)),
        pl.BlockSpec((BLK, BLK), lambda n: (jnp.maximum(b(n) - 1, 0), vo)),
        pl.BlockSpec((BLK, BLK), lambda n: (0, ko)),
        pl.BlockSpec((BLK, BLK), lambda n: (0, vo)),
        pl.BlockSpec((1, BLK), lambda n: (0, 0)),
    ]


def attn_fwd(proj, sinks, *, q_off, k_off, v_off, name):
    t = proj.shape[0]
    nb = t // BLK

    def kern(q_ref, kc_ref, vc_ref, kp_ref, vp_ref, km_ref, vm_ref, sk_ref, o_ref, ot_ref):
        n = pl.program_id(0)
        ok, dist = _attn_masks(n)
        k_all = jnp.concatenate([kp_ref[...], kc_ref[...], km_ref[...]], axis=0)
        v_all = jnp.concatenate([vp_ref[...], vc_ref[...], vm_ref[...]], axis=0)
        k_alt = pltpu.roll(k_all, HEAD_DIM, 1)
        v_alt = pltpu.roll(v_all, HEAD_DIM, 1)
        low = _lanes((BLK, BLK)) < HEAD_DIM
        outs = {}
        for g in range(2):
            for e in range(2):
                qm = _stack_tiles(q_ref, g, low == (e == 0))
                kk = k_all if g == e else k_alt
                vv = v_all if g == e else v_alt
                slope, sink = _stack_cols(_stack_heads(g, e), sk_ref[...])
                p, _ = _attn_probs(qm, kk, ok, dist, slope, sink)
                outs[g, e] = _dot(p, vv)
        for hp in range(N_Q_HEADS // 2):
            g, rs = hp // STK, slice((hp % STK) * BLK, (hp % STK + 1) * BLK)
            o = jnp.where(low, outs[g, 0][rs], outs[g, 1][rs])
            o_ref[:, hp * BLK:(hp + 1) * BLK] = o.astype(o_ref.dtype)
            ot_ref[hp * BLK:(hp + 1) * BLK, :] = o.T.astype(ot_ref.dtype)

    sk = jnp.zeros((1, BLK), F32).at[0, :N_Q_HEADS].set(sinks)
    return pl.pallas_call(
        kern, name=name, grid=(nb,),
        in_specs=_attn_specs(t, q_off, k_off, v_off, False),
        out_specs=[pl.BlockSpec((BLK, 1024), lambda n: (n, 0)), pl.BlockSpec((1024, BLK), lambda n: (0, n))],
        out_shape=[jax.ShapeDtypeStruct((t, 1024), MXU), jax.ShapeDtypeStruct((1024, t), MXU)],
        compiler_params=_cp("parallel"),
    )(proj, proj, proj, proj, proj, proj, proj, sk)


def attn_bwd(proj, sinks, dy, *, q_off, k_off, v_off, dy_off, name):
    t = proj.shape[0]
    nb = t // BLK
    do = dy_off // 1024

    def kern(q_ref, kc_ref, vc_ref, kp_ref, vp_ref, km_ref, vm_ref, sk_ref, do_ref,
             dq_ref, dk_ref, dv_ref, dsk_ref, ck_ref, cv_ref, mk_ref, mv_ref):
        n = pl.program_id(0)
        blk = nb - 1 - n

        @pl.when(n == 0)
        def _():
            for r_ in (ck_ref, cv_ref, mk_ref, mv_ref, dsk_ref):
                r_[...] = jnp.zeros_like(r_)

        ok, dist = _attn_masks(blk)
        k_all = jnp.concatenate([kp_ref[...], kc_ref[...], km_ref[...]], axis=0)
        v_all = jnp.concatenate([vp_ref[...], vc_ref[...], vm_ref[...]], axis=0)
        k_alt = pltpu.roll(k_all, HEAD_DIM, 1)
        v_alt = pltpu.roll(v_all, HEAD_DIM, 1)
        low = _lanes((BLK, BLK)) < HEAD_DIM
        lane1 = _lanes((1, BLK))
        dk_all = jnp.zeros((3 * BLK, BLK), F32)
        dv_all = jnp.zeros((3 * BLK, BLK), F32)
        dsk = jnp.zeros((1, BLK), F32)
        dqs = {}
        for g in range(2):
            for e in range(2):
                sel = low == (e == 0)
                heads = _stack_heads(g, e)
                qm = _stack_tiles(q_ref, g, sel)
                dom = _stack_tiles(do_ref, g, sel)
                kk = k_all if g == e else k_alt
                vv = v_all if g == e else v_alt
                slope, sink = _stack_cols(heads, sk_ref[...])
                p, psink = _attn_probs(qm, kk, ok, dist, slope, sink)
                dp = _dot_nt(dom, vv)
                delta = jnp.sum(p * dp, axis=-1, keepdims=True)
                ds = p * (dp - delta) * _SCALE
                psd = psink * delta
                for i, h in enumerate(heads):
                    dsk = dsk + jnp.where(lane1 == h, -jnp.sum(psd[i * BLK:(i + 1) * BLK], axis=0, keepdims=True), 0.0)
                dqs[g, e] = _dot(ds, kk)
                dkh = _dot_tn(ds, qm)
                dvh = _dot_tn(p, dom)
                if g != e:
                    dkh = pltpu.roll(dkh, HEAD_DIM, 1)
                    dvh = pltpu.roll(dvh, HEAD_DIM, 1)
                dk_all = dk_all + dkh
                dv_all = dv_all + dvh
        for hp in range(N_Q_HEADS // 2):
            g, rs = hp // STK, slice((hp % STK) * BLK, (hp % STK + 1) * BLK)
            dq_ref[:, hp * BLK:(hp + 1) * BLK] = jnp.where(low, dqs[g, 0][rs], dqs[g, 1][rs]).astype(dq_ref.dtype)
        dsk_ref[...] += dsk
        mk_ref[...] += dk_all[2 * BLK:3 * BLK]
        mv_ref[...] += dv_all[2 * BLK:3 * BLK]
        is0 = blk == 0
        dk_ref[...] = (dk_all[BLK:2 * BLK] + ck_ref[...] + jnp.where(is0, mk_ref[...], 0.0)).astype(dk_ref.dtype)
        dv_ref[...] = (dv_all[BLK:2 * BLK] + cv_ref[...] + jnp.where(is0, mv_ref[...], 0.0)).astype(dv_ref.dtype)
        ck_ref[...] = dk_all[0:BLK]
        cv_ref[...] = dv_all[0:BLK]

    sk = jnp.zeros((1, BLK), F32).at[0, :N_Q_HEADS].set(sinks)
    kv = pl.BlockSpec((BLK, BLK), lambda n: (nb - 1 - n, 0))
    res = pl.pallas_call(
        kern, name=name, grid=(nb,),
        in_specs=_attn_specs(t, q_off, k_off, v_off, True)
        + [pl.BlockSpec((BLK, 1024), lambda n: (nb - 1 - n, do))],
        out_specs=[pl.BlockSpec((BLK, 1024), lambda n: (nb - 1 - n, 0)), kv, kv,
                   pl.BlockSpec((1, BLK), lambda n: (0, 0))],
        out_shape=[jax.ShapeDtypeStruct((t, 1024), MXU), jax.ShapeDtypeStruct((t, BLK), MXU),
                   jax.ShapeDtypeStruct((t, BLK), MXU), jax.ShapeDtypeStruct((1, BLK), F32)],
        scratch_shapes=[pltpu.VMEM((BLK, BLK), F32)] * 4,
        compiler_params=_cp("arbitrary"),
    )(proj, proj, proj, proj, proj, proj, proj, sk, dy)
    return res[0], res[1], res[2], res[3][0, :N_Q_HEADS]


GW = D_SSM // SSD_GROUPS
EXP_ROWS = 3 * BLK + 8
RED_ROWS = EXP_ROWS + 8


def _head_expand():
    ch = jnp.arange(D_SSM) // HEAD_DIM
    return (jnp.arange(BLK)[:, None] == ch[None, :]).astype(BF16)


def _ssd_decay(raw, dtb, alog, rowv):
    valid = rowv & (_lanes((BLK, BLK)) < SSD_HEADS)
    pre = raw + dtb
    dtp = jnp.where(valid, _softplus(pre), 0.0)
    av = -jnp.exp(alog)
    cs = _cumsum_fwd(dtp * av)
    cs_last = _row_at(cs, BLK - 1)
    return valid, pre, dtp, av, cs, jnp.exp(cs), jnp.exp(cs_last - cs), jnp.exp(cs_last)


def _head_col(x, h):
    return jnp.sum(jnp.where(_lanes(x.shape) == h, x, 0.0), axis=1, keepdims=True)


def _ssd_group_fwd(g, xdt, cs, cst, bg, cg, tril, low):
    cb = _dot_nt(cg, bg)
    ys, lm = [], []
    for j in range(2):
        xp = xdt[:, g * GW + j * BLK:g * GW + (j + 1) * BLK]
        hv = []
        for e in range(2):
            h = 4 * g + 2 * j + e
            seg = _head_col(cs, h) - _row_at(cst, h)
            lmat = jnp.where(tril, jnp.exp(jnp.minimum(seg, 0.0)), 0.0)
            mmat = cb * lmat
            lm.append((lmat, mmat))
            hv.append(_dot(mmat, xp))
        ys.append(jnp.where(low, hv[0], hv[1]))
    return jnp.concatenate(ys, axis=1), lm


def _ssd_specs(t, z_off, dt_off, rev):
    nb = t // BLK
    zo, dto = z_off // D_SSM, dt_off // BLK

    def b(n):
        return nb - 1 - n if rev else n

    vec = lambda w: pl.BlockSpec((1, w), lambda n: (0, 0))
    return [
        pl.BlockSpec((BLK, D_SSM), lambda n: (b(n), 0)),
        pl.BlockSpec((BLK, 1024), lambda n: (b(n), 2)),
        pl.BlockSpec((BLK, 1024), lambda n: (b(n), 3)),
        pl.BlockSpec((BLK, D_SSM), lambda n: (b(n), zo)),
        pl.BlockSpec((BLK, BLK), lambda n: (b(n), dto)),
        vec(BLK), vec(BLK), vec(D_SSM), vec(D_SSM),
        pl.BlockSpec((BLK, D_SSM), lambda n: (0, 0)),
    ]


def _pad128(v):
    return jnp.zeros((1, BLK), F32).at[0, :v.shape[0]].set(v)


def ssd_fwd(xbc, proj, dt_bias, a_log, d_skip, gate_norm, *, z_off, dt_off, name):
    t = xbc.shape[0]
    nb = t // BLK

    def kern(x_ref, b_ref, c_ref, z_ref, dt_ref, dtb_ref, alog_ref, dsk_ref, gn_ref, e_ref,
             yn_ref, ynt_ref, st_ref, p_ref):
        n = pl.program_id(0)

        @pl.when(n == 0)
        def _():
            p_ref[...] = jnp.zeros_like(p_ref)

        rowv = (n * BLK + _rows((BLK, BLK))) >= PAD
        _, _, dtp, _, cs, ecs, w, dec = _ssd_decay(dt_ref[...], dtb_ref[...], alog_ref[...], rowv)
        ex = _dot_split(jnp.concatenate([dtp, ecs, w, jnp.broadcast_to(dec, (8, BLK))], axis=0),
                        e_ref[...])
        dtp_c, ecs_c, w_c = ex[0:BLK], ex[BLK:2 * BLK], ex[2 * BLK:3 * BLK]
        dec_c = jnp.max(ex[3 * BLK:EXP_ROWS], axis=0, keepdims=True)
        xv = x_ref[...]
        xdt = xv * dtp_c
        wx = w_c * xdt
        cst = cs.T
        tril = _rows((BLK, BLK)) >= _lanes((BLK, BLK))
        low = _lanes((BLK, BLK)) < HEAD_DIM
        st_ref[0] = p_ref[...]
        for g in range(SSD_GROUPS):
            gs = slice(g * GW, (g + 1) * GW)
            bg = b_ref[:, g * BLK:(g + 1) * BLK]
            cg = c_ref[:, g * BLK:(g + 1) * BLK]
            pg = p_ref[g]
            ydiag, _ = _ssd_group_fwd(g, xdt, cs, cst, bg, cg, tril, low)
            y = ydiag + _dot(cg, pg) * ecs_c[:, gs] + dsk_ref[:, gs] * xv[:, gs]
            p_ref[g] = pg * dec_c[:, gs] + _dot_tn(bg, wx[:, gs])
            yz = y * _silu(z_ref[:, gs])
            r = lax.rsqrt(jnp.mean(yz * yz, axis=-1, keepdims=True) + EPS)
            yn = yz * r * gn_ref[:, gs]
            yn_ref[:, gs] = yn.astype(yn_ref.dtype)
            ynt_ref[gs, :] = yn.T.astype(ynt_ref.dtype)

    return pl.pallas_call(
        kern, name=name, grid=(nb,),
        in_specs=_ssd_specs(t, z_off, dt_off, False),
        out_specs=[pl.BlockSpec((BLK, D_SSM), lambda n: (n, 0)),
                   pl.BlockSpec((D_SSM, BLK), lambda n: (0, n)),
                   pl.BlockSpec((1, SSD_GROUPS, BLK, GW), lambda n: (n, 0, 0, 0))],
        out_shape=[jax.ShapeDtypeStruct((t, D_SSM), MXU), jax.ShapeDtypeStruct((D_SSM, t), MXU),
                   jax.ShapeDtypeStruct((nb, SSD_GROUPS, BLK, GW), F32)],
        scratch_shapes=[pltpu.VMEM((SSD_GROUPS, BLK, GW), F32)],
        compiler_params=_cp("arbitrary"),
    )(xbc, xbc, xbc, proj, proj, _pad128(dt_bias), _pad128(a_log),
      jnp.repeat(d_skip, HEAD_DIM).reshape(1, D_SSM), gate_norm.reshape(1, D_SSM), _head_expand())


def ssd_bwd(xbc, proj, st, dyn, dt_bias, a_log, d_skip, gate_norm, *, z_off, dt_off, name):
    t = xbc.shape[0]
    nb = t // BLK

    def kern(x_ref, b_ref, c_ref, z_ref, dt_ref, dtb_ref, alog_ref, dsk_ref, gn_ref, e_ref,
             et_ref, st_ref, dyn_ref,
             dxbc_ref, dz_ref, draw_ref, dgn_ref, ddsk_ref, ddtb_ref, dalog_ref,
             dp_ref, tr_ref):
        n = pl.program_id(0)
        blk = nb - 1 - n

        @pl.when(n == 0)
        def _():
            for r_ in (dp_ref, dgn_ref, ddsk_ref, ddtb_ref, dalog_ref):
                r_[...] = jnp.zeros_like(r_)

        rowv = (blk * BLK + _rows((BLK, BLK))) >= PAD
        valid, pre, dtp, av, cs, ecs, w, dec = _ssd_decay(dt_ref[...], dtb_ref[...], alog_ref[...], rowv)
        ex = _dot_split(jnp.concatenate([dtp, ecs, w, jnp.broadcast_to(dec, (8, BLK))], axis=0),
                        e_ref[...])
        dtp_c, ecs_c, w_c = ex[0:BLK], ex[BLK:2 * BLK], ex[2 * BLK:3 * BLK]
        dec_c = jnp.max(ex[3 * BLK:EXP_ROWS], axis=0, keepdims=True)
        xv = x_ref[...]
        xdt = xv * dtp_c
        wx = w_c * xdt
        cst = cs.T
        tril = _rows((BLK, BLK)) >= _lanes((BLK, BLK))
        lane = _lanes((BLK, BLK))
        rowi = _rows((BLK, BLK))
        low = lane < HEAD_DIM
        dcs = jnp.zeros((BLK, BLK), F32)
        dcst = jnp.zeros((BLK, BLK), F32)
        for g in range(SSD_GROUPS):
            gs = slice(g * GW, (g + 1) * GW)
            bg = b_ref[:, g * BLK:(g + 1) * BLK]
            cg = c_ref[:, g * BLK:(g + 1) * BLK]
            pg = st_ref[0, g]
            dpn = dp_ref[g]
            xg = xv[:, gs]
            ydiag, lm = _ssd_group_fwd(g, xdt, cs, cst, bg, cg, tril, low)
            yoff = _dot(cg, pg) * ecs_c[:, gs]
            y = ydiag + yoff + dsk_ref[:, gs] * xg
            zz = z_ref[:, gs]
            sz = _silu(zz)
            yz = y * sz
            r = lax.rsqrt(jnp.mean(yz * yz, axis=-1, keepdims=True) + EPS)
            yhat = yz * r
            dynv = dyn_ref[:, gs].astype(F32)
            gy = dynv * gn_ref[:, gs]
            dgn_ref[:, gs] += jnp.sum(dynv * yhat, axis=0, keepdims=True)
            dyz = r * (gy - yhat * jnp.mean(gy * yhat, axis=-1, keepdims=True))
            dy = dyz * sz
            dz_ref[:, gs] = (dyz * y * _silu_grad(zz)).astype(dz_ref.dtype)
            tr_ref[EXP_ROWS:RED_ROWS, gs] = jnp.broadcast_to(
                jnp.sum(dy * xg, axis=0, keepdims=True), (8, GW))
            dx = dsk_ref[:, gs] * dy
            dwx = _dot(bg, dpn)
            dxdt = w_c[:, gs] * dwx
            tr_ref[0:BLK, gs] = dwx * wx[:, gs]
            dbg = _dot_nt(wx[:, gs], dpn)
            dzo = ecs_c[:, gs] * dy
            tr_ref[BLK:2 * BLK, gs] = dy * yoff
            dcg = _dot_nt(dzo, pg)
            dp_ref[g] = dec_c[:, gs] * dpn + _dot_tn(cg, dzo)
            tr_ref[3 * BLK:EXP_ROWS, gs] = jnp.broadcast_to(
                jnp.sum(dpn * pg, axis=0, keepdims=True), (8, GW))
            dcb = jnp.zeros((BLK, BLK), F32)
            pairs = []
            for j in range(2):
                ps = slice(g * GW + j * BLK, g * GW + (j + 1) * BLK)
                xp = xdt[:, ps]
                dyp = dy[:, j * BLK:(j + 1) * BLK]
                acc = jnp.zeros((BLK, BLK), F32)
                for e in range(2):
                    h = 4 * g + 2 * j + e
                    lmat, mmat = lm[2 * j + e]
                    dyh = jnp.where(low == (e == 0), dyp, 0.0)
                    dm = jnp.where(tril, _dot_nt(dyh, xp), 0.0)
                    nh = dm * mmat
                    dcs = dcs + jnp.where(lane == h, jnp.sum(nh, axis=1, keepdims=True), 0.0)
                    dcst = dcst - jnp.where(rowi == h, jnp.sum(nh, axis=0, keepdims=True), 0.0)
                    dcb = dcb + dm * lmat
                    acc = acc + _dot_tn(mmat, dyh)
                pairs.append(acc)
            dxdt = dxdt + jnp.concatenate(pairs, axis=1)
            dcg = dcg + _dot(dcb, bg)
            dbg = dbg + _dot_tn(dcb, cg)
            tr_ref[2 * BLK:3 * BLK, gs] = dxdt * xg
            dxbc_ref[:, gs] = dx + dxdt * dtp_c[:, gs]
            dxbc_ref[:, D_SSM + g * BLK:D_SSM + (g + 1) * BLK] = dbg
            dxbc_ref[:, D_SSM + 1024 + g * BLK:D_SSM + 1024 + (g + 1) * BLK] = dcg
        red = _dot(tr_ref[...], et_ref[...])
        r1, r2, r3 = red[0:BLK], red[BLK:2 * BLK], red[2 * BLK:3 * BLK]
        ddec = jnp.max(red[3 * BLK:EXP_ROWS], axis=0, keepdims=True)
        ddsk_ref[...] += jnp.max(red[EXP_ROWS:RED_ROWS], axis=0, keepdims=True)
        dcs = dcs + dcst.T - r1 + r2
        dcs_last = jnp.sum(r1, axis=0, keepdims=True) + ddec * dec
        dcs = dcs + jnp.where(rowi == BLK - 1, dcs_last, 0.0)
        dda = _cumsum_rev(dcs)
        ddtp = r3 + dda * av
        dalog_ref[...] += jnp.sum(dda * dtp, axis=0, keepdims=True) * av
        draw = jnp.where(valid, ddtp * _sigmoid(pre), 0.0)
        ddtb_ref[...] += jnp.sum(draw, axis=0, keepdims=True)
        draw_ref[...] = draw.astype(draw_ref.dtype)

    vec = lambda w_: pl.BlockSpec((1, w_), lambda n: (0, 0))
    rb = lambda w_: pl.BlockSpec((BLK, w_), lambda n: (nb - 1 - n, 0))
    e = _head_expand()
    res = pl.pallas_call(
        kern, name=name, grid=(nb,),
        in_specs=_ssd_specs(t, z_off, dt_off, True)
        + [pl.BlockSpec((D_SSM, BLK), lambda n: (0, 0)),
           pl.BlockSpec((1, SSD_GROUPS, BLK, GW), lambda n: (nb - 1 - n, 0, 0, 0)),
           rb(D_SSM)],
        out_specs=[rb(2 * D_SSM), rb(D_SSM), rb(BLK), vec(D_SSM), vec(BLK), vec(BLK), vec(BLK)],
        out_shape=[jax.ShapeDtypeStruct((t, 2 * D_SSM), F32), jax.ShapeDtypeStruct((t, D_SSM), MXU),
                   jax.ShapeDtypeStruct((t, BLK), MXU), jax.ShapeDtypeStruct((1, D_SSM), F32),
                   jax.ShapeDtypeStruct((1, BLK), F32), jax.ShapeDtypeStruct((1, BLK), F32),
                   jax.ShapeDtypeStruct((1, BLK), F32)],
        scratch_shapes=[pltpu.VMEM((SSD_GROUPS, BLK, GW), F32), pltpu.VMEM((RED_ROWS, D_SSM), F32)],
        compiler_params=_cp("arbitrary"),
    )(xbc, xbc, xbc, proj, proj, _pad128(dt_bias), _pad128(a_log),
      jnp.repeat(d_skip, HEAD_DIM).reshape(1, D_SSM), gate_norm.reshape(1, D_SSM), e, e.T, st, dyn)
    dxbc, dz, draw, dgn, ddsk, ddtb, dalog = res
    return dxbc, dz, draw, dgn[0], ddsk[0, :SSD_HEADS], ddtb[0, :SSD_HEADS], dalog[0, :SSD_HEADS]


def loss_fwd_bwd(h, target, *, name):
    t, d = h.shape
    nb = t // BLK

    def kern(h_ref, t_ref, loss_ref, dh_ref):
        n = pl.program_id(0)
        err = jnp.where(n > 0, h_ref[...] - t_ref[...], 0.0)
        dh_ref[...] = err * (1.0 / d)
        part = (0.5 / d) * jnp.sum(jnp.sum(err * err, axis=1, keepdims=True), axis=0, keepdims=True)

        @pl.when(n == 0)
        def _():
            loss_ref[...] = part

        @pl.when(n > 0)
        def _():
            loss_ref[...] += part

    return pl.pallas_call(
        kern, name=name, grid=(nb,),
        in_specs=[pl.BlockSpec((BLK, d), lambda n: (n, 0)),
                  pl.BlockSpec((BLK, d), lambda n: (jnp.maximum(n - 1, 0), 0))],
        out_specs=[pl.BlockSpec((1, 1), lambda n: (0, 0)), pl.BlockSpec((BLK, d), lambda n: (n, 0))],
        out_shape=[jax.ShapeDtypeStruct((1, 1), F32), jax.ShapeDtypeStruct((t, d), F32)],
        compiler_params=_cp("arbitrary"),
    )(h, target)


def _ew_tile(r, c):
    cap = max(16, (256 * 1024) // c)
    best = None
    for tr in range(16, min(r, cap) + 1, 16):
        if r % tr == 0:
            best = tr
    return best if best is not None else r


def adamw(parts, w, m, v, *, name):
    npart, r, c = parts.shape
    tr = _ew_tile(r, c)

    def kern(p_ref, w_ref, m_ref, v_ref, g_ref, d_ref, m2_ref, v2_ref):
        g = p_ref[0].astype(F32)
        for k in range(1, npart):
            g = g + p_ref[k].astype(F32)
        m2 = ADAM_B1 * m_ref[...] + (1.0 - ADAM_B1) * g
        v2 = ADAM_B2 * v_ref[...] + (1.0 - ADAM_B2) * (g * g)
        m_hat = m2 / (1.0 - ADAM_B1 ** ADAM_STEP)
        v_hat = v2 / (1.0 - ADAM_B2 ** ADAM_STEP)
        g_ref[...] = g
        d_ref[...] = -ADAM_LR * (m_hat / (jnp.sqrt(v_hat) + ADAM_EPS) + ADAM_WD * w_ref[...])
        m2_ref[...] = m2
        v2_ref[...] = v2

    row = pl.BlockSpec((tr, c), lambda i: (i, 0))
    sds = jax.ShapeDtypeStruct((r, c), F32)
    return pl.pallas_call(
        kern, name=name, grid=(r // tr,),
        in_specs=[pl.BlockSpec((npart, tr, c), lambda i: (0, i, 0)), row, row, row],
        out_specs=[row, row, row, row], out_shape=[sds, sds, sds, sds],
        compiler_params=_cp("parallel"),
    )(parts, w, m, v)


def pair_add(p, land, *, name):
    _, r, c = p.shape
    tr = _ew_tile(r, c)
    core = lax.axis_index("c").astype(jnp.int32).reshape(1)

    def kern(c_ref, p_ref, l_ref, o_ref):
        o_ref[...] = (p_ref[...] + l_ref[...]).astype(o_ref.dtype)

    return pl.pallas_call(
        kern, name=name,
        grid_spec=pltpu.PrefetchScalarGridSpec(
            num_scalar_prefetch=1, grid=(4, r // tr),
            in_specs=[pl.BlockSpec((1, tr, c), lambda k, i, c_ref: (2 * k + c_ref[0], i, 0)),
                      pl.BlockSpec((1, tr, c), lambda k, i, c_ref: (k, i, 0))],
            out_specs=pl.BlockSpec((1, tr, c), lambda k, i, c_ref: (k, i, 0))),
        out_shape=jax.ShapeDtypeStruct((4, r, c), BF16),
        compiler_params=_cp("parallel", "parallel"),
    )(core, p, land)


def _me():
    return lax.axis_index("x"), lax.axis_index("y"), lax.axis_index("c")


def all_gather(xs, *, name):
    n = len(xs)

    def body(*refs):
        x_refs, out_refs = refs[:n], refs[n:2 * n]
        send_sems, recv_sems, local_sems = refs[2 * n:]
        mx, my, mc = _me()
        me, sib = (mx, my, mc), (mx, my, 1 - mc)
        chips = [(1 - mx, my), (mx, 1 - my), (1 - mx, 1 - my)]

        def rows(i, px, py, pc):
            return out_refs[i].at[4 * px + 2 * py + pc]

        def copy(i, k, block, to, src=None):
            return pltpu.make_async_remote_copy(
                src_ref=rows(i, *block) if src is None else src, dst_ref=rows(i, *block),
                send_sem=send_sems.at[7 * i + k], recv_sem=recv_sems.at[7 * i + k],
                device_id=to, device_id_type=MESH)

        mine = [pltpu.make_async_copy(x_refs[i], rows(i, *me), local_sems.at[i]) for i in range(n)]
        first = []
        for i in range(n):
            mine[i].start()
            first.append(copy(i, 0, me, sib, src=x_refs[i]))
            first += [copy(i, 1 + j, me, (*chip, mc), src=x_refs[i]) for j, chip in enumerate(chips)]
        for cp in first:
            cp.start()
        passed = []
        for i in range(n):
            for j, chip in enumerate(chips):
                copy(i, 1 + j, (*chip, mc), me).wait_recv()
                passed.append(copy(i, 4 + j, (*chip, mc), sib))
                passed[-1].start()
        for i in range(n):
            copy(i, 0, sib, me).wait_recv()
            for j, chip in enumerate(chips):
                copy(i, 4 + j, (*chip, 1 - mc), me).wait_recv()
        for cp in first + passed:
            cp.wait_send()
        for cp in mine:
            cp.wait()

    return pl.pallas_call(
        body, name=name,
        out_shape=[jax.ShapeDtypeStruct((N_DEV,) + x.shape, x.dtype) for x in xs],
        in_specs=[ANY] * n, out_specs=[ANY] * n,
        scratch_shapes=[pltpu.SemaphoreType.DMA((7 * n,)), pltpu.SemaphoreType.DMA((7 * n,)),
                        pltpu.SemaphoreType.DMA((n,))],
    )(*xs)


def pair_exchange(ps, *, name):
    n = len(ps)

    def body(*refs):
        p_refs, out_refs = refs[:n], refs[n:2 * n]
        send_sems, recv_sems = refs[2 * n:]
        mx, my, mc = _me()
        cps = [pltpu.make_async_remote_copy(
            src_ref=p_refs[i].at[2 * k + (1 - mc)], dst_ref=out_refs[i].at[k],
            send_sem=send_sems.at[4 * i + k], recv_sem=recv_sems.at[4 * i + k],
            device_id=(mx, my, 1 - mc), device_id_type=MESH) for i in range(n) for k in range(4)]
        for cp in cps:
            cp.start()
        for cp in cps:
            cp.wait_recv()
        for cp in cps:
            cp.wait_send()

    return pl.pallas_call(
        body, name=name,
        out_shape=[jax.ShapeDtypeStruct((4,) + p.shape[1:], p.dtype) for p in ps],
        in_specs=[ANY] * n, out_specs=[ANY] * n,
        scratch_shapes=[pltpu.SemaphoreType.DMA((4 * n,)), pltpu.SemaphoreType.DMA((4 * n,))],
    )(*ps)


def chip_exchange(qs, *, name):
    n = len(qs)

    def body(*refs):
        q_refs, out_refs = refs[:n], refs[n:2 * n]
        send_sems, recv_sems, local_sems = refs[2 * n:]
        mx, my, mc = _me()
        mine = 2 * mx + my
        chips = [(1 - mx, my), (mx, 1 - my), (1 - mx, 1 - my)]
        local, sends, recvs = [], [], []
        for i in range(n):
            local.append(pltpu.make_async_copy(q_refs[i].at[mine], out_refs[i].at[mine], local_sems.at[i]))
            for k, (px, py) in enumerate(chips):
                sems = dict(send_sem=send_sems.at[3 * i + k], recv_sem=recv_sems.at[3 * i + k],
                            device_id=(px, py, mc), device_id_type=MESH)
                sends.append(pltpu.make_async_remote_copy(
                    src_ref=q_refs[i].at[2 * px + py], dst_ref=out_refs[i].at[mine], **sems))
                recvs.append(pltpu.make_async_remote_copy(
                    src_ref=q_refs[i].at[mine], dst_ref=out_refs[i].at[2 * px + py], **sems))
        for cp in local + sends:
            cp.start()
        for cp in recvs:
            cp.wait_recv()
        for cp in sends:
            cp.wait_send()
        for cp in local:
            cp.wait()

    return pl.pallas_call(
        body, name=name,
        out_shape=[jax.ShapeDtypeStruct(q.shape, q.dtype) for q in qs],
        in_specs=[ANY] * n, out_specs=[ANY] * n,
        scratch_shapes=[pltpu.SemaphoreType.DMA((3 * n,)), pltpu.SemaphoreType.DMA((3 * n,)),
                        pltpu.SemaphoreType.DMA((n,))],
    )(*qs)


WEIGHTS = [
    "meta_tokens", "l0_mix_pre_norm", "l0_mix_post_norm", "l0_w_in", "l0_lru_conv_w", "l0_lru_conv_b",
    "l0_lru_w_a", "l0_lru_b_a", "l0_lru_w_x", "l0_lru_b_x", "l0_lru_lambda", "l0_attn_sinks", "l0_w_out",
    "l0_ffn_pre_norm", "l0_ffn_post_norm", "l0_ffn_w_up", "l0_ffn_conv_w", "l0_ffn_conv_b", "l0_ffn_w_down",
    "l1_mix_pre_norm", "l1_mix_post_norm", "l1_w_in", "l1_ssm_conv_w", "l1_ssm_conv_b", "l1_dt_bias",
    "l1_a_log", "l1_d_skip", "l1_gate_norm", "l1_w_out", "l1_ffn_pre_norm", "l1_ffn_post_norm",
    "l1_ffn_w_up", "l1_ffn_conv_w", "l1_ffn_conv_b", "l1_ffn_w_down",
]
INPUTS = (["x"] + WEIGHTS + ["loss_target"] + ["m_" + n for n in WEIGHTS] + ["v_" + n for n in WEIGHTS])

MATS = {"l0_w_in": ("col", (1024, 3328)), "l0_w_out": ("row", (2048, 1024)),
        "l0_ffn_w_up": ("col", (1024, 5632)), "l0_ffn_w_down": ("row", (2816, 1024)),
        "l1_w_in": ("col", (1024, 6176)), "l1_w_out": ("row", (2048, 1024)),
        "l1_ffn_w_up": ("col", (1024, 5632)), "l1_ffn_w_down": ("row", (2816, 1024))}
SMALL_SHARDED = {"meta_tokens": ("col", (16, 1024)), "l0_lru_conv_w": ("col", (4, 1024)),
                 "l0_ffn_conv_w": ("col", (3, 5632)), "l1_ssm_conv_w": ("col", (4, 4096)),
                 "l1_ffn_conv_w": ("col", (3, 5632))}
SHARDED = {**MATS, **SMALL_SHARDED}
REPLICATED = [n for n in WEIGHTS if n not in SHARDED]
PACK_W = 1024
SMALL_W = 128


def _shard_shape(name):
    kind, (r, c) = SHARDED[name]
    return (r, c // N_DEV) if kind == "col" else (r // N_DEV, c)


def _rows_of(numel, width):
    return -(-numel // width)


def _to_rows(a, width):
    flat = a.reshape(-1)
    rows = _rows_of(flat.shape[0], width)
    return jnp.pad(flat, (0, rows * width - flat.shape[0])).reshape(rows, width)


def _pack(arrs, width, total_rows):
    slab = jnp.concatenate([_to_rows(a, width) for a in arrs], axis=0)
    return jnp.pad(slab, ((0, total_rows - slab.shape[0]), (0, 0)))


def _unpack(slab, shapes, width):
    out, off = [], 0
    for shp in shapes:
        numel = math.prod(shp)
        rows = _rows_of(numel, width)
        out.append(slab[off:off + rows].reshape(-1)[:numel].reshape(shp))
        off += rows
    return out


def _round_up(n, m):
    return -(-n // m) * m


def _by_dest(name, g):
    kind, (r, c) = SHARDED[name]
    if kind == "col":
        return g.reshape(r, N_DEV, c // N_DEV).transpose(1, 0, 2)
    return g.reshape(N_DEV, r // N_DEV, c)


def _from_shards(name, blocks):
    kind, (r, c) = SHARDED[name]
    return blocks.transpose(1, 0, 2).reshape(r, c) if kind == "col" else blocks.reshape(r, c)


def _gather_params(a):
    names = list(SHARDED)
    got = all_gather([a[n].astype(MXU) if n in MATS else a[n] for n in names], name="gather_params")
    return {n: _from_shards(n, blocks) for n, blocks in zip(names, got)}


L1_IN_PAD = 6272


def _ffn_fwd(h, a, w, pfx):
    u, ut = rmsnorm_fwd(h, a[pfx + "ffn_pre_norm"], out_dtype=MXU, name=pfx + "ffn_pre", with_t=True)
    up = matmul(u, w[pfx + "ffn_w_up"], name=pfx + "ffn_up")
    act, act_t = dwconv_fwd(up, a[pfx + "ffn_conv_w"], a[pfx + "ffn_conv_b"], mode="geglu", x_off=0,
                            c_out=D_FF, cblk=256, out_dtype=MXU, name=pfx + "ffn_act", with_t=True)
    down = matmul(act, w[pfx + "ffn_w_down"], name=pfx + "ffn_down")
    out = rmsnorm_fwd(down, a[pfx + "ffn_post_norm"], res=h, out_dtype=F32, name=pfx + "ffn_post")
    return out, (h, ut, up, act_t, down)


def _ffn_bwd(dh, saved, a, w, pfx, g):
    h, ut, up, act_t, down = saved
    dd, g[pfx + "ffn_post_norm"] = rmsnorm_bwd(down, a[pfx + "ffn_post_norm"], dh, out_dtype=MXU,
                                               name=pfx + "ffn_post_bwd")
    dact = matmul(dd, w[pfx + "ffn_w_down"], trans_b=True, name=pfx + "ffn_down_dx")
    g[pfx + "ffn_w_down"] = matmul(act_t, dd, name=pfx + "ffn_down_dw")
    dups, g[pfx + "ffn_conv_w"], g[pfx + "ffn_conv_b"] = dwconv_bwd(
        up, a[pfx + "ffn_conv_w"], a[pfx + "ffn_conv_b"], dact, mode="geglu", x_off=0, c_out=D_FF,
        cblk=256, name=pfx + "ffn_act_bwd")
    g[pfx + "ffn_w_up"] = jnp.concatenate(
        [matmul(ut, d, name=pfx + "ffn_up_dw%d" % i) for i, d in enumerate(dups)], axis=1)
    du = matmul_cat(dups, w[pfx + "ffn_w_up"], trans_b=True, name=pfx + "ffn_up_dx")
    dh_in, g[pfx + "ffn_pre_norm"] = rmsnorm_bwd(h, a[pfx + "ffn_pre_norm"], du, res=dh, out_dtype=F32,
                                                 name=pfx + "ffn_pre_bwd")
    return dh_in


def _local_step(a, w):
    x = a["x"][0]
    seq = x.shape[0]
    h0 = jnp.concatenate([jnp.zeros((PAD, D_MODEL), F32), a["meta_tokens"], x], axis=0)
    g = {}

    u0, u0t = rmsnorm_fwd(h0, a["l0_mix_pre_norm"], out_dtype=MXU, name="l0_mix_pre", with_t=True)
    proj0 = matmul(u0, w["l0_w_in"], name="l0_in")
    lru = (a["l0_lru_conv_w"], a["l0_lru_conv_b"], a["l0_lru_w_a"], a["l0_lru_b_a"], a["l0_lru_w_x"],
           a["l0_lru_b_x"], a["l0_lru_lambda"])
    ya, ya_t, hl = lru_fwd(proj0, *lru, gate_off=0, xr_off=1024, name="l0_lru")
    yb, yb_t = attn_fwd(proj0, a["l0_attn_sinks"], q_off=2048, k_off=3072, v_off=3200, name="l0_attn")
    o0 = matmul_cat([ya, yb], w["l0_w_out"], name="l0_out")
    h1 = rmsnorm_fwd(o0, a["l0_mix_post_norm"], res=h0, out_dtype=F32, name="l0_mix_post")
    h2, ffn0 = _ffn_fwd(h1, a, w, "l0_")

    u2, u2t = rmsnorm_fwd(h2, a["l1_mix_pre_norm"], out_dtype=MXU, name="l1_mix_pre", with_t=True)
    proj1 = matmul(u2, w["l1_w_in"], name="l1_in")
    xbc = dwconv_fwd(proj1, a["l1_ssm_conv_w"], a["l1_ssm_conv_b"], mode="silu", x_off=D_SSM,
                     c_out=2 * D_SSM, cblk=512, out_dtype=F32, name="l1_ssm_conv")
    ssd = (a["l1_dt_bias"], a["l1_a_log"], a["l1_d_skip"], a["l1_gate_norm"])
    yn, yn_t, st = ssd_fwd(xbc, proj1, *ssd, z_off=0, dt_off=3 * D_SSM, name="l1_ssd")
    o1 = matmul(yn, w["l1_w_out"], name="l1_out")
    h3 = rmsnorm_fwd(o1, a["l1_mix_post_norm"], res=h2, out_dtype=F32, name="l1_mix_post")
    h4, ffn1 = _ffn_fwd(h3, a, w, "l1_")

    loss, dh4 = loss_fwd_bwd(h4, a["loss_target"][0], name="loss")

    dh3 = _ffn_bwd(dh4, ffn1, a, w, "l1_", g)
    do1, g["l1_mix_post_norm"] = rmsnorm_bwd(o1, a["l1_mix_post_norm"], dh3, out_dtype=MXU,
                                             name="l1_mix_post_bwd")
    dyn = matmul(do1, w["l1_w_out"], trans_b=True, name="l1_out_dx")
    g["l1_w_out"] = matmul(yn_t, do1, name="l1_out_dw")
    (dxbc, dz, draw, g["l1_gate_norm"], g["l1_d_skip"], g["l1_dt_bias"], g["l1_a_log"]) = ssd_bwd(
        xbc, proj1, st, dyn, *ssd, z_off=0, dt_off=3 * D_SSM, name="l1_ssd_bwd")
    (dxin,), g["l1_ssm_conv_w"], g["l1_ssm_conv_b"] = dwconv_bwd(
        proj1, a["l1_ssm_conv_w"], a["l1_ssm_conv_b"], dxbc, mode="silu", x_off=D_SSM,
        c_out=2 * D_SSM, cblk=512, name="l1_ssm_conv_bwd")
    g["l1_w_in"] = jnp.concatenate(
        [matmul(u2t, dz, name="l1_in_dw_z"), matmul(u2t, dxin, name="l1_in_dw_x"),
         matmul(u2t, draw, name="l1_in_dw_dt")[:, :SSD_HEADS]], axis=1)
    du2 = matmul_cat([dz, dxin, draw], w["l1_w_in"], trans_b=True, name="l1_in_dx")
    dh2, g["l1_mix_pre_norm"] = rmsnorm_bwd(h2, a["l1_mix_pre_norm"], du2, res=dh3, out_dtype=F32,
                                            name="l1_mix_pre_bwd")

    dh1 = _ffn_bwd(dh2, ffn0, a, w, "l0_", g)
    do0, g["l0_mix_post_norm"] = rmsnorm_bwd(o0, a["l0_mix_post_norm"], dh1, out_dtype=MXU,
                                             name="l0_mix_post_bwd")
    dy = matmul(do0, w["l0_w_out"], trans_b=True, name="l0_out_dx")
    g["l0_w_out"] = jnp.concatenate([matmul(ya_t, do0, name="l0_out_dw_a"),
                                     matmul(yb_t, do0, name="l0_out_dw_b")], axis=0)
    (dgate, dxr, g["l0_lru_conv_w"], dcb, g["l0_lru_w_a"], dba, g["l0_lru_w_x"], dbx, dlam) = lru_bwd(
        proj0, hl, dy, *lru, gate_off=0, xr_off=1024, dy_off=0, name="l0_lru_bwd")
    g["l0_lru_conv_b"], g["l0_lru_b_a"], g["l0_lru_b_x"], g["l0_lru_lambda"] = dcb[0], dba[0], dbx[0], dlam[0]
    dq, dk, dv, g["l0_attn_sinks"] = attn_bwd(proj0, a["l0_attn_sinks"], dy, q_off=2048, k_off=3072,
                                              v_off=3200, dy_off=1024, name="l0_attn_bwd")
    dproj0 = [dgate, dxr, dq, dk, dv]
    g["l0_w_in"] = jnp.concatenate(
        [matmul(u0t, d, name="l0_in_dw%d" % i) for i, d in enumerate(dproj0)], axis=1)
    du0 = matmul_cat(dproj0, w["l0_w_in"], trans_b=True, name="l0_in_dx")
    dh0, g["l0_mix_pre_norm"] = rmsnorm_bwd(h0, a["l0_mix_pre_norm"], du0, res=dh1, out_dtype=F32,
                                            name="l0_mix_pre_bwd")
    g["meta_tokens"] = dh0[PAD:BLK]
    for n in REPLICATED:
        g[n] = g[n].reshape(a[n].shape)
    return loss[0, 0], dh0[BLK:].reshape(1, seq, D_MODEL), g


def kernel(*args):
    a = dict(zip(INPUTS, args))
    full = _gather_params(a)
    w = {n: full[n] for n in MATS}
    w["l1_w_in"] = jnp.pad(w["l1_w_in"], ((0, 0), (0, L1_IN_PAD - w["l1_w_in"].shape[1])))
    loss_part, grad_x, g = _local_step({**a, **{n: full[n] for n in SMALL_SHARDED}}, w)
    loss = lax.psum(loss_part, ("x", "y", "c"))

    sh_names = list(SHARDED)
    parts = [_by_dest(n, g[n]) for n in sh_names]
    from_sibling = pair_exchange(parts, name="rs_pair")
    pairs = [pair_add(p, l, name="rs_pair_add_" + n) for n, p, l in zip(sh_names, parts, from_sibling)]
    landed = chip_exchange(pairs, name="rs_chip")
    sh_out = {n: adamw(l, a[n], a["m_" + n], a["v_" + n], name="adamw_" + n)
              for n, l in zip(sh_names, landed)}

    rp_shapes = [a[n].shape for n in REPLICATED]
    rrows = _round_up(sum(_rows_of(math.prod(s), SMALL_W) for s in rp_shapes), 128)
    gathered = all_gather([_pack([g[n] for n in REPLICATED], SMALL_W, rrows)], name="gather_small_grads")[0]
    rp_out = adamw(gathered, *[_pack([a[p + n] for n in REPLICATED], SMALL_W, rrows) for p in ("", "m_", "v_")],
                   name="adamw_replicated")
    rp_out = [dict(zip(REPLICATED, _unpack(s, rp_shapes, SMALL_W))) for s in rp_out]

    outs = [loss, grad_x]
    for k in range(4):
        outs += [sh_out[n][k] if n in SHARDED else rp_out[k][n] for n in WEIGHTS]
    return tuple(outs)
```

```python
import functools
import math

import jax
import jax.numpy as jnp
from jax import lax
from jax.experimental import pallas as pl
from jax.experimental.pallas import tpu as pltpu

F32 = jnp.float32
BF16 = jnp.bfloat16
MXU = jnp.bfloat16

D_MODEL = 1024
N_META = 16
BLK = 128
PAD = BLK - N_META
D_RNN = 1024
LRU_C = 8.0
N_Q_HEADS = 16
HEAD_DIM = 64
D_SSM = 2048
SSD_HEADS = 32
SSD_GROUPS = 8
D_FF = 2816
EPS = 1e-6
NEG = -1e30
N_DEV = 8

ADAM_LR = 0.001
ADAM_B1 = 0.9
ADAM_B2 = 0.999
ADAM_EPS = 1e-08
ADAM_WD = 0.01
ADAM_STEP = 10

VMEM_LIMIT = 56 * 1024 * 1024
MESH = pl.DeviceIdType.MESH
ANY = pl.BlockSpec(memory_space=pl.ANY)


def _cp(*sem):
    return pltpu.CompilerParams(dimension_semantics=sem, vmem_limit_bytes=VMEM_LIMIT)


def _pick(n, cands):
    for c in cands:
        if n % c == 0:
            return c
    return n


def _dot(a, b):
    return jnp.dot(a.astype(MXU), b.astype(MXU), preferred_element_type=F32)


def _dot_nt(a, b):
    return lax.dot_general(a.astype(MXU), b.astype(MXU), (((1,), (1,)), ((), ())),
                           preferred_element_type=F32)


def _dot_tn(a, b):
    return jnp.dot(a.T.astype(MXU), b.astype(MXU), preferred_element_type=F32)


def _dot_split(v, e):
    hi = v.astype(BF16)
    lo = (v - hi.astype(F32)).astype(BF16)
    return (jnp.dot(hi, e, preferred_element_type=F32)
            + jnp.dot(lo, e, preferred_element_type=F32))


def _sigmoid(x):
    return 1.0 / (1.0 + jnp.exp(-x))


def _log1p(x):
    u = 1.0 + x
    return jnp.where(u == 1.0, x, jnp.log(u) * (x / jnp.where(u == 1.0, 1.0, u - 1.0)))


def _expm1(x):
    u = jnp.exp(x)
    um1 = u - 1.0
    lg = jnp.log(jnp.where(u > 0.0, u, 1.0))
    safe = (um1 != 0.0) & (um1 != -1.0)
    return jnp.where(um1 == 0.0, x, jnp.where(um1 == -1.0, -1.0,
                                               um1 * (x / jnp.where(safe, lg, 1.0))))


def _softplus(x):
    return jnp.maximum(x, 0.0) + _log1p(jnp.exp(-jnp.abs(x)))


_GC = math.sqrt(2.0 / math.pi)


def _gelu(x):
    t = jnp.tanh(_GC * (x + 0.044715 * x * x * x))
    return 0.5 * x * (1.0 + t)


def _gelu_grad(x):
    t = jnp.tanh(_GC * (x + 0.044715 * x * x * x))
    return 0.5 * (1.0 + t) + 0.5 * x * (1.0 - t * t) * (_GC * (1.0 + 3.0 * 0.044715 * x * x))


def _silu(x):
    return x * _sigmoid(x)


def _silu_grad(x):
    s = _sigmoid(x)
    return s * (1.0 + x * (1.0 - s))


def _rows(shape):
    return lax.broadcasted_iota(jnp.int32, shape, 0)


def _lanes(shape):
    return lax.broadcasted_iota(jnp.int32, shape, 1)


def _shift_down(x, tail, d):
    if d == 0:
        return x
    n = x.shape[0]
    xr = pltpu.roll(x, d, 0)
    tr = pltpu.roll(tail, d, 0)
    first = jnp.where(_rows(tr.shape) < d, tr, xr[0:8])
    return jnp.concatenate([first, xr[8:n]], axis=0)


def _shift_up(x, head, d):
    if d == 0:
        return x
    n = x.shape[0]
    xr = pltpu.roll(x, n - d, 0)
    hr = pltpu.roll(head, 8 - d, 0)
    last = jnp.where(_rows(hr.shape) >= 8 - d, hr, xr[n - 8:n])
    return jnp.concatenate([xr[0:n - 8], last], axis=0)


def _row_at(x, i):
    return jnp.sum(jnp.where(_rows(x.shape) == i, x, 0.0), axis=0, keepdims=True)


def _scan_fwd(a, u):
    n = a.shape[0]
    ri = _rows(a.shape)
    d = 1
    while d < n:
        m = ri >= d
        us = jnp.where(m, pltpu.roll(u, d, 0), 0.0)
        as_ = jnp.where(m, pltpu.roll(a, d, 0), 1.0)
        u = u + a * us
        a = a * as_
        d *= 2
    return a, u


def _scan_rev(c, u):
    n = c.shape[0]
    ri = _rows(c.shape)
    d = 1
    while d < n:
        m = ri < n - d
        us = jnp.where(m, pltpu.roll(u, n - d, 0), 0.0)
        cs = jnp.where(m, pltpu.roll(c, n - d, 0), 1.0)
        u = u + c * us
        c = c * cs
        d *= 2
    return c, u


def _cumsum_fwd(x):
    n = x.shape[0]
    ri = _rows(x.shape)
    d = 1
    while d < n:
        x = x + jnp.where(ri >= d, pltpu.roll(x, d, 0), 0.0)
        d *= 2
    return x


def _cumsum_rev(x):
    n = x.shape[0]
    ri = _rows(x.shape)
    d = 1
    while d < n:
        x = x + jnp.where(ri < n - d, pltpu.roll(x, n - d, 0), 0.0)
        d *= 2
    return x


MATMUL_VMEM = 40 * 1024 * 1024


def _matmul_tiles(m, n, k, tk, out_bytes):
    best = None
    for tm in (1664, 1408, 1040, 1024, 832, 640, 512, 384, 256, 128):
        if m % tm:
            continue
        for tn in (2048, 1408, 1024, 896, 640, 512, 384, 256, 128):
            if n % tn:
                continue
            vmem = 2 * (tm * tk * 2 + tk * tn * 2 + tm * tn * out_bytes) + (tm * tn * 4 if k > tk else 0)
            if vmem > MATMUL_VMEM:
                continue
            traffic = (n // tn) * m * k * 2 + (m // tm) * k * n * 2
            if best is None or traffic < best[0]:
                best = (traffic, tm, tn)
    return (best[1], best[2]) if best else (m, n)


def matmul(a, b, *, trans_b=False, out_dtype=F32, name):
    m, k = a.shape
    n = b.shape[0] if trans_b else b.shape[1]
    tk = k if k <= 2048 else _pick(k, (1664, 1408, 1024, 896, 512, 256, 128))
    nk = k // tk
    tm, tn = _matmul_tiles(m, n, k, tk, jnp.dtype(out_dtype).itemsize)

    def product(a_ref, b_ref):
        return _dot_nt(a_ref[...], b_ref[...]) if trans_b else _dot(a_ref[...], b_ref[...])

    def kern_once(a_ref, b_ref, o_ref):
        o_ref[...] = product(a_ref, b_ref).astype(o_ref.dtype)

    def kern_acc(a_ref, b_ref, o_ref, acc_ref):
        kk = pl.program_id(2)

        @pl.when(kk == 0)
        def _():
            acc_ref[...] = product(a_ref, b_ref)

        @pl.when(kk > 0)
        def _():
            acc_ref[...] += product(a_ref, b_ref)

        @pl.when(kk == nk - 1)
        def _():
            o_ref[...] = acc_ref[...].astype(o_ref.dtype)

    b_spec = (pl.BlockSpec((tn, tk), lambda i, j, kk: (j, kk)) if trans_b
              else pl.BlockSpec((tk, tn), lambda i, j, kk: (kk, j)))
    return pl.pallas_call(
        kern_once if nk == 1 else kern_acc, name=name,
        grid=(m // tm, n // tn, nk),
        in_specs=[pl.BlockSpec((tm, tk), lambda i, j, kk: (i, kk)), b_spec],
        out_specs=pl.BlockSpec((tm, tn), lambda i, j, kk: (i, j)),
        out_shape=jax.ShapeDtypeStruct((m, n), out_dtype),
        scratch_shapes=[] if nk == 1 else [pltpu.VMEM((tm, tn), F32)],
        compiler_params=_cp("parallel", "parallel", "arbitrary"),
    )(a, b)


def matmul_cat(a_list, b, *, trans_b=False, out_dtype=F32, name):
    m = a_list[0].shape[0]
    ks = [x.shape[1] for x in a_list]
    ktot = sum(ks)
    n = b.shape[0] if trans_b else b.shape[1]
    tn = _pick(n, (512, 256, 128))
    tm = next((c for c in (1664, 1040, 832, 640, 512, 384, 256, 128)
               if m % c == 0 and c * ktot * 2 <= 8 * 1024 * 1024), m)
    na = len(a_list)

    def kern(*refs):
        b_ref, o_ref = refs[na], refs[na + 1]
        acc, off = None, 0
        for a_ref, k in zip(refs[:na], ks):
            if trans_b:
                part = _dot_nt(a_ref[...], b_ref[:, off:off + k])
            else:
                part = _dot(a_ref[...], b_ref[off:off + k, :])
            acc = part if acc is None else acc + part
            off += k
        o_ref[...] = acc.astype(o_ref.dtype)

    b_spec = (pl.BlockSpec((tn, ktot), lambda i, j: (j, 0)) if trans_b
              else pl.BlockSpec((ktot, tn), lambda i, j: (0, j)))
    return pl.pallas_call(
        kern, name=name, grid=(m // tm, n // tn),
        in_specs=[pl.BlockSpec((tm, k), lambda i, j: (i, 0)) for k in ks] + [b_spec],
        out_specs=pl.BlockSpec((tm, tn), lambda i, j: (i, j)),
        out_shape=jax.ShapeDtypeStruct((m, n), out_dtype),
        compiler_params=_cp("parallel", "parallel"),
    )(*a_list, b)


def _row_tile(t):
    return _pick(t, (832, 640, 512, 384, 256, 128))


def rmsnorm_fwd(x, w, res=None, *, out_dtype, name, with_t=False):
    t, d = x.shape
    tr = _conv_tile(t) if with_t else _row_tile(t)

    def kern(*refs):
        x_ref, w_ref = refs[0], refs[1]
        o_ref = refs[-2] if with_t else refs[-1]
        xv = x_ref[...]
        r = lax.rsqrt(jnp.mean(xv * xv, axis=-1, keepdims=True) + EPS)
        y = xv * r * w_ref[...]
        if res is not None:
            y = refs[2][...] + y
        o_ref[...] = y.astype(o_ref.dtype)
        if with_t:
            refs[-1][...] = y.T.astype(o_ref.dtype)

    row = pl.BlockSpec((tr, d), lambda i: (i, 0))
    vec = pl.BlockSpec((1, d), lambda i: (0, 0))
    ins = [x, w.reshape(1, d)] + ([] if res is None else [res])
    specs = [row, vec] + ([] if res is None else [row])
    out_specs, out_shape = row, jax.ShapeDtypeStruct((t, d), out_dtype)
    if with_t:
        out_specs = [row, pl.BlockSpec((d, tr), lambda i: (0, i))]
        out_shape = [out_shape, jax.ShapeDtypeStruct((d, t), out_dtype)]
    return pl.pallas_call(
        kern, name=name, grid=(t // tr,), in_specs=specs, out_specs=out_specs, out_shape=out_shape,
        compiler_params=_cp("parallel"),
    )(*ins)


def rmsnorm_bwd(x, w, dy, res=None, *, out_dtype, name):
    t, d = x.shape
    tr = _row_tile(t)

    def kern(*refs):
        if res is None:
            x_ref, w_ref, dy_ref, dx_ref, dw_ref = refs
        else:
            x_ref, w_ref, dy_ref, r_ref, dx_ref, dw_ref = refs
        i = pl.program_id(0)
        xv = x_ref[...]
        dyv = dy_ref[...].astype(F32)
        r = lax.rsqrt(jnp.mean(xv * xv, axis=-1, keepdims=True) + EPS)
        xh = xv * r
        g = dyv * w_ref[...]
        dx = r * (g - xh * jnp.mean(g * xh, axis=-1, keepdims=True))
        if res is not None:
            dx = r_ref[...] + dx
        dx_ref[...] = dx.astype(dx_ref.dtype)
        part = jnp.sum(dyv * xh, axis=0, keepdims=True)

        @pl.when(i == 0)
        def _():
            dw_ref[...] = part

        @pl.when(i > 0)
        def _():
            dw_ref[...] += part

    row = pl.BlockSpec((tr, d), lambda i: (i, 0))
    vec = pl.BlockSpec((1, d), lambda i: (0, 0))
    ins = [x, w.reshape(1, d), dy] + ([] if res is None else [res])
    specs = [row, vec, row] + ([] if res is None else [row])
    return pl.pallas_call(
        kern, name=name, grid=(t // tr,), in_specs=specs, out_specs=[row, vec],
        out_shape=[jax.ShapeDtypeStruct((t, d), out_dtype), jax.ShapeDtypeStruct((1, d), F32)],
        compiler_params=_cp("arbitrary"),
    )(*ins)


def _conv_tile(t):
    return _pick(t, (640, 384, 256, 128))


def _conv_apply(x, tail, cw, cb, ksz):
    y = cb
    for k in range(ksz):
        y = y + cw[k:k + 1, :] * _shift_down(x, tail, ksz - 1 - k)
    return y


def dwconv_fwd(x, cw, cb, *, mode, x_off, c_out, cblk, out_dtype, name, with_t=False):
    t = x.shape[0]
    ksz = cw.shape[0]
    tb = _conv_tile(t)
    nb, ncb, t8 = t // tb, c_out // cblk, tb // 8
    xo = x_off // cblk
    nin = 2 if mode == "geglu" else 1

    def kern(*refs):
        o_ref = refs[-2] if with_t else refs[-1]
        n = pl.program_id(1)
        for c in range(cblk // BLK):
            ls = slice(c * BLK, (c + 1) * BLK)
            for s in range(tb // BLK):
                rs = slice(s * BLK, (s + 1) * BLK)
                valid = (n * tb + s * BLK + _rows((BLK, BLK))) >= PAD
                hs = []
                for q in range(nin):
                    x_ref, t_ref, w_ref, b_ref = refs[4 * q:4 * q + 4]
                    tail = (jnp.where(n > 0, t_ref[:, ls], 0.0) if s == 0
                            else x_ref[s * BLK - 8:s * BLK, ls])
                    hs.append(_conv_apply(x_ref[rs, ls], tail, w_ref[:, ls], b_ref[:, ls], ksz))
                y = _gelu(hs[0]) * hs[1] if mode == "geglu" else _silu(hs[0])
                y = jnp.where(valid, y, 0.0)
                o_ref[rs, ls] = y.astype(o_ref.dtype)
                if with_t:
                    refs[-1][ls, rs] = y.T.astype(o_ref.dtype)

    ins, specs = [], []
    for q in range(nin):
        co = xo + q * ncb
        wo = q * ncb
        ins += [x, x, cw, cb.reshape(1, -1)]
        specs += [
            pl.BlockSpec((tb, cblk), lambda j, n, co=co: (n, co + j)),
            pl.BlockSpec((8, cblk), lambda j, n, co=co: (jnp.maximum(n * t8 - 1, 0), co + j)),
            pl.BlockSpec((ksz, cblk), lambda j, n, wo=wo: (0, wo + j)),
            pl.BlockSpec((1, cblk), lambda j, n, wo=wo: (0, wo + j)),
        ]
    out_specs = pl.BlockSpec((tb, cblk), lambda j, n: (n, j))
    out_shape = jax.ShapeDtypeStruct((t, c_out), out_dtype)
    if with_t:
        out_specs = [out_specs, pl.BlockSpec((cblk, tb), lambda j, n: (j, n))]
        out_shape = [out_shape, jax.ShapeDtypeStruct((c_out, t), out_dtype)]
    return pl.pallas_call(
        kern, name=name, grid=(ncb, nb), in_specs=specs, out_specs=out_specs, out_shape=out_shape,
        compiler_params=_cp("parallel", "parallel"),
    )(*ins)


def dwconv_bwd(x, cw, cb, dy, *, mode, x_off, c_out, cblk, name, carry=None):
    t = x.shape[0]
    ksz = cw.shape[0]
    tb = _conv_tile(t)
    nb, ncb, t8 = t // tb, c_out // cblk, tb // 8
    xo = x_off // cblk
    nin = 2 if mode == "geglu" else 1
    ctot = nin * c_out

    def kern(*refs):
        dy_ref = refs[4 * nin]
        outs = refs[4 * nin + 1:4 * nin + 1 + 3 * nin]
        heads = refs[4 * nin + 1 + 3 * nin:]
        n = pl.program_id(1)
        blk = nb - 1 - n

        @pl.when(n == 0)
        def _():
            for q in range(nin):
                heads[q][...] = jnp.zeros_like(heads[q])
                outs[3 * q + 1][...] = jnp.zeros_like(outs[3 * q + 1])
                outs[3 * q + 2][...] = jnp.zeros_like(outs[3 * q + 2])

        for c in range(cblk // BLK):
            ls = slice(c * BLK, (c + 1) * BLK)
            head = [heads[q][:, ls] for q in range(nin)]
            dwa = [[None] * ksz for _ in range(nin)]
            dba = [None] * nin
            for s in reversed(range(tb // BLK)):
                rs = slice(s * BLK, (s + 1) * BLK)
                valid = (blk * tb + s * BLK + _rows((BLK, BLK))) >= PAD
                xs, tails, hs = [], [], []
                for q in range(nin):
                    x_ref, t_ref, w_ref, b_ref = refs[4 * q:4 * q + 4]
                    tail = (jnp.where(blk > 0, t_ref[:, ls], 0.0) if s == 0
                            else x_ref[s * BLK - 8:s * BLK, ls])
                    xs.append(x_ref[rs, ls])
                    tails.append(tail)
                    hs.append(_conv_apply(xs[q], tail, w_ref[:, ls], b_ref[:, ls], ksz))
                dyv = dy_ref[rs, ls].astype(F32)
                if mode == "geglu":
                    dhs = [dyv * hs[1] * _gelu_grad(hs[0]), dyv * _gelu(hs[0])]
                else:
                    dhs = [dyv * _silu_grad(hs[0])]
                for q in range(nin):
                    w_ref = refs[4 * q + 2]
                    dh = jnp.where(valid, dhs[q], 0.0)
                    dx = jnp.zeros_like(dh)
                    for k in range(ksz):
                        sh = ksz - 1 - k
                        dx = dx + w_ref[k:k + 1, ls] * _shift_up(dh, head[q], sh)
                        part = jnp.sum(dh * _shift_down(xs[q], tails[q], sh), axis=0, keepdims=True)
                        dwa[q][k] = part if dwa[q][k] is None else dwa[q][k] + part
                    outs[3 * q][rs, ls] = jnp.where(valid, dx, 0.0).astype(outs[3 * q].dtype)
                    part = jnp.sum(dh, axis=0, keepdims=True)
                    dba[q] = part if dba[q] is None else dba[q] + part
                    head[q] = dh[0:8]
            for q in range(nin):
                outs[3 * q + 1][:, ls] += jnp.concatenate(dwa[q], axis=0)
                outs[3 * q + 2][:, ls] += dba[q]
                heads[q][:, ls] = head[q]

    ins, specs, out_specs, out_shape, scratch = [], [], [], [], []
    for q in range(nin):
        co = xo + q * ncb
        wo = q * ncb
        ins += [x, x, cw, cb.reshape(1, -1)]
        specs += [
            pl.BlockSpec((tb, cblk), lambda j, n, co=co: (nb - 1 - n, co + j)),
            pl.BlockSpec((8, cblk), lambda j, n, co=co: (jnp.maximum((nb - 1 - n) * t8 - 1, 0), co + j)),
            pl.BlockSpec((ksz, cblk), lambda j, n, wo=wo: (0, wo + j)),
            pl.BlockSpec((1, cblk), lambda j, n, wo=wo: (0, wo + j)),
        ]
        out_specs += [
            pl.BlockSpec((tb, cblk), lambda j, n: (nb - 1 - n, j)),
            pl.BlockSpec((ksz, cblk), lambda j, n: (0, j)),
            pl.BlockSpec((1, cblk), lambda j, n: (0, j)),
        ]
        out_shape += [jax.ShapeDtypeStruct((t, c_out), MXU),
                      jax.ShapeDtypeStruct((ksz, c_out), F32),
                      jax.ShapeDtypeStruct((1, c_out), F32)]
        scratch.append(pltpu.VMEM((8, cblk), F32))
    ins.append(dy)
    specs.append(pl.BlockSpec((tb, cblk), lambda j, n: (nb - 1 - n, j)))
    res = _call(kern, ins, name=name, grid=(ncb, nb), in_specs=specs, out_specs=out_specs,
                out_shape=out_shape, scratch_shapes=scratch, sem=("parallel", "arbitrary"), carry=carry)
    dxs = [res[3 * q] for q in range(nin)]
    dcw = jnp.concatenate([res[3 * q + 1] for q in range(nin)], axis=1)
    dcb = jnp.concatenate([res[3 * q + 2] for q in range(nin)], axis=1)
    return dxs, dcw, dcb.reshape(ctot), res[3 * nin:]


def _lru_tile(t):
    return _pick(t, (640, 384, 256, 128))


def _lru_gates(xc, wa, ba, wx, bx, sp):
    r = _sigmoid(_dot(xc, wa) + ba)
    i = _sigmoid(_dot(xc, wx) + bx)
    log_a = -LRU_C * r * sp
    a = jnp.exp(log_a)
    mult = jnp.sqrt(-_expm1(2.0 * log_a))
    return r, i, a, mult


def lru_fwd(proj, cw, cb, wa, ba, wx, bx, lam, *, gate_off, xr_off, name, carry=None):
    t = proj.shape[0]
    tb = _lru_tile(t)
    nb, ns, t8 = t // tb, tb // BLK, tb // 8
    go, xo = gate_off // BLK, xr_off // BLK

    def kern(g_ref, x_ref, xt_ref, cw_ref, cb_ref, wa_ref, ba_ref, wx_ref, bx_ref, lam_ref,
             y_ref, yt_ref, h_ref, hc_ref):
        n = pl.program_id(1)

        @pl.when(n == 0)
        def _():
            hc_ref[...] = jnp.zeros_like(hc_ref)

        sp = _softplus(-lam_ref[...])
        hprev = hc_ref[0:1, :]
        for s in range(ns):
            sl = slice(s * BLK, (s + 1) * BLK)
            xv = x_ref[sl, :]
            tail = jnp.where(n > 0, xt_ref[...], 0.0) if s == 0 else x_ref[s * BLK - 8:s * BLK, :]
            valid = (n * tb + s * BLK + _rows((BLK, BLK))) >= PAD
            xc = jnp.where(valid, _conv_apply(xv, tail, cw_ref[...], cb_ref[...], 4), 0.0)
            _, i, a, mult = _lru_gates(xc, wa_ref[0], ba_ref[...], wx_ref[0], bx_ref[...], sp)
            u = mult * (i * xc)
            ca, cu = _scan_fwd(a, u)
            h = cu + ca * hprev
            hprev = _row_at(h, BLK - 1)
            h_ref[sl, :] = h
            y = _gelu(g_ref[sl, :]) * h
            y_ref[sl, :] = y.astype(y_ref.dtype)
            yt_ref[:, sl] = y.T.astype(yt_ref.dtype)
        hc_ref[...] = jnp.broadcast_to(hprev, hc_ref.shape)

    vec = pl.BlockSpec((1, BLK), lambda j, n: (0, j))
    mat = pl.BlockSpec((1, BLK, BLK), lambda j, n: (j, 0, 0))
    return _call(
        kern, (proj, proj, proj, cw, cb.reshape(1, -1), wa, ba.reshape(1, -1), wx, bx.reshape(1, -1),
               lam.reshape(1, -1)),
        name=name, grid=(D_RNN // BLK, nb),
        in_specs=[
            pl.BlockSpec((tb, BLK), lambda j, n: (n, go + j)),
            pl.BlockSpec((tb, BLK), lambda j, n: (n, xo + j)),
            pl.BlockSpec((8, BLK), lambda j, n: (jnp.maximum(n * t8 - 1, 0), xo + j)),
            pl.BlockSpec((4, BLK), lambda j, n: (0, j)), vec, mat, vec, mat, vec, vec,
        ],
        out_specs=[pl.BlockSpec((tb, BLK), lambda j, n: (n, j)),
                   pl.BlockSpec((BLK, tb), lambda j, n: (j, n)),
                   pl.BlockSpec((tb, BLK), lambda j, n: (n, j))],
        out_shape=[jax.ShapeDtypeStruct((t, D_RNN), MXU), jax.ShapeDtypeStruct((D_RNN, t), MXU),
                   jax.ShapeDtypeStruct((t, D_RNN), F32)],
        scratch_shapes=[pltpu.VMEM((8, BLK), F32)],
        sem=("parallel", "arbitrary"), carry=carry)


def lru_bwd(proj, h, dy, cw, cb, wa, ba, wx, bx, lam, *, gate_off, xr_off, dy_off, name):
    t = proj.shape[0]
    tb = _lru_tile(t)
    nb, ns, t8 = t // tb, tb // BLK, tb // 8
    go, xo, do = gate_off // BLK, xr_off // BLK, dy_off // BLK

    def kern(g_ref, x_ref, xt_ref, h_ref, ht_ref, dy_ref, cw_ref, cb_ref, wa_ref, ba_ref,
             wx_ref, bx_ref, lam_ref,
             dg_ref, dx_ref, dcw_ref, dcb_ref, dwa_ref, dba_ref, dwx_ref, dbx_ref, dlam_ref,
             gin_ref, head_ref):
        n = pl.program_id(1)
        blk = nb - 1 - n

        @pl.when(n == 0)
        def _():
            gin_ref[...] = jnp.zeros_like(gin_ref)
            head_ref[...] = jnp.zeros_like(head_ref)
            for r_ in (dcw_ref, dcb_ref, dwa_ref, dba_ref, dwx_ref, dbx_ref, dlam_ref):
                r_[...] = jnp.zeros_like(r_)

        lamv = lam_ref[...]
        sp = _softplus(-lamv)
        dsp_dlam = -_sigmoid(-lamv)
        g_in = gin_ref[0:1, :]
        head = head_ref[...]
        ones8 = jnp.ones((8, BLK), F32)
        for s in reversed(range(ns)):
            sl = slice(s * BLK, (s + 1) * BLK)
            xv = x_ref[sl, :]
            if s == 0:
                tail = jnp.where(blk > 0, xt_ref[...], 0.0)
                htail = jnp.where(blk > 0, ht_ref[...], 0.0)
            else:
                tail = x_ref[s * BLK - 8:s * BLK, :]
                htail = h_ref[s * BLK - 8:s * BLK, :]
            valid = (blk * tb + s * BLK + _rows((BLK, BLK))) >= PAD
            xc = jnp.where(valid, _conv_apply(xv, tail, cw_ref[...], cb_ref[...], 4), 0.0)
            wav, wxv = wa_ref[0], wx_ref[0]
            r, i, a, mult = _lru_gates(xc, wav, ba_ref[...], wxv, bx_ref[...], sp)
            hv = h_ref[sl, :]
            hprev = _shift_down(hv, htail, 1)
            gv = g_ref[sl, :]
            dyv = dy_ref[sl, :].astype(F32)
            dh = dyv * _gelu(gv)
            dg_ref[sl, :] = (dyv * hv * _gelu_grad(gv)).astype(dg_ref.dtype)
            c = _shift_up(a, ones8, 1)
            cc, cu = _scan_rev(c, dh)
            gg = cu + cc * g_in
            g_in = _row_at(a * gg, 0)
            da = gg * hprev
            di = gg * mult * xc
            dxc = gg * mult * i
            dmult = gg * i * xc
            dlog_a = da * a - dmult * (a * a) / mult
            dr = dlog_a * (-LRU_C * sp)
            dlam_ref[...] += jnp.sum(dlog_a * (-LRU_C) * r, axis=0, keepdims=True) * dsp_dlam
            dpr = dr * r * (1.0 - r)
            dpi = di * i * (1.0 - i)
            dxc = dxc + _dot_nt(dpr, wav) + _dot_nt(dpi, wxv)
            dxc = jnp.where(valid, dxc, 0.0)
            dpr = jnp.where(valid, dpr, 0.0)
            dpi = jnp.where(valid, dpi, 0.0)
            dwa_ref[0] += _dot_tn(xc, dpr)
            dwx_ref[0] += _dot_tn(xc, dpi)
            dba_ref[...] += jnp.sum(dpr, axis=0, keepdims=True)
            dbx_ref[...] += jnp.sum(dpi, axis=0, keepdims=True)
            dx = jnp.zeros_like(dxc)
            dws = []
            for k in range(4):
                dx = dx + cw_ref[k:k + 1, :] * _shift_up(dxc, head, 3 - k)
                dws.append(jnp.sum(dxc * _shift_down(xv, tail, 3 - k), axis=0, keepdims=True))
            dx_ref[sl, :] = jnp.where(valid, dx, 0.0).astype(dx_ref.dtype)
            dcw_ref[...] += jnp.concatenate(dws, axis=0)
            dcb_ref[...] += jnp.sum(dxc, axis=0, keepdims=True)
            head = dxc[0:8]
        gin_ref[...] = jnp.broadcast_to(g_in, gin_ref.shape)
        head_ref[...] = head

    vec = pl.BlockSpec((1, BLK), lambda j, n: (0, j))
    mat = pl.BlockSpec((1, BLK, BLK), lambda j, n: (j, 0, 0))
    cws = pl.BlockSpec((4, BLK), lambda j, n: (0, j))

    def rb(off):
        return pl.BlockSpec((tb, BLK), lambda j, n: (nb - 1 - n, off + j))

    def tl(off):
        return pl.BlockSpec((8, BLK), lambda j, n: (jnp.maximum((nb - 1 - n) * t8 - 1, 0), off + j))

    return pl.pallas_call(
        kern, name=name, grid=(D_RNN // BLK, nb),
        in_specs=[rb(go), rb(xo), tl(xo), rb(0), tl(0), rb(do), cws, vec, mat, vec, mat, vec, vec],
        out_specs=[rb(0), rb(0), cws, vec, mat, vec, mat, vec, vec],
        out_shape=[jax.ShapeDtypeStruct((t, D_RNN), MXU), jax.ShapeDtypeStruct((t, D_RNN), MXU),
                   jax.ShapeDtypeStruct((4, D_RNN), F32), jax.ShapeDtypeStruct((1, D_RNN), F32),
                   jax.ShapeDtypeStruct((8, BLK, BLK), F32), jax.ShapeDtypeStruct((1, D_RNN), F32),
                   jax.ShapeDtypeStruct((8, BLK, BLK), F32), jax.ShapeDtypeStruct((1, D_RNN), F32),
                   jax.ShapeDtypeStruct((1, D_RNN), F32)],
        scratch_shapes=[pltpu.VMEM((8, BLK), F32), pltpu.VMEM((8, BLK), F32)],
        compiler_params=_cp("parallel", "arbitrary"),
    )(proj, proj, proj, h, h, dy, cw, cb.reshape(1, -1), wa, ba.reshape(1, -1), wx,
      bx.reshape(1, -1), lam.reshape(1, -1))


_SCALE = HEAD_DIM ** -0.5


STK = 4


def _attn_masks(n):
    qi = _rows((STK * BLK, 3 * BLK)) & (BLK - 1)
    c = _lanes((STK * BLK, 3 * BLK))
    tq = n * BLK + qi - PAD
    s_band = (n - 1) * BLK + c - PAD
    d_band = tq - s_band
    ok_band = (s_band >= N_META) & (d_band >= 0) & (d_band < BLK)
    jm = c - 2 * BLK
    d_meta = tq - (jm - PAD)
    ok_meta = (jm >= PAD) & (d_meta >= 0)
    is_band = c < 2 * BLK
    ok = (is_band & ok_band) | (jnp.logical_not(is_band) & ok_meta)
    dist = jnp.where(is_band, d_band, jnp.minimum(d_meta, BLK)).astype(F32)
    return ok, dist


def _stack_heads(g, e):
    return [8 * g + 2 * i + e for i in range(STK)]


def _stack_cols(heads, sk):
    slope = jnp.concatenate(
        [jnp.full((BLK, 1), 2.0 ** (-8.0 * (h + 1) / N_Q_HEADS), F32) for h in heads], axis=0)
    sink = jnp.concatenate(
        [jnp.broadcast_to(jnp.sum(jnp.where(_lanes(sk.shape) == h, sk, 0.0), axis=1, keepdims=True),
                          (BLK, 1)) for h in heads], axis=0)
    return slope, sink


def _stack_tiles(ref, g, sel):
    return jnp.concatenate(
        [jnp.where(sel, ref[:, (4 * g + i) * BLK:(4 * g + i + 1) * BLK].astype(F32), 0.0)
         for i in range(STK)], axis=0)


def _attn_probs(qm, kk, ok, dist, slope, sink):
    s = _dot_nt(qm, kk) * _SCALE - slope * dist
    s = jnp.where(ok, s, NEG)
    mx = jnp.maximum(jnp.max(s, axis=-1, keepdims=True), sink)
    p = jnp.exp(s - mx)
    es = jnp.exp(sink - mx)
    inv = 1.0 / (jnp.sum(p, axis=-1, keepdims=True) + es)
    return p * inv, es * inv


def _attn_specs(t, q_off, k_off, v_off, rev):
    nb = t // BLK
    qo, ko, vo = q_off // 1024, k_off // BLK, v_off // BLK

    def b(n):
        return nb - 1 - n if rev else n

    return [
        pl.BlockSpec((BLK, 1024), lambda n: (b(n), qo)),
        pl.BlockSpec((BLK, BLK), lambda n: (b(n), ko)),
        pl.BlockSpec((BLK, BLK), lambda n: (b(n), vo)),
        pl.BlockSpec((BLK, BLK), lambda n: (jnp.maximum(b(n) - 1, 0), ko)),
        pl.BlockSpec((BLK, BLK), lambda n: (jnp.maximum(b(n) - 1, 0), vo)),
        pl.BlockSpec((BLK, BLK), lambda n: (0, ko)),
        pl.BlockSpec((BLK, BLK), lambda n: (0, vo)),
        pl.BlockSpec((1, BLK), lambda n: (0, 0)),
    ]


def attn_fwd(proj, sinks, *, q_off, k_off, v_off, name, carry=None):
    t = proj.shape[0]
    nb = t // BLK

    def kern(q_ref, kc_ref, vc_ref, kp_ref, vp_ref, km_ref, vm_ref, sk_ref, o_ref):
        n = pl.program_id(0)
        ok, dist = _attn_masks(n)
        k_all = jnp.concatenate([kp_ref[...], kc_ref[...], km_ref[...]], axis=0)
        v_all = jnp.concatenate([vp_ref[...], vc_ref[...], vm_ref[...]], axis=0)
        k_alt = pltpu.roll(k_all, HEAD_DIM, 1)
        v_alt = pltpu.roll(v_all, HEAD_DIM, 1)
        low = _lanes((BLK, BLK)) < HEAD_DIM
        outs = {}
        for g in range(2):
            for e in range(2):
                qm = _stack_tiles(q_ref, g, low == (e == 0))
                kk = k_all if g == e else k_alt
                vv = v_all if g == e else v_alt
                slope, sink = _stack_cols(_stack_heads(g, e), sk_ref[...])
                p, _ = _attn_probs(qm, kk, ok, dist, slope, sink)
                outs[g, e] = _dot(p, vv)
        for hp in range(N_Q_HEADS // 2):
            g, rs = hp // STK, slice((hp % STK) * BLK, (hp % STK + 1) * BLK)
            o_ref[:, hp * BLK:(hp + 1) * BLK] = jnp.where(low, outs[g, 0][rs], outs[g, 1][rs]).astype(o_ref.dtype)

    sk = jnp.zeros((1, BLK), F32).at[0, :N_Q_HEADS].set(sinks)
    return _call(
        kern, (proj, proj, proj, proj, proj, proj, proj, sk), name=name, grid=(nb,),
        in_specs=_attn_specs(t, q_off, k_off, v_off, False),
        out_specs=[pl.BlockSpec((BLK, 1024), lambda n: (n, 0))],
        out_shape=[jax.ShapeDtypeStruct((t, 1024), MXU)],
        sem=("parallel",), carry=carry)


def attn_bwd(proj, sinks, dy, *, q_off, k_off, v_off, dy_off, name, carry=None):
    t = proj.shape[0]
    nb = t // BLK
    do = dy_off // 1024

    def kern(q_ref, kc_ref, vc_ref, kp_ref, vp_ref, km_ref, vm_ref, sk_ref, do_ref,
             dq_ref, dk_ref, dv_ref, dsk_ref, ck_ref, cv_ref, mk_ref, mv_ref):
        n = pl.program_id(0)
        blk = nb - 1 - n

        @pl.when(n == 0)
        def _():
            for r_ in (ck_ref, cv_ref, mk_ref, mv_ref, dsk_ref):
                r_[...] = jnp.zeros_like(r_)

        ok, dist = _attn_masks(blk)
        k_all = jnp.concatenate([kp_ref[...], kc_ref[...], km_ref[...]], axis=0)
        v_all = jnp.concatenate([vp_ref[...], vc_ref[...], vm_ref[...]], axis=0)
        k_alt = pltpu.roll(k_all, HEAD_DIM, 1)
        v_alt = pltpu.roll(v_all, HEAD_DIM, 1)
        low = _lanes((BLK, BLK)) < HEAD_DIM
        lane1 = _lanes((1, BLK))
        dk_all = jnp.zeros((3 * BLK, BLK), F32)
        dv_all = jnp.zeros((3 * BLK, BLK), F32)
        dsk = jnp.zeros((1, BLK), F32)
        dqs = {}
        for g in range(2):
            for e in range(2):
                sel = low == (e == 0)
                heads = _stack_heads(g, e)
                qm = _stack_tiles(q_ref, g, sel)
                dom = _stack_tiles(do_ref, g, sel)
                kk = k_all if g == e else k_alt
                vv = v_all if g == e else v_alt
                slope, sink = _stack_cols(heads, sk_ref[...])
                p, psink = _attn_probs(qm, kk, ok, dist, slope, sink)
                dp = _dot_nt(dom, vv)
                delta = jnp.sum(p * dp, axis=-1, keepdims=True)
                ds = p * (dp - delta) * _SCALE
                psd = psink * delta
                for i, h in enumerate(heads):
                    dsk = dsk + jnp.where(lane1 == h, -jnp.sum(psd[i * BLK:(i + 1) * BLK], axis=0, keepdims=True), 0.0)
                dqs[g, e] = _dot(ds, kk)
                dkh = _dot_tn(ds, qm)
                dvh = _dot_tn(p, dom)
                if g != e:
                    dkh = pltpu.roll(dkh, HEAD_DIM, 1)
                    dvh = pltpu.roll(dvh, HEAD_DIM, 1)
                dk_all = dk_all + dkh
                dv_all = dv_all + dvh
        for hp in range(N_Q_HEADS // 2):
            g, rs = hp // STK, slice((hp % STK) * BLK, (hp % STK + 1) * BLK)
            dq_ref[:, hp * BLK:(hp + 1) * BLK] = jnp.where(low, dqs[g, 0][rs], dqs[g, 1][rs]).astype(dq_ref.dtype)
        dsk_ref[...] += dsk
        mk_ref[...] += dk_all[2 * BLK:3 * BLK]
        mv_ref[...] += dv_all[2 * BLK:3 * BLK]
        is0 = blk == 0
        dk_ref[...] = (dk_all[BLK:2 * BLK] + ck_ref[...] + jnp.where(is0, mk_ref[...], 0.0)).astype(dk_ref.dtype)
        dv_ref[...] = (dv_all[BLK:2 * BLK] + cv_ref[...] + jnp.where(is0, mv_ref[...], 0.0)).astype(dv_ref.dtype)
        ck_ref[...] = dk_all[0:BLK]
        cv_ref[...] = dv_all[0:BLK]

    sk = jnp.zeros((1, BLK), F32).at[0, :N_Q_HEADS].set(sinks)
    kv = pl.BlockSpec((BLK, BLK), lambda n: (nb - 1 - n, 0))
    res = _call(
        kern, (proj, proj, proj, proj, proj, proj, proj, sk, dy), name=name, grid=(nb,),
        in_specs=_attn_specs(t, q_off, k_off, v_off, True)
        + [pl.BlockSpec((BLK, 1024), lambda n: (nb - 1 - n, do))],
        out_specs=[pl.BlockSpec((BLK, 1024), lambda n: (nb - 1 - n, 0)), kv, kv,
                   pl.BlockSpec((1, BLK), lambda n: (0, 0))],
        out_shape=[jax.ShapeDtypeStruct((t, 1024), MXU), jax.ShapeDtypeStruct((t, BLK), MXU),
                   jax.ShapeDtypeStruct((t, BLK), MXU), jax.ShapeDtypeStruct((1, BLK), F32)],
        scratch_shapes=[pltpu.VMEM((BLK, BLK), F32)] * 4,
        sem=("arbitrary",), carry=carry)
    return [res[0], res[1], res[2], res[3][0, :N_Q_HEADS]] + res[4:]


GW = D_SSM // SSD_GROUPS
EXP_ROWS = 3 * BLK + 8
RED_ROWS = EXP_ROWS + 8


def _head_expand():
    ch = jnp.arange(D_SSM) // HEAD_DIM
    return (jnp.arange(BLK)[:, None] == ch[None, :]).astype(BF16)


def _ssd_decay(raw, dtb, alog, rowv):
    valid = rowv & (_lanes((BLK, BLK)) < SSD_HEADS)
    pre = raw + dtb
    dtp = jnp.where(valid, _softplus(pre), 0.0)
    av = -jnp.exp(alog)
    cs = _cumsum_fwd(dtp * av)
    cs_last = _row_at(cs, BLK - 1)
    return valid, pre, dtp, av, cs, jnp.exp(cs), jnp.exp(cs_last - cs), jnp.exp(cs_last)


def _head_col(x, h):
    return jnp.sum(jnp.where(_lanes(x.shape) == h, x, 0.0), axis=1, keepdims=True)


def _ssd_group_fwd(g, xdt, cs, cst, bg, cg, tril, low):
    cb = _dot_nt(cg, bg)
    ys, lm = [], []
    for j in range(2):
        xp = xdt[:, g * GW + j * BLK:g * GW + (j + 1) * BLK]
        hv = []
        for e in range(2):
            h = 4 * g + 2 * j + e
            seg = _head_col(cs, h) - _row_at(cst, h)
            lmat = jnp.where(tril, jnp.exp(jnp.minimum(seg, 0.0)), 0.0)
            mmat = cb * lmat
            lm.append((lmat, mmat))
            hv.append(_dot(mmat, xp))
        ys.append(jnp.where(low, hv[0], hv[1]))
    return jnp.concatenate(ys, axis=1), lm


def _ssd_specs(t, z_off, dt_off, rev):
    nb = t // BLK
    zo, dto = z_off // D_SSM, dt_off // BLK

    def b(n):
        return nb - 1 - n if rev else n

    vec = lambda w: pl.BlockSpec((1, w), lambda n: (0, 0))
    return [
        pl.BlockSpec((BLK, D_SSM), lambda n: (b(n), 0)),
        pl.BlockSpec((BLK, 1024), lambda n: (b(n), 2)),
        pl.BlockSpec((BLK, 1024), lambda n: (b(n), 3)),
        pl.BlockSpec((BLK, D_SSM), lambda n: (b(n), zo)),
        pl.BlockSpec((BLK, BLK), lambda n: (b(n), dto)),
        vec(BLK), vec(BLK), vec(D_SSM), vec(D_SSM),
        pl.BlockSpec((BLK, D_SSM), lambda n: (0, 0)),
    ]


def _pad128(v):
    return jnp.zeros((1, BLK), F32).at[0, :v.shape[0]].set(v)


def ssd_fwd(xbc, proj, dt_bias, a_log, d_skip, gate_norm, *, z_off, dt_off, name):
    t = xbc.shape[0]
    nb = t // BLK

    def kern(x_ref, b_ref, c_ref, z_ref, dt_ref, dtb_ref, alog_ref, dsk_ref, gn_ref, e_ref,
             yn_ref, ynt_ref, st_ref, p_ref):
        n = pl.program_id(0)

        @pl.when(n == 0)
        def _():
            p_ref[...] = jnp.zeros_like(p_ref)

        rowv = (n * BLK + _rows((BLK, BLK))) >= PAD
        _, _, dtp, _, cs, ecs, w, dec = _ssd_decay(dt_ref[...], dtb_ref[...], alog_ref[...], rowv)
        ex = _dot_split(jnp.concatenate([dtp, ecs, w, jnp.broadcast_to(dec, (8, BLK))], axis=0),
                        e_ref[...])
        dtp_c, ecs_c, w_c = ex[0:BLK], ex[BLK:2 * BLK], ex[2 * BLK:3 * BLK]
        dec_c = jnp.max(ex[3 * BLK:EXP_ROWS], axis=0, keepdims=True)
        xv = x_ref[...]
        xdt = xv * dtp_c
        wx = w_c * xdt
        cst = cs.T
        tril = _rows((BLK, BLK)) >= _lanes((BLK, BLK))
        low = _lanes((BLK, BLK)) < HEAD_DIM
        st_ref[0] = p_ref[...]
        for g in range(SSD_GROUPS):
            gs = slice(g * GW, (g + 1) * GW)
            bg = b_ref[:, g * BLK:(g + 1) * BLK]
            cg = c_ref[:, g * BLK:(g + 1) * BLK]
            pg = p_ref[g]
            ydiag, _ = _ssd_group_fwd(g, xdt, cs, cst, bg, cg, tril, low)
            y = ydiag + _dot(cg, pg) * ecs_c[:, gs] + dsk_ref[:, gs] * xv[:, gs]
            p_ref[g] = pg * dec_c[:, gs] + _dot_tn(bg, wx[:, gs])
            yz = y * _silu(z_ref[:, gs])
            r = lax.rsqrt(jnp.mean(yz * yz, axis=-1, keepdims=True) + EPS)
            yn = yz * r * gn_ref[:, gs]
            yn_ref[:, gs] = yn.astype(yn_ref.dtype)
            ynt_ref[gs, :] = yn.T.astype(ynt_ref.dtype)

    return pl.pallas_call(
        kern, name=name, grid=(nb,),
        in_specs=_ssd_specs(t, z_off, dt_off, False),
        out_specs=[pl.BlockSpec((BLK, D_SSM), lambda n: (n, 0)),
                   pl.BlockSpec((D_SSM, BLK), lambda n: (0, n)),
                   pl.BlockSpec((1, SSD_GROUPS, BLK, GW), lambda n: (n, 0, 0, 0))],
        out_shape=[jax.ShapeDtypeStruct((t, D_SSM), MXU), jax.ShapeDtypeStruct((D_SSM, t), MXU),
                   jax.ShapeDtypeStruct((nb, SSD_GROUPS, BLK, GW), F32)],
        scratch_shapes=[pltpu.VMEM((SSD_GROUPS, BLK, GW), F32)],
        compiler_params=_cp("arbitrary"),
    )(xbc, xbc, xbc, proj, proj, _pad128(dt_bias), _pad128(a_log),
      jnp.repeat(d_skip, HEAD_DIM).reshape(1, D_SSM), gate_norm.reshape(1, D_SSM), _head_expand())


def ssd_bwd(xbc, proj, st, dyn, dt_bias, a_log, d_skip, gate_norm, *, z_off, dt_off, name, carry=None):
    t = xbc.shape[0]
    nb = t // BLK

    def kern(x_ref, b_ref, c_ref, z_ref, dt_ref, dtb_ref, alog_ref, dsk_ref, gn_ref, e_ref,
             et_ref, st_ref, dyn_ref,
             dxbc_ref, dz_ref, draw_ref, dgn_ref, ddsk_ref, ddtb_ref, dalog_ref,
             dp_ref, tr_ref):
        n = pl.program_id(0)
        blk = nb - 1 - n

        @pl.when(n == 0)
        def _():
            for r_ in (dp_ref, dgn_ref, ddsk_ref, ddtb_ref, dalog_ref):
                r_[...] = jnp.zeros_like(r_)

        rowv = (blk * BLK + _rows((BLK, BLK))) >= PAD
        valid, pre, dtp, av, cs, ecs, w, dec = _ssd_decay(dt_ref[...], dtb_ref[...], alog_ref[...], rowv)
        ex = _dot_split(jnp.concatenate([dtp, ecs, w, jnp.broadcast_to(dec, (8, BLK))], axis=0),
                        e_ref[...])
        dtp_c, ecs_c, w_c = ex[0:BLK], ex[BLK:2 * BLK], ex[2 * BLK:3 * BLK]
        dec_c = jnp.max(ex[3 * BLK:EXP_ROWS], axis=0, keepdims=True)
        xv = x_ref[...]
        xdt = xv * dtp_c
        wx = w_c * xdt
        cst = cs.T
        tril = _rows((BLK, BLK)) >= _lanes((BLK, BLK))
        lane = _lanes((BLK, BLK))
        rowi = _rows((BLK, BLK))
        low = lane < HEAD_DIM
        dcs = jnp.zeros((BLK, BLK), F32)
        dcst = jnp.zeros((BLK, BLK), F32)
        for g in range(SSD_GROUPS):
            gs = slice(g * GW, (g + 1) * GW)
            bg = b_ref[:, g * BLK:(g + 1) * BLK]
            cg = c_ref[:, g * BLK:(g + 1) * BLK]
            pg = st_ref[0, g]
            dpn = dp_ref[g]
            xg = xv[:, gs]
            ydiag, lm = _ssd_group_fwd(g, xdt, cs, cst, bg, cg, tril, low)
            yoff = _dot(cg, pg) * ecs_c[:, gs]
            y = ydiag + yoff + dsk_ref[:, gs] * xg
            zz = z_ref[:, gs]
            sz = _silu(zz)
            yz = y * sz
            r = lax.rsqrt(jnp.mean(yz * yz, axis=-1, keepdims=True) + EPS)
            yhat = yz * r
            dynv = dyn_ref[:, gs].astype(F32)
            gy = dynv * gn_ref[:, gs]
            dgn_ref[:, gs] += jnp.sum(dynv * yhat, axis=0, keepdims=True)
            dyz = r * (gy - yhat * jnp.mean(gy * yhat, axis=-1, keepdims=True))
            dy = dyz * sz
            dz_ref[:, gs] = (dyz * y * _silu_grad(zz)).astype(dz_ref.dtype)
            tr_ref[EXP_ROWS:RED_ROWS, gs] = jnp.broadcast_to(
                jnp.sum(dy * xg, axis=0, keepdims=True), (8, GW))
            dx = dsk_ref[:, gs] * dy
            dwx = _dot(bg, dpn)
            dxdt = w_c[:, gs] * dwx
            tr_ref[0:BLK, gs] = dwx * wx[:, gs]
            dbg = _dot_nt(wx[:, gs], dpn)
            dzo = ecs_c[:, gs] * dy
            tr_ref[BLK:2 * BLK, gs] = dy * yoff
            dcg = _dot_nt(dzo, pg)
            dp_ref[g] = dec_c[:, gs] * dpn + _dot_tn(cg, dzo)
            tr_ref[3 * BLK:EXP_ROWS, gs] = jnp.broadcast_to(
                jnp.sum(dpn * pg, axis=0, keepdims=True), (8, GW))
            dcb = jnp.zeros((BLK, BLK), F32)
            pairs = []
            for j in range(2):
                ps = slice(g * GW + j * BLK, g * GW + (j + 1) * BLK)
                xp = xdt[:, ps]
                dyp = dy[:, j * BLK:(j + 1) * BLK]
                acc = jnp.zeros((BLK, BLK), F32)
                for e in range(2):
                    h = 4 * g + 2 * j + e
                    lmat, mmat = lm[2 * j + e]
                    dyh = jnp.where(low == (e == 0), dyp, 0.0)
                    dm = jnp.where(tril, _dot_nt(dyh, xp), 0.0)
                    nh = dm * mmat
                    dcs = dcs + jnp.where(lane == h, jnp.sum(nh, axis=1, keepdims=True), 0.0)
                    dcst = dcst - jnp.where(rowi == h, jnp.sum(nh, axis=0, keepdims=True), 0.0)
                    dcb = dcb + dm * lmat
                    acc = acc + _dot_tn(mmat, dyh)
                pairs.append(acc)
            dxdt = dxdt + jnp.concatenate(pairs, axis=1)
            dcg = dcg + _dot(dcb, bg)
            dbg = dbg + _dot_tn(dcb, cg)
            tr_ref[2 * BLK:3 * BLK, gs] = dxdt * xg
            dxbc_ref[:, gs] = dx + dxdt * dtp_c[:, gs]
            dxbc_ref[:, D_SSM + g * BLK:D_SSM + (g + 1) * BLK] = dbg
            dxbc_ref[:, D_SSM + 1024 + g * BLK:D_SSM + 1024 + (g + 1) * BLK] = dcg
        red = _dot(tr_ref[...], et_ref[...])
        r1, r2, r3 = red[0:BLK], red[BLK:2 * BLK], red[2 * BLK:3 * BLK]
        ddec = jnp.max(red[3 * BLK:EXP_ROWS], axis=0, keepdims=True)
        ddsk_ref[...] += jnp.max(red[EXP_ROWS:RED_ROWS], axis=0, keepdims=True)
        dcs = dcs + dcst.T - r1 + r2
        dcs_last = jnp.sum(r1, axis=0, keepdims=True) + ddec * dec
        dcs = dcs + jnp.where(rowi == BLK - 1, dcs_last, 0.0)
        dda = _cumsum_rev(dcs)
        ddtp = r3 + dda * av
        dalog_ref[...] += jnp.sum(dda * dtp, axis=0, keepdims=True) * av
        draw = jnp.where(valid, ddtp * _sigmoid(pre), 0.0)
        ddtb_ref[...] += jnp.sum(draw, axis=0, keepdims=True)
        draw_ref[...] = draw.astype(draw_ref.dtype)

    vec = lambda w_: pl.BlockSpec((1, w_), lambda n: (0, 0))
    rb = lambda w_: pl.BlockSpec((BLK, w_), lambda n: (nb - 1 - n, 0))
    e = _head_expand()
    res = _call(
        kern, (xbc, xbc, xbc, proj, proj, _pad128(dt_bias), _pad128(a_log),
               jnp.repeat(d_skip, HEAD_DIM).reshape(1, D_SSM), gate_norm.reshape(1, D_SSM), e, e.T, st, dyn),
        name=name, grid=(nb,),
        in_specs=_ssd_specs(t, z_off, dt_off, True)
        + [pl.BlockSpec((D_SSM, BLK), lambda n: (0, 0)),
           pl.BlockSpec((1, SSD_GROUPS, BLK, GW), lambda n: (nb - 1 - n, 0, 0, 0)),
           rb(D_SSM)],
        out_specs=[rb(2 * D_SSM), rb(D_SSM), rb(BLK), vec(D_SSM), vec(BLK), vec(BLK), vec(BLK)],
        out_shape=[jax.ShapeDtypeStruct((t, 2 * D_SSM), F32), jax.ShapeDtypeStruct((t, D_SSM), MXU),
                   jax.ShapeDtypeStruct((t, BLK), MXU), jax.ShapeDtypeStruct((1, D_SSM), F32),
                   jax.ShapeDtypeStruct((1, BLK), F32), jax.ShapeDtypeStruct((1, BLK), F32),
                   jax.ShapeDtypeStruct((1, BLK), F32)],
        scratch_shapes=[pltpu.VMEM((SSD_GROUPS, BLK, GW), F32), pltpu.VMEM((RED_ROWS, D_SSM), F32)],
        sem=("arbitrary",), carry=carry)
    dxbc, dz, draw, dgn, ddsk, ddtb, dalog = res[:7]
    return [dxbc, dz, draw, dgn[0], ddsk[0, :SSD_HEADS], ddtb[0, :SSD_HEADS], dalog[0, :SSD_HEADS]] + res[7:]


def loss_fwd_bwd(h, target, *, name):
    t, d = h.shape
    nb = t // BLK

    def kern(h_ref, t_ref, loss_ref, dh_ref):
        n = pl.program_id(0)
        err = jnp.where(n > 0, h_ref[...] - t_ref[...], 0.0)
        dh_ref[...] = err * (1.0 / d)
        part = (0.5 / d) * jnp.sum(jnp.sum(err * err, axis=1, keepdims=True), axis=0, keepdims=True)

        @pl.when(n == 0)
        def _():
            loss_ref[...] = part

        @pl.when(n > 0)
        def _():
            loss_ref[...] += part

    return pl.pallas_call(
        kern, name=name, grid=(nb,),
        in_specs=[pl.BlockSpec((BLK, d), lambda n: (n, 0)),
                  pl.BlockSpec((BLK, d), lambda n: (jnp.maximum(n - 1, 0), 0))],
        out_specs=[pl.BlockSpec((1, 1), lambda n: (0, 0)), pl.BlockSpec((BLK, d), lambda n: (n, 0))],
        out_shape=[jax.ShapeDtypeStruct((1, 1), F32), jax.ShapeDtypeStruct((t, d), F32)],
        compiler_params=_cp("arbitrary"),
    )(h, target)


def _ew_tile(r, c):
    cap = max(16, (256 * 1024) // c)
    best = None
    for tr in range(16, min(r, cap) + 1, 16):
        if r % tr == 0:
            best = tr
    return best if best is not None else r


def adamw(parts, w, m, v, *, name):
    npart, r, c = parts.shape
    tr = _ew_tile(r, c)

    def kern(p_ref, w_ref, m_ref, v_ref, g_ref, d_ref, m2_ref, v2_ref):
        g = p_ref[0].astype(F32)
        for k in range(1, npart):
            g = g + p_ref[k].astype(F32)
        m2 = ADAM_B1 * m_ref[...] + (1.0 - ADAM_B1) * g
        v2 = ADAM_B2 * v_ref[...] + (1.0 - ADAM_B2) * (g * g)
        m_hat = m2 / (1.0 - ADAM_B1 ** ADAM_STEP)
        v_hat = v2 / (1.0 - ADAM_B2 ** ADAM_STEP)
        g_ref[...] = g
        d_ref[...] = -ADAM_LR * (m_hat / (jnp.sqrt(v_hat) + ADAM_EPS) + ADAM_WD * w_ref[...])
        m2_ref[...] = m2
        v2_ref[...] = v2

    row = pl.BlockSpec((tr, c), lambda i: (i, 0))
    sds = jax.ShapeDtypeStruct((r, c), F32)
    return pl.pallas_call(
        kern, name=name, grid=(r // tr,),
        in_specs=[pl.BlockSpec((npart, tr, c), lambda i: (0, i, 0)), row, row, row],
        out_specs=[row, row, row, row], out_shape=[sds, sds, sds, sds],
        compiler_params=_cp("parallel"),
    )(parts, w, m, v)


def pair_add(p, land, *, name):
    _, r, c = p.shape
    tr = _ew_tile(r, c)
    core = lax.axis_index("c").astype(jnp.int32).reshape(1)

    def kern(c_ref, p_ref, l_ref, o_ref):
        o_ref[...] = (p_ref[...] + l_ref[...]).astype(o_ref.dtype)

    return pl.pallas_call(
        kern, name=name,
        grid_spec=pltpu.PrefetchScalarGridSpec(
            num_scalar_prefetch=1, grid=(4, r // tr),
            in_specs=[pl.BlockSpec((1, tr, c), lambda k, i, c_ref: (2 * k + c_ref[0], i, 0)),
                      pl.BlockSpec((1, tr, c), lambda k, i, c_ref: (k, i, 0))],
            out_specs=pl.BlockSpec((1, tr, c), lambda k, i, c_ref: (k, i, 0))),
        out_shape=jax.ShapeDtypeStruct((4, r, c), BF16),
        compiler_params=_cp("parallel", "parallel"),
    )(core, p, land)


def _me():
    return lax.axis_index("x"), lax.axis_index("y"), lax.axis_index("c")


def all_gather(xs, *, name):
    n = len(xs)

    def body(*refs):
        x_refs, out_refs = refs[:n], refs[n:2 * n]
        send_sems, recv_sems, local_sems = refs[2 * n:]
        mx, my, mc = _me()
        me, sib = (mx, my, mc), (mx, my, 1 - mc)
        chips = [(1 - mx, my), (mx, 1 - my), (1 - mx, 1 - my)]

        def rows(i, px, py, pc):
            return out_refs[i].at[4 * px + 2 * py + pc]

        def copy(i, k, block, to, src=None):
            return pltpu.make_async_remote_copy(
                src_ref=rows(i, *block) if src is None else src, dst_ref=rows(i, *block),
                send_sem=send_sems.at[7 * i + k], recv_sem=recv_sems.at[7 * i + k],
                device_id=to, device_id_type=MESH)

        mine = [pltpu.make_async_copy(x_refs[i], rows(i, *me), local_sems.at[i]) for i in range(n)]
        first = []
        for i in range(n):
            mine[i].start()
            first.append(copy(i, 0, me, sib, src=x_refs[i]))
            first += [copy(i, 1 + j, me, (*chip, mc), src=x_refs[i]) for j, chip in enumerate(chips)]
        for cp in first:
            cp.start()
        passed = []
        for i in range(n):
            for j, chip in enumerate(chips):
                copy(i, 1 + j, (*chip, mc), me).wait_recv()
                passed.append(copy(i, 4 + j, (*chip, mc), sib))
                passed[-1].start()
        for i in range(n):
            copy(i, 0, sib, me).wait_recv()
            for j, chip in enumerate(chips):
                copy(i, 4 + j, (*chip, 1 - mc), me).wait_recv()
        for cp in first + passed:
            cp.wait_send()
        for cp in mine:
            cp.wait()

    return pl.pallas_call(
        body, name=name,
        out_shape=[jax.ShapeDtypeStruct((N_DEV,) + x.shape, x.dtype) for x in xs],
        in_specs=[ANY] * n, out_specs=[ANY] * n,
        scratch_shapes=[pltpu.SemaphoreType.DMA((7 * n,)), pltpu.SemaphoreType.DMA((7 * n,)),
                        pltpu.SemaphoreType.DMA((n,))],
    )(*xs)


def pair_exchange(ps, *, name):
    n = len(ps)

    def body(*refs):
        p_refs, out_refs = refs[:n], refs[n:2 * n]
        send_sems, recv_sems = refs[2 * n:]
        mx, my, mc = _me()
        cps = [pltpu.make_async_remote_copy(
            src_ref=p_refs[i].at[2 * k + (1 - mc)], dst_ref=out_refs[i].at[k],
            send_sem=send_sems.at[4 * i + k], recv_sem=recv_sems.at[4 * i + k],
            device_id=(mx, my, 1 - mc), device_id_type=MESH) for i in range(n) for k in range(4)]
        for cp in cps:
            cp.start()
        for cp in cps:
            cp.wait_recv()
        for cp in cps:
            cp.wait_send()

    return pl.pallas_call(
        body, name=name,
        out_shape=[jax.ShapeDtypeStruct((4,) + p.shape[1:], p.dtype) for p in ps],
        in_specs=[ANY] * n, out_specs=[ANY] * n,
        scratch_shapes=[pltpu.SemaphoreType.DMA((4 * n,)), pltpu.SemaphoreType.DMA((4 * n,))],
    )(*ps)


def chip_exchange(qs, *, name):
    n = len(qs)

    def body(*refs):
        q_refs, out_refs = refs[:n], refs[n:2 * n]
        send_sems, recv_sems, local_sems = refs[2 * n:]
        mx, my, mc = _me()
        mine = 2 * mx + my
        chips = [(1 - mx, my), (mx, 1 - my), (1 - mx, 1 - my)]
        local, sends, recvs = [], [], []
        for i in range(n):
            local.append(pltpu.make_async_copy(q_refs[i].at[mine], out_refs[i].at[mine], local_sems.at[i]))
            for k, (px, py) in enumerate(chips):
                sems = dict(send_sem=send_sems.at[3 * i + k], recv_sem=recv_sems.at[3 * i + k],
                            device_id=(px, py, mc), device_id_type=MESH)
                sends.append(pltpu.make_async_remote_copy(
                    src_ref=q_refs[i].at[2 * px + py], dst_ref=out_refs[i].at[mine], **sems))
                recvs.append(pltpu.make_async_remote_copy(
                    src_ref=q_refs[i].at[mine], dst_ref=out_refs[i].at[2 * px + py], **sems))
        for cp in local + sends:
            cp.start()
        for cp in recvs:
            cp.wait_recv()
        for cp in sends:
            cp.wait_send()
        for cp in local:
            cp.wait()

    return pl.pallas_call(
        body, name=name,
        out_shape=[jax.ShapeDtypeStruct(q.shape, q.dtype) for q in qs],
        in_specs=[ANY] * n, out_specs=[ANY] * n,
        scratch_shapes=[pltpu.SemaphoreType.DMA((3 * n,)), pltpu.SemaphoreType.DMA((3 * n,)),
                        pltpu.SemaphoreType.DMA((n,))],
    )(*qs)


class _Carry:
    def __init__(self, inputs, out_shapes, sems, start, finish):
        self.inputs, self.out_shapes, self.sems = list(inputs), list(out_shapes), list(sems)
        self.start, self.finish = start, finish


def _call(kern, args, *, name, grid, in_specs, out_specs, out_shape, scratch_shapes=(), sem, carry=None):
    in_specs, out_specs, out_shape = list(in_specs), list(out_specs), list(out_shape)
    scratch_shapes = list(scratch_shapes)
    if carry is None:
        return list(pl.pallas_call(
            kern, name=name, grid=grid, in_specs=in_specs, out_specs=out_specs, out_shape=out_shape,
            scratch_shapes=scratch_shapes, compiler_params=_cp(*sem))(*args))
    ni, no, ns = len(in_specs), len(out_specs), len(scratch_shapes)
    ci, co = len(carry.inputs), len(carry.out_shapes)

    def body(*refs):
        o0 = ni + ci
        s0 = o0 + no + co
        ids = [pl.program_id(d) for d in range(len(grid))]
        first = functools.reduce(jnp.logical_and, [i == 0 for i in ids])
        last = functools.reduce(jnp.logical_and, [i == g - 1 for i, g in zip(ids, grid)])
        cin, cout, sems = refs[ni:o0], refs[o0 + no:s0], refs[s0 + ns:]

        @pl.when(first)
        def _():
            carry.start(cin, cout, sems)

        kern(*refs[:ni], *refs[o0:o0 + no], *refs[s0:s0 + ns])

        @pl.when(last)
        def _():
            carry.finish(cin, cout, sems)

    return list(pl.pallas_call(
        body, name=name, grid=grid, in_specs=in_specs + [ANY] * ci, out_specs=out_specs + [ANY] * co,
        out_shape=out_shape + carry.out_shapes, scratch_shapes=scratch_shapes + carry.sems,
        compiler_params=_cp(*(["arbitrary"] * len(grid))))(*args, *carry.inputs))


def gather_carry(xs):
    n = len(xs)

    def copies(cin, cout, sems, with_recv=True):
        mx, my, mc = _me()
        me = 4 * mx + 2 * my + mc
        peers = [(mx, my, 1 - mc), (1 - mx, my, mc), (mx, 1 - my, mc), (1 - mx, 1 - my, mc)]
        local, send, recv = [], [], []
        for i in range(n):
            local.append(pltpu.make_async_copy(cin[i], cout[i].at[me], sems[2].at[i]))
            for k, peer in enumerate(peers):
                common = dict(send_sem=sems[0].at[4 * i + k], recv_sem=sems[1].at[4 * i + k],
                              device_id=peer, device_id_type=MESH)
                send.append(pltpu.make_async_remote_copy(src_ref=cin[i], dst_ref=cout[i].at[me], **common))
                if with_recv:
                    recv.append(pltpu.make_async_remote_copy(
                        src_ref=cin[i], dst_ref=cout[i].at[4 * peer[0] + 2 * peer[1] + peer[2]], **common))
        return local, send, recv

    def start(cin, cout, sems):
        local, send, _ = copies(cin, cout, sems, with_recv=False)
        for cp in local + send:
            cp.start()

    def finish(cin, cout, sems):
        local, send, recv = copies(cin, cout, sems)
        for cp in recv:
            cp.wait_recv()
        for cp in send:
            cp.wait_send()
        for cp in local:
            cp.wait()

    return _Carry(xs, [jax.ShapeDtypeStruct((N_DEV,) + x.shape, x.dtype) for x in xs],
                  [pltpu.SemaphoreType.DMA((4 * n,)), pltpu.SemaphoreType.DMA((4 * n,)),
                   pltpu.SemaphoreType.DMA((n,))], start, finish)


def gather_relay(outs, *, name):
    n = len(outs)

    def body(*refs):
        bufs = refs[n:2 * n]
        send_sems, recv_sems = refs[2 * n:]
        mx, my, mc = _me()
        chips = [(1 - mx, my), (mx, 1 - my), (1 - mx, 1 - my)]
        send, recv = [], []
        for i in range(n):
            for j, (px, py) in enumerate(chips):
                common = dict(send_sem=send_sems.at[3 * i + j], recv_sem=recv_sems.at[3 * i + j],
                              device_id=(mx, my, 1 - mc), device_id_type=MESH)
                mine = bufs[i].at[4 * px + 2 * py + mc]
                send.append(pltpu.make_async_remote_copy(src_ref=mine, dst_ref=mine, **common))
                recv.append(pltpu.make_async_remote_copy(
                    src_ref=mine, dst_ref=bufs[i].at[4 * px + 2 * py + (1 - mc)], **common))
        for cp in send:
            cp.start()
        for cp in recv:
            cp.wait_recv()
        for cp in send:
            cp.wait_send()

    return pl.pallas_call(
        body, name=name, out_shape=[jax.ShapeDtypeStruct(o.shape, o.dtype) for o in outs],
        in_specs=[ANY] * n, out_specs=[ANY] * n, input_output_aliases={i: i for i in range(n)},
        scratch_shapes=[pltpu.SemaphoreType.DMA((3 * n,)), pltpu.SemaphoreType.DMA((3 * n,))],
    )(*outs)


def chip_carry(qs):
    n = len(qs)

    def copies(cin, cout, sems, with_recv=True):
        mx, my, mc = _me()
        mine = 2 * mx + my
        chips = [(1 - mx, my), (mx, 1 - my), (1 - mx, 1 - my)]
        local, send, recv = [], [], []
        for i in range(n):
            local.append(pltpu.make_async_copy(cin[i].at[mine], cout[i].at[mine], sems[2].at[i]))
            for k, (px, py) in enumerate(chips):
                common = dict(send_sem=sems[0].at[3 * i + k], recv_sem=sems[1].at[3 * i + k],
                              device_id=(px, py, mc), device_id_type=MESH)
                send.append(pltpu.make_async_remote_copy(
                    src_ref=cin[i].at[2 * px + py], dst_ref=cout[i].at[mine], **common))
                if with_recv:
                    recv.append(pltpu.make_async_remote_copy(
                        src_ref=cin[i].at[mine], dst_ref=cout[i].at[2 * px + py], **common))
        return local, send, recv

    def start(cin, cout, sems):
        local, send, _ = copies(cin, cout, sems, with_recv=False)
        for cp in local + send:
            cp.start()

    def finish(cin, cout, sems):
        local, send, recv = copies(cin, cout, sems)
        for cp in recv:
            cp.wait_recv()
        for cp in send:
            cp.wait_send()
        for cp in local:
            cp.wait()

    return _Carry(qs, [jax.ShapeDtypeStruct(q.shape, q.dtype) for q in qs],
                  [pltpu.SemaphoreType.DMA((3 * n,)), pltpu.SemaphoreType.DMA((3 * n,)),
                   pltpu.SemaphoreType.DMA((n,))], start, finish)


WEIGHTS = [
    "meta_tokens", "l0_mix_pre_norm", "l0_mix_post_norm", "l0_w_in", "l0_lru_conv_w", "l0_lru_conv_b",
    "l0_lru_w_a", "l0_lru_b_a", "l0_lru_w_x", "l0_lru_b_x", "l0_lru_lambda", "l0_attn_sinks", "l0_w_out",
    "l0_ffn_pre_norm", "l0_ffn_post_norm", "l0_ffn_w_up", "l0_ffn_conv_w", "l0_ffn_conv_b", "l0_ffn_w_down",
    "l1_mix_pre_norm", "l1_mix_post_norm", "l1_w_in", "l1_ssm_conv_w", "l1_ssm_conv_b", "l1_dt_bias",
    "l1_a_log", "l1_d_skip", "l1_gate_norm", "l1_w_out", "l1_ffn_pre_norm", "l1_ffn_post_norm",
    "l1_ffn_w_up", "l1_ffn_conv_w", "l1_ffn_conv_b", "l1_ffn_w_down",
]
INPUTS = (["x"] + WEIGHTS + ["loss_target"] + ["m_" + n for n in WEIGHTS] + ["v_" + n for n in WEIGHTS])

MATS = {"l0_w_in": ("col", (1024, 3328)), "l0_w_out": ("row", (2048, 1024)),
        "l0_ffn_w_up": ("col", (1024, 5632)), "l0_ffn_w_down": ("row", (2816, 1024)),
        "l1_w_in": ("col", (1024, 6176)), "l1_w_out": ("row", (2048, 1024)),
        "l1_ffn_w_up": ("col", (1024, 5632)), "l1_ffn_w_down": ("row", (2816, 1024))}
SMALL_SHARDED = {"meta_tokens": ("col", (16, 1024)), "l0_lru_conv_w": ("col", (4, 1024)),
                 "l0_ffn_conv_w": ("col", (3, 5632)), "l1_ssm_conv_w": ("col", (4, 4096)),
                 "l1_ffn_conv_w": ("col", (3, 5632))}
SHARDED = {**MATS, **SMALL_SHARDED}
REPLICATED = [n for n in WEIGHTS if n not in SHARDED]
PACK_W = 1024
SMALL_W = 128


def _shard_shape(name):
    kind, (r, c) = SHARDED[name]
    return (r, c // N_DEV) if kind == "col" else (r // N_DEV, c)


def _rows_of(numel, width):
    return -(-numel // width)


def _to_rows(a, width):
    flat = a.reshape(-1)
    rows = _rows_of(flat.shape[0], width)
    return jnp.pad(flat, (0, rows * width - flat.shape[0])).reshape(rows, width)


def _pack(arrs, width, total_rows):
    slab = jnp.concatenate([_to_rows(a, width) for a in arrs], axis=0)
    return jnp.pad(slab, ((0, total_rows - slab.shape[0]), (0, 0)))


def _unpack(slab, shapes, width):
    out, off = [], 0
    for shp in shapes:
        numel = math.prod(shp)
        rows = _rows_of(numel, width)
        out.append(slab[off:off + rows].reshape(-1)[:numel].reshape(shp))
        off += rows
    return out


def _round_up(n, m):
    return -(-n // m) * m


def _by_dest(name, g):
    kind, (r, c) = SHARDED[name]
    if kind == "col":
        return g.reshape(r, N_DEV, c // N_DEV).transpose(1, 0, 2)
    return g.reshape(N_DEV, r // N_DEV, c)


def _from_shards(name, blocks):
    kind, (r, c) = SHARDED[name]
    return blocks.transpose(1, 0, 2).reshape(r, c) if kind == "col" else blocks.reshape(r, c)


L1_IN_PAD = 6272


def _ffn_fwd(h, a, w, pfx):
    u, ut = rmsnorm_fwd(h, a[pfx + "ffn_pre_norm"], out_dtype=MXU, name=pfx + "ffn_pre", with_t=True)
    up = matmul(u, w[pfx + "ffn_w_up"], name=pfx + "ffn_up")
    act, act_t = dwconv_fwd(up, a[pfx + "ffn_conv_w"], a[pfx + "ffn_conv_b"], mode="geglu", x_off=0,
                            c_out=D_FF, cblk=256, out_dtype=MXU, name=pfx + "ffn_act", with_t=True)
    down = matmul(act, w[pfx + "ffn_w_down"], name=pfx + "ffn_down")
    out = rmsnorm_fwd(down, a[pfx + "ffn_post_norm"], res=h, out_dtype=F32, name=pfx + "ffn_post")
    return out, (h, ut, up, act_t, down)


def _ffn_bwd(dh, saved, a, w, pfx, g, carry=None):
    h, ut, up, act_t, down = saved
    dd, g[pfx + "ffn_post_norm"] = rmsnorm_bwd(down, a[pfx + "ffn_post_norm"], dh, out_dtype=MXU,
                                               name=pfx + "ffn_post_bwd")
    dact = matmul(dd, w[pfx + "ffn_w_down"], trans_b=True, name=pfx + "ffn_down_dx")
    g[pfx + "ffn_w_down"] = matmul(act_t, dd, name=pfx + "ffn_down_dw")
    dups, g[pfx + "ffn_conv_w"], g[pfx + "ffn_conv_b"], carried = dwconv_bwd(
        up, a[pfx + "ffn_conv_w"], a[pfx + "ffn_conv_b"], dact, mode="geglu", x_off=0, c_out=D_FF,
        cblk=256, name=pfx + "ffn_act_bwd", carry=carry)
    g[pfx + "ffn_w_up"] = jnp.concatenate(
        [matmul(ut, d, name=pfx + "ffn_up_dw%d" % i) for i, d in enumerate(dups)], axis=1)
    du = matmul_cat(dups, w[pfx + "ffn_w_up"], trans_b=True, name=pfx + "ffn_up_dx")
    dh_in, g[pfx + "ffn_pre_norm"] = rmsnorm_bwd(h, a[pfx + "ffn_pre_norm"], du, res=dh, out_dtype=F32,
                                                 name=pfx + "ffn_pre_bwd")
    return dh_in, carried


GATHER_EARLY = ["l0_w_out", "l0_ffn_w_up", "l0_ffn_w_down"]
GATHER_LATE = ["l1_w_in", "l1_w_out", "l1_ffn_w_up", "l1_ffn_w_down"]
RS_GROUPS = [["l1_ffn_w_down", "l1_ffn_w_up"],
             ["l1_w_out", "l1_w_in"],
             ["l0_ffn_w_down", "l0_ffn_w_up"]]


def _rs_pair_stage(names, g):
    parts = [_by_dest(n, g[n]) for n in names]
    from_sibling = pair_exchange(parts, name="rs_pair_" + names[0])
    return [pair_add(p, l, name="rs_pair_add_" + n) for n, p, l in zip(names, parts, from_sibling)]


def _local_step(a, w, shards):
    x = a["x"][0]
    seq = x.shape[0]
    h0 = jnp.concatenate([jnp.zeros((PAD, D_MODEL), F32), a["meta_tokens"], x], axis=0)
    g, landed = {}, {}

    u0, u0t = rmsnorm_fwd(h0, a["l0_mix_pre_norm"], out_dtype=MXU, name="l0_mix_pre", with_t=True)
    proj0 = matmul(u0, w["l0_w_in"], name="l0_in")
    lru = (a["l0_lru_conv_w"], a["l0_lru_conv_b"], a["l0_lru_w_a"], a["l0_lru_b_a"], a["l0_lru_w_x"],
           a["l0_lru_b_x"], a["l0_lru_lambda"])
    ya, ya_t, hl, *early = lru_fwd(proj0, *lru, gate_off=0, xr_off=1024, name="l0_lru",
                                   carry=gather_carry([shards[n] for n in GATHER_EARLY]))
    yb, *late = attn_fwd(proj0, a["l0_attn_sinks"], q_off=2048, k_off=3072, v_off=3200, name="l0_attn",
                         carry=gather_carry([shards[n] for n in GATHER_LATE]))
    relayed = gather_relay(early + late, name="gather_relay")
    w = dict(w, **{n: _from_shards(n, blocks) for n, blocks in zip(GATHER_EARLY + GATHER_LATE, relayed)})
    w["l1_w_in"] = jnp.pad(w["l1_w_in"], ((0, 0), (0, L1_IN_PAD - w["l1_w_in"].shape[1])))
    o0 = matmul_cat([ya, yb], w["l0_w_out"], name="l0_out")
    h1 = rmsnorm_fwd(o0, a["l0_mix_post_norm"], res=h0, out_dtype=F32, name="l0_mix_post")
    h2, ffn0 = _ffn_fwd(h1, a, w, "l0_")

    u2, u2t = rmsnorm_fwd(h2, a["l1_mix_pre_norm"], out_dtype=MXU, name="l1_mix_pre", with_t=True)
    proj1 = matmul(u2, w["l1_w_in"], name="l1_in")
    xbc = dwconv_fwd(proj1, a["l1_ssm_conv_w"], a["l1_ssm_conv_b"], mode="silu", x_off=D_SSM,
                     c_out=2 * D_SSM, cblk=512, out_dtype=F32, name="l1_ssm_conv")
    ssd = (a["l1_dt_bias"], a["l1_a_log"], a["l1_d_skip"], a["l1_gate_norm"])
    yn, yn_t, st = ssd_fwd(xbc, proj1, *ssd, z_off=0, dt_off=3 * D_SSM, name="l1_ssd")
    o1 = matmul(yn, w["l1_w_out"], name="l1_out")
    h3 = rmsnorm_fwd(o1, a["l1_mix_post_norm"], res=h2, out_dtype=F32, name="l1_mix_post")
    h4, ffn1 = _ffn_fwd(h3, a, w, "l1_")

    loss, dh4 = loss_fwd_bwd(h4, a["loss_target"][0], name="loss")

    dh3, _ = _ffn_bwd(dh4, ffn1, a, w, "l1_", g)
    do1, g["l1_mix_post_norm"] = rmsnorm_bwd(o1, a["l1_mix_post_norm"], dh3, out_dtype=MXU,
                                             name="l1_mix_post_bwd")
    dyn = matmul(do1, w["l1_w_out"], trans_b=True, name="l1_out_dx")
    g["l1_w_out"] = matmul(yn_t, do1, name="l1_out_dw")
    (dxbc, dz, draw, g["l1_gate_norm"], g["l1_d_skip"], g["l1_dt_bias"], g["l1_a_log"], *got) = ssd_bwd(
        xbc, proj1, st, dyn, *ssd, z_off=0, dt_off=3 * D_SSM, name="l1_ssd_bwd",
        carry=chip_carry(_rs_pair_stage(RS_GROUPS[0], g)))
    landed.update(zip(RS_GROUPS[0], got))
    (dxin,), g["l1_ssm_conv_w"], g["l1_ssm_conv_b"], _ = dwconv_bwd(
        proj1, a["l1_ssm_conv_w"], a["l1_ssm_conv_b"], dxbc, mode="silu", x_off=D_SSM,
        c_out=2 * D_SSM, cblk=512, name="l1_ssm_conv_bwd")
    g["l1_w_in"] = jnp.concatenate(
        [matmul(u2t, dz, name="l1_in_dw_z"), matmul(u2t, dxin, name="l1_in_dw_x"),
         matmul(u2t, draw, name="l1_in_dw_dt")[:, :SSD_HEADS]], axis=1)
    du2 = matmul_cat([dz, dxin, draw], w["l1_w_in"], trans_b=True, name="l1_in_dx")
    dh2, g["l1_mix_pre_norm"] = rmsnorm_bwd(h2, a["l1_mix_pre_norm"], du2, res=dh3, out_dtype=F32,
                                            name="l1_mix_pre_bwd")

    dh1, got = _ffn_bwd(dh2, ffn0, a, w, "l0_", g, carry=chip_carry(_rs_pair_stage(RS_GROUPS[1], g)))
    landed.update(zip(RS_GROUPS[1], got))
    do0, g["l0_mix_post_norm"] = rmsnorm_bwd(o0, a["l0_mix_post_norm"], dh1, out_dtype=MXU,
                                             name="l0_mix_post_bwd")
    dy = matmul(do0, w["l0_w_out"], trans_b=True, name="l0_out_dx")
    g["l0_w_out"] = jnp.concatenate([matmul(ya_t, do0, name="l0_out_dw_a"),
                                     matmul(yb.T, do0, name="l0_out_dw_b")], axis=0)
    (dgate, dxr, g["l0_lru_conv_w"], dcb, g["l0_lru_w_a"], dba, g["l0_lru_w_x"], dbx, dlam) = lru_bwd(
        proj0, hl, dy, *lru, gate_off=0, xr_off=1024, dy_off=0, name="l0_lru_bwd")
    g["l0_lru_conv_b"], g["l0_lru_b_a"], g["l0_lru_b_x"], g["l0_lru_lambda"] = dcb[0], dba[0], dbx[0], dlam[0]
    dq, dk, dv, g["l0_attn_sinks"], *got = attn_bwd(
        proj0, a["l0_attn_sinks"], dy, q_off=2048, k_off=3072, v_off=3200, dy_off=1024, name="l0_attn_bwd",
        carry=chip_carry(_rs_pair_stage(RS_GROUPS[2], g)))
    landed.update(zip(RS_GROUPS[2], got))
    dproj0 = [dgate, dxr, dq, dk, dv]
    g["l0_w_in"] = jnp.concatenate(
        [matmul(u0t, d, name="l0_in_dw%d" % i) for i, d in enumerate(dproj0)], axis=1)
    du0 = matmul_cat(dproj0, w["l0_w_in"], trans_b=True, name="l0_in_dx")
    dh0, g["l0_mix_pre_norm"] = rmsnorm_bwd(h0, a["l0_mix_pre_norm"], du0, res=dh1, out_dtype=F32,
                                            name="l0_mix_pre_bwd")
    g["meta_tokens"] = dh0[PAD:BLK]
    for n in REPLICATED:
        g[n] = g[n].reshape(a[n].shape)
    return loss[0, 0], dh0[BLK:].reshape(1, seq, D_MODEL), g, landed


def kernel(*args):
    a = dict(zip(INPUTS, args))
    first = list(SMALL_SHARDED) + ["l0_w_in"]
    got = all_gather([a[n].astype(MXU) if n in MATS else a[n] for n in first], name="gather_first")
    full = {n: _from_shards(n, blocks) for n, blocks in zip(first, got)}
    shards = {n: a[n].astype(MXU) for n in GATHER_EARLY + GATHER_LATE}
    loss_part, grad_x, g, landed = _local_step(
        {**a, **{n: full[n] for n in SMALL_SHARDED}}, {"l0_w_in": full["l0_w_in"]}, shards)
    loss = lax.psum(loss_part, ("x", "y", "c"))

    last = [n for n in SHARDED if n not in landed]
    landed.update(zip(last, chip_exchange(_rs_pair_stage(last, g), name="rs_chip")))
    sh_out = {n: adamw(landed[n], a[n], a["m_" + n], a["v_" + n], name="adamw_" + n) for n in SHARDED}

    rp_shapes = [a[n].shape for n in REPLICATED]
    rrows = _round_up(sum(_rows_of(math.prod(s), SMALL_W) for s in rp_shapes), 128)
    gathered = all_gather([_pack([g[n] for n in REPLICATED], SMALL_W, rrows)], name="gather_small_grads")[0]
    rp_out = adamw(gathered, *[_pack([a[p + n] for n in REPLICATED], SMALL_W, rrows) for p in ("", "m_", "v_")],
                   name="adamw_replicated")
    rp_out = [dict(zip(REPLICATED, _unpack(s, rp_shapes, SMALL_W))) for s in rp_out]

    outs = [loss, grad_x]
    for k in range(4):
        outs += [sh_out[n][k] if n in SHARDED else rp_out[k][n] for n in WEIGHTS]
    return tuple(outs)
```

```python
import functools
import math

import jax
import jax.numpy as jnp
import numpy as np
from jax import lax
from jax.experimental import pallas as pl
from jax.experimental.pallas import tpu as pltpu

F32 = jnp.float32
BF16 = jnp.bfloat16
MXU = jnp.bfloat16

D_MODEL = 1024
N_META = 16
BLK = 128
PAD = BLK - N_META
D_RNN = 1024
LRU_C = 8.0
N_Q_HEADS = 16
HEAD_DIM = 64
D_SSM = 2048
SSD_HEADS = 32
SSD_GROUPS = 8
D_FF = 2816
EPS = 1e-6
NEG = -1e30
N_DEV = 8

ADAM_LR = 0.001
ADAM_B1 = 0.9
ADAM_B2 = 0.999
ADAM_EPS = 1e-08
ADAM_WD = 0.01
ADAM_STEP = 10

VMEM_LIMIT = 56 * 1024 * 1024
MESH = pl.DeviceIdType.MESH
ANY = pl.BlockSpec(memory_space=pl.ANY)


def _cp(*sem):
    return pltpu.CompilerParams(dimension_semantics=sem, vmem_limit_bytes=VMEM_LIMIT)


def _pick(n, cands):
    for c in cands:
        if n % c == 0:
            return c
    return n


def _dot(a, b):
    return jnp.dot(a.astype(MXU), b.astype(MXU), preferred_element_type=F32)


def _dot_nt(a, b):
    return lax.dot_general(a.astype(MXU), b.astype(MXU), (((1,), (1,)), ((), ())),
                           preferred_element_type=F32)


def _dot_tn(a, b):
    return jnp.dot(a.T.astype(MXU), b.astype(MXU), preferred_element_type=F32)


def _dot_split(v, e):
    hi = v.astype(BF16)
    lo = (v - hi.astype(F32)).astype(BF16)
    return (jnp.dot(hi, e, preferred_element_type=F32)
            + jnp.dot(lo, e, preferred_element_type=F32))


def _sigmoid(x):
    return 1.0 / (1.0 + jnp.exp(-x))


def _log1p(x):
    u = 1.0 + x
    return jnp.where(u == 1.0, x, jnp.log(u) * (x / jnp.where(u == 1.0, 1.0, u - 1.0)))


def _expm1(x):
    u = jnp.exp(x)
    um1 = u - 1.0
    lg = jnp.log(jnp.where(u > 0.0, u, 1.0))
    safe = (um1 != 0.0) & (um1 != -1.0)
    return jnp.where(um1 == 0.0, x, jnp.where(um1 == -1.0, -1.0,
                                               um1 * (x / jnp.where(safe, lg, 1.0))))


def _softplus(x):
    return jnp.maximum(x, 0.0) + _log1p(jnp.exp(-jnp.abs(x)))


_GC = math.sqrt(2.0 / math.pi)


def _gelu(x):
    t = jnp.tanh(_GC * (x + 0.044715 * x * x * x))
    return 0.5 * x * (1.0 + t)


def _gelu_grad(x):
    t = jnp.tanh(_GC * (x + 0.044715 * x * x * x))
    return 0.5 * (1.0 + t) + 0.5 * x * (1.0 - t * t) * (_GC * (1.0 + 3.0 * 0.044715 * x * x))


def _silu(x):
    return x * _sigmoid(x)


def _silu_grad(x):
    s = _sigmoid(x)
    return s * (1.0 + x * (1.0 - s))


def _rows(shape):
    return lax.broadcasted_iota(jnp.int32, shape, 0)


def _lanes(shape):
    return lax.broadcasted_iota(jnp.int32, shape, 1)


def _shift_down(x, tail, d):
    if d == 0:
        return x
    n = x.shape[0]
    xr = pltpu.roll(x, d, 0)
    tr = pltpu.roll(tail, d, 0)
    first = jnp.where(_rows(tr.shape) < d, tr, xr[0:8])
    return jnp.concatenate([first, xr[8:n]], axis=0)


def _shift_up(x, head, d):
    if d == 0:
        return x
    n = x.shape[0]
    xr = pltpu.roll(x, n - d, 0)
    hr = pltpu.roll(head, 8 - d, 0)
    last = jnp.where(_rows(hr.shape) >= 8 - d, hr, xr[n - 8:n])
    return jnp.concatenate([xr[0:n - 8], last], axis=0)


def _row_at(x, i):
    return jnp.sum(jnp.where(_rows(x.shape) == i, x, 0.0), axis=0, keepdims=True)


def _scan_fwd(a, u):
    n = a.shape[0]
    ri = _rows(a.shape)
    d = 1
    while d < n:
        m = ri >= d
        us = jnp.where(m, pltpu.roll(u, d, 0), 0.0)
        as_ = jnp.where(m, pltpu.roll(a, d, 0), 1.0)
        u = u + a * us
        a = a * as_
        d *= 2
    return a, u


def _scan_rev(c, u):
    n = c.shape[0]
    ri = _rows(c.shape)
    d = 1
    while d < n:
        m = ri < n - d
        us = jnp.where(m, pltpu.roll(u, n - d, 0), 0.0)
        cs = jnp.where(m, pltpu.roll(c, n - d, 0), 1.0)
        u = u + c * us
        c = c * cs
        d *= 2
    return c, u


def _cumsum_fwd(x):
    n = x.shape[0]
    ri = _rows(x.shape)
    d = 1
    while d < n:
        x = x + jnp.where(ri >= d, pltpu.roll(x, d, 0), 0.0)
        d *= 2
    return x


def _cumsum_rev(x):
    n = x.shape[0]
    ri = _rows(x.shape)
    d = 1
    while d < n:
        x = x + jnp.where(ri < n - d, pltpu.roll(x, n - d, 0), 0.0)
        d *= 2
    return x


MATMUL_VMEM = 40 * 1024 * 1024


def _matmul_tiles(m, n, k, tk, out_bytes):
    best = None
    for tm in (1664, 1408, 1040, 1024, 832, 640, 512, 384, 256, 128):
        if m % tm:
            continue
        for tn in (2048, 1664, 1408, 1024, 896, 640, 512, 384, 256, 128):
            if n % tn:
                continue
            vmem = 2 * (tm * tk * 2 + tk * tn * 2 + tm * tn * out_bytes) + (tm * tn * 4 if k > tk else 0)
            if vmem > MATMUL_VMEM:
                continue
            traffic = (n // tn) * m * k * 2 + (m // tm) * k * n * 2
            if best is None or traffic < best[0]:
                best = (traffic, tm, tn)
    return (best[1], best[2]) if best else (m, n)


def matmul(a, b, *, trans_b=False, out_dtype=F32, name):
    m, k = a.shape
    n = b.shape[0] if trans_b else b.shape[1]
    tk = k if k <= 2048 else _pick(k, (1664, 1408, 1024, 896, 512, 256, 128))
    nk = k // tk
    tm, tn = _matmul_tiles(m, n, k, tk, jnp.dtype(out_dtype).itemsize)

    def product(a_ref, b_ref):
        return _dot_nt(a_ref[...], b_ref[...]) if trans_b else _dot(a_ref[...], b_ref[...])

    def kern_once(a_ref, b_ref, o_ref):
        o_ref[...] = product(a_ref, b_ref).astype(o_ref.dtype)

    def kern_acc(a_ref, b_ref, o_ref, acc_ref):
        kk = pl.program_id(2)

        @pl.when(kk == 0)
        def _():
            acc_ref[...] = product(a_ref, b_ref)

        @pl.when(kk > 0)
        def _():
            acc_ref[...] += product(a_ref, b_ref)

        @pl.when(kk == nk - 1)
        def _():
            o_ref[...] = acc_ref[...].astype(o_ref.dtype)

    b_spec = (pl.BlockSpec((tn, tk), lambda i, j, kk: (j, kk)) if trans_b
              else pl.BlockSpec((tk, tn), lambda i, j, kk: (kk, j)))
    return pl.pallas_call(
        kern_once if nk == 1 else kern_acc, name=name,
        grid=(m // tm, n // tn, nk),
        in_specs=[pl.BlockSpec((tm, tk), lambda i, j, kk: (i, kk)), b_spec],
        out_specs=pl.BlockSpec((tm, tn), lambda i, j, kk: (i, j)),
        out_shape=jax.ShapeDtypeStruct((m, n), out_dtype),
        scratch_shapes=[] if nk == 1 else [pltpu.VMEM((tm, tn), F32)],
        compiler_params=_cp("parallel", "parallel", "arbitrary"),
    )(a, b)


def matmul_cat(a_list, b, *, trans_b=False, out_dtype=F32, name, carry=None):
    m = a_list[0].shape[0]
    ks = [x.shape[1] for x in a_list]
    ktot = sum(ks)
    n = b.shape[0] if trans_b else b.shape[1]
    tn = _pick(n, (512, 256, 128))
    tm = next((c for c in (1664, 1040, 832, 640, 512, 384, 256, 128)
               if m % c == 0 and c * ktot * 2 <= 8 * 1024 * 1024), m)
    na = len(a_list)

    def kern(*refs):
        b_ref, o_ref = refs[na], refs[na + 1]
        acc, off = None, 0
        for a_ref, k in zip(refs[:na], ks):
            if trans_b:
                part = _dot_nt(a_ref[...], b_ref[:, off:off + k])
            else:
                part = _dot(a_ref[...], b_ref[off:off + k, :])
            acc = part if acc is None else acc + part
            off += k
        o_ref[...] = acc.astype(o_ref.dtype)

    b_spec = (pl.BlockSpec((tn, ktot), lambda i, j: (j, 0)) if trans_b
              else pl.BlockSpec((ktot, tn), lambda i, j: (0, j)))
    res = _call(
        kern, (*a_list, b), name=name, grid=(m // tm, n // tn),
        in_specs=[pl.BlockSpec((tm, k), lambda i, j: (i, 0)) for k in ks] + [b_spec],
        out_specs=[pl.BlockSpec((tm, tn), lambda i, j: (i, j))],
        out_shape=[jax.ShapeDtypeStruct((m, n), out_dtype)],
        sem=("parallel", "parallel"), carry=carry)
    return res[0] if carry is None else (res[0], res[1:])


def _row_tile(t):
    return _pick(t, (832, 640, 512, 384, 256, 128))


def rmsnorm_fwd(x, w, res=None, *, out_dtype, name, with_t=False):
    t, d = x.shape
    tr = _conv_tile(t) if with_t else _row_tile(t)

    def kern(*refs):
        x_ref, w_ref = refs[0], refs[1]
        o_ref = refs[-2] if with_t else refs[-1]
        xv = x_ref[...]
        r = lax.rsqrt(jnp.mean(xv * xv, axis=-1, keepdims=True) + EPS)
        y = xv * r * w_ref[...]
        if res is not None:
            y = refs[2][...] + y
        o_ref[...] = y.astype(o_ref.dtype)
        if with_t:
            refs[-1][...] = y.T.astype(o_ref.dtype)

    row = pl.BlockSpec((tr, d), lambda i: (i, 0))
    vec = pl.BlockSpec((1, d), lambda i: (0, 0))
    ins = [x, w.reshape(1, d)] + ([] if res is None else [res])
    specs = [row, vec] + ([] if res is None else [row])
    out_specs, out_shape = row, jax.ShapeDtypeStruct((t, d), out_dtype)
    if with_t:
        out_specs = [row, pl.BlockSpec((d, tr), lambda i: (0, i))]
        out_shape = [out_shape, jax.ShapeDtypeStruct((d, t), out_dtype)]
    return pl.pallas_call(
        kern, name=name, grid=(t // tr,), in_specs=specs, out_specs=out_specs, out_shape=out_shape,
        compiler_params=_cp("parallel"),
    )(*ins)


def rmsnorm_bwd(x, w, dy, res=None, *, out_dtype, name):
    t, d = x.shape
    tr = _row_tile(t)

    def kern(*refs):
        if res is None:
            x_ref, w_ref, dy_ref, dx_ref, dw_ref = refs
        else:
            x_ref, w_ref, dy_ref, r_ref, dx_ref, dw_ref = refs
        i = pl.program_id(0)
        xv = x_ref[...]
        dyv = dy_ref[...].astype(F32)
        r = lax.rsqrt(jnp.mean(xv * xv, axis=-1, keepdims=True) + EPS)
        xh = xv * r
        g = dyv * w_ref[...]
        dx = r * (g - xh * jnp.mean(g * xh, axis=-1, keepdims=True))
        if res is not None:
            dx = r_ref[...] + dx
        dx_ref[...] = dx.astype(dx_ref.dtype)
        part = jnp.sum(dyv * xh, axis=0, keepdims=True)

        @pl.when(i == 0)
        def _():
            dw_ref[...] = part

        @pl.when(i > 0)
        def _():
            dw_ref[...] += part

    row = pl.BlockSpec((tr, d), lambda i: (i, 0))
    vec = pl.BlockSpec((1, d), lambda i: (0, 0))
    ins = [x, w.reshape(1, d), dy] + ([] if res is None else [res])
    specs = [row, vec, row] + ([] if res is None else [row])
    return pl.pallas_call(
        kern, name=name, grid=(t // tr,), in_specs=specs, out_specs=[row, vec],
        out_shape=[jax.ShapeDtypeStruct((t, d), out_dtype), jax.ShapeDtypeStruct((1, d), F32)],
        compiler_params=_cp("arbitrary"),
    )(*ins)


def _conv_tile(t):
    return _pick(t, (640, 384, 256, 128))


def _conv_apply(x, tail, cw, cb, ksz):
    y = cb
    for k in range(ksz):
        y = y + cw[k:k + 1, :] * _shift_down(x, tail, ksz - 1 - k)
    return y


def dwconv_fwd(x, cw, cb, *, mode, x_off, c_out, cblk, out_dtype, name, with_t=False):
    t = x.shape[0]
    ksz = cw.shape[0]
    tb = _conv_tile(t)
    nb, ncb, t8 = t // tb, c_out // cblk, tb // 8
    xo = x_off // cblk
    nin = 2 if mode == "geglu" else 1

    def kern(*refs):
        o_ref = refs[-2] if with_t else refs[-1]
        n = pl.program_id(1)
        for c in range(cblk // BLK):
            ls = slice(c * BLK, (c + 1) * BLK)
            for s in range(tb // BLK):
                rs = slice(s * BLK, (s + 1) * BLK)
                valid = (n * tb + s * BLK + _rows((BLK, BLK))) >= PAD
                hs = []
                for q in range(nin):
                    x_ref, t_ref, w_ref, b_ref = refs[4 * q:4 * q + 4]
                    tail = (jnp.where(n > 0, t_ref[:, ls], 0.0) if s == 0
                            else x_ref[s * BLK - 8:s * BLK, ls])
                    hs.append(_conv_apply(x_ref[rs, ls], tail, w_ref[:, ls], b_ref[:, ls], ksz))
                y = _gelu(hs[0]) * hs[1] if mode == "geglu" else _silu(hs[0])
                y = jnp.where(valid, y, 0.0)
                o_ref[rs, ls] = y.astype(o_ref.dtype)
                if with_t:
                    refs[-1][ls, rs] = y.T.astype(o_ref.dtype)

    ins, specs = [], []
    for q in range(nin):
        co = xo + q * ncb
        wo = q * ncb
        ins += [x, x, cw, cb.reshape(1, -1)]
        specs += [
            pl.BlockSpec((tb, cblk), lambda j, n, co=co: (n, co + j)),
            pl.BlockSpec((8, cblk), lambda j, n, co=co: (jnp.maximum(n * t8 - 1, 0), co + j)),
            pl.BlockSpec((ksz, cblk), lambda j, n, wo=wo: (0, wo + j)),
            pl.BlockSpec((1, cblk), lambda j, n, wo=wo: (0, wo + j)),
        ]
    out_specs = pl.BlockSpec((tb, cblk), lambda j, n: (n, j))
    out_shape = jax.ShapeDtypeStruct((t, c_out), out_dtype)
    if with_t:
        out_specs = [out_specs, pl.BlockSpec((cblk, tb), lambda j, n: (j, n))]
        out_shape = [out_shape, jax.ShapeDtypeStruct((c_out, t), out_dtype)]
    return pl.pallas_call(
        kern, name=name, grid=(ncb, nb), in_specs=specs, out_specs=out_specs, out_shape=out_shape,
        compiler_params=_cp("parallel", "parallel"),
    )(*ins)


def dwconv_bwd(x, cw, cb, dy, *, mode, x_off, c_out, cblk, name, carry=None):
    t = x.shape[0]
    ksz = cw.shape[0]
    tb = _conv_tile(t)
    nb, ncb, t8 = t // tb, c_out // cblk, tb // 8
    xo = x_off // cblk
    nin = 2 if mode == "geglu" else 1
    ctot = nin * c_out

    def kern(*refs):
        dy_ref = refs[4 * nin]
        outs = refs[4 * nin + 1:4 * nin + 1 + 3 * nin]
        heads = refs[4 * nin + 1 + 3 * nin:]
        n = pl.program_id(1)
        blk = nb - 1 - n

        @pl.when(n == 0)
        def _():
            for q in range(nin):
                heads[q][...] = jnp.zeros_like(heads[q])
                outs[3 * q + 1][...] = jnp.zeros_like(outs[3 * q + 1])
                outs[3 * q + 2][...] = jnp.zeros_like(outs[3 * q + 2])

        for c in range(cblk // BLK):
            ls = slice(c * BLK, (c + 1) * BLK)
            head = [heads[q][:, ls] for q in range(nin)]
            dwa = [[None] * ksz for _ in range(nin)]
            dba = [None] * nin
            for s in reversed(range(tb // BLK)):
                rs = slice(s * BLK, (s + 1) * BLK)
                valid = (blk * tb + s * BLK + _rows((BLK, BLK))) >= PAD
                xs, tails, hs = [], [], []
                for q in range(nin):
                    x_ref, t_ref, w_ref, b_ref = refs[4 * q:4 * q + 4]
                    tail = (jnp.where(blk > 0, t_ref[:, ls], 0.0) if s == 0
                            else x_ref[s * BLK - 8:s * BLK, ls])
                    xs.append(x_ref[rs, ls])
                    tails.append(tail)
                    hs.append(_conv_apply(xs[q], tail, w_ref[:, ls], b_ref[:, ls], ksz))
                dyv = dy_ref[rs, ls].astype(F32)
                if mode == "geglu":
                    dhs = [dyv * hs[1] * _gelu_grad(hs[0]), dyv * _gelu(hs[0])]
                else:
                    dhs = [dyv * _silu_grad(hs[0])]
                for q in range(nin):
                    w_ref = refs[4 * q + 2]
                    dh = jnp.where(valid, dhs[q], 0.0)
                    dx = jnp.zeros_like(dh)
                    for k in range(ksz):
                        sh = ksz - 1 - k
                        dx = dx + w_ref[k:k + 1, ls] * _shift_up(dh, head[q], sh)
                        part = jnp.sum(dh * _shift_down(xs[q], tails[q], sh), axis=0, keepdims=True)
                        dwa[q][k] = part if dwa[q][k] is None else dwa[q][k] + part
                    outs[3 * q][rs, ls] = jnp.where(valid, dx, 0.0).astype(outs[3 * q].dtype)
                    part = jnp.sum(dh, axis=0, keepdims=True)
                    dba[q] = part if dba[q] is None else dba[q] + part
                    head[q] = dh[0:8]
            for q in range(nin):
                outs[3 * q + 1][:, ls] += jnp.concatenate(dwa[q], axis=0)
                outs[3 * q + 2][:, ls] += dba[q]
                heads[q][:, ls] = head[q]

    ins, specs, out_specs, out_shape, scratch = [], [], [], [], []
    for q in range(nin):
        co = xo + q * ncb
        wo = q * ncb
        ins += [x, x, cw, cb.reshape(1, -1)]
        specs += [
            pl.BlockSpec((tb, cblk), lambda j, n, co=co: (nb - 1 - n, co + j)),
            pl.BlockSpec((8, cblk), lambda j, n, co=co: (jnp.maximum((nb - 1 - n) * t8 - 1, 0), co + j)),
            pl.BlockSpec((ksz, cblk), lambda j, n, wo=wo: (0, wo + j)),
            pl.BlockSpec((1, cblk), lambda j, n, wo=wo: (0, wo + j)),
        ]
        out_specs += [
            pl.BlockSpec((tb, cblk), lambda j, n: (nb - 1 - n, j)),
            pl.BlockSpec((ksz, cblk), lambda j, n: (0, j)),
            pl.BlockSpec((1, cblk), lambda j, n: (0, j)),
        ]
        out_shape += [jax.ShapeDtypeStruct((t, c_out), MXU),
                      jax.ShapeDtypeStruct((ksz, c_out), F32),
                      jax.ShapeDtypeStruct((1, c_out), F32)]
        scratch.append(pltpu.VMEM((8, cblk), F32))
    ins.append(dy)
    specs.append(pl.BlockSpec((tb, cblk), lambda j, n: (nb - 1 - n, j)))
    res = _call(kern, ins, name=name, grid=(ncb, nb), in_specs=specs, out_specs=out_specs,
                out_shape=out_shape, scratch_shapes=scratch, sem=("parallel", "arbitrary"), carry=carry)
    dxs = [res[3 * q] for q in range(nin)]
    dcw = jnp.concatenate([res[3 * q + 1] for q in range(nin)], axis=1)
    dcb = jnp.concatenate([res[3 * q + 2] for q in range(nin)], axis=1)
    return dxs, dcw, dcb.reshape(ctot), res[3 * nin:]


def _lru_tile(t):
    return _pick(t, (640, 384, 256, 128))


def _lru_gates(xc, wa, ba, wx, bx, sp):
    r = _sigmoid(_dot(xc, wa) + ba)
    i = _sigmoid(_dot(xc, wx) + bx)
    log_a = -LRU_C * r * sp
    a = jnp.exp(log_a)
    mult = jnp.sqrt(-_expm1(2.0 * log_a))
    return r, i, a, mult


def lru_fwd(proj, cw, cb, wa, ba, wx, bx, lam, *, gate_off, xr_off, name, carry=None):
    t = proj.shape[0]
    tb = _lru_tile(t)
    nb, ns, t8 = t // tb, tb // BLK, tb // 8
    go, xo = gate_off // BLK, xr_off // BLK

    def kern(g_ref, x_ref, xt_ref, cw_ref, cb_ref, wa_ref, ba_ref, wx_ref, bx_ref, lam_ref,
             y_ref, yt_ref, h_ref, hc_ref):
        n = pl.program_id(1)

        @pl.when(n == 0)
        def _():
            hc_ref[...] = jnp.zeros_like(hc_ref)

        sp = _softplus(-lam_ref[...])
        hprev = hc_ref[0:1, :]
        for s in range(ns):
            sl = slice(s * BLK, (s + 1) * BLK)
            xv = x_ref[sl, :]
            tail = jnp.where(n > 0, xt_ref[...], 0.0) if s == 0 else x_ref[s * BLK - 8:s * BLK, :]
            valid = (n * tb + s * BLK + _rows((BLK, BLK))) >= PAD
            xc = jnp.where(valid, _conv_apply(xv, tail, cw_ref[...], cb_ref[...], 4), 0.0)
            _, i, a, mult = _lru_gates(xc, wa_ref[0], ba_ref[...], wx_ref[0], bx_ref[...], sp)
            u = mult * (i * xc)
            ca, cu = _scan_fwd(a, u)
            h = cu + ca * hprev
            hprev = _row_at(h, BLK - 1)
            h_ref[sl, :] = h
            y = _gelu(g_ref[sl, :]) * h
            y_ref[sl, :] = y.astype(y_ref.dtype)
            yt_ref[:, sl] = y.T.astype(yt_ref.dtype)
        hc_ref[...] = jnp.broadcast_to(hprev, hc_ref.shape)

    vec = pl.BlockSpec((1, BLK), lambda j, n: (0, j))
    mat = pl.BlockSpec((1, BLK, BLK), lambda j, n: (j, 0, 0))
    return _call(
        kern, (proj, proj, proj, cw, cb.reshape(1, -1), wa, ba.reshape(1, -1), wx, bx.reshape(1, -1),
               lam.reshape(1, -1)),
        name=name, grid=(D_RNN // BLK, nb),
        in_specs=[
            pl.BlockSpec((tb, BLK), lambda j, n: (n, go + j)),
            pl.BlockSpec((tb, BLK), lambda j, n: (n, xo + j)),
            pl.BlockSpec((8, BLK), lambda j, n: (jnp.maximum(n * t8 - 1, 0), xo + j)),
            pl.BlockSpec((4, BLK), lambda j, n: (0, j)), vec, mat, vec, mat, vec, vec,
        ],
        out_specs=[pl.BlockSpec((tb, BLK), lambda j, n: (n, j)),
                   pl.BlockSpec((BLK, tb), lambda j, n: (j, n)),
                   pl.BlockSpec((tb, BLK), lambda j, n: (n, j))],
        out_shape=[jax.ShapeDtypeStruct((t, D_RNN), MXU), jax.ShapeDtypeStruct((D_RNN, t), MXU),
                   jax.ShapeDtypeStruct((t, D_RNN), F32)],
        scratch_shapes=[pltpu.VMEM((8, BLK), F32)],
        sem=("parallel", "arbitrary"), carry=carry)


def lru_bwd(proj, h, dy, cw, cb, wa, ba, wx, bx, lam, *, gate_off, xr_off, dy_off, name):
    t = proj.shape[0]
    tb = _lru_tile(t)
    nb, ns, t8 = t // tb, tb // BLK, tb // 8
    go, xo, do = gate_off // BLK, xr_off // BLK, dy_off // BLK

    def kern(g_ref, x_ref, xt_ref, h_ref, ht_ref, dy_ref, cw_ref, cb_ref, wa_ref, ba_ref,
             wx_ref, bx_ref, lam_ref,
             dg_ref, dx_ref, dcw_ref, dcb_ref, dwa_ref, dba_ref, dwx_ref, dbx_ref, dlam_ref,
             gin_ref, head_ref):
        n = pl.program_id(1)
        blk = nb - 1 - n

        @pl.when(n == 0)
        def _():
            gin_ref[...] = jnp.zeros_like(gin_ref)
            head_ref[...] = jnp.zeros_like(head_ref)
            for r_ in (dcw_ref, dcb_ref, dwa_ref, dba_ref, dwx_ref, dbx_ref, dlam_ref):
                r_[...] = jnp.zeros_like(r_)

        lamv = lam_ref[...]
        sp = _softplus(-lamv)
        dsp_dlam = -_sigmoid(-lamv)
        g_in = gin_ref[0:1, :]
        head = head_ref[...]
        ones8 = jnp.ones((8, BLK), F32)
        for s in reversed(range(ns)):
            sl = slice(s * BLK, (s + 1) * BLK)
            xv = x_ref[sl, :]
            if s == 0:
                tail = jnp.where(blk > 0, xt_ref[...], 0.0)
                htail = jnp.where(blk > 0, ht_ref[...], 0.0)
            else:
                tail = x_ref[s * BLK - 8:s * BLK, :]
                htail = h_ref[s * BLK - 8:s * BLK, :]
            valid = (blk * tb + s * BLK + _rows((BLK, BLK))) >= PAD
            xc = jnp.where(valid, _conv_apply(xv, tail, cw_ref[...], cb_ref[...], 4), 0.0)
            wav, wxv = wa_ref[0], wx_ref[0]
            r, i, a, mult = _lru_gates(xc, wav, ba_ref[...], wxv, bx_ref[...], sp)
            hv = h_ref[sl, :]
            hprev = _shift_down(hv, htail, 1)
            gv = g_ref[sl, :]
            dyv = dy_ref[sl, :].astype(F32)
            dh = dyv * _gelu(gv)
            dg_ref[sl, :] = (dyv * hv * _gelu_grad(gv)).astype(dg_ref.dtype)
            c = _shift_up(a, ones8, 1)
            cc, cu = _scan_rev(c, dh)
            gg = cu + cc * g_in
            g_in = _row_at(a * gg, 0)
            da = gg * hprev
            di = gg * mult * xc
            dxc = gg * mult * i
            dmult = gg * i * xc
            dlog_a = da * a - dmult * (a * a) / mult
            dr = dlog_a * (-LRU_C * sp)
            dlam_ref[...] += jnp.sum(dlog_a * (-LRU_C) * r, axis=0, keepdims=True) * dsp_dlam
            dpr = dr * r * (1.0 - r)
            dpi = di * i * (1.0 - i)
            dxc = dxc + _dot_nt(dpr, wav) + _dot_nt(dpi, wxv)
            dxc = jnp.where(valid, dxc, 0.0)
            dpr = jnp.where(valid, dpr, 0.0)
            dpi = jnp.where(valid, dpi, 0.0)
            dwa_ref[0] += _dot_tn(xc, dpr)
            dwx_ref[0] += _dot_tn(xc, dpi)
            dba_ref[...] += jnp.sum(dpr, axis=0, keepdims=True)
            dbx_ref[...] += jnp.sum(dpi, axis=0, keepdims=True)
            dx = jnp.zeros_like(dxc)
            dws = []
            for k in range(4):
                dx = dx + cw_ref[k:k + 1, :] * _shift_up(dxc, head, 3 - k)
                dws.append(jnp.sum(dxc * _shift_down(xv, tail, 3 - k), axis=0, keepdims=True))
            dx_ref[sl, :] = jnp.where(valid, dx, 0.0).astype(dx_ref.dtype)
            dcw_ref[...] += jnp.concatenate(dws, axis=0)
            dcb_ref[...] += jnp.sum(dxc, axis=0, keepdims=True)
            head = dxc[0:8]
        gin_ref[...] = jnp.broadcast_to(g_in, gin_ref.shape)
        head_ref[...] = head

    vec = pl.BlockSpec((1, BLK), lambda j, n: (0, j))
    mat = pl.BlockSpec((1, BLK, BLK), lambda j, n: (j, 0, 0))
    cws = pl.BlockSpec((4, BLK), lambda j, n: (0, j))

    def rb(off):
        return pl.BlockSpec((tb, BLK), lambda j, n: (nb - 1 - n, off + j))

    def tl(off):
        return pl.BlockSpec((8, BLK), lambda j, n: (jnp.maximum((nb - 1 - n) * t8 - 1, 0), off + j))

    return pl.pallas_call(
        kern, name=name, grid=(D_RNN // BLK, nb),
        in_specs=[rb(go), rb(xo), tl(xo), rb(0), tl(0), rb(do), cws, vec, mat, vec, mat, vec, vec],
        out_specs=[rb(0), rb(0), cws, vec, mat, vec, mat, vec, vec],
        out_shape=[jax.ShapeDtypeStruct((t, D_RNN), MXU), jax.ShapeDtypeStruct((t, D_RNN), MXU),
                   jax.ShapeDtypeStruct((4, D_RNN), F32), jax.ShapeDtypeStruct((1, D_RNN), F32),
                   jax.ShapeDtypeStruct((8, BLK, BLK), F32), jax.ShapeDtypeStruct((1, D_RNN), F32),
                   jax.ShapeDtypeStruct((8, BLK, BLK), F32), jax.ShapeDtypeStruct((1, D_RNN), F32),
                   jax.ShapeDtypeStruct((1, D_RNN), F32)],
        scratch_shapes=[pltpu.VMEM((8, BLK), F32), pltpu.VMEM((8, BLK), F32)],
        compiler_params=_cp("parallel", "arbitrary"),
    )(proj, proj, proj, h, h, dy, cw, cb.reshape(1, -1), wa, ba.reshape(1, -1), wx,
      bx.reshape(1, -1), lam.reshape(1, -1))


_SCALE = HEAD_DIM ** -0.5


STK = 4


def _attn_masks(n):
    qi = np.arange(STK * BLK)[:, None] % BLK
    c = np.arange(3 * BLK)[None, :]
    tq = n * BLK + qi - PAD
    s_band = (n - 1) * BLK + c - PAD
    d_band = tq - s_band
    ok_band = (s_band >= N_META) & (d_band >= 0) & (d_band < BLK)
    jm = c - 2 * BLK
    d_meta = tq - (jm - PAD)
    ok_meta = (jm >= PAD) & (d_meta >= 0)
    is_band = c < 2 * BLK
    ok = np.where(is_band, ok_band, ok_meta)
    dist = np.where(is_band, d_band, np.minimum(d_meta, BLK)).astype(np.float32)
    return ok, dist


def _stack_heads(g, e):
    return [8 * g + 2 * i + e for i in range(STK)]


def _attn_bias_table():
    tabs = []
    for n in range(3):
        ok, dist = _attn_masks(n)
        per = []
        for g in range(2):
            for e in range(2):
                slope = np.repeat(np.array([2.0 ** (-8.0 * (h + 1) / N_Q_HEADS) for h in _stack_heads(g, e)],
                                           np.float32), BLK)[:, None]
                per.append(np.where(ok, -(slope * dist), np.float32(NEG)).astype(np.float32))
        tabs.append(np.stack(per))
    return jnp.asarray(np.stack(tabs))


def _stack_sinks(heads, sk):
    return jnp.concatenate(
        [jnp.broadcast_to(jnp.sum(jnp.where(_lanes(sk.shape) == h, sk, 0.0), axis=1, keepdims=True),
                          (BLK, 1)) for h in heads], axis=0)


def _stack_tiles(ref, g, sel):
    return jnp.concatenate(
        [jnp.where(sel, ref[:, (4 * g + i) * BLK:(4 * g + i + 1) * BLK].astype(F32), 0.0)
         for i in range(STK)], axis=0)


def _attn_probs(qm, kk, bias, sink):
    s = _dot_nt(qm, kk) * _SCALE + bias
    mx = jnp.maximum(jnp.max(s, axis=-1, keepdims=True), sink)
    p = jnp.exp(s - mx)
    es = jnp.exp(sink - mx)
    inv = 1.0 / (jnp.sum(p, axis=-1, keepdims=True) + es)
    return p * inv, es * inv


def _attn_specs(t, q_off, k_off, v_off, rev):
    nb = t // BLK
    qo, ko, vo = q_off // 1024, k_off // BLK, v_off // BLK

    def b(n):
        return nb - 1 - n if rev else n

    return [
        pl.BlockSpec((BLK, 1024), lambda n: (b(n), qo)),
        pl.BlockSpec((BLK, BLK), lambda n: (b(n), ko)),
        pl.BlockSpec((BLK, BLK), lambda n: (b(n), vo)),
        pl.BlockSpec((BLK, BLK), lambda n: (jnp.maximum(b(n) - 1, 0), ko)),
        pl.BlockSpec((BLK, BLK), lambda n: (jnp.maximum(b(n) - 1, 0), vo)),
        pl.BlockSpec((BLK, BLK), lambda n: (0, ko)),
        pl.BlockSpec((BLK, BLK), lambda n: (0, vo)),
        pl.BlockSpec((1, BLK), lambda n: (0, 0)),
        pl.BlockSpec((1, 4, STK * BLK, 3 * BLK), lambda n: (jnp.minimum(b(n), 2), 0, 0, 0)),
    ]


def attn_fwd(proj, sinks, *, q_off, k_off, v_off, name, carry=None):
    t = proj.shape[0]
    nb = t // BLK

    def kern(q_ref, kc_ref, vc_ref, kp_ref, vp_ref, km_ref, vm_ref, sk_ref, tab_ref, o_ref):
        k_all = jnp.concatenate([kp_ref[...], kc_ref[...], km_ref[...]], axis=0)
        v_all = jnp.concatenate([vp_ref[...], vc_ref[...], vm_ref[...]], axis=0)
        k_alt = pltpu.roll(k_all, HEAD_DIM, 1)
        v_alt = pltpu.roll(v_all, HEAD_DIM, 1)
        low = _lanes((BLK, BLK)) < HEAD_DIM
        outs = {}
        for g in range(2):
            for e in range(2):
                qm = _stack_tiles(q_ref, g, low == (e == 0))
                kk = k_all if g == e else k_alt
                vv = v_all if g == e else v_alt
                sink = _stack_sinks(_stack_heads(g, e), sk_ref[...])
                p, _ = _attn_probs(qm, kk, tab_ref[0, 2 * g + e], sink)
                outs[g, e] = _dot(p, vv)
        for hp in range(N_Q_HEADS // 2):
            g, rs = hp // STK, slice((hp % STK) * BLK, (hp % STK + 1) * BLK)
            o_ref[:, hp * BLK:(hp + 1) * BLK] = jnp.where(low, outs[g, 0][rs], outs[g, 1][rs]).astype(o_ref.dtype)

    sk = jnp.zeros((1, BLK), F32).at[0, :N_Q_HEADS].set(sinks)
    return _call(
        kern, (proj, proj, proj, proj, proj, proj, proj, sk, _attn_bias_table()), name=name, grid=(nb,),
        in_specs=_attn_specs(t, q_off, k_off, v_off, False),
        out_specs=[pl.BlockSpec((BLK, 1024), lambda n: (n, 0))],
        out_shape=[jax.ShapeDtypeStruct((t, 1024), MXU)],
        sem=("parallel",), carry=carry)


def attn_bwd(proj, sinks, dy, *, q_off, k_off, v_off, dy_off, name, carry=None):
    t = proj.shape[0]
    nb = t // BLK
    do = dy_off // 1024

    def kern(q_ref, kc_ref, vc_ref, kp_ref, vp_ref, km_ref, vm_ref, sk_ref, tab_ref, do_ref,
             dq_ref, dk_ref, dv_ref, dsk_ref, ck_ref, cv_ref, mk_ref, mv_ref):
        n = pl.program_id(0)
        blk = nb - 1 - n

        @pl.when(n == 0)
        def _():
            for r_ in (ck_ref, cv_ref, mk_ref, mv_ref, dsk_ref):
                r_[...] = jnp.zeros_like(r_)

        k_all = jnp.concatenate([kp_ref[...], kc_ref[...], km_ref[...]], axis=0)
        v_all = jnp.concatenate([vp_ref[...], vc_ref[...], vm_ref[...]], axis=0)
        k_alt = pltpu.roll(k_all, HEAD_DIM, 1)
        v_alt = pltpu.roll(v_all, HEAD_DIM, 1)
        low = _lanes((BLK, BLK)) < HEAD_DIM
        lane1 = _lanes((1, BLK))
        dk_all = jnp.zeros((3 * BLK, BLK), F32)
        dv_all = jnp.zeros((3 * BLK, BLK), F32)
        dsk = jnp.zeros((1, BLK), F32)
        dqs = {}
        for g in range(2):
            for e in range(2):
                sel = low == (e == 0)
                heads = _stack_heads(g, e)
                qm = _stack_tiles(q_ref, g, sel)
                dom = _stack_tiles(do_ref, g, sel)
                kk = k_all if g == e else k_alt
                vv = v_all if g == e else v_alt
                p, psink = _attn_probs(qm, kk, tab_ref[0, 2 * g + e], _stack_sinks(heads, sk_ref[...]))
                dp = _dot_nt(dom, vv)
                delta = jnp.sum(p * dp, axis=-1, keepdims=True)
                ds = p * (dp - delta) * _SCALE
                psd = psink * delta
                for i, h in enumerate(heads):
                    dsk = dsk + jnp.where(lane1 == h, -jnp.sum(psd[i * BLK:(i + 1) * BLK], axis=0, keepdims=True), 0.0)
                dqs[g, e] = _dot(ds, kk)
                dkh = _dot_tn(ds, qm)
                dvh = _dot_tn(p, dom)
                if g != e:
                    dkh = pltpu.roll(dkh, HEAD_DIM, 1)
                    dvh = pltpu.roll(dvh, HEAD_DIM, 1)
                dk_all = dk_all + dkh
                dv_all = dv_all + dvh
        for hp in range(N_Q_HEADS // 2):
            g, rs = hp // STK, slice((hp % STK) * BLK, (hp % STK + 1) * BLK)
            dq_ref[:, hp * BLK:(hp + 1) * BLK] = jnp.where(low, dqs[g, 0][rs], dqs[g, 1][rs]).astype(dq_ref.dtype)
        dsk_ref[...] += dsk
        mk_ref[...] += dk_all[2 * BLK:3 * BLK]
        mv_ref[...] += dv_all[2 * BLK:3 * BLK]
        is0 = blk == 0
        dk_ref[...] = (dk_all[BLK:2 * BLK] + ck_ref[...] + jnp.where(is0, mk_ref[...], 0.0)).astype(dk_ref.dtype)
        dv_ref[...] = (dv_all[BLK:2 * BLK] + cv_ref[...] + jnp.where(is0, mv_ref[...], 0.0)).astype(dv_ref.dtype)
        ck_ref[...] = dk_all[0:BLK]
        cv_ref[...] = dv_all[0:BLK]

    sk = jnp.zeros((1, BLK), F32).at[0, :N_Q_HEADS].set(sinks)
    kv = pl.BlockSpec((BLK, BLK), lambda n: (nb - 1 - n, 0))
    res = _call(
        kern, (proj, proj, proj, proj, proj, proj, proj, sk, _attn_bias_table(), dy), name=name, grid=(nb,),
        in_specs=_attn_specs(t, q_off, k_off, v_off, True)
        + [pl.BlockSpec((BLK, 1024), lambda n: (nb - 1 - n, do))],
        out_specs=[pl.BlockSpec((BLK, 1024), lambda n: (nb - 1 - n, 0)), kv, kv,
                   pl.BlockSpec((1, BLK), lambda n: (0, 0))],
        out_shape=[jax.ShapeDtypeStruct((t, 1024), MXU), jax.ShapeDtypeStruct((t, BLK), MXU),
                   jax.ShapeDtypeStruct((t, BLK), MXU), jax.ShapeDtypeStruct((1, BLK), F32)],
        scratch_shapes=[pltpu.VMEM((BLK, BLK), F32)] * 4,
        sem=("arbitrary",), carry=carry)
    return [res[0], res[1], res[2], res[3][0, :N_Q_HEADS]] + res[4:]


GW = D_SSM // SSD_GROUPS
EXP_ROWS = 3 * BLK + 8
RED_ROWS = EXP_ROWS + 8


def _head_expand():
    ch = jnp.arange(D_SSM) // HEAD_DIM
    return (jnp.arange(BLK)[:, None] == ch[None, :]).astype(BF16)


def _ssd_decay(raw, dtb, alog, rowv):
    valid = rowv & (_lanes((BLK, BLK)) < SSD_HEADS)
    pre = raw + dtb
    dtp = jnp.where(valid, _softplus(pre), 0.0)
    av = -jnp.exp(alog)
    cs = _cumsum_fwd(dtp * av)
    cs_last = _row_at(cs, BLK - 1)
    return valid, pre, dtp, av, cs, jnp.exp(cs), jnp.exp(cs_last - cs), jnp.exp(cs_last)


def _head_col(x, h):
    return jnp.sum(jnp.where(_lanes(x.shape) == h, x, 0.0), axis=1, keepdims=True)


def _ssd_group_fwd(g, xdt, cs, cst, bg, cg, tril, low):
    cb = _dot_nt(cg, bg)
    ys, lm = [], []
    for j in range(2):
        xp = xdt[:, g * GW + j * BLK:g * GW + (j + 1) * BLK]
        hv = []
        for e in range(2):
            h = 4 * g + 2 * j + e
            seg = _head_col(cs, h) - _row_at(cst, h)
            lmat = jnp.where(tril, jnp.exp(jnp.minimum(seg, 0.0)), 0.0)
            mmat = cb * lmat
            lm.append((lmat, mmat))
            hv.append(_dot(mmat, xp))
        ys.append(jnp.where(low, hv[0], hv[1]))
    return jnp.concatenate(ys, axis=1), lm


def _ssd_specs(t, z_off, dt_off, rev):
    nb = t // BLK
    zo, dto = z_off // D_SSM, dt_off // BLK

    def b(n):
        return nb - 1 - n if rev else n

    vec = lambda w: pl.BlockSpec((1, w), lambda n: (0, 0))
    return [
        pl.BlockSpec((BLK, D_SSM), lambda n: (b(n), 0)),
        pl.BlockSpec((BLK, 1024), lambda n: (b(n), 2)),
        pl.BlockSpec((BLK, 1024), lambda n: (b(n), 3)),
        pl.BlockSpec((BLK, D_SSM), lambda n: (b(n), zo)),
        pl.BlockSpec((BLK, BLK), lambda n: (b(n), dto)),
        vec(BLK), vec(BLK), vec(D_SSM), vec(D_SSM),
        pl.BlockSpec((BLK, D_SSM), lambda n: (0, 0)),
    ]


def _pad128(v):
    return jnp.zeros((1, BLK), F32).at[0, :v.shape[0]].set(v)


def ssd_fwd(xbc, proj, dt_bias, a_log, d_skip, gate_norm, *, z_off, dt_off, name):
    t = xbc.shape[0]
    nb = t // BLK

    def kern(x_ref, b_ref, c_ref, z_ref, dt_ref, dtb_ref, alog_ref, dsk_ref, gn_ref, e_ref,
             yn_ref, ynt_ref, st_ref, p_ref):
        n = pl.program_id(0)

        @pl.when(n == 0)
        def _():
            p_ref[...] = jnp.zeros_like(p_ref)

        rowv = (n * BLK + _rows((BLK, BLK))) >= PAD
        _, _, dtp, _, cs, ecs, w, dec = _ssd_decay(dt_ref[...], dtb_ref[...], alog_ref[...], rowv)
        ex = _dot_split(jnp.concatenate([dtp, ecs, w, jnp.broadcast_to(dec, (8, BLK))], axis=0),
                        e_ref[...])
        dtp_c, ecs_c, w_c = ex[0:BLK], ex[BLK:2 * BLK], ex[2 * BLK:3 * BLK]
        dec_c = jnp.max(ex[3 * BLK:EXP_ROWS], axis=0, keepdims=True)
        xv = x_ref[...]
        xdt = xv * dtp_c
        wx = w_c * xdt
        cst = cs.T
        tril = _rows((BLK, BLK)) >= _lanes((BLK, BLK))
        low = _lanes((BLK, BLK)) < HEAD_DIM
        st_ref[0] = p_ref[...]
        for g in range(SSD_GROUPS):
            gs = slice(g * GW, (g + 1) * GW)
            bg = b_ref[:, g * BLK:(g + 1) * BLK]
            cg = c_ref[:, g * BLK:(g + 1) * BLK]
            pg = p_ref[g]
            ydiag, _ = _ssd_group_fwd(g, xdt, cs, cst, bg, cg, tril, low)
            y = ydiag + _dot(cg, pg) * ecs_c[:, gs] + dsk_ref[:, gs] * xv[:, gs]
            p_ref[g] = pg * dec_c[:, gs] + _dot_tn(bg, wx[:, gs])
            yz = y * _silu(z_ref[:, gs])
            r = lax.rsqrt(jnp.mean(yz * yz, axis=-1, keepdims=True) + EPS)
            yn = yz * r * gn_ref[:, gs]
            yn_ref[:, gs] = yn.astype(yn_ref.dtype)
            ynt_ref[gs, :] = yn.T.astype(ynt_ref.dtype)

    return pl.pallas_call(
        kern, name=name, grid=(nb,),
        in_specs=_ssd_specs(t, z_off, dt_off, False),
        out_specs=[pl.BlockSpec((BLK, D_SSM), lambda n: (n, 0)),
                   pl.BlockSpec((D_SSM, BLK), lambda n: (0, n)),
                   pl.BlockSpec((1, SSD_GROUPS, BLK, GW), lambda n: (n, 0, 0, 0))],
        out_shape=[jax.ShapeDtypeStruct((t, D_SSM), MXU), jax.ShapeDtypeStruct((D_SSM, t), MXU),
                   jax.ShapeDtypeStruct((nb, SSD_GROUPS, BLK, GW), F32)],
        scratch_shapes=[pltpu.VMEM((SSD_GROUPS, BLK, GW), F32)],
        compiler_params=_cp("arbitrary"),
    )(xbc, xbc, xbc, proj, proj, _pad128(dt_bias), _pad128(a_log),
      jnp.repeat(d_skip, HEAD_DIM).reshape(1, D_SSM), gate_norm.reshape(1, D_SSM), _head_expand())


def ssd_bwd(xbc, proj, st, dyn, dt_bias, a_log, d_skip, gate_norm, *, z_off, dt_off, name, carry=None):
    t = xbc.shape[0]
    nb = t // BLK

    def kern(x_ref, b_ref, c_ref, z_ref, dt_ref, dtb_ref, alog_ref, dsk_ref, gn_ref, e_ref,
             et_ref, st_ref, dyn_ref,
             dxbc_ref, dz_ref, draw_ref, dgn_ref, ddsk_ref, ddtb_ref, dalog_ref,
             dp_ref, tr_ref):
        n = pl.program_id(0)
        blk = nb - 1 - n

        @pl.when(n == 0)
        def _():
            for r_ in (dp_ref, dgn_ref, ddsk_ref, ddtb_ref, dalog_ref):
                r_[...] = jnp.zeros_like(r_)

        rowv = (blk * BLK + _rows((BLK, BLK))) >= PAD
        valid, pre, dtp, av, cs, ecs, w, dec = _ssd_decay(dt_ref[...], dtb_ref[...], alog_ref[...], rowv)
        ex = _dot_split(jnp.concatenate([dtp, ecs, w, jnp.broadcast_to(dec, (8, BLK))], axis=0),
                        e_ref[...])
        dtp_c, ecs_c, w_c = ex[0:BLK], ex[BLK:2 * BLK], ex[2 * BLK:3 * BLK]
        dec_c = jnp.max(ex[3 * BLK:EXP_ROWS], axis=0, keepdims=True)
        xv = x_ref[...]
        xdt = xv * dtp_c
        wx = w_c * xdt
        cst = cs.T
        tril = _rows((BLK, BLK)) >= _lanes((BLK, BLK))
        lane = _lanes((BLK, BLK))
        rowi = _rows((BLK, BLK))
        low = lane < HEAD_DIM
        dcs = jnp.zeros((BLK, BLK), F32)
        dcst = jnp.zeros((BLK, BLK), F32)
        for g in range(SSD_GROUPS):
            gs = slice(g * GW, (g + 1) * GW)
            bg = b_ref[:, g * BLK:(g + 1) * BLK]
            cg = c_ref[:, g * BLK:(g + 1) * BLK]
            pg = st_ref[0, g]
            dpn = dp_ref[g]
            xg = xv[:, gs]
            ydiag, lm = _ssd_group_fwd(g, xdt, cs, cst, bg, cg, tril, low)
            yoff = _dot(cg, pg) * ecs_c[:, gs]
            y = ydiag + yoff + dsk_ref[:, gs] * xg
            zz = z_ref[:, gs]
            sz = _silu(zz)
            yz = y * sz
            r = lax.rsqrt(jnp.mean(yz * yz, axis=-1, keepdims=True) + EPS)
            yhat = yz * r
            dynv = dyn_ref[:, gs].astype(F32)
            gy = dynv * gn_ref[:, gs]
            dgn_ref[:, gs] += jnp.sum(dynv * yhat, axis=0, keepdims=True)
            dyz = r * (gy - yhat * jnp.mean(gy * yhat, axis=-1, keepdims=True))
            dy = dyz * sz
            dz_ref[:, gs] = (dyz * y * _silu_grad(zz)).astype(dz_ref.dtype)
            tr_ref[EXP_ROWS:RED_ROWS, gs] = jnp.broadcast_to(
                jnp.sum(dy * xg, axis=0, keepdims=True), (8, GW))
            dx = dsk_ref[:, gs] * dy
            dwx = _dot(bg, dpn)
            dxdt = w_c[:, gs] * dwx
            tr_ref[0:BLK, gs] = dwx * wx[:, gs]
            dbg = _dot_nt(wx[:, gs], dpn)
            dzo = ecs_c[:, gs] * dy
            tr_ref[BLK:2 * BLK, gs] = dy * yoff
            dcg = _dot_nt(dzo, pg)
            dp_ref[g] = dec_c[:, gs] * dpn + _dot_tn(cg, dzo)
            tr_ref[3 * BLK:EXP_ROWS, gs] = jnp.broadcast_to(
                jnp.sum(dpn * pg, axis=0, keepdims=True), (8, GW))
            dcb = jnp.zeros((BLK, BLK), F32)
            pairs = []
            for j in range(2):
                ps = slice(g * GW + j * BLK, g * GW + (j + 1) * BLK)
                xp = xdt[:, ps]
                dyp = dy[:, j * BLK:(j + 1) * BLK]
                acc = jnp.zeros((BLK, BLK), F32)
                for e in range(2):
                    h = 4 * g + 2 * j + e
                    lmat, mmat = lm[2 * j + e]
                    dyh = jnp.where(low == (e == 0), dyp, 0.0)
                    dm = jnp.where(tril, _dot_nt(dyh, xp), 0.0)
                    nh = dm * mmat
                    dcs = dcs + jnp.where(lane == h, jnp.sum(nh, axis=1, keepdims=True), 0.0)
                    dcst = dcst - jnp.where(rowi == h, jnp.sum(nh, axis=0, keepdims=True), 0.0)
                    dcb = dcb + dm * lmat
                    acc = acc + _dot_tn(mmat, dyh)
                pairs.append(acc)
            dxdt = dxdt + jnp.concatenate(pairs, axis=1)
            dcg = dcg + _dot(dcb, bg)
            dbg = dbg + _dot_tn(dcb, cg)
            tr_ref[2 * BLK:3 * BLK, gs] = dxdt * xg
            dxbc_ref[:, gs] = dx + dxdt * dtp_c[:, gs]
            dxbc_ref[:, D_SSM + g * BLK:D_SSM + (g + 1) * BLK] = dbg
            dxbc_ref[:, D_SSM + 1024 + g * BLK:D_SSM + 1024 + (g + 1) * BLK] = dcg
        red = _dot(tr_ref[...], et_ref[...])
        r1, r2, r3 = red[0:BLK], red[BLK:2 * BLK], red[2 * BLK:3 * BLK]
        ddec = jnp.max(red[3 * BLK:EXP_ROWS], axis=0, keepdims=True)
        ddsk_ref[...] += jnp.max(red[EXP_ROWS:RED_ROWS], axis=0, keepdims=True)
        dcs = dcs + dcst.T - r1 + r2
        dcs_last = jnp.sum(r1, axis=0, keepdims=True) + ddec * dec
        dcs = dcs + jnp.where(rowi == BLK - 1, dcs_last, 0.0)
        dda = _cumsum_rev(dcs)
        ddtp = r3 + dda * av
        dalog_ref[...] += jnp.sum(dda * dtp, axis=0, keepdims=True) * av
        draw = jnp.where(valid, ddtp * _sigmoid(pre), 0.0)
        ddtb_ref[...] += jnp.sum(draw, axis=0, keepdims=True)
        draw_ref[...] = draw.astype(draw_ref.dtype)

    vec = lambda w_: pl.BlockSpec((1, w_), lambda n: (0, 0))
    rb = lambda w_: pl.BlockSpec((BLK, w_), lambda n: (nb - 1 - n, 0))
    e = _head_expand()
    res = _call(
        kern, (xbc, xbc, xbc, proj, proj, _pad128(dt_bias), _pad128(a_log),
               jnp.repeat(d_skip, HEAD_DIM).reshape(1, D_SSM), gate_norm.reshape(1, D_SSM), e, e.T, st, dyn),
        name=name, grid=(nb,),
        in_specs=_ssd_specs(t, z_off, dt_off, True)
        + [pl.BlockSpec((D_SSM, BLK), lambda n: (0, 0)),
           pl.BlockSpec((1, SSD_GROUPS, BLK, GW), lambda n: (nb - 1 - n, 0, 0, 0)),
           rb(D_SSM)],
        out_specs=[rb(2 * D_SSM), rb(D_SSM), rb(BLK), vec(D_SSM), vec(BLK), vec(BLK), vec(BLK)],
        out_shape=[jax.ShapeDtypeStruct((t, 2 * D_SSM), F32), jax.ShapeDtypeStruct((t, D_SSM), MXU),
                   jax.ShapeDtypeStruct((t, BLK), MXU), jax.ShapeDtypeStruct((1, D_SSM), F32),
                   jax.ShapeDtypeStruct((1, BLK), F32), jax.ShapeDtypeStruct((1, BLK), F32),
                   jax.ShapeDtypeStruct((1, BLK), F32)],
        scratch_shapes=[pltpu.VMEM((SSD_GROUPS, BLK, GW), F32), pltpu.VMEM((RED_ROWS, D_SSM), F32)],
        sem=("arbitrary",), carry=carry)
    dxbc, dz, draw, dgn, ddsk, ddtb, dalog = res[:7]
    return [dxbc, dz, draw, dgn[0], ddsk[0, :SSD_HEADS], ddtb[0, :SSD_HEADS], dalog[0, :SSD_HEADS]] + res[7:]


def loss_fwd_bwd(h, target, *, name):
    t, d = h.shape
    nb = t // BLK

    def kern(h_ref, t_ref, loss_ref, dh_ref):
        n = pl.program_id(0)
        err = jnp.where(n > 0, h_ref[...] - t_ref[...], 0.0)
        dh_ref[...] = err * (1.0 / d)
        part = (0.5 / d) * jnp.sum(jnp.sum(err * err, axis=1, keepdims=True), axis=0, keepdims=True)

        @pl.when(n == 0)
        def _():
            loss_ref[...] = part

        @pl.when(n > 0)
        def _():
            loss_ref[...] += part

    return pl.pallas_call(
        kern, name=name, grid=(nb,),
        in_specs=[pl.BlockSpec((BLK, d), lambda n: (n, 0)),
                  pl.BlockSpec((BLK, d), lambda n: (jnp.maximum(n - 1, 0), 0))],
        out_specs=[pl.BlockSpec((1, 1), lambda n: (0, 0)), pl.BlockSpec((BLK, d), lambda n: (n, 0))],
        out_shape=[jax.ShapeDtypeStruct((1, 1), F32), jax.ShapeDtypeStruct((t, d), F32)],
        compiler_params=_cp("arbitrary"),
    )(h, target)


def _ew_tile(r, c):
    cap = max(16, (256 * 1024) // c)
    best = None
    for tr in range(16, min(r, cap) + 1, 16):
        if r % tr == 0:
            best = tr
    return best if best is not None else r


def adamw(parts, w, m, v, *, name):
    npart, r, c = parts.shape
    tr = _ew_tile(r, c)

    def kern(p_ref, w_ref, m_ref, v_ref, g_ref, d_ref, m2_ref, v2_ref):
        g = p_ref[0].astype(F32)
        for k in range(1, npart):
            g = g + p_ref[k].astype(F32)
        m2 = ADAM_B1 * m_ref[...] + (1.0 - ADAM_B1) * g
        v2 = ADAM_B2 * v_ref[...] + (1.0 - ADAM_B2) * (g * g)
        m_hat = m2 / (1.0 - ADAM_B1 ** ADAM_STEP)
        v_hat = v2 / (1.0 - ADAM_B2 ** ADAM_STEP)
        g_ref[...] = g
        d_ref[...] = -ADAM_LR * (m_hat / (jnp.sqrt(v_hat) + ADAM_EPS) + ADAM_WD * w_ref[...])
        m2_ref[...] = m2
        v2_ref[...] = v2

    row = pl.BlockSpec((tr, c), lambda i: (i, 0))
    sds = jax.ShapeDtypeStruct((r, c), F32)
    return pl.pallas_call(
        kern, name=name, grid=(r // tr,),
        in_specs=[pl.BlockSpec((npart, tr, c), lambda i: (0, i, 0)), row, row, row],
        out_specs=[row, row, row, row], out_shape=[sds, sds, sds, sds],
        compiler_params=_cp("parallel"),
    )(parts, w, m, v)


def pair_add(p, land, *, name):
    _, r, c = p.shape
    tr = _ew_tile(r, c)
    core = lax.axis_index("c").astype(jnp.int32).reshape(1)

    def kern(c_ref, p_ref, l_ref, o_ref):
        o_ref[...] = (p_ref[...] + l_ref[...]).astype(o_ref.dtype)

    return pl.pallas_call(
        kern, name=name,
        grid_spec=pltpu.PrefetchScalarGridSpec(
            num_scalar_prefetch=1, grid=(4, r // tr),
            in_specs=[pl.BlockSpec((1, tr, c), lambda k, i, c_ref: (2 * k + c_ref[0], i, 0)),
                      pl.BlockSpec((1, tr, c), lambda k, i, c_ref: (k, i, 0))],
            out_specs=pl.BlockSpec((1, tr, c), lambda k, i, c_ref: (k, i, 0))),
        out_shape=jax.ShapeDtypeStruct((4, r, c), BF16),
        compiler_params=_cp("parallel", "parallel"),
    )(core, p, land)


def _me():
    return lax.axis_index("x"), lax.axis_index("y"), lax.axis_index("c")


def all_gather(xs, *, name):
    n = len(xs)

    def body(*refs):
        x_refs, out_refs = refs[:n], refs[n:2 * n]
        send_sems, recv_sems, local_sems = refs[2 * n:]
        mx, my, mc = _me()
        me, sib = (mx, my, mc), (mx, my, 1 - mc)
        chips = [(1 - mx, my), (mx, 1 - my), (1 - mx, 1 - my)]

        def rows(i, px, py, pc):
            return out_refs[i].at[4 * px + 2 * py + pc]

        def copy(i, k, block, to, src=None):
            return pltpu.make_async_remote_copy(
                src_ref=rows(i, *block) if src is None else src, dst_ref=rows(i, *block),
                send_sem=send_sems.at[7 * i + k], recv_sem=recv_sems.at[7 * i + k],
                device_id=to, device_id_type=MESH)

        mine = [pltpu.make_async_copy(x_refs[i], rows(i, *me), local_sems.at[i]) for i in range(n)]
        first = []
        for i in range(n):
            mine[i].start()
            first.append(copy(i, 0, me, sib, src=x_refs[i]))
            first += [copy(i, 1 + j, me, (*chip, mc), src=x_refs[i]) for j, chip in enumerate(chips)]
        for cp in first:
            cp.start()
        passed = []
        for i in range(n):
            for j, chip in enumerate(chips):
                copy(i, 1 + j, (*chip, mc), me).wait_recv()
                passed.append(copy(i, 4 + j, (*chip, mc), sib))
                passed[-1].start()
        for i in range(n):
            copy(i, 0, sib, me).wait_recv()
            for j, chip in enumerate(chips):
                copy(i, 4 + j, (*chip, 1 - mc), me).wait_recv()
        for cp in first + passed:
            cp.wait_send()
        for cp in mine:
            cp.wait()

    return pl.pallas_call(
        body, name=name,
        out_shape=[jax.ShapeDtypeStruct((N_DEV,) + x.shape, x.dtype) for x in xs],
        in_specs=[ANY] * n, out_specs=[ANY] * n,
        scratch_shapes=[pltpu.SemaphoreType.DMA((7 * n,)), pltpu.SemaphoreType.DMA((7 * n,)),
                        pltpu.SemaphoreType.DMA((n,))],
    )(*xs)


def pair_exchange(ps, *, name):
    n = len(ps)

    def body(*refs):
        p_refs, out_refs = refs[:n], refs[n:2 * n]
        send_sems, recv_sems = refs[2 * n:]
        mx, my, mc = _me()
        cps = [pltpu.make_async_remote_copy(
            src_ref=p_refs[i].at[2 * k + (1 - mc)], dst_ref=out_refs[i].at[k],
            send_sem=send_sems.at[4 * i + k], recv_sem=recv_sems.at[4 * i + k],
            device_id=(mx, my, 1 - mc), device_id_type=MESH) for i in range(n) for k in range(4)]
        for cp in cps:
            cp.start()
        for cp in cps:
            cp.wait_recv()
        for cp in cps:
            cp.wait_send()

    return pl.pallas_call(
        body, name=name,
        out_shape=[jax.ShapeDtypeStruct((4,) + p.shape[1:], p.dtype) for p in ps],
        in_specs=[ANY] * n, out_specs=[ANY] * n,
        scratch_shapes=[pltpu.SemaphoreType.DMA((4 * n,)), pltpu.SemaphoreType.DMA((4 * n,))],
    )(*ps)


def chip_exchange(qs, *, name):
    n = len(qs)

    def body(*refs):
        q_refs, out_refs = refs[:n], refs[n:2 * n]
        send_sems, recv_sems, local_sems = refs[2 * n:]
        mx, my, mc = _me()
        mine = 2 * mx + my
        chips = [(1 - mx, my), (mx, 1 - my), (1 - mx, 1 - my)]
        local, sends, recvs = [], [], []
        for i in range(n):
            local.append(pltpu.make_async_copy(q_refs[i].at[mine], out_refs[i].at[mine], local_sems.at[i]))
            for k, (px, py) in enumerate(chips):
                sems = dict(send_sem=send_sems.at[3 * i + k], recv_sem=recv_sems.at[3 * i + k],
                            device_id=(px, py, mc), device_id_type=MESH)
                sends.append(pltpu.make_async_remote_copy(
                    src_ref=q_refs[i].at[2 * px + py], dst_ref=out_refs[i].at[mine], **sems))
                recvs.append(pltpu.make_async_remote_copy(
                    src_ref=q_refs[i].at[mine], dst_ref=out_refs[i].at[2 * px + py], **sems))
        for cp in local + sends:
            cp.start()
        for cp in recvs:
            cp.wait_recv()
        for cp in sends:
            cp.wait_send()
        for cp in local:
            cp.wait()

    return pl.pallas_call(
        body, name=name,
        out_shape=[jax.ShapeDtypeStruct(q.shape, q.dtype) for q in qs],
        in_specs=[ANY] * n, out_specs=[ANY] * n,
        scratch_shapes=[pltpu.SemaphoreType.DMA((3 * n,)), pltpu.SemaphoreType.DMA((3 * n,)),
                        pltpu.SemaphoreType.DMA((n,))],
    )(*qs)


class _Carry:
    def __init__(self, inputs, out_shapes, sems, start, finish):
        self.inputs, self.out_shapes, self.sems = list(inputs), list(out_shapes), list(sems)
        self.start, self.finish = start, finish


def _call(kern, args, *, name, grid, in_specs, out_specs, out_shape, scratch_shapes=(), sem, carry=None):
    in_specs, out_specs, out_shape = list(in_specs), list(out_specs), list(out_shape)
    scratch_shapes = list(scratch_shapes)
    if carry is None:
        return list(pl.pallas_call(
            kern, name=name, grid=grid, in_specs=in_specs, out_specs=out_specs, out_shape=out_shape,
            scratch_shapes=scratch_shapes, compiler_params=_cp(*sem))(*args))
    ni, no, ns = len(in_specs), len(out_specs), len(scratch_shapes)
    ci, co = len(carry.inputs), len(carry.out_shapes)

    def body(*refs):
        o0 = ni + ci
        s0 = o0 + no + co
        ids = [pl.program_id(d) for d in range(len(grid))]
        first = functools.reduce(jnp.logical_and, [i == 0 for i in ids])
        last = functools.reduce(jnp.logical_and, [i == g - 1 for i, g in zip(ids, grid)])
        cin, cout, sems = refs[ni:o0], refs[o0 + no:s0], refs[s0 + ns:]

        @pl.when(first)
        def _():
            carry.start(cin, cout, sems)

        kern(*refs[:ni], *refs[o0:o0 + no], *refs[s0:s0 + ns])

        @pl.when(last)
        def _():
            carry.finish(cin, cout, sems)

    return list(pl.pallas_call(
        body, name=name, grid=grid, in_specs=in_specs + [ANY] * ci, out_specs=out_specs + [ANY] * co,
        out_shape=out_shape + carry.out_shapes, scratch_shapes=scratch_shapes + carry.sems,
        compiler_params=_cp(*(["arbitrary"] * len(grid))))(*args, *carry.inputs))


def gather_carry(xs):
    n = len(xs)

    def copies(cin, cout, sems, with_recv=True):
        mx, my, mc = _me()
        me = 4 * mx + 2 * my + mc
        peers = [(mx, my, 1 - mc), (1 - mx, my, mc), (mx, 1 - my, mc), (1 - mx, 1 - my, mc)]
        local, send, recv = [], [], []
        for i in range(n):
            local.append(pltpu.make_async_copy(cin[i], cout[i].at[me], sems[2].at[i]))
            for k, peer in enumerate(peers):
                common = dict(send_sem=sems[0].at[4 * i + k], recv_sem=sems[1].at[4 * i + k],
                              device_id=peer, device_id_type=MESH)
                send.append(pltpu.make_async_remote_copy(src_ref=cin[i], dst_ref=cout[i].at[me], **common))
                if with_recv:
                    recv.append(pltpu.make_async_remote_copy(
                        src_ref=cin[i], dst_ref=cout[i].at[4 * peer[0] + 2 * peer[1] + peer[2]], **common))
        return local, send, recv

    def start(cin, cout, sems):
        local, send, _ = copies(cin, cout, sems, with_recv=False)
        for cp in local + send:
            cp.start()

    def finish(cin, cout, sems):
        local, send, recv = copies(cin, cout, sems)
        for cp in recv:
            cp.wait_recv()
        for cp in send:
            cp.wait_send()
        for cp in local:
            cp.wait()

    return _Carry(xs, [jax.ShapeDtypeStruct((N_DEV,) + x.shape, x.dtype) for x in xs],
                  [pltpu.SemaphoreType.DMA((4 * n,)), pltpu.SemaphoreType.DMA((4 * n,)),
                   pltpu.SemaphoreType.DMA((n,))], start, finish)


def gather_relay(outs, *, name):
    n = len(outs)

    def body(*refs):
        bufs = refs[n:2 * n]
        send_sems, recv_sems = refs[2 * n:]
        mx, my, mc = _me()
        chips = [(1 - mx, my), (mx, 1 - my), (1 - mx, 1 - my)]
        send, recv = [], []
        for i in range(n):
            for j, (px, py) in enumerate(chips):
                common = dict(send_sem=send_sems.at[3 * i + j], recv_sem=recv_sems.at[3 * i + j],
                              device_id=(mx, my, 1 - mc), device_id_type=MESH)
                mine = bufs[i].at[4 * px + 2 * py + mc]
                send.append(pltpu.make_async_remote_copy(src_ref=mine, dst_ref=mine, **common))
                recv.append(pltpu.make_async_remote_copy(
                    src_ref=mine, dst_ref=bufs[i].at[4 * px + 2 * py + (1 - mc)], **common))
        for cp in send:
            cp.start()
        for cp in recv:
            cp.wait_recv()
        for cp in send:
            cp.wait_send()

    return pl.pallas_call(
        body, name=name, out_shape=[jax.ShapeDtypeStruct(o.shape, o.dtype) for o in outs],
        in_specs=[ANY] * n, out_specs=[ANY] * n, input_output_aliases={i: i for i in range(n)},
        scratch_shapes=[pltpu.SemaphoreType.DMA((3 * n,)), pltpu.SemaphoreType.DMA((3 * n,))],
    )(*outs)


def pair_carry(ps):
    n = len(ps)

    def copies(cin, cout, sems):
        mx, my, mc = _me()
        return [pltpu.make_async_remote_copy(
            src_ref=cin[i].at[2 * k + (1 - mc)], dst_ref=cout[i].at[k],
            send_sem=sems[0].at[4 * i + k], recv_sem=sems[1].at[4 * i + k],
            device_id=(mx, my, 1 - mc), device_id_type=MESH) for i in range(n) for k in range(4)]

    def start(cin, cout, sems):
        for cp in copies(cin, cout, sems):
            cp.start()

    def finish(cin, cout, sems):
        cps = copies(cin, cout, sems)
        for cp in cps:
            cp.wait_recv()
        for cp in cps:
            cp.wait_send()

    return _Carry(ps, [jax.ShapeDtypeStruct((4,) + p.shape[1:], p.dtype) for p in ps],
                  [pltpu.SemaphoreType.DMA((4 * n,)), pltpu.SemaphoreType.DMA((4 * n,))], start, finish)


def chip_carry(qs):
    n = len(qs)

    def copies(cin, cout, sems, with_recv=True):
        mx, my, mc = _me()
        mine = 2 * mx + my
        chips = [(1 - mx, my), (mx, 1 - my), (1 - mx, 1 - my)]
        local, send, recv = [], [], []
        for i in range(n):
            local.append(pltpu.make_async_copy(cin[i].at[mine], cout[i].at[mine], sems[2].at[i]))
            for k, (px, py) in enumerate(chips):
                common = dict(send_sem=sems[0].at[3 * i + k], recv_sem=sems[1].at[3 * i + k],
                              device_id=(px, py, mc), device_id_type=MESH)
                send.append(pltpu.make_async_remote_copy(
                    src_ref=cin[i].at[2 * px + py], dst_ref=cout[i].at[mine], **common))
                if with_recv:
                    recv.append(pltpu.make_async_remote_copy(
                        src_ref=cin[i].at[mine], dst_ref=cout[i].at[2 * px + py], **common))
        return local, send, recv

    def start(cin, cout, sems):
        local, send, _ = copies(cin, cout, sems, with_recv=False)
        for cp in local + send:
            cp.start()

    def finish(cin, cout, sems):
        local, send, recv = copies(cin, cout, sems)
        for cp in recv:
            cp.wait_recv()
        for cp in send:
            cp.wait_send()
        for cp in local:
            cp.wait()

    return _Carry(qs, [jax.ShapeDtypeStruct(q.shape, q.dtype) for q in qs],
                  [pltpu.SemaphoreType.DMA((3 * n,)), pltpu.SemaphoreType.DMA((3 * n,)),
                   pltpu.SemaphoreType.DMA((n,))], start, finish)


WEIGHTS = [
    "meta_tokens", "l0_mix_pre_norm", "l0_mix_post_norm", "l0_w_in", "l0_lru_conv_w", "l0_lru_conv_b",
    "l0_lru_w_a", "l0_lru_b_a", "l0_lru_w_x", "l0_lru_b_x", "l0_lru_lambda", "l0_attn_sinks", "l0_w_out",
    "l0_ffn_pre_norm", "l0_ffn_post_norm", "l0_ffn_w_up", "l0_ffn_conv_w", "l0_ffn_conv_b", "l0_ffn_w_down",
    "l1_mix_pre_norm", "l1_mix_post_norm", "l1_w_in", "l1_ssm_conv_w", "l1_ssm_conv_b", "l1_dt_bias",
    "l1_a_log", "l1_d_skip", "l1_gate_norm", "l1_w_out", "l1_ffn_pre_norm", "l1_ffn_post_norm",
    "l1_ffn_w_up", "l1_ffn_conv_w", "l1_ffn_conv_b", "l1_ffn_w_down",
]
INPUTS = (["x"] + WEIGHTS + ["loss_target"] + ["m_" + n for n in WEIGHTS] + ["v_" + n for n in WEIGHTS])

MATS = {"l0_w_in": ("col", (1024, 3328)), "l0_w_out": ("row", (2048, 1024)),
        "l0_ffn_w_up": ("col", (1024, 5632)), "l0_ffn_w_down": ("row", (2816, 1024)),
        "l1_w_in": ("col", (1024, 6176)), "l1_w_out": ("row", (2048, 1024)),
        "l1_ffn_w_up": ("col", (1024, 5632)), "l1_ffn_w_down": ("row", (2816, 1024))}
SMALL_SHARDED = {"meta_tokens": ("col", (16, 1024)), "l0_lru_conv_w": ("col", (4, 1024)),
                 "l0_ffn_conv_w": ("col", (3, 5632)), "l1_ssm_conv_w": ("col", (4, 4096)),
                 "l1_ffn_conv_w": ("col", (3, 5632))}
SHARDED = {**MATS, **SMALL_SHARDED}
REPLICATED = [n for n in WEIGHTS if n not in SHARDED]
PACK_W = 1024
SMALL_W = 128


def _shard_shape(name):
    kind, (r, c) = SHARDED[name]
    return (r, c // N_DEV) if kind == "col" else (r // N_DEV, c)


def _rows_of(numel, width):
    return -(-numel // width)


def _to_rows(a, width):
    flat = a.reshape(-1)
    rows = _rows_of(flat.shape[0], width)
    return jnp.pad(flat, (0, rows * width - flat.shape[0])).reshape(rows, width)


def _pack(arrs, width, total_rows):
    slab = jnp.concatenate([_to_rows(a, width) for a in arrs], axis=0)
    return jnp.pad(slab, ((0, total_rows - slab.shape[0]), (0, 0)))


def _unpack(slab, shapes, width):
    out, off = [], 0
    for shp in shapes:
        numel = math.prod(shp)
        rows = _rows_of(numel, width)
        out.append(slab[off:off + rows].reshape(-1)[:numel].reshape(shp))
        off += rows
    return out


def _round_up(n, m):
    return -(-n // m) * m


def _by_dest(name, g):
    kind, (r, c) = SHARDED[name]
    if kind == "col":
        return g.reshape(r, N_DEV, c // N_DEV).transpose(1, 0, 2)
    return g.reshape(N_DEV, r // N_DEV, c)


def _from_shards(name, blocks):
    kind, (r, c) = SHARDED[name]
    return blocks.transpose(1, 0, 2).reshape(r, c) if kind == "col" else blocks.reshape(r, c)


L1_IN_PAD = 6272


def _ffn_fwd(h, a, w, pfx):
    u, ut = rmsnorm_fwd(h, a[pfx + "ffn_pre_norm"], out_dtype=MXU, name=pfx + "ffn_pre", with_t=True)
    up = matmul(u, w[pfx + "ffn_w_up"], name=pfx + "ffn_up")
    act, act_t = dwconv_fwd(up, a[pfx + "ffn_conv_w"], a[pfx + "ffn_conv_b"], mode="geglu", x_off=0,
                            c_out=D_FF, cblk=256, out_dtype=MXU, name=pfx + "ffn_act", with_t=True)
    down = matmul(act, w[pfx + "ffn_w_down"], name=pfx + "ffn_down")
    out = rmsnorm_fwd(down, a[pfx + "ffn_post_norm"], res=h, out_dtype=F32, name=pfx + "ffn_post")
    return out, (h, ut, up, act_t, down)


def _dx_and_pair_stage(names, g, a_list, b, *, name):
    parts = [_by_dest(n, g[n]) for n in names]
    out, from_sibling = matmul_cat(a_list, b, trans_b=True, name=name, carry=pair_carry(parts))
    return out, [pair_add(p, l, name="rs_pair_add_" + n) for n, p, l in zip(names, parts, from_sibling)]


def _ffn_bwd(dh, saved, a, w, pfx, g, carry=None):
    h, ut, up, act_t, down = saved
    dd, g[pfx + "ffn_post_norm"] = rmsnorm_bwd(down, a[pfx + "ffn_post_norm"], dh, out_dtype=MXU,
                                               name=pfx + "ffn_post_bwd")
    dact = matmul(dd, w[pfx + "ffn_w_down"], trans_b=True, name=pfx + "ffn_down_dx")
    g[pfx + "ffn_w_down"] = matmul(act_t, dd, name=pfx + "ffn_down_dw")
    dups, g[pfx + "ffn_conv_w"], g[pfx + "ffn_conv_b"], carried = dwconv_bwd(
        up, a[pfx + "ffn_conv_w"], a[pfx + "ffn_conv_b"], dact, mode="geglu", x_off=0, c_out=D_FF,
        cblk=256, name=pfx + "ffn_act_bwd", carry=carry)
    g[pfx + "ffn_w_up"] = jnp.concatenate(
        [matmul(ut, d, name=pfx + "ffn_up_dw%d" % i) for i, d in enumerate(dups)], axis=1)
    du, q = _dx_and_pair_stage([pfx + "ffn_w_down", pfx + "ffn_w_up"], g, dups, w[pfx + "ffn_w_up"],
                               name=pfx + "ffn_up_dx")
    dh_in, g[pfx + "ffn_pre_norm"] = rmsnorm_bwd(h, a[pfx + "ffn_pre_norm"], du, res=dh, out_dtype=F32,
                                                 name=pfx + "ffn_pre_bwd")
    return dh_in, carried, q


GATHER_EARLY = ["l0_w_out", "l0_ffn_w_up", "l0_ffn_w_down"]
GATHER_LATE = ["l1_w_in", "l1_w_out", "l1_ffn_w_up", "l1_ffn_w_down"]
RS_L1_FFN = ["l1_ffn_w_down", "l1_ffn_w_up"]
RS_L1_MIX = ["l1_w_out", "l1_w_in"]
RS_L0_FFN = ["l0_ffn_w_down", "l0_ffn_w_up"]
RS_LAST = ["l0_w_out", "l0_w_in", "l0_lru_conv_w", "l0_ffn_conv_w", "l1_ssm_conv_w", "l1_ffn_conv_w"]


def _local_step(a, w, shards):
    x = a["x"][0]
    seq = x.shape[0]
    h0 = jnp.concatenate([jnp.zeros((PAD, D_MODEL), F32), a["meta_tokens"], x], axis=0)
    g, landed = {}, {}

    u0, u0t = rmsnorm_fwd(h0, a["l0_mix_pre_norm"], out_dtype=MXU, name="l0_mix_pre", with_t=True)
    proj0 = matmul(u0, w["l0_w_in"], name="l0_in")
    lru = (a["l0_lru_conv_w"], a["l0_lru_conv_b"], a["l0_lru_w_a"], a["l0_lru_b_a"], a["l0_lru_w_x"],
           a["l0_lru_b_x"], a["l0_lru_lambda"])
    ya, ya_t, hl, *early = lru_fwd(proj0, *lru, gate_off=0, xr_off=1024, name="l0_lru",
                                   carry=gather_carry([shards[n] for n in GATHER_EARLY]))
    yb, *late = attn_fwd(proj0, a["l0_attn_sinks"], q_off=2048, k_off=3072, v_off=3200, name="l0_attn",
                         carry=gather_carry([shards[n] for n in GATHER_LATE]))
    relayed = gather_relay(early + late, name="gather_relay")
    w = dict(w, **{n: _from_shards(n, blocks) for n, blocks in zip(GATHER_EARLY + GATHER_LATE, relayed)})
    w["l1_w_in"] = jnp.pad(w["l1_w_in"], ((0, 0), (0, L1_IN_PAD - w["l1_w_in"].shape[1])))
    o0 = matmul_cat([ya, yb], w["l0_w_out"], name="l0_out")
    h1 = rmsnorm_fwd(o0, a["l0_mix_post_norm"], res=h0, out_dtype=F32, name="l0_mix_post")
    h2, ffn0 = _ffn_fwd(h1, a, w, "l0_")

    u2, u2t = rmsnorm_fwd(h2, a["l1_mix_pre_norm"], out_dtype=MXU, name="l1_mix_pre", with_t=True)
    proj1 = matmul(u2, w["l1_w_in"], name="l1_in")
    xbc = dwconv_fwd(proj1, a["l1_ssm_conv_w"], a["l1_ssm_conv_b"], mode="silu", x_off=D_SSM,
                     c_out=2 * D_SSM, cblk=512, out_dtype=F32, name="l1_ssm_conv")
    ssd = (a["l1_dt_bias"], a["l1_a_log"], a["l1_d_skip"], a["l1_gate_norm"])
    yn, yn_t, st = ssd_fwd(xbc, proj1, *ssd, z_off=0, dt_off=3 * D_SSM, name="l1_ssd")
    o1 = matmul(yn, w["l1_w_out"], name="l1_out")
    h3 = rmsnorm_fwd(o1, a["l1_mix_post_norm"], res=h2, out_dtype=F32, name="l1_mix_post")
    h4, ffn1 = _ffn_fwd(h3, a, w, "l1_")

    loss, dh4 = loss_fwd_bwd(h4, a["loss_target"][0], name="loss")

    dh3, _, q_l1_ffn = _ffn_bwd(dh4, ffn1, a, w, "l1_", g)
    do1, g["l1_mix_post_norm"] = rmsnorm_bwd(o1, a["l1_mix_post_norm"], dh3, out_dtype=MXU,
                                             name="l1_mix_post_bwd")
    dyn = matmul(do1, w["l1_w_out"], trans_b=True, name="l1_out_dx")
    g["l1_w_out"] = matmul(yn_t, do1, name="l1_out_dw")
    (dxbc, dz, draw, g["l1_gate_norm"], g["l1_d_skip"], g["l1_dt_bias"], g["l1_a_log"], *got) = ssd_bwd(
        xbc, proj1, st, dyn, *ssd, z_off=0, dt_off=3 * D_SSM, name="l1_ssd_bwd",
        carry=chip_carry(q_l1_ffn))
    landed.update(zip(RS_L1_FFN, got))
    (dxin,), g["l1_ssm_conv_w"], g["l1_ssm_conv_b"], _ = dwconv_bwd(
        proj1, a["l1_ssm_conv_w"], a["l1_ssm_conv_b"], dxbc, mode="silu", x_off=D_SSM,
        c_out=2 * D_SSM, cblk=512, name="l1_ssm_conv_bwd")
    g["l1_w_in"] = jnp.concatenate(
        [matmul(u2t, dz, name="l1_in_dw_z"), matmul(u2t, dxin, name="l1_in_dw_x"),
         matmul(u2t, draw, name="l1_in_dw_dt")[:, :SSD_HEADS]], axis=1)
    du2, q_l1_mix = _dx_and_pair_stage(RS_L1_MIX, g, [dz, dxin, draw], w["l1_w_in"], name="l1_in_dx")
    dh2, g["l1_mix_pre_norm"] = rmsnorm_bwd(h2, a["l1_mix_pre_norm"], du2, res=dh3, out_dtype=F32,
                                            name="l1_mix_pre_bwd")

    dh1, got, q_l0_ffn = _ffn_bwd(dh2, ffn0, a, w, "l0_", g, carry=chip_carry(q_l1_mix))
    landed.update(zip(RS_L1_MIX, got))
    do0, g["l0_mix_post_norm"] = rmsnorm_bwd(o0, a["l0_mix_post_norm"], dh1, out_dtype=MXU,
                                             name="l0_mix_post_bwd")
    dy = matmul(do0, w["l0_w_out"], trans_b=True, name="l0_out_dx")
    g["l0_w_out"] = jnp.concatenate([matmul(ya_t, do0, name="l0_out_dw_a"),
                                     matmul(yb.T, do0, name="l0_out_dw_b")], axis=0)
    (dgate, dxr, g["l0_lru_conv_w"], dcb, g["l0_lru_w_a"], dba, g["l0_lru_w_x"], dbx, dlam) = lru_bwd(
        proj0, hl, dy, *lru, gate_off=0, xr_off=1024, dy_off=0, name="l0_lru_bwd")
    g["l0_lru_conv_b"], g["l0_lru_b_a"], g["l0_lru_b_x"], g["l0_lru_lambda"] = dcb[0], dba[0], dbx[0], dlam[0]
    dq, dk, dv, g["l0_attn_sinks"], *got = attn_bwd(
        proj0, a["l0_attn_sinks"], dy, q_off=2048, k_off=3072, v_off=3200, dy_off=1024, name="l0_attn_bwd",
        carry=chip_carry(q_l0_ffn))
    landed.update(zip(RS_L0_FFN, got))
    dproj0 = [dgate, dxr, dq, dk, dv]
    g["l0_w_in"] = jnp.concatenate(
        [matmul(u0t, d, name="l0_in_dw%d" % i) for i, d in enumerate(dproj0)], axis=1)
    du0, q_last = _dx_and_pair_stage(RS_LAST, g, dproj0, w["l0_w_in"], name="l0_in_dx")
    dh0, g["l0_mix_pre_norm"] = rmsnorm_bwd(h0, a["l0_mix_pre_norm"], du0, res=dh1, out_dtype=F32,
                                            name="l0_mix_pre_bwd")
    g["meta_tokens"] = dh0[PAD:BLK]
    meta = _by_dest("meta_tokens", g["meta_tokens"])
    q_meta = pair_add(meta, pair_exchange([meta], name="rs_pair_meta")[0], name="rs_pair_add_meta_tokens")
    landed.update(zip(RS_LAST + ["meta_tokens"], chip_exchange(q_last + [q_meta], name="rs_chip")))
    for n in REPLICATED:
        g[n] = g[n].reshape(a[n].shape)
    return loss[0, 0], dh0[BLK:].reshape(1, seq, D_MODEL), g, landed


def kernel(*args):
    a = dict(zip(INPUTS, args))
    first = list(SMALL_SHARDED) + ["l0_w_in"]
    got = all_gather([a[n].astype(MXU) if n in MATS else a[n] for n in first], name="gather_first")
    full = {n: _from_shards(n, blocks) for n, blocks in zip(first, got)}
    shards = {n: a[n].astype(MXU) for n in GATHER_EARLY + GATHER_LATE}
    loss_part, grad_x, g, landed = _local_step(
        {**a, **{n: full[n] for n in SMALL_SHARDED}}, {"l0_w_in": full["l0_w_in"]}, shards)
    loss = lax.psum(loss_part, ("x", "y", "c"))

    sh_out = {n: adamw(landed[n], a[n], a["m_" + n], a["v_" + n], name="adamw_" + n) for n in SHARDED}

    rp_shapes = [a[n].shape for n in REPLICATED]
    rrows = _round_up(sum(_rows_of(math.prod(s), SMALL_W) for s in rp_shapes), 128)
    gathered = all_gather([_pack([g[n] for n in REPLICATED], SMALL_W, rrows)], name="gather_small_grads")[0]
    rp_out = adamw(gathered, *[_pack([a[p + n] for n in REPLICATED], SMALL_W, rrows) for p in ("", "m_", "v_")],
                   name="adamw_replicated")
    rp_out = [dict(zip(REPLICATED, _unpack(s, rp_shapes, SMALL_W))) for s in rp_out]

    outs = [loss, grad_x]
    for k in range(4):
        outs += [sh_out[n][k] if n in SHARDED else rp_out[k][n] for n in WEIGHTS]
    return tuple(outs)
```

```python
import functools
import math

import jax
import jax.numpy as jnp
import numpy as np
from jax import lax
from jax.experimental import pallas as pl
from jax.experimental.pallas import tpu as pltpu

F32 = jnp.float32
BF16 = jnp.bfloat16
MXU = jnp.bfloat16

D_MODEL = 1024
N_META = 16
BLK = 128
PAD = BLK - N_META
D_RNN = 1024
LRU_C = 8.0
N_Q_HEADS = 16
HEAD_DIM = 64
D_SSM = 2048
SSD_HEADS = 32
SSD_GROUPS = 8
D_FF = 2816
EPS = 1e-6
NEG = -1e30
N_DEV = 8

ADAM_LR = 0.001
ADAM_B1 = 0.9
ADAM_B2 = 0.999
ADAM_EPS = 1e-08
ADAM_WD = 0.01
ADAM_STEP = 10

VMEM_LIMIT = 56 * 1024 * 1024
MESH = pl.DeviceIdType.MESH
ANY = pl.BlockSpec(memory_space=pl.ANY)


def _cp(*sem):
    return pltpu.CompilerParams(dimension_semantics=sem, vmem_limit_bytes=VMEM_LIMIT)


def _pick(n, cands):
    for c in cands:
        if n % c == 0:
            return c
    return n


def _dot(a, b):
    return jnp.dot(a.astype(MXU), b.astype(MXU), preferred_element_type=F32)


def _dot_nt(a, b):
    return lax.dot_general(a.astype(MXU), b.astype(MXU), (((1,), (1,)), ((), ())),
                           preferred_element_type=F32)


def _dot_tn(a, b):
    return jnp.dot(a.T.astype(MXU), b.astype(MXU), preferred_element_type=F32)


def _dot_split(v, e):
    hi = v.astype(BF16)
    lo = (v - hi.astype(F32)).astype(BF16)
    return (jnp.dot(hi, e, preferred_element_type=F32)
            + jnp.dot(lo, e, preferred_element_type=F32))


def _sigmoid(x):
    return 1.0 / (1.0 + jnp.exp(-x))


def _log1p(x):
    u = 1.0 + x
    return jnp.where(u == 1.0, x, jnp.log(u) * (x / jnp.where(u == 1.0, 1.0, u - 1.0)))


def _expm1(x):
    u = jnp.exp(x)
    um1 = u - 1.0
    lg = jnp.log(jnp.where(u > 0.0, u, 1.0))
    safe = (um1 != 0.0) & (um1 != -1.0)
    return jnp.where(um1 == 0.0, x, jnp.where(um1 == -1.0, -1.0,
                                               um1 * (x / jnp.where(safe, lg, 1.0))))


def _softplus(x):
    return jnp.maximum(x, 0.0) + _log1p(jnp.exp(-jnp.abs(x)))


_GC = math.sqrt(2.0 / math.pi)


def _gelu(x):
    t = jnp.tanh(_GC * (x + 0.044715 * x * x * x))
    return 0.5 * x * (1.0 + t)


def _gelu_grad(x):
    t = jnp.tanh(_GC * (x + 0.044715 * x * x * x))
    return 0.5 * (1.0 + t) + 0.5 * x * (1.0 - t * t) * (_GC * (1.0 + 3.0 * 0.044715 * x * x))


def _silu(x):
    return x * _sigmoid(x)


def _silu_grad(x):
    s = _sigmoid(x)
    return s * (1.0 + x * (1.0 - s))


def _rows(shape):
    return lax.broadcasted_iota(jnp.int32, shape, 0)


def _lanes(shape):
    return lax.broadcasted_iota(jnp.int32, shape, 1)


def _shift_down(x, tail, d):
    if d == 0:
        return x
    n = x.shape[0]
    xr = pltpu.roll(x, d, 0)
    tr = pltpu.roll(tail, d, 0)
    first = jnp.where(_rows(tr.shape) < d, tr, xr[0:8])
    return jnp.concatenate([first, xr[8:n]], axis=0)


def _shift_up(x, head, d):
    if d == 0:
        return x
    n = x.shape[0]
    xr = pltpu.roll(x, n - d, 0)
    hr = pltpu.roll(head, 8 - d, 0)
    last = jnp.where(_rows(hr.shape) >= 8 - d, hr, xr[n - 8:n])
    return jnp.concatenate([xr[0:n - 8], last], axis=0)


def _keep(x, valid, s):
    return jnp.where(valid, x, 0.0) if s == 0 else x


def _row_at(x, i):
    return jnp.sum(jnp.where(_rows(x.shape) == i, x, 0.0), axis=0, keepdims=True)


def _scan_fwd(a, u):
    n = a.shape[0]
    ri = _rows(a.shape)
    d = 1
    while d < n:
        m = ri >= d
        us = jnp.where(m, pltpu.roll(u, d, 0), 0.0)
        as_ = jnp.where(m, pltpu.roll(a, d, 0), 1.0)
        u = u + a * us
        a = a * as_
        d *= 2
    return a, u


def _scan_rev(c, u):
    n = c.shape[0]
    ri = _rows(c.shape)
    d = 1
    while d < n:
        m = ri < n - d
        us = jnp.where(m, pltpu.roll(u, n - d, 0), 0.0)
        cs = jnp.where(m, pltpu.roll(c, n - d, 0), 1.0)
        u = u + c * us
        c = c * cs
        d *= 2
    return c, u


def _cumsum_fwd(x):
    n = x.shape[0]
    ri = _rows(x.shape)
    d = 1
    while d < n:
        x = x + jnp.where(ri >= d, pltpu.roll(x, d, 0), 0.0)
        d *= 2
    return x


def _cumsum_rev(x):
    n = x.shape[0]
    ri = _rows(x.shape)
    d = 1
    while d < n:
        x = x + jnp.where(ri < n - d, pltpu.roll(x, n - d, 0), 0.0)
        d *= 2
    return x


MATMUL_VMEM = 40 * 1024 * 1024


def _matmul_tiles(m, n, k, tk, out_bytes):
    best = None
    for tm in (1664, 1408, 1040, 1024, 832, 640, 512, 384, 256, 128):
        if m % tm:
            continue
        for tn in (2048, 1664, 1408, 1024, 896, 640, 512, 384, 256, 128):
            if n % tn:
                continue
            vmem = 2 * (tm * tk * 2 + tk * tn * 2 + tm * tn * out_bytes) + (tm * tn * 4 if k > tk else 0)
            if vmem > MATMUL_VMEM:
                continue
            traffic = (n // tn) * m * k * 2 + (m // tm) * k * n * 2
            if best is None or traffic < best[0]:
                best = (traffic, tm, tn)
    return (best[1], best[2]) if best else (m, n)


def matmul(a, b, *, trans_b=False, out_dtype=F32, name):
    m, k = a.shape
    n = b.shape[0] if trans_b else b.shape[1]
    tk = k if k <= 2048 else _pick(k, (1664, 1408, 1024, 896, 512, 256, 128))
    nk = k // tk
    tm, tn = _matmul_tiles(m, n, k, tk, jnp.dtype(out_dtype).itemsize)

    def product(a_ref, b_ref):
        return _dot_nt(a_ref[...], b_ref[...]) if trans_b else _dot(a_ref[...], b_ref[...])

    def kern_once(a_ref, b_ref, o_ref):
        o_ref[...] = product(a_ref, b_ref).astype(o_ref.dtype)

    def kern_acc(a_ref, b_ref, o_ref, acc_ref):
        kk = pl.program_id(2)

        @pl.when(kk == 0)
        def _():
            acc_ref[...] = product(a_ref, b_ref)

        @pl.when(kk > 0)
        def _():
            acc_ref[...] += product(a_ref, b_ref)

        @pl.when(kk == nk - 1)
        def _():
            o_ref[...] = acc_ref[...].astype(o_ref.dtype)

    b_spec = (pl.BlockSpec((tn, tk), lambda i, j, kk: (j, kk)) if trans_b
              else pl.BlockSpec((tk, tn), lambda i, j, kk: (kk, j)))
    return pl.pallas_call(
        kern_once if nk == 1 else kern_acc, name=name,
        grid=(m // tm, n // tn, nk),
        in_specs=[pl.BlockSpec((tm, tk), lambda i, j, kk: (i, kk)), b_spec],
        out_specs=pl.BlockSpec((tm, tn), lambda i, j, kk: (i, j)),
        out_shape=jax.ShapeDtypeStruct((m, n), out_dtype),
        scratch_shapes=[] if nk == 1 else [pltpu.VMEM((tm, tn), F32)],
        compiler_params=_cp("parallel", "parallel", "arbitrary"),
    )(a, b)


def matmul_cat(a_list, b, *, trans_b=False, out_dtype=F32, name, carry=None):
    m = a_list[0].shape[0]
    ks = [x.shape[1] for x in a_list]
    ktot = sum(ks)
    n = b.shape[0] if trans_b else b.shape[1]
    tn = _pick(n, (512, 256, 128))
    tm = next((c for c in (1664, 1040, 832, 640, 512, 384, 256, 128)
               if m % c == 0 and c * ktot * 2 <= 8 * 1024 * 1024), m)
    na = len(a_list)

    def kern(*refs):
        b_ref, o_ref = refs[na], refs[na + 1]
        acc, off = None, 0
        for a_ref, k in zip(refs[:na], ks):
            if trans_b:
                part = _dot_nt(a_ref[...], b_ref[:, off:off + k])
            else:
                part = _dot(a_ref[...], b_ref[off:off + k, :])
            acc = part if acc is None else acc + part
            off += k
        o_ref[...] = acc.astype(o_ref.dtype)

    b_spec = (pl.BlockSpec((tn, ktot), lambda i, j: (j, 0)) if trans_b
              else pl.BlockSpec((ktot, tn), lambda i, j: (0, j)))
    res = _call(
        kern, (*a_list, b), name=name, grid=(m // tm, n // tn),
        in_specs=[pl.BlockSpec((tm, k), lambda i, j: (i, 0)) for k in ks] + [b_spec],
        out_specs=[pl.BlockSpec((tm, tn), lambda i, j: (i, j))],
        out_shape=[jax.ShapeDtypeStruct((m, n), out_dtype)],
        sem=("parallel", "parallel"), carry=carry)
    return res[0] if carry is None else (res[0], res[1:])


def _row_tile(t):
    return _pick(t, (832, 640, 512, 384, 256, 128))


def rmsnorm_fwd(x, w, res=None, *, out_dtype, name, with_t=False):
    t, d = x.shape
    tr = _conv_tile(t) if with_t else _row_tile(t)

    def kern(*refs):
        x_ref, w_ref = refs[0], refs[1]
        o_ref = refs[-2] if with_t else refs[-1]
        xv = x_ref[...]
        r = lax.rsqrt(jnp.mean(xv * xv, axis=-1, keepdims=True) + EPS)
        y = xv * r * w_ref[...]
        if res is not None:
            y = refs[2][...] + y
        o_ref[...] = y.astype(o_ref.dtype)
        if with_t:
            refs[-1][...] = y.T.astype(o_ref.dtype)

    row = pl.BlockSpec((tr, d), lambda i: (i, 0))
    vec = pl.BlockSpec((1, d), lambda i: (0, 0))
    ins = [x, w.reshape(1, d)] + ([] if res is None else [res])
    specs = [row, vec] + ([] if res is None else [row])
    out_specs, out_shape = row, jax.ShapeDtypeStruct((t, d), out_dtype)
    if with_t:
        out_specs = [row, pl.BlockSpec((d, tr), lambda i: (0, i))]
        out_shape = [out_shape, jax.ShapeDtypeStruct((d, t), out_dtype)]
    return pl.pallas_call(
        kern, name=name, grid=(t // tr,), in_specs=specs, out_specs=out_specs, out_shape=out_shape,
        compiler_params=_cp("parallel"),
    )(*ins)


def rmsnorm_bwd(x, w, dy, res=None, *, out_dtype, name):
    t, d = x.shape
    tr = _row_tile(t)

    def kern(*refs):
        if res is None:
            x_ref, w_ref, dy_ref, dx_ref, dw_ref = refs
        else:
            x_ref, w_ref, dy_ref, r_ref, dx_ref, dw_ref = refs
        i = pl.program_id(0)
        xv = x_ref[...]
        dyv = dy_ref[...].astype(F32)
        r = lax.rsqrt(jnp.mean(xv * xv, axis=-1, keepdims=True) + EPS)
        xh = xv * r
        g = dyv * w_ref[...]
        dx = r * (g - xh * jnp.mean(g * xh, axis=-1, keepdims=True))
        if res is not None:
            dx = r_ref[...] + dx
        dx_ref[...] = dx.astype(dx_ref.dtype)
        part = jnp.sum(dyv * xh, axis=0, keepdims=True)

        @pl.when(i == 0)
        def _():
            dw_ref[...] = part

        @pl.when(i > 0)
        def _():
            dw_ref[...] += part

    row = pl.BlockSpec((tr, d), lambda i: (i, 0))
    vec = pl.BlockSpec((1, d), lambda i: (0, 0))
    ins = [x, w.reshape(1, d), dy] + ([] if res is None else [res])
    specs = [row, vec, row] + ([] if res is None else [row])
    return pl.pallas_call(
        kern, name=name, grid=(t // tr,), in_specs=specs, out_specs=[row, vec],
        out_shape=[jax.ShapeDtypeStruct((t, d), out_dtype), jax.ShapeDtypeStruct((1, d), F32)],
        compiler_params=_cp("arbitrary"),
    )(*ins)


def _conv_tile(t):
    return _pick(t, (640, 384, 256, 128))


def _conv_apply(x, tail, cw, cb, ksz):
    y = cb
    for k in range(ksz):
        y = y + cw[k:k + 1, :] * _shift_down(x, tail, ksz - 1 - k)
    return y


def dwconv_fwd(x, cw, cb, *, mode, x_off, c_out, cblk, out_dtype, name, with_t=False):
    t = x.shape[0]
    ksz = cw.shape[0]
    tb = _conv_tile(t)
    nb, ncb, t8 = t // tb, c_out // cblk, tb // 8
    xo = x_off // cblk
    nin = 2 if mode == "geglu" else 1

    def kern(*refs):
        o_ref = refs[-2] if with_t else refs[-1]
        n = pl.program_id(1)
        for c in range(cblk // BLK):
            ls = slice(c * BLK, (c + 1) * BLK)
            for s in range(tb // BLK):
                rs = slice(s * BLK, (s + 1) * BLK)
                valid = (n * tb + s * BLK + _rows((BLK, BLK))) >= PAD
                hs = []
                for q in range(nin):
                    x_ref, t_ref, w_ref, b_ref = refs[4 * q:4 * q + 4]
                    tail = (jnp.where(n > 0, t_ref[:, ls], 0.0) if s == 0
                            else x_ref[s * BLK - 8:s * BLK, ls])
                    hs.append(_conv_apply(x_ref[rs, ls], tail, w_ref[:, ls], b_ref[:, ls], ksz))
                y = _gelu(hs[0]) * hs[1] if mode == "geglu" else _silu(hs[0])
                y = _keep(y, valid, s)
                o_ref[rs, ls] = y.astype(o_ref.dtype)
                if with_t:
                    refs[-1][ls, rs] = y.T.astype(o_ref.dtype)

    ins, specs = [], []
    for q in range(nin):
        co = xo + q * ncb
        wo = q * ncb
        ins += [x, x, cw, cb.reshape(1, -1)]
        specs += [
            pl.BlockSpec((tb, cblk), lambda j, n, co=co: (n, co + j)),
            pl.BlockSpec((8, cblk), lambda j, n, co=co: (jnp.maximum(n * t8 - 1, 0), co + j)),
            pl.BlockSpec((ksz, cblk), lambda j, n, wo=wo: (0, wo + j)),
            pl.BlockSpec((1, cblk), lambda j, n, wo=wo: (0, wo + j)),
        ]
    out_specs = pl.BlockSpec((tb, cblk), lambda j, n: (n, j))
    out_shape = jax.ShapeDtypeStruct((t, c_out), out_dtype)
    if with_t:
        out_specs = [out_specs, pl.BlockSpec((cblk, tb), lambda j, n: (j, n))]
        out_shape = [out_shape, jax.ShapeDtypeStruct((c_out, t), out_dtype)]
    return pl.pallas_call(
        kern, name=name, grid=(ncb, nb), in_specs=specs, out_specs=out_specs, out_shape=out_shape,
        compiler_params=_cp("parallel", "parallel"),
    )(*ins)


def dwconv_bwd(x, cw, cb, dy, *, mode, x_off, c_out, cblk, name, carry=None):
    t = x.shape[0]
    ksz = cw.shape[0]
    tb = _conv_tile(t)
    nb, ncb, t8 = t // tb, c_out // cblk, tb // 8
    xo = x_off // cblk
    nin = 2 if mode == "geglu" else 1
    ctot = nin * c_out

    def kern(*refs):
        dy_ref = refs[4 * nin]
        outs = refs[4 * nin + 1:4 * nin + 1 + 3 * nin]
        heads = refs[4 * nin + 1 + 3 * nin:]
        n = pl.program_id(1)
        blk = nb - 1 - n

        @pl.when(n == 0)
        def _():
            for q in range(nin):
                heads[q][...] = jnp.zeros_like(heads[q])
                outs[3 * q + 1][...] = jnp.zeros_like(outs[3 * q + 1])
                outs[3 * q + 2][...] = jnp.zeros_like(outs[3 * q + 2])

        for c in range(cblk // BLK):
            ls = slice(c * BLK, (c + 1) * BLK)
            head = [heads[q][:, ls] for q in range(nin)]
            dwa = [[None] * ksz for _ in range(nin)]
            dba = [None] * nin
            for s in reversed(range(tb // BLK)):
                rs = slice(s * BLK, (s + 1) * BLK)
                valid = (blk * tb + s * BLK + _rows((BLK, BLK))) >= PAD
                xs, tails, hs = [], [], []
                for q in range(nin):
                    x_ref, t_ref, w_ref, b_ref = refs[4 * q:4 * q + 4]
                    tail = (jnp.where(blk > 0, t_ref[:, ls], 0.0) if s == 0
                            else x_ref[s * BLK - 8:s * BLK, ls])
                    xs.append(x_ref[rs, ls])
                    tails.append(tail)
                    hs.append(_conv_apply(xs[q], tail, w_ref[:, ls], b_ref[:, ls], ksz))
                dyv = dy_ref[rs, ls].astype(F32)
                if mode == "geglu":
                    dhs = [dyv * hs[1] * _gelu_grad(hs[0]), dyv * _gelu(hs[0])]
                else:
                    dhs = [dyv * _silu_grad(hs[0])]
                for q in range(nin):
                    w_ref = refs[4 * q + 2]
                    dh = _keep(dhs[q], valid, s)
                    dx = jnp.zeros_like(dh)
                    for k in range(ksz):
                        sh = ksz - 1 - k
                        dx = dx + w_ref[k:k + 1, ls] * _shift_up(dh, head[q], sh)
                        part = jnp.sum(dh * _shift_down(xs[q], tails[q], sh), axis=0, keepdims=True)
                        dwa[q][k] = part if dwa[q][k] is None else dwa[q][k] + part
                    outs[3 * q][rs, ls] = _keep(dx, valid, s).astype(outs[3 * q].dtype)
                    part = jnp.sum(dh, axis=0, keepdims=True)
                    dba[q] = part if dba[q] is None else dba[q] + part
                    head[q] = dh[0:8]
            for q in range(nin):
                outs[3 * q + 1][:, ls] += jnp.concatenate(dwa[q], axis=0)
                outs[3 * q + 2][:, ls] += dba[q]
                heads[q][:, ls] = head[q]

    ins, specs, out_specs, out_shape, scratch = [], [], [], [], []
    for q in range(nin):
        co = xo + q * ncb
        wo = q * ncb
        ins += [x, x, cw, cb.reshape(1, -1)]
        specs += [
            pl.BlockSpec((tb, cblk), lambda j, n, co=co: (nb - 1 - n, co + j)),
            pl.BlockSpec((8, cblk), lambda j, n, co=co: (jnp.maximum((nb - 1 - n) * t8 - 1, 0), co + j)),
            pl.BlockSpec((ksz, cblk), lambda j, n, wo=wo: (0, wo + j)),
            pl.BlockSpec((1, cblk), lambda j, n, wo=wo: (0, wo + j)),
        ]
        out_specs += [
            pl.BlockSpec((tb, cblk), lambda j, n: (nb - 1 - n, j)),
            pl.BlockSpec((ksz, cblk), lambda j, n: (0, j)),
            pl.BlockSpec((1, cblk), lambda j, n: (0, j)),
        ]
        out_shape += [jax.ShapeDtypeStruct((t, c_out), MXU),
                      jax.ShapeDtypeStruct((ksz, c_out), F32),
                      jax.ShapeDtypeStruct((1, c_out), F32)]
        scratch.append(pltpu.VMEM((8, cblk), F32))
    ins.append(dy)
    specs.append(pl.BlockSpec((tb, cblk), lambda j, n: (nb - 1 - n, j)))
    res = _call(kern, ins, name=name, grid=(ncb, nb), in_specs=specs, out_specs=out_specs,
                out_shape=out_shape, scratch_shapes=scratch, sem=("parallel", "arbitrary"), carry=carry)
    dxs = [res[3 * q] for q in range(nin)]
    dcw = jnp.concatenate([res[3 * q + 1] for q in range(nin)], axis=1)
    dcb = jnp.concatenate([res[3 * q + 2] for q in range(nin)], axis=1)
    return dxs, dcw, dcb.reshape(ctot), res[3 * nin:]


def _lru_tile(t):
    return _pick(t, (640, 384, 256, 128))


def _lru_gates(xc, wa, ba, wx, bx, sp):
    r = _sigmoid(_dot(xc, wa) + ba)
    i = _sigmoid(_dot(xc, wx) + bx)
    log_a = -LRU_C * r * sp
    a = jnp.exp(log_a)
    mult = jnp.sqrt(-_expm1(2.0 * log_a))
    return r, i, a, mult


def lru_fwd(proj, cw, cb, wa, ba, wx, bx, lam, *, gate_off, xr_off, name, carry=None):
    t = proj.shape[0]
    tb = _lru_tile(t)
    nb, ns, t8 = t // tb, tb // BLK, tb // 8
    go, xo = gate_off // BLK, xr_off // BLK

    def kern(g_ref, x_ref, xt_ref, cw_ref, cb_ref, wa_ref, ba_ref, wx_ref, bx_ref, lam_ref,
             y_ref, yt_ref, h_ref, hc_ref):
        n = pl.program_id(1)

        @pl.when(n == 0)
        def _():
            hc_ref[...] = jnp.zeros_like(hc_ref)

        sp = _softplus(-lam_ref[...])
        hprev = hc_ref[0:1, :]
        for s in range(ns):
            sl = slice(s * BLK, (s + 1) * BLK)
            xv = x_ref[sl, :]
            tail = jnp.where(n > 0, xt_ref[...], 0.0) if s == 0 else x_ref[s * BLK - 8:s * BLK, :]
            valid = (n * tb + s * BLK + _rows((BLK, BLK))) >= PAD
            xc = _keep(_conv_apply(xv, tail, cw_ref[...], cb_ref[...], 4), valid, s)
            _, i, a, mult = _lru_gates(xc, wa_ref[0], ba_ref[...], wx_ref[0], bx_ref[...], sp)
            u = mult * (i * xc)
            ca, cu = _scan_fwd(a, u)
            h = cu + ca * hprev
            hprev = _row_at(h, BLK - 1)
            h_ref[sl, :] = h
            y = _gelu(g_ref[sl, :]) * h
            y_ref[sl, :] = y.astype(y_ref.dtype)
            yt_ref[:, sl] = y.T.astype(yt_ref.dtype)
        hc_ref[...] = jnp.broadcast_to(hprev, hc_ref.shape)

    vec = pl.BlockSpec((1, BLK), lambda j, n: (0, j))
    mat = pl.BlockSpec((1, BLK, BLK), lambda j, n: (j, 0, 0))
    return _call(
        kern, (proj, proj, proj, cw, cb.reshape(1, -1), wa, ba.reshape(1, -1), wx, bx.reshape(1, -1),
               lam.reshape(1, -1)),
        name=name, grid=(D_RNN // BLK, nb),
        in_specs=[
            pl.BlockSpec((tb, BLK), lambda j, n: (n, go + j)),
            pl.BlockSpec((tb, BLK), lambda j, n: (n, xo + j)),
            pl.BlockSpec((8, BLK), lambda j, n: (jnp.maximum(n * t8 - 1, 0), xo + j)),
            pl.BlockSpec((4, BLK), lambda j, n: (0, j)), vec, mat, vec, mat, vec, vec,
        ],
        out_specs=[pl.BlockSpec((tb, BLK), lambda j, n: (n, j)),
                   pl.BlockSpec((BLK, tb), lambda j, n: (j, n)),
                   pl.BlockSpec((tb, BLK), lambda j, n: (n, j))],
        out_shape=[jax.ShapeDtypeStruct((t, D_RNN), MXU), jax.ShapeDtypeStruct((D_RNN, t), MXU),
                   jax.ShapeDtypeStruct((t, D_RNN), F32)],
        scratch_shapes=[pltpu.VMEM((8, BLK), F32)],
        sem=("parallel", "arbitrary"), carry=carry)


def lru_bwd(proj, h, dy, cw, cb, wa, ba, wx, bx, lam, *, gate_off, xr_off, dy_off, name):
    t = proj.shape[0]
    tb = _lru_tile(t)
    nb, ns, t8 = t // tb, tb // BLK, tb // 8
    go, xo, do = gate_off // BLK, xr_off // BLK, dy_off // BLK

    def kern(g_ref, x_ref, xt_ref, h_ref, ht_ref, dy_ref, cw_ref, cb_ref, wa_ref, ba_ref,
             wx_ref, bx_ref, lam_ref,
             dg_ref, dx_ref, dcw_ref, dcb_ref, dwa_ref, dba_ref, dwx_ref, dbx_ref, dlam_ref,
             gin_ref, head_ref):
        n = pl.program_id(1)
        blk = nb - 1 - n

        @pl.when(n == 0)
        def _():
            gin_ref[...] = jnp.zeros_like(gin_ref)
            head_ref[...] = jnp.zeros_like(head_ref)
            for r_ in (dcw_ref, dcb_ref, dwa_ref, dba_ref, dwx_ref, dbx_ref, dlam_ref):
                r_[...] = jnp.zeros_like(r_)

        lamv = lam_ref[...]
        sp = _softplus(-lamv)
        dsp_dlam = -_sigmoid(-lamv)
        g_in = gin_ref[0:1, :]
        head = head_ref[...]
        ones8 = jnp.ones((8, BLK), F32)
        for s in reversed(range(ns)):
            sl = slice(s * BLK, (s + 1) * BLK)
            xv = x_ref[sl, :]
            if s == 0:
                tail = jnp.where(blk > 0, xt_ref[...], 0.0)
                htail = jnp.where(blk > 0, ht_ref[...], 0.0)
            else:
                tail = x_ref[s * BLK - 8:s * BLK, :]
                htail = h_ref[s * BLK - 8:s * BLK, :]
            valid = (blk * tb + s * BLK + _rows((BLK, BLK))) >= PAD
            xc = _keep(_conv_apply(xv, tail, cw_ref[...], cb_ref[...], 4), valid, s)
            wav, wxv = wa_ref[0], wx_ref[0]
            r, i, a, mult = _lru_gates(xc, wav, ba_ref[...], wxv, bx_ref[...], sp)
            hv = h_ref[sl, :]
            hprev = _shift_down(hv, htail, 1)
            gv = g_ref[sl, :]
            dyv = dy_ref[sl, :].astype(F32)
            dh = dyv * _gelu(gv)
            dg_ref[sl, :] = (dyv * hv * _gelu_grad(gv)).astype(dg_ref.dtype)
            c = _shift_up(a, ones8, 1)
            cc, cu = _scan_rev(c, dh)
            gg = cu + cc * g_in
            g_in = _row_at(a * gg, 0)
            da = gg * hprev
            di = gg * mult * xc
            dxc = gg * mult * i
            dmult = gg * i * xc
            dlog_a = da * a - dmult * (a * a) / mult
            dr = dlog_a * (-LRU_C * sp)
            dlam_ref[...] += jnp.sum(dlog_a * (-LRU_C) * r, axis=0, keepdims=True) * dsp_dlam
            dpr = dr * r * (1.0 - r)
            dpi = di * i * (1.0 - i)
            dxc = dxc + _dot_nt(dpr, wav) + _dot_nt(dpi, wxv)
            dxc, dpr, dpi = _keep(dxc, valid, s), _keep(dpr, valid, s), _keep(dpi, valid, s)
            dwa_ref[0] += _dot_tn(xc, dpr)
            dwx_ref[0] += _dot_tn(xc, dpi)
            dba_ref[...] += jnp.sum(dpr, axis=0, keepdims=True)
            dbx_ref[...] += jnp.sum(dpi, axis=0, keepdims=True)
            dx = jnp.zeros_like(dxc)
            dws = []
            for k in range(4):
                dx = dx + cw_ref[k:k + 1, :] * _shift_up(dxc, head, 3 - k)
                dws.append(jnp.sum(dxc * _shift_down(xv, tail, 3 - k), axis=0, keepdims=True))
            dx_ref[sl, :] = _keep(dx, valid, s).astype(dx_ref.dtype)
            dcw_ref[...] += jnp.concatenate(dws, axis=0)
            dcb_ref[...] += jnp.sum(dxc, axis=0, keepdims=True)
            head = dxc[0:8]
        gin_ref[...] = jnp.broadcast_to(g_in, gin_ref.shape)
        head_ref[...] = head

    vec = pl.BlockSpec((1, BLK), lambda j, n: (0, j))
    mat = pl.BlockSpec((1, BLK, BLK), lambda j, n: (j, 0, 0))
    cws = pl.BlockSpec((4, BLK), lambda j, n: (0, j))

    def rb(off):
        return pl.BlockSpec((tb, BLK), lambda j, n: (nb - 1 - n, off + j))

    def tl(off):
        return pl.BlockSpec((8, BLK), lambda j, n: (jnp.maximum((nb - 1 - n) * t8 - 1, 0), off + j))

    return pl.pallas_call(
        kern, name=name, grid=(D_RNN // BLK, nb),
        in_specs=[rb(go), rb(xo), tl(xo), rb(0), tl(0), rb(do), cws, vec, mat, vec, mat, vec, vec],
        out_specs=[rb(0), rb(0), cws, vec, mat, vec, mat, vec, vec],
        out_shape=[jax.ShapeDtypeStruct((t, D_RNN), MXU), jax.ShapeDtypeStruct((t, D_RNN), MXU),
                   jax.ShapeDtypeStruct((4, D_RNN), F32), jax.ShapeDtypeStruct((1, D_RNN), F32),
                   jax.ShapeDtypeStruct((8, BLK, BLK), F32), jax.ShapeDtypeStruct((1, D_RNN), F32),
                   jax.ShapeDtypeStruct((8, BLK, BLK), F32), jax.ShapeDtypeStruct((1, D_RNN), F32),
                   jax.ShapeDtypeStruct((1, D_RNN), F32)],
        scratch_shapes=[pltpu.VMEM((8, BLK), F32), pltpu.VMEM((8, BLK), F32)],
        compiler_params=_cp("parallel", "arbitrary"),
    )(proj, proj, proj, h, h, dy, cw, cb.reshape(1, -1), wa, ba.reshape(1, -1), wx,
      bx.reshape(1, -1), lam.reshape(1, -1))


_SCALE = HEAD_DIM ** -0.5


STK = 4


def _attn_masks(n):
    qi = np.arange(STK * BLK)[:, None] % BLK
    c = np.arange(3 * BLK)[None, :]
    tq = n * BLK + qi - PAD
    s_band = (n - 1) * BLK + c - PAD
    d_band = tq - s_band
    ok_band = (s_band >= N_META) & (d_band >= 0) & (d_band < BLK)
    jm = c - 2 * BLK
    d_meta = tq - (jm - PAD)
    ok_meta = (jm >= PAD) & (d_meta >= 0)
    is_band = c < 2 * BLK
    ok = np.where(is_band, ok_band, ok_meta)
    dist = np.where(is_band, d_band, np.minimum(d_meta, BLK)).astype(np.float32)
    return ok, dist


def _stack_heads(g, e):
    return [8 * g + 2 * i + e for i in range(STK)]


def _attn_bias_table():
    tabs = []
    for n in range(3):
        ok, dist = _attn_masks(n)
        per = []
        for g in range(2):
            for e in range(2):
                slope = np.repeat(np.array([2.0 ** (-8.0 * (h + 1) / N_Q_HEADS) for h in _stack_heads(g, e)],
                                           np.float32), BLK)[:, None]
                per.append(np.where(ok, -(slope * dist), np.float32(NEG)).astype(np.float32))
        tabs.append(np.stack(per))
    return jnp.asarray(np.stack(tabs))


def _stack_sinks(heads, sk):
    return jnp.concatenate(
        [jnp.broadcast_to(jnp.sum(jnp.where(_lanes(sk.shape) == h, sk, 0.0), axis=1, keepdims=True),
                          (BLK, 1)) for h in heads], axis=0)


def _stack_tiles(ref, g, sel):
    return jnp.concatenate(
        [jnp.where(sel, ref[:, (4 * g + i) * BLK:(4 * g + i + 1) * BLK].astype(F32), 0.0)
         for i in range(STK)], axis=0)


def _attn_probs(qk, bias, sink):
    s = qk * _SCALE + bias
    mx = jnp.maximum(jnp.max(s, axis=-1, keepdims=True), sink)
    p = jnp.exp(s - mx)
    es = jnp.exp(sink - mx)
    inv = 1.0 / (jnp.sum(p, axis=-1, keepdims=True) + es)
    return p * inv, es * inv


def _attn_specs(t, q_off, k_off, v_off, rev):
    nb = t // BLK
    qo, ko, vo = q_off // 1024, k_off // BLK, v_off // BLK

    def b(n):
        return nb - 1 - n if rev else n

    return [
        pl.BlockSpec((BLK, 1024), lambda n: (b(n), qo)),
        pl.BlockSpec((BLK, BLK), lambda n: (b(n), ko)),
        pl.BlockSpec((BLK, BLK), lambda n: (b(n), vo)),
        pl.BlockSpec((BLK, BLK), lambda n: (jnp.maximum(b(n) - 1, 0), ko)),
        pl.BlockSpec((BLK, BLK), lambda n: (jnp.maximum(b(n) - 1, 0), vo)),
        pl.BlockSpec((BLK, BLK), lambda n: (0, ko)),
        pl.BlockSpec((BLK, BLK), lambda n: (0, vo)),
        pl.BlockSpec((1, BLK), lambda n: (0, 0)),
        pl.BlockSpec((1, 4, STK * BLK, 3 * BLK), lambda n: (jnp.minimum(b(n), 2), 0, 0, 0)),
    ]


def attn_fwd(proj, sinks, *, q_off, k_off, v_off, name, carry=None):
    t = proj.shape[0]
    nb = t // BLK

    def kern(q_ref, kc_ref, vc_ref, kp_ref, vp_ref, km_ref, vm_ref, sk_ref, tab_ref, o_ref):
        k_all = jnp.concatenate([kp_ref[...], kc_ref[...], km_ref[...]], axis=0)
        v_all = jnp.concatenate([vp_ref[...], vc_ref[...], vm_ref[...]], axis=0)
        k_alt = pltpu.roll(k_all, HEAD_DIM, 1)
        v_alt = pltpu.roll(v_all, HEAD_DIM, 1)
        low = _lanes((BLK, BLK)) < HEAD_DIM
        stacks = [(g, e) for g in range(2) for e in range(2)]
        qk = {(g, e): _dot_nt(_stack_tiles(q_ref, g, low == (e == 0)), k_all if g == e else k_alt)
              for g, e in stacks}
        ps = {(g, e): _attn_probs(qk[g, e], tab_ref[0, 2 * g + e],
                                  _stack_sinks(_stack_heads(g, e), sk_ref[...]))[0] for g, e in stacks}
        outs = {(g, e): _dot(ps[g, e], v_all if g == e else v_alt) for g, e in stacks}
        for hp in range(N_Q_HEADS // 2):
            g, rs = hp // STK, slice((hp % STK) * BLK, (hp % STK + 1) * BLK)
            o_ref[:, hp * BLK:(hp + 1) * BLK] = jnp.where(low, outs[g, 0][rs], outs[g, 1][rs]).astype(o_ref.dtype)

    sk = jnp.zeros((1, BLK), F32).at[0, :N_Q_HEADS].set(sinks)
    return _call(
        kern, (proj, proj, proj, proj, proj, proj, proj, sk, _attn_bias_table()), name=name, grid=(nb,),
        in_specs=_attn_specs(t, q_off, k_off, v_off, False),
        out_specs=[pl.BlockSpec((BLK, 1024), lambda n: (n, 0))],
        out_shape=[jax.ShapeDtypeStruct((t, 1024), MXU)],
        sem=("parallel",), carry=carry)


def attn_bwd(proj, sinks, dy, *, q_off, k_off, v_off, dy_off, name, carry=None):
    t = proj.shape[0]
    nb = t // BLK
    do = dy_off // 1024

    def kern(q_ref, kc_ref, vc_ref, kp_ref, vp_ref, km_ref, vm_ref, sk_ref, tab_ref, do_ref,
             dq_ref, dk_ref, dv_ref, dsk_ref, ck_ref, cv_ref, mk_ref, mv_ref):
        n = pl.program_id(0)
        blk = nb - 1 - n

        @pl.when(n == 0)
        def _():
            for r_ in (ck_ref, cv_ref, mk_ref, mv_ref, dsk_ref):
                r_[...] = jnp.zeros_like(r_)

        k_all = jnp.concatenate([kp_ref[...], kc_ref[...], km_ref[...]], axis=0)
        v_all = jnp.concatenate([vp_ref[...], vc_ref[...], vm_ref[...]], axis=0)
        k_alt = pltpu.roll(k_all, HEAD_DIM, 1)
        v_alt = pltpu.roll(v_all, HEAD_DIM, 1)
        low = _lanes((BLK, BLK)) < HEAD_DIM
        lane1 = _lanes((1, BLK))
        dk_all = jnp.zeros((3 * BLK, BLK), F32)
        dv_all = jnp.zeros((3 * BLK, BLK), F32)
        dsk = jnp.zeros((1, BLK), F32)
        stacks = [(g, e) for g in range(2) for e in range(2)]
        qm = {(g, e): _stack_tiles(q_ref, g, low == (e == 0)) for g, e in stacks}
        dom = {(g, e): _stack_tiles(do_ref, g, low == (e == 0)) for g, e in stacks}
        qk = {(g, e): _dot_nt(qm[g, e], k_all if g == e else k_alt) for g, e in stacks}
        dp = {(g, e): _dot_nt(dom[g, e], v_all if g == e else v_alt) for g, e in stacks}
        ps, dss = {}, {}
        for g, e in stacks:
            heads = _stack_heads(g, e)
            p, psink = _attn_probs(qk[g, e], tab_ref[0, 2 * g + e], _stack_sinks(heads, sk_ref[...]))
            delta = jnp.sum(p * dp[g, e], axis=-1, keepdims=True)
            ps[g, e] = p
            dss[g, e] = p * (dp[g, e] - delta) * _SCALE
            psd = psink * delta
            for i, h in enumerate(heads):
                dsk = dsk + jnp.where(lane1 == h, -jnp.sum(psd[i * BLK:(i + 1) * BLK], axis=0, keepdims=True), 0.0)
        dqs = {(g, e): _dot(dss[g, e], k_all if g == e else k_alt) for g, e in stacks}
        for g, e in stacks:
            dkh = _dot_tn(dss[g, e], qm[g, e])
            dvh = _dot_tn(ps[g, e], dom[g, e])
            if g != e:
                dkh = pltpu.roll(dkh, HEAD_DIM, 1)
                dvh = pltpu.roll(dvh, HEAD_DIM, 1)
            dk_all = dk_all + dkh
            dv_all = dv_all + dvh
        for hp in range(N_Q_HEADS // 2):
            g, rs = hp // STK, slice((hp % STK) * BLK, (hp % STK + 1) * BLK)
            dq_ref[:, hp * BLK:(hp + 1) * BLK] = jnp.where(low, dqs[g, 0][rs], dqs[g, 1][rs]).astype(dq_ref.dtype)
        dsk_ref[...] += dsk
        mk_ref[...] += dk_all[2 * BLK:3 * BLK]
        mv_ref[...] += dv_all[2 * BLK:3 * BLK]
        is0 = blk == 0
        dk_ref[...] = (dk_all[BLK:2 * BLK] + ck_ref[...] + jnp.where(is0, mk_ref[...], 0.0)).astype(dk_ref.dtype)
        dv_ref[...] = (dv_all[BLK:2 * BLK] + cv_ref[...] + jnp.where(is0, mv_ref[...], 0.0)).astype(dv_ref.dtype)
        ck_ref[...] = dk_all[0:BLK]
        cv_ref[...] = dv_all[0:BLK]

    sk = jnp.zeros((1, BLK), F32).at[0, :N_Q_HEADS].set(sinks)
    kv = pl.BlockSpec((BLK, BLK), lambda n: (nb - 1 - n, 0))
    res = _call(
        kern, (proj, proj, proj, proj, proj, proj, proj, sk, _attn_bias_table(), dy), name=name, grid=(nb,),
        in_specs=_attn_specs(t, q_off, k_off, v_off, True)
        + [pl.BlockSpec((BLK, 1024), lambda n: (nb - 1 - n, do))],
        out_specs=[pl.BlockSpec((BLK, 1024), lambda n: (nb - 1 - n, 0)), kv, kv,
                   pl.BlockSpec((1, BLK), lambda n: (0, 0))],
        out_shape=[jax.ShapeDtypeStruct((t, 1024), MXU), jax.ShapeDtypeStruct((t, BLK), MXU),
                   jax.ShapeDtypeStruct((t, BLK), MXU), jax.ShapeDtypeStruct((1, BLK), F32)],
        scratch_shapes=[pltpu.VMEM((BLK, BLK), F32)] * 4,
        sem=("arbitrary",), carry=carry)
    return [res[0], res[1], res[2], res[3][0, :N_Q_HEADS]] + res[4:]


GW = D_SSM // SSD_GROUPS
EXP_ROWS = 3 * BLK + 8
RED_ROWS = EXP_ROWS + 8


def _head_expand():
    ch = jnp.arange(D_SSM) // HEAD_DIM
    return (jnp.arange(BLK)[:, None] == ch[None, :]).astype(BF16)


def _ssd_decay(raw, dtb, alog, rowv):
    valid = rowv & (_lanes((BLK, BLK)) < SSD_HEADS)
    pre = raw + dtb
    dtp = jnp.where(valid, _softplus(pre), 0.0)
    av = -jnp.exp(alog)
    cs = _cumsum_fwd(dtp * av)
    cs_last = _row_at(cs, BLK - 1)
    return valid, pre, dtp, av, cs, jnp.exp(cs), jnp.exp(cs_last - cs), jnp.exp(cs_last)


def _head_col(x, h):
    return jnp.sum(jnp.where(_lanes(x.shape) == h, x, 0.0), axis=1, keepdims=True)


def _ssd_group_fwd(g, xdt, cs, cst, bg, cg, tril, low):
    cb = _dot_nt(cg, bg)
    ys, lm = [], []
    for j in range(2):
        xp = xdt[:, g * GW + j * BLK:g * GW + (j + 1) * BLK]
        hv = []
        for e in range(2):
            h = 4 * g + 2 * j + e
            seg = _head_col(cs, h) - _row_at(cst, h)
            lmat = jnp.where(tril, jnp.exp(jnp.minimum(seg, 0.0)), 0.0)
            mmat = cb * lmat
            lm.append((lmat, mmat))
            hv.append(_dot(mmat, xp))
        ys.append(jnp.where(low, hv[0], hv[1]))
    return jnp.concatenate(ys, axis=1), lm


def _ssd_specs(t, z_off, dt_off, rev):
    nb = t // BLK
    zo, dto = z_off // D_SSM, dt_off // BLK

    def b(n):
        return nb - 1 - n if rev else n

    vec = lambda w: pl.BlockSpec((1, w), lambda n: (0, 0))
    return [
        pl.BlockSpec((BLK, D_SSM), lambda n: (b(n), 0)),
        pl.BlockSpec((BLK, 1024), lambda n: (b(n), 2)),
        pl.BlockSpec((BLK, 1024), lambda n: (b(n), 3)),
        pl.BlockSpec((BLK, D_SSM), lambda n: (b(n), zo)),
        pl.BlockSpec((BLK, BLK), lambda n: (b(n), dto)),
        vec(BLK), vec(BLK), vec(D_SSM), vec(D_SSM),
        pl.BlockSpec((BLK, D_SSM), lambda n: (0, 0)),
    ]


def _pad128(v):
    return jnp.zeros((1, BLK), F32).at[0, :v.shape[0]].set(v)


def ssd_fwd(xbc, proj, dt_bias, a_log, d_skip, gate_norm, *, z_off, dt_off, name):
    t = xbc.shape[0]
    nb = t // BLK

    def kern(x_ref, b_ref, c_ref, z_ref, dt_ref, dtb_ref, alog_ref, dsk_ref, gn_ref, e_ref,
             yn_ref, ynt_ref, st_ref, p_ref):
        n = pl.program_id(0)

        @pl.when(n == 0)
        def _():
            p_ref[...] = jnp.zeros_like(p_ref)

        rowv = (n * BLK + _rows((BLK, BLK))) >= PAD
        _, _, dtp, _, cs, ecs, w, dec = _ssd_decay(dt_ref[...], dtb_ref[...], alog_ref[...], rowv)
        ex = _dot_split(jnp.concatenate([dtp, ecs, w, jnp.broadcast_to(dec, (8, BLK))], axis=0),
                        e_ref[...])
        dtp_c, ecs_c, w_c = ex[0:BLK], ex[BLK:2 * BLK], ex[2 * BLK:3 * BLK]
        dec_c = jnp.max(ex[3 * BLK:EXP_ROWS], axis=0, keepdims=True)
        xv = x_ref[...]
        xdt = xv * dtp_c
        wx = w_c * xdt
        cst = cs.T
        tril = _rows((BLK, BLK)) >= _lanes((BLK, BLK))
        low = _lanes((BLK, BLK)) < HEAD_DIM
        st_ref[0] = p_ref[...]
        for g in range(SSD_GROUPS):
            gs = slice(g * GW, (g + 1) * GW)
            bg = b_ref[:, g * BLK:(g + 1) * BLK]
            cg = c_ref[:, g * BLK:(g + 1) * BLK]
            pg = p_ref[g]
            ydiag, _ = _ssd_group_fwd(g, xdt, cs, cst, bg, cg, tril, low)
            y = ydiag + _dot(cg, pg) * ecs_c[:, gs] + dsk_ref[:, gs] * xv[:, gs]
            p_ref[g] = pg * dec_c[:, gs] + _dot_tn(bg, wx[:, gs])
            yz = y * _silu(z_ref[:, gs])
            r = lax.rsqrt(jnp.mean(yz * yz, axis=-1, keepdims=True) + EPS)
            yn = yz * r * gn_ref[:, gs]
            yn_ref[:, gs] = yn.astype(yn_ref.dtype)
            ynt_ref[gs, :] = yn.T.astype(ynt_ref.dtype)

    return pl.pallas_call(
        kern, name=name, grid=(nb,),
        in_specs=_ssd_specs(t, z_off, dt_off, False),
        out_specs=[pl.BlockSpec((BLK, D_SSM), lambda n: (n, 0)),
                   pl.BlockSpec((D_SSM, BLK), lambda n: (0, n)),
                   pl.BlockSpec((1, SSD_GROUPS, BLK, GW), lambda n: (n, 0, 0, 0))],
        out_shape=[jax.ShapeDtypeStruct((t, D_SSM), MXU), jax.ShapeDtypeStruct((D_SSM, t), MXU),
                   jax.ShapeDtypeStruct((nb, SSD_GROUPS, BLK, GW), F32)],
        scratch_shapes=[pltpu.VMEM((SSD_GROUPS, BLK, GW), F32)],
        compiler_params=_cp("arbitrary"),
    )(xbc, xbc, xbc, proj, proj, _pad128(dt_bias), _pad128(a_log),
      jnp.repeat(d_skip, HEAD_DIM).reshape(1, D_SSM), gate_norm.reshape(1, D_SSM), _head_expand())


def ssd_bwd(xbc, proj, st, dyn, dt_bias, a_log, d_skip, gate_norm, *, z_off, dt_off, name, carry=None):
    t = xbc.shape[0]
    nb = t // BLK

    def kern(x_ref, b_ref, c_ref, z_ref, dt_ref, dtb_ref, alog_ref, dsk_ref, gn_ref, e_ref,
             et_ref, st_ref, dyn_ref,
             dxbc_ref, dz_ref, draw_ref, dgn_ref, ddsk_ref, ddtb_ref, dalog_ref,
             dp_ref, tr_ref):
        n = pl.program_id(0)
        blk = nb - 1 - n

        @pl.when(n == 0)
        def _():
            for r_ in (dp_ref, dgn_ref, ddsk_ref, ddtb_ref, dalog_ref):
                r_[...] = jnp.zeros_like(r_)

        rowv = (blk * BLK + _rows((BLK, BLK))) >= PAD
        valid, pre, dtp, av, cs, ecs, w, dec = _ssd_decay(dt_ref[...], dtb_ref[...], alog_ref[...], rowv)
        ex = _dot_split(jnp.concatenate([dtp, ecs, w, jnp.broadcast_to(dec, (8, BLK))], axis=0),
                        e_ref[...])
        dtp_c, ecs_c, w_c = ex[0:BLK], ex[BLK:2 * BLK], ex[2 * BLK:3 * BLK]
        dec_c = jnp.max(ex[3 * BLK:EXP_ROWS], axis=0, keepdims=True)
        xv = x_ref[...]
        xdt = xv * dtp_c
        wx = w_c * xdt
        cst = cs.T
        tril = _rows((BLK, BLK)) >= _lanes((BLK, BLK))
        lane = _lanes((BLK, BLK))
        rowi = _rows((BLK, BLK))
        low = lane < HEAD_DIM
        dcs = jnp.zeros((BLK, BLK), F32)
        dcst = jnp.zeros((BLK, BLK), F32)
        for g in range(SSD_GROUPS):
            gs = slice(g * GW, (g + 1) * GW)
            bg = b_ref[:, g * BLK:(g + 1) * BLK]
            cg = c_ref[:, g * BLK:(g + 1) * BLK]
            pg = st_ref[0, g]
            dpn = dp_ref[g]
            xg = xv[:, gs]
            ydiag, lm = _ssd_group_fwd(g, xdt, cs, cst, bg, cg, tril, low)
            yoff = _dot(cg, pg) * ecs_c[:, gs]
            y = ydiag + yoff + dsk_ref[:, gs] * xg
            zz = z_ref[:, gs]
            sz = _silu(zz)
            yz = y * sz
            r = lax.rsqrt(jnp.mean(yz * yz, axis=-1, keepdims=True) + EPS)
            yhat = yz * r
            dynv = dyn_ref[:, gs].astype(F32)
            gy = dynv * gn_ref[:, gs]
            dgn_ref[:, gs] += jnp.sum(dynv * yhat, axis=0, keepdims=True)
            dyz = r * (gy - yhat * jnp.mean(gy * yhat, axis=-1, keepdims=True))
            dy = dyz * sz
            dz_ref[:, gs] = (dyz * y * _silu_grad(zz)).astype(dz_ref.dtype)
            tr_ref[EXP_ROWS:RED_ROWS, gs] = jnp.broadcast_to(
                jnp.sum(dy * xg, axis=0, keepdims=True), (8, GW))
            dx = dsk_ref[:, gs] * dy
            dwx = _dot(bg, dpn)
            dxdt = w_c[:, gs] * dwx
            tr_ref[0:BLK, gs] = dwx * wx[:, gs]
            dbg = _dot_nt(wx[:, gs], dpn)
            dzo = ecs_c[:, gs] * dy
            tr_ref[BLK:2 * BLK, gs] = dy * yoff
            dcg = _dot_nt(dzo, pg)
            dp_ref[g] = dec_c[:, gs] * dpn + _dot_tn(cg, dzo)
            tr_ref[3 * BLK:EXP_ROWS, gs] = jnp.broadcast_to(
                jnp.sum(dpn * pg, axis=0, keepdims=True), (8, GW))
            dcb = jnp.zeros((BLK, BLK), F32)
            pairs = []
            for j in range(2):
                ps = slice(g * GW + j * BLK, g * GW + (j + 1) * BLK)
                xp = xdt[:, ps]
                dyp = dy[:, j * BLK:(j + 1) * BLK]
                acc = jnp.zeros((BLK, BLK), F32)
                for e in range(2):
                    h = 4 * g + 2 * j + e
                    lmat, mmat = lm[2 * j + e]
                    dyh = jnp.where(low == (e == 0), dyp, 0.0)
                    dm = jnp.where(tril, _dot_nt(dyh, xp), 0.0)
                    nh = dm * mmat
                    dcs = dcs + jnp.where(lane == h, jnp.sum(nh, axis=1, keepdims=True), 0.0)
                    dcst = dcst - jnp.where(rowi == h, jnp.sum(nh, axis=0, keepdims=True), 0.0)
                    dcb = dcb + dm * lmat
                    acc = acc + _dot_tn(mmat, dyh)
                pairs.append(acc)
            dxdt = dxdt + jnp.concatenate(pairs, axis=1)
            dcg = dcg + _dot(dcb, bg)
            dbg = dbg + _dot_tn(dcb, cg)
            tr_ref[2 * BLK:3 * BLK, gs] = dxdt * xg
            dxbc_ref[:, gs] = dx + dxdt * dtp_c[:, gs]
            dxbc_ref[:, D_SSM + g * BLK:D_SSM + (g + 1) * BLK] = dbg
            dxbc_ref[:, D_SSM + 1024 + g * BLK:D_SSM + 1024 + (g + 1) * BLK] = dcg
        red = _dot(tr_ref[...], et_ref[...])
        r1, r2, r3 = red[0:BLK], red[BLK:2 * BLK], red[2 * BLK:3 * BLK]
        ddec = jnp.max(red[3 * BLK:EXP_ROWS], axis=0, keepdims=True)
        ddsk_ref[...] += jnp.max(red[EXP_ROWS:RED_ROWS], axis=0, keepdims=True)
        dcs = dcs + dcst.T - r1 + r2
        dcs_last = jnp.sum(r1, axis=0, keepdims=True) + ddec * dec
        dcs = dcs + jnp.where(rowi == BLK - 1, dcs_last, 0.0)
        dda = _cumsum_rev(dcs)
        ddtp = r3 + dda * av
        dalog_ref[...] += jnp.sum(dda * dtp, axis=0, keepdims=True) * av
        draw = jnp.where(valid, ddtp * _sigmoid(pre), 0.0)
        ddtb_ref[...] += jnp.sum(draw, axis=0, keepdims=True)
        draw_ref[...] = draw.astype(draw_ref.dtype)

    vec = lambda w_: pl.BlockSpec((1, w_), lambda n: (0, 0))
    rb = lambda w_: pl.BlockSpec((BLK, w_), lambda n: (nb - 1 - n, 0))
    e = _head_expand()
    res = _call(
        kern, (xbc, xbc, xbc, proj, proj, _pad128(dt_bias), _pad128(a_log),
               jnp.repeat(d_skip, HEAD_DIM).reshape(1, D_SSM), gate_norm.reshape(1, D_SSM), e, e.T, st, dyn),
        name=name, grid=(nb,),
        in_specs=_ssd_specs(t, z_off, dt_off, True)
        + [pl.BlockSpec((D_SSM, BLK), lambda n: (0, 0)),
           pl.BlockSpec((1, SSD_GROUPS, BLK, GW), lambda n: (nb - 1 - n, 0, 0, 0)),
           rb(D_SSM)],
        out_specs=[rb(2 * D_SSM), rb(D_SSM), rb(BLK), vec(D_SSM), vec(BLK), vec(BLK), vec(BLK)],
        out_shape=[jax.ShapeDtypeStruct((t, 2 * D_SSM), F32), jax.ShapeDtypeStruct((t, D_SSM), MXU),
                   jax.ShapeDtypeStruct((t, BLK), MXU), jax.ShapeDtypeStruct((1, D_SSM), F32),
                   jax.ShapeDtypeStruct((1, BLK), F32), jax.ShapeDtypeStruct((1, BLK), F32),
                   jax.ShapeDtypeStruct((1, BLK), F32)],
        scratch_shapes=[pltpu.VMEM((SSD_GROUPS, BLK, GW), F32), pltpu.VMEM((RED_ROWS, D_SSM), F32)],
        sem=("arbitrary",), carry=carry)
    dxbc, dz, draw, dgn, ddsk, ddtb, dalog = res[:7]
    return [dxbc, dz, draw, dgn[0], ddsk[0, :SSD_HEADS], ddtb[0, :SSD_HEADS], dalog[0, :SSD_HEADS]] + res[7:]


def loss_fwd_bwd(h, target, *, name):
    t, d = h.shape
    nb = t // BLK

    def kern(h_ref, t_ref, loss_ref, dh_ref):
        n = pl.program_id(0)
        err = jnp.where(n > 0, h_ref[...] - t_ref[...], 0.0)
        dh_ref[...] = err * (1.0 / d)
        part = (0.5 / d) * jnp.sum(jnp.sum(err * err, axis=1, keepdims=True), axis=0, keepdims=True)

        @pl.when(n == 0)
        def _():
            loss_ref[...] = part

        @pl.when(n > 0)
        def _():
            loss_ref[...] += part

    return pl.pallas_call(
        kern, name=name, grid=(nb,),
        in_specs=[pl.BlockSpec((BLK, d), lambda n: (n, 0)),
                  pl.BlockSpec((BLK, d), lambda n: (jnp.maximum(n - 1, 0), 0))],
        out_specs=[pl.BlockSpec((1, 1), lambda n: (0, 0)), pl.BlockSpec((BLK, d), lambda n: (n, 0))],
        out_shape=[jax.ShapeDtypeStruct((1, 1), F32), jax.ShapeDtypeStruct((t, d), F32)],
        compiler_params=_cp("arbitrary"),
    )(h, target)


def _ew_tile(r, c):
    cap = max(16, (256 * 1024) // c)
    best = None
    for tr in range(16, min(r, cap) + 1, 16):
        if r % tr == 0:
            best = tr
    return best if best is not None else r


def adamw(parts, w, m, v, *, name):
    npart, r, c = parts.shape
    tr = _ew_tile(r, c)

    def kern(p_ref, w_ref, m_ref, v_ref, g_ref, d_ref, m2_ref, v2_ref):
        g = p_ref[0].astype(F32)
        for k in range(1, npart):
            g = g + p_ref[k].astype(F32)
        m2 = ADAM_B1 * m_ref[...] + (1.0 - ADAM_B1) * g
        v2 = ADAM_B2 * v_ref[...] + (1.0 - ADAM_B2) * (g * g)
        m_hat = m2 / (1.0 - ADAM_B1 ** ADAM_STEP)
        v_hat = v2 / (1.0 - ADAM_B2 ** ADAM_STEP)
        g_ref[...] = g
        d_ref[...] = -ADAM_LR * (m_hat / (jnp.sqrt(v_hat) + ADAM_EPS) + ADAM_WD * w_ref[...])
        m2_ref[...] = m2
        v2_ref[...] = v2

    row = pl.BlockSpec((tr, c), lambda i: (i, 0))
    sds = jax.ShapeDtypeStruct((r, c), F32)
    return pl.pallas_call(
        kern, name=name, grid=(r // tr,),
        in_specs=[pl.BlockSpec((npart, tr, c), lambda i: (0, i, 0)), row, row, row],
        out_specs=[row, row, row, row], out_shape=[sds, sds, sds, sds],
        compiler_params=_cp("parallel"),
    )(parts, w, m, v)


def pair_add(p, land, *, name):
    _, r, c = p.shape
    tr = _ew_tile(r, c)
    core = lax.axis_index("c").astype(jnp.int32).reshape(1)

    def kern(c_ref, p_ref, l_ref, o_ref):
        o_ref[...] = (p_ref[...] + l_ref[...]).astype(o_ref.dtype)

    return pl.pallas_call(
        kern, name=name,
        grid_spec=pltpu.PrefetchScalarGridSpec(
            num_scalar_prefetch=1, grid=(4, r // tr),
            in_specs=[pl.BlockSpec((1, tr, c), lambda k, i, c_ref: (2 * k + c_ref[0], i, 0)),
                      pl.BlockSpec((1, tr, c), lambda k, i, c_ref: (k, i, 0))],
            out_specs=pl.BlockSpec((1, tr, c), lambda k, i, c_ref: (k, i, 0))),
        out_shape=jax.ShapeDtypeStruct((4, r, c), BF16),
        compiler_params=_cp("parallel", "parallel"),
    )(core, p, land)


def _me():
    return lax.axis_index("x"), lax.axis_index("y"), lax.axis_index("c")


def all_gather(xs, *, name):
    n = len(xs)

    def body(*refs):
        x_refs, out_refs = refs[:n], refs[n:2 * n]
        send_sems, recv_sems, local_sems = refs[2 * n:]
        mx, my, mc = _me()
        me, sib = (mx, my, mc), (mx, my, 1 - mc)
        chips = [(1 - mx, my), (mx, 1 - my), (1 - mx, 1 - my)]

        def rows(i, px, py, pc):
            return out_refs[i].at[4 * px + 2 * py + pc]

        def copy(i, k, block, to, src=None):
            return pltpu.make_async_remote_copy(
                src_ref=rows(i, *block) if src is None else src, dst_ref=rows(i, *block),
                send_sem=send_sems.at[7 * i + k], recv_sem=recv_sems.at[7 * i + k],
                device_id=to, device_id_type=MESH)

        mine = [pltpu.make_async_copy(x_refs[i], rows(i, *me), local_sems.at[i]) for i in range(n)]
        first = []
        for i in range(n):
            mine[i].start()
            first.append(copy(i, 0, me, sib, src=x_refs[i]))
            first += [copy(i, 1 + j, me, (*chip, mc), src=x_refs[i]) for j, chip in enumerate(chips)]
        for cp in first:
            cp.start()
        passed = []
        for i in range(n):
            for j, chip in enumerate(chips):
                copy(i, 1 + j, (*chip, mc), me).wait_recv()
                passed.append(copy(i, 4 + j, (*chip, mc), sib))
                passed[-1].start()
        for i in range(n):
            copy(i, 0, sib, me).wait_recv()
            for j, chip in enumerate(chips):
                copy(i, 4 + j, (*chip, 1 - mc), me).wait_recv()
        for cp in first + passed:
            cp.wait_send()
        for cp in mine:
            cp.wait()

    return pl.pallas_call(
        body, name=name,
        out_shape=[jax.ShapeDtypeStruct((N_DEV,) + x.shape, x.dtype) for x in xs],
        in_specs=[ANY] * n, out_specs=[ANY] * n,
        scratch_shapes=[pltpu.SemaphoreType.DMA((7 * n,)), pltpu.SemaphoreType.DMA((7 * n,)),
                        pltpu.SemaphoreType.DMA((n,))],
    )(*xs)


def pair_exchange(ps, *, name):
    n = len(ps)

    def body(*refs):
        p_refs, out_refs = refs[:n], refs[n:2 * n]
        send_sems, recv_sems = refs[2 * n:]
        mx, my, mc = _me()
        cps = [pltpu.make_async_remote_copy(
            src_ref=p_refs[i].at[2 * k + (1 - mc)], dst_ref=out_refs[i].at[k],
            send_sem=send_sems.at[4 * i + k], recv_sem=recv_sems.at[4 * i + k],
            device_id=(mx, my, 1 - mc), device_id_type=MESH) for i in range(n) for k in range(4)]
        for cp in cps:
            cp.start()
        for cp in cps:
            cp.wait_recv()
        for cp in cps:
            cp.wait_send()

    return pl.pallas_call(
        body, name=name,
        out_shape=[jax.ShapeDtypeStruct((4,) + p.shape[1:], p.dtype) for p in ps],
        in_specs=[ANY] * n, out_specs=[ANY] * n,
        scratch_shapes=[pltpu.SemaphoreType.DMA((4 * n,)), pltpu.SemaphoreType.DMA((4 * n,))],
    )(*ps)


def chip_exchange(qs, *, name):
    n = len(qs)

    def body(*refs):
        q_refs, out_refs = refs[:n], refs[n:2 * n]
        send_sems, recv_sems, local_sems = refs[2 * n:]
        mx, my, mc = _me()
        mine = 2 * mx + my
        chips = [(1 - mx, my), (mx, 1 - my), (1 - mx, 1 - my)]
        local, sends, recvs = [], [], []
        for i in range(n):
            local.append(pltpu.make_async_copy(q_refs[i].at[mine], out_refs[i].at[mine], local_sems.at[i]))
            for k, (px, py) in enumerate(chips):
                sems = dict(send_sem=send_sems.at[3 * i + k], recv_sem=recv_sems.at[3 * i + k],
                            device_id=(px, py, mc), device_id_type=MESH)
                sends.append(pltpu.make_async_remote_copy(
                    src_ref=q_refs[i].at[2 * px + py], dst_ref=out_refs[i].at[mine], **sems))
                recvs.append(pltpu.make_async_remote_copy(
                    src_ref=q_refs[i].at[mine], dst_ref=out_refs[i].at[2 * px + py], **sems))
        for cp in local + sends:
            cp.start()
        for cp in recvs:
            cp.wait_recv()
        for cp in sends:
            cp.wait_send()
        for cp in local:
            cp.wait()

    return pl.pallas_call(
        body, name=name,
        out_shape=[jax.ShapeDtypeStruct(q.shape, q.dtype) for q in qs],
        in_specs=[ANY] * n, out_specs=[ANY] * n,
        scratch_shapes=[pltpu.SemaphoreType.DMA((3 * n,)), pltpu.SemaphoreType.DMA((3 * n,)),
                        pltpu.SemaphoreType.DMA((n,))],
    )(*qs)


class _Carry:
    def __init__(self, inputs, out_shapes, sems, start, finish):
        self.inputs, self.out_shapes, self.sems = list(inputs), list(out_shapes), list(sems)
        self.start, self.finish = start, finish


def _call(kern, args, *, name, grid, in_specs, out_specs, out_shape, scratch_shapes=(), sem, carry=None):
    in_specs, out_specs, out_shape = list(in_specs), list(out_specs), list(out_shape)
    scratch_shapes = list(scratch_shapes)
    if carry is None:
        return list(pl.pallas_call(
            kern, name=name, grid=grid, in_specs=in_specs, out_specs=out_specs, out_shape=out_shape,
            scratch_shapes=scratch_shapes, compiler_params=_cp(*sem))(*args))
    ni, no, ns = len(in_specs), len(out_specs), len(scratch_shapes)
    ci, co = len(carry.inputs), len(carry.out_shapes)

    def body(*refs):
        o0 = ni + ci
        s0 = o0 + no + co
        ids = [pl.program_id(d) for d in range(len(grid))]
        first = functools.reduce(jnp.logical_and, [i == 0 for i in ids])
        last = functools.reduce(jnp.logical_and, [i == g - 1 for i, g in zip(ids, grid)])
        cin, cout, sems = refs[ni:o0], refs[o0 + no:s0], refs[s0 + ns:]

        @pl.when(first)
        def _():
            carry.start(cin, cout, sems)

        kern(*refs[:ni], *refs[o0:o0 + no], *refs[s0:s0 + ns])

        @pl.when(last)
        def _():
            carry.finish(cin, cout, sems)

    return list(pl.pallas_call(
        body, name=name, grid=grid, in_specs=in_specs + [ANY] * ci, out_specs=out_specs + [ANY] * co,
        out_shape=out_shape + carry.out_shapes, scratch_shapes=scratch_shapes + carry.sems,
        compiler_params=_cp(*(["arbitrary"] * len(grid))))(*args, *carry.inputs))


def gather_carry(xs):
    n = len(xs)

    def copies(cin, cout, sems, with_recv=True):
        mx, my, mc = _me()
        me = 4 * mx + 2 * my + mc
        peers = [(mx, my, 1 - mc), (1 - mx, my, mc), (mx, 1 - my, mc), (1 - mx, 1 - my, mc)]
        local, send, recv = [], [], []
        for i in range(n):
            local.append(pltpu.make_async_copy(cin[i], cout[i].at[me], sems[2].at[i]))
            for k, peer in enumerate(peers):
                common = dict(send_sem=sems[0].at[4 * i + k], recv_sem=sems[1].at[4 * i + k],
                              device_id=peer, device_id_type=MESH)
                send.append(pltpu.make_async_remote_copy(src_ref=cin[i], dst_ref=cout[i].at[me], **common))
                if with_recv:
                    recv.append(pltpu.make_async_remote_copy(
                        src_ref=cin[i], dst_ref=cout[i].at[4 * peer[0] + 2 * peer[1] + peer[2]], **common))
        return local, send, recv

    def start(cin, cout, sems):
        local, send, _ = copies(cin, cout, sems, with_recv=False)
        for cp in local + send:
            cp.start()

    def finish(cin, cout, sems):
        local, send, recv = copies(cin, cout, sems)
        for cp in recv:
            cp.wait_recv()
        for cp in send:
            cp.wait_send()
        for cp in local:
            cp.wait()

    return _Carry(xs, [jax.ShapeDtypeStruct((N_DEV,) + x.shape, x.dtype) for x in xs],
                  [pltpu.SemaphoreType.DMA((4 * n,)), pltpu.SemaphoreType.DMA((4 * n,)),
                   pltpu.SemaphoreType.DMA((n,))], start, finish)


def gather_relay(outs, *, name):
    n = len(outs)

    def body(*refs):
        bufs = refs[n:2 * n]
        send_sems, recv_sems = refs[2 * n:]
        mx, my, mc = _me()
        chips = [(1 - mx, my), (mx, 1 - my), (1 - mx, 1 - my)]
        send, recv = [], []
        for i in range(n):
            for j, (px, py) in enumerate(chips):
                common = dict(send_sem=send_sems.at[3 * i + j], recv_sem=recv_sems.at[3 * i + j],
                              device_id=(mx, my, 1 - mc), device_id_type=MESH)
                mine = bufs[i].at[4 * px + 2 * py + mc]
                send.append(pltpu.make_async_remote_copy(src_ref=mine, dst_ref=mine, **common))
                recv.append(pltpu.make_async_remote_copy(
                    src_ref=mine, dst_ref=bufs[i].at[4 * px + 2 * py + (1 - mc)], **common))
        for cp in send:
            cp.start()
        for cp in recv:
            cp.wait_recv()
        for cp in send:
            cp.wait_send()

    return pl.pallas_call(
        body, name=name, out_shape=[jax.ShapeDtypeStruct(o.shape, o.dtype) for o in outs],
        in_specs=[ANY] * n, out_specs=[ANY] * n, input_output_aliases={i: i for i in range(n)},
        scratch_shapes=[pltpu.SemaphoreType.DMA((3 * n,)), pltpu.SemaphoreType.DMA((3 * n,))],
    )(*outs)


def pair_carry(ps):
    n = len(ps)

    def copies(cin, cout, sems):
        mx, my, mc = _me()
        return [pltpu.make_async_remote_copy(
            src_ref=cin[i].at[2 * k + (1 - mc)], dst_ref=cout[i].at[k],
            send_sem=sems[0].at[4 * i + k], recv_sem=sems[1].at[4 * i + k],
            device_id=(mx, my, 1 - mc), device_id_type=MESH) for i in range(n) for k in range(4)]

    def start(cin, cout, sems):
        for cp in copies(cin, cout, sems):
            cp.start()

    def finish(cin, cout, sems):
        cps = copies(cin, cout, sems)
        for cp in cps:
            cp.wait_recv()
        for cp in cps:
            cp.wait_send()

    return _Carry(ps, [jax.ShapeDtypeStruct((4,) + p.shape[1:], p.dtype) for p in ps],
                  [pltpu.SemaphoreType.DMA((4 * n,)), pltpu.SemaphoreType.DMA((4 * n,))], start, finish)


def chip_carry(qs):
    n = len(qs)

    def copies(cin, cout, sems, with_recv=True):
        mx, my, mc = _me()
        mine = 2 * mx + my
        chips = [(1 - mx, my), (mx, 1 - my), (1 - mx, 1 - my)]
        local, send, recv = [], [], []
        for i in range(n):
            local.append(pltpu.make_async_copy(cin[i].at[mine], cout[i].at[mine], sems[2].at[i]))
            for k, (px, py) in enumerate(chips):
                common = dict(send_sem=sems[0].at[3 * i + k], recv_sem=sems[1].at[3 * i + k],
                              device_id=(px, py, mc), device_id_type=MESH)
                send.append(pltpu.make_async_remote_copy(
                    src_ref=cin[i].at[2 * px + py], dst_ref=cout[i].at[mine], **common))
                if with_recv:
                    recv.append(pltpu.make_async_remote_copy(
                        src_ref=cin[i].at[mine], dst_ref=cout[i].at[2 * px + py], **common))
        return local, send, recv

    def start(cin, cout, sems):
        local, send, _ = copies(cin, cout, sems, with_recv=False)
        for cp in local + send:
            cp.start()

    def finish(cin, cout, sems):
        local, send, recv = copies(cin, cout, sems)
        for cp in recv:
            cp.wait_recv()
        for cp in send:
            cp.wait_send()
        for cp in local:
            cp.wait()

    return _Carry(qs, [jax.ShapeDtypeStruct(q.shape, q.dtype) for q in qs],
                  [pltpu.SemaphoreType.DMA((3 * n,)), pltpu.SemaphoreType.DMA((3 * n,)),
                   pltpu.SemaphoreType.DMA((n,))], start, finish)


WEIGHTS = [
    "meta_tokens", "l0_mix_pre_norm", "l0_mix_post_norm", "l0_w_in", "l0_lru_conv_w", "l0_lru_conv_b",
    "l0_lru_w_a", "l0_lru_b_a", "l0_lru_w_x", "l0_lru_b_x", "l0_lru_lambda", "l0_attn_sinks", "l0_w_out",
    "l0_ffn_pre_norm", "l0_ffn_post_norm", "l0_ffn_w_up", "l0_ffn_conv_w", "l0_ffn_conv_b", "l0_ffn_w_down",
    "l1_mix_pre_norm", "l1_mix_post_norm", "l1_w_in", "l1_ssm_conv_w", "l1_ssm_conv_b", "l1_dt_bias",
    "l1_a_log", "l1_d_skip", "l1_gate_norm", "l1_w_out", "l1_ffn_pre_norm", "l1_ffn_post_norm",
    "l1_ffn_w_up", "l1_ffn_conv_w", "l1_ffn_conv_b", "l1_ffn_w_down",
]
INPUTS = (["x"] + WEIGHTS + ["loss_target"] + ["m_" + n for n in WEIGHTS] + ["v_" + n for n in WEIGHTS])

MATS = {"l0_w_in": ("col", (1024, 3328)), "l0_w_out": ("row", (2048, 1024)),
        "l0_ffn_w_up": ("col", (1024, 5632)), "l0_ffn_w_down": ("row", (2816, 1024)),
        "l1_w_in": ("col", (1024, 6176)), "l1_w_out": ("row", (2048, 1024)),
        "l1_ffn_w_up": ("col", (1024, 5632)), "l1_ffn_w_down": ("row", (2816, 1024))}
SMALL_SHARDED = {"meta_tokens": ("col", (16, 1024)), "l0_lru_conv_w": ("col", (4, 1024)),
                 "l0_ffn_conv_w": ("col", (3, 5632)), "l1_ssm_conv_w": ("col", (4, 4096)),
                 "l1_ffn_conv_w": ("col", (3, 5632))}
SHARDED = {**MATS, **SMALL_SHARDED}
REPLICATED = [n for n in WEIGHTS if n not in SHARDED]
PACK_W = 1024
SMALL_W = 128


def _shard_shape(name):
    kind, (r, c) = SHARDED[name]
    return (r, c // N_DEV) if kind == "col" else (r // N_DEV, c)


def _rows_of(numel, width):
    return -(-numel // width)


def _to_rows(a, width):
    flat = a.reshape(-1)
    rows = _rows_of(flat.shape[0], width)
    return jnp.pad(flat, (0, rows * width - flat.shape[0])).reshape(rows, width)


def _pack(arrs, width, total_rows):
    slab = jnp.concatenate([_to_rows(a, width) for a in arrs], axis=0)
    return jnp.pad(slab, ((0, total_rows - slab.shape[0]), (0, 0)))


def _unpack(slab, shapes, width):
    out, off = [], 0
    for shp in shapes:
        numel = math.prod(shp)
        rows = _rows_of(numel, width)
        out.append(slab[off:off + rows].reshape(-1)[:numel].reshape(shp))
        off += rows
    return out


def _round_up(n, m):
    return -(-n // m) * m


def _by_dest(name, g):
    kind, (r, c) = SHARDED[name]
    if kind == "col":
        return g.reshape(r, N_DEV, c // N_DEV).transpose(1, 0, 2)
    return g.reshape(N_DEV, r // N_DEV, c)


def _from_shards(name, blocks):
    kind, (r, c) = SHARDED[name]
    return blocks.transpose(1, 0, 2).reshape(r, c) if kind == "col" else blocks.reshape(r, c)


L1_IN_PAD = 6272


def _ffn_fwd(h, a, w, pfx):
    u, ut = rmsnorm_fwd(h, a[pfx + "ffn_pre_norm"], out_dtype=MXU, name=pfx + "ffn_pre", with_t=True)
    up = matmul(u, w[pfx + "ffn_w_up"], name=pfx + "ffn_up")
    act, act_t = dwconv_fwd(up, a[pfx + "ffn_conv_w"], a[pfx + "ffn_conv_b"], mode="geglu", x_off=0,
                            c_out=D_FF, cblk=256, out_dtype=MXU, name=pfx + "ffn_act", with_t=True)
    down = matmul(act, w[pfx + "ffn_w_down"], name=pfx + "ffn_down")
    out = rmsnorm_fwd(down, a[pfx + "ffn_post_norm"], res=h, out_dtype=F32, name=pfx + "ffn_post")
    return out, (h, ut, up, act_t, down)


def _dx_and_pair_stage(names, g, a_list, b, *, name):
    parts = [_by_dest(n, g[n]) for n in names]
    out, from_sibling = matmul_cat(a_list, b, trans_b=True, name=name, carry=pair_carry(parts))
    return out, [pair_add(p, l, name="rs_pair_add_" + n) for n, p, l in zip(names, parts, from_sibling)]


def _ffn_bwd(dh, saved, a, w, pfx, g, carry=None):
    h, ut, up, act_t, down = saved
    dd, g[pfx + "ffn_post_norm"] = rmsnorm_bwd(down, a[pfx + "ffn_post_norm"], dh, out_dtype=MXU,
                                               name=pfx + "ffn_post_bwd")
    dact = matmul(dd, w[pfx + "ffn_w_down"], trans_b=True, name=pfx + "ffn_down_dx")
    g[pfx + "ffn_w_down"] = matmul(act_t, dd, name=pfx + "ffn_down_dw")
    dups, g[pfx + "ffn_conv_w"], g[pfx + "ffn_conv_b"], carried = dwconv_bwd(
        up, a[pfx + "ffn_conv_w"], a[pfx + "ffn_conv_b"], dact, mode="geglu", x_off=0, c_out=D_FF,
        cblk=256, name=pfx + "ffn_act_bwd", carry=carry)
    g[pfx + "ffn_w_up"] = jnp.concatenate(
        [matmul(ut, d, name=pfx + "ffn_up_dw%d" % i) for i, d in enumerate(dups)], axis=1)
    du, q = _dx_and_pair_stage([pfx + "ffn_w_down", pfx + "ffn_w_up"], g, dups, w[pfx + "ffn_w_up"],
                               name=pfx + "ffn_up_dx")
    dh_in, g[pfx + "ffn_pre_norm"] = rmsnorm_bwd(h, a[pfx + "ffn_pre_norm"], du, res=dh, out_dtype=F32,
                                                 name=pfx + "ffn_pre_bwd")
    return dh_in, carried, q


GATHER_EARLY = ["l0_w_out", "l0_ffn_w_up", "l0_ffn_w_down"]
GATHER_LATE = ["l1_w_in", "l1_w_out", "l1_ffn_w_up", "l1_ffn_w_down"]
RS_L1_FFN = ["l1_ffn_w_down", "l1_ffn_w_up"]
RS_L1_MIX = ["l1_w_out", "l1_w_in"]
RS_L0_FFN = ["l0_ffn_w_down", "l0_ffn_w_up"]
RS_LAST = ["l0_w_out", "l0_w_in", "l0_lru_conv_w", "l0_ffn_conv_w", "l1_ssm_conv_w", "l1_ffn_conv_w"]


def _local_step(a, w, shards):
    x = a["x"][0]
    seq = x.shape[0]
    h0 = jnp.concatenate([jnp.zeros((PAD, D_MODEL), F32), a["meta_tokens"], x], axis=0)
    g, landed = {}, {}

    u0, u0t = rmsnorm_fwd(h0, a["l0_mix_pre_norm"], out_dtype=MXU, name="l0_mix_pre", with_t=True)
    proj0 = matmul(u0, w["l0_w_in"], name="l0_in")
    lru = (a["l0_lru_conv_w"], a["l0_lru_conv_b"], a["l0_lru_w_a"], a["l0_lru_b_a"], a["l0_lru_w_x"],
           a["l0_lru_b_x"], a["l0_lru_lambda"])
    ya, ya_t, hl, *early = lru_fwd(proj0, *lru, gate_off=0, xr_off=1024, name="l0_lru",
                                   carry=gather_carry([shards[n] for n in GATHER_EARLY]))
    yb, *late = attn_fwd(proj0, a["l0_attn_sinks"], q_off=2048, k_off=3072, v_off=3200, name="l0_attn",
                         carry=gather_carry([shards[n] for n in GATHER_LATE]))
    relayed = gather_relay(early + late, name="gather_relay")
    w = dict(w, **{n: _from_shards(n, blocks) for n, blocks in zip(GATHER_EARLY + GATHER_LATE, relayed)})
    w["l1_w_in"] = jnp.pad(w["l1_w_in"], ((0, 0), (0, L1_IN_PAD - w["l1_w_in"].shape[1])))
    o0 = matmul_cat([ya, yb], w["l0_w_out"], name="l0_out")
    h1 = rmsnorm_fwd(o0, a["l0_mix_post_norm"], res=h0, out_dtype=F32, name="l0_mix_post")
    h2, ffn0 = _ffn_fwd(h1, a, w, "l0_")

    u2, u2t = rmsnorm_fwd(h2, a["l1_mix_pre_norm"], out_dtype=MXU, name="l1_mix_pre", with_t=True)
    proj1 = matmul(u2, w["l1_w_in"], name="l1_in")
    xbc = dwconv_fwd(proj1, a["l1_ssm_conv_w"], a["l1_ssm_conv_b"], mode="silu", x_off=D_SSM,
                     c_out=2 * D_SSM, cblk=512, out_dtype=F32, name="l1_ssm_conv")
    ssd = (a["l1_dt_bias"], a["l1_a_log"], a["l1_d_skip"], a["l1_gate_norm"])
    yn, yn_t, st = ssd_fwd(xbc, proj1, *ssd, z_off=0, dt_off=3 * D_SSM, name="l1_ssd")
    o1 = matmul(yn, w["l1_w_out"], name="l1_out")
    h3 = rmsnorm_fwd(o1, a["l1_mix_post_norm"], res=h2, out_dtype=F32, name="l1_mix_post")
    h4, ffn1 = _ffn_fwd(h3, a, w, "l1_")

    loss, dh4 = loss_fwd_bwd(h4, a["loss_target"][0], name="loss")

    dh3, _, q_l1_ffn = _ffn_bwd(dh4, ffn1, a, w, "l1_", g)
    do1, g["l1_mix_post_norm"] = rmsnorm_bwd(o1, a["l1_mix_post_norm"], dh3, out_dtype=MXU,
                                             name="l1_mix_post_bwd")
    dyn = matmul(do1, w["l1_w_out"], trans_b=True, name="l1_out_dx")
    g["l1_w_out"] = matmul(yn_t, do1, name="l1_out_dw")
    (dxbc, dz, draw, g["l1_gate_norm"], g["l1_d_skip"], g["l1_dt_bias"], g["l1_a_log"], *got) = ssd_bwd(
        xbc, proj1, st, dyn, *ssd, z_off=0, dt_off=3 * D_SSM, name="l1_ssd_bwd",
        carry=chip_carry(q_l1_ffn))
    landed.update(zip(RS_L1_FFN, got))
    (dxin,), g["l1_ssm_conv_w"], g["l1_ssm_conv_b"], _ = dwconv_bwd(
        proj1, a["l1_ssm_conv_w"], a["l1_ssm_conv_b"], dxbc, mode="silu", x_off=D_SSM,
        c_out=2 * D_SSM, cblk=512, name="l1_ssm_conv_bwd")
    g["l1_w_in"] = jnp.concatenate(
        [matmul(u2t, dz, name="l1_in_dw_z"), matmul(u2t, dxin, name="l1_in_dw_x"),
         matmul(u2t, draw, name="l1_in_dw_dt")[:, :SSD_HEADS]], axis=1)
    du2, q_l1_mix = _dx_and_pair_stage(RS_L1_MIX, g, [dz, dxin, draw], w["l1_w_in"], name="l1_in_dx")
    dh2, g["l1_mix_pre_norm"] = rmsnorm_bwd(h2, a["l1_mix_pre_norm"], du2, res=dh3, out_dtype=F32,
                                            name="l1_mix_pre_bwd")

    dh1, got, q_l0_ffn = _ffn_bwd(dh2, ffn0, a, w, "l0_", g, carry=chip_carry(q_l1_mix))
    landed.update(zip(RS_L1_MIX, got))
    do0, g["l0_mix_post_norm"] = rmsnorm_bwd(o0, a["l0_mix_post_norm"], dh1, out_dtype=MXU,
                                             name="l0_mix_post_bwd")
    dy = matmul(do0, w["l0_w_out"], trans_b=True, name="l0_out_dx")
    g["l0_w_out"] = jnp.concatenate([matmul(ya_t, do0, name="l0_out_dw_a"),
                                     matmul(yb.T, do0, name="l0_out_dw_b")], axis=0)
    (dgate, dxr, g["l0_lru_conv_w"], dcb, g["l0_lru_w_a"], dba, g["l0_lru_w_x"], dbx, dlam) = lru_bwd(
        proj0, hl, dy, *lru, gate_off=0, xr_off=1024, dy_off=0, name="l0_lru_bwd")
    g["l0_lru_conv_b"], g["l0_lru_b_a"], g["l0_lru_b_x"], g["l0_lru_lambda"] = dcb[0], dba[0], dbx[0], dlam[0]
    dq, dk, dv, g["l0_attn_sinks"], *got = attn_bwd(
        proj0, a["l0_attn_sinks"], dy, q_off=2048, k_off=3072, v_off=3200, dy_off=1024, name="l0_attn_bwd",
        carry=chip_carry(q_l0_ffn))
    landed.update(zip(RS_L0_FFN, got))
    dproj0 = [dgate, dxr, dq, dk, dv]
    g["l0_w_in"] = jnp.concatenate(
        [matmul(u0t, d, name="l0_in_dw%d" % i) for i, d in enumerate(dproj0)], axis=1)
    du0, q_last = _dx_and_pair_stage(RS_LAST, g, dproj0, w["l0_w_in"], name="l0_in_dx")
    dh0, g["l0_mix_pre_norm"] = rmsnorm_bwd(h0, a["l0_mix_pre_norm"], du0, res=dh1, out_dtype=F32,
                                            name="l0_mix_pre_bwd")
    g["meta_tokens"] = dh0[PAD:BLK]
    meta = _by_dest("meta_tokens", g["meta_tokens"])
    q_meta = pair_add(meta, pair_exchange([meta], name="rs_pair_meta")[0], name="rs_pair_add_meta_tokens")
    landed.update(zip(RS_LAST + ["meta_tokens"], chip_exchange(q_last + [q_meta], name="rs_chip")))
    for n in REPLICATED:
        g[n] = g[n].reshape(a[n].shape)
    return loss[0, 0], dh0[BLK:].reshape(1, seq, D_MODEL), g, landed


def kernel(*args):
    a = dict(zip(INPUTS, args))
    first = list(SMALL_SHARDED) + ["l0_w_in"]
    got = all_gather([a[n].astype(MXU) if n in MATS else a[n] for n in first], name="gather_first")
    full = {n: _from_shards(n, blocks) for n, blocks in zip(first, got)}
    shards = {n: a[n].astype(MXU) for n in GATHER_EARLY + GATHER_LATE}
    loss_part, grad_x, g, landed = _local_step(
        {**a, **{n: full[n] for n in SMALL_SHARDED}}, {"l0_w_in": full["l0_w_in"]}, shards)
    loss = lax.psum(loss_part, ("x", "y", "c"))

    sh_out = {n: adamw(landed[n], a[n], a["m_" + n], a["v_" + n], name="adamw_" + n) for n in SHARDED}

    rp_shapes = [a[n].shape for n in REPLICATED]
    rrows = _round_up(sum(_rows_of(math.prod(s), SMALL_W) for s in rp_shapes), 128)
    gathered = all_gather([_pack([g[n] for n in REPLICATED], SMALL_W, rrows)], name="gather_small_grads")[0]
    rp_out = adamw(gathered, *[_pack([a[p + n] for n in REPLICATED], SMALL_W, rrows) for p in ("", "m_", "v_")],
                   name="adamw_replicated")
    rp_out = [dict(zip(REPLICATED, _unpack(s, rp_shapes, SMALL_W))) for s in rp_out]

    outs = [loss, grad_x]
    for k in range(4):
        outs += [sh_out[n][k] if n in SHARDED else rp_out[k][n] for n in WEIGHTS]
    return tuple(outs)
```

```python
import functools
import math

import jax
import jax.numpy as jnp
import numpy as np
from jax import lax
from jax.experimental import pallas as pl
from jax.experimental.pallas import tpu as pltpu

F32 = jnp.float32
BF16 = jnp.bfloat16
MXU = jnp.bfloat16

D_MODEL = 1024
N_META = 16
BLK = 128
PAD = BLK - N_META
D_RNN = 1024
LRU_C = 8.0
N_Q_HEADS = 16
HEAD_DIM = 64
D_SSM = 2048
SSD_HEADS = 32
SSD_GROUPS = 8
D_FF = 2816
EPS = 1e-6
NEG = -1e30
N_DEV = 8

ADAM_LR = 0.001
ADAM_B1 = 0.9
ADAM_B2 = 0.999
ADAM_EPS = 1e-08
ADAM_WD = 0.01
ADAM_STEP = 10

VMEM_LIMIT = 56 * 1024 * 1024
MESH = pl.DeviceIdType.MESH
ANY = pl.BlockSpec(memory_space=pl.ANY)


def _cp(*sem):
    return pltpu.CompilerParams(dimension_semantics=sem, vmem_limit_bytes=VMEM_LIMIT)


def _pick(n, cands):
    for c in cands:
        if n % c == 0:
            return c
    return n


def _dot(a, b):
    return jnp.dot(a.astype(MXU), b.astype(MXU), preferred_element_type=F32)


def _dot_nt(a, b):
    return lax.dot_general(a.astype(MXU), b.astype(MXU), (((1,), (1,)), ((), ())),
                           preferred_element_type=F32)


def _dot_tn(a, b):
    return jnp.dot(a.T.astype(MXU), b.astype(MXU), preferred_element_type=F32)


def _sigmoid(x):
    return 1.0 / (1.0 + jnp.exp(-x))


def _log1p(x):
    u = 1.0 + x
    return jnp.where(u == 1.0, x, jnp.log(u) * (x / jnp.where(u == 1.0, 1.0, u - 1.0)))


def _expm1(x):
    u = jnp.exp(x)
    um1 = u - 1.0
    lg = jnp.log(jnp.where(u > 0.0, u, 1.0))
    safe = (um1 != 0.0) & (um1 != -1.0)
    return jnp.where(um1 == 0.0, x, jnp.where(um1 == -1.0, -1.0,
                                               um1 * (x / jnp.where(safe, lg, 1.0))))


def _softplus(x):
    return jnp.maximum(x, 0.0) + _log1p(jnp.exp(-jnp.abs(x)))


_GC = math.sqrt(2.0 / math.pi)


def _gelu(x):
    t = jnp.tanh(_GC * (x + 0.044715 * x * x * x))
    return 0.5 * x * (1.0 + t)


def _gelu_grad(x):
    t = jnp.tanh(_GC * (x + 0.044715 * x * x * x))
    return 0.5 * (1.0 + t) + 0.5 * x * (1.0 - t * t) * (_GC * (1.0 + 3.0 * 0.044715 * x * x))


def _silu(x):
    return x * _sigmoid(x)


def _silu_grad(x):
    s = _sigmoid(x)
    return s * (1.0 + x * (1.0 - s))


def _rows(shape):
    return lax.broadcasted_iota(jnp.int32, shape, 0)


def _lanes(shape):
    return lax.broadcasted_iota(jnp.int32, shape, 1)


def _shift_down(x, tail, d):
    if d == 0:
        return x
    n = x.shape[0]
    xr = pltpu.roll(x, d, 0)
    tr = pltpu.roll(tail, d, 0)
    first = jnp.where(_rows(tr.shape) < d, tr, xr[0:8])
    return jnp.concatenate([first, xr[8:n]], axis=0)


def _shift_up(x, head, d):
    if d == 0:
        return x
    n = x.shape[0]
    xr = pltpu.roll(x, n - d, 0)
    hr = pltpu.roll(head, 8 - d, 0)
    last = jnp.where(_rows(hr.shape) >= 8 - d, hr, xr[n - 8:n])
    return jnp.concatenate([xr[0:n - 8], last], axis=0)


def _keep(x, valid, s):
    return jnp.where(valid, x, 0.0) if s == 0 else x


def _row_at(x, i):
    return jnp.sum(jnp.where(_rows(x.shape) == i, x, 0.0), axis=0, keepdims=True)


def _scan_fwd(a, u):
    n = a.shape[0]
    ri = _rows(a.shape)
    d = 1
    while d < n:
        m = ri >= d
        us = jnp.where(m, pltpu.roll(u, d, 0), 0.0)
        as_ = jnp.where(m, pltpu.roll(a, d, 0), 1.0)
        u = u + a * us
        a = a * as_
        d *= 2
    return a, u


def _scan_rev(c, u):
    n = c.shape[0]
    ri = _rows(c.shape)
    d = 1
    while d < n:
        m = ri < n - d
        us = jnp.where(m, pltpu.roll(u, n - d, 0), 0.0)
        cs = jnp.where(m, pltpu.roll(c, n - d, 0), 1.0)
        u = u + c * us
        c = c * cs
        d *= 2
    return c, u


def _cumsum_fwd(x):
    n = x.shape[0]
    ri = _rows(x.shape)
    d = 1
    while d < n:
        x = x + jnp.where(ri >= d, pltpu.roll(x, d, 0), 0.0)
        d *= 2
    return x


def _cumsum_rev(x):
    n = x.shape[0]
    ri = _rows(x.shape)
    d = 1
    while d < n:
        x = x + jnp.where(ri < n - d, pltpu.roll(x, n - d, 0), 0.0)
        d *= 2
    return x


MATMUL_VMEM = 40 * 1024 * 1024


def _matmul_tiles(m, n, k, tk, out_bytes):
    best = None
    for tm in (1664, 1408, 1040, 1024, 832, 640, 512, 384, 256, 128):
        if m % tm:
            continue
        for tn in (2048, 1664, 1408, 1024, 896, 640, 512, 384, 256, 128):
            if n % tn:
                continue
            vmem = 2 * (tm * tk * 2 + tk * tn * 2 + tm * tn * out_bytes) + (tm * tn * 4 if k > tk else 0)
            if vmem > MATMUL_VMEM:
                continue
            traffic = (n // tn) * m * k * 2 + (m // tm) * k * n * 2
            if best is None or traffic < best[0]:
                best = (traffic, tm, tn)
    return (best[1], best[2]) if best else (m, n)


def matmul(a, b, *, trans_b=False, out_dtype=F32, name):
    m, k = a.shape
    n = b.shape[0] if trans_b else b.shape[1]
    tk = k if k <= 2048 else _pick(k, (1664, 1408, 1024, 896, 512, 256, 128))
    nk = k // tk
    tm, tn = _matmul_tiles(m, n, k, tk, jnp.dtype(out_dtype).itemsize)

    def product(a_ref, b_ref):
        return _dot_nt(a_ref[...], b_ref[...]) if trans_b else _dot(a_ref[...], b_ref[...])

    def kern_once(a_ref, b_ref, o_ref):
        o_ref[...] = product(a_ref, b_ref).astype(o_ref.dtype)

    def kern_acc(a_ref, b_ref, o_ref, acc_ref):
        kk = pl.program_id(2)

        @pl.when(kk == 0)
        def _():
            acc_ref[...] = product(a_ref, b_ref)

        @pl.when(kk > 0)
        def _():
            acc_ref[...] += product(a_ref, b_ref)

        @pl.when(kk == nk - 1)
        def _():
            o_ref[...] = acc_ref[...].astype(o_ref.dtype)

    b_spec = (pl.BlockSpec((tn, tk), lambda i, j, kk: (j, kk)) if trans_b
              else pl.BlockSpec((tk, tn), lambda i, j, kk: (kk, j)))
    return pl.pallas_call(
        kern_once if nk == 1 else kern_acc, name=name,
        grid=(m // tm, n // tn, nk),
        in_specs=[pl.BlockSpec((tm, tk), lambda i, j, kk: (i, kk)), b_spec],
        out_specs=pl.BlockSpec((tm, tn), lambda i, j, kk: (i, j)),
        out_shape=jax.ShapeDtypeStruct((m, n), out_dtype),
        scratch_shapes=[] if nk == 1 else [pltpu.VMEM((tm, tn), F32)],
        compiler_params=_cp("parallel", "parallel", "arbitrary"),
    )(a, b)


def matmul_cat(a_list, b, *, trans_b=False, out_dtype=F32, name, carry=None):
    m = a_list[0].shape[0]
    ks = [x.shape[1] for x in a_list]
    ktot = sum(ks)
    n = b.shape[0] if trans_b else b.shape[1]
    tn = _pick(n, (512, 256, 128))
    tm = next((c for c in (1664, 1040, 832, 640, 512, 384, 256, 128)
               if m % c == 0 and c * ktot * 2 <= 8 * 1024 * 1024), m)
    na = len(a_list)

    def kern(*refs):
        b_ref, o_ref = refs[na], refs[na + 1]
        acc, off = None, 0
        for a_ref, k in zip(refs[:na], ks):
            if trans_b:
                part = _dot_nt(a_ref[...], b_ref[:, off:off + k])
            else:
                part = _dot(a_ref[...], b_ref[off:off + k, :])
            acc = part if acc is None else acc + part
            off += k
        o_ref[...] = acc.astype(o_ref.dtype)

    b_spec = (pl.BlockSpec((tn, ktot), lambda i, j: (j, 0)) if trans_b
              else pl.BlockSpec((ktot, tn), lambda i, j: (0, j)))
    res = _call(
        kern, (*a_list, b), name=name, grid=(m // tm, n // tn),
        in_specs=[pl.BlockSpec((tm, k), lambda i, j: (i, 0)) for k in ks] + [b_spec],
        out_specs=[pl.BlockSpec((tm, tn), lambda i, j: (i, j))],
        out_shape=[jax.ShapeDtypeStruct((m, n), out_dtype)],
        sem=("parallel", "parallel"), carry=carry)
    return res[0] if carry is None else (res[0], res[1:])


def _row_tile(t):
    return _pick(t, (832, 640, 512, 384, 256, 128))


def rmsnorm_fwd(x, w, res=None, *, out_dtype, name, with_t=False):
    t, d = x.shape
    tr = _conv_tile(t) if with_t else _row_tile(t)

    def kern(*refs):
        x_ref, w_ref = refs[0], refs[1]
        o_ref = refs[-2] if with_t else refs[-1]
        xv = x_ref[...]
        r = lax.rsqrt(jnp.mean(xv * xv, axis=-1, keepdims=True) + EPS)
        y = xv * r * w_ref[...]
        if res is not None:
            y = refs[2][...] + y
        o_ref[...] = y.astype(o_ref.dtype)
        if with_t:
            refs[-1][...] = y.T.astype(o_ref.dtype)

    row = pl.BlockSpec((tr, d), lambda i: (i, 0))
    vec = pl.BlockSpec((1, d), lambda i: (0, 0))
    ins = [x, w.reshape(1, d)] + ([] if res is None else [res])
    specs = [row, vec] + ([] if res is None else [row])
    out_specs, out_shape = row, jax.ShapeDtypeStruct((t, d), out_dtype)
    if with_t:
        out_specs = [row, pl.BlockSpec((d, tr), lambda i: (0, i))]
        out_shape = [out_shape, jax.ShapeDtypeStruct((d, t), out_dtype)]
    return pl.pallas_call(
        kern, name=name, grid=(t // tr,), in_specs=specs, out_specs=out_specs, out_shape=out_shape,
        compiler_params=_cp("parallel"),
    )(*ins)


def rmsnorm_bwd(x, w, dy, res=None, *, out_dtype, name):
    t, d = x.shape
    tr = _row_tile(t)

    def kern(*refs):
        if res is None:
            x_ref, w_ref, dy_ref, dx_ref, dw_ref = refs
        else:
            x_ref, w_ref, dy_ref, r_ref, dx_ref, dw_ref = refs
        i = pl.program_id(0)
        xv = x_ref[...]
        dyv = dy_ref[...].astype(F32)
        r = lax.rsqrt(jnp.mean(xv * xv, axis=-1, keepdims=True) + EPS)
        xh = xv * r
        g = dyv * w_ref[...]
        dx = r * (g - xh * jnp.mean(g * xh, axis=-1, keepdims=True))
        if res is not None:
            dx = r_ref[...] + dx
        dx_ref[...] = dx.astype(dx_ref.dtype)
        part = jnp.sum(dyv * xh, axis=0, keepdims=True)

        @pl.when(i == 0)
        def _():
            dw_ref[...] = part

        @pl.when(i > 0)
        def _():
            dw_ref[...] += part

    row = pl.BlockSpec((tr, d), lambda i: (i, 0))
    vec = pl.BlockSpec((1, d), lambda i: (0, 0))
    ins = [x, w.reshape(1, d), dy] + ([] if res is None else [res])
    specs = [row, vec, row] + ([] if res is None else [row])
    return pl.pallas_call(
        kern, name=name, grid=(t // tr,), in_specs=specs, out_specs=[row, vec],
        out_shape=[jax.ShapeDtypeStruct((t, d), out_dtype), jax.ShapeDtypeStruct((1, d), F32)],
        compiler_params=_cp("arbitrary"),
    )(*ins)


def _conv_tile(t):
    return _pick(t, (640, 384, 256, 128))


def _conv_apply(x, tail, cw, cb, ksz):
    y = cb
    for k in range(ksz):
        y = y + cw[k:k + 1, :] * _shift_down(x, tail, ksz - 1 - k)
    return y


def dwconv_fwd(x, cw, cb, *, mode, x_off, c_out, cblk, out_dtype, name, with_t=False):
    t = x.shape[0]
    ksz = cw.shape[0]
    tb = _conv_tile(t)
    nb, ncb, t8 = t // tb, c_out // cblk, tb // 8
    xo = x_off // cblk
    nin = 2 if mode == "geglu" else 1

    def kern(*refs):
        o_ref = refs[-2] if with_t else refs[-1]
        n = pl.program_id(1)
        for c in range(cblk // BLK):
            ls = slice(c * BLK, (c + 1) * BLK)
            for s in range(tb // BLK):
                rs = slice(s * BLK, (s + 1) * BLK)
                valid = (n * tb + s * BLK + _rows((BLK, BLK))) >= PAD
                hs = []
                for q in range(nin):
                    x_ref, t_ref, w_ref, b_ref = refs[4 * q:4 * q + 4]
                    tail = (jnp.where(n > 0, t_ref[:, ls], 0.0) if s == 0
                            else x_ref[s * BLK - 8:s * BLK, ls])
                    hs.append(_conv_apply(x_ref[rs, ls], tail, w_ref[:, ls], b_ref[:, ls], ksz))
                y = _gelu(hs[0]) * hs[1] if mode == "geglu" else _silu(hs[0])
                y = _keep(y, valid, s)
                o_ref[rs, ls] = y.astype(o_ref.dtype)
                if with_t:
                    refs[-1][ls, rs] = y.T.astype(o_ref.dtype)

    ins, specs = [], []
    for q in range(nin):
        co = xo + q * ncb
        wo = q * ncb
        ins += [x, x, cw, cb.reshape(1, -1)]
        specs += [
            pl.BlockSpec((tb, cblk), lambda j, n, co=co: (n, co + j)),
            pl.BlockSpec((8, cblk), lambda j, n, co=co: (jnp.maximum(n * t8 - 1, 0), co + j)),
            pl.BlockSpec((ksz, cblk), lambda j, n, wo=wo: (0, wo + j)),
            pl.BlockSpec((1, cblk), lambda j, n, wo=wo: (0, wo + j)),
        ]
    out_specs = pl.BlockSpec((tb, cblk), lambda j, n: (n, j))
    out_shape = jax.ShapeDtypeStruct((t, c_out), out_dtype)
    if with_t:
        out_specs = [out_specs, pl.BlockSpec((cblk, tb), lambda j, n: (j, n))]
        out_shape = [out_shape, jax.ShapeDtypeStruct((c_out, t), out_dtype)]
    return pl.pallas_call(
        kern, name=name, grid=(ncb, nb), in_specs=specs, out_specs=out_specs, out_shape=out_shape,
        compiler_params=_cp("parallel", "parallel"),
    )(*ins)


def dwconv_bwd(x, cw, cb, dy, *, mode, x_off, c_out, cblk, name, carry=None):
    t = x.shape[0]
    ksz = cw.shape[0]
    tb = _conv_tile(t)
    nb, ncb, t8 = t // tb, c_out // cblk, tb // 8
    xo = x_off // cblk
    nin = 2 if mode == "geglu" else 1
    ctot = nin * c_out

    def kern(*refs):
        dy_ref = refs[4 * nin]
        outs = refs[4 * nin + 1:4 * nin + 1 + 3 * nin]
        heads = refs[4 * nin + 1 + 3 * nin:]
        n = pl.program_id(1)
        blk = nb - 1 - n

        @pl.when(n == 0)
        def _():
            for q in range(nin):
                heads[q][...] = jnp.zeros_like(heads[q])
                outs[3 * q + 1][...] = jnp.zeros_like(outs[3 * q + 1])
                outs[3 * q + 2][...] = jnp.zeros_like(outs[3 * q + 2])

        for c in range(cblk // BLK):
            ls = slice(c * BLK, (c + 1) * BLK)
            head = [heads[q][:, ls] for q in range(nin)]
            dwa = [[None] * ksz for _ in range(nin)]
            dba = [None] * nin
            for s in reversed(range(tb // BLK)):
                rs = slice(s * BLK, (s + 1) * BLK)
                valid = (blk * tb + s * BLK + _rows((BLK, BLK))) >= PAD
                xs, tails, hs = [], [], []
                for q in range(nin):
                    x_ref, t_ref, w_ref, b_ref = refs[4 * q:4 * q + 4]
                    tail = (jnp.where(blk > 0, t_ref[:, ls], 0.0) if s == 0
                            else x_ref[s * BLK - 8:s * BLK, ls])
                    xs.append(x_ref[rs, ls])
                    tails.append(tail)
                    hs.append(_conv_apply(xs[q], tail, w_ref[:, ls], b_ref[:, ls], ksz))
                dyv = dy_ref[rs, ls].astype(F32)
                if mode == "geglu":
                    dhs = [dyv * hs[1] * _gelu_grad(hs[0]), dyv * _gelu(hs[0])]
                else:
                    dhs = [dyv * _silu_grad(hs[0])]
                for q in range(nin):
                    w_ref = refs[4 * q + 2]
                    dh = _keep(dhs[q], valid, s)
                    dx = jnp.zeros_like(dh)
                    for k in range(ksz):
                        sh = ksz - 1 - k
                        dx = dx + w_ref[k:k + 1, ls] * _shift_up(dh, head[q], sh)
                        part = jnp.sum(dh * _shift_down(xs[q], tails[q], sh), axis=0, keepdims=True)
                        dwa[q][k] = part if dwa[q][k] is None else dwa[q][k] + part
                    outs[3 * q][rs, ls] = _keep(dx, valid, s).astype(outs[3 * q].dtype)
                    part = jnp.sum(dh, axis=0, keepdims=True)
                    dba[q] = part if dba[q] is None else dba[q] + part
                    head[q] = dh[0:8]
            for q in range(nin):
                outs[3 * q + 1][:, ls] += jnp.concatenate(dwa[q], axis=0)
                outs[3 * q + 2][:, ls] += dba[q]
                heads[q][:, ls] = head[q]

    ins, specs, out_specs, out_shape, scratch = [], [], [], [], []
    for q in range(nin):
        co = xo + q * ncb
        wo = q * ncb
        ins += [x, x, cw, cb.reshape(1, -1)]
        specs += [
            pl.BlockSpec((tb, cblk), lambda j, n, co=co: (nb - 1 - n, co + j)),
            pl.BlockSpec((8, cblk), lambda j, n, co=co: (jnp.maximum((nb - 1 - n) * t8 - 1, 0), co + j)),
            pl.BlockSpec((ksz, cblk), lambda j, n, wo=wo: (0, wo + j)),
            pl.BlockSpec((1, cblk), lambda j, n, wo=wo: (0, wo + j)),
        ]
        out_specs += [
            pl.BlockSpec((tb, cblk), lambda j, n: (nb - 1 - n, j)),
            pl.BlockSpec((ksz, cblk), lambda j, n: (0, j)),
            pl.BlockSpec((1, cblk), lambda j, n: (0, j)),
        ]
        out_shape += [jax.ShapeDtypeStruct((t, c_out), MXU),
                      jax.ShapeDtypeStruct((ksz, c_out), F32),
                      jax.ShapeDtypeStruct((1, c_out), F32)]
        scratch.append(pltpu.VMEM((8, cblk), F32))
    ins.append(dy)
    specs.append(pl.BlockSpec((tb, cblk), lambda j, n: (nb - 1 - n, j)))
    res = _call(kern, ins, name=name, grid=(ncb, nb), in_specs=specs, out_specs=out_specs,
                out_shape=out_shape, scratch_shapes=scratch, sem=("parallel", "arbitrary"), carry=carry)
    dxs = [res[3 * q] for q in range(nin)]
    dcw = jnp.concatenate([res[3 * q + 1] for q in range(nin)], axis=1)
    dcb = jnp.concatenate([res[3 * q + 2] for q in range(nin)], axis=1)
    return dxs, dcw, dcb.reshape(ctot), res[3 * nin:]


def _lru_tile(t):
    return _pick(t, (640, 384, 256, 128))


def _lru_gates(xc, wa, ba, wx, bx, sp):
    r = _sigmoid(_dot(xc, wa) + ba)
    i = _sigmoid(_dot(xc, wx) + bx)
    log_a = -LRU_C * r * sp
    a = jnp.exp(log_a)
    mult = jnp.sqrt(-_expm1(2.0 * log_a))
    return r, i, a, mult


def lru_fwd(proj, cw, cb, wa, ba, wx, bx, lam, *, gate_off, xr_off, name, carry=None):
    t = proj.shape[0]
    tb = _lru_tile(t)
    nb, ns, t8 = t // tb, tb // BLK, tb // 8
    go, xo = gate_off // BLK, xr_off // BLK

    def kern(g_ref, x_ref, xt_ref, cw_ref, cb_ref, wa_ref, ba_ref, wx_ref, bx_ref, lam_ref,
             y_ref, yt_ref, h_ref, hc_ref):
        n = pl.program_id(1)

        @pl.when(n == 0)
        def _():
            hc_ref[...] = jnp.zeros_like(hc_ref)

        sp = _softplus(-lam_ref[...])
        hprev = hc_ref[0:1, :]
        for s in range(ns):
            sl = slice(s * BLK, (s + 1) * BLK)
            xv = x_ref[sl, :]
            tail = jnp.where(n > 0, xt_ref[...], 0.0) if s == 0 else x_ref[s * BLK - 8:s * BLK, :]
            valid = (n * tb + s * BLK + _rows((BLK, BLK))) >= PAD
            xc = _keep(_conv_apply(xv, tail, cw_ref[...], cb_ref[...], 4), valid, s)
            _, i, a, mult = _lru_gates(xc, wa_ref[0], ba_ref[...], wx_ref[0], bx_ref[...], sp)
            u = mult * (i * xc)
            ca, cu = _scan_fwd(a, u)
            h = cu + ca * hprev
            hprev = _row_at(h, BLK - 1)
            h_ref[sl, :] = h
            y = _gelu(g_ref[sl, :]) * h
            y_ref[sl, :] = y.astype(y_ref.dtype)
            yt_ref[:, sl] = y.T.astype(yt_ref.dtype)
        hc_ref[...] = jnp.broadcast_to(hprev, hc_ref.shape)

    vec = pl.BlockSpec((1, BLK), lambda j, n: (0, j))
    mat = pl.BlockSpec((1, BLK, BLK), lambda j, n: (j, 0, 0))
    return _call(
        kern, (proj, proj, proj, cw, cb.reshape(1, -1), wa, ba.reshape(1, -1), wx, bx.reshape(1, -1),
               lam.reshape(1, -1)),
        name=name, grid=(D_RNN // BLK, nb),
        in_specs=[
            pl.BlockSpec((tb, BLK), lambda j, n: (n, go + j)),
            pl.BlockSpec((tb, BLK), lambda j, n: (n, xo + j)),
            pl.BlockSpec((8, BLK), lambda j, n: (jnp.maximum(n * t8 - 1, 0), xo + j)),
            pl.BlockSpec((4, BLK), lambda j, n: (0, j)), vec, mat, vec, mat, vec, vec,
        ],
        out_specs=[pl.BlockSpec((tb, BLK), lambda j, n: (n, j)),
                   pl.BlockSpec((BLK, tb), lambda j, n: (j, n)),
                   pl.BlockSpec((tb, BLK), lambda j, n: (n, j))],
        out_shape=[jax.ShapeDtypeStruct((t, D_RNN), MXU), jax.ShapeDtypeStruct((D_RNN, t), MXU),
                   jax.ShapeDtypeStruct((t, D_RNN), F32)],
        scratch_shapes=[pltpu.VMEM((8, BLK), F32)],
        sem=("parallel", "arbitrary"), carry=carry)


def lru_bwd(proj, h, dy, cw, cb, wa, ba, wx, bx, lam, *, gate_off, xr_off, dy_off, name):
    t = proj.shape[0]
    tb = _lru_tile(t)
    nb, ns, t8 = t // tb, tb // BLK, tb // 8
    go, xo, do = gate_off // BLK, xr_off // BLK, dy_off // BLK

    def kern(g_ref, x_ref, xt_ref, h_ref, ht_ref, dy_ref, cw_ref, cb_ref, wa_ref, ba_ref,
             wx_ref, bx_ref, lam_ref,
             dg_ref, dx_ref, dcw_ref, dcb_ref, dwa_ref, dba_ref, dwx_ref, dbx_ref, dlam_ref,
             gin_ref, head_ref):
        n = pl.program_id(1)
        blk = nb - 1 - n

        @pl.when(n == 0)
        def _():
            gin_ref[...] = jnp.zeros_like(gin_ref)
            head_ref[...] = jnp.zeros_like(head_ref)
            for r_ in (dcw_ref, dcb_ref, dwa_ref, dba_ref, dwx_ref, dbx_ref, dlam_ref):
                r_[...] = jnp.zeros_like(r_)

        lamv = lam_ref[...]
        sp = _softplus(-lamv)
        dsp_dlam = -_sigmoid(-lamv)
        g_in = gin_ref[0:1, :]
        head = head_ref[...]
        ones8 = jnp.ones((8, BLK), F32)
        for s in reversed(range(ns)):
            sl = slice(s * BLK, (s + 1) * BLK)
            xv = x_ref[sl, :]
            if s == 0:
                tail = jnp.where(blk > 0, xt_ref[...], 0.0)
                htail = jnp.where(blk > 0, ht_ref[...], 0.0)
            else:
                tail = x_ref[s * BLK - 8:s * BLK, :]
                htail = h_ref[s * BLK - 8:s * BLK, :]
            valid = (blk * tb + s * BLK + _rows((BLK, BLK))) >= PAD
            xc = _keep(_conv_apply(xv, tail, cw_ref[...], cb_ref[...], 4), valid, s)
            wav, wxv = wa_ref[0], wx_ref[0]
            r, i, a, mult = _lru_gates(xc, wav, ba_ref[...], wxv, bx_ref[...], sp)
            hv = h_ref[sl, :]
            hprev = _shift_down(hv, htail, 1)
            gv = g_ref[sl, :]
            dyv = dy_ref[sl, :].astype(F32)
            dh = dyv * _gelu(gv)
            dg_ref[sl, :] = (dyv * hv * _gelu_grad(gv)).astype(dg_ref.dtype)
            c = _shift_up(a, ones8, 1)
            cc, cu = _scan_rev(c, dh)
            gg = cu + cc * g_in
            g_in = _row_at(a * gg, 0)
            da = gg * hprev
            di = gg * mult * xc
            dxc = gg * mult * i
            dmult = gg * i * xc
            dlog_a = da * a - dmult * (a * a) / mult
            dr = dlog_a * (-LRU_C * sp)
            dlam_ref[...] += jnp.sum(dlog_a * (-LRU_C) * r, axis=0, keepdims=True) * dsp_dlam
            dpr = dr * r * (1.0 - r)
            dpi = di * i * (1.0 - i)
            dxc = dxc + _dot_nt(dpr, wav) + _dot_nt(dpi, wxv)
            dxc, dpr, dpi = _keep(dxc, valid, s), _keep(dpr, valid, s), _keep(dpi, valid, s)
            dwa_ref[0] += _dot_tn(xc, dpr)
            dwx_ref[0] += _dot_tn(xc, dpi)
            dba_ref[...] += jnp.sum(dpr, axis=0, keepdims=True)
            dbx_ref[...] += jnp.sum(dpi, axis=0, keepdims=True)
            dx = jnp.zeros_like(dxc)
            dws = []
            for k in range(4):
                dx = dx + cw_ref[k:k + 1, :] * _shift_up(dxc, head, 3 - k)
                dws.append(jnp.sum(dxc * _shift_down(xv, tail, 3 - k), axis=0, keepdims=True))
            dx_ref[sl, :] = _keep(dx, valid, s).astype(dx_ref.dtype)
            dcw_ref[...] += jnp.concatenate(dws, axis=0)
            dcb_ref[...] += jnp.sum(dxc, axis=0, keepdims=True)
            head = dxc[0:8]
        gin_ref[...] = jnp.broadcast_to(g_in, gin_ref.shape)
        head_ref[...] = head

    vec = pl.BlockSpec((1, BLK), lambda j, n: (0, j))
    mat = pl.BlockSpec((1, BLK, BLK), lambda j, n: (j, 0, 0))
    cws = pl.BlockSpec((4, BLK), lambda j, n: (0, j))

    def rb(off):
        return pl.BlockSpec((tb, BLK), lambda j, n: (nb - 1 - n, off + j))

    def tl(off):
        return pl.BlockSpec((8, BLK), lambda j, n: (jnp.maximum((nb - 1 - n) * t8 - 1, 0), off + j))

    return pl.pallas_call(
        kern, name=name, grid=(D_RNN // BLK, nb),
        in_specs=[rb(go), rb(xo), tl(xo), rb(0), tl(0), rb(do), cws, vec, mat, vec, mat, vec, vec],
        out_specs=[rb(0), rb(0), cws, vec, mat, vec, mat, vec, vec],
        out_shape=[jax.ShapeDtypeStruct((t, D_RNN), MXU), jax.ShapeDtypeStruct((t, D_RNN), MXU),
                   jax.ShapeDtypeStruct((4, D_RNN), F32), jax.ShapeDtypeStruct((1, D_RNN), F32),
                   jax.ShapeDtypeStruct((8, BLK, BLK), F32), jax.ShapeDtypeStruct((1, D_RNN), F32),
                   jax.ShapeDtypeStruct((8, BLK, BLK), F32), jax.ShapeDtypeStruct((1, D_RNN), F32),
                   jax.ShapeDtypeStruct((1, D_RNN), F32)],
        scratch_shapes=[pltpu.VMEM((8, BLK), F32), pltpu.VMEM((8, BLK), F32)],
        compiler_params=_cp("parallel", "arbitrary"),
    )(proj, proj, proj, h, h, dy, cw, cb.reshape(1, -1), wa, ba.reshape(1, -1), wx,
      bx.reshape(1, -1), lam.reshape(1, -1))


_SCALE = HEAD_DIM ** -0.5


STK = 4


def _attn_masks(n):
    qi = np.arange(STK * BLK)[:, None] % BLK
    c = np.arange(3 * BLK)[None, :]
    tq = n * BLK + qi - PAD
    s_band = (n - 1) * BLK + c - PAD
    d_band = tq - s_band
    ok_band = (s_band >= N_META) & (d_band >= 0) & (d_band < BLK)
    jm = c - 2 * BLK
    d_meta = tq - (jm - PAD)
    ok_meta = (jm >= PAD) & (d_meta >= 0)
    is_band = c < 2 * BLK
    ok = np.where(is_band, ok_band, ok_meta)
    dist = np.where(is_band, d_band, np.minimum(d_meta, BLK)).astype(np.float32)
    return ok, dist


def _stack_heads(g, e):
    return [8 * g + 2 * i + e for i in range(STK)]


def _attn_bias_table():
    tabs = []
    for n in range(3):
        ok, dist = _attn_masks(n)
        per = []
        for g in range(2):
            for e in range(2):
                slope = np.repeat(np.array([2.0 ** (-8.0 * (h + 1) / N_Q_HEADS) for h in _stack_heads(g, e)],
                                           np.float32), BLK)[:, None]
                per.append(np.where(ok, -(slope * dist), np.float32(NEG)).astype(np.float32))
        tabs.append(np.stack(per))
    return jnp.asarray(np.stack(tabs))


def _stack_sinks(heads, sk):
    return jnp.concatenate(
        [jnp.broadcast_to(jnp.sum(jnp.where(_lanes(sk.shape) == h, sk, 0.0), axis=1, keepdims=True),
                          (BLK, 1)) for h in heads], axis=0)


def _stack_tiles(ref, g, sel):
    return jnp.concatenate(
        [jnp.where(sel, ref[:, (4 * g + i) * BLK:(4 * g + i + 1) * BLK].astype(F32), 0.0)
         for i in range(STK)], axis=0)


def _attn_probs(qk, bias, sink):
    s = qk * _SCALE + bias
    mx = jnp.maximum(jnp.max(s, axis=-1, keepdims=True), sink)
    p = jnp.exp(s - mx)
    es = jnp.exp(sink - mx)
    inv = 1.0 / (jnp.sum(p, axis=-1, keepdims=True) + es)
    return p * inv, es * inv


def _attn_specs(t, q_off, k_off, v_off, rev):
    nb = t // BLK
    qo, ko, vo = q_off // 1024, k_off // BLK, v_off // BLK

    def b(n):
        return nb - 1 - n if rev else n

    return [
        pl.BlockSpec((BLK, 1024), lambda n: (b(n), qo)),
        pl.BlockSpec((BLK, BLK), lambda n: (b(n), ko)),
        pl.BlockSpec((BLK, BLK), lambda n: (b(n), vo)),
        pl.BlockSpec((BLK, BLK), lambda n: (jnp.maximum(b(n) - 1, 0), ko)),
        pl.BlockSpec((BLK, BLK), lambda n: (jnp.maximum(b(n) - 1, 0), vo)),
        pl.BlockSpec((BLK, BLK), lambda n: (0, ko)),
        pl.BlockSpec((BLK, BLK), lambda n: (0, vo)),
        pl.BlockSpec((1, BLK), lambda n: (0, 0)),
        pl.BlockSpec((1, 4, STK * BLK, 3 * BLK), lambda n: (jnp.minimum(b(n), 2), 0, 0, 0)),
    ]


def attn_fwd(proj, sinks, *, q_off, k_off, v_off, name, carry=None):
    t = proj.shape[0]
    nb = t // BLK

    def kern(q_ref, kc_ref, vc_ref, kp_ref, vp_ref, km_ref, vm_ref, sk_ref, tab_ref, o_ref):
        k_all = jnp.concatenate([kp_ref[...], kc_ref[...], km_ref[...]], axis=0)
        v_all = jnp.concatenate([vp_ref[...], vc_ref[...], vm_ref[...]], axis=0)
        k_alt = pltpu.roll(k_all, HEAD_DIM, 1)
        v_alt = pltpu.roll(v_all, HEAD_DIM, 1)
        low = _lanes((BLK, BLK)) < HEAD_DIM
        stacks = [(g, e) for g in range(2) for e in range(2)]
        qk = {(g, e): _dot_nt(_stack_tiles(q_ref, g, low == (e == 0)), k_all if g == e else k_alt)
              for g, e in stacks}
        ps = {(g, e): _attn_probs(qk[g, e], tab_ref[0, 2 * g + e],
                                  _stack_sinks(_stack_heads(g, e), sk_ref[...]))[0] for g, e in stacks}
        outs = {(g, e): _dot(ps[g, e], v_all if g == e else v_alt) for g, e in stacks}
        for hp in range(N_Q_HEADS // 2):
            g, rs = hp // STK, slice((hp % STK) * BLK, (hp % STK + 1) * BLK)
            o_ref[:, hp * BLK:(hp + 1) * BLK] = jnp.where(low, outs[g, 0][rs], outs[g, 1][rs]).astype(o_ref.dtype)

    sk = jnp.zeros((1, BLK), F32).at[0, :N_Q_HEADS].set(sinks)
    return _call(
        kern, (proj, proj, proj, proj, proj, proj, proj, sk, _attn_bias_table()), name=name, grid=(nb,),
        in_specs=_attn_specs(t, q_off, k_off, v_off, False),
        out_specs=[pl.BlockSpec((BLK, 1024), lambda n: (n, 0))],
        out_shape=[jax.ShapeDtypeStruct((t, 1024), MXU)],
        sem=("parallel",), carry=carry)


def attn_bwd(proj, sinks, dy, *, q_off, k_off, v_off, dy_off, name, carry=None):
    t = proj.shape[0]
    nb = t // BLK
    do = dy_off // 1024

    def kern(q_ref, kc_ref, vc_ref, kp_ref, vp_ref, km_ref, vm_ref, sk_ref, tab_ref, do_ref,
             dq_ref, dk_ref, dv_ref, dsk_ref, ck_ref, cv_ref, mk_ref, mv_ref):
        n = pl.program_id(0)
        blk = nb - 1 - n

        @pl.when(n == 0)
        def _():
            for r_ in (ck_ref, cv_ref, mk_ref, mv_ref, dsk_ref):
                r_[...] = jnp.zeros_like(r_)

        k_all = jnp.concatenate([kp_ref[...], kc_ref[...], km_ref[...]], axis=0)
        v_all = jnp.concatenate([vp_ref[...], vc_ref[...], vm_ref[...]], axis=0)
        k_alt = pltpu.roll(k_all, HEAD_DIM, 1)
        v_alt = pltpu.roll(v_all, HEAD_DIM, 1)
        low = _lanes((BLK, BLK)) < HEAD_DIM
        lane1 = _lanes((1, BLK))
        dk_all = jnp.zeros((3 * BLK, BLK), F32)
        dv_all = jnp.zeros((3 * BLK, BLK), F32)
        dsk = jnp.zeros((1, BLK), F32)
        stacks = [(g, e) for g in range(2) for e in range(2)]
        qm = {(g, e): _stack_tiles(q_ref, g, low == (e == 0)) for g, e in stacks}
        dom = {(g, e): _stack_tiles(do_ref, g, low == (e == 0)) for g, e in stacks}
        qk = {(g, e): _dot_nt(qm[g, e], k_all if g == e else k_alt) for g, e in stacks}
        dp = {(g, e): _dot_nt(dom[g, e], v_all if g == e else v_alt) for g, e in stacks}
        ps, dss = {}, {}
        for g, e in stacks:
            heads = _stack_heads(g, e)
            p, psink = _attn_probs(qk[g, e], tab_ref[0, 2 * g + e], _stack_sinks(heads, sk_ref[...]))
            delta = jnp.sum(p * dp[g, e], axis=-1, keepdims=True)
            ps[g, e] = p
            dss[g, e] = p * (dp[g, e] - delta) * _SCALE
            psd = psink * delta
            for i, h in enumerate(heads):
                dsk = dsk + jnp.where(lane1 == h, -jnp.sum(psd[i * BLK:(i + 1) * BLK], axis=0, keepdims=True), 0.0)
        dqs = {(g, e): _dot(dss[g, e], k_all if g == e else k_alt) for g, e in stacks}
        for g, e in stacks:
            dkh = _dot_tn(dss[g, e], qm[g, e])
            dvh = _dot_tn(ps[g, e], dom[g, e])
            if g != e:
                dkh = pltpu.roll(dkh, HEAD_DIM, 1)
                dvh = pltpu.roll(dvh, HEAD_DIM, 1)
            dk_all = dk_all + dkh
            dv_all = dv_all + dvh
        for hp in range(N_Q_HEADS // 2):
            g, rs = hp // STK, slice((hp % STK) * BLK, (hp % STK + 1) * BLK)
            dq_ref[:, hp * BLK:(hp + 1) * BLK] = jnp.where(low, dqs[g, 0][rs], dqs[g, 1][rs]).astype(dq_ref.dtype)
        dsk_ref[...] += dsk
        mk_ref[...] += dk_all[2 * BLK:3 * BLK]
        mv_ref[...] += dv_all[2 * BLK:3 * BLK]
        is0 = blk == 0
        dk_ref[...] = (dk_all[BLK:2 * BLK] + ck_ref[...] + jnp.where(is0, mk_ref[...], 0.0)).astype(dk_ref.dtype)
        dv_ref[...] = (dv_all[BLK:2 * BLK] + cv_ref[...] + jnp.where(is0, mv_ref[...], 0.0)).astype(dv_ref.dtype)
        ck_ref[...] = dk_all[0:BLK]
        cv_ref[...] = dv_all[0:BLK]

    sk = jnp.zeros((1, BLK), F32).at[0, :N_Q_HEADS].set(sinks)
    kv = pl.BlockSpec((BLK, BLK), lambda n: (nb - 1 - n, 0))
    res = _call(
        kern, (proj, proj, proj, proj, proj, proj, proj, sk, _attn_bias_table(), dy), name=name, grid=(nb,),
        in_specs=_attn_specs(t, q_off, k_off, v_off, True)
        + [pl.BlockSpec((BLK, 1024), lambda n: (nb - 1 - n, do))],
        out_specs=[pl.BlockSpec((BLK, 1024), lambda n: (nb - 1 - n, 0)), kv, kv,
                   pl.BlockSpec((1, BLK), lambda n: (0, 0))],
        out_shape=[jax.ShapeDtypeStruct((t, 1024), MXU), jax.ShapeDtypeStruct((t, BLK), MXU),
                   jax.ShapeDtypeStruct((t, BLK), MXU), jax.ShapeDtypeStruct((1, BLK), F32)],
        scratch_shapes=[pltpu.VMEM((BLK, BLK), F32)] * 4,
        sem=("arbitrary",), carry=carry)
    return [res[0], res[1], res[2], res[3][0, :N_Q_HEADS]] + res[4:]


GW = D_SSM // SSD_GROUPS
EXP_ROWS = 3 * BLK + 8
RED_ROWS = EXP_ROWS + 8


def _head_expand():
    ch = jnp.arange(D_SSM) // HEAD_DIM
    return (jnp.arange(BLK)[:, None] == ch[None, :]).astype(BF16)


def _ssd_decay(raw, dtb, alog, rowv):
    valid = rowv & (_lanes((BLK, BLK)) < SSD_HEADS)
    pre = raw + dtb
    dtp = jnp.where(valid, _softplus(pre), 0.0)
    av = -jnp.exp(alog)
    cs = _cumsum_fwd(dtp * av)
    cs_last = _row_at(cs, BLK - 1)
    return valid, pre, dtp, av, cs, jnp.exp(cs), jnp.exp(cs_last - cs), jnp.exp(cs_last)


def _head_col(x, h):
    return jnp.sum(jnp.where(_lanes(x.shape) == h, x, 0.0), axis=1, keepdims=True)


def _ssd_group_fwd(g, xdt, cs, cst, cb, tril, low):
    lm = []
    for k in range(4):
        h = 4 * g + k
        seg = _head_col(cs, h) - _row_at(cst, h)
        lmat = jnp.where(tril, jnp.exp(jnp.minimum(seg, 0.0)), 0.0)
        lm.append((lmat, cb * lmat))
    hv = [_dot(lm[k][1], xdt[:, g * GW + (k // 2) * BLK:g * GW + (k // 2 + 1) * BLK]) for k in range(4)]
    return jnp.concatenate([jnp.where(low, hv[0], hv[1]), jnp.where(low, hv[2], hv[3])], axis=1), lm


def _expand_heads(dtp, ecs, w, dec, e):
    ex = _dot(jnp.concatenate([dtp, ecs, w, jnp.broadcast_to(dec, (8, BLK))], axis=0), e)
    return ex[0:BLK], ex[BLK:2 * BLK], ex[2 * BLK:3 * BLK], jnp.max(ex[3 * BLK:EXP_ROWS], axis=0, keepdims=True)


def _ssd_specs(t, z_off, dt_off, rev):
    nb = t // BLK
    zo, dto = z_off // D_SSM, dt_off // BLK

    def b(n):
        return nb - 1 - n if rev else n

    vec = lambda w: pl.BlockSpec((1, w), lambda n: (0, 0))
    return [
        pl.BlockSpec((BLK, D_SSM), lambda n: (b(n), 0)),
        pl.BlockSpec((BLK, 1024), lambda n: (b(n), 2)),
        pl.BlockSpec((BLK, 1024), lambda n: (b(n), 3)),
        pl.BlockSpec((BLK, D_SSM), lambda n: (b(n), zo)),
        pl.BlockSpec((BLK, BLK), lambda n: (b(n), dto)),
        vec(BLK), vec(BLK), vec(D_SSM), vec(D_SSM),
        pl.BlockSpec((BLK, D_SSM), lambda n: (0, 0)),
    ]


def _pad128(v):
    return jnp.zeros((1, BLK), F32).at[0, :v.shape[0]].set(v)


def ssd_fwd(xbc, proj, dt_bias, a_log, d_skip, gate_norm, *, z_off, dt_off, name):
    t = xbc.shape[0]
    nb = t // BLK

    def kern(x_ref, b_ref, c_ref, z_ref, dt_ref, dtb_ref, alog_ref, dsk_ref, gn_ref, e_ref,
             yn_ref, ynt_ref, st_ref, p_ref):
        n = pl.program_id(0)

        @pl.when(n == 0)
        def _():
            p_ref[...] = jnp.zeros_like(p_ref)

        bgs = [b_ref[:, g * BLK:(g + 1) * BLK] for g in range(SSD_GROUPS)]
        cgs = [c_ref[:, g * BLK:(g + 1) * BLK] for g in range(SSD_GROUPS)]
        cbs = [_dot_nt(cgs[g], bgs[g]) for g in range(SSD_GROUPS)]
        zs = [_dot(cgs[g], p_ref[g]) for g in range(SSD_GROUPS)]
        rowv = (n * BLK + _rows((BLK, BLK))) >= PAD
        _, _, dtp, _, cs, ecs, w, dec = _ssd_decay(dt_ref[...], dtb_ref[...], alog_ref[...], rowv)
        dtp_c, ecs_c, w_c, dec_c = _expand_heads(dtp, ecs, w, dec, e_ref[...])
        xv = x_ref[...]
        xdt = xv * dtp_c
        wx = w_c * xdt
        cst = cs.T
        tril = _rows((BLK, BLK)) >= _lanes((BLK, BLK))
        low = _lanes((BLK, BLK)) < HEAD_DIM
        st_ref[0] = p_ref[...]
        for g in range(SSD_GROUPS):
            gs = slice(g * GW, (g + 1) * GW)
            bg = bgs[g]
            pg = p_ref[g]
            ydiag, _ = _ssd_group_fwd(g, xdt, cs, cst, cbs[g], tril, low)
            y = ydiag + zs[g] * ecs_c[:, gs] + dsk_ref[:, gs] * xv[:, gs]
            p_ref[g] = pg * dec_c[:, gs] + _dot_tn(bg, wx[:, gs])
            yz = y * _silu(z_ref[:, gs])
            r = lax.rsqrt(jnp.mean(yz * yz, axis=-1, keepdims=True) + EPS)
            yn = yz * r * gn_ref[:, gs]
            yn_ref[:, gs] = yn.astype(yn_ref.dtype)
            ynt_ref[gs, :] = yn.T.astype(ynt_ref.dtype)

    return pl.pallas_call(
        kern, name=name, grid=(nb,),
        in_specs=_ssd_specs(t, z_off, dt_off, False),
        out_specs=[pl.BlockSpec((BLK, D_SSM), lambda n: (n, 0)),
                   pl.BlockSpec((D_SSM, BLK), lambda n: (0, n)),
                   pl.BlockSpec((1, SSD_GROUPS, BLK, GW), lambda n: (n, 0, 0, 0))],
        out_shape=[jax.ShapeDtypeStruct((t, D_SSM), MXU), jax.ShapeDtypeStruct((D_SSM, t), MXU),
                   jax.ShapeDtypeStruct((nb, SSD_GROUPS, BLK, GW), F32)],
        scratch_shapes=[pltpu.VMEM((SSD_GROUPS, BLK, GW), F32)],
        compiler_params=_cp("arbitrary"),
    )(xbc, xbc, xbc, proj, proj, _pad128(dt_bias), _pad128(a_log),
      jnp.repeat(d_skip, HEAD_DIM).reshape(1, D_SSM), gate_norm.reshape(1, D_SSM), _head_expand())


def ssd_bwd(xbc, proj, st, dyn, dt_bias, a_log, d_skip, gate_norm, *, z_off, dt_off, name, carry=None):
    t = xbc.shape[0]
    nb = t // BLK

    def kern(x_ref, b_ref, c_ref, z_ref, dt_ref, dtb_ref, alog_ref, dsk_ref, gn_ref, e_ref,
             et_ref, st_ref, dyn_ref,
             dxbc_ref, dz_ref, draw_ref, dgn_ref, ddsk_ref, ddtb_ref, dalog_ref,
             dp_ref, tr_ref):
        n = pl.program_id(0)
        blk = nb - 1 - n

        @pl.when(n == 0)
        def _():
            for r_ in (dp_ref, dgn_ref, ddsk_ref, ddtb_ref, dalog_ref):
                r_[...] = jnp.zeros_like(r_)

        bgs = [b_ref[:, g * BLK:(g + 1) * BLK] for g in range(SSD_GROUPS)]
        cgs = [c_ref[:, g * BLK:(g + 1) * BLK] for g in range(SSD_GROUPS)]
        cbs = [_dot_nt(cgs[g], bgs[g]) for g in range(SSD_GROUPS)]
        zs = [_dot(cgs[g], st_ref[0, g]) for g in range(SSD_GROUPS)]
        dwxs = [_dot(bgs[g], dp_ref[g]) for g in range(SSD_GROUPS)]
        rowv = (blk * BLK + _rows((BLK, BLK))) >= PAD
        valid, pre, dtp, av, cs, ecs, w, dec = _ssd_decay(dt_ref[...], dtb_ref[...], alog_ref[...], rowv)
        dtp_c, ecs_c, w_c, dec_c = _expand_heads(dtp, ecs, w, dec, e_ref[...])
        xv = x_ref[...]
        xdt = xv * dtp_c
        wx = w_c * xdt
        cst = cs.T
        tril = _rows((BLK, BLK)) >= _lanes((BLK, BLK))
        lane = _lanes((BLK, BLK))
        rowi = _rows((BLK, BLK))
        low = lane < HEAD_DIM
        dcs = jnp.zeros((BLK, BLK), F32)
        dcst = jnp.zeros((BLK, BLK), F32)
        for g in range(SSD_GROUPS):
            gs = slice(g * GW, (g + 1) * GW)
            bg, cg = bgs[g], cgs[g]
            pg = st_ref[0, g]
            dpn = dp_ref[g]
            xg = xv[:, gs]
            ydiag, lm = _ssd_group_fwd(g, xdt, cs, cst, cbs[g], tril, low)
            yoff = zs[g] * ecs_c[:, gs]
            y = ydiag + yoff + dsk_ref[:, gs] * xg
            zz = z_ref[:, gs]
            sz = _silu(zz)
            yz = y * sz
            r = lax.rsqrt(jnp.mean(yz * yz, axis=-1, keepdims=True) + EPS)
            yhat = yz * r
            dynv = dyn_ref[:, gs].astype(F32)
            gy = dynv * gn_ref[:, gs]
            dgn_ref[:, gs] += jnp.sum(dynv * yhat, axis=0, keepdims=True)
            dyz = r * (gy - yhat * jnp.mean(gy * yhat, axis=-1, keepdims=True))
            dy = dyz * sz
            dz_ref[:, gs] = (dyz * y * _silu_grad(zz)).astype(dz_ref.dtype)
            tr_ref[EXP_ROWS:RED_ROWS, gs] = jnp.broadcast_to(
                jnp.sum(dy * xg, axis=0, keepdims=True), (8, GW))
            dx = dsk_ref[:, gs] * dy
            dwx = dwxs[g]
            dxdt = w_c[:, gs] * dwx
            tr_ref[0:BLK, gs] = dwx * wx[:, gs]
            dbg = _dot_nt(wx[:, gs], dpn)
            dzo = ecs_c[:, gs] * dy
            tr_ref[BLK:2 * BLK, gs] = dy * yoff
            dcg = _dot_nt(dzo, pg)
            dp_ref[g] = dec_c[:, gs] * dpn + _dot_tn(cg, dzo)
            tr_ref[3 * BLK:EXP_ROWS, gs] = jnp.broadcast_to(
                jnp.sum(dpn * pg, axis=0, keepdims=True), (8, GW))
            dyh = [jnp.where(low == (k % 2 == 0), dy[:, (k // 2) * BLK:(k // 2 + 1) * BLK], 0.0) for k in range(4)]
            dms = [_dot_nt(dyh[k], xdt[:, g * GW + (k // 2) * BLK:g * GW + (k // 2 + 1) * BLK]) for k in range(4)]
            accs = [_dot_tn(lm[k][1], dyh[k]) for k in range(4)]
            dcb = jnp.zeros((BLK, BLK), F32)
            for k in range(4):
                h = 4 * g + k
                lmat, mmat = lm[k]
                dm = jnp.where(tril, dms[k], 0.0)
                nh = dm * mmat
                dcs = dcs + jnp.where(lane == h, jnp.sum(nh, axis=1, keepdims=True), 0.0)
                dcst = dcst - jnp.where(rowi == h, jnp.sum(nh, axis=0, keepdims=True), 0.0)
                dcb = dcb + dm * lmat
            dxdt = dxdt + jnp.concatenate([accs[0] + accs[1], accs[2] + accs[3]], axis=1)
            dcg = dcg + _dot(dcb, bg)
            dbg = dbg + _dot_tn(dcb, cg)
            tr_ref[2 * BLK:3 * BLK, gs] = dxdt * xg
            dxbc_ref[:, gs] = dx + dxdt * dtp_c[:, gs]
            dxbc_ref[:, D_SSM + g * BLK:D_SSM + (g + 1) * BLK] = dbg
            dxbc_ref[:, D_SSM + 1024 + g * BLK:D_SSM + 1024 + (g + 1) * BLK] = dcg
        red = _dot(tr_ref[...], et_ref[...])
        r1, r2, r3 = red[0:BLK], red[BLK:2 * BLK], red[2 * BLK:3 * BLK]
        ddec = jnp.max(red[3 * BLK:EXP_ROWS], axis=0, keepdims=True)
        ddsk_ref[...] += jnp.max(red[EXP_ROWS:RED_ROWS], axis=0, keepdims=True)
        dcs = dcs + dcst.T - r1 + r2
        dcs_last = jnp.sum(r1, axis=0, keepdims=True) + ddec * dec
        dcs = dcs + jnp.where(rowi == BLK - 1, dcs_last, 0.0)
        dda = _cumsum_rev(dcs)
        ddtp = r3 + dda * av
        dalog_ref[...] += jnp.sum(dda * dtp, axis=0, keepdims=True) * av
        draw = jnp.where(valid, ddtp * _sigmoid(pre), 0.0)
        ddtb_ref[...] += jnp.sum(draw, axis=0, keepdims=True)
        draw_ref[...] = draw.astype(draw_ref.dtype)

    vec = lambda w_: pl.BlockSpec((1, w_), lambda n: (0, 0))
    rb = lambda w_: pl.BlockSpec((BLK, w_), lambda n: (nb - 1 - n, 0))
    e = _head_expand()
    res = _call(
        kern, (xbc, xbc, xbc, proj, proj, _pad128(dt_bias), _pad128(a_log),
               jnp.repeat(d_skip, HEAD_DIM).reshape(1, D_SSM), gate_norm.reshape(1, D_SSM), e, e.T, st, dyn),
        name=name, grid=(nb,),
        in_specs=_ssd_specs(t, z_off, dt_off, True)
        + [pl.BlockSpec((D_SSM, BLK), lambda n: (0, 0)),
           pl.BlockSpec((1, SSD_GROUPS, BLK, GW), lambda n: (nb - 1 - n, 0, 0, 0)),
           rb(D_SSM)],
        out_specs=[rb(2 * D_SSM), rb(D_SSM), rb(BLK), vec(D_SSM), vec(BLK), vec(BLK), vec(BLK)],
        out_shape=[jax.ShapeDtypeStruct((t, 2 * D_SSM), F32), jax.ShapeDtypeStruct((t, D_SSM), MXU),
                   jax.ShapeDtypeStruct((t, BLK), MXU), jax.ShapeDtypeStruct((1, D_SSM), F32),
                   jax.ShapeDtypeStruct((1, BLK), F32), jax.ShapeDtypeStruct((1, BLK), F32),
                   jax.ShapeDtypeStruct((1, BLK), F32)],
        scratch_shapes=[pltpu.VMEM((SSD_GROUPS, BLK, GW), F32), pltpu.VMEM((RED_ROWS, D_SSM), F32)],
        sem=("arbitrary",), carry=carry)
    dxbc, dz, draw, dgn, ddsk, ddtb, dalog = res[:7]
    return [dxbc, dz, draw, dgn[0], ddsk[0, :SSD_HEADS], ddtb[0, :SSD_HEADS], dalog[0, :SSD_HEADS]] + res[7:]


def loss_fwd_bwd(h, target, *, name):
    t, d = h.shape
    nb = t // BLK

    def kern(h_ref, t_ref, loss_ref, dh_ref):
        n = pl.program_id(0)
        err = jnp.where(n > 0, h_ref[...] - t_ref[...], 0.0)
        dh_ref[...] = err * (1.0 / d)
        part = (0.5 / d) * jnp.sum(jnp.sum(err * err, axis=1, keepdims=True), axis=0, keepdims=True)

        @pl.when(n == 0)
        def _():
            loss_ref[...] = part

        @pl.when(n > 0)
        def _():
            loss_ref[...] += part

    return pl.pallas_call(
        kern, name=name, grid=(nb,),
        in_specs=[pl.BlockSpec((BLK, d), lambda n: (n, 0)),
                  pl.BlockSpec((BLK, d), lambda n: (jnp.maximum(n - 1, 0), 0))],
        out_specs=[pl.BlockSpec((1, 1), lambda n: (0, 0)), pl.BlockSpec((BLK, d), lambda n: (n, 0))],
        out_shape=[jax.ShapeDtypeStruct((1, 1), F32), jax.ShapeDtypeStruct((t, d), F32)],
        compiler_params=_cp("arbitrary"),
    )(h, target)


def _ew_tile(r, c):
    cap = max(16, (256 * 1024) // c)
    best = None
    for tr in range(16, min(r, cap) + 1, 16):
        if r % tr == 0:
            best = tr
    return best if best is not None else r


def adamw(parts, w, m, v, *, name):
    npart, r, c = parts.shape
    tr = _ew_tile(r, c)

    def kern(p_ref, w_ref, m_ref, v_ref, g_ref, d_ref, m2_ref, v2_ref):
        g = p_ref[0].astype(F32)
        for k in range(1, npart):
            g = g + p_ref[k].astype(F32)
        m2 = ADAM_B1 * m_ref[...] + (1.0 - ADAM_B1) * g
        v2 = ADAM_B2 * v_ref[...] + (1.0 - ADAM_B2) * (g * g)
        m_hat = m2 / (1.0 - ADAM_B1 ** ADAM_STEP)
        v_hat = v2 / (1.0 - ADAM_B2 ** ADAM_STEP)
        g_ref[...] = g
        d_ref[...] = -ADAM_LR * (m_hat / (jnp.sqrt(v_hat) + ADAM_EPS) + ADAM_WD * w_ref[...])
        m2_ref[...] = m2
        v2_ref[...] = v2

    row = pl.BlockSpec((tr, c), lambda i: (i, 0))
    sds = jax.ShapeDtypeStruct((r, c), F32)
    return pl.pallas_call(
        kern, name=name, grid=(r // tr,),
        in_specs=[pl.BlockSpec((npart, tr, c), lambda i: (0, i, 0)), row, row, row],
        out_specs=[row, row, row, row], out_shape=[sds, sds, sds, sds],
        compiler_params=_cp("parallel"),
    )(parts, w, m, v)


def pair_add(p, land, *, name):
    _, r, c = p.shape
    tr = _ew_tile(r, c)
    core = lax.axis_index("c").astype(jnp.int32).reshape(1)

    def kern(c_ref, p_ref, l_ref, o_ref):
        o_ref[...] = (p_ref[...] + l_ref[...]).astype(o_ref.dtype)

    return pl.pallas_call(
        kern, name=name,
        grid_spec=pltpu.PrefetchScalarGridSpec(
            num_scalar_prefetch=1, grid=(4, r // tr),
            in_specs=[pl.BlockSpec((1, tr, c), lambda k, i, c_ref: (2 * k + c_ref[0], i, 0)),
                      pl.BlockSpec((1, tr, c), lambda k, i, c_ref: (k, i, 0))],
            out_specs=pl.BlockSpec((1, tr, c), lambda k, i, c_ref: (k, i, 0))),
        out_shape=jax.ShapeDtypeStruct((4, r, c), BF16),
        compiler_params=_cp("parallel", "parallel"),
    )(core, p, land)


def _me():
    return lax.axis_index("x"), lax.axis_index("y"), lax.axis_index("c")


def all_gather(xs, *, name):
    n = len(xs)

    def body(*refs):
        x_refs, out_refs = refs[:n], refs[n:2 * n]
        send_sems, recv_sems, local_sems = refs[2 * n:]
        mx, my, mc = _me()
        me, sib = (mx, my, mc), (mx, my, 1 - mc)
        chips = [(1 - mx, my), (mx, 1 - my), (1 - mx, 1 - my)]

        def rows(i, px, py, pc):
            return out_refs[i].at[4 * px + 2 * py + pc]

        def copy(i, k, block, to, src=None):
            return pltpu.make_async_remote_copy(
                src_ref=rows(i, *block) if src is None else src, dst_ref=rows(i, *block),
                send_sem=send_sems.at[7 * i + k], recv_sem=recv_sems.at[7 * i + k],
                device_id=to, device_id_type=MESH)

        mine = [pltpu.make_async_copy(x_refs[i], rows(i, *me), local_sems.at[i]) for i in range(n)]
        first = []
        for i in range(n):
            mine[i].start()
            first.append(copy(i, 0, me, sib, src=x_refs[i]))
            first += [copy(i, 1 + j, me, (*chip, mc), src=x_refs[i]) for j, chip in enumerate(chips)]
        for cp in first:
            cp.start()
        passed = []
        for i in range(n):
            for j, chip in enumerate(chips):
                copy(i, 1 + j, (*chip, mc), me).wait_recv()
                passed.append(copy(i, 4 + j, (*chip, mc), sib))
                passed[-1].start()
        for i in range(n):
            copy(i, 0, sib, me).wait_recv()
            for j, chip in enumerate(chips):
                copy(i, 4 + j, (*chip, 1 - mc), me).wait_recv()
        for cp in first + passed:
            cp.wait_send()
        for cp in mine:
            cp.wait()

    return pl.pallas_call(
        body, name=name,
        out_shape=[jax.ShapeDtypeStruct((N_DEV,) + x.shape, x.dtype) for x in xs],
        in_specs=[ANY] * n, out_specs=[ANY] * n,
        scratch_shapes=[pltpu.SemaphoreType.DMA((7 * n,)), pltpu.SemaphoreType.DMA((7 * n,)),
                        pltpu.SemaphoreType.DMA((n,))],
    )(*xs)


def pair_exchange(ps, *, name):
    n = len(ps)

    def body(*refs):
        p_refs, out_refs = refs[:n], refs[n:2 * n]
        send_sems, recv_sems = refs[2 * n:]
        mx, my, mc = _me()
        cps = [pltpu.make_async_remote_copy(
            src_ref=p_refs[i].at[2 * k + (1 - mc)], dst_ref=out_refs[i].at[k],
            send_sem=send_sems.at[4 * i + k], recv_sem=recv_sems.at[4 * i + k],
            device_id=(mx, my, 1 - mc), device_id_type=MESH) for i in range(n) for k in range(4)]
        for cp in cps:
            cp.start()
        for cp in cps:
            cp.wait_recv()
        for cp in cps:
            cp.wait_send()

    return pl.pallas_call(
        body, name=name,
        out_shape=[jax.ShapeDtypeStruct((4,) + p.shape[1:], p.dtype) for p in ps],
        in_specs=[ANY] * n, out_specs=[ANY] * n,
        scratch_shapes=[pltpu.SemaphoreType.DMA((4 * n,)), pltpu.SemaphoreType.DMA((4 * n,))],
    )(*ps)


def chip_exchange(qs, *, name):
    n = len(qs)

    def body(*refs):
        q_refs, out_refs = refs[:n], refs[n:2 * n]
        send_sems, recv_sems, local_sems = refs[2 * n:]
        mx, my, mc = _me()
        mine = 2 * mx + my
        chips = [(1 - mx, my), (mx, 1 - my), (1 - mx, 1 - my)]
        local, sends, recvs = [], [], []
        for i in range(n):
            local.append(pltpu.make_async_copy(q_refs[i].at[mine], out_refs[i].at[mine], local_sems.at[i]))
            for k, (px, py) in enumerate(chips):
                sems = dict(send_sem=send_sems.at[3 * i + k], recv_sem=recv_sems.at[3 * i + k],
                            device_id=(px, py, mc), device_id_type=MESH)
                sends.append(pltpu.make_async_remote_copy(
                    src_ref=q_refs[i].at[2 * px + py], dst_ref=out_refs[i].at[mine], **sems))
                recvs.append(pltpu.make_async_remote_copy(
                    src_ref=q_refs[i].at[mine], dst_ref=out_refs[i].at[2 * px + py], **sems))
        for cp in local + sends:
            cp.start()
        for cp in recvs:
            cp.wait_recv()
        for cp in sends:
            cp.wait_send()
        for cp in local:
            cp.wait()

    return pl.pallas_call(
        body, name=name,
        out_shape=[jax.ShapeDtypeStruct(q.shape, q.dtype) for q in qs],
        in_specs=[ANY] * n, out_specs=[ANY] * n,
        scratch_shapes=[pltpu.SemaphoreType.DMA((3 * n,)), pltpu.SemaphoreType.DMA((3 * n,)),
                        pltpu.SemaphoreType.DMA((n,))],
    )(*qs)


class _Carry:
    def __init__(self, inputs, out_shapes, sems, start, finish):
        self.inputs, self.out_shapes, self.sems = list(inputs), list(out_shapes), list(sems)
        self.start, self.finish = start, finish


def _call(kern, args, *, name, grid, in_specs, out_specs, out_shape, scratch_shapes=(), sem, carry=None):
    in_specs, out_specs, out_shape = list(in_specs), list(out_specs), list(out_shape)
    scratch_shapes = list(scratch_shapes)
    if carry is None:
        return list(pl.pallas_call(
            kern, name=name, grid=grid, in_specs=in_specs, out_specs=out_specs, out_shape=out_shape,
            scratch_shapes=scratch_shapes, compiler_params=_cp(*sem))(*args))
    ni, no, ns = len(in_specs), len(out_specs), len(scratch_shapes)
    ci, co = len(carry.inputs), len(carry.out_shapes)

    def body(*refs):
        o0 = ni + ci
        s0 = o0 + no + co
        ids = [pl.program_id(d) for d in range(len(grid))]
        first = functools.reduce(jnp.logical_and, [i == 0 for i in ids])
        last = functools.reduce(jnp.logical_and, [i == g - 1 for i, g in zip(ids, grid)])
        cin, cout, sems = refs[ni:o0], refs[o0 + no:s0], refs[s0 + ns:]

        @pl.when(first)
        def _():
            carry.start(cin, cout, sems)

        kern(*refs[:ni], *refs[o0:o0 + no], *refs[s0:s0 + ns])

        @pl.when(last)
        def _():
            carry.finish(cin, cout, sems)

    return list(pl.pallas_call(
        body, name=name, grid=grid, in_specs=in_specs + [ANY] * ci, out_specs=out_specs + [ANY] * co,
        out_shape=out_shape + carry.out_shapes, scratch_shapes=scratch_shapes + carry.sems,
        compiler_params=_cp(*(["arbitrary"] * len(grid))))(*args, *carry.inputs))


def gather_carry(xs):
    n = len(xs)

    def copies(cin, cout, sems, with_recv=True):
        mx, my, mc = _me()
        me = 4 * mx + 2 * my + mc
        peers = [(mx, my, 1 - mc), (1 - mx, my, mc), (mx, 1 - my, mc), (1 - mx, 1 - my, mc)]
        local, send, recv = [], [], []
        for i in range(n):
            local.append(pltpu.make_async_copy(cin[i], cout[i].at[me], sems[2].at[i]))
            for k, peer in enumerate(peers):
                common = dict(send_sem=sems[0].at[4 * i + k], recv_sem=sems[1].at[4 * i + k],
                              device_id=peer, device_id_type=MESH)
                send.append(pltpu.make_async_remote_copy(src_ref=cin[i], dst_ref=cout[i].at[me], **common))
                if with_recv:
                    recv.append(pltpu.make_async_remote_copy(
                        src_ref=cin[i], dst_ref=cout[i].at[4 * peer[0] + 2 * peer[1] + peer[2]], **common))
        return local, send, recv

    def start(cin, cout, sems):
        local, send, _ = copies(cin, cout, sems, with_recv=False)
        for cp in local + send:
            cp.start()

    def finish(cin, cout, sems):
        local, send, recv = copies(cin, cout, sems)
        for cp in recv:
            cp.wait_recv()
        for cp in send:
            cp.wait_send()
        for cp in local:
            cp.wait()

    return _Carry(xs, [jax.ShapeDtypeStruct((N_DEV,) + x.shape, x.dtype) for x in xs],
                  [pltpu.SemaphoreType.DMA((4 * n,)), pltpu.SemaphoreType.DMA((4 * n,)),
                   pltpu.SemaphoreType.DMA((n,))], start, finish)


def gather_relay(outs, *, name):
    n = len(outs)

    def body(*refs):
        bufs = refs[n:2 * n]
        send_sems, recv_sems = refs[2 * n:]
        mx, my, mc = _me()
        chips = [(1 - mx, my), (mx, 1 - my), (1 - mx, 1 - my)]
        send, recv = [], []
        for i in range(n):
            for j, (px, py) in enumerate(chips):
                common = dict(send_sem=send_sems.at[3 * i + j], recv_sem=recv_sems.at[3 * i + j],
                              device_id=(mx, my, 1 - mc), device_id_type=MESH)
                mine = bufs[i].at[4 * px + 2 * py + mc]
                send.append(pltpu.make_async_remote_copy(src_ref=mine, dst_ref=mine, **common))
                recv.append(pltpu.make_async_remote_copy(
                    src_ref=mine, dst_ref=bufs[i].at[4 * px + 2 * py + (1 - mc)], **common))
        for cp in send:
            cp.start()
        for cp in recv:
            cp.wait_recv()
        for cp in send:
            cp.wait_send()

    return pl.pallas_call(
        body, name=name, out_shape=[jax.ShapeDtypeStruct(o.shape, o.dtype) for o in outs],
        in_specs=[ANY] * n, out_specs=[ANY] * n, input_output_aliases={i: i for i in range(n)},
        scratch_shapes=[pltpu.SemaphoreType.DMA((3 * n,)), pltpu.SemaphoreType.DMA((3 * n,))],
    )(*outs)


def pair_carry(ps):
    n = len(ps)

    def copies(cin, cout, sems):
        mx, my, mc = _me()
        return [pltpu.make_async_remote_copy(
            src_ref=cin[i].at[2 * k + (1 - mc)], dst_ref=cout[i].at[k],
            send_sem=sems[0].at[4 * i + k], recv_sem=sems[1].at[4 * i + k],
            device_id=(mx, my, 1 - mc), device_id_type=MESH) for i in range(n) for k in range(4)]

    def start(cin, cout, sems):
        for cp in copies(cin, cout, sems):
            cp.start()

    def finish(cin, cout, sems):
        cps = copies(cin, cout, sems)
        for cp in cps:
            cp.wait_recv()
        for cp in cps:
            cp.wait_send()

    return _Carry(ps, [jax.ShapeDtypeStruct((4,) + p.shape[1:], p.dtype) for p in ps],
                  [pltpu.SemaphoreType.DMA((4 * n,)), pltpu.SemaphoreType.DMA((4 * n,))], start, finish)


def chip_carry(qs):
    n = len(qs)

    def copies(cin, cout, sems, with_recv=True):
        mx, my, mc = _me()
        mine = 2 * mx + my
        chips = [(1 - mx, my), (mx, 1 - my), (1 - mx, 1 - my)]
        local, send, recv = [], [], []
        for i in range(n):
            local.append(pltpu.make_async_copy(cin[i].at[mine], cout[i].at[mine], sems[2].at[i]))
            for k, (px, py) in enumerate(chips):
                common = dict(send_sem=sems[0].at[3 * i + k], recv_sem=sems[1].at[3 * i + k],
                              device_id=(px, py, mc), device_id_type=MESH)
                send.append(pltpu.make_async_remote_copy(
                    src_ref=cin[i].at[2 * px + py], dst_ref=cout[i].at[mine], **common))
                if with_recv:
                    recv.append(pltpu.make_async_remote_copy(
                        src_ref=cin[i].at[mine], dst_ref=cout[i].at[2 * px + py], **common))
        return local, send, recv

    def start(cin, cout, sems):
        local, send, _ = copies(cin, cout, sems, with_recv=False)
        for cp in local + send:
            cp.start()

    def finish(cin, cout, sems):
        local, send, recv = copies(cin, cout, sems)
        for cp in recv:
            cp.wait_recv()
        for cp in send:
            cp.wait_send()
        for cp in local:
            cp.wait()

    return _Carry(qs, [jax.ShapeDtypeStruct(q.shape, q.dtype) for q in qs],
                  [pltpu.SemaphoreType.DMA((3 * n,)), pltpu.SemaphoreType.DMA((3 * n,)),
                   pltpu.SemaphoreType.DMA((n,))], start, finish)


WEIGHTS = [
    "meta_tokens", "l0_mix_pre_norm", "l0_mix_post_norm", "l0_w_in", "l0_lru_conv_w", "l0_lru_conv_b",
    "l0_lru_w_a", "l0_lru_b_a", "l0_lru_w_x", "l0_lru_b_x", "l0_lru_lambda", "l0_attn_sinks", "l0_w_out",
    "l0_ffn_pre_norm", "l0_ffn_post_norm", "l0_ffn_w_up", "l0_ffn_conv_w", "l0_ffn_conv_b", "l0_ffn_w_down",
    "l1_mix_pre_norm", "l1_mix_post_norm", "l1_w_in", "l1_ssm_conv_w", "l1_ssm_conv_b", "l1_dt_bias",
    "l1_a_log", "l1_d_skip", "l1_gate_norm", "l1_w_out", "l1_ffn_pre_norm", "l1_ffn_post_norm",
    "l1_ffn_w_up", "l1_ffn_conv_w", "l1_ffn_conv_b", "l1_ffn_w_down",
]
INPUTS = (["x"] + WEIGHTS + ["loss_target"] + ["m_" + n for n in WEIGHTS] + ["v_" + n for n in WEIGHTS])

MATS = {"l0_w_in": ("col", (1024, 3328)), "l0_w_out": ("row", (2048, 1024)),
        "l0_ffn_w_up": ("col", (1024, 5632)), "l0_ffn_w_down": ("row", (2816, 1024)),
        "l1_w_in": ("col", (1024, 6176)), "l1_w_out": ("row", (2048, 1024)),
        "l1_ffn_w_up": ("col", (1024, 5632)), "l1_ffn_w_down": ("row", (2816, 1024))}
SMALL_SHARDED = {"meta_tokens": ("col", (16, 1024)), "l0_lru_conv_w": ("col", (4, 1024)),
                 "l0_ffn_conv_w": ("col", (3, 5632)), "l1_ssm_conv_w": ("col", (4, 4096)),
                 "l1_ffn_conv_w": ("col", (3, 5632))}
SHARDED = {**MATS, **SMALL_SHARDED}
REPLICATED = [n for n in WEIGHTS if n not in SHARDED]
PACK_W = 1024
SMALL_W = 128


def _shard_shape(name):
    kind, (r, c) = SHARDED[name]
    return (r, c // N_DEV) if kind == "col" else (r // N_DEV, c)


def _rows_of(numel, width):
    return -(-numel // width)


def _to_rows(a, width):
    flat = a.reshape(-1)
    rows = _rows_of(flat.shape[0], width)
    return jnp.pad(flat, (0, rows * width - flat.shape[0])).reshape(rows, width)


def _pack(arrs, width, total_rows):
    slab = jnp.concatenate([_to_rows(a, width) for a in arrs], axis=0)
    return jnp.pad(slab, ((0, total_rows - slab.shape[0]), (0, 0)))


def _unpack(slab, shapes, width):
    out, off = [], 0
    for shp in shapes:
        numel = math.prod(shp)
        rows = _rows_of(numel, width)
        out.append(slab[off:off + rows].reshape(-1)[:numel].reshape(shp))
        off += rows
    return out


def _round_up(n, m):
    return -(-n // m) * m


def _by_dest(name, g):
    kind, (r, c) = SHARDED[name]
    if kind == "col":
        return g.reshape(r, N_DEV, c // N_DEV).transpose(1, 0, 2)
    return g.reshape(N_DEV, r // N_DEV, c)


def _from_shards(name, blocks):
    kind, (r, c) = SHARDED[name]
    return blocks.transpose(1, 0, 2).reshape(r, c) if kind == "col" else blocks.reshape(r, c)


L1_IN_PAD = 6272


def _ffn_fwd(h, a, w, pfx):
    u, ut = rmsnorm_fwd(h, a[pfx + "ffn_pre_norm"], out_dtype=MXU, name=pfx + "ffn_pre", with_t=True)
    up = matmul(u, w[pfx + "ffn_w_up"], name=pfx + "ffn_up")
    act, act_t = dwconv_fwd(up, a[pfx + "ffn_conv_w"], a[pfx + "ffn_conv_b"], mode="geglu", x_off=0,
                            c_out=D_FF, cblk=256, out_dtype=MXU, name=pfx + "ffn_act", with_t=True)
    down = matmul(act, w[pfx + "ffn_w_down"], name=pfx + "ffn_down")
    out = rmsnorm_fwd(down, a[pfx + "ffn_post_norm"], res=h, out_dtype=F32, name=pfx + "ffn_post")
    return out, (h, ut, up, act_t, down)


def _dx_and_pair_stage(names, g, a_list, b, *, name):
    parts = [_by_dest(n, g[n]) for n in names]
    out, from_sibling = matmul_cat(a_list, b, trans_b=True, name=name, carry=pair_carry(parts))
    return out, [pair_add(p, l, name="rs_pair_add_" + n) for n, p, l in zip(names, parts, from_sibling)]


def _ffn_bwd(dh, saved, a, w, pfx, g, carry=None):
    h, ut, up, act_t, down = saved
    dd, g[pfx + "ffn_post_norm"] = rmsnorm_bwd(down, a[pfx + "ffn_post_norm"], dh, out_dtype=MXU,
                                               name=pfx + "ffn_post_bwd")
    dact = matmul(dd, w[pfx + "ffn_w_down"], trans_b=True, name=pfx + "ffn_down_dx")
    g[pfx + "ffn_w_down"] = matmul(act_t, dd, name=pfx + "ffn_down_dw")
    dups, g[pfx + "ffn_conv_w"], g[pfx + "ffn_conv_b"], carried = dwconv_bwd(
        up, a[pfx + "ffn_conv_w"], a[pfx + "ffn_conv_b"], dact, mode="geglu", x_off=0, c_out=D_FF,
        cblk=256, name=pfx + "ffn_act_bwd", carry=carry)
    g[pfx + "ffn_w_up"] = jnp.concatenate(
        [matmul(ut, d, name=pfx + "ffn_up_dw%d" % i) for i, d in enumerate(dups)], axis=1)
    du, q = _dx_and_pair_stage([pfx + "ffn_w_down", pfx + "ffn_w_up"], g, dups, w[pfx + "ffn_w_up"],
                               name=pfx + "ffn_up_dx")
    dh_in, g[pfx + "ffn_pre_norm"] = rmsnorm_bwd(h, a[pfx + "ffn_pre_norm"], du, res=dh, out_dtype=F32,
                                                 name=pfx + "ffn_pre_bwd")
    return dh_in, carried, q


GATHER_EARLY = ["l0_w_out", "l0_ffn_w_up", "l0_ffn_w_down"]
GATHER_LATE = ["l1_w_in", "l1_w_out", "l1_ffn_w_up", "l1_ffn_w_down"]
RS_L1_FFN = ["l1_ffn_w_down", "l1_ffn_w_up"]
RS_L1_MIX = ["l1_w_out", "l1_w_in"]
RS_L0_FFN = ["l0_ffn_w_down", "l0_ffn_w_up"]
RS_LAST = ["l0_w_out", "l0_w_in", "l0_lru_conv_w", "l0_ffn_conv_w", "l1_ssm_conv_w", "l1_ffn_conv_w"]


def _local_step(a, w, shards):
    x = a["x"][0]
    seq = x.shape[0]
    h0 = jnp.concatenate([jnp.zeros((PAD, D_MODEL), F32), a["meta_tokens"], x], axis=0)
    g, landed = {}, {}

    u0, u0t = rmsnorm_fwd(h0, a["l0_mix_pre_norm"], out_dtype=MXU, name="l0_mix_pre", with_t=True)
    proj0 = matmul(u0, w["l0_w_in"], name="l0_in")
    lru = (a["l0_lru_conv_w"], a["l0_lru_conv_b"], a["l0_lru_w_a"], a["l0_lru_b_a"], a["l0_lru_w_x"],
           a["l0_lru_b_x"], a["l0_lru_lambda"])
    ya, ya_t, hl, *early = lru_fwd(proj0, *lru, gate_off=0, xr_off=1024, name="l0_lru",
                                   carry=gather_carry([shards[n] for n in GATHER_EARLY]))
    yb, *late = attn_fwd(proj0, a["l0_attn_sinks"], q_off=2048, k_off=3072, v_off=3200, name="l0_attn",
                         carry=gather_carry([shards[n] for n in GATHER_LATE]))
    relayed = gather_relay(early + late, name="gather_relay")
    w = dict(w, **{n: _from_shards(n, blocks) for n, blocks in zip(GATHER_EARLY + GATHER_LATE, relayed)})
    w["l1_w_in"] = jnp.pad(w["l1_w_in"], ((0, 0), (0, L1_IN_PAD - w["l1_w_in"].shape[1])))
    o0 = matmul_cat([ya, yb], w["l0_w_out"], name="l0_out")
    h1 = rmsnorm_fwd(o0, a["l0_mix_post_norm"], res=h0, out_dtype=F32, name="l0_mix_post")
    h2, ffn0 = _ffn_fwd(h1, a, w, "l0_")

    u2, u2t = rmsnorm_fwd(h2, a["l1_mix_pre_norm"], out_dtype=MXU, name="l1_mix_pre", with_t=True)
    proj1 = matmul(u2, w["l1_w_in"], name="l1_in")
    xbc = dwconv_fwd(proj1, a["l1_ssm_conv_w"], a["l1_ssm_conv_b"], mode="silu", x_off=D_SSM,
                     c_out=2 * D_SSM, cblk=512, out_dtype=F32, name="l1_ssm_conv")
    ssd = (a["l1_dt_bias"], a["l1_a_log"], a["l1_d_skip"], a["l1_gate_norm"])
    yn, yn_t, st = ssd_fwd(xbc, proj1, *ssd, z_off=0, dt_off=3 * D_SSM, name="l1_ssd")
    o1 = matmul(yn, w["l1_w_out"], name="l1_out")
    h3 = rmsnorm_fwd(o1, a["l1_mix_post_norm"], res=h2, out_dtype=F32, name="l1_mix_post")
    h4, ffn1 = _ffn_fwd(h3, a, w, "l1_")

    loss, dh4 = loss_fwd_bwd(h4, a["loss_target"][0], name="loss")

    dh3, _, q_l1_ffn = _ffn_bwd(dh4, ffn1, a, w, "l1_", g)
    do1, g["l1_mix_post_norm"] = rmsnorm_bwd(o1, a["l1_mix_post_norm"], dh3, out_dtype=MXU,
                                             name="l1_mix_post_bwd")
    dyn = matmul(do1, w["l1_w_out"], trans_b=True, name="l1_out_dx")
    g["l1_w_out"] = matmul(yn_t, do1, name="l1_out_dw")
    (dxbc, dz, draw, g["l1_gate_norm"], g["l1_d_skip"], g["l1_dt_bias"], g["l1_a_log"], *got) = ssd_bwd(
        xbc, proj1, st, dyn, *ssd, z_off=0, dt_off=3 * D_SSM, name="l1_ssd_bwd",
        carry=chip_carry(q_l1_ffn))
    landed.update(zip(RS_L1_FFN, got))
    (dxin,), g["l1_ssm_conv_w"], g["l1_ssm_conv_b"], _ = dwconv_bwd(
        proj1, a["l1_ssm_conv_w"], a["l1_ssm_conv_b"], dxbc, mode="silu", x_off=D_SSM,
        c_out=2 * D_SSM, cblk=512, name="l1_ssm_conv_bwd")
    g["l1_w_in"] = jnp.concatenate(
        [matmul(u2t, dz, name="l1_in_dw_z"), matmul(u2t, dxin, name="l1_in_dw_x"),
         matmul(u2t, draw, name="l1_in_dw_dt")[:, :SSD_HEADS]], axis=1)
    du2, q_l1_mix = _dx_and_pair_stage(RS_L1_MIX, g, [dz, dxin, draw], w["l1_w_in"], name="l1_in_dx")
    dh2, g["l1_mix_pre_norm"] = rmsnorm_bwd(h2, a["l1_mix_pre_norm"], du2, res=dh3, out_dtype=F32,
                                            name="l1_mix_pre_bwd")

    dh1, got, q_l0_ffn = _ffn_bwd(dh2, ffn0, a, w, "l0_", g, carry=chip_carry(q_l1_mix))
    landed.update(zip(RS_L1_MIX, got))
    do0, g["l0_mix_post_norm"] = rmsnorm_bwd(o0, a["l0_mix_post_norm"], dh1, out_dtype=MXU,
                                             name="l0_mix_post_bwd")
    dy = matmul(do0, w["l0_w_out"], trans_b=True, name="l0_out_dx")
    g["l0_w_out"] = jnp.concatenate([matmul(ya_t, do0, name="l0_out_dw_a"),
                                     matmul(yb.T, do0, name="l0_out_dw_b")], axis=0)
    (dgate, dxr, g["l0_lru_conv_w"], dcb, g["l0_lru_w_a"], dba, g["l0_lru_w_x"], dbx, dlam) = lru_bwd(
        proj0, hl, dy, *lru, gate_off=0, xr_off=1024, dy_off=0, name="l0_lru_bwd")
    g["l0_lru_conv_b"], g["l0_lru_b_a"], g["l0_lru_b_x"], g["l0_lru_lambda"] = dcb[0], dba[0], dbx[0], dlam[0]
    dq, dk, dv, g["l0_attn_sinks"], *got = attn_bwd(
        proj0, a["l0_attn_sinks"], dy, q_off=2048, k_off=3072, v_off=3200, dy_off=1024, name="l0_attn_bwd",
        carry=chip_carry(q_l0_ffn))
    landed.update(zip(RS_L0_FFN, got))
    dproj0 = [dgate, dxr, dq, dk, dv]
    g["l0_w_in"] = jnp.concatenate(
        [matmul(u0t, d, name="l0_in_dw%d" % i) for i, d in enumerate(dproj0)], axis=1)
    du0, q_last = _dx_and_pair_stage(RS_LAST, g, dproj0, w["l0_w_in"], name="l0_in_dx")
    dh0, g["l0_mix_pre_norm"] = rmsnorm_bwd(h0, a["l0_mix_pre_norm"], du0, res=dh1, out_dtype=F32,
                                            name="l0_mix_pre_bwd")
    g["meta_tokens"] = dh0[PAD:BLK]
    meta = _by_dest("meta_tokens", g["meta_tokens"])
    q_meta = pair_add(meta, pair_exchange([meta], name="rs_pair_meta")[0], name="rs_pair_add_meta_tokens")
    landed.update(zip(RS_LAST + ["meta_tokens"], chip_exchange(q_last + [q_meta], name="rs_chip")))
    for n in REPLICATED:
        g[n] = g[n].reshape(a[n].shape)
    return loss[0, 0], dh0[BLK:].reshape(1, seq, D_MODEL), g, landed


def kernel(*args):
    a = dict(zip(INPUTS, args))
    first = list(SMALL_SHARDED) + ["l0_w_in"]
    got = all_gather([a[n].astype(MXU) if n in MATS else a[n] for n in first], name="gather_first")
    full = {n: _from_shards(n, blocks) for n, blocks in zip(first, got)}
    shards = {n: a[n].astype(MXU) for n in GATHER_EARLY + GATHER_LATE}
    loss_part, grad_x, g, landed = _local_step(
        {**a, **{n: full[n] for n in SMALL_SHARDED}}, {"l0_w_in": full["l0_w_in"]}, shards)
    loss = lax.psum(loss_part, ("x", "y", "c"))

    sh_out = {n: adamw(landed[n], a[n], a["m_" + n], a["v_" + n], name="adamw_" + n) for n in SHARDED}

    rp_shapes = [a[n].shape for n in REPLICATED]
    rrows = _round_up(sum(_rows_of(math.prod(s), SMALL_W) for s in rp_shapes), 128)
    gathered = all_gather([_pack([g[n] for n in REPLICATED], SMALL_W, rrows)], name="gather_small_grads")[0]
    rp_out = adamw(gathered, *[_pack([a[p + n] for n in REPLICATED], SMALL_W, rrows) for p in ("", "m_", "v_")],
                   name="adamw_replicated")
    rp_out = [dict(zip(REPLICATED, _unpack(s, rp_shapes, SMALL_W))) for s in rp_out]

    outs = [loss, grad_x]
    for k in range(4):
        outs += [sh_out[n][k] if n in SHARDED else rp_out[k][n] for n in WEIGHTS]
    return tuple(outs)
```

```python
import functools
import math

import jax
import jax.numpy as jnp
import numpy as np
from jax import lax
from jax.experimental import pallas as pl
from jax.experimental.pallas import tpu as pltpu

F32 = jnp.float32
BF16 = jnp.bfloat16
MXU = jnp.bfloat16

D_MODEL = 1024
N_META = 16
BLK = 128
PAD = BLK - N_META
D_RNN = 1024
LRU_C = 8.0
N_Q_HEADS = 16
HEAD_DIM = 64
D_SSM = 2048
SSD_HEADS = 32
SSD_GROUPS = 8
D_FF = 2816
EPS = 1e-6
NEG = -1e30
N_DEV = 8

ADAM_LR = 0.001
ADAM_B1 = 0.9
ADAM_B2 = 0.999
ADAM_EPS = 1e-08
ADAM_WD = 0.01
ADAM_STEP = 10

VMEM_LIMIT = 56 * 1024 * 1024
MESH = pl.DeviceIdType.MESH
ANY = pl.BlockSpec(memory_space=pl.ANY)


def _cp(*sem):
    return pltpu.CompilerParams(dimension_semantics=sem, vmem_limit_bytes=VMEM_LIMIT)


def _pick(n, cands):
    for c in cands:
        if n % c == 0:
            return c
    return n


def _dot(a, b):
    return jnp.dot(a.astype(MXU), b.astype(MXU), preferred_element_type=F32)


def _dot_nt(a, b):
    return lax.dot_general(a.astype(MXU), b.astype(MXU), (((1,), (1,)), ((), ())),
                           preferred_element_type=F32)


def _dot_tn(a, b):
    return jnp.dot(a.T.astype(MXU), b.astype(MXU), preferred_element_type=F32)


def _sigmoid(x):
    return 1.0 / (1.0 + jnp.exp(-x))


def _log1p(x):
    u = 1.0 + x
    return jnp.where(u == 1.0, x, jnp.log(u) * (x / jnp.where(u == 1.0, 1.0, u - 1.0)))


def _expm1(x):
    u = jnp.exp(x)
    um1 = u - 1.0
    lg = jnp.log(jnp.where(u > 0.0, u, 1.0))
    safe = (um1 != 0.0) & (um1 != -1.0)
    return jnp.where(um1 == 0.0, x, jnp.where(um1 == -1.0, -1.0,
                                               um1 * (x / jnp.where(safe, lg, 1.0))))


def _softplus(x):
    return jnp.maximum(x, 0.0) + _log1p(jnp.exp(-jnp.abs(x)))


_GC = math.sqrt(2.0 / math.pi)


def _gelu(x):
    t = jnp.tanh(_GC * (x + 0.044715 * x * x * x))
    return 0.5 * x * (1.0 + t)


def _gelu_grad(x):
    t = jnp.tanh(_GC * (x + 0.044715 * x * x * x))
    return 0.5 * (1.0 + t) + 0.5 * x * (1.0 - t * t) * (_GC * (1.0 + 3.0 * 0.044715 * x * x))


def _silu(x):
    return x * _sigmoid(x)


def _silu_grad(x):
    s = _sigmoid(x)
    return s * (1.0 + x * (1.0 - s))


def _rows(shape):
    return lax.broadcasted_iota(jnp.int32, shape, 0)


def _lanes(shape):
    return lax.broadcasted_iota(jnp.int32, shape, 1)


def _shift_down(x, tail, d):
    if d == 0:
        return x
    n = x.shape[0]
    xr = pltpu.roll(x, d, 0)
    tr = pltpu.roll(tail, d, 0)
    first = jnp.where(_rows(tr.shape) < d, tr, xr[0:8])
    return jnp.concatenate([first, xr[8:n]], axis=0)


def _shift_up(x, head, d):
    if d == 0:
        return x
    n = x.shape[0]
    xr = pltpu.roll(x, n - d, 0)
    hr = pltpu.roll(head, 8 - d, 0)
    last = jnp.where(_rows(hr.shape) >= 8 - d, hr, xr[n - 8:n])
    return jnp.concatenate([xr[0:n - 8], last], axis=0)


def _keep(x, valid, s):
    return jnp.where(valid, x, 0.0) if s == 0 else x


def _row_at(x, i):
    return jnp.sum(jnp.where(_rows(x.shape) == i, x, 0.0), axis=0, keepdims=True)


def _scan_fwd(a, u):
    n = a.shape[0]
    ri = _rows(a.shape)
    d = 1
    while d < n:
        m = ri >= d
        us = jnp.where(m, pltpu.roll(u, d, 0), 0.0)
        as_ = jnp.where(m, pltpu.roll(a, d, 0), 1.0)
        u = u + a * us
        a = a * as_
        d *= 2
    return a, u


def _scan_rev(c, u):
    n = c.shape[0]
    ri = _rows(c.shape)
    d = 1
    while d < n:
        m = ri < n - d
        us = jnp.where(m, pltpu.roll(u, n - d, 0), 0.0)
        cs = jnp.where(m, pltpu.roll(c, n - d, 0), 1.0)
        u = u + c * us
        c = c * cs
        d *= 2
    return c, u


def _cumsum_fwd(x):
    n = x.shape[0]
    ri = _rows(x.shape)
    d = 1
    while d < n:
        x = x + jnp.where(ri >= d, pltpu.roll(x, d, 0), 0.0)
        d *= 2
    return x


def _cumsum_rev(x):
    n = x.shape[0]
    ri = _rows(x.shape)
    d = 1
    while d < n:
        x = x + jnp.where(ri < n - d, pltpu.roll(x, n - d, 0), 0.0)
        d *= 2
    return x


MATMUL_VMEM = 40 * 1024 * 1024


def _matmul_tiles(m, n, k, tk, out_bytes):
    best = None
    for tm in (1664, 1408, 1040, 1024, 832, 640, 512, 384, 256, 128):
        if m % tm:
            continue
        for tn in (2048, 1664, 1408, 1024, 896, 640, 512, 384, 256, 128):
            if n % tn:
                continue
            vmem = 2 * (tm * tk * 2 + tk * tn * 2 + tm * tn * out_bytes) + (tm * tn * 4 if k > tk else 0)
            if vmem > MATMUL_VMEM:
                continue
            traffic = (n // tn) * m * k * 2 + (m // tm) * k * n * 2
            if best is None or traffic < best[0]:
                best = (traffic, tm, tn)
    return (best[1], best[2]) if best else (m, n)


def matmul(a, b, *, trans_b=False, out_dtype=F32, name):
    m, k = a.shape
    n = b.shape[0] if trans_b else b.shape[1]
    tk = k if k <= 2048 else _pick(k, (1664, 1408, 1024, 896, 512, 256, 128))
    nk = k // tk
    tm, tn = _matmul_tiles(m, n, k, tk, jnp.dtype(out_dtype).itemsize)

    def product(a_ref, b_ref):
        return _dot_nt(a_ref[...], b_ref[...]) if trans_b else _dot(a_ref[...], b_ref[...])

    def kern_once(a_ref, b_ref, o_ref):
        o_ref[...] = product(a_ref, b_ref).astype(o_ref.dtype)

    def kern_acc(a_ref, b_ref, o_ref, acc_ref):
        kk = pl.program_id(2)

        @pl.when(kk == 0)
        def _():
            acc_ref[...] = product(a_ref, b_ref)

        @pl.when(kk > 0)
        def _():
            acc_ref[...] += product(a_ref, b_ref)

        @pl.when(kk == nk - 1)
        def _():
            o_ref[...] = acc_ref[...].astype(o_ref.dtype)

    b_spec = (pl.BlockSpec((tn, tk), lambda i, j, kk: (j, kk)) if trans_b
              else pl.BlockSpec((tk, tn), lambda i, j, kk: (kk, j)))
    return pl.pallas_call(
        kern_once if nk == 1 else kern_acc, name=name,
        grid=(m // tm, n // tn, nk),
        in_specs=[pl.BlockSpec((tm, tk), lambda i, j, kk: (i, kk)), b_spec],
        out_specs=pl.BlockSpec((tm, tn), lambda i, j, kk: (i, j)),
        out_shape=jax.ShapeDtypeStruct((m, n), out_dtype),
        scratch_shapes=[] if nk == 1 else [pltpu.VMEM((tm, tn), F32)],
        compiler_params=_cp("parallel", "parallel", "arbitrary"),
    )(a, b)


def matmul_cat(a_list, b, *, trans_b=False, out_dtype=F32, name, carry=None):
    m = a_list[0].shape[0]
    ks = [x.shape[1] for x in a_list]
    ktot = sum(ks)
    n = b.shape[0] if trans_b else b.shape[1]
    tn = _pick(n, (512, 256, 128))
    tm = next((c for c in (1664, 1040, 832, 640, 512, 384, 256, 128)
               if m % c == 0 and c * ktot * 2 <= 8 * 1024 * 1024), m)
    na = len(a_list)

    def kern(*refs):
        b_ref, o_ref = refs[na], refs[na + 1]
        acc, off = None, 0
        for a_ref, k in zip(refs[:na], ks):
            if trans_b:
                part = _dot_nt(a_ref[...], b_ref[:, off:off + k])
            else:
                part = _dot(a_ref[...], b_ref[off:off + k, :])
            acc = part if acc is None else acc + part
            off += k
        o_ref[...] = acc.astype(o_ref.dtype)

    b_spec = (pl.BlockSpec((tn, ktot), lambda i, j: (j, 0)) if trans_b
              else pl.BlockSpec((ktot, tn), lambda i, j: (0, j)))
    res = _call(
        kern, (*a_list, b), name=name, grid=(m // tm, n // tn),
        in_specs=[pl.BlockSpec((tm, k), lambda i, j: (i, 0)) for k in ks] + [b_spec],
        out_specs=[pl.BlockSpec((tm, tn), lambda i, j: (i, j))],
        out_shape=[jax.ShapeDtypeStruct((m, n), out_dtype)],
        sem=("parallel", "parallel"), carry=carry)
    return res[0] if carry is None else (res[0], res[1:])


def _row_tile(t):
    return _pick(t, (832, 640, 512, 384, 256, 128))


def rmsnorm_fwd(x, w, res=None, *, out_dtype, name, with_t=False):
    t, d = x.shape
    tr = _conv_tile(t) if with_t else _row_tile(t)

    def kern(*refs):
        x_ref, w_ref = refs[0], refs[1]
        o_ref = refs[-2] if with_t else refs[-1]
        xv = x_ref[...]
        r = lax.rsqrt(jnp.mean(xv * xv, axis=-1, keepdims=True) + EPS)
        y = xv * r * w_ref[...]
        if res is not None:
            y = refs[2][...] + y
        o_ref[...] = y.astype(o_ref.dtype)
        if with_t:
            refs[-1][...] = y.T.astype(o_ref.dtype)

    row = pl.BlockSpec((tr, d), lambda i: (i, 0))
    vec = pl.BlockSpec((1, d), lambda i: (0, 0))
    ins = [x, w.reshape(1, d)] + ([] if res is None else [res])
    specs = [row, vec] + ([] if res is None else [row])
    out_specs, out_shape = row, jax.ShapeDtypeStruct((t, d), out_dtype)
    if with_t:
        out_specs = [row, pl.BlockSpec((d, tr), lambda i: (0, i))]
        out_shape = [out_shape, jax.ShapeDtypeStruct((d, t), out_dtype)]
    return pl.pallas_call(
        kern, name=name, grid=(t // tr,), in_specs=specs, out_specs=out_specs, out_shape=out_shape,
        compiler_params=_cp("parallel"),
    )(*ins)


def rmsnorm_bwd(x, w, dy, res=None, *, out_dtype, name):
    t, d = x.shape
    tr = _row_tile(t)

    def kern(*refs):
        if res is None:
            x_ref, w_ref, dy_ref, dx_ref, dw_ref = refs
        else:
            x_ref, w_ref, dy_ref, r_ref, dx_ref, dw_ref = refs
        i = pl.program_id(0)
        xv = x_ref[...]
        dyv = dy_ref[...].astype(F32)
        r = lax.rsqrt(jnp.mean(xv * xv, axis=-1, keepdims=True) + EPS)
        xh = xv * r
        g = dyv * w_ref[...]
        dx = r * (g - xh * jnp.mean(g * xh, axis=-1, keepdims=True))
        if res is not None:
            dx = r_ref[...] + dx
        dx_ref[...] = dx.astype(dx_ref.dtype)
        part = jnp.sum(dyv * xh, axis=0, keepdims=True)

        @pl.when(i == 0)
        def _():
            dw_ref[...] = part

        @pl.when(i > 0)
        def _():
            dw_ref[...] += part

    row = pl.BlockSpec((tr, d), lambda i: (i, 0))
    vec = pl.BlockSpec((1, d), lambda i: (0, 0))
    ins = [x, w.reshape(1, d), dy] + ([] if res is None else [res])
    specs = [row, vec, row] + ([] if res is None else [row])
    return pl.pallas_call(
        kern, name=name, grid=(t // tr,), in_specs=specs, out_specs=[row, vec],
        out_shape=[jax.ShapeDtypeStruct((t, d), out_dtype), jax.ShapeDtypeStruct((1, d), F32)],
        compiler_params=_cp("arbitrary"),
    )(*ins)


def _conv_tile(t):
    return _pick(t, (640, 384, 256, 128))


def _conv_apply(x, tail, cw, cb, ksz):
    y = cb
    for k in range(ksz):
        y = y + cw[k:k + 1, :] * _shift_down(x, tail, ksz - 1 - k)
    return y


def dwconv_fwd(x, cw, cb, *, mode, x_off, c_out, cblk, out_dtype, name, with_t=False):
    t = x.shape[0]
    ksz = cw.shape[0]
    tb = _conv_tile(t)
    nb, ncb, t8 = t // tb, c_out // cblk, tb // 8
    xo = x_off // cblk
    nin = 2 if mode == "geglu" else 1

    def kern(*refs):
        o_ref = refs[-2] if with_t else refs[-1]
        n = pl.program_id(1)
        for c in range(cblk // BLK):
            ls = slice(c * BLK, (c + 1) * BLK)
            for s in range(tb // BLK):
                rs = slice(s * BLK, (s + 1) * BLK)
                valid = (n * tb + s * BLK + _rows((BLK, BLK))) >= PAD
                hs = []
                for q in range(nin):
                    x_ref, t_ref, w_ref, b_ref = refs[4 * q:4 * q + 4]
                    tail = (jnp.where(n > 0, t_ref[:, ls], 0.0) if s == 0
                            else x_ref[s * BLK - 8:s * BLK, ls])
                    hs.append(_conv_apply(x_ref[rs, ls], tail, w_ref[:, ls], b_ref[:, ls], ksz))
                y = _gelu(hs[0]) * hs[1] if mode == "geglu" else _silu(hs[0])
                y = _keep(y, valid, s)
                o_ref[rs, ls] = y.astype(o_ref.dtype)
                if with_t:
                    refs[-1][ls, rs] = y.T.astype(o_ref.dtype)

    ins, specs = [], []
    for q in range(nin):
        co = xo + q * ncb
        wo = q * ncb
        ins += [x, x, cw, cb.reshape(1, -1)]
        specs += [
            pl.BlockSpec((tb, cblk), lambda j, n, co=co: (n, co + j)),
            pl.BlockSpec((8, cblk), lambda j, n, co=co: (jnp.maximum(n * t8 - 1, 0), co + j)),
            pl.BlockSpec((ksz, cblk), lambda j, n, wo=wo: (0, wo + j)),
            pl.BlockSpec((1, cblk), lambda j, n, wo=wo: (0, wo + j)),
        ]
    out_specs = pl.BlockSpec((tb, cblk), lambda j, n: (n, j))
    out_shape = jax.ShapeDtypeStruct((t, c_out), out_dtype)
    if with_t:
        out_specs = [out_specs, pl.BlockSpec((cblk, tb), lambda j, n: (j, n))]
        out_shape = [out_shape, jax.ShapeDtypeStruct((c_out, t), out_dtype)]
    return pl.pallas_call(
        kern, name=name, grid=(ncb, nb), in_specs=specs, out_specs=out_specs, out_shape=out_shape,
        compiler_params=_cp("parallel", "parallel"),
    )(*ins)


def dwconv_bwd(x, cw, cb, dy, *, mode, x_off, c_out, cblk, name, carry=None):
    t = x.shape[0]
    ksz = cw.shape[0]
    tb = _conv_tile(t)
    nb, ncb, t8 = t // tb, c_out // cblk, tb // 8
    xo = x_off // cblk
    nin = 2 if mode == "geglu" else 1
    ctot = nin * c_out

    def kern(*refs):
        dy_ref = refs[4 * nin]
        outs = refs[4 * nin + 1:4 * nin + 1 + 3 * nin]
        heads = refs[4 * nin + 1 + 3 * nin:]
        n = pl.program_id(1)
        blk = nb - 1 - n

        @pl.when(n == 0)
        def _():
            for q in range(nin):
                heads[q][...] = jnp.zeros_like(heads[q])
                outs[3 * q + 1][...] = jnp.zeros_like(outs[3 * q + 1])
                outs[3 * q + 2][...] = jnp.zeros_like(outs[3 * q + 2])

        for c in range(cblk // BLK):
            ls = slice(c * BLK, (c + 1) * BLK)
            head = [heads[q][:, ls] for q in range(nin)]
            dwa = [[None] * ksz for _ in range(nin)]
            dba = [None] * nin
            for s in reversed(range(tb // BLK)):
                rs = slice(s * BLK, (s + 1) * BLK)
                valid = (blk * tb + s * BLK + _rows((BLK, BLK))) >= PAD
                xs, tails, hs = [], [], []
                for q in range(nin):
                    x_ref, t_ref, w_ref, b_ref = refs[4 * q:4 * q + 4]
                    tail = (jnp.where(blk > 0, t_ref[:, ls], 0.0) if s == 0
                            else x_ref[s * BLK - 8:s * BLK, ls])
                    xs.append(x_ref[rs, ls])
                    tails.append(tail)
                    hs.append(_conv_apply(xs[q], tail, w_ref[:, ls], b_ref[:, ls], ksz))
                dyv = dy_ref[rs, ls].astype(F32)
                if mode == "geglu":
                    dhs = [dyv * hs[1] * _gelu_grad(hs[0]), dyv * _gelu(hs[0])]
                else:
                    dhs = [dyv * _silu_grad(hs[0])]
                for q in range(nin):
                    w_ref = refs[4 * q + 2]
                    dh = _keep(dhs[q], valid, s)
                    dx = jnp.zeros_like(dh)
                    for k in range(ksz):
                        sh = ksz - 1 - k
                        dx = dx + w_ref[k:k + 1, ls] * _shift_up(dh, head[q], sh)
                        part = jnp.sum(dh * _shift_down(xs[q], tails[q], sh), axis=0, keepdims=True)
                        dwa[q][k] = part if dwa[q][k] is None else dwa[q][k] + part
                    outs[3 * q][rs, ls] = _keep(dx, valid, s).astype(outs[3 * q].dtype)
                    part = jnp.sum(dh, axis=0, keepdims=True)
                    dba[q] = part if dba[q] is None else dba[q] + part
                    head[q] = dh[0:8]
            for q in range(nin):
                outs[3 * q + 1][:, ls] += jnp.concatenate(dwa[q], axis=0)
                outs[3 * q + 2][:, ls] += dba[q]
                heads[q][:, ls] = head[q]

    ins, specs, out_specs, out_shape, scratch = [], [], [], [], []
    for q in range(nin):
        co = xo + q * ncb
        wo = q * ncb
        ins += [x, x, cw, cb.reshape(1, -1)]
        specs += [
            pl.BlockSpec((tb, cblk), lambda j, n, co=co: (nb - 1 - n, co + j)),
            pl.BlockSpec((8, cblk), lambda j, n, co=co: (jnp.maximum((nb - 1 - n) * t8 - 1, 0), co + j)),
            pl.BlockSpec((ksz, cblk), lambda j, n, wo=wo: (0, wo + j)),
            pl.BlockSpec((1, cblk), lambda j, n, wo=wo: (0, wo + j)),
        ]
        out_specs += [
            pl.BlockSpec((tb, cblk), lambda j, n: (nb - 1 - n, j)),
            pl.BlockSpec((ksz, cblk), lambda j, n: (0, j)),
            pl.BlockSpec((1, cblk), lambda j, n: (0, j)),
        ]
        out_shape += [jax.ShapeDtypeStruct((t, c_out), MXU),
                      jax.ShapeDtypeStruct((ksz, c_out), F32),
                      jax.ShapeDtypeStruct((1, c_out), F32)]
        scratch.append(pltpu.VMEM((8, cblk), F32))
    ins.append(dy)
    specs.append(pl.BlockSpec((tb, cblk), lambda j, n: (nb - 1 - n, j)))
    res = _call(kern, ins, name=name, grid=(ncb, nb), in_specs=specs, out_specs=out_specs,
                out_shape=out_shape, scratch_shapes=scratch, sem=("parallel", "arbitrary"), carry=carry)
    dxs = [res[3 * q] for q in range(nin)]
    dcw = jnp.concatenate([res[3 * q + 1] for q in range(nin)], axis=1)
    dcb = jnp.concatenate([res[3 * q + 2] for q in range(nin)], axis=1)
    return dxs, dcw, dcb.reshape(ctot), res[3 * nin:]


def _lru_tile(t):
    return _pick(t, (640, 384, 256, 128))


def _lru_gates(xc, wa, ba, wx, bx, sp):
    r = _sigmoid(_dot(xc, wa) + ba)
    i = _sigmoid(_dot(xc, wx) + bx)
    log_a = -LRU_C * r * sp
    a = jnp.exp(log_a)
    mult = jnp.sqrt(-_expm1(2.0 * log_a))
    return r, i, a, mult


def lru_fwd(proj, cw, cb, wa, ba, wx, bx, lam, *, gate_off, xr_off, name, carry=None):
    t = proj.shape[0]
    tb = _lru_tile(t)
    nb, ns, t8 = t // tb, tb // BLK, tb // 8
    go, xo = gate_off // BLK, xr_off // BLK

    def kern(g_ref, x_ref, xt_ref, cw_ref, cb_ref, wa_ref, ba_ref, wx_ref, bx_ref, lam_ref,
             y_ref, yt_ref, h_ref, hc_ref):
        n = pl.program_id(1)

        @pl.when(n == 0)
        def _():
            hc_ref[...] = jnp.zeros_like(hc_ref)

        sp = _softplus(-lam_ref[...])
        hprev = hc_ref[0:1, :]
        scans = []
        for s in range(ns):
            sl = slice(s * BLK, (s + 1) * BLK)
            xv = x_ref[sl, :]
            tail = jnp.where(n > 0, xt_ref[...], 0.0) if s == 0 else x_ref[s * BLK - 8:s * BLK, :]
            valid = (n * tb + s * BLK + _rows((BLK, BLK))) >= PAD
            xc = _keep(_conv_apply(xv, tail, cw_ref[...], cb_ref[...], 4), valid, s)
            _, i, a, mult = _lru_gates(xc, wa_ref[0], ba_ref[...], wx_ref[0], bx_ref[...], sp)
            scans.append(_scan_fwd(a, mult * (i * xc)))
        for s in range(ns):
            sl = slice(s * BLK, (s + 1) * BLK)
            ca, cu = scans[s]
            h = cu + ca * hprev
            hprev = _row_at(h, BLK - 1)
            h_ref[sl, :] = h
            y = _gelu(g_ref[sl, :]) * h
            y_ref[sl, :] = y.astype(y_ref.dtype)
            yt_ref[:, sl] = y.T.astype(yt_ref.dtype)
        hc_ref[...] = jnp.broadcast_to(hprev, hc_ref.shape)

    vec = pl.BlockSpec((1, BLK), lambda j, n: (0, j))
    mat = pl.BlockSpec((1, BLK, BLK), lambda j, n: (j, 0, 0))
    return _call(
        kern, (proj, proj, proj, cw, cb.reshape(1, -1), wa, ba.reshape(1, -1), wx, bx.reshape(1, -1),
               lam.reshape(1, -1)),
        name=name, grid=(D_RNN // BLK, nb),
        in_specs=[
            pl.BlockSpec((tb, BLK), lambda j, n: (n, go + j)),
            pl.BlockSpec((tb, BLK), lambda j, n: (n, xo + j)),
            pl.BlockSpec((8, BLK), lambda j, n: (jnp.maximum(n * t8 - 1, 0), xo + j)),
            pl.BlockSpec((4, BLK), lambda j, n: (0, j)), vec, mat, vec, mat, vec, vec,
        ],
        out_specs=[pl.BlockSpec((tb, BLK), lambda j, n: (n, j)),
                   pl.BlockSpec((BLK, tb), lambda j, n: (j, n)),
                   pl.BlockSpec((tb, BLK), lambda j, n: (n, j))],
        out_shape=[jax.ShapeDtypeStruct((t, D_RNN), MXU), jax.ShapeDtypeStruct((D_RNN, t), MXU),
                   jax.ShapeDtypeStruct((t, D_RNN), F32)],
        scratch_shapes=[pltpu.VMEM((8, BLK), F32)],
        sem=("parallel", "arbitrary"), carry=carry)


def lru_bwd(proj, h, dy, cw, cb, wa, ba, wx, bx, lam, *, gate_off, xr_off, dy_off, name):
    t = proj.shape[0]
    tb = _lru_tile(t)
    nb, ns, t8 = t // tb, tb // BLK, tb // 8
    go, xo, do = gate_off // BLK, xr_off // BLK, dy_off // BLK

    def kern(g_ref, x_ref, xt_ref, h_ref, ht_ref, dy_ref, cw_ref, cb_ref, wa_ref, ba_ref,
             wx_ref, bx_ref, lam_ref,
             dg_ref, dx_ref, dcw_ref, dcb_ref, dwa_ref, dba_ref, dwx_ref, dbx_ref, dlam_ref,
             gin_ref, head_ref):
        n = pl.program_id(1)
        blk = nb - 1 - n

        @pl.when(n == 0)
        def _():
            gin_ref[...] = jnp.zeros_like(gin_ref)
            head_ref[...] = jnp.zeros_like(head_ref)
            for r_ in (dcw_ref, dcb_ref, dwa_ref, dba_ref, dwx_ref, dbx_ref, dlam_ref):
                r_[...] = jnp.zeros_like(r_)

        lamv = lam_ref[...]
        sp = _softplus(-lamv)
        dsp_dlam = -_sigmoid(-lamv)
        g_in = gin_ref[0:1, :]
        head = head_ref[...]
        ones8 = jnp.ones((8, BLK), F32)
        wav, wxv = wa_ref[0], wx_ref[0]
        staged = {}
        for s in range(ns):
            sl = slice(s * BLK, (s + 1) * BLK)
            xv = x_ref[sl, :]
            if s == 0:
                tail = jnp.where(blk > 0, xt_ref[...], 0.0)
                htail = jnp.where(blk > 0, ht_ref[...], 0.0)
            else:
                tail = x_ref[s * BLK - 8:s * BLK, :]
                htail = h_ref[s * BLK - 8:s * BLK, :]
            valid = (blk * tb + s * BLK + _rows((BLK, BLK))) >= PAD
            xc = _keep(_conv_apply(xv, tail, cw_ref[...], cb_ref[...], 4), valid, s)
            r, i, a, mult = _lru_gates(xc, wav, ba_ref[...], wxv, bx_ref[...], sp)
            hv = h_ref[sl, :]
            hprev = _shift_down(hv, htail, 1)
            gv = g_ref[sl, :]
            dyv = dy_ref[sl, :].astype(F32)
            dg_ref[sl, :] = (dyv * hv * _gelu_grad(gv)).astype(dg_ref.dtype)
            cc, cu = _scan_rev(_shift_up(a, ones8, 1), dyv * _gelu(gv))
            staged[s] = (xv, tail, valid, xc, r, i, a, mult, hprev, cc, cu)
        for s in reversed(range(ns)):
            sl = slice(s * BLK, (s + 1) * BLK)
            xv, tail, valid, xc, r, i, a, mult, hprev, cc, cu = staged[s]
            gg = cu + cc * g_in
            g_in = _row_at(a * gg, 0)
            da = gg * hprev
            di = gg * mult * xc
            dxc = gg * mult * i
            dmult = gg * i * xc
            dlog_a = da * a - dmult * (a * a) / mult
            dr = dlog_a * (-LRU_C * sp)
            dlam_ref[...] += jnp.sum(dlog_a * (-LRU_C) * r, axis=0, keepdims=True) * dsp_dlam
            dpr = dr * r * (1.0 - r)
            dpi = di * i * (1.0 - i)
            dxc = dxc + _dot_nt(dpr, wav) + _dot_nt(dpi, wxv)
            dxc, dpr, dpi = _keep(dxc, valid, s), _keep(dpr, valid, s), _keep(dpi, valid, s)
            dwa_ref[0] += _dot_tn(xc, dpr)
            dwx_ref[0] += _dot_tn(xc, dpi)
            dba_ref[...] += jnp.sum(dpr, axis=0, keepdims=True)
            dbx_ref[...] += jnp.sum(dpi, axis=0, keepdims=True)
            dx = jnp.zeros_like(dxc)
            dws = []
            for k in range(4):
                dx = dx + cw_ref[k:k + 1, :] * _shift_up(dxc, head, 3 - k)
                dws.append(jnp.sum(dxc * _shift_down(xv, tail, 3 - k), axis=0, keepdims=True))
            dx_ref[sl, :] = _keep(dx, valid, s).astype(dx_ref.dtype)
            dcw_ref[...] += jnp.concatenate(dws, axis=0)
            dcb_ref[...] += jnp.sum(dxc, axis=0, keepdims=True)
            head = dxc[0:8]
        gin_ref[...] = jnp.broadcast_to(g_in, gin_ref.shape)
        head_ref[...] = head

    vec = pl.BlockSpec((1, BLK), lambda j, n: (0, j))
    mat = pl.BlockSpec((1, BLK, BLK), lambda j, n: (j, 0, 0))
    cws = pl.BlockSpec((4, BLK), lambda j, n: (0, j))

    def rb(off):
        return pl.BlockSpec((tb, BLK), lambda j, n: (nb - 1 - n, off + j))

    def tl(off):
        return pl.BlockSpec((8, BLK), lambda j, n: (jnp.maximum((nb - 1 - n) * t8 - 1, 0), off + j))

    return pl.pallas_call(
        kern, name=name, grid=(D_RNN // BLK, nb),
        in_specs=[rb(go), rb(xo), tl(xo), rb(0), tl(0), rb(do), cws, vec, mat, vec, mat, vec, vec],
        out_specs=[rb(0), rb(0), cws, vec, mat, vec, mat, vec, vec],
        out_shape=[jax.ShapeDtypeStruct((t, D_RNN), MXU), jax.ShapeDtypeStruct((t, D_RNN), MXU),
                   jax.ShapeDtypeStruct((4, D_RNN), F32), jax.ShapeDtypeStruct((1, D_RNN), F32),
                   jax.ShapeDtypeStruct((8, BLK, BLK), F32), jax.ShapeDtypeStruct((1, D_RNN), F32),
                   jax.ShapeDtypeStruct((8, BLK, BLK), F32), jax.ShapeDtypeStruct((1, D_RNN), F32),
                   jax.ShapeDtypeStruct((1, D_RNN), F32)],
        scratch_shapes=[pltpu.VMEM((8, BLK), F32), pltpu.VMEM((8, BLK), F32)],
        compiler_params=_cp("parallel", "arbitrary"),
    )(proj, proj, proj, h, h, dy, cw, cb.reshape(1, -1), wa, ba.reshape(1, -1), wx,
      bx.reshape(1, -1), lam.reshape(1, -1))


_SCALE = HEAD_DIM ** -0.5


STK = 4


def _attn_masks(n):
    qi = np.arange(STK * BLK)[:, None] % BLK
    c = np.arange(3 * BLK)[None, :]
    tq = n * BLK + qi - PAD
    s_band = (n - 1) * BLK + c - PAD
    d_band = tq - s_band
    ok_band = (s_band >= N_META) & (d_band >= 0) & (d_band < BLK)
    jm = c - 2 * BLK
    d_meta = tq - (jm - PAD)
    ok_meta = (jm >= PAD) & (d_meta >= 0)
    is_band = c < 2 * BLK
    ok = np.where(is_band, ok_band, ok_meta)
    dist = np.where(is_band, d_band, np.minimum(d_meta, BLK)).astype(np.float32)
    return ok, dist


def _stack_heads(g, e):
    return [8 * g + 2 * i + e for i in range(STK)]


def _attn_bias_table():
    tabs = []
    for n in range(3):
        ok, dist = _attn_masks(n)
        per = []
        for g in range(2):
            for e in range(2):
                slope = np.repeat(np.array([2.0 ** (-8.0 * (h + 1) / N_Q_HEADS) for h in _stack_heads(g, e)],
                                           np.float32), BLK)[:, None]
                per.append(np.where(ok, -(slope * dist), np.float32(NEG)).astype(np.float32))
        tabs.append(np.stack(per))
    return jnp.asarray(np.stack(tabs))


def _stack_sinks(heads, sk):
    return jnp.concatenate(
        [jnp.broadcast_to(jnp.sum(jnp.where(_lanes(sk.shape) == h, sk, 0.0), axis=1, keepdims=True),
                          (BLK, 1)) for h in heads], axis=0)


def _stack_tiles(ref, g, sel):
    return jnp.concatenate(
        [jnp.where(sel, ref[:, (4 * g + i) * BLK:(4 * g + i + 1) * BLK].astype(F32), 0.0)
         for i in range(STK)], axis=0)


def _attn_probs(qk, bias, sink):
    s = qk * _SCALE + bias
    mx = jnp.maximum(jnp.max(s, axis=-1, keepdims=True), sink)
    p = jnp.exp(s - mx)
    es = jnp.exp(sink - mx)
    inv = 1.0 / (jnp.sum(p, axis=-1, keepdims=True) + es)
    return p * inv, es * inv


def _attn_specs(t, q_off, k_off, v_off, rev):
    nb = t // BLK
    qo, ko, vo = q_off // 1024, k_off // BLK, v_off // BLK

    def b(n):
        return nb - 1 - n if rev else n

    return [
        pl.BlockSpec((BLK, 1024), lambda n: (b(n), qo)),
        pl.BlockSpec((BLK, BLK), lambda n: (b(n), ko)),
        pl.BlockSpec((BLK, BLK), lambda n: (b(n), vo)),
        pl.BlockSpec((BLK, BLK), lambda n: (jnp.maximum(b(n) - 1, 0), ko)),
        pl.BlockSpec((BLK, BLK), lambda n: (jnp.maximum(b(n) - 1, 0), vo)),
        pl.BlockSpec((BLK, BLK), lambda n: (0, ko)),
        pl.BlockSpec((BLK, BLK), lambda n: (0, vo)),
        pl.BlockSpec((1, BLK), lambda n: (0, 0)),
        pl.BlockSpec((1, 4, STK * BLK, 3 * BLK), lambda n: (jnp.minimum(b(n), 2), 0, 0, 0)),
    ]


def attn_fwd(proj, sinks, *, q_off, k_off, v_off, name, carry=None):
    t = proj.shape[0]
    nb = t // BLK

    def kern(q_ref, kc_ref, vc_ref, kp_ref, vp_ref, km_ref, vm_ref, sk_ref, tab_ref, o_ref):
        k_all = jnp.concatenate([kp_ref[...], kc_ref[...], km_ref[...]], axis=0)
        v_all = jnp.concatenate([vp_ref[...], vc_ref[...], vm_ref[...]], axis=0)
        k_alt = pltpu.roll(k_all, HEAD_DIM, 1)
        v_alt = pltpu.roll(v_all, HEAD_DIM, 1)
        low = _lanes((BLK, BLK)) < HEAD_DIM
        stacks = [(g, e) for g in range(2) for e in range(2)]
        qk = {(g, e): _dot_nt(_stack_tiles(q_ref, g, low == (e == 0)), k_all if g == e else k_alt)
              for g, e in stacks}
        ps = {(g, e): _attn_probs(qk[g, e], tab_ref[0, 2 * g + e],
                                  _stack_sinks(_stack_heads(g, e), sk_ref[...]))[0] for g, e in stacks}
        outs = {(g, e): _dot(ps[g, e], v_all if g == e else v_alt) for g, e in stacks}
        for hp in range(N_Q_HEADS // 2):
            g, rs = hp // STK, slice((hp % STK) * BLK, (hp % STK + 1) * BLK)
            o_ref[:, hp * BLK:(hp + 1) * BLK] = jnp.where(low, outs[g, 0][rs], outs[g, 1][rs]).astype(o_ref.dtype)

    sk = jnp.zeros((1, BLK), F32).at[0, :N_Q_HEADS].set(sinks)
    return _call(
        kern, (proj, proj, proj, proj, proj, proj, proj, sk, _attn_bias_table()), name=name, grid=(nb,),
        in_specs=_attn_specs(t, q_off, k_off, v_off, False),
        out_specs=[pl.BlockSpec((BLK, 1024), lambda n: (n, 0))],
        out_shape=[jax.ShapeDtypeStruct((t, 1024), MXU)],
        sem=("parallel",), carry=carry)


def attn_bwd(proj, sinks, dy, *, q_off, k_off, v_off, dy_off, name, carry=None):
    t = proj.shape[0]
    nb = t // BLK
    do = dy_off // 1024

    def kern(q_ref, kc_ref, vc_ref, kp_ref, vp_ref, km_ref, vm_ref, sk_ref, tab_ref, do_ref,
             dq_ref, dk_ref, dv_ref, dsk_ref, ck_ref, cv_ref, mk_ref, mv_ref):
        n = pl.program_id(0)
        blk = nb - 1 - n

        @pl.when(n == 0)
        def _():
            for r_ in (ck_ref, cv_ref, mk_ref, mv_ref, dsk_ref):
                r_[...] = jnp.zeros_like(r_)

        k_all = jnp.concatenate([kp_ref[...], kc_ref[...], km_ref[...]], axis=0)
        v_all = jnp.concatenate([vp_ref[...], vc_ref[...], vm_ref[...]], axis=0)
        k_alt = pltpu.roll(k_all, HEAD_DIM, 1)
        v_alt = pltpu.roll(v_all, HEAD_DIM, 1)
        low = _lanes((BLK, BLK)) < HEAD_DIM
        lane1 = _lanes((1, BLK))
        dk_all = jnp.zeros((3 * BLK, BLK), F32)
        dv_all = jnp.zeros((3 * BLK, BLK), F32)
        dsk = jnp.zeros((1, BLK), F32)
        stacks = [(g, e) for g in range(2) for e in range(2)]
        qm = {(g, e): _stack_tiles(q_ref, g, low == (e == 0)) for g, e in stacks}
        dom = {(g, e): _stack_tiles(do_ref, g, low == (e == 0)) for g, e in stacks}
        qk = {(g, e): _dot_nt(qm[g, e], k_all if g == e else k_alt) for g, e in stacks}
        dp = {(g, e): _dot_nt(dom[g, e], v_all if g == e else v_alt) for g, e in stacks}
        ps, dss = {}, {}
        for g, e in stacks:
            heads = _stack_heads(g, e)
            p, psink = _attn_probs(qk[g, e], tab_ref[0, 2 * g + e], _stack_sinks(heads, sk_ref[...]))
            delta = jnp.sum(p * dp[g, e], axis=-1, keepdims=True)
            ps[g, e] = p
            dss[g, e] = p * (dp[g, e] - delta) * _SCALE
            psd = psink * delta
            for i, h in enumerate(heads):
                dsk = dsk + jnp.where(lane1 == h, -jnp.sum(psd[i * BLK:(i + 1) * BLK], axis=0, keepdims=True), 0.0)
        dqs = {(g, e): _dot(dss[g, e], k_all if g == e else k_alt) for g, e in stacks}
        for g, e in stacks:
            dkh = _dot_tn(dss[g, e], qm[g, e])
            dvh = _dot_tn(ps[g, e], dom[g, e])
            if g != e:
                dkh = pltpu.roll(dkh, HEAD_DIM, 1)
                dvh = pltpu.roll(dvh, HEAD_DIM, 1)
            dk_all = dk_all + dkh
            dv_all = dv_all + dvh
        for hp in range(N_Q_HEADS // 2):
            g, rs = hp // STK, slice((hp % STK) * BLK, (hp % STK + 1) * BLK)
            dq_ref[:, hp * BLK:(hp + 1) * BLK] = jnp.where(low, dqs[g, 0][rs], dqs[g, 1][rs]).astype(dq_ref.dtype)
        dsk_ref[...] += dsk
        mk_ref[...] += dk_all[2 * BLK:3 * BLK]
        mv_ref[...] += dv_all[2 * BLK:3 * BLK]
        is0 = blk == 0
        dk_ref[...] = (dk_all[BLK:2 * BLK] + ck_ref[...] + jnp.where(is0, mk_ref[...], 0.0)).astype(dk_ref.dtype)
        dv_ref[...] = (dv_all[BLK:2 * BLK] + cv_ref[...] + jnp.where(is0, mv_ref[...], 0.0)).astype(dv_ref.dtype)
        ck_ref[...] = dk_all[0:BLK]
        cv_ref[...] = dv_all[0:BLK]

    sk = jnp.zeros((1, BLK), F32).at[0, :N_Q_HEADS].set(sinks)
    kv = pl.BlockSpec((BLK, BLK), lambda n: (nb - 1 - n, 0))
    res = _call(
        kern, (proj, proj, proj, proj, proj, proj, proj, sk, _attn_bias_table(), dy), name=name, grid=(nb,),
        in_specs=_attn_specs(t, q_off, k_off, v_off, True)
        + [pl.BlockSpec((BLK, 1024), lambda n: (nb - 1 - n, do))],
        out_specs=[pl.BlockSpec((BLK, 1024), lambda n: (nb - 1 - n, 0)), kv, kv,
                   pl.BlockSpec((1, BLK), lambda n: (0, 0))],
        out_shape=[jax.ShapeDtypeStruct((t, 1024), MXU), jax.ShapeDtypeStruct((t, BLK), MXU),
                   jax.ShapeDtypeStruct((t, BLK), MXU), jax.ShapeDtypeStruct((1, BLK), F32)],
        scratch_shapes=[pltpu.VMEM((BLK, BLK), F32)] * 4,
        sem=("arbitrary",), carry=carry)
    return [res[0], res[1], res[2], res[3][0, :N_Q_HEADS]] + res[4:]


GW = D_SSM // SSD_GROUPS
EXP_ROWS = 3 * BLK + 8
RED_ROWS = EXP_ROWS + 8


def _head_expand():
    ch = jnp.arange(D_SSM) // HEAD_DIM
    return (jnp.arange(BLK)[:, None] == ch[None, :]).astype(BF16)


def _ssd_decay(raw, dtb, alog, rowv):
    valid = rowv & (_lanes((BLK, BLK)) < SSD_HEADS)
    pre = raw + dtb
    dtp = jnp.where(valid, _softplus(pre), 0.0)
    av = -jnp.exp(alog)
    cs = _cumsum_fwd(dtp * av)
    cs_last = _row_at(cs, BLK - 1)
    return valid, pre, dtp, av, cs, jnp.exp(cs), jnp.exp(cs_last - cs), jnp.exp(cs_last)


def _head_col(x, h):
    return jnp.sum(jnp.where(_lanes(x.shape) == h, x, 0.0), axis=1, keepdims=True)


def _ssd_group_fwd(g, xdt, cs, cst, cb, tril, low):
    lm = []
    for k in range(4):
        h = 4 * g + k
        seg = _head_col(cs, h) - _row_at(cst, h)
        lmat = jnp.where(tril, jnp.exp(jnp.minimum(seg, 0.0)), 0.0)
        lm.append((lmat, cb * lmat))
    hv = [_dot(lm[k][1], xdt[:, g * GW + (k // 2) * BLK:g * GW + (k // 2 + 1) * BLK]) for k in range(4)]
    return jnp.concatenate([jnp.where(low, hv[0], hv[1]), jnp.where(low, hv[2], hv[3])], axis=1), lm


def ssd_decay(proj, dt_bias, a_log, *, dt_off, name):
    t = proj.shape[0]
    tb = _conv_tile(t)
    dto = dt_off // BLK

    def kern(dt_ref, dtb_ref, alog_ref, o_ref):
        n = pl.program_id(0)
        for s in range(tb // BLK):
            rs = slice(s * BLK, (s + 1) * BLK)
            rowv = (n * tb + s * BLK + _rows((BLK, BLK))) >= PAD
            _, _, dtp, _, cs, ecs, w, _ = _ssd_decay(dt_ref[rs, :], dtb_ref[...], alog_ref[...], rowv)
            for k, v in enumerate((dtp, cs, ecs, w)):
                o_ref[rs, k * BLK:(k + 1) * BLK] = v

    vec = pl.BlockSpec((1, BLK), lambda n: (0, 0))
    return pl.pallas_call(
        kern, name=name, grid=(t // tb,),
        in_specs=[pl.BlockSpec((tb, BLK), lambda n: (n, dto)), vec, vec],
        out_specs=pl.BlockSpec((tb, 4 * BLK), lambda n: (n, 0)),
        out_shape=jax.ShapeDtypeStruct((t, 4 * BLK), F32),
        compiler_params=_cp("parallel"),
    )(proj, _pad128(dt_bias), _pad128(a_log))


def _load_decay(d_ref):
    dtp, cs, ecs, w = (d_ref[:, k * BLK:(k + 1) * BLK] for k in range(4))
    return dtp, cs, ecs, w, _row_at(ecs, BLK - 1)


def _expand_heads(dtp, ecs, w, dec, e):
    ex = _dot(jnp.concatenate([dtp, ecs, w, jnp.broadcast_to(dec, (8, BLK))], axis=0), e)
    return ex[0:BLK], ex[BLK:2 * BLK], ex[2 * BLK:3 * BLK], jnp.max(ex[3 * BLK:EXP_ROWS], axis=0, keepdims=True)


def _ssd_specs(t, z_off, dt_off, rev):
    nb = t // BLK
    zo, dto = z_off // D_SSM, dt_off // BLK

    def b(n):
        return nb - 1 - n if rev else n

    vec = lambda w: pl.BlockSpec((1, w), lambda n: (0, 0))
    return [
        pl.BlockSpec((BLK, D_SSM), lambda n: (b(n), 0)),
        pl.BlockSpec((BLK, 1024), lambda n: (b(n), 2)),
        pl.BlockSpec((BLK, 1024), lambda n: (b(n), 3)),
        pl.BlockSpec((BLK, D_SSM), lambda n: (b(n), zo)),
        pl.BlockSpec((BLK, BLK), lambda n: (b(n), dto)),
        vec(BLK), vec(BLK), vec(D_SSM), vec(D_SSM),
        pl.BlockSpec((BLK, D_SSM), lambda n: (0, 0)),
        pl.BlockSpec((BLK, 4 * BLK), lambda n: (b(n), 0)),
    ]


def _pad128(v):
    return jnp.zeros((1, BLK), F32).at[0, :v.shape[0]].set(v)


def ssd_fwd(xbc, proj, decay, dt_bias, a_log, d_skip, gate_norm, *, z_off, dt_off, name):
    t = xbc.shape[0]
    nb = t // BLK

    def kern(x_ref, b_ref, c_ref, z_ref, dt_ref, dtb_ref, alog_ref, dsk_ref, gn_ref, e_ref, d_ref,
             yn_ref, ynt_ref, st_ref, p_ref):
        n = pl.program_id(0)

        @pl.when(n == 0)
        def _():
            p_ref[...] = jnp.zeros_like(p_ref)

        bgs = [b_ref[:, g * BLK:(g + 1) * BLK] for g in range(SSD_GROUPS)]
        cgs = [c_ref[:, g * BLK:(g + 1) * BLK] for g in range(SSD_GROUPS)]
        cbs = [_dot_nt(cgs[g], bgs[g]) for g in range(SSD_GROUPS)]
        zs = [_dot(cgs[g], p_ref[g]) for g in range(SSD_GROUPS)]
        dtp, cs, ecs, w, dec = _load_decay(d_ref)
        dtp_c, ecs_c, w_c, dec_c = _expand_heads(dtp, ecs, w, dec, e_ref[...])
        xv = x_ref[...]
        xdt = xv * dtp_c
        wx = w_c * xdt
        cst = cs.T
        tril = _rows((BLK, BLK)) >= _lanes((BLK, BLK))
        low = _lanes((BLK, BLK)) < HEAD_DIM
        st_ref[0] = p_ref[...]
        for g in range(SSD_GROUPS):
            gs = slice(g * GW, (g + 1) * GW)
            bg = bgs[g]
            pg = p_ref[g]
            ydiag, _ = _ssd_group_fwd(g, xdt, cs, cst, cbs[g], tril, low)
            y = ydiag + zs[g] * ecs_c[:, gs] + dsk_ref[:, gs] * xv[:, gs]
            p_ref[g] = pg * dec_c[:, gs] + _dot_tn(bg, wx[:, gs])
            yz = y * _silu(z_ref[:, gs])
            r = lax.rsqrt(jnp.mean(yz * yz, axis=-1, keepdims=True) + EPS)
            yn = yz * r * gn_ref[:, gs]
            yn_ref[:, gs] = yn.astype(yn_ref.dtype)
            ynt_ref[gs, :] = yn.T.astype(ynt_ref.dtype)

    return pl.pallas_call(
        kern, name=name, grid=(nb,),
        in_specs=_ssd_specs(t, z_off, dt_off, False),
        out_specs=[pl.BlockSpec((BLK, D_SSM), lambda n: (n, 0)),
                   pl.BlockSpec((D_SSM, BLK), lambda n: (0, n)),
                   pl.BlockSpec((1, SSD_GROUPS, BLK, GW), lambda n: (n, 0, 0, 0))],
        out_shape=[jax.ShapeDtypeStruct((t, D_SSM), MXU), jax.ShapeDtypeStruct((D_SSM, t), MXU),
                   jax.ShapeDtypeStruct((nb, SSD_GROUPS, BLK, GW), F32)],
        scratch_shapes=[pltpu.VMEM((SSD_GROUPS, BLK, GW), F32)],
        compiler_params=_cp("arbitrary"),
    )(xbc, xbc, xbc, proj, proj, _pad128(dt_bias), _pad128(a_log),
      jnp.repeat(d_skip, HEAD_DIM).reshape(1, D_SSM), gate_norm.reshape(1, D_SSM), _head_expand(), decay)


def ssd_bwd(xbc, proj, decay, st, dyn, dt_bias, a_log, d_skip, gate_norm, *, z_off, dt_off, name, carry=None):
    t = xbc.shape[0]
    nb = t // BLK

    def kern(x_ref, b_ref, c_ref, z_ref, dt_ref, dtb_ref, alog_ref, dsk_ref, gn_ref, e_ref, d_ref,
             et_ref, st_ref, dyn_ref,
             dxbc_ref, dz_ref, draw_ref, dgn_ref, ddsk_ref, ddtb_ref, dalog_ref,
             dp_ref, tr_ref):
        n = pl.program_id(0)
        blk = nb - 1 - n

        @pl.when(n == 0)
        def _():
            for r_ in (dp_ref, dgn_ref, ddsk_ref, ddtb_ref, dalog_ref):
                r_[...] = jnp.zeros_like(r_)

        bgs = [b_ref[:, g * BLK:(g + 1) * BLK] for g in range(SSD_GROUPS)]
        cgs = [c_ref[:, g * BLK:(g + 1) * BLK] for g in range(SSD_GROUPS)]
        cbs = [_dot_nt(cgs[g], bgs[g]) for g in range(SSD_GROUPS)]
        zs = [_dot(cgs[g], st_ref[0, g]) for g in range(SSD_GROUPS)]
        dwxs = [_dot(bgs[g], dp_ref[g]) for g in range(SSD_GROUPS)]
        valid = ((blk * BLK + _rows((BLK, BLK))) >= PAD) & (_lanes((BLK, BLK)) < SSD_HEADS)
        pre = dt_ref[...] + dtb_ref[...]
        av = -jnp.exp(alog_ref[...])
        dtp, cs, ecs, w, dec = _load_decay(d_ref)
        dtp_c, ecs_c, w_c, dec_c = _expand_heads(dtp, ecs, w, dec, e_ref[...])
        xv = x_ref[...]
        xdt = xv * dtp_c
        wx = w_c * xdt
        cst = cs.T
        tril = _rows((BLK, BLK)) >= _lanes((BLK, BLK))
        lane = _lanes((BLK, BLK))
        rowi = _rows((BLK, BLK))
        low = lane < HEAD_DIM
        dcs = jnp.zeros((BLK, BLK), F32)
        dcst = jnp.zeros((BLK, BLK), F32)
        for g in range(SSD_GROUPS):
            gs = slice(g * GW, (g + 1) * GW)
            bg, cg = bgs[g], cgs[g]
            pg = st_ref[0, g]
            dpn = dp_ref[g]
            xg = xv[:, gs]
            ydiag, lm = _ssd_group_fwd(g, xdt, cs, cst, cbs[g], tril, low)
            yoff = zs[g] * ecs_c[:, gs]
            y = ydiag + yoff + dsk_ref[:, gs] * xg
            zz = z_ref[:, gs]
            sz = _silu(zz)
            yz = y * sz
            r = lax.rsqrt(jnp.mean(yz * yz, axis=-1, keepdims=True) + EPS)
            yhat = yz * r
            dynv = dyn_ref[:, gs].astype(F32)
            gy = dynv * gn_ref[:, gs]
            dgn_ref[:, gs] += jnp.sum(dynv * yhat, axis=0, keepdims=True)
            dyz = r * (gy - yhat * jnp.mean(gy * yhat, axis=-1, keepdims=True))
            dy = dyz * sz
            dz_ref[:, gs] = (dyz * y * _silu_grad(zz)).astype(dz_ref.dtype)
            tr_ref[EXP_ROWS:RED_ROWS, gs] = jnp.broadcast_to(
                jnp.sum(dy * xg, axis=0, keepdims=True), (8, GW))
            dx = dsk_ref[:, gs] * dy
            dwx = dwxs[g]
            dxdt = w_c[:, gs] * dwx
            tr_ref[0:BLK, gs] = dwx * wx[:, gs]
            dbg = _dot_nt(wx[:, gs], dpn)
            dzo = ecs_c[:, gs] * dy
            tr_ref[BLK:2 * BLK, gs] = dy * yoff
            dcg = _dot_nt(dzo, pg)
            dp_ref[g] = dec_c[:, gs] * dpn + _dot_tn(cg, dzo)
            tr_ref[3 * BLK:EXP_ROWS, gs] = jnp.broadcast_to(
                jnp.sum(dpn * pg, axis=0, keepdims=True), (8, GW))
            dyh = [jnp.where(low == (k % 2 == 0), dy[:, (k // 2) * BLK:(k // 2 + 1) * BLK], 0.0) for k in range(4)]
            dms = [_dot_nt(dyh[k], xdt[:, g * GW + (k // 2) * BLK:g * GW + (k // 2 + 1) * BLK]) for k in range(4)]
            accs = [_dot_tn(lm[k][1], dyh[k]) for k in range(4)]
            dcb = jnp.zeros((BLK, BLK), F32)
            for k in range(4):
                h = 4 * g + k
                lmat, mmat = lm[k]
                dm = jnp.where(tril, dms[k], 0.0)
                nh = dm * mmat
                dcs = dcs + jnp.where(lane == h, jnp.sum(nh, axis=1, keepdims=True), 0.0)
                dcst = dcst - jnp.where(rowi == h, jnp.sum(nh, axis=0, keepdims=True), 0.0)
                dcb = dcb + dm * lmat
            dxdt = dxdt + jnp.concatenate([accs[0] + accs[1], accs[2] + accs[3]], axis=1)
            dcg = dcg + _dot(dcb, bg)
            dbg = dbg + _dot_tn(dcb, cg)
            tr_ref[2 * BLK:3 * BLK, gs] = dxdt * xg
            dxbc_ref[:, gs] = dx + dxdt * dtp_c[:, gs]
            dxbc_ref[:, D_SSM + g * BLK:D_SSM + (g + 1) * BLK] = dbg
            dxbc_ref[:, D_SSM + 1024 + g * BLK:D_SSM + 1024 + (g + 1) * BLK] = dcg
        red = _dot(tr_ref[...], et_ref[...])
        r1, r2, r3 = red[0:BLK], red[BLK:2 * BLK], red[2 * BLK:3 * BLK]
        ddec = jnp.max(red[3 * BLK:EXP_ROWS], axis=0, keepdims=True)
        ddsk_ref[...] += jnp.max(red[EXP_ROWS:RED_ROWS], axis=0, keepdims=True)
        dcs = dcs + dcst.T - r1 + r2
        dcs_last = jnp.sum(r1, axis=0, keepdims=True) + ddec * dec
        dcs = dcs + jnp.where(rowi == BLK - 1, dcs_last, 0.0)
        dda = _cumsum_rev(dcs)
        ddtp = r3 + dda * av
        dalog_ref[...] += jnp.sum(dda * dtp, axis=0, keepdims=True) * av
        draw = jnp.where(valid, ddtp * _sigmoid(pre), 0.0)
        ddtb_ref[...] += jnp.sum(draw, axis=0, keepdims=True)
        draw_ref[...] = draw.astype(draw_ref.dtype)

    vec = lambda w_: pl.BlockSpec((1, w_), lambda n: (0, 0))
    rb = lambda w_: pl.BlockSpec((BLK, w_), lambda n: (nb - 1 - n, 0))
    e = _head_expand()
    res = _call(
        kern, (xbc, xbc, xbc, proj, proj, _pad128(dt_bias), _pad128(a_log),
               jnp.repeat(d_skip, HEAD_DIM).reshape(1, D_SSM), gate_norm.reshape(1, D_SSM), e, decay, e.T, st, dyn),
        name=name, grid=(nb,),
        in_specs=_ssd_specs(t, z_off, dt_off, True)
        + [pl.BlockSpec((D_SSM, BLK), lambda n: (0, 0)),
           pl.BlockSpec((1, SSD_GROUPS, BLK, GW), lambda n: (nb - 1 - n, 0, 0, 0)),
           rb(D_SSM)],
        out_specs=[rb(2 * D_SSM), rb(D_SSM), rb(BLK), vec(D_SSM), vec(BLK), vec(BLK), vec(BLK)],
        out_shape=[jax.ShapeDtypeStruct((t, 2 * D_SSM), F32), jax.ShapeDtypeStruct((t, D_SSM), MXU),
                   jax.ShapeDtypeStruct((t, BLK), MXU), jax.ShapeDtypeStruct((1, D_SSM), F32),
                   jax.ShapeDtypeStruct((1, BLK), F32), jax.ShapeDtypeStruct((1, BLK), F32),
                   jax.ShapeDtypeStruct((1, BLK), F32)],
        scratch_shapes=[pltpu.VMEM((SSD_GROUPS, BLK, GW), F32), pltpu.VMEM((RED_ROWS, D_SSM), F32)],
        sem=("arbitrary",), carry=carry)
    dxbc, dz, draw, dgn, ddsk, ddtb, dalog = res[:7]
    return [dxbc, dz, draw, dgn[0], ddsk[0, :SSD_HEADS], ddtb[0, :SSD_HEADS], dalog[0, :SSD_HEADS]] + res[7:]


def loss_fwd_bwd(h, target, *, name):
    t, d = h.shape
    nb = t // BLK

    def kern(h_ref, t_ref, loss_ref, dh_ref):
        n = pl.program_id(0)
        err = jnp.where(n > 0, h_ref[...] - t_ref[...], 0.0)
        dh_ref[...] = err * (1.0 / d)
        part = (0.5 / d) * jnp.sum(jnp.sum(err * err, axis=1, keepdims=True), axis=0, keepdims=True)

        @pl.when(n == 0)
        def _():
            loss_ref[...] = part

        @pl.when(n > 0)
        def _():
            loss_ref[...] += part

    return pl.pallas_call(
        kern, name=name, grid=(nb,),
        in_specs=[pl.BlockSpec((BLK, d), lambda n: (n, 0)),
                  pl.BlockSpec((BLK, d), lambda n: (jnp.maximum(n - 1, 0), 0))],
        out_specs=[pl.BlockSpec((1, 1), lambda n: (0, 0)), pl.BlockSpec((BLK, d), lambda n: (n, 0))],
        out_shape=[jax.ShapeDtypeStruct((1, 1), F32), jax.ShapeDtypeStruct((t, d), F32)],
        compiler_params=_cp("arbitrary"),
    )(h, target)


def _ew_tile(r, c):
    cap = max(16, (256 * 1024) // c)
    best = None
    for tr in range(16, min(r, cap) + 1, 16):
        if r % tr == 0:
            best = tr
    return best if best is not None else r


def adamw(parts, w, m, v, *, name):
    npart, r, c = parts.shape
    tr = _ew_tile(r, c)

    def kern(p_ref, w_ref, m_ref, v_ref, g_ref, d_ref, m2_ref, v2_ref):
        g = p_ref[0].astype(F32)
        for k in range(1, npart):
            g = g + p_ref[k].astype(F32)
        m2 = ADAM_B1 * m_ref[...] + (1.0 - ADAM_B1) * g
        v2 = ADAM_B2 * v_ref[...] + (1.0 - ADAM_B2) * (g * g)
        m_hat = m2 / (1.0 - ADAM_B1 ** ADAM_STEP)
        v_hat = v2 / (1.0 - ADAM_B2 ** ADAM_STEP)
        g_ref[...] = g
        d_ref[...] = -ADAM_LR * (m_hat / (jnp.sqrt(v_hat) + ADAM_EPS) + ADAM_WD * w_ref[...])
        m2_ref[...] = m2
        v2_ref[...] = v2

    row = pl.BlockSpec((tr, c), lambda i: (i, 0))
    sds = jax.ShapeDtypeStruct((r, c), F32)
    return pl.pallas_call(
        kern, name=name, grid=(r // tr,),
        in_specs=[pl.BlockSpec((npart, tr, c), lambda i: (0, i, 0)), row, row, row],
        out_specs=[row, row, row, row], out_shape=[sds, sds, sds, sds],
        compiler_params=_cp("parallel"),
    )(parts, w, m, v)


def pair_add(p, land, *, name):
    _, r, c = p.shape
    tr = _ew_tile(r, c)
    core = lax.axis_index("c").astype(jnp.int32).reshape(1)

    def kern(c_ref, p_ref, l_ref, o_ref):
        o_ref[...] = (p_ref[...] + l_ref[...]).astype(o_ref.dtype)

    return pl.pallas_call(
        kern, name=name,
        grid_spec=pltpu.PrefetchScalarGridSpec(
            num_scalar_prefetch=1, grid=(4, r // tr),
            in_specs=[pl.BlockSpec((1, tr, c), lambda k, i, c_ref: (2 * k + c_ref[0], i, 0)),
                      pl.BlockSpec((1, tr, c), lambda k, i, c_ref: (k, i, 0))],
            out_specs=pl.BlockSpec((1, tr, c), lambda k, i, c_ref: (k, i, 0))),
        out_shape=jax.ShapeDtypeStruct((4, r, c), BF16),
        compiler_params=_cp("parallel", "parallel"),
    )(core, p, land)


def _me():
    return lax.axis_index("x"), lax.axis_index("y"), lax.axis_index("c")


def all_gather(xs, *, name):
    n = len(xs)

    def body(*refs):
        x_refs, out_refs = refs[:n], refs[n:2 * n]
        send_sems, recv_sems, local_sems = refs[2 * n:]
        mx, my, mc = _me()
        me, sib = (mx, my, mc), (mx, my, 1 - mc)
        chips = [(1 - mx, my), (mx, 1 - my), (1 - mx, 1 - my)]

        def rows(i, px, py, pc):
            return out_refs[i].at[4 * px + 2 * py + pc]

        def copy(i, k, block, to, src=None):
            return pltpu.make_async_remote_copy(
                src_ref=rows(i, *block) if src is None else src, dst_ref=rows(i, *block),
                send_sem=send_sems.at[7 * i + k], recv_sem=recv_sems.at[7 * i + k],
                device_id=to, device_id_type=MESH)

        mine = [pltpu.make_async_copy(x_refs[i], rows(i, *me), local_sems.at[i]) for i in range(n)]
        first = []
        for i in range(n):
            mine[i].start()
            first.append(copy(i, 0, me, sib, src=x_refs[i]))
            first += [copy(i, 1 + j, me, (*chip, mc), src=x_refs[i]) for j, chip in enumerate(chips)]
        for cp in first:
            cp.start()
        passed = []
        for i in range(n):
            for j, chip in enumerate(chips):
                copy(i, 1 + j, (*chip, mc), me).wait_recv()
                passed.append(copy(i, 4 + j, (*chip, mc), sib))
                passed[-1].start()
        for i in range(n):
            copy(i, 0, sib, me).wait_recv()
            for j, chip in enumerate(chips):
                copy(i, 4 + j, (*chip, 1 - mc), me).wait_recv()
        for cp in first + passed:
            cp.wait_send()
        for cp in mine:
            cp.wait()

    return pl.pallas_call(
        body, name=name,
        out_shape=[jax.ShapeDtypeStruct((N_DEV,) + x.shape, x.dtype) for x in xs],
        in_specs=[ANY] * n, out_specs=[ANY] * n,
        scratch_shapes=[pltpu.SemaphoreType.DMA((7 * n,)), pltpu.SemaphoreType.DMA((7 * n,)),
                        pltpu.SemaphoreType.DMA((n,))],
    )(*xs)


def pair_exchange(ps, *, name):
    n = len(ps)

    def body(*refs):
        p_refs, out_refs = refs[:n], refs[n:2 * n]
        send_sems, recv_sems = refs[2 * n:]
        mx, my, mc = _me()
        cps = [pltpu.make_async_remote_copy(
            src_ref=p_refs[i].at[2 * k + (1 - mc)], dst_ref=out_refs[i].at[k],
            send_sem=send_sems.at[4 * i + k], recv_sem=recv_sems.at[4 * i + k],
            device_id=(mx, my, 1 - mc), device_id_type=MESH) for i in range(n) for k in range(4)]
        for cp in cps:
            cp.start()
        for cp in cps:
            cp.wait_recv()
        for cp in cps:
            cp.wait_send()

    return pl.pallas_call(
        body, name=name,
        out_shape=[jax.ShapeDtypeStruct((4,) + p.shape[1:], p.dtype) for p in ps],
        in_specs=[ANY] * n, out_specs=[ANY] * n,
        scratch_shapes=[pltpu.SemaphoreType.DMA((4 * n,)), pltpu.SemaphoreType.DMA((4 * n,))],
    )(*ps)


def chip_exchange(qs, *, name):
    n = len(qs)

    def body(*refs):
        q_refs, out_refs = refs[:n], refs[n:2 * n]
        send_sems, recv_sems, local_sems = refs[2 * n:]
        mx, my, mc = _me()
        mine = 2 * mx + my
        chips = [(1 - mx, my), (mx, 1 - my), (1 - mx, 1 - my)]
        local, sends, recvs = [], [], []
        for i in range(n):
            local.append(pltpu.make_async_copy(q_refs[i].at[mine], out_refs[i].at[mine], local_sems.at[i]))
            for k, (px, py) in enumerate(chips):
                sems = dict(send_sem=send_sems.at[3 * i + k], recv_sem=recv_sems.at[3 * i + k],
                            device_id=(px, py, mc), device_id_type=MESH)
                sends.append(pltpu.make_async_remote_copy(
                    src_ref=q_refs[i].at[2 * px + py], dst_ref=out_refs[i].at[mine], **sems))
                recvs.append(pltpu.make_async_remote_copy(
                    src_ref=q_refs[i].at[mine], dst_ref=out_refs[i].at[2 * px + py], **sems))
        for cp in local + sends:
            cp.start()
        for cp in recvs:
            cp.wait_recv()
        for cp in sends:
            cp.wait_send()
        for cp in local:
            cp.wait()

    return pl.pallas_call(
        body, name=name,
        out_shape=[jax.ShapeDtypeStruct(q.shape, q.dtype) for q in qs],
        in_specs=[ANY] * n, out_specs=[ANY] * n,
        scratch_shapes=[pltpu.SemaphoreType.DMA((3 * n,)), pltpu.SemaphoreType.DMA((3 * n,)),
                        pltpu.SemaphoreType.DMA((n,))],
    )(*qs)


class _Carry:
    def __init__(self, inputs, out_shapes, sems, start, finish):
        self.inputs, self.out_shapes, self.sems = list(inputs), list(out_shapes), list(sems)
        self.start, self.finish = start, finish


def _call(kern, args, *, name, grid, in_specs, out_specs, out_shape, scratch_shapes=(), sem, carry=None):
    in_specs, out_specs, out_shape = list(in_specs), list(out_specs), list(out_shape)
    scratch_shapes = list(scratch_shapes)
    if carry is None:
        return list(pl.pallas_call(
            kern, name=name, grid=grid, in_specs=in_specs, out_specs=out_specs, out_shape=out_shape,
            scratch_shapes=scratch_shapes, compiler_params=_cp(*sem))(*args))
    ni, no, ns = len(in_specs), len(out_specs), len(scratch_shapes)
    ci, co = len(carry.inputs), len(carry.out_shapes)

    def body(*refs):
        o0 = ni + ci
        s0 = o0 + no + co
        ids = [pl.program_id(d) for d in range(len(grid))]
        first = functools.reduce(jnp.logical_and, [i == 0 for i in ids])
        last = functools.reduce(jnp.logical_and, [i == g - 1 for i, g in zip(ids, grid)])
        cin, cout, sems = refs[ni:o0], refs[o0 + no:s0], refs[s0 + ns:]

        @pl.when(first)
        def _():
            carry.start(cin, cout, sems)

        kern(*refs[:ni], *refs[o0:o0 + no], *refs[s0:s0 + ns])

        @pl.when(last)
        def _():
            carry.finish(cin, cout, sems)

    return list(pl.pallas_call(
        body, name=name, grid=grid, in_specs=in_specs + [ANY] * ci, out_specs=out_specs + [ANY] * co,
        out_shape=out_shape + carry.out_shapes, scratch_shapes=scratch_shapes + carry.sems,
        compiler_params=_cp(*(["arbitrary"] * len(grid))))(*args, *carry.inputs))


def gather_carry(xs):
    n = len(xs)

    def copies(cin, cout, sems, with_recv=True):
        mx, my, mc = _me()
        me = 4 * mx + 2 * my + mc
        peers = [(mx, my, 1 - mc), (1 - mx, my, mc), (mx, 1 - my, mc), (1 - mx, 1 - my, mc)]
        local, send, recv = [], [], []
        for i in range(n):
            local.append(pltpu.make_async_copy(cin[i], cout[i].at[me], sems[2].at[i]))
            for k, peer in enumerate(peers):
                common = dict(send_sem=sems[0].at[4 * i + k], recv_sem=sems[1].at[4 * i + k],
                              device_id=peer, device_id_type=MESH)
                send.append(pltpu.make_async_remote_copy(src_ref=cin[i], dst_ref=cout[i].at[me], **common))
                if with_recv:
                    recv.append(pltpu.make_async_remote_copy(
                        src_ref=cin[i], dst_ref=cout[i].at[4 * peer[0] + 2 * peer[1] + peer[2]], **common))
        return local, send, recv

    def start(cin, cout, sems):
        local, send, _ = copies(cin, cout, sems, with_recv=False)
        for cp in local + send:
            cp.start()

    def finish(cin, cout, sems):
        local, send, recv = copies(cin, cout, sems)
        for cp in recv:
            cp.wait_recv()
        for cp in send:
            cp.wait_send()
        for cp in local:
            cp.wait()

    return _Carry(xs, [jax.ShapeDtypeStruct((N_DEV,) + x.shape, x.dtype) for x in xs],
                  [pltpu.SemaphoreType.DMA((4 * n,)), pltpu.SemaphoreType.DMA((4 * n,)),
                   pltpu.SemaphoreType.DMA((n,))], start, finish)


def gather_relay(outs, *, name):
    n = len(outs)

    def body(*refs):
        bufs = refs[n:2 * n]
        send_sems, recv_sems = refs[2 * n:]
        mx, my, mc = _me()
        chips = [(1 - mx, my), (mx, 1 - my), (1 - mx, 1 - my)]
        send, recv = [], []
        for i in range(n):
            for j, (px, py) in enumerate(chips):
                common = dict(send_sem=send_sems.at[3 * i + j], recv_sem=recv_sems.at[3 * i + j],
                              device_id=(mx, my, 1 - mc), device_id_type=MESH)
                mine = bufs[i].at[4 * px + 2 * py + mc]
                send.append(pltpu.make_async_remote_copy(src_ref=mine, dst_ref=mine, **common))
                recv.append(pltpu.make_async_remote_copy(
                    src_ref=mine, dst_ref=bufs[i].at[4 * px + 2 * py + (1 - mc)], **common))
        for cp in send:
            cp.start()
        for cp in recv:
            cp.wait_recv()
        for cp in send:
            cp.wait_send()

    return pl.pallas_call(
        body, name=name, out_shape=[jax.ShapeDtypeStruct(o.shape, o.dtype) for o in outs],
        in_specs=[ANY] * n, out_specs=[ANY] * n, input_output_aliases={i: i for i in range(n)},
        scratch_shapes=[pltpu.SemaphoreType.DMA((3 * n,)), pltpu.SemaphoreType.DMA((3 * n,))],
    )(*outs)


def pair_carry(ps):
    n = len(ps)

    def copies(cin, cout, sems):
        mx, my, mc = _me()
        return [pltpu.make_async_remote_copy(
            src_ref=cin[i].at[2 * k + (1 - mc)], dst_ref=cout[i].at[k],
            send_sem=sems[0].at[4 * i + k], recv_sem=sems[1].at[4 * i + k],
            device_id=(mx, my, 1 - mc), device_id_type=MESH) for i in range(n) for k in range(4)]

    def start(cin, cout, sems):
        for cp in copies(cin, cout, sems):
            cp.start()

    def finish(cin, cout, sems):
        cps = copies(cin, cout, sems)
        for cp in cps:
            cp.wait_recv()
        for cp in cps:
            cp.wait_send()

    return _Carry(ps, [jax.ShapeDtypeStruct((4,) + p.shape[1:], p.dtype) for p in ps],
                  [pltpu.SemaphoreType.DMA((4 * n,)), pltpu.SemaphoreType.DMA((4 * n,))], start, finish)


def chip_carry(qs):
    n = len(qs)

    def copies(cin, cout, sems, with_recv=True):
        mx, my, mc = _me()
        mine = 2 * mx + my
        chips = [(1 - mx, my), (mx, 1 - my), (1 - mx, 1 - my)]
        local, send, recv = [], [], []
        for i in range(n):
            local.append(pltpu.make_async_copy(cin[i].at[mine], cout[i].at[mine], sems[2].at[i]))
            for k, (px, py) in enumerate(chips):
                common = dict(send_sem=sems[0].at[3 * i + k], recv_sem=sems[1].at[3 * i + k],
                              device_id=(px, py, mc), device_id_type=MESH)
                send.append(pltpu.make_async_remote_copy(
                    src_ref=cin[i].at[2 * px + py], dst_ref=cout[i].at[mine], **common))
                if with_recv:
                    recv.append(pltpu.make_async_remote_copy(
                        src_ref=cin[i].at[mine], dst_ref=cout[i].at[2 * px + py], **common))
        return local, send, recv

    def start(cin, cout, sems):
        local, send, _ = copies(cin, cout, sems, with_recv=False)
        for cp in local + send:
            cp.start()

    def finish(cin, cout, sems):
        local, send, recv = copies(cin, cout, sems)
        for cp in recv:
            cp.wait_recv()
        for cp in send:
            cp.wait_send()
        for cp in local:
            cp.wait()

    return _Carry(qs, [jax.ShapeDtypeStruct(q.shape, q.dtype) for q in qs],
                  [pltpu.SemaphoreType.DMA((3 * n,)), pltpu.SemaphoreType.DMA((3 * n,)),
                   pltpu.SemaphoreType.DMA((n,))], start, finish)


WEIGHTS = [
    "meta_tokens", "l0_mix_pre_norm", "l0_mix_post_norm", "l0_w_in", "l0_lru_conv_w", "l0_lru_conv_b",
    "l0_lru_w_a", "l0_lru_b_a", "l0_lru_w_x", "l0_lru_b_x", "l0_lru_lambda", "l0_attn_sinks", "l0_w_out",
    "l0_ffn_pre_norm", "l0_ffn_post_norm", "l0_ffn_w_up", "l0_ffn_conv_w", "l0_ffn_conv_b", "l0_ffn_w_down",
    "l1_mix_pre_norm", "l1_mix_post_norm", "l1_w_in", "l1_ssm_conv_w", "l1_ssm_conv_b", "l1_dt_bias",
    "l1_a_log", "l1_d_skip", "l1_gate_norm", "l1_w_out", "l1_ffn_pre_norm", "l1_ffn_post_norm",
    "l1_ffn_w_up", "l1_ffn_conv_w", "l1_ffn_conv_b", "l1_ffn_w_down",
]
INPUTS = (["x"] + WEIGHTS + ["loss_target"] + ["m_" + n for n in WEIGHTS] + ["v_" + n for n in WEIGHTS])

MATS = {"l0_w_in": ("col", (1024, 3328)), "l0_w_out": ("row", (2048, 1024)),
        "l0_ffn_w_up": ("col", (1024, 5632)), "l0_ffn_w_down": ("row", (2816, 1024)),
        "l1_w_in": ("col", (1024, 6176)), "l1_w_out": ("row", (2048, 1024)),
        "l1_ffn_w_up": ("col", (1024, 5632)), "l1_ffn_w_down": ("row", (2816, 1024))}
SMALL_SHARDED = {"meta_tokens": ("col", (16, 1024)), "l0_lru_conv_w": ("col", (4, 1024)),
                 "l0_ffn_conv_w": ("col", (3, 5632)), "l1_ssm_conv_w": ("col", (4, 4096)),
                 "l1_ffn_conv_w": ("col", (3, 5632))}
SHARDED = {**MATS, **SMALL_SHARDED}
REPLICATED = [n for n in WEIGHTS if n not in SHARDED]
PACK_W = 1024
SMALL_W = 128


def _shard_shape(name):
    kind, (r, c) = SHARDED[name]
    return (r, c // N_DEV) if kind == "col" else (r // N_DEV, c)


def _rows_of(numel, width):
    return -(-numel // width)


def _to_rows(a, width):
    flat = a.reshape(-1)
    rows = _rows_of(flat.shape[0], width)
    return jnp.pad(flat, (0, rows * width - flat.shape[0])).reshape(rows, width)


def _pack(arrs, width, total_rows):
    slab = jnp.concatenate([_to_rows(a, width) for a in arrs], axis=0)
    return jnp.pad(slab, ((0, total_rows - slab.shape[0]), (0, 0)))


def _unpack(slab, shapes, width):
    out, off = [], 0
    for shp in shapes:
        numel = math.prod(shp)
        rows = _rows_of(numel, width)
        out.append(slab[off:off + rows].reshape(-1)[:numel].reshape(shp))
        off += rows
    return out


def _round_up(n, m):
    return -(-n // m) * m


def _by_dest(name, g):
    kind, (r, c) = SHARDED[name]
    if kind == "col":
        return g.reshape(r, N_DEV, c // N_DEV).transpose(1, 0, 2)
    return g.reshape(N_DEV, r // N_DEV, c)


def _from_shards(name, blocks):
    kind, (r, c) = SHARDED[name]
    return blocks.transpose(1, 0, 2).reshape(r, c) if kind == "col" else blocks.reshape(r, c)


L1_IN_PAD = 6272


def _ffn_fwd(h, a, w, pfx):
    u, ut = rmsnorm_fwd(h, a[pfx + "ffn_pre_norm"], out_dtype=MXU, name=pfx + "ffn_pre", with_t=True)
    up = matmul(u, w[pfx + "ffn_w_up"], name=pfx + "ffn_up")
    act, act_t = dwconv_fwd(up, a[pfx + "ffn_conv_w"], a[pfx + "ffn_conv_b"], mode="geglu", x_off=0,
                            c_out=D_FF, cblk=256, out_dtype=MXU, name=pfx + "ffn_act", with_t=True)
    down = matmul(act, w[pfx + "ffn_w_down"], name=pfx + "ffn_down")
    out = rmsnorm_fwd(down, a[pfx + "ffn_post_norm"], res=h, out_dtype=F32, name=pfx + "ffn_post")
    return out, (h, ut, up, act_t, down)


def _dx_and_pair_stage(names, g, a_list, b, *, name):
    parts = [_by_dest(n, g[n]) for n in names]
    out, from_sibling = matmul_cat(a_list, b, trans_b=True, name=name, carry=pair_carry(parts))
    return out, [pair_add(p, l, name="rs_pair_add_" + n) for n, p, l in zip(names, parts, from_sibling)]


def _ffn_bwd(dh, saved, a, w, pfx, g, carry=None):
    h, ut, up, act_t, down = saved
    dd, g[pfx + "ffn_post_norm"] = rmsnorm_bwd(down, a[pfx + "ffn_post_norm"], dh, out_dtype=MXU,
                                               name=pfx + "ffn_post_bwd")
    dact = matmul(dd, w[pfx + "ffn_w_down"], trans_b=True, name=pfx + "ffn_down_dx")
    g[pfx + "ffn_w_down"] = matmul(act_t, dd, name=pfx + "ffn_down_dw")
    dups, g[pfx + "ffn_conv_w"], g[pfx + "ffn_conv_b"], carried = dwconv_bwd(
        up, a[pfx + "ffn_conv_w"], a[pfx + "ffn_conv_b"], dact, mode="geglu", x_off=0, c_out=D_FF,
        cblk=256, name=pfx + "ffn_act_bwd", carry=carry)
    g[pfx + "ffn_w_up"] = jnp.concatenate(
        [matmul(ut, d, name=pfx + "ffn_up_dw%d" % i) for i, d in enumerate(dups)], axis=1)
    du, q = _dx_and_pair_stage([pfx + "ffn_w_down", pfx + "ffn_w_up"], g, dups, w[pfx + "ffn_w_up"],
                               name=pfx + "ffn_up_dx")
    dh_in, g[pfx + "ffn_pre_norm"] = rmsnorm_bwd(h, a[pfx + "ffn_pre_norm"], du, res=dh, out_dtype=F32,
                                                 name=pfx + "ffn_pre_bwd")
    return dh_in, carried, q


GATHER_EARLY = ["l0_w_out", "l0_ffn_w_up", "l0_ffn_w_down"]
GATHER_LATE = ["l1_w_in", "l1_w_out", "l1_ffn_w_up", "l1_ffn_w_down"]
RS_L1_FFN = ["l1_ffn_w_down", "l1_ffn_w_up"]
RS_L1_MIX = ["l1_w_out", "l1_w_in"]
RS_L0_FFN = ["l0_ffn_w_down", "l0_ffn_w_up"]
RS_LAST = ["l0_w_out", "l0_w_in", "l0_lru_conv_w", "l0_ffn_conv_w", "l1_ssm_conv_w", "l1_ffn_conv_w"]


def _local_step(a, w, shards):
    x = a["x"][0]
    seq = x.shape[0]
    h0 = jnp.concatenate([jnp.zeros((PAD, D_MODEL), F32), a["meta_tokens"], x], axis=0)
    g, landed = {}, {}

    u0, u0t = rmsnorm_fwd(h0, a["l0_mix_pre_norm"], out_dtype=MXU, name="l0_mix_pre", with_t=True)
    proj0 = matmul(u0, w["l0_w_in"], name="l0_in")
    lru = (a["l0_lru_conv_w"], a["l0_lru_conv_b"], a["l0_lru_w_a"], a["l0_lru_b_a"], a["l0_lru_w_x"],
           a["l0_lru_b_x"], a["l0_lru_lambda"])
    ya, ya_t, hl, *early = lru_fwd(proj0, *lru, gate_off=0, xr_off=1024, name="l0_lru",
                                   carry=gather_carry([shards[n] for n in GATHER_EARLY]))
    yb, *late = attn_fwd(proj0, a["l0_attn_sinks"], q_off=2048, k_off=3072, v_off=3200, name="l0_attn",
                         carry=gather_carry([shards[n] for n in GATHER_LATE]))
    relayed = gather_relay(early + late, name="gather_relay")
    w = dict(w, **{n: _from_shards(n, blocks) for n, blocks in zip(GATHER_EARLY + GATHER_LATE, relayed)})
    w["l1_w_in"] = jnp.pad(w["l1_w_in"], ((0, 0), (0, L1_IN_PAD - w["l1_w_in"].shape[1])))
    o0 = matmul_cat([ya, yb], w["l0_w_out"], name="l0_out")
    h1 = rmsnorm_fwd(o0, a["l0_mix_post_norm"], res=h0, out_dtype=F32, name="l0_mix_post")
    h2, ffn0 = _ffn_fwd(h1, a, w, "l0_")

    u2, u2t = rmsnorm_fwd(h2, a["l1_mix_pre_norm"], out_dtype=MXU, name="l1_mix_pre", with_t=True)
    proj1 = matmul(u2, w["l1_w_in"], name="l1_in")
    xbc = dwconv_fwd(proj1, a["l1_ssm_conv_w"], a["l1_ssm_conv_b"], mode="silu", x_off=D_SSM,
                     c_out=2 * D_SSM, cblk=512, out_dtype=F32, name="l1_ssm_conv")
    ssd = (a["l1_dt_bias"], a["l1_a_log"], a["l1_d_skip"], a["l1_gate_norm"])
    decay = ssd_decay(proj1, a["l1_dt_bias"], a["l1_a_log"], dt_off=3 * D_SSM, name="l1_ssd_decay")
    yn, yn_t, st = ssd_fwd(xbc, proj1, decay, *ssd, z_off=0, dt_off=3 * D_SSM, name="l1_ssd")
    o1 = matmul(yn, w["l1_w_out"], name="l1_out")
    h3 = rmsnorm_fwd(o1, a["l1_mix_post_norm"], res=h2, out_dtype=F32, name="l1_mix_post")
    h4, ffn1 = _ffn_fwd(h3, a, w, "l1_")

    loss, dh4 = loss_fwd_bwd(h4, a["loss_target"][0], name="loss")

    dh3, _, q_l1_ffn = _ffn_bwd(dh4, ffn1, a, w, "l1_", g)
    do1, g["l1_mix_post_norm"] = rmsnorm_bwd(o1, a["l1_mix_post_norm"], dh3, out_dtype=MXU,
                                             name="l1_mix_post_bwd")
    dyn = matmul(do1, w["l1_w_out"], trans_b=True, name="l1_out_dx")
    g["l1_w_out"] = matmul(yn_t, do1, name="l1_out_dw")
    (dxbc, dz, draw, g["l1_gate_norm"], g["l1_d_skip"], g["l1_dt_bias"], g["l1_a_log"], *got) = ssd_bwd(
        xbc, proj1, decay, st, dyn, *ssd, z_off=0, dt_off=3 * D_SSM, name="l1_ssd_bwd",
        carry=chip_carry(q_l1_ffn))
    landed.update(zip(RS_L1_FFN, got))
    (dxin,), g["l1_ssm_conv_w"], g["l1_ssm_conv_b"], _ = dwconv_bwd(
        proj1, a["l1_ssm_conv_w"], a["l1_ssm_conv_b"], dxbc, mode="silu", x_off=D_SSM,
        c_out=2 * D_SSM, cblk=512, name="l1_ssm_conv_bwd")
    g["l1_w_in"] = jnp.concatenate(
        [matmul(u2t, dz, name="l1_in_dw_z"), matmul(u2t, dxin, name="l1_in_dw_x"),
         matmul(u2t, draw, name="l1_in_dw_dt")[:, :SSD_HEADS]], axis=1)
    du2, q_l1_mix = _dx_and_pair_stage(RS_L1_MIX, g, [dz, dxin, draw], w["l1_w_in"], name="l1_in_dx")
    dh2, g["l1_mix_pre_norm"] = rmsnorm_bwd(h2, a["l1_mix_pre_norm"], du2, res=dh3, out_dtype=F32,
                                            name="l1_mix_pre_bwd")

    dh1, got, q_l0_ffn = _ffn_bwd(dh2, ffn0, a, w, "l0_", g, carry=chip_carry(q_l1_mix))
    landed.update(zip(RS_L1_MIX, got))
    do0, g["l0_mix_post_norm"] = rmsnorm_bwd(o0, a["l0_mix_post_norm"], dh1, out_dtype=MXU,
                                             name="l0_mix_post_bwd")
    dy = matmul(do0, w["l0_w_out"], trans_b=True, name="l0_out_dx")
    g["l0_w_out"] = jnp.concatenate([matmul(ya_t, do0, name="l0_out_dw_a"),
                                     matmul(yb.T, do0, name="l0_out_dw_b")], axis=0)
    (dgate, dxr, g["l0_lru_conv_w"], dcb, g["l0_lru_w_a"], dba, g["l0_lru_w_x"], dbx, dlam) = lru_bwd(
        proj0, hl, dy, *lru, gate_off=0, xr_off=1024, dy_off=0, name="l0_lru_bwd")
    g["l0_lru_conv_b"], g["l0_lru_b_a"], g["l0_lru_b_x"], g["l0_lru_lambda"] = dcb[0], dba[0], dbx[0], dlam[0]
    dq, dk, dv, g["l0_attn_sinks"], *got = attn_bwd(
        proj0, a["l0_attn_sinks"], dy, q_off=2048, k_off=3072, v_off=3200, dy_off=1024, name="l0_attn_bwd",
        carry=chip_carry(q_l0_ffn))
    landed.update(zip(RS_L0_FFN, got))
    dproj0 = [dgate, dxr, dq, dk, dv]
    g["l0_w_in"] = jnp.concatenate(
        [matmul(u0t, d, name="l0_in_dw%d" % i) for i, d in enumerate(dproj0)], axis=1)
    du0, q_last = _dx_and_pair_stage(RS_LAST, g, dproj0, w["l0_w_in"], name="l0_in_dx")
    dh0, g["l0_mix_pre_norm"] = rmsnorm_bwd(h0, a["l0_mix_pre_norm"], du0, res=dh1, out_dtype=F32,
                                            name="l0_mix_pre_bwd")
    g["meta_tokens"] = dh0[PAD:BLK]
    meta = _by_dest("meta_tokens", g["meta_tokens"])
    q_meta = pair_add(meta, pair_exchange([meta], name="rs_pair_meta")[0], name="rs_pair_add_meta_tokens")
    landed.update(zip(RS_LAST + ["meta_tokens"], chip_exchange(q_last + [q_meta], name="rs_chip")))
    for n in REPLICATED:
        g[n] = g[n].reshape(a[n].shape)
    return loss[0, 0], dh0[BLK:].reshape(1, seq, D_MODEL), g, landed


def kernel(*args):
    a = dict(zip(INPUTS, args))
    first = list(SMALL_SHARDED) + ["l0_w_in"]
    got = all_gather([a[n].astype(MXU) if n in MATS else a[n] for n in first], name="gather_first")
    full = {n: _from_shards(n, blocks) for n, blocks in zip(first, got)}
    shards = {n: a[n].astype(MXU) for n in GATHER_EARLY + GATHER_LATE}
    loss_part, grad_x, g, landed = _local_step(
        {**a, **{n: full[n] for n in SMALL_SHARDED}}, {"l0_w_in": full["l0_w_in"]}, shards)
    loss = lax.psum(loss_part, ("x", "y", "c"))

    sh_out = {n: adamw(landed[n], a[n], a["m_" + n], a["v_" + n], name="adamw_" + n) for n in SHARDED}

    rp_shapes = [a[n].shape for n in REPLICATED]
    rrows = _round_up(sum(_rows_of(math.prod(s), SMALL_W) for s in rp_shapes), 128)
    gathered = all_gather([_pack([g[n] for n in REPLICATED], SMALL_W, rrows)], name="gather_small_grads")[0]
    rp_out = adamw(gathered, *[_pack([a[p + n] for n in REPLICATED], SMALL_W, rrows) for p in ("", "m_", "v_")],
                   name="adamw_replicated")
    rp_out = [dict(zip(REPLICATED, _unpack(s, rp_shapes, SMALL_W))) for s in rp_out]

    outs = [loss, grad_x]
    for k in range(4):
        outs += [sh_out[n][k] if n in SHARDED else rp_out[k][n] for n in WEIGHTS]
    return tuple(outs)
```

```python
import functools
import math

import jax
import jax.numpy as jnp
import numpy as np
from jax import lax
from jax.experimental import pallas as pl
from jax.experimental.pallas import tpu as pltpu

F32 = jnp.float32
BF16 = jnp.bfloat16
MXU = jnp.bfloat16

D_MODEL = 1024
N_META = 16
BLK = 128
PAD = BLK - N_META
D_RNN = 1024
LRU_C = 8.0
N_Q_HEADS = 16
HEAD_DIM = 64
D_SSM = 2048
SSD_HEADS = 32
SSD_GROUPS = 8
D_FF = 2816
EPS = 1e-6
NEG = -1e30
N_DEV = 8

ADAM_LR = 0.001
ADAM_B1 = 0.9
ADAM_B2 = 0.999
ADAM_EPS = 1e-08
ADAM_WD = 0.01
ADAM_STEP = 10

VMEM_LIMIT = 56 * 1024 * 1024
MESH = pl.DeviceIdType.MESH
ANY = pl.BlockSpec(memory_space=pl.ANY)


def _cp(*sem):
    return pltpu.CompilerParams(dimension_semantics=sem, vmem_limit_bytes=VMEM_LIMIT)


def _pick(n, cands):
    for c in cands:
        if n % c == 0:
            return c
    return n


def _dot(a, b):
    return jnp.dot(a.astype(MXU), b.astype(MXU), preferred_element_type=F32)


def _dot_nt(a, b):
    return lax.dot_general(a.astype(MXU), b.astype(MXU), (((1,), (1,)), ((), ())),
                           preferred_element_type=F32)


def _dot_tn(a, b):
    return jnp.dot(a.T.astype(MXU), b.astype(MXU), preferred_element_type=F32)


def _sigmoid(x):
    return 1.0 / (1.0 + jnp.exp(-x))


def _log1p(x):
    u = 1.0 + x
    return jnp.where(u == 1.0, x, jnp.log(u) * (x / jnp.where(u == 1.0, 1.0, u - 1.0)))


def _expm1(x):
    u = jnp.exp(x)
    um1 = u - 1.0
    lg = jnp.log(jnp.where(u > 0.0, u, 1.0))
    safe = (um1 != 0.0) & (um1 != -1.0)
    return jnp.where(um1 == 0.0, x, jnp.where(um1 == -1.0, -1.0,
                                               um1 * (x / jnp.where(safe, lg, 1.0))))


def _softplus(x):
    return jnp.maximum(x, 0.0) + _log1p(jnp.exp(-jnp.abs(x)))


_GC = math.sqrt(2.0 / math.pi)


def _gelu(x):
    t = jnp.tanh(_GC * (x + 0.044715 * x * x * x))
    return 0.5 * x * (1.0 + t)


def _gelu_grad(x):
    t = jnp.tanh(_GC * (x + 0.044715 * x * x * x))
    return 0.5 * (1.0 + t) + 0.5 * x * (1.0 - t * t) * (_GC * (1.0 + 3.0 * 0.044715 * x * x))


def _silu(x):
    return x * _sigmoid(x)


def _silu_grad(x):
    s = _sigmoid(x)
    return s * (1.0 + x * (1.0 - s))


def _rows(shape):
    return lax.broadcasted_iota(jnp.int32, shape, 0)


def _lanes(shape):
    return lax.broadcasted_iota(jnp.int32, shape, 1)


def _shift_down(x, tail, d):
    if d == 0:
        return x
    n = x.shape[0]
    xr = pltpu.roll(x, d, 0)
    tr = pltpu.roll(tail, d, 0)
    first = jnp.where(_rows(tr.shape) < d, tr, xr[0:8])
    return jnp.concatenate([first, xr[8:n]], axis=0)


def _shift_up(x, head, d):
    if d == 0:
        return x
    n = x.shape[0]
    xr = pltpu.roll(x, n - d, 0)
    hr = pltpu.roll(head, 8 - d, 0)
    last = jnp.where(_rows(hr.shape) >= 8 - d, hr, xr[n - 8:n])
    return jnp.concatenate([xr[0:n - 8], last], axis=0)


def _keep(x, valid, s):
    return jnp.where(valid, x, 0.0) if s == 0 else x


def _row_at(x, i):
    return jnp.sum(jnp.where(_rows(x.shape) == i, x, 0.0), axis=0, keepdims=True)


def _scan_fwd(a, u):
    n = a.shape[0]
    ri = _rows(a.shape)
    d = 1
    while d < n:
        m = ri >= d
        us = jnp.where(m, pltpu.roll(u, d, 0), 0.0)
        as_ = jnp.where(m, pltpu.roll(a, d, 0), 1.0)
        u = u + a * us
        a = a * as_
        d *= 2
    return a, u


def _scan_rev(c, u):
    n = c.shape[0]
    ri = _rows(c.shape)
    d = 1
    while d < n:
        m = ri < n - d
        us = jnp.where(m, pltpu.roll(u, n - d, 0), 0.0)
        cs = jnp.where(m, pltpu.roll(c, n - d, 0), 1.0)
        u = u + c * us
        c = c * cs
        d *= 2
    return c, u


def _cumsum_fwd(x):
    n = x.shape[0]
    ri = _rows(x.shape)
    d = 1
    while d < n:
        x = x + jnp.where(ri >= d, pltpu.roll(x, d, 0), 0.0)
        d *= 2
    return x


def _cumsum_rev(x):
    n = x.shape[0]
    ri = _rows(x.shape)
    d = 1
    while d < n:
        x = x + jnp.where(ri < n - d, pltpu.roll(x, n - d, 0), 0.0)
        d *= 2
    return x


MATMUL_VMEM = 40 * 1024 * 1024


def _matmul_tiles(m, n, k, tk, out_bytes):
    best = None
    for tm in (1664, 1408, 1040, 1024, 832, 640, 512, 384, 256, 128):
        if m % tm:
            continue
        for tn in (2048, 1664, 1408, 1024, 896, 640, 512, 384, 256, 128):
            if n % tn:
                continue
            vmem = 2 * (tm * tk * 2 + tk * tn * 2 + tm * tn * out_bytes) + (tm * tn * 4 if k > tk else 0)
            if vmem > MATMUL_VMEM:
                continue
            traffic = (n // tn) * m * k * 2 + (m // tm) * k * n * 2
            if best is None or traffic < best[0]:
                best = (traffic, tm, tn)
    return (best[1], best[2]) if best else (m, n)


def matmul(a, b, *, trans_b=False, out_dtype=F32, name):
    m, k = a.shape
    n = b.shape[0] if trans_b else b.shape[1]
    tk = k if k <= 2048 else _pick(k, (1664, 1408, 1024, 896, 512, 256, 128))
    nk = k // tk
    tm, tn = _matmul_tiles(m, n, k, tk, jnp.dtype(out_dtype).itemsize)

    def product(a_ref, b_ref):
        return _dot_nt(a_ref[...], b_ref[...]) if trans_b else _dot(a_ref[...], b_ref[...])

    def kern_once(a_ref, b_ref, o_ref):
        o_ref[...] = product(a_ref, b_ref).astype(o_ref.dtype)

    def kern_acc(a_ref, b_ref, o_ref, acc_ref):
        kk = pl.program_id(2)

        @pl.when(kk == 0)
        def _():
            acc_ref[...] = product(a_ref, b_ref)

        @pl.when(kk > 0)
        def _():
            acc_ref[...] += product(a_ref, b_ref)

        @pl.when(kk == nk - 1)
        def _():
            o_ref[...] = acc_ref[...].astype(o_ref.dtype)

    b_spec = (pl.BlockSpec((tn, tk), lambda i, j, kk: (j, kk)) if trans_b
              else pl.BlockSpec((tk, tn), lambda i, j, kk: (kk, j)))
    return pl.pallas_call(
        kern_once if nk == 1 else kern_acc, name=name,
        grid=(m // tm, n // tn, nk),
        in_specs=[pl.BlockSpec((tm, tk), lambda i, j, kk: (i, kk)), b_spec],
        out_specs=pl.BlockSpec((tm, tn), lambda i, j, kk: (i, j)),
        out_shape=jax.ShapeDtypeStruct((m, n), out_dtype),
        scratch_shapes=[] if nk == 1 else [pltpu.VMEM((tm, tn), F32)],
        compiler_params=_cp("parallel", "parallel", "arbitrary"),
    )(a, b)


def matmul_cat(a_list, b, *, trans_b=False, out_dtype=F32, name, carry=None):
    m = a_list[0].shape[0]
    ks = [x.shape[1] for x in a_list]
    ktot = sum(ks)
    n = b.shape[0] if trans_b else b.shape[1]
    tn = _pick(n, (512, 256, 128))
    tm = next((c for c in (1664, 1040, 832, 640, 512, 384, 256, 128)
               if m % c == 0 and c * ktot * 2 <= 8 * 1024 * 1024), m)
    na = len(a_list)

    def kern(*refs):
        b_ref, o_ref = refs[na], refs[na + 1]
        acc, off = None, 0
        for a_ref, k in zip(refs[:na], ks):
            if trans_b:
                part = _dot_nt(a_ref[...], b_ref[:, off:off + k])
            else:
                part = _dot(a_ref[...], b_ref[off:off + k, :])
            acc = part if acc is None else acc + part
            off += k
        o_ref[...] = acc.astype(o_ref.dtype)

    b_spec = (pl.BlockSpec((tn, ktot), lambda i, j: (j, 0)) if trans_b
              else pl.BlockSpec((ktot, tn), lambda i, j: (0, j)))
    res = _call(
        kern, (*a_list, b), name=name, grid=(m // tm, n // tn),
        in_specs=[pl.BlockSpec((tm, k), lambda i, j: (i, 0)) for k in ks] + [b_spec],
        out_specs=[pl.BlockSpec((tm, tn), lambda i, j: (i, j))],
        out_shape=[jax.ShapeDtypeStruct((m, n), out_dtype)],
        sem=("parallel", "parallel"), carry=carry)
    return res[0] if carry is None else (res[0], res[1:])


def _row_tile(t):
    return _pick(t, (832, 640, 512, 384, 256, 128))


def rmsnorm_fwd(x, w, res=None, *, out_dtype, name, with_t=False, carry=None):
    t, d = x.shape
    tr = _conv_tile(t) if with_t else _row_tile(t)

    def kern(*refs):
        x_ref, w_ref = refs[0], refs[1]
        o_ref = refs[-2] if with_t else refs[-1]
        xv = x_ref[...]
        r = lax.rsqrt(jnp.mean(xv * xv, axis=-1, keepdims=True) + EPS)
        y = xv * r * w_ref[...]
        if res is not None:
            y = refs[2][...] + y
        o_ref[...] = y.astype(o_ref.dtype)
        if with_t:
            refs[-1][...] = y.T.astype(o_ref.dtype)

    row = pl.BlockSpec((tr, d), lambda i: (i, 0))
    vec = pl.BlockSpec((1, d), lambda i: (0, 0))
    ins = [x, w.reshape(1, d)] + ([] if res is None else [res])
    specs = [row, vec] + ([] if res is None else [row])
    out_specs, out_shape = [row], [jax.ShapeDtypeStruct((t, d), out_dtype)]
    if with_t:
        out_specs.append(pl.BlockSpec((d, tr), lambda i: (0, i)))
        out_shape.append(jax.ShapeDtypeStruct((d, t), out_dtype))
    res_ = _call(kern, ins, name=name, grid=(t // tr,), in_specs=specs, out_specs=out_specs,
                 out_shape=out_shape, sem=("parallel",), carry=carry)
    return res_[0] if len(res_) == 1 else res_


def rmsnorm_bwd(x, w, dy, res=None, *, out_dtype, name, carry=None):
    t, d = x.shape
    tr = _row_tile(t)

    def kern(*refs):
        if res is None:
            x_ref, w_ref, dy_ref, dx_ref, dw_ref = refs
        else:
            x_ref, w_ref, dy_ref, r_ref, dx_ref, dw_ref = refs
        i = pl.program_id(0)
        xv = x_ref[...]
        dyv = dy_ref[...].astype(F32)
        r = lax.rsqrt(jnp.mean(xv * xv, axis=-1, keepdims=True) + EPS)
        xh = xv * r
        g = dyv * w_ref[...]
        dx = r * (g - xh * jnp.mean(g * xh, axis=-1, keepdims=True))
        if res is not None:
            dx = r_ref[...] + dx
        dx_ref[...] = dx.astype(dx_ref.dtype)
        part = jnp.sum(dyv * xh, axis=0, keepdims=True)

        @pl.when(i == 0)
        def _():
            dw_ref[...] = part

        @pl.when(i > 0)
        def _():
            dw_ref[...] += part

    row = pl.BlockSpec((tr, d), lambda i: (i, 0))
    vec = pl.BlockSpec((1, d), lambda i: (0, 0))
    ins = [x, w.reshape(1, d), dy] + ([] if res is None else [res])
    specs = [row, vec, row] + ([] if res is None else [row])
    return _call(kern, ins, name=name, grid=(t // tr,), in_specs=specs, out_specs=[row, vec],
                 out_shape=[jax.ShapeDtypeStruct((t, d), out_dtype), jax.ShapeDtypeStruct((1, d), F32)],
                 sem=("arbitrary",), carry=carry)


def _conv_tile(t):
    return _pick(t, (640, 384, 256, 128))


def _conv_apply(x, tail, cw, cb, ksz):
    y = cb
    for k in range(ksz):
        y = y + cw[k:k + 1, :] * _shift_down(x, tail, ksz - 1 - k)
    return y


def dwconv_fwd(x, cw, cb, *, mode, x_off, c_out, cblk, out_dtype, name, with_t=False):
    t = x.shape[0]
    ksz = cw.shape[0]
    tb = _conv_tile(t)
    nb, ncb, t8 = t // tb, c_out // cblk, tb // 8
    xo = x_off // cblk
    nin = 2 if mode == "geglu" else 1

    def kern(*refs):
        o_ref = refs[-2] if with_t else refs[-1]
        n = pl.program_id(1)
        for c in range(cblk // BLK):
            ls = slice(c * BLK, (c + 1) * BLK)
            for s in range(tb // BLK):
                rs = slice(s * BLK, (s + 1) * BLK)
                valid = (n * tb + s * BLK + _rows((BLK, BLK))) >= PAD
                hs = []
                for q in range(nin):
                    x_ref, t_ref, w_ref, b_ref = refs[4 * q:4 * q + 4]
                    tail = (jnp.where(n > 0, t_ref[:, ls], 0.0) if s == 0
                            else x_ref[s * BLK - 8:s * BLK, ls])
                    hs.append(_conv_apply(x_ref[rs, ls], tail, w_ref[:, ls], b_ref[:, ls], ksz))
                y = _gelu(hs[0]) * hs[1] if mode == "geglu" else _silu(hs[0])
                y = _keep(y, valid, s)
                o_ref[rs, ls] = y.astype(o_ref.dtype)
                if with_t:
                    refs[-1][ls, rs] = y.T.astype(o_ref.dtype)

    ins, specs = [], []
    for q in range(nin):
        co = xo + q * ncb
        wo = q * ncb
        ins += [x, x, cw, cb.reshape(1, -1)]
        specs += [
            pl.BlockSpec((tb, cblk), lambda j, n, co=co: (n, co + j)),
            pl.BlockSpec((8, cblk), lambda j, n, co=co: (jnp.maximum(n * t8 - 1, 0), co + j)),
            pl.BlockSpec((ksz, cblk), lambda j, n, wo=wo: (0, wo + j)),
            pl.BlockSpec((1, cblk), lambda j, n, wo=wo: (0, wo + j)),
        ]
    out_specs = pl.BlockSpec((tb, cblk), lambda j, n: (n, j))
    out_shape = jax.ShapeDtypeStruct((t, c_out), out_dtype)
    if with_t:
        out_specs = [out_specs, pl.BlockSpec((cblk, tb), lambda j, n: (j, n))]
        out_shape = [out_shape, jax.ShapeDtypeStruct((c_out, t), out_dtype)]
    return pl.pallas_call(
        kern, name=name, grid=(ncb, nb), in_specs=specs, out_specs=out_specs, out_shape=out_shape,
        compiler_params=_cp("parallel", "parallel"),
    )(*ins)


def dwconv_bwd(x, cw, cb, dy, *, mode, x_off, c_out, cblk, name, carry=None):
    t = x.shape[0]
    ksz = cw.shape[0]
    tb = _conv_tile(t)
    nb, ncb, t8 = t // tb, c_out // cblk, tb // 8
    xo = x_off // cblk
    nin = 2 if mode == "geglu" else 1
    ctot = nin * c_out

    def kern(*refs):
        dy_ref = refs[4 * nin]
        outs = refs[4 * nin + 1:4 * nin + 1 + 3 * nin]
        heads = refs[4 * nin + 1 + 3 * nin:]
        n = pl.program_id(1)
        blk = nb - 1 - n

        @pl.when(n == 0)
        def _():
            for q in range(nin):
                heads[q][...] = jnp.zeros_like(heads[q])
                outs[3 * q + 1][...] = jnp.zeros_like(outs[3 * q + 1])
                outs[3 * q + 2][...] = jnp.zeros_like(outs[3 * q + 2])

        for c in range(cblk // BLK):
            ls = slice(c * BLK, (c + 1) * BLK)
            head = [heads[q][:, ls] for q in range(nin)]
            dwa = [[None] * ksz for _ in range(nin)]
            dba = [None] * nin
            for s in reversed(range(tb // BLK)):
                rs = slice(s * BLK, (s + 1) * BLK)
                valid = (blk * tb + s * BLK + _rows((BLK, BLK))) >= PAD
                xs, tails, hs = [], [], []
                for q in range(nin):
                    x_ref, t_ref, w_ref, b_ref = refs[4 * q:4 * q + 4]
                    tail = (jnp.where(blk > 0, t_ref[:, ls], 0.0) if s == 0
                            else x_ref[s * BLK - 8:s * BLK, ls])
                    xs.append(x_ref[rs, ls])
                    tails.append(tail)
                    hs.append(_conv_apply(xs[q], tail, w_ref[:, ls], b_ref[:, ls], ksz))
                dyv = dy_ref[rs, ls].astype(F32)
                if mode == "geglu":
                    dhs = [dyv * hs[1] * _gelu_grad(hs[0]), dyv * _gelu(hs[0])]
                else:
                    dhs = [dyv * _silu_grad(hs[0])]
                for q in range(nin):
                    w_ref = refs[4 * q + 2]
                    dh = _keep(dhs[q], valid, s)
                    dx = jnp.zeros_like(dh)
                    for k in range(ksz):
                        sh = ksz - 1 - k
                        dx = dx + w_ref[k:k + 1, ls] * _shift_up(dh, head[q], sh)
                        part = jnp.sum(dh * _shift_down(xs[q], tails[q], sh), axis=0, keepdims=True)
                        dwa[q][k] = part if dwa[q][k] is None else dwa[q][k] + part
                    outs[3 * q][rs, ls] = _keep(dx, valid, s).astype(outs[3 * q].dtype)
                    part = jnp.sum(dh, axis=0, keepdims=True)
                    dba[q] = part if dba[q] is None else dba[q] + part
                    head[q] = dh[0:8]
            for q in range(nin):
                outs[3 * q + 1][:, ls] += jnp.concatenate(dwa[q], axis=0)
                outs[3 * q + 2][:, ls] += dba[q]
                heads[q][:, ls] = head[q]

    ins, specs, out_specs, out_shape, scratch = [], [], [], [], []
    for q in range(nin):
        co = xo + q * ncb
        wo = q * ncb
        ins += [x, x, cw, cb.reshape(1, -1)]
        specs += [
            pl.BlockSpec((tb, cblk), lambda j, n, co=co: (nb - 1 - n, co + j)),
            pl.BlockSpec((8, cblk), lambda j, n, co=co: (jnp.maximum((nb - 1 - n) * t8 - 1, 0), co + j)),
            pl.BlockSpec((ksz, cblk), lambda j, n, wo=wo: (0, wo + j)),
            pl.BlockSpec((1, cblk), lambda j, n, wo=wo: (0, wo + j)),
        ]
        out_specs += [
            pl.BlockSpec((tb, cblk), lambda j, n: (nb - 1 - n, j)),
            pl.BlockSpec((ksz, cblk), lambda j, n: (0, j)),
            pl.BlockSpec((1, cblk), lambda j, n: (0, j)),
        ]
        out_shape += [jax.ShapeDtypeStruct((t, c_out), MXU),
                      jax.ShapeDtypeStruct((ksz, c_out), F32),
                      jax.ShapeDtypeStruct((1, c_out), F32)]
        scratch.append(pltpu.VMEM((8, cblk), F32))
    ins.append(dy)
    specs.append(pl.BlockSpec((tb, cblk), lambda j, n: (nb - 1 - n, j)))
    res = _call(kern, ins, name=name, grid=(ncb, nb), in_specs=specs, out_specs=out_specs,
                out_shape=out_shape, scratch_shapes=scratch, sem=("parallel", "arbitrary"), carry=carry)
    dxs = [res[3 * q] for q in range(nin)]
    dcw = jnp.concatenate([res[3 * q + 1] for q in range(nin)], axis=1)
    dcb = jnp.concatenate([res[3 * q + 2] for q in range(nin)], axis=1)
    return dxs, dcw, dcb.reshape(ctot), res[3 * nin:]


def _lru_tile(t):
    return _pick(t, (640, 384, 256, 128))


def _lru_gates(xc, wa, ba, wx, bx, sp):
    r = _sigmoid(_dot(xc, wa) + ba)
    i = _sigmoid(_dot(xc, wx) + bx)
    log_a = -LRU_C * r * sp
    a = jnp.exp(log_a)
    mult = jnp.sqrt(-_expm1(2.0 * log_a))
    return r, i, a, mult


def lru_fwd(proj, cw, cb, wa, ba, wx, bx, lam, *, gate_off, xr_off, name, carry=None):
    t = proj.shape[0]
    tb = _lru_tile(t)
    nb, ns, t8 = t // tb, tb // BLK, tb // 8
    go, xo = gate_off // BLK, xr_off // BLK

    def kern(g_ref, x_ref, xt_ref, cw_ref, cb_ref, wa_ref, ba_ref, wx_ref, bx_ref, lam_ref,
             y_ref, yt_ref, h_ref, hc_ref):
        n = pl.program_id(1)

        @pl.when(n == 0)
        def _():
            hc_ref[...] = jnp.zeros_like(hc_ref)

        sp = _softplus(-lam_ref[...])
        hprev = hc_ref[0:1, :]
        scans = []
        for s in range(ns):
            sl = slice(s * BLK, (s + 1) * BLK)
            xv = x_ref[sl, :]
            tail = jnp.where(n > 0, xt_ref[...], 0.0) if s == 0 else x_ref[s * BLK - 8:s * BLK, :]
            valid = (n * tb + s * BLK + _rows((BLK, BLK))) >= PAD
            xc = _keep(_conv_apply(xv, tail, cw_ref[...], cb_ref[...], 4), valid, s)
            _, i, a, mult = _lru_gates(xc, wa_ref[0], ba_ref[...], wx_ref[0], bx_ref[...], sp)
            scans.append(_scan_fwd(a, mult * (i * xc)))
        for s in range(ns):
            sl = slice(s * BLK, (s + 1) * BLK)
            ca, cu = scans[s]
            h = cu + ca * hprev
            hprev = _row_at(h, BLK - 1)
            h_ref[sl, :] = h
            y = _gelu(g_ref[sl, :]) * h
            y_ref[sl, :] = y.astype(y_ref.dtype)
            yt_ref[:, sl] = y.T.astype(yt_ref.dtype)
        hc_ref[...] = jnp.broadcast_to(hprev, hc_ref.shape)

    vec = pl.BlockSpec((1, BLK), lambda j, n: (0, j))
    mat = pl.BlockSpec((1, BLK, BLK), lambda j, n: (j, 0, 0))
    return _call(
        kern, (proj, proj, proj, cw, cb.reshape(1, -1), wa, ba.reshape(1, -1), wx, bx.reshape(1, -1),
               lam.reshape(1, -1)),
        name=name, grid=(D_RNN // BLK, nb),
        in_specs=[
            pl.BlockSpec((tb, BLK), lambda j, n: (n, go + j)),
            pl.BlockSpec((tb, BLK), lambda j, n: (n, xo + j)),
            pl.BlockSpec((8, BLK), lambda j, n: (jnp.maximum(n * t8 - 1, 0), xo + j)),
            pl.BlockSpec((4, BLK), lambda j, n: (0, j)), vec, mat, vec, mat, vec, vec,
        ],
        out_specs=[pl.BlockSpec((tb, BLK), lambda j, n: (n, j)),
                   pl.BlockSpec((BLK, tb), lambda j, n: (j, n)),
                   pl.BlockSpec((tb, BLK), lambda j, n: (n, j))],
        out_shape=[jax.ShapeDtypeStruct((t, D_RNN), MXU), jax.ShapeDtypeStruct((D_RNN, t), MXU),
                   jax.ShapeDtypeStruct((t, D_RNN), F32)],
        scratch_shapes=[pltpu.VMEM((8, BLK), F32)],
        sem=("parallel", "arbitrary"), carry=carry)


def lru_bwd(proj, h, dy, cw, cb, wa, ba, wx, bx, lam, *, gate_off, xr_off, dy_off, name):
    t = proj.shape[0]
    tb = _lru_tile(t)
    nb, ns, t8 = t // tb, tb // BLK, tb // 8
    go, xo, do = gate_off // BLK, xr_off // BLK, dy_off // BLK

    def kern(g_ref, x_ref, xt_ref, h_ref, ht_ref, dy_ref, cw_ref, cb_ref, wa_ref, ba_ref,
             wx_ref, bx_ref, lam_ref,
             dg_ref, dx_ref, dcw_ref, dcb_ref, dwa_ref, dba_ref, dwx_ref, dbx_ref, dlam_ref,
             gin_ref, head_ref):
        n = pl.program_id(1)
        blk = nb - 1 - n

        @pl.when(n == 0)
        def _():
            gin_ref[...] = jnp.zeros_like(gin_ref)
            head_ref[...] = jnp.zeros_like(head_ref)
            for r_ in (dcw_ref, dcb_ref, dwa_ref, dba_ref, dwx_ref, dbx_ref, dlam_ref):
                r_[...] = jnp.zeros_like(r_)

        lamv = lam_ref[...]
        sp = _softplus(-lamv)
        dsp_dlam = -_sigmoid(-lamv)
        g_in = gin_ref[0:1, :]
        head = head_ref[...]
        ones8 = jnp.ones((8, BLK), F32)
        wav, wxv = wa_ref[0], wx_ref[0]
        staged = {}
        for s in range(ns):
            sl = slice(s * BLK, (s + 1) * BLK)
            xv = x_ref[sl, :]
            if s == 0:
                tail = jnp.where(blk > 0, xt_ref[...], 0.0)
                htail = jnp.where(blk > 0, ht_ref[...], 0.0)
            else:
                tail = x_ref[s * BLK - 8:s * BLK, :]
                htail = h_ref[s * BLK - 8:s * BLK, :]
            valid = (blk * tb + s * BLK + _rows((BLK, BLK))) >= PAD
            xc = _keep(_conv_apply(xv, tail, cw_ref[...], cb_ref[...], 4), valid, s)
            r, i, a, mult = _lru_gates(xc, wav, ba_ref[...], wxv, bx_ref[...], sp)
            hv = h_ref[sl, :]
            hprev = _shift_down(hv, htail, 1)
            gv = g_ref[sl, :]
            dyv = dy_ref[sl, :].astype(F32)
            dg_ref[sl, :] = (dyv * hv * _gelu_grad(gv)).astype(dg_ref.dtype)
            cc, cu = _scan_rev(_shift_up(a, ones8, 1), dyv * _gelu(gv))
            staged[s] = (xv, tail, valid, xc, r, i, a, mult, hprev, cc, cu)
        for s in reversed(range(ns)):
            sl = slice(s * BLK, (s + 1) * BLK)
            xv, tail, valid, xc, r, i, a, mult, hprev, cc, cu = staged[s]
            gg = cu + cc * g_in
            g_in = _row_at(a * gg, 0)
            da = gg * hprev
            di = gg * mult * xc
            dxc = gg * mult * i
            dmult = gg * i * xc
            dlog_a = da * a - dmult * (a * a) / mult
            dr = dlog_a * (-LRU_C * sp)
            dlam_ref[...] += jnp.sum(dlog_a * (-LRU_C) * r, axis=0, keepdims=True) * dsp_dlam
            dpr = dr * r * (1.0 - r)
            dpi = di * i * (1.0 - i)
            dxc = dxc + _dot_nt(dpr, wav) + _dot_nt(dpi, wxv)
            dxc, dpr, dpi = _keep(dxc, valid, s), _keep(dpr, valid, s), _keep(dpi, valid, s)
            dwa_ref[0] += _dot_tn(xc, dpr)
            dwx_ref[0] += _dot_tn(xc, dpi)
            dba_ref[...] += jnp.sum(dpr, axis=0, keepdims=True)
            dbx_ref[...] += jnp.sum(dpi, axis=0, keepdims=True)
            dx = jnp.zeros_like(dxc)
            dws = []
            for k in range(4):
                dx = dx + cw_ref[k:k + 1, :] * _shift_up(dxc, head, 3 - k)
                dws.append(jnp.sum(dxc * _shift_down(xv, tail, 3 - k), axis=0, keepdims=True))
            dx_ref[sl, :] = _keep(dx, valid, s).astype(dx_ref.dtype)
            dcw_ref[...] += jnp.concatenate(dws, axis=0)
            dcb_ref[...] += jnp.sum(dxc, axis=0, keepdims=True)
            head = dxc[0:8]
        gin_ref[...] = jnp.broadcast_to(g_in, gin_ref.shape)
        head_ref[...] = head

    vec = pl.BlockSpec((1, BLK), lambda j, n: (0, j))
    mat = pl.BlockSpec((1, BLK, BLK), lambda j, n: (j, 0, 0))
    cws = pl.BlockSpec((4, BLK), lambda j, n: (0, j))

    def rb(off):
        return pl.BlockSpec((tb, BLK), lambda j, n: (nb - 1 - n, off + j))

    def tl(off):
        return pl.BlockSpec((8, BLK), lambda j, n: (jnp.maximum((nb - 1 - n) * t8 - 1, 0), off + j))

    return pl.pallas_call(
        kern, name=name, grid=(D_RNN // BLK, nb),
        in_specs=[rb(go), rb(xo), tl(xo), rb(0), tl(0), rb(do), cws, vec, mat, vec, mat, vec, vec],
        out_specs=[rb(0), rb(0), cws, vec, mat, vec, mat, vec, vec],
        out_shape=[jax.ShapeDtypeStruct((t, D_RNN), MXU), jax.ShapeDtypeStruct((t, D_RNN), MXU),
                   jax.ShapeDtypeStruct((4, D_RNN), F32), jax.ShapeDtypeStruct((1, D_RNN), F32),
                   jax.ShapeDtypeStruct((8, BLK, BLK), F32), jax.ShapeDtypeStruct((1, D_RNN), F32),
                   jax.ShapeDtypeStruct((8, BLK, BLK), F32), jax.ShapeDtypeStruct((1, D_RNN), F32),
                   jax.ShapeDtypeStruct((1, D_RNN), F32)],
        scratch_shapes=[pltpu.VMEM((8, BLK), F32), pltpu.VMEM((8, BLK), F32)],
        compiler_params=_cp("parallel", "arbitrary"),
    )(proj, proj, proj, h, h, dy, cw, cb.reshape(1, -1), wa, ba.reshape(1, -1), wx,
      bx.reshape(1, -1), lam.reshape(1, -1))


_SCALE = HEAD_DIM ** -0.5


STK = 4


def _attn_masks(n):
    qi = np.arange(STK * BLK)[:, None] % BLK
    c = np.arange(3 * BLK)[None, :]
    tq = n * BLK + qi - PAD
    s_band = (n - 1) * BLK + c - PAD
    d_band = tq - s_band
    ok_band = (s_band >= N_META) & (d_band >= 0) & (d_band < BLK)
    jm = c - 2 * BLK
    d_meta = tq - (jm - PAD)
    ok_meta = (jm >= PAD) & (d_meta >= 0)
    is_band = c < 2 * BLK
    ok = np.where(is_band, ok_band, ok_meta)
    dist = np.where(is_band, d_band, np.minimum(d_meta, BLK)).astype(np.float32)
    return ok, dist


def _stack_heads(g, e):
    return [8 * g + 2 * i + e for i in range(STK)]


def _attn_bias_table():
    tabs = []
    for n in range(3):
        ok, dist = _attn_masks(n)
        per = []
        for g in range(2):
            for e in range(2):
                slope = np.repeat(np.array([2.0 ** (-8.0 * (h + 1) / N_Q_HEADS) for h in _stack_heads(g, e)],
                                           np.float32), BLK)[:, None]
                per.append(np.where(ok, -(slope * dist), np.float32(NEG)).astype(np.float32))
        tabs.append(np.stack(per))
    return jnp.asarray(np.stack(tabs))


def _stack_sinks(heads, sk):
    return jnp.concatenate(
        [jnp.broadcast_to(jnp.sum(jnp.where(_lanes(sk.shape) == h, sk, 0.0), axis=1, keepdims=True),
                          (BLK, 1)) for h in heads], axis=0)


def _stack_tiles(ref, g, sel):
    return jnp.concatenate(
        [jnp.where(sel, ref[:, (4 * g + i) * BLK:(4 * g + i + 1) * BLK].astype(F32), 0.0)
         for i in range(STK)], axis=0)


def _attn_probs(qk, bias, sink):
    s = qk * _SCALE + bias
    mx = jnp.maximum(jnp.max(s, axis=-1, keepdims=True), sink)
    p = jnp.exp(s - mx)
    es = jnp.exp(sink - mx)
    inv = 1.0 / (jnp.sum(p, axis=-1, keepdims=True) + es)
    return p * inv, es * inv


def _attn_specs(t, q_off, k_off, v_off, rev):
    nb = t // BLK
    qo, ko, vo = q_off // 1024, k_off // BLK, v_off // BLK

    def b(n):
        return nb - 1 - n if rev else n

    return [
        pl.BlockSpec((BLK, 1024), lambda n: (b(n), qo)),
        pl.BlockSpec((BLK, BLK), lambda n: (b(n), ko)),
        pl.BlockSpec((BLK, BLK), lambda n: (b(n), vo)),
        pl.BlockSpec((BLK, BLK), lambda n: (jnp.maximum(b(n) - 1, 0), ko)),
        pl.BlockSpec((BLK, BLK), lambda n: (jnp.maximum(b(n) - 1, 0), vo)),
        pl.BlockSpec((BLK, BLK), lambda n: (0, ko)),
        pl.BlockSpec((BLK, BLK), lambda n: (0, vo)),
        pl.BlockSpec((1, BLK), lambda n: (0, 0)),
        pl.BlockSpec((1, 4, STK * BLK, 3 * BLK), lambda n: (jnp.minimum(b(n), 2), 0, 0, 0)),
    ]


def attn_fwd(proj, sinks, *, q_off, k_off, v_off, name, carry=None):
    t = proj.shape[0]
    nb = t // BLK

    def kern(q_ref, kc_ref, vc_ref, kp_ref, vp_ref, km_ref, vm_ref, sk_ref, tab_ref, o_ref):
        k_all = jnp.concatenate([kp_ref[...], kc_ref[...], km_ref[...]], axis=0)
        v_all = jnp.concatenate([vp_ref[...], vc_ref[...], vm_ref[...]], axis=0)
        k_alt = pltpu.roll(k_all, HEAD_DIM, 1)
        v_alt = pltpu.roll(v_all, HEAD_DIM, 1)
        low = _lanes((BLK, BLK)) < HEAD_DIM
        stacks = [(g, e) for g in range(2) for e in range(2)]
        qk = {(g, e): _dot_nt(_stack_tiles(q_ref, g, low == (e == 0)), k_all if g == e else k_alt)
              for g, e in stacks}
        ps = {(g, e): _attn_probs(qk[g, e], tab_ref[0, 2 * g + e],
                                  _stack_sinks(_stack_heads(g, e), sk_ref[...]))[0] for g, e in stacks}
        outs = {(g, e): _dot(ps[g, e], v_all if g == e else v_alt) for g, e in stacks}
        for hp in range(N_Q_HEADS // 2):
            g, rs = hp // STK, slice((hp % STK) * BLK, (hp % STK + 1) * BLK)
            o_ref[:, hp * BLK:(hp + 1) * BLK] = jnp.where(low, outs[g, 0][rs], outs[g, 1][rs]).astype(o_ref.dtype)

    sk = jnp.zeros((1, BLK), F32).at[0, :N_Q_HEADS].set(sinks)
    return _call(
        kern, (proj, proj, proj, proj, proj, proj, proj, sk, _attn_bias_table()), name=name, grid=(nb,),
        in_specs=_attn_specs(t, q_off, k_off, v_off, False),
        out_specs=[pl.BlockSpec((BLK, 1024), lambda n: (n, 0))],
        out_shape=[jax.ShapeDtypeStruct((t, 1024), MXU)],
        sem=("parallel",), carry=carry)


def attn_bwd(proj, sinks, dy, *, q_off, k_off, v_off, dy_off, name, carry=None):
    t = proj.shape[0]
    nb = t // BLK
    do = dy_off // 1024

    def kern(q_ref, kc_ref, vc_ref, kp_ref, vp_ref, km_ref, vm_ref, sk_ref, tab_ref, do_ref,
             dq_ref, dk_ref, dv_ref, dsk_ref, ck_ref, cv_ref, mk_ref, mv_ref):
        n = pl.program_id(0)
        blk = nb - 1 - n

        @pl.when(n == 0)
        def _():
            for r_ in (ck_ref, cv_ref, mk_ref, mv_ref, dsk_ref):
                r_[...] = jnp.zeros_like(r_)

        k_all = jnp.concatenate([kp_ref[...], kc_ref[...], km_ref[...]], axis=0)
        v_all = jnp.concatenate([vp_ref[...], vc_ref[...], vm_ref[...]], axis=0)
        k_alt = pltpu.roll(k_all, HEAD_DIM, 1)
        v_alt = pltpu.roll(v_all, HEAD_DIM, 1)
        low = _lanes((BLK, BLK)) < HEAD_DIM
        lane1 = _lanes((1, BLK))
        dk_all = jnp.zeros((3 * BLK, BLK), F32)
        dv_all = jnp.zeros((3 * BLK, BLK), F32)
        dsk = jnp.zeros((1, BLK), F32)
        stacks = [(g, e) for g in range(2) for e in range(2)]
        qm = {(g, e): _stack_tiles(q_ref, g, low == (e == 0)) for g, e in stacks}
        dom = {(g, e): _stack_tiles(do_ref, g, low == (e == 0)) for g, e in stacks}
        qk = {(g, e): _dot_nt(qm[g, e], k_all if g == e else k_alt) for g, e in stacks}
        dp = {(g, e): _dot_nt(dom[g, e], v_all if g == e else v_alt) for g, e in stacks}
        ps, dss = {}, {}
        for g, e in stacks:
            heads = _stack_heads(g, e)
            p, psink = _attn_probs(qk[g, e], tab_ref[0, 2 * g + e], _stack_sinks(heads, sk_ref[...]))
            delta = jnp.sum(p * dp[g, e], axis=-1, keepdims=True)
            ps[g, e] = p
            dss[g, e] = p * (dp[g, e] - delta) * _SCALE
            psd = psink * delta
            for i, h in enumerate(heads):
                dsk = dsk + jnp.where(lane1 == h, -jnp.sum(psd[i * BLK:(i + 1) * BLK], axis=0, keepdims=True), 0.0)
        dqs = {(g, e): _dot(dss[g, e], k_all if g == e else k_alt) for g, e in stacks}
        for g, e in stacks:
            dkh = _dot_tn(dss[g, e], qm[g, e])
            dvh = _dot_tn(ps[g, e], dom[g, e])
            if g != e:
                dkh = pltpu.roll(dkh, HEAD_DIM, 1)
                dvh = pltpu.roll(dvh, HEAD_DIM, 1)
            dk_all = dk_all + dkh
            dv_all = dv_all + dvh
        for hp in range(N_Q_HEADS // 2):
            g, rs = hp // STK, slice((hp % STK) * BLK, (hp % STK + 1) * BLK)
            dq_ref[:, hp * BLK:(hp + 1) * BLK] = jnp.where(low, dqs[g, 0][rs], dqs[g, 1][rs]).astype(dq_ref.dtype)
        dsk_ref[...] += dsk
        mk_ref[...] += dk_all[2 * BLK:3 * BLK]
        mv_ref[...] += dv_all[2 * BLK:3 * BLK]
        is0 = blk == 0
        dk_ref[...] = (dk_all[BLK:2 * BLK] + ck_ref[...] + jnp.where(is0, mk_ref[...], 0.0)).astype(dk_ref.dtype)
        dv_ref[...] = (dv_all[BLK:2 * BLK] + cv_ref[...] + jnp.where(is0, mv_ref[...], 0.0)).astype(dv_ref.dtype)
        ck_ref[...] = dk_all[0:BLK]
        cv_ref[...] = dv_all[0:BLK]

    sk = jnp.zeros((1, BLK), F32).at[0, :N_Q_HEADS].set(sinks)
    kv = pl.BlockSpec((BLK, BLK), lambda n: (nb - 1 - n, 0))
    res = _call(
        kern, (proj, proj, proj, proj, proj, proj, proj, sk, _attn_bias_table(), dy), name=name, grid=(nb,),
        in_specs=_attn_specs(t, q_off, k_off, v_off, True)
        + [pl.BlockSpec((BLK, 1024), lambda n: (nb - 1 - n, do))],
        out_specs=[pl.BlockSpec((BLK, 1024), lambda n: (nb - 1 - n, 0)), kv, kv,
                   pl.BlockSpec((1, BLK), lambda n: (0, 0))],
        out_shape=[jax.ShapeDtypeStruct((t, 1024), MXU), jax.ShapeDtypeStruct((t, BLK), MXU),
                   jax.ShapeDtypeStruct((t, BLK), MXU), jax.ShapeDtypeStruct((1, BLK), F32)],
        scratch_shapes=[pltpu.VMEM((BLK, BLK), F32)] * 4,
        sem=("arbitrary",), carry=carry)
    return [res[0], res[1], res[2], res[3][0, :N_Q_HEADS]] + res[4:]


GW = D_SSM // SSD_GROUPS
EXP_ROWS = 3 * BLK + 8
RED_ROWS = EXP_ROWS + 8


def _head_expand():
    ch = jnp.arange(D_SSM) // HEAD_DIM
    return (jnp.arange(BLK)[:, None] == ch[None, :]).astype(BF16)


def _ssd_decay(raw, dtb, alog, rowv):
    valid = rowv & (_lanes((BLK, BLK)) < SSD_HEADS)
    pre = raw + dtb
    dtp = jnp.where(valid, _softplus(pre), 0.0)
    av = -jnp.exp(alog)
    cs = _cumsum_fwd(dtp * av)
    cs_last = _row_at(cs, BLK - 1)
    return valid, pre, dtp, av, cs, jnp.exp(cs), jnp.exp(cs_last - cs), jnp.exp(cs_last)


def _head_col(x, h):
    return jnp.sum(jnp.where(_lanes(x.shape) == h, x, 0.0), axis=1, keepdims=True)


def _ssd_group_fwd(g, xdt, cs, cst, cb, tril, low):
    lm = []
    for k in range(4):
        h = 4 * g + k
        seg = _head_col(cs, h) - _row_at(cst, h)
        lmat = jnp.where(tril, jnp.exp(jnp.minimum(seg, 0.0)), 0.0)
        lm.append((lmat, cb * lmat))
    hv = [_dot(lm[k][1], xdt[:, g * GW + (k // 2) * BLK:g * GW + (k // 2 + 1) * BLK]) for k in range(4)]
    return jnp.concatenate([jnp.where(low, hv[0], hv[1]), jnp.where(low, hv[2], hv[3])], axis=1), lm


def ssd_decay(proj, dt_bias, a_log, *, dt_off, name):
    t = proj.shape[0]
    tb = _conv_tile(t)
    dto = dt_off // BLK

    def kern(dt_ref, dtb_ref, alog_ref, o_ref):
        n = pl.program_id(0)
        for s in range(tb // BLK):
            rs = slice(s * BLK, (s + 1) * BLK)
            rowv = (n * tb + s * BLK + _rows((BLK, BLK))) >= PAD
            _, _, dtp, _, cs, ecs, w, _ = _ssd_decay(dt_ref[rs, :], dtb_ref[...], alog_ref[...], rowv)
            for k, v in enumerate((dtp, cs, ecs, w)):
                o_ref[rs, k * BLK:(k + 1) * BLK] = v

    vec = pl.BlockSpec((1, BLK), lambda n: (0, 0))
    return pl.pallas_call(
        kern, name=name, grid=(t // tb,),
        in_specs=[pl.BlockSpec((tb, BLK), lambda n: (n, dto)), vec, vec],
        out_specs=pl.BlockSpec((tb, 4 * BLK), lambda n: (n, 0)),
        out_shape=jax.ShapeDtypeStruct((t, 4 * BLK), F32),
        compiler_params=_cp("parallel"),
    )(proj, _pad128(dt_bias), _pad128(a_log))


def _load_decay(d_ref):
    dtp, cs, ecs, w = (d_ref[:, k * BLK:(k + 1) * BLK] for k in range(4))
    return dtp, cs, ecs, w, _row_at(ecs, BLK - 1)


def _expand_heads(dtp, ecs, w, dec, e):
    ex = _dot(jnp.concatenate([dtp, ecs, w, jnp.broadcast_to(dec, (8, BLK))], axis=0), e)
    return ex[0:BLK], ex[BLK:2 * BLK], ex[2 * BLK:3 * BLK], jnp.max(ex[3 * BLK:EXP_ROWS], axis=0, keepdims=True)


def _ssd_specs(t, z_off, dt_off, rev):
    nb = t // BLK
    zo, dto = z_off // D_SSM, dt_off // BLK

    def b(n):
        return nb - 1 - n if rev else n

    vec = lambda w: pl.BlockSpec((1, w), lambda n: (0, 0))
    return [
        pl.BlockSpec((BLK, D_SSM), lambda n: (b(n), 0)),
        pl.BlockSpec((BLK, 1024), lambda n: (b(n), 2)),
        pl.BlockSpec((BLK, 1024), lambda n: (b(n), 3)),
        pl.BlockSpec((BLK, D_SSM), lambda n: (b(n), zo)),
        pl.BlockSpec((BLK, BLK), lambda n: (b(n), dto)),
        vec(BLK), vec(BLK), vec(D_SSM), vec(D_SSM),
        pl.BlockSpec((BLK, D_SSM), lambda n: (0, 0)),
        pl.BlockSpec((BLK, 4 * BLK), lambda n: (b(n), 0)),
    ]


def _pad128(v):
    return jnp.zeros((1, BLK), F32).at[0, :v.shape[0]].set(v)


def ssd_fwd(xbc, proj, decay, dt_bias, a_log, d_skip, gate_norm, *, z_off, dt_off, name):
    t = xbc.shape[0]
    nb = t // BLK

    def kern(x_ref, b_ref, c_ref, z_ref, dt_ref, dtb_ref, alog_ref, dsk_ref, gn_ref, e_ref, d_ref,
             yn_ref, ynt_ref, st_ref, p_ref):
        n = pl.program_id(0)

        @pl.when(n == 0)
        def _():
            p_ref[...] = jnp.zeros_like(p_ref)

        bgs = [b_ref[:, g * BLK:(g + 1) * BLK] for g in range(SSD_GROUPS)]
        cgs = [c_ref[:, g * BLK:(g + 1) * BLK] for g in range(SSD_GROUPS)]
        cbs = [_dot_nt(cgs[g], bgs[g]) for g in range(SSD_GROUPS)]
        zs = [_dot(cgs[g], p_ref[g]) for g in range(SSD_GROUPS)]
        dtp, cs, ecs, w, dec = _load_decay(d_ref)
        dtp_c, ecs_c, w_c, dec_c = _expand_heads(dtp, ecs, w, dec, e_ref[...])
        xv = x_ref[...]
        xdt = xv * dtp_c
        wx = w_c * xdt
        cst = cs.T
        tril = _rows((BLK, BLK)) >= _lanes((BLK, BLK))
        low = _lanes((BLK, BLK)) < HEAD_DIM
        st_ref[0] = p_ref[...]
        for g in range(SSD_GROUPS):
            gs = slice(g * GW, (g + 1) * GW)
            bg = bgs[g]
            pg = p_ref[g]
            ydiag, _ = _ssd_group_fwd(g, xdt, cs, cst, cbs[g], tril, low)
            y = ydiag + zs[g] * ecs_c[:, gs] + dsk_ref[:, gs] * xv[:, gs]
            p_ref[g] = pg * dec_c[:, gs] + _dot_tn(bg, wx[:, gs])
            yz = y * _silu(z_ref[:, gs])
            r = lax.rsqrt(jnp.mean(yz * yz, axis=-1, keepdims=True) + EPS)
            yn = yz * r * gn_ref[:, gs]
            yn_ref[:, gs] = yn.astype(yn_ref.dtype)
            ynt_ref[gs, :] = yn.T.astype(ynt_ref.dtype)

    return pl.pallas_call(
        kern, name=name, grid=(nb,),
        in_specs=_ssd_specs(t, z_off, dt_off, False),
        out_specs=[pl.BlockSpec((BLK, D_SSM), lambda n: (n, 0)),
                   pl.BlockSpec((D_SSM, BLK), lambda n: (0, n)),
                   pl.BlockSpec((1, SSD_GROUPS, BLK, GW), lambda n: (n, 0, 0, 0))],
        out_shape=[jax.ShapeDtypeStruct((t, D_SSM), MXU), jax.ShapeDtypeStruct((D_SSM, t), MXU),
                   jax.ShapeDtypeStruct((nb, SSD_GROUPS, BLK, GW), F32)],
        scratch_shapes=[pltpu.VMEM((SSD_GROUPS, BLK, GW), F32)],
        compiler_params=_cp("arbitrary"),
    )(xbc, xbc, xbc, proj, proj, _pad128(dt_bias), _pad128(a_log),
      jnp.repeat(d_skip, HEAD_DIM).reshape(1, D_SSM), gate_norm.reshape(1, D_SSM), _head_expand(), decay)


def ssd_bwd(xbc, proj, decay, st, dyn, dt_bias, a_log, d_skip, gate_norm, *, z_off, dt_off, name, carry=None):
    t = xbc.shape[0]
    nb = t // BLK

    def kern(x_ref, b_ref, c_ref, z_ref, dt_ref, dtb_ref, alog_ref, dsk_ref, gn_ref, e_ref, d_ref,
             et_ref, st_ref, dyn_ref,
             dxbc_ref, dz_ref, draw_ref, dgn_ref, ddsk_ref, ddtb_ref, dalog_ref,
             dp_ref, tr_ref):
        n = pl.program_id(0)
        blk = nb - 1 - n

        @pl.when(n == 0)
        def _():
            for r_ in (dp_ref, dgn_ref, ddsk_ref, ddtb_ref, dalog_ref):
                r_[...] = jnp.zeros_like(r_)

        bgs = [b_ref[:, g * BLK:(g + 1) * BLK] for g in range(SSD_GROUPS)]
        cgs = [c_ref[:, g * BLK:(g + 1) * BLK] for g in range(SSD_GROUPS)]
        cbs = [_dot_nt(cgs[g], bgs[g]) for g in range(SSD_GROUPS)]
        zs = [_dot(cgs[g], st_ref[0, g]) for g in range(SSD_GROUPS)]
        dwxs = [_dot(bgs[g], dp_ref[g]) for g in range(SSD_GROUPS)]
        valid = ((blk * BLK + _rows((BLK, BLK))) >= PAD) & (_lanes((BLK, BLK)) < SSD_HEADS)
        pre = dt_ref[...] + dtb_ref[...]
        av = -jnp.exp(alog_ref[...])
        dtp, cs, ecs, w, dec = _load_decay(d_ref)
        dtp_c, ecs_c, w_c, dec_c = _expand_heads(dtp, ecs, w, dec, e_ref[...])
        xv = x_ref[...]
        xdt = xv * dtp_c
        wx = w_c * xdt
        cst = cs.T
        tril = _rows((BLK, BLK)) >= _lanes((BLK, BLK))
        lane = _lanes((BLK, BLK))
        rowi = _rows((BLK, BLK))
        low = lane < HEAD_DIM
        dcs = jnp.zeros((BLK, BLK), F32)
        dcst = jnp.zeros((BLK, BLK), F32)
        for g in range(SSD_GROUPS):
            gs = slice(g * GW, (g + 1) * GW)
            bg, cg = bgs[g], cgs[g]
            pg = st_ref[0, g]
            dpn = dp_ref[g]
            xg = xv[:, gs]
            ydiag, lm = _ssd_group_fwd(g, xdt, cs, cst, cbs[g], tril, low)
            yoff = zs[g] * ecs_c[:, gs]
            y = ydiag + yoff + dsk_ref[:, gs] * xg
            zz = z_ref[:, gs]
            sz = _silu(zz)
            yz = y * sz
            r = lax.rsqrt(jnp.mean(yz * yz, axis=-1, keepdims=True) + EPS)
            yhat = yz * r
            dynv = dyn_ref[:, gs].astype(F32)
            gy = dynv * gn_ref[:, gs]
            dgn_ref[:, gs] += jnp.sum(dynv * yhat, axis=0, keepdims=True)
            dyz = r * (gy - yhat * jnp.mean(gy * yhat, axis=-1, keepdims=True))
            dy = dyz * sz
            dz_ref[:, gs] = (dyz * y * _silu_grad(zz)).astype(dz_ref.dtype)
            tr_ref[EXP_ROWS:RED_ROWS, gs] = jnp.broadcast_to(
                jnp.sum(dy * xg, axis=0, keepdims=True), (8, GW))
            dx = dsk_ref[:, gs] * dy
            dwx = dwxs[g]
            dxdt = w_c[:, gs] * dwx
            tr_ref[0:BLK, gs] = dwx * wx[:, gs]
            dbg = _dot_nt(wx[:, gs], dpn)
            dzo = ecs_c[:, gs] * dy
            tr_ref[BLK:2 * BLK, gs] = dy * yoff
            dcg = _dot_nt(dzo, pg)
            dp_ref[g] = dec_c[:, gs] * dpn + _dot_tn(cg, dzo)
            tr_ref[3 * BLK:EXP_ROWS, gs] = jnp.broadcast_to(
                jnp.sum(dpn * pg, axis=0, keepdims=True), (8, GW))
            dyh = [jnp.where(low == (k % 2 == 0), dy[:, (k // 2) * BLK:(k // 2 + 1) * BLK], 0.0) for k in range(4)]
            dms = [_dot_nt(dyh[k], xdt[:, g * GW + (k // 2) * BLK:g * GW + (k // 2 + 1) * BLK]) for k in range(4)]
            accs = [_dot_tn(lm[k][1], dyh[k]) for k in range(4)]
            dcb = jnp.zeros((BLK, BLK), F32)
            for k in range(4):
                h = 4 * g + k
                lmat, mmat = lm[k]
                dm = jnp.where(tril, dms[k], 0.0)
                nh = dm * mmat
                dcs = dcs + jnp.where(lane == h, jnp.sum(nh, axis=1, keepdims=True), 0.0)
                dcst = dcst - jnp.where(rowi == h, jnp.sum(nh, axis=0, keepdims=True), 0.0)
                dcb = dcb + dm * lmat
            dxdt = dxdt + jnp.concatenate([accs[0] + accs[1], accs[2] + accs[3]], axis=1)
            dcg = dcg + _dot(dcb, bg)
            dbg = dbg + _dot_tn(dcb, cg)
            tr_ref[2 * BLK:3 * BLK, gs] = dxdt * xg
            dxbc_ref[:, gs] = dx + dxdt * dtp_c[:, gs]
            dxbc_ref[:, D_SSM + g * BLK:D_SSM + (g + 1) * BLK] = dbg
            dxbc_ref[:, D_SSM + 1024 + g * BLK:D_SSM + 1024 + (g + 1) * BLK] = dcg
        red = _dot(tr_ref[...], et_ref[...])
        r1, r2, r3 = red[0:BLK], red[BLK:2 * BLK], red[2 * BLK:3 * BLK]
        ddec = jnp.max(red[3 * BLK:EXP_ROWS], axis=0, keepdims=True)
        ddsk_ref[...] += jnp.max(red[EXP_ROWS:RED_ROWS], axis=0, keepdims=True)
        dcs = dcs + dcst.T - r1 + r2
        dcs_last = jnp.sum(r1, axis=0, keepdims=True) + ddec * dec
        dcs = dcs + jnp.where(rowi == BLK - 1, dcs_last, 0.0)
        dda = _cumsum_rev(dcs)
        ddtp = r3 + dda * av
        dalog_ref[...] += jnp.sum(dda * dtp, axis=0, keepdims=True) * av
        draw = jnp.where(valid, ddtp * _sigmoid(pre), 0.0)
        ddtb_ref[...] += jnp.sum(draw, axis=0, keepdims=True)
        draw_ref[...] = draw.astype(draw_ref.dtype)

    vec = lambda w_: pl.BlockSpec((1, w_), lambda n: (0, 0))
    rb = lambda w_: pl.BlockSpec((BLK, w_), lambda n: (nb - 1 - n, 0))
    e = _head_expand()
    res = _call(
        kern, (xbc, xbc, xbc, proj, proj, _pad128(dt_bias), _pad128(a_log),
               jnp.repeat(d_skip, HEAD_DIM).reshape(1, D_SSM), gate_norm.reshape(1, D_SSM), e, decay, e.T, st, dyn),
        name=name, grid=(nb,),
        in_specs=_ssd_specs(t, z_off, dt_off, True)
        + [pl.BlockSpec((D_SSM, BLK), lambda n: (0, 0)),
           pl.BlockSpec((1, SSD_GROUPS, BLK, GW), lambda n: (nb - 1 - n, 0, 0, 0)),
           rb(D_SSM)],
        out_specs=[rb(2 * D_SSM), rb(D_SSM), rb(BLK), vec(D_SSM), vec(BLK), vec(BLK), vec(BLK)],
        out_shape=[jax.ShapeDtypeStruct((t, 2 * D_SSM), F32), jax.ShapeDtypeStruct((t, D_SSM), MXU),
                   jax.ShapeDtypeStruct((t, BLK), MXU), jax.ShapeDtypeStruct((1, D_SSM), F32),
                   jax.ShapeDtypeStruct((1, BLK), F32), jax.ShapeDtypeStruct((1, BLK), F32),
                   jax.ShapeDtypeStruct((1, BLK), F32)],
        scratch_shapes=[pltpu.VMEM((SSD_GROUPS, BLK, GW), F32), pltpu.VMEM((RED_ROWS, D_SSM), F32)],
        sem=("arbitrary",), carry=carry)
    dxbc, dz, draw, dgn, ddsk, ddtb, dalog = res[:7]
    return [dxbc, dz, draw, dgn[0], ddsk[0, :SSD_HEADS], ddtb[0, :SSD_HEADS], dalog[0, :SSD_HEADS]] + res[7:]


def loss_fwd_bwd(h, target, *, name):
    t, d = h.shape
    nb = t // BLK

    def kern(h_ref, t_ref, loss_ref, dh_ref):
        n = pl.program_id(0)
        err = jnp.where(n > 0, h_ref[...] - t_ref[...], 0.0)
        dh_ref[...] = err * (1.0 / d)
        part = (0.5 / d) * jnp.sum(jnp.sum(err * err, axis=1, keepdims=True), axis=0, keepdims=True)

        @pl.when(n == 0)
        def _():
            loss_ref[...] = part

        @pl.when(n > 0)
        def _():
            loss_ref[...] += part

    return pl.pallas_call(
        kern, name=name, grid=(nb,),
        in_specs=[pl.BlockSpec((BLK, d), lambda n: (n, 0)),
                  pl.BlockSpec((BLK, d), lambda n: (jnp.maximum(n - 1, 0), 0))],
        out_specs=[pl.BlockSpec((1, 1), lambda n: (0, 0)), pl.BlockSpec((BLK, d), lambda n: (n, 0))],
        out_shape=[jax.ShapeDtypeStruct((1, 1), F32), jax.ShapeDtypeStruct((t, d), F32)],
        compiler_params=_cp("arbitrary"),
    )(h, target)


def _ew_tile(r, c):
    cap = max(16, (256 * 1024) // c)
    best = None
    for tr in range(16, min(r, cap) + 1, 16):
        if r % tr == 0:
            best = tr
    return best if best is not None else r


def adamw(parts, w, m, v, *, name):
    npart, r, c = parts.shape
    tr = _ew_tile(r, c)

    def kern(p_ref, w_ref, m_ref, v_ref, g_ref, d_ref, m2_ref, v2_ref):
        g = p_ref[0].astype(F32)
        for k in range(1, npart):
            g = g + p_ref[k].astype(F32)
        m2 = ADAM_B1 * m_ref[...] + (1.0 - ADAM_B1) * g
        v2 = ADAM_B2 * v_ref[...] + (1.0 - ADAM_B2) * (g * g)
        m_hat = m2 / (1.0 - ADAM_B1 ** ADAM_STEP)
        v_hat = v2 / (1.0 - ADAM_B2 ** ADAM_STEP)
        g_ref[...] = g
        d_ref[...] = -ADAM_LR * (m_hat / (jnp.sqrt(v_hat) + ADAM_EPS) + ADAM_WD * w_ref[...])
        m2_ref[...] = m2
        v2_ref[...] = v2

    row = pl.BlockSpec((tr, c), lambda i: (i, 0))
    sds = jax.ShapeDtypeStruct((r, c), F32)
    return pl.pallas_call(
        kern, name=name, grid=(r // tr,),
        in_specs=[pl.BlockSpec((npart, tr, c), lambda i: (0, i, 0)), row, row, row],
        out_specs=[row, row, row, row], out_shape=[sds, sds, sds, sds],
        compiler_params=_cp("parallel"),
    )(parts, w, m, v)


def pair_add(p, land, *, name):
    _, r, c = p.shape
    tr = _ew_tile(r, c)
    core = lax.axis_index("c").astype(jnp.int32).reshape(1)

    def kern(c_ref, p_ref, l_ref, o_ref):
        o_ref[...] = (p_ref[...] + l_ref[...]).astype(o_ref.dtype)

    return pl.pallas_call(
        kern, name=name,
        grid_spec=pltpu.PrefetchScalarGridSpec(
            num_scalar_prefetch=1, grid=(4, r // tr),
            in_specs=[pl.BlockSpec((1, tr, c), lambda k, i, c_ref: (2 * k + c_ref[0], i, 0)),
                      pl.BlockSpec((1, tr, c), lambda k, i, c_ref: (k, i, 0))],
            out_specs=pl.BlockSpec((1, tr, c), lambda k, i, c_ref: (k, i, 0))),
        out_shape=jax.ShapeDtypeStruct((4, r, c), BF16),
        compiler_params=_cp("parallel", "parallel"),
    )(core, p, land)


def _me():
    return lax.axis_index("x"), lax.axis_index("y"), lax.axis_index("c")


def all_gather(xs, *, name):
    n = len(xs)

    def body(*refs):
        x_refs, out_refs = refs[:n], refs[n:2 * n]
        send_sems, recv_sems, local_sems = refs[2 * n:]
        mx, my, mc = _me()
        me, sib = (mx, my, mc), (mx, my, 1 - mc)
        chips = [(1 - mx, my), (mx, 1 - my), (1 - mx, 1 - my)]

        def rows(i, px, py, pc):
            return out_refs[i].at[4 * px + 2 * py + pc]

        def copy(i, k, block, to, src=None):
            return pltpu.make_async_remote_copy(
                src_ref=rows(i, *block) if src is None else src, dst_ref=rows(i, *block),
                send_sem=send_sems.at[7 * i + k], recv_sem=recv_sems.at[7 * i + k],
                device_id=to, device_id_type=MESH)

        mine = [pltpu.make_async_copy(x_refs[i], rows(i, *me), local_sems.at[i]) for i in range(n)]
        first = []
        for i in range(n):
            mine[i].start()
            first.append(copy(i, 0, me, sib, src=x_refs[i]))
            first += [copy(i, 1 + j, me, (*chip, mc), src=x_refs[i]) for j, chip in enumerate(chips)]
        for cp in first:
            cp.start()
        passed = []
        for i in range(n):
            for j, chip in enumerate(chips):
                copy(i, 1 + j, (*chip, mc), me).wait_recv()
                passed.append(copy(i, 4 + j, (*chip, mc), sib))
                passed[-1].start()
        for i in range(n):
            copy(i, 0, sib, me).wait_recv()
            for j, chip in enumerate(chips):
                copy(i, 4 + j, (*chip, 1 - mc), me).wait_recv()
        for cp in first + passed:
            cp.wait_send()
        for cp in mine:
            cp.wait()

    return pl.pallas_call(
        body, name=name,
        out_shape=[jax.ShapeDtypeStruct((N_DEV,) + x.shape, x.dtype) for x in xs],
        in_specs=[ANY] * n, out_specs=[ANY] * n,
        scratch_shapes=[pltpu.SemaphoreType.DMA((7 * n,)), pltpu.SemaphoreType.DMA((7 * n,)),
                        pltpu.SemaphoreType.DMA((n,))],
    )(*xs)


def pair_exchange(ps, *, name):
    n = len(ps)

    def body(*refs):
        p_refs, out_refs = refs[:n], refs[n:2 * n]
        send_sems, recv_sems = refs[2 * n:]
        mx, my, mc = _me()
        cps = [pltpu.make_async_remote_copy(
            src_ref=p_refs[i].at[2 * k + (1 - mc)], dst_ref=out_refs[i].at[k],
            send_sem=send_sems.at[4 * i + k], recv_sem=recv_sems.at[4 * i + k],
            device_id=(mx, my, 1 - mc), device_id_type=MESH) for i in range(n) for k in range(4)]
        for cp in cps:
            cp.start()
        for cp in cps:
            cp.wait_recv()
        for cp in cps:
            cp.wait_send()

    return pl.pallas_call(
        body, name=name,
        out_shape=[jax.ShapeDtypeStruct((4,) + p.shape[1:], p.dtype) for p in ps],
        in_specs=[ANY] * n, out_specs=[ANY] * n,
        scratch_shapes=[pltpu.SemaphoreType.DMA((4 * n,)), pltpu.SemaphoreType.DMA((4 * n,))],
    )(*ps)


def chip_exchange(qs, *, name):
    n = len(qs)

    def body(*refs):
        q_refs, out_refs = refs[:n], refs[n:2 * n]
        send_sems, recv_sems, local_sems = refs[2 * n:]
        mx, my, mc = _me()
        mine = 2 * mx + my
        chips = [(1 - mx, my), (mx, 1 - my), (1 - mx, 1 - my)]
        local, sends, recvs = [], [], []
        for i in range(n):
            local.append(pltpu.make_async_copy(q_refs[i].at[mine], out_refs[i].at[mine], local_sems.at[i]))
            for k, (px, py) in enumerate(chips):
                sems = dict(send_sem=send_sems.at[3 * i + k], recv_sem=recv_sems.at[3 * i + k],
                            device_id=(px, py, mc), device_id_type=MESH)
                sends.append(pltpu.make_async_remote_copy(
                    src_ref=q_refs[i].at[2 * px + py], dst_ref=out_refs[i].at[mine], **sems))
                recvs.append(pltpu.make_async_remote_copy(
                    src_ref=q_refs[i].at[mine], dst_ref=out_refs[i].at[2 * px + py], **sems))
        for cp in local + sends:
            cp.start()
        for cp in recvs:
            cp.wait_recv()
        for cp in sends:
            cp.wait_send()
        for cp in local:
            cp.wait()

    return pl.pallas_call(
        body, name=name,
        out_shape=[jax.ShapeDtypeStruct(q.shape, q.dtype) for q in qs],
        in_specs=[ANY] * n, out_specs=[ANY] * n,
        scratch_shapes=[pltpu.SemaphoreType.DMA((3 * n,)), pltpu.SemaphoreType.DMA((3 * n,)),
                        pltpu.SemaphoreType.DMA((n,))],
    )(*qs)


class _Carry:
    def __init__(self, inputs, out_shapes, sems, start, finish):
        self.inputs, self.out_shapes, self.sems = list(inputs), list(out_shapes), list(sems)
        self.start, self.finish = start, finish


def _call(kern, args, *, name, grid, in_specs, out_specs, out_shape, scratch_shapes=(), sem, carry=None):
    in_specs, out_specs, out_shape = list(in_specs), list(out_specs), list(out_shape)
    scratch_shapes = list(scratch_shapes)
    if carry is None:
        return list(pl.pallas_call(
            kern, name=name, grid=grid, in_specs=in_specs, out_specs=out_specs, out_shape=out_shape,
            scratch_shapes=scratch_shapes, compiler_params=_cp(*sem))(*args))
    ni, no, ns = len(in_specs), len(out_specs), len(scratch_shapes)
    ci, co = len(carry.inputs), len(carry.out_shapes)

    def body(*refs):
        o0 = ni + ci
        s0 = o0 + no + co
        ids = [pl.program_id(d) for d in range(len(grid))]
        first = functools.reduce(jnp.logical_and, [i == 0 for i in ids])
        last = functools.reduce(jnp.logical_and, [i == g - 1 for i, g in zip(ids, grid)])
        cin, cout, sems = refs[ni:o0], refs[o0 + no:s0], refs[s0 + ns:]

        @pl.when(first)
        def _():
            carry.start(cin, cout, sems)

        kern(*refs[:ni], *refs[o0:o0 + no], *refs[s0:s0 + ns])

        @pl.when(last)
        def _():
            carry.finish(cin, cout, sems)

    return list(pl.pallas_call(
        body, name=name, grid=grid, in_specs=in_specs + [ANY] * ci, out_specs=out_specs + [ANY] * co,
        out_shape=out_shape + carry.out_shapes, scratch_shapes=scratch_shapes + carry.sems,
        compiler_params=_cp(*(["arbitrary"] * len(grid))))(*args, *carry.inputs))


def merge_carries(cs):
    def split(seq, counts):
        out, off = [], 0
        for k in counts:
            out.append(seq[off:off + k])
            off += k
        return out

    def parts(cin, cout, sems):
        return zip(cs, split(cin, [len(c.inputs) for c in cs]), split(cout, [len(c.out_shapes) for c in cs]),
                   split(sems, [len(c.sems) for c in cs]))

    def start(cin, cout, sems):
        for c, i, o, s in parts(cin, cout, sems):
            c.start(i, o, s)

    def finish(cin, cout, sems):
        for c, i, o, s in parts(cin, cout, sems):
            c.finish(i, o, s)

    return _Carry(sum((c.inputs for c in cs), []), sum((c.out_shapes for c in cs), []),
                  sum((c.sems for c in cs), []), start, finish)


def gather_carry(xs):
    n = len(xs)

    def copies(cin, cout, sems, with_recv=True):
        mx, my, mc = _me()
        me = 4 * mx + 2 * my + mc
        peers = [(mx, my, 1 - mc), (1 - mx, my, mc), (mx, 1 - my, mc), (1 - mx, 1 - my, mc)]
        local, send, recv = [], [], []
        for i in range(n):
            local.append(pltpu.make_async_copy(cin[i], cout[i].at[me], sems[2].at[i]))
            for k, peer in enumerate(peers):
                common = dict(send_sem=sems[0].at[4 * i + k], recv_sem=sems[1].at[4 * i + k],
                              device_id=peer, device_id_type=MESH)
                send.append(pltpu.make_async_remote_copy(src_ref=cin[i], dst_ref=cout[i].at[me], **common))
                if with_recv:
                    recv.append(pltpu.make_async_remote_copy(
                        src_ref=cin[i], dst_ref=cout[i].at[4 * peer[0] + 2 * peer[1] + peer[2]], **common))
        return local, send, recv

    def start(cin, cout, sems):
        local, send, _ = copies(cin, cout, sems, with_recv=False)
        for cp in local + send:
            cp.start()

    def finish(cin, cout, sems):
        local, send, recv = copies(cin, cout, sems)
        for cp in recv:
            cp.wait_recv()
        for cp in send:
            cp.wait_send()
        for cp in local:
            cp.wait()

    return _Carry(xs, [jax.ShapeDtypeStruct((N_DEV,) + x.shape, x.dtype) for x in xs],
                  [pltpu.SemaphoreType.DMA((4 * n,)), pltpu.SemaphoreType.DMA((4 * n,)),
                   pltpu.SemaphoreType.DMA((n,))], start, finish)


def gather_relay(outs, *, name):
    n = len(outs)

    def body(*refs):
        bufs = refs[n:2 * n]
        send_sems, recv_sems = refs[2 * n:]
        mx, my, mc = _me()
        chips = [(1 - mx, my), (mx, 1 - my), (1 - mx, 1 - my)]
        send, recv = [], []
        for i in range(n):
            for j, (px, py) in enumerate(chips):
                common = dict(send_sem=send_sems.at[3 * i + j], recv_sem=recv_sems.at[3 * i + j],
                              device_id=(mx, my, 1 - mc), device_id_type=MESH)
                mine = bufs[i].at[4 * px + 2 * py + mc]
                send.append(pltpu.make_async_remote_copy(src_ref=mine, dst_ref=mine, **common))
                recv.append(pltpu.make_async_remote_copy(
                    src_ref=mine, dst_ref=bufs[i].at[4 * px + 2 * py + (1 - mc)], **common))
        for cp in send:
            cp.start()
        for cp in recv:
            cp.wait_recv()
        for cp in send:
            cp.wait_send()

    return pl.pallas_call(
        body, name=name, out_shape=[jax.ShapeDtypeStruct(o.shape, o.dtype) for o in outs],
        in_specs=[ANY] * n, out_specs=[ANY] * n, input_output_aliases={i: i for i in range(n)},
        scratch_shapes=[pltpu.SemaphoreType.DMA((3 * n,)), pltpu.SemaphoreType.DMA((3 * n,))],
    )(*outs)


def pair_carry(ps):
    n = len(ps)

    def copies(cin, cout, sems):
        mx, my, mc = _me()
        return [pltpu.make_async_remote_copy(
            src_ref=cin[i].at[2 * k + (1 - mc)], dst_ref=cout[i].at[k],
            send_sem=sems[0].at[4 * i + k], recv_sem=sems[1].at[4 * i + k],
            device_id=(mx, my, 1 - mc), device_id_type=MESH) for i in range(n) for k in range(4)]

    def start(cin, cout, sems):
        for cp in copies(cin, cout, sems):
            cp.start()

    def finish(cin, cout, sems):
        cps = copies(cin, cout, sems)
        for cp in cps:
            cp.wait_recv()
        for cp in cps:
            cp.wait_send()

    return _Carry(ps, [jax.ShapeDtypeStruct((4,) + p.shape[1:], p.dtype) for p in ps],
                  [pltpu.SemaphoreType.DMA((4 * n,)), pltpu.SemaphoreType.DMA((4 * n,))], start, finish)


def chip_carry(qs):
    n = len(qs)

    def copies(cin, cout, sems, with_recv=True):
        mx, my, mc = _me()
        mine = 2 * mx + my
        chips = [(1 - mx, my), (mx, 1 - my), (1 - mx, 1 - my)]
        local, send, recv = [], [], []
        for i in range(n):
            local.append(pltpu.make_async_copy(cin[i].at[mine], cout[i].at[mine], sems[2].at[i]))
            for k, (px, py) in enumerate(chips):
                common = dict(send_sem=sems[0].at[3 * i + k], recv_sem=sems[1].at[3 * i + k],
                              device_id=(px, py, mc), device_id_type=MESH)
                send.append(pltpu.make_async_remote_copy(
                    src_ref=cin[i].at[2 * px + py], dst_ref=cout[i].at[mine], **common))
                if with_recv:
                    recv.append(pltpu.make_async_remote_copy(
                        src_ref=cin[i].at[mine], dst_ref=cout[i].at[2 * px + py], **common))
        return local, send, recv

    def start(cin, cout, sems):
        local, send, _ = copies(cin, cout, sems, with_recv=False)
        for cp in local + send:
            cp.start()

    def finish(cin, cout, sems):
        local, send, recv = copies(cin, cout, sems)
        for cp in recv:
            cp.wait_recv()
        for cp in send:
            cp.wait_send()
        for cp in local:
            cp.wait()

    return _Carry(qs, [jax.ShapeDtypeStruct(q.shape, q.dtype) for q in qs],
                  [pltpu.SemaphoreType.DMA((3 * n,)), pltpu.SemaphoreType.DMA((3 * n,)),
                   pltpu.SemaphoreType.DMA((n,))], start, finish)


WEIGHTS = [
    "meta_tokens", "l0_mix_pre_norm", "l0_mix_post_norm", "l0_w_in", "l0_lru_conv_w", "l0_lru_conv_b",
    "l0_lru_w_a", "l0_lru_b_a", "l0_lru_w_x", "l0_lru_b_x", "l0_lru_lambda", "l0_attn_sinks", "l0_w_out",
    "l0_ffn_pre_norm", "l0_ffn_post_norm", "l0_ffn_w_up", "l0_ffn_conv_w", "l0_ffn_conv_b", "l0_ffn_w_down",
    "l1_mix_pre_norm", "l1_mix_post_norm", "l1_w_in", "l1_ssm_conv_w", "l1_ssm_conv_b", "l1_dt_bias",
    "l1_a_log", "l1_d_skip", "l1_gate_norm", "l1_w_out", "l1_ffn_pre_norm", "l1_ffn_post_norm",
    "l1_ffn_w_up", "l1_ffn_conv_w", "l1_ffn_conv_b", "l1_ffn_w_down",
]
INPUTS = (["x"] + WEIGHTS + ["loss_target"] + ["m_" + n for n in WEIGHTS] + ["v_" + n for n in WEIGHTS])

MATS = {"l0_w_in": ("col", (1024, 3328)), "l0_w_out": ("row", (2048, 1024)),
        "l0_ffn_w_up": ("col", (1024, 5632)), "l0_ffn_w_down": ("row", (2816, 1024)),
        "l1_w_in": ("col", (1024, 6176)), "l1_w_out": ("row", (2048, 1024)),
        "l1_ffn_w_up": ("col", (1024, 5632)), "l1_ffn_w_down": ("row", (2816, 1024))}
SMALL_SHARDED = {"meta_tokens": ("col", (16, 1024)), "l0_lru_conv_w": ("col", (4, 1024)),
                 "l0_ffn_conv_w": ("col", (3, 5632)), "l1_ssm_conv_w": ("col", (4, 4096)),
                 "l1_ffn_conv_w": ("col", (3, 5632))}
SHARDED = {**MATS, **SMALL_SHARDED}
REPLICATED = [n for n in WEIGHTS if n not in SHARDED]
SHAPES = {n: ((8, BLK, BLK) if n.endswith(("lru_w_a", "lru_w_x")) else (N_Q_HEADS,) if n.endswith("attn_sinks")
              else (2 * D_FF,) if n.endswith("ffn_conv_b") else (2 * D_SSM,) if n.endswith("ssm_conv_b")
              else (SSD_HEADS,) if n.endswith(("dt_bias", "a_log", "d_skip")) else (D_SSM,) if n.endswith("gate_norm")
              else (D_MODEL,)) for n in REPLICATED}
PACK_W = 1024
SMALL_W = 128


def _shard_shape(name):
    kind, (r, c) = SHARDED[name]
    return (r, c // N_DEV) if kind == "col" else (r // N_DEV, c)


def _rows_of(numel, width):
    return -(-numel // width)


def _to_rows(a, width):
    flat = a.reshape(-1)
    rows = _rows_of(flat.shape[0], width)
    return jnp.pad(flat, (0, rows * width - flat.shape[0])).reshape(rows, width)


def _pack(arrs, width, total_rows):
    slab = jnp.concatenate([_to_rows(a, width) for a in arrs], axis=0)
    return jnp.pad(slab, ((0, total_rows - slab.shape[0]), (0, 0)))


def _unpack(slab, shapes, width):
    out, off = [], 0
    for shp in shapes:
        numel = math.prod(shp)
        rows = _rows_of(numel, width)
        out.append(slab[off:off + rows].reshape(-1)[:numel].reshape(shp))
        off += rows
    return out


def _round_up(n, m):
    return -(-n // m) * m


def _by_dest(name, g):
    kind, (r, c) = SHARDED[name]
    if kind == "col":
        return g.reshape(r, N_DEV, c // N_DEV).transpose(1, 0, 2)
    return g.reshape(N_DEV, r // N_DEV, c)


def _from_shards(name, blocks):
    kind, (r, c) = SHARDED[name]
    return blocks.transpose(1, 0, 2).reshape(r, c) if kind == "col" else blocks.reshape(r, c)


L1_IN_PAD = 6272


def _ffn_fwd(h, a, w, pfx):
    u, ut = rmsnorm_fwd(h, a[pfx + "ffn_pre_norm"], out_dtype=MXU, name=pfx + "ffn_pre", with_t=True)
    up = matmul(u, w[pfx + "ffn_w_up"], name=pfx + "ffn_up")
    act, act_t = dwconv_fwd(up, a[pfx + "ffn_conv_w"], a[pfx + "ffn_conv_b"], mode="geglu", x_off=0,
                            c_out=D_FF, cblk=256, out_dtype=MXU, name=pfx + "ffn_act", with_t=True)
    down = matmul(act, w[pfx + "ffn_w_down"], name=pfx + "ffn_down")
    out = rmsnorm_fwd(down, a[pfx + "ffn_post_norm"], res=h, out_dtype=F32, name=pfx + "ffn_post")
    return out, (h, ut, up, act_t, down)


def _dx_and_pair_stage(names, g, a_list, b, *, name):
    parts = [_by_dest(n, g[n]) for n in names]
    out, from_sibling = matmul_cat(a_list, b, trans_b=True, name=name, carry=pair_carry(parts))
    return out, [pair_add(p, l, name="rs_pair_add_" + n) for n, p, l in zip(names, parts, from_sibling)]


def _ffn_bwd(dh, saved, a, w, pfx, g, carry=None):
    h, ut, up, act_t, down = saved
    dd, g[pfx + "ffn_post_norm"] = rmsnorm_bwd(down, a[pfx + "ffn_post_norm"], dh, out_dtype=MXU,
                                               name=pfx + "ffn_post_bwd")
    dact = matmul(dd, w[pfx + "ffn_w_down"], trans_b=True, name=pfx + "ffn_down_dx")
    g[pfx + "ffn_w_down"] = matmul(act_t, dd, name=pfx + "ffn_down_dw")
    dups, g[pfx + "ffn_conv_w"], g[pfx + "ffn_conv_b"], carried = dwconv_bwd(
        up, a[pfx + "ffn_conv_w"], a[pfx + "ffn_conv_b"], dact, mode="geglu", x_off=0, c_out=D_FF,
        cblk=256, name=pfx + "ffn_act_bwd", carry=carry)
    g[pfx + "ffn_w_up"] = jnp.concatenate(
        [matmul(ut, d, name=pfx + "ffn_up_dw%d" % i) for i, d in enumerate(dups)], axis=1)
    du, q = _dx_and_pair_stage([pfx + "ffn_w_down", pfx + "ffn_w_up"], g, dups, w[pfx + "ffn_w_up"],
                               name=pfx + "ffn_up_dx")
    dh_in, g[pfx + "ffn_pre_norm"] = rmsnorm_bwd(h, a[pfx + "ffn_pre_norm"], du, res=dh, out_dtype=F32,
                                                 name=pfx + "ffn_pre_bwd")
    return dh_in, carried, q


GATHER_EARLY = ["l0_w_out", "l0_ffn_w_up", "l0_ffn_w_down"]
GATHER_LATE = ["l1_w_in", "l1_w_out", "l1_ffn_w_up", "l1_ffn_w_down"]
RS_L1_FFN = ["l1_ffn_w_down", "l1_ffn_w_up"]
RS_L1_MIX = ["l1_w_out", "l1_w_in"]
RS_L0_FFN = ["l0_ffn_w_down", "l0_ffn_w_up"]
RS_LAST = ["l0_w_out", "l0_w_in", "l0_lru_conv_w", "l0_ffn_conv_w", "l1_ssm_conv_w", "l1_ffn_conv_w"]


REPL_LATE = ["l0_attn_sinks", "l0_mix_pre_norm"]
REPL_EARLY = [n for n in REPLICATED if n not in REPL_LATE]


def _local_step(a, shards):
    x = a["x"][0]
    seq = x.shape[0]
    h0 = jnp.concatenate([jnp.zeros((PAD, D_MODEL), F32), a["meta_tokens"], x], axis=0)
    g, landed = {}, {}

    u0, u0t, w_in0 = rmsnorm_fwd(h0, a["l0_mix_pre_norm"], out_dtype=MXU, name="l0_mix_pre", with_t=True,
                                 carry=gather_carry([shards["l0_w_in"]]))
    w = {"l0_w_in": _from_shards("l0_w_in", gather_relay([w_in0], name="gather_relay_first")[0])}
    proj0 = matmul(u0, w["l0_w_in"], name="l0_in")
    lru = (a["l0_lru_conv_w"], a["l0_lru_conv_b"], a["l0_lru_w_a"], a["l0_lru_b_a"], a["l0_lru_w_x"],
           a["l0_lru_b_x"], a["l0_lru_lambda"])
    ya, ya_t, hl, *early = lru_fwd(proj0, *lru, gate_off=0, xr_off=1024, name="l0_lru",
                                   carry=gather_carry([shards[n] for n in GATHER_EARLY]))
    yb, *late = attn_fwd(proj0, a["l0_attn_sinks"], q_off=2048, k_off=3072, v_off=3200, name="l0_attn",
                         carry=gather_carry([shards[n] for n in GATHER_LATE]))
    relayed = gather_relay(early + late, name="gather_relay")
    w = dict(w, **{n: _from_shards(n, blocks) for n, blocks in zip(GATHER_EARLY + GATHER_LATE, relayed)})
    w["l1_w_in"] = jnp.pad(w["l1_w_in"], ((0, 0), (0, L1_IN_PAD - w["l1_w_in"].shape[1])))
    o0 = matmul_cat([ya, yb], w["l0_w_out"], name="l0_out")
    h1 = rmsnorm_fwd(o0, a["l0_mix_post_norm"], res=h0, out_dtype=F32, name="l0_mix_post")
    h2, ffn0 = _ffn_fwd(h1, a, w, "l0_")

    u2, u2t = rmsnorm_fwd(h2, a["l1_mix_pre_norm"], out_dtype=MXU, name="l1_mix_pre", with_t=True)
    proj1 = matmul(u2, w["l1_w_in"], name="l1_in")
    xbc = dwconv_fwd(proj1, a["l1_ssm_conv_w"], a["l1_ssm_conv_b"], mode="silu", x_off=D_SSM,
                     c_out=2 * D_SSM, cblk=512, out_dtype=F32, name="l1_ssm_conv")
    ssd = (a["l1_dt_bias"], a["l1_a_log"], a["l1_d_skip"], a["l1_gate_norm"])
    decay = ssd_decay(proj1, a["l1_dt_bias"], a["l1_a_log"], dt_off=3 * D_SSM, name="l1_ssd_decay")
    yn, yn_t, st = ssd_fwd(xbc, proj1, decay, *ssd, z_off=0, dt_off=3 * D_SSM, name="l1_ssd")
    o1 = matmul(yn, w["l1_w_out"], name="l1_out")
    h3 = rmsnorm_fwd(o1, a["l1_mix_post_norm"], res=h2, out_dtype=F32, name="l1_mix_post")
    h4, ffn1 = _ffn_fwd(h3, a, w, "l1_")

    loss, dh4 = loss_fwd_bwd(h4, a["loss_target"][0], name="loss")

    dh3, _, q_l1_ffn = _ffn_bwd(dh4, ffn1, a, w, "l1_", g)
    do1, g["l1_mix_post_norm"] = rmsnorm_bwd(o1, a["l1_mix_post_norm"], dh3, out_dtype=MXU,
                                             name="l1_mix_post_bwd")
    dyn = matmul(do1, w["l1_w_out"], trans_b=True, name="l1_out_dx")
    g["l1_w_out"] = matmul(yn_t, do1, name="l1_out_dw")
    (dxbc, dz, draw, g["l1_gate_norm"], g["l1_d_skip"], g["l1_dt_bias"], g["l1_a_log"], *got) = ssd_bwd(
        xbc, proj1, decay, st, dyn, *ssd, z_off=0, dt_off=3 * D_SSM, name="l1_ssd_bwd",
        carry=chip_carry(q_l1_ffn))
    landed.update(zip(RS_L1_FFN, got))
    (dxin,), g["l1_ssm_conv_w"], g["l1_ssm_conv_b"], _ = dwconv_bwd(
        proj1, a["l1_ssm_conv_w"], a["l1_ssm_conv_b"], dxbc, mode="silu", x_off=D_SSM,
        c_out=2 * D_SSM, cblk=512, name="l1_ssm_conv_bwd")
    g["l1_w_in"] = jnp.concatenate(
        [matmul(u2t, dz, name="l1_in_dw_z"), matmul(u2t, dxin, name="l1_in_dw_x"),
         matmul(u2t, draw, name="l1_in_dw_dt")[:, :SSD_HEADS]], axis=1)
    du2, q_l1_mix = _dx_and_pair_stage(RS_L1_MIX, g, [dz, dxin, draw], w["l1_w_in"], name="l1_in_dx")
    dh2, g["l1_mix_pre_norm"] = rmsnorm_bwd(h2, a["l1_mix_pre_norm"], du2, res=dh3, out_dtype=F32,
                                            name="l1_mix_pre_bwd")

    dh1, got, q_l0_ffn = _ffn_bwd(dh2, ffn0, a, w, "l0_", g, carry=chip_carry(q_l1_mix))
    landed.update(zip(RS_L1_MIX, got))
    do0, g["l0_mix_post_norm"] = rmsnorm_bwd(o0, a["l0_mix_post_norm"], dh1, out_dtype=MXU,
                                             name="l0_mix_post_bwd")
    dy = matmul(do0, w["l0_w_out"], trans_b=True, name="l0_out_dx")
    g["l0_w_out"] = jnp.concatenate([matmul(ya_t, do0, name="l0_out_dw_a"),
                                     matmul(yb.T, do0, name="l0_out_dw_b")], axis=0)
    (dgate, dxr, g["l0_lru_conv_w"], dcb, g["l0_lru_w_a"], dba, g["l0_lru_w_x"], dbx, dlam) = lru_bwd(
        proj0, hl, dy, *lru, gate_off=0, xr_off=1024, dy_off=0, name="l0_lru_bwd")
    g["l0_lru_conv_b"], g["l0_lru_b_a"], g["l0_lru_b_x"], g["l0_lru_lambda"] = dcb[0], dba[0], dbx[0], dlam[0]
    dq, dk, dv, g["l0_attn_sinks"], *got = attn_bwd(
        proj0, a["l0_attn_sinks"], dy, q_off=2048, k_off=3072, v_off=3200, dy_off=1024, name="l0_attn_bwd",
        carry=merge_carries([chip_carry(q_l0_ffn), gather_carry([_pack_repl(g, REPL_EARLY)])]))
    landed.update(zip(RS_L0_FFN, got[:2]))
    repl_early = gather_relay(got[2:], name="gather_relay_small_grads")[0]
    dproj0 = [dgate, dxr, dq, dk, dv]
    g["l0_w_in"] = jnp.concatenate(
        [matmul(u0t, d, name="l0_in_dw%d" % i) for i, d in enumerate(dproj0)], axis=1)
    du0, q_last = _dx_and_pair_stage(RS_LAST, g, dproj0, w["l0_w_in"], name="l0_in_dx")
    dh0, g["l0_mix_pre_norm"], *got = rmsnorm_bwd(h0, a["l0_mix_pre_norm"], du0, res=dh1, out_dtype=F32,
                                                  name="l0_mix_pre_bwd", carry=chip_carry(q_last))
    landed.update(zip(RS_LAST, got))
    g["meta_tokens"] = dh0[PAD:BLK]
    meta = _by_dest("meta_tokens", g["meta_tokens"])
    q_meta = pair_add(meta, pair_exchange([meta], name="rs_pair_meta")[0], name="rs_pair_add_meta_tokens")
    landed["meta_tokens"] = chip_exchange([q_meta], name="rs_chip_meta")[0]
    return loss[0, 0], dh0[BLK:].reshape(1, seq, D_MODEL), g, landed, repl_early


def _repl_rows(names):
    return _round_up(sum(_rows_of(math.prod(SHAPES[n]), SMALL_W) for n in names), 16)


def _pack_repl(vals, names):
    return _pack([vals[n] for n in names], SMALL_W, _repl_rows(names))


def kernel(*args):
    a = dict(zip(INPUTS, args))
    first = list(SMALL_SHARDED)
    full = {n: _from_shards(n, blocks) for n, blocks in
            zip(first, all_gather([a[n] for n in first], name="gather_first"))}
    shards = {n: a[n].astype(MXU) for n in MATS}
    loss_part, grad_x, g, landed, repl_early = _local_step({**a, **full}, shards)
    loss = lax.psum(loss_part, ("x", "y", "c"))

    sh_out = {n: adamw(landed[n], a[n], a["m_" + n], a["v_" + n], name="adamw_" + n) for n in SHARDED}

    repl_late = all_gather([_pack_repl(g, REPL_LATE)], name="gather_small_grads")[0]
    rp_out = [{}, {}, {}, {}]
    for names, parts in ((REPL_EARLY, repl_early), (REPL_LATE, repl_late)):
        res = adamw(parts, *[_pack_repl({n: a[p + n] for n in names}, names) for p in ("", "m_", "v_")],
                    name="adamw_replicated_%d" % len(names))
        for k in range(4):
            rp_out[k].update(zip(names, _unpack(res[k], [SHAPES[n] for n in names], SMALL_W)))

    outs = [loss, grad_x]
    for k in range(4):
        outs += [sh_out[n][k] if n in SHARDED else rp_out[k][n] for n in WEIGHTS]
    return tuple(outs)
```

```python
import functools
import math

import jax
import jax.numpy as jnp
import numpy as np
from jax import lax
from jax.experimental import pallas as pl
from jax.experimental.pallas import tpu as pltpu

F32 = jnp.float32
BF16 = jnp.bfloat16
MXU = jnp.bfloat16

D_MODEL = 1024
N_META = 16
BLK = 128
PAD = BLK - N_META
D_RNN = 1024
LRU_C = 8.0
N_Q_HEADS = 16
HEAD_DIM = 64
D_SSM = 2048
SSD_HEADS = 32
SSD_GROUPS = 8
D_FF = 2816
EPS = 1e-6
NEG = -1e30
N_DEV = 8

ADAM_LR = 0.001
ADAM_B1 = 0.9
ADAM_B2 = 0.999
ADAM_EPS = 1e-08
ADAM_WD = 0.01
ADAM_STEP = 10

VMEM_LIMIT = 56 * 1024 * 1024
MESH = pl.DeviceIdType.MESH
ANY = pl.BlockSpec(memory_space=pl.ANY)


def _cp(*sem):
    return pltpu.CompilerParams(dimension_semantics=sem, vmem_limit_bytes=VMEM_LIMIT)


def _pick(n, cands):
    for c in cands:
        if n % c == 0:
            return c
    return n


def _dot(a, b):
    return jnp.dot(a.astype(MXU), b.astype(MXU), preferred_element_type=F32)


def _dot_nt(a, b):
    return lax.dot_general(a.astype(MXU), b.astype(MXU), (((1,), (1,)), ((), ())),
                           preferred_element_type=F32)


def _dot_tn(a, b):
    return jnp.dot(a.T.astype(MXU), b.astype(MXU), preferred_element_type=F32)


def _sigmoid(x):
    return 1.0 / (1.0 + jnp.exp(-x))


def _log1p(x):
    u = 1.0 + x
    return jnp.where(u == 1.0, x, jnp.log(u) * (x / jnp.where(u == 1.0, 1.0, u - 1.0)))


def _expm1(x):
    u = jnp.exp(x)
    um1 = u - 1.0
    lg = jnp.log(jnp.where(u > 0.0, u, 1.0))
    safe = (um1 != 0.0) & (um1 != -1.0)
    return jnp.where(um1 == 0.0, x, jnp.where(um1 == -1.0, -1.0,
                                               um1 * (x / jnp.where(safe, lg, 1.0))))


def _softplus(x):
    return jnp.maximum(x, 0.0) + _log1p(jnp.exp(-jnp.abs(x)))


_GC = math.sqrt(2.0 / math.pi)


def _gelu(x):
    t = jnp.tanh(_GC * (x + 0.044715 * x * x * x))
    return 0.5 * x * (1.0 + t)


def _gelu_grad(x):
    t = jnp.tanh(_GC * (x + 0.044715 * x * x * x))
    return 0.5 * (1.0 + t) + 0.5 * x * (1.0 - t * t) * (_GC * (1.0 + 3.0 * 0.044715 * x * x))


def _silu(x):
    return x * _sigmoid(x)


def _silu_grad(x):
    s = _sigmoid(x)
    return s * (1.0 + x * (1.0 - s))


def _rows(shape):
    return lax.broadcasted_iota(jnp.int32, shape, 0)


def _lanes(shape):
    return lax.broadcasted_iota(jnp.int32, shape, 1)


def _shift_down(x, tail, d):
    if d == 0:
        return x
    n = x.shape[0]
    xr = pltpu.roll(x, d, 0)
    tr = pltpu.roll(tail, d, 0)
    first = jnp.where(_rows(tr.shape) < d, tr, xr[0:8])
    return jnp.concatenate([first, xr[8:n]], axis=0)


def _shift_up(x, head, d):
    if d == 0:
        return x
    n = x.shape[0]
    xr = pltpu.roll(x, n - d, 0)
    hr = pltpu.roll(head, 8 - d, 0)
    last = jnp.where(_rows(hr.shape) >= 8 - d, hr, xr[n - 8:n])
    return jnp.concatenate([xr[0:n - 8], last], axis=0)


def _keep(x, valid, s):
    return jnp.where(valid, x, 0.0) if s == 0 else x


def _row_at(x, i):
    return jnp.sum(jnp.where(_rows(x.shape) == i, x, 0.0), axis=0, keepdims=True)


def _scan_fwd(a, u):
    n = a.shape[0]
    ri = _rows(a.shape)
    d = 1
    while d < n:
        m = ri >= d
        us = jnp.where(m, pltpu.roll(u, d, 0), 0.0)
        as_ = jnp.where(m, pltpu.roll(a, d, 0), 1.0)
        u = u + a * us
        a = a * as_
        d *= 2
    return a, u


def _scan_rev(c, u):
    n = c.shape[0]
    ri = _rows(c.shape)
    d = 1
    while d < n:
        m = ri < n - d
        us = jnp.where(m, pltpu.roll(u, n - d, 0), 0.0)
        cs = jnp.where(m, pltpu.roll(c, n - d, 0), 1.0)
        u = u + c * us
        c = c * cs
        d *= 2
    return c, u


def _cumsum_fwd(x):
    n = x.shape[0]
    ri = _rows(x.shape)
    d = 1
    while d < n:
        x = x + jnp.where(ri >= d, pltpu.roll(x, d, 0), 0.0)
        d *= 2
    return x


def _cumsum_rev(x):
    n = x.shape[0]
    ri = _rows(x.shape)
    d = 1
    while d < n:
        x = x + jnp.where(ri < n - d, pltpu.roll(x, n - d, 0), 0.0)
        d *= 2
    return x


MATMUL_VMEM = 40 * 1024 * 1024


def _matmul_tiles(m, n, k, tk, out_bytes):
    best = None
    for tm in (1664, 1408, 1040, 1024, 832, 640, 512, 384, 256, 128):
        if m % tm:
            continue
        for tn in (2048, 1664, 1408, 1024, 896, 640, 512, 384, 256, 128):
            if n % tn:
                continue
            vmem = 2 * (tm * tk * 2 + tk * tn * 2 + tm * tn * out_bytes) + (tm * tn * 4 if k > tk else 0)
            if vmem > MATMUL_VMEM:
                continue
            traffic = (n // tn) * m * k * 2 + (m // tm) * k * n * 2
            if best is None or traffic < best[0]:
                best = (traffic, tm, tn)
    return (best[1], best[2]) if best else (m, n)


def matmul(a, b, *, trans_b=False, out_dtype=F32, name):
    m, k = a.shape
    n = b.shape[0] if trans_b else b.shape[1]
    tk = k if k <= 2048 else _pick(k, (1664, 1408, 1024, 896, 512, 256, 128))
    nk = k // tk
    tm, tn = _matmul_tiles(m, n, k, tk, jnp.dtype(out_dtype).itemsize)

    def product(a_ref, b_ref):
        return _dot_nt(a_ref[...], b_ref[...]) if trans_b else _dot(a_ref[...], b_ref[...])

    def kern_once(a_ref, b_ref, o_ref):
        o_ref[...] = product(a_ref, b_ref).astype(o_ref.dtype)

    def kern_acc(a_ref, b_ref, o_ref, acc_ref):
        kk = pl.program_id(2)

        @pl.when(kk == 0)
        def _():
            acc_ref[...] = product(a_ref, b_ref)

        @pl.when(kk > 0)
        def _():
            acc_ref[...] += product(a_ref, b_ref)

        @pl.when(kk == nk - 1)
        def _():
            o_ref[...] = acc_ref[...].astype(o_ref.dtype)

    b_spec = (pl.BlockSpec((tn, tk), lambda i, j, kk: (j, kk)) if trans_b
              else pl.BlockSpec((tk, tn), lambda i, j, kk: (kk, j)))
    return pl.pallas_call(
        kern_once if nk == 1 else kern_acc, name=name,
        grid=(m // tm, n // tn, nk),
        in_specs=[pl.BlockSpec((tm, tk), lambda i, j, kk: (i, kk)), b_spec],
        out_specs=pl.BlockSpec((tm, tn), lambda i, j, kk: (i, j)),
        out_shape=jax.ShapeDtypeStruct((m, n), out_dtype),
        scratch_shapes=[] if nk == 1 else [pltpu.VMEM((tm, tn), F32)],
        compiler_params=_cp("parallel", "parallel", "arbitrary"),
    )(a, b)


def matmul_cat(a_list, b, *, trans_b=False, out_dtype=F32, name, carry=None):
    m = a_list[0].shape[0]
    ks = [x.shape[1] for x in a_list]
    ktot = sum(ks)
    n = b.shape[0] if trans_b else b.shape[1]
    tn = _pick(n, (512, 256, 128))
    tm = next((c for c in (1664, 1040, 832, 640, 512, 384, 256, 128)
               if m % c == 0 and c * ktot * 2 <= 8 * 1024 * 1024), m)
    na = len(a_list)

    def kern(*refs):
        b_ref, o_ref = refs[na], refs[na + 1]
        acc, off = None, 0
        for a_ref, k in zip(refs[:na], ks):
            if trans_b:
                part = _dot_nt(a_ref[...], b_ref[:, off:off + k])
            else:
                part = _dot(a_ref[...], b_ref[off:off + k, :])
            acc = part if acc is None else acc + part
            off += k
        o_ref[...] = acc.astype(o_ref.dtype)

    b_spec = (pl.BlockSpec((tn, ktot), lambda i, j: (j, 0)) if trans_b
              else pl.BlockSpec((ktot, tn), lambda i, j: (0, j)))
    res = _call(
        kern, (*a_list, b), name=name, grid=(m // tm, n // tn),
        in_specs=[pl.BlockSpec((tm, k), lambda i, j: (i, 0)) for k in ks] + [b_spec],
        out_specs=[pl.BlockSpec((tm, tn), lambda i, j: (i, j))],
        out_shape=[jax.ShapeDtypeStruct((m, n), out_dtype)],
        sem=("parallel", "parallel"), carry=carry)
    return res[0] if carry is None else (res[0], res[1:])


def _row_tile(t):
    return _pick(t, (832, 640, 512, 384, 256, 128))


def rmsnorm_fwd(x, w, res=None, *, out_dtype, name, with_t=False, carry=None):
    t, d = x.shape
    tr = _conv_tile(t) if with_t else _row_tile(t)

    def kern(*refs):
        x_ref, w_ref = refs[0], refs[1]
        o_ref = refs[-2] if with_t else refs[-1]
        xv = x_ref[...]
        r = lax.rsqrt(jnp.mean(xv * xv, axis=-1, keepdims=True) + EPS)
        y = xv * r * w_ref[...]
        if res is not None:
            y = refs[2][...] + y
        o_ref[...] = y.astype(o_ref.dtype)
        if with_t:
            refs[-1][...] = y.T.astype(o_ref.dtype)

    row = pl.BlockSpec((tr, d), lambda i: (i, 0))
    vec = pl.BlockSpec((1, d), lambda i: (0, 0))
    ins = [x, w.reshape(1, d)] + ([] if res is None else [res])
    specs = [row, vec] + ([] if res is None else [row])
    out_specs, out_shape = [row], [jax.ShapeDtypeStruct((t, d), out_dtype)]
    if with_t:
        out_specs.append(pl.BlockSpec((d, tr), lambda i: (0, i)))
        out_shape.append(jax.ShapeDtypeStruct((d, t), out_dtype))
    res_ = _call(kern, ins, name=name, grid=(t // tr,), in_specs=specs, out_specs=out_specs,
                 out_shape=out_shape, sem=("parallel",), carry=carry)
    return res_[0] if len(res_) == 1 else res_


def rmsnorm_bwd(x, w, dy, res=None, *, out_dtype, name, carry=None):
    t, d = x.shape
    tr = _row_tile(t)

    def kern(*refs):
        if res is None:
            x_ref, w_ref, dy_ref, dx_ref, dw_ref = refs
        else:
            x_ref, w_ref, dy_ref, r_ref, dx_ref, dw_ref = refs
        i = pl.program_id(0)
        xv = x_ref[...]
        dyv = dy_ref[...].astype(F32)
        r = lax.rsqrt(jnp.mean(xv * xv, axis=-1, keepdims=True) + EPS)
        xh = xv * r
        g = dyv * w_ref[...]
        dx = r * (g - xh * jnp.mean(g * xh, axis=-1, keepdims=True))
        if res is not None:
            dx = r_ref[...] + dx
        dx_ref[...] = dx.astype(dx_ref.dtype)
        part = jnp.sum(dyv * xh, axis=0, keepdims=True)

        @pl.when(i == 0)
        def _():
            dw_ref[...] = part

        @pl.when(i > 0)
        def _():
            dw_ref[...] += part

    row = pl.BlockSpec((tr, d), lambda i: (i, 0))
    vec = pl.BlockSpec((1, d), lambda i: (0, 0))
    ins = [x, w.reshape(1, d), dy] + ([] if res is None else [res])
    specs = [row, vec, row] + ([] if res is None else [row])
    return _call(kern, ins, name=name, grid=(t // tr,), in_specs=specs, out_specs=[row, vec],
                 out_shape=[jax.ShapeDtypeStruct((t, d), out_dtype), jax.ShapeDtypeStruct((1, d), F32)],
                 sem=("arbitrary",), carry=carry)


def _conv_tile(t):
    return _pick(t, (640, 384, 256, 128))


def _conv_apply(x, tail, cw, cb, ksz):
    y = cb
    for k in range(ksz):
        y = y + cw[k:k + 1, :] * _shift_down(x, tail, ksz - 1 - k)
    return y


def dwconv_fwd(x, cw, cb, *, mode, x_off, c_out, cblk, out_dtype, name, with_t=False):
    t = x.shape[0]
    ksz = cw.shape[0]
    tb = _conv_tile(t)
    nb, ncb, t8 = t // tb, c_out // cblk, tb // 8
    xo = x_off // cblk
    nin = 2 if mode == "geglu" else 1

    def kern(*refs):
        o_ref = refs[-2] if with_t else refs[-1]
        n = pl.program_id(1)
        for c in range(cblk // BLK):
            ls = slice(c * BLK, (c + 1) * BLK)
            for s in range(tb // BLK):
                rs = slice(s * BLK, (s + 1) * BLK)
                valid = (n * tb + s * BLK + _rows((BLK, BLK))) >= PAD
                hs = []
                for q in range(nin):
                    x_ref, t_ref, w_ref, b_ref = refs[4 * q:4 * q + 4]
                    tail = (jnp.where(n > 0, t_ref[:, ls], 0.0) if s == 0
                            else x_ref[s * BLK - 8:s * BLK, ls])
                    hs.append(_conv_apply(x_ref[rs, ls], tail, w_ref[:, ls], b_ref[:, ls], ksz))
                y = _gelu(hs[0]) * hs[1] if mode == "geglu" else _silu(hs[0])
                y = _keep(y, valid, s)
                o_ref[rs, ls] = y.astype(o_ref.dtype)
                if with_t:
                    refs[-1][ls, rs] = y.T.astype(o_ref.dtype)

    ins, specs = [], []
    for q in range(nin):
        co = xo + q * ncb
        wo = q * ncb
        ins += [x, x, cw, cb.reshape(1, -1)]
        specs += [
            pl.BlockSpec((tb, cblk), lambda j, n, co=co: (n, co + j)),
            pl.BlockSpec((8, cblk), lambda j, n, co=co: (jnp.maximum(n * t8 - 1, 0), co + j)),
            pl.BlockSpec((ksz, cblk), lambda j, n, wo=wo: (0, wo + j)),
            pl.BlockSpec((1, cblk), lambda j, n, wo=wo: (0, wo + j)),
        ]
    out_specs = pl.BlockSpec((tb, cblk), lambda j, n: (n, j))
    out_shape = jax.ShapeDtypeStruct((t, c_out), out_dtype)
    if with_t:
        out_specs = [out_specs, pl.BlockSpec((cblk, tb), lambda j, n: (j, n))]
        out_shape = [out_shape, jax.ShapeDtypeStruct((c_out, t), out_dtype)]
    return pl.pallas_call(
        kern, name=name, grid=(ncb, nb), in_specs=specs, out_specs=out_specs, out_shape=out_shape,
        compiler_params=_cp("parallel", "parallel"),
    )(*ins)


def dwconv_bwd(x, cw, cb, dy, *, mode, x_off, c_out, cblk, name, carry=None):
    t = x.shape[0]
    ksz = cw.shape[0]
    tb = _conv_tile(t)
    nb, ncb, t8 = t // tb, c_out // cblk, tb // 8
    xo = x_off // cblk
    nin = 2 if mode == "geglu" else 1
    ctot = nin * c_out

    def kern(*refs):
        dy_ref = refs[4 * nin]
        outs = refs[4 * nin + 1:4 * nin + 1 + 3 * nin]
        heads = refs[4 * nin + 1 + 3 * nin:]
        n = pl.program_id(1)
        blk = nb - 1 - n

        @pl.when(n == 0)
        def _():
            for q in range(nin):
                heads[q][...] = jnp.zeros_like(heads[q])
                outs[3 * q + 1][...] = jnp.zeros_like(outs[3 * q + 1])
                outs[3 * q + 2][...] = jnp.zeros_like(outs[3 * q + 2])

        for c in range(cblk // BLK):
            ls = slice(c * BLK, (c + 1) * BLK)
            head = [heads[q][:, ls] for q in range(nin)]
            dwa = [[None] * ksz for _ in range(nin)]
            dba = [None] * nin
            for s in reversed(range(tb // BLK)):
                rs = slice(s * BLK, (s + 1) * BLK)
                valid = (blk * tb + s * BLK + _rows((BLK, BLK))) >= PAD
                xs, tails, hs = [], [], []
                for q in range(nin):
                    x_ref, t_ref, w_ref, b_ref = refs[4 * q:4 * q + 4]
                    tail = (jnp.where(blk > 0, t_ref[:, ls], 0.0) if s == 0
                            else x_ref[s * BLK - 8:s * BLK, ls])
                    xs.append(x_ref[rs, ls])
                    tails.append(tail)
                    hs.append(_conv_apply(xs[q], tail, w_ref[:, ls], b_ref[:, ls], ksz))
                dyv = dy_ref[rs, ls].astype(F32)
                if mode == "geglu":
                    dhs = [dyv * hs[1] * _gelu_grad(hs[0]), dyv * _gelu(hs[0])]
                else:
                    dhs = [dyv * _silu_grad(hs[0])]
                for q in range(nin):
                    w_ref = refs[4 * q + 2]
                    dh = _keep(dhs[q], valid, s)
                    dx = jnp.zeros_like(dh)
                    for k in range(ksz):
                        sh = ksz - 1 - k
                        dx = dx + w_ref[k:k + 1, ls] * _shift_up(dh, head[q], sh)
                        part = jnp.sum(dh * _shift_down(xs[q], tails[q], sh), axis=0, keepdims=True)
                        dwa[q][k] = part if dwa[q][k] is None else dwa[q][k] + part
                    outs[3 * q][rs, ls] = _keep(dx, valid, s).astype(outs[3 * q].dtype)
                    part = jnp.sum(dh, axis=0, keepdims=True)
                    dba[q] = part if dba[q] is None else dba[q] + part
                    head[q] = dh[0:8]
            for q in range(nin):
                outs[3 * q + 1][:, ls] += jnp.concatenate(dwa[q], axis=0)
                outs[3 * q + 2][:, ls] += dba[q]
                heads[q][:, ls] = head[q]

    ins, specs, out_specs, out_shape, scratch = [], [], [], [], []
    for q in range(nin):
        co = xo + q * ncb
        wo = q * ncb
        ins += [x, x, cw, cb.reshape(1, -1)]
        specs += [
            pl.BlockSpec((tb, cblk), lambda j, n, co=co: (nb - 1 - n, co + j)),
            pl.BlockSpec((8, cblk), lambda j, n, co=co: (jnp.maximum((nb - 1 - n) * t8 - 1, 0), co + j)),
            pl.BlockSpec((ksz, cblk), lambda j, n, wo=wo: (0, wo + j)),
            pl.BlockSpec((1, cblk), lambda j, n, wo=wo: (0, wo + j)),
        ]
        out_specs += [
            pl.BlockSpec((tb, cblk), lambda j, n: (nb - 1 - n, j)),
            pl.BlockSpec((ksz, cblk), lambda j, n: (0, j)),
            pl.BlockSpec((1, cblk), lambda j, n: (0, j)),
        ]
        out_shape += [jax.ShapeDtypeStruct((t, c_out), MXU),
                      jax.ShapeDtypeStruct((ksz, c_out), F32),
                      jax.ShapeDtypeStruct((1, c_out), F32)]
        scratch.append(pltpu.VMEM((8, cblk), F32))
    ins.append(dy)
    specs.append(pl.BlockSpec((tb, cblk), lambda j, n: (nb - 1 - n, j)))
    res = _call(kern, ins, name=name, grid=(ncb, nb), in_specs=specs, out_specs=out_specs,
                out_shape=out_shape, scratch_shapes=scratch, sem=("parallel", "arbitrary"), carry=carry)
    dxs = [res[3 * q] for q in range(nin)]
    dcw = jnp.concatenate([res[3 * q + 1] for q in range(nin)], axis=1)
    dcb = jnp.concatenate([res[3 * q + 2] for q in range(nin)], axis=1)
    return dxs, dcw, dcb.reshape(ctot), res[3 * nin:]


def _lru_tile(t):
    return _pick(t, (640, 384, 256, 128))


def _lru_gates(xc, wa, ba, wx, bx, sp):
    r = _sigmoid(_dot(xc, wa) + ba)
    i = _sigmoid(_dot(xc, wx) + bx)
    log_a = -LRU_C * r * sp
    a = jnp.exp(log_a)
    mult = jnp.sqrt(-_expm1(2.0 * log_a))
    return r, i, a, mult


def lru_fwd(proj, cw, cb, wa, ba, wx, bx, lam, *, gate_off, xr_off, name, carry=None):
    t = proj.shape[0]
    tb = _lru_tile(t)
    nb, ns, t8 = t // tb, tb // BLK, tb // 8
    go, xo = gate_off // BLK, xr_off // BLK

    def kern(g_ref, x_ref, xt_ref, cw_ref, cb_ref, wa_ref, ba_ref, wx_ref, bx_ref, lam_ref,
             y_ref, yt_ref, h_ref, hc_ref):
        n = pl.program_id(1)

        @pl.when(n == 0)
        def _():
            hc_ref[...] = jnp.zeros_like(hc_ref)

        sp = _softplus(-lam_ref[...])
        hprev = hc_ref[0:1, :]
        scans = []
        for s in range(ns):
            sl = slice(s * BLK, (s + 1) * BLK)
            xv = x_ref[sl, :]
            tail = jnp.where(n > 0, xt_ref[...], 0.0) if s == 0 else x_ref[s * BLK - 8:s * BLK, :]
            valid = (n * tb + s * BLK + _rows((BLK, BLK))) >= PAD
            xc = _keep(_conv_apply(xv, tail, cw_ref[...], cb_ref[...], 4), valid, s)
            _, i, a, mult = _lru_gates(xc, wa_ref[0], ba_ref[...], wx_ref[0], bx_ref[...], sp)
            scans.append(_scan_fwd(a, mult * (i * xc)))
        for s in range(ns):
            sl = slice(s * BLK, (s + 1) * BLK)
            ca, cu = scans[s]
            h = cu + ca * hprev
            hprev = _row_at(h, BLK - 1)
            h_ref[sl, :] = h
            y = _gelu(g_ref[sl, :]) * h
            y_ref[sl, :] = y.astype(y_ref.dtype)
            yt_ref[:, sl] = y.T.astype(yt_ref.dtype)
        hc_ref[...] = jnp.broadcast_to(hprev, hc_ref.shape)

    vec = pl.BlockSpec((1, BLK), lambda j, n: (0, j))
    mat = pl.BlockSpec((1, BLK, BLK), lambda j, n: (j, 0, 0))
    return _call(
        kern, (proj, proj, proj, cw, cb.reshape(1, -1), wa, ba.reshape(1, -1), wx, bx.reshape(1, -1),
               lam.reshape(1, -1)),
        name=name, grid=(D_RNN // BLK, nb),
        in_specs=[
            pl.BlockSpec((tb, BLK), lambda j, n: (n, go + j)),
            pl.BlockSpec((tb, BLK), lambda j, n: (n, xo + j)),
            pl.BlockSpec((8, BLK), lambda j, n: (jnp.maximum(n * t8 - 1, 0), xo + j)),
            pl.BlockSpec((4, BLK), lambda j, n: (0, j)), vec, mat, vec, mat, vec, vec,
        ],
        out_specs=[pl.BlockSpec((tb, BLK), lambda j, n: (n, j)),
                   pl.BlockSpec((BLK, tb), lambda j, n: (j, n)),
                   pl.BlockSpec((tb, BLK), lambda j, n: (n, j))],
        out_shape=[jax.ShapeDtypeStruct((t, D_RNN), MXU), jax.ShapeDtypeStruct((D_RNN, t), MXU),
                   jax.ShapeDtypeStruct((t, D_RNN), F32)],
        scratch_shapes=[pltpu.VMEM((8, BLK), F32)],
        sem=("parallel", "arbitrary"), carry=carry)


def lru_bwd(proj, h, dy, cw, cb, wa, ba, wx, bx, lam, *, gate_off, xr_off, dy_off, name, carry=None):
    t = proj.shape[0]
    tb = _lru_tile(t)
    nb, ns, t8 = t // tb, tb // BLK, tb // 8
    go, xo, do = gate_off // BLK, xr_off // BLK, dy_off // BLK

    def kern(g_ref, x_ref, xt_ref, h_ref, ht_ref, dy_ref, cw_ref, cb_ref, wa_ref, ba_ref,
             wx_ref, bx_ref, lam_ref,
             dg_ref, dx_ref, dcw_ref, dcb_ref, dwa_ref, dba_ref, dwx_ref, dbx_ref, dlam_ref,
             gin_ref, head_ref):
        n = pl.program_id(1)
        blk = nb - 1 - n

        @pl.when(n == 0)
        def _():
            gin_ref[...] = jnp.zeros_like(gin_ref)
            head_ref[...] = jnp.zeros_like(head_ref)
            for r_ in (dcw_ref, dcb_ref, dwa_ref, dba_ref, dwx_ref, dbx_ref, dlam_ref):
                r_[...] = jnp.zeros_like(r_)

        lamv = lam_ref[...]
        sp = _softplus(-lamv)
        dsp_dlam = -_sigmoid(-lamv)
        g_in = gin_ref[0:1, :]
        head = head_ref[...]
        ones8 = jnp.ones((8, BLK), F32)
        wav, wxv = wa_ref[0], wx_ref[0]
        staged = {}
        for s in range(ns):
            sl = slice(s * BLK, (s + 1) * BLK)
            xv = x_ref[sl, :]
            if s == 0:
                tail = jnp.where(blk > 0, xt_ref[...], 0.0)
                htail = jnp.where(blk > 0, ht_ref[...], 0.0)
            else:
                tail = x_ref[s * BLK - 8:s * BLK, :]
                htail = h_ref[s * BLK - 8:s * BLK, :]
            valid = (blk * tb + s * BLK + _rows((BLK, BLK))) >= PAD
            xc = _keep(_conv_apply(xv, tail, cw_ref[...], cb_ref[...], 4), valid, s)
            r, i, a, mult = _lru_gates(xc, wav, ba_ref[...], wxv, bx_ref[...], sp)
            hv = h_ref[sl, :]
            hprev = _shift_down(hv, htail, 1)
            gv = g_ref[sl, :]
            dyv = dy_ref[sl, :].astype(F32)
            dg_ref[sl, :] = (dyv * hv * _gelu_grad(gv)).astype(dg_ref.dtype)
            cc, cu = _scan_rev(_shift_up(a, ones8, 1), dyv * _gelu(gv))
            staged[s] = (xv, tail, valid, xc, r, i, a, mult, hprev, cc, cu)
        for s in reversed(range(ns)):
            sl = slice(s * BLK, (s + 1) * BLK)
            xv, tail, valid, xc, r, i, a, mult, hprev, cc, cu = staged[s]
            gg = cu + cc * g_in
            g_in = _row_at(a * gg, 0)
            da = gg * hprev
            di = gg * mult * xc
            dxc = gg * mult * i
            dmult = gg * i * xc
            dlog_a = da * a - dmult * (a * a) / mult
            dr = dlog_a * (-LRU_C * sp)
            dlam_ref[...] += jnp.sum(dlog_a * (-LRU_C) * r, axis=0, keepdims=True) * dsp_dlam
            dpr = dr * r * (1.0 - r)
            dpi = di * i * (1.0 - i)
            dxc = dxc + _dot_nt(dpr, wav) + _dot_nt(dpi, wxv)
            dxc, dpr, dpi = _keep(dxc, valid, s), _keep(dpr, valid, s), _keep(dpi, valid, s)
            dwa_ref[0] += _dot_tn(xc, dpr)
            dwx_ref[0] += _dot_tn(xc, dpi)
            dba_ref[...] += jnp.sum(dpr, axis=0, keepdims=True)
            dbx_ref[...] += jnp.sum(dpi, axis=0, keepdims=True)
            dx = jnp.zeros_like(dxc)
            dws = []
            for k in range(4):
                dx = dx + cw_ref[k:k + 1, :] * _shift_up(dxc, head, 3 - k)
                dws.append(jnp.sum(dxc * _shift_down(xv, tail, 3 - k), axis=0, keepdims=True))
            dx_ref[sl, :] = _keep(dx, valid, s).astype(dx_ref.dtype)
            dcw_ref[...] += jnp.concatenate(dws, axis=0)
            dcb_ref[...] += jnp.sum(dxc, axis=0, keepdims=True)
            head = dxc[0:8]
        gin_ref[...] = jnp.broadcast_to(g_in, gin_ref.shape)
        head_ref[...] = head

    vec = pl.BlockSpec((1, BLK), lambda j, n: (0, j))
    mat = pl.BlockSpec((1, BLK, BLK), lambda j, n: (j, 0, 0))
    cws = pl.BlockSpec((4, BLK), lambda j, n: (0, j))

    def rb(off):
        return pl.BlockSpec((tb, BLK), lambda j, n: (nb - 1 - n, off + j))

    def tl(off):
        return pl.BlockSpec((8, BLK), lambda j, n: (jnp.maximum((nb - 1 - n) * t8 - 1, 0), off + j))

    return _call(
        kern, (proj, proj, proj, h, h, dy, cw, cb.reshape(1, -1), wa, ba.reshape(1, -1), wx,
               bx.reshape(1, -1), lam.reshape(1, -1)),
        name=name, grid=(D_RNN // BLK, nb),
        in_specs=[rb(go), rb(xo), tl(xo), rb(0), tl(0), rb(do), cws, vec, mat, vec, mat, vec, vec],
        out_specs=[rb(0), rb(0), cws, vec, mat, vec, mat, vec, vec],
        out_shape=[jax.ShapeDtypeStruct((t, D_RNN), MXU), jax.ShapeDtypeStruct((t, D_RNN), MXU),
                   jax.ShapeDtypeStruct((4, D_RNN), F32), jax.ShapeDtypeStruct((1, D_RNN), F32),
                   jax.ShapeDtypeStruct((8, BLK, BLK), F32), jax.ShapeDtypeStruct((1, D_RNN), F32),
                   jax.ShapeDtypeStruct((8, BLK, BLK), F32), jax.ShapeDtypeStruct((1, D_RNN), F32),
                   jax.ShapeDtypeStruct((1, D_RNN), F32)],
        scratch_shapes=[pltpu.VMEM((8, BLK), F32), pltpu.VMEM((8, BLK), F32)],
        sem=("parallel", "arbitrary"), carry=carry)


_SCALE = HEAD_DIM ** -0.5


STK = 4


def _attn_masks(n):
    qi = np.arange(STK * BLK)[:, None] % BLK
    c = np.arange(3 * BLK)[None, :]
    tq = n * BLK + qi - PAD
    s_band = (n - 1) * BLK + c - PAD
    d_band = tq - s_band
    ok_band = (s_band >= N_META) & (d_band >= 0) & (d_band < BLK)
    jm = c - 2 * BLK
    d_meta = tq - (jm - PAD)
    ok_meta = (jm >= PAD) & (d_meta >= 0)
    is_band = c < 2 * BLK
    ok = np.where(is_band, ok_band, ok_meta)
    dist = np.where(is_band, d_band, np.minimum(d_meta, BLK)).astype(np.float32)
    return ok, dist


def _stack_heads(g, e):
    return [8 * g + 2 * i + e for i in range(STK)]


def _attn_bias_table():
    tabs = []
    for n in range(3):
        ok, dist = _attn_masks(n)
        per = []
        for g in range(2):
            for e in range(2):
                slope = np.repeat(np.array([2.0 ** (-8.0 * (h + 1) / N_Q_HEADS) for h in _stack_heads(g, e)],
                                           np.float32), BLK)[:, None]
                per.append(np.where(ok, -(slope * dist), np.float32(NEG)).astype(np.float32))
        tabs.append(np.stack(per))
    return jnp.asarray(np.stack(tabs))


def _stack_sinks(heads, sk):
    return jnp.concatenate(
        [jnp.broadcast_to(jnp.sum(jnp.where(_lanes(sk.shape) == h, sk, 0.0), axis=1, keepdims=True),
                          (BLK, 1)) for h in heads], axis=0)


def _stack_tiles(ref, g, sel):
    return jnp.concatenate(
        [jnp.where(sel, ref[:, (4 * g + i) * BLK:(4 * g + i + 1) * BLK].astype(F32), 0.0)
         for i in range(STK)], axis=0)


def _attn_probs(qk, bias, sink):
    s = qk * _SCALE + bias
    mx = jnp.maximum(jnp.max(s, axis=-1, keepdims=True), sink)
    p = jnp.exp(s - mx)
    es = jnp.exp(sink - mx)
    inv = 1.0 / (jnp.sum(p, axis=-1, keepdims=True) + es)
    return p * inv, es * inv


def _attn_specs(t, q_off, k_off, v_off, rev):
    nb = t // BLK
    qo, ko, vo = q_off // 1024, k_off // BLK, v_off // BLK

    def b(n):
        return nb - 1 - n if rev else n

    return [
        pl.BlockSpec((BLK, 1024), lambda n: (b(n), qo)),
        pl.BlockSpec((BLK, BLK), lambda n: (b(n), ko)),
        pl.BlockSpec((BLK, BLK), lambda n: (b(n), vo)),
        pl.BlockSpec((BLK, BLK), lambda n: (jnp.maximum(b(n) - 1, 0), ko)),
        pl.BlockSpec((BLK, BLK), lambda n: (jnp.maximum(b(n) - 1, 0), vo)),
        pl.BlockSpec((BLK, BLK), lambda n: (0, ko)),
        pl.BlockSpec((BLK, BLK), lambda n: (0, vo)),
        pl.BlockSpec((1, BLK), lambda n: (0, 0)),
        pl.BlockSpec((1, 4, STK * BLK, 3 * BLK), lambda n: (jnp.minimum(b(n), 2), 0, 0, 0)),
    ]


def attn_fwd(proj, sinks, *, q_off, k_off, v_off, name, carry=None):
    t = proj.shape[0]
    nb = t // BLK

    def kern(q_ref, kc_ref, vc_ref, kp_ref, vp_ref, km_ref, vm_ref, sk_ref, tab_ref, o_ref):
        k_all = jnp.concatenate([kp_ref[...], kc_ref[...], km_ref[...]], axis=0)
        v_all = jnp.concatenate([vp_ref[...], vc_ref[...], vm_ref[...]], axis=0)
        k_alt = pltpu.roll(k_all, HEAD_DIM, 1)
        v_alt = pltpu.roll(v_all, HEAD_DIM, 1)
        low = _lanes((BLK, BLK)) < HEAD_DIM
        stacks = [(g, e) for g in range(2) for e in range(2)]
        qk = {(g, e): _dot_nt(_stack_tiles(q_ref, g, low == (e == 0)), k_all if g == e else k_alt)
              for g, e in stacks}
        ps = {(g, e): _attn_probs(qk[g, e], tab_ref[0, 2 * g + e],
                                  _stack_sinks(_stack_heads(g, e), sk_ref[...]))[0] for g, e in stacks}
        outs = {(g, e): _dot(ps[g, e], v_all if g == e else v_alt) for g, e in stacks}
        for hp in range(N_Q_HEADS // 2):
            g, rs = hp // STK, slice((hp % STK) * BLK, (hp % STK + 1) * BLK)
            o_ref[:, hp * BLK:(hp + 1) * BLK] = jnp.where(low, outs[g, 0][rs], outs[g, 1][rs]).astype(o_ref.dtype)

    sk = jnp.zeros((1, BLK), F32).at[0, :N_Q_HEADS].set(sinks)
    return _call(
        kern, (proj, proj, proj, proj, proj, proj, proj, sk, _attn_bias_table()), name=name, grid=(nb,),
        in_specs=_attn_specs(t, q_off, k_off, v_off, False),
        out_specs=[pl.BlockSpec((BLK, 1024), lambda n: (n, 0))],
        out_shape=[jax.ShapeDtypeStruct((t, 1024), MXU)],
        sem=("parallel",), carry=carry)


def attn_bwd(proj, sinks, dy, *, q_off, k_off, v_off, dy_off, name, carry=None):
    t = proj.shape[0]
    nb = t // BLK
    do = dy_off // 1024

    def kern(q_ref, kc_ref, vc_ref, kp_ref, vp_ref, km_ref, vm_ref, sk_ref, tab_ref, do_ref,
             dq_ref, dk_ref, dv_ref, dsk_ref, ck_ref, cv_ref, mk_ref, mv_ref):
        n = pl.program_id(0)
        blk = nb - 1 - n

        @pl.when(n == 0)
        def _():
            for r_ in (ck_ref, cv_ref, mk_ref, mv_ref, dsk_ref):
                r_[...] = jnp.zeros_like(r_)

        k_all = jnp.concatenate([kp_ref[...], kc_ref[...], km_ref[...]], axis=0)
        v_all = jnp.concatenate([vp_ref[...], vc_ref[...], vm_ref[...]], axis=0)
        k_alt = pltpu.roll(k_all, HEAD_DIM, 1)
        v_alt = pltpu.roll(v_all, HEAD_DIM, 1)
        low = _lanes((BLK, BLK)) < HEAD_DIM
        lane1 = _lanes((1, BLK))
        dk_all = jnp.zeros((3 * BLK, BLK), F32)
        dv_all = jnp.zeros((3 * BLK, BLK), F32)
        dsk = jnp.zeros((1, BLK), F32)
        stacks = [(g, e) for g in range(2) for e in range(2)]
        qm = {(g, e): _stack_tiles(q_ref, g, low == (e == 0)) for g, e in stacks}
        dom = {(g, e): _stack_tiles(do_ref, g, low == (e == 0)) for g, e in stacks}
        qk = {(g, e): _dot_nt(qm[g, e], k_all if g == e else k_alt) for g, e in stacks}
        dp = {(g, e): _dot_nt(dom[g, e], v_all if g == e else v_alt) for g, e in stacks}
        ps, dss = {}, {}
        for g, e in stacks:
            heads = _stack_heads(g, e)
            p, psink = _attn_probs(qk[g, e], tab_ref[0, 2 * g + e], _stack_sinks(heads, sk_ref[...]))
            delta = jnp.sum(p * dp[g, e], axis=-1, keepdims=True)
            ps[g, e] = p
            dss[g, e] = p * (dp[g, e] - delta) * _SCALE
            psd = psink * delta
            for i, h in enumerate(heads):
                dsk = dsk + jnp.where(lane1 == h, -jnp.sum(psd[i * BLK:(i + 1) * BLK], axis=0, keepdims=True), 0.0)
        dqs = {(g, e): _dot(dss[g, e], k_all if g == e else k_alt) for g, e in stacks}
        for g, e in stacks:
            dkh = _dot_tn(dss[g, e], qm[g, e])
            dvh = _dot_tn(ps[g, e], dom[g, e])
            if g != e:
                dkh = pltpu.roll(dkh, HEAD_DIM, 1)
                dvh = pltpu.roll(dvh, HEAD_DIM, 1)
            dk_all = dk_all + dkh
            dv_all = dv_all + dvh
        for hp in range(N_Q_HEADS // 2):
            g, rs = hp // STK, slice((hp % STK) * BLK, (hp % STK + 1) * BLK)
            dq_ref[:, hp * BLK:(hp + 1) * BLK] = jnp.where(low, dqs[g, 0][rs], dqs[g, 1][rs]).astype(dq_ref.dtype)
        dsk_ref[...] += dsk
        mk_ref[...] += dk_all[2 * BLK:3 * BLK]
        mv_ref[...] += dv_all[2 * BLK:3 * BLK]
        is0 = blk == 0
        dk_ref[...] = (dk_all[BLK:2 * BLK] + ck_ref[...] + jnp.where(is0, mk_ref[...], 0.0)).astype(dk_ref.dtype)
        dv_ref[...] = (dv_all[BLK:2 * BLK] + cv_ref[...] + jnp.where(is0, mv_ref[...], 0.0)).astype(dv_ref.dtype)
        ck_ref[...] = dk_all[0:BLK]
        cv_ref[...] = dv_all[0:BLK]

    sk = jnp.zeros((1, BLK), F32).at[0, :N_Q_HEADS].set(sinks)
    kv = pl.BlockSpec((BLK, BLK), lambda n: (nb - 1 - n, 0))
    res = _call(
        kern, (proj, proj, proj, proj, proj, proj, proj, sk, _attn_bias_table(), dy), name=name, grid=(nb,),
        in_specs=_attn_specs(t, q_off, k_off, v_off, True)
        + [pl.BlockSpec((BLK, 1024), lambda n: (nb - 1 - n, do))],
        out_specs=[pl.BlockSpec((BLK, 1024), lambda n: (nb - 1 - n, 0)), kv, kv,
                   pl.BlockSpec((1, BLK), lambda n: (0, 0))],
        out_shape=[jax.ShapeDtypeStruct((t, 1024), MXU), jax.ShapeDtypeStruct((t, BLK), MXU),
                   jax.ShapeDtypeStruct((t, BLK), MXU), jax.ShapeDtypeStruct((1, BLK), F32)],
        scratch_shapes=[pltpu.VMEM((BLK, BLK), F32)] * 4,
        sem=("arbitrary",), carry=carry)
    return [res[0], res[1], res[2], res[3][0, :N_Q_HEADS]] + res[4:]


GW = D_SSM // SSD_GROUPS
EXP_ROWS = 3 * BLK + 8
RED_ROWS = EXP_ROWS + 8


def _head_expand():
    ch = jnp.arange(D_SSM) // HEAD_DIM
    return (jnp.arange(BLK)[:, None] == ch[None, :]).astype(BF16)


def _ssd_decay(raw, dtb, alog, rowv):
    valid = rowv & (_lanes((BLK, BLK)) < SSD_HEADS)
    pre = raw + dtb
    dtp = jnp.where(valid, _softplus(pre), 0.0)
    av = -jnp.exp(alog)
    cs = _cumsum_fwd(dtp * av)
    cs_last = _row_at(cs, BLK - 1)
    return valid, pre, dtp, av, cs, jnp.exp(cs), jnp.exp(cs_last - cs), jnp.exp(cs_last)


def _head_col(x, h):
    return jnp.sum(jnp.where(_lanes(x.shape) == h, x, 0.0), axis=1, keepdims=True)


def _ssd_group_fwd(g, xdt, cs, cst, cb, tril, low):
    lm = []
    for k in range(4):
        h = 4 * g + k
        seg = _head_col(cs, h) - _row_at(cst, h)
        lmat = jnp.where(tril, jnp.exp(jnp.minimum(seg, 0.0)), 0.0)
        lm.append((lmat, cb * lmat))
    hv = [_dot(lm[k][1], xdt[:, g * GW + (k // 2) * BLK:g * GW + (k // 2 + 1) * BLK]) for k in range(4)]
    return jnp.concatenate([jnp.where(low, hv[0], hv[1]), jnp.where(low, hv[2], hv[3])], axis=1), lm


def ssd_decay(proj, dt_bias, a_log, *, dt_off, name):
    t = proj.shape[0]
    tb = _conv_tile(t)
    dto = dt_off // BLK

    def kern(dt_ref, dtb_ref, alog_ref, o_ref):
        n = pl.program_id(0)
        for s in range(tb // BLK):
            rs = slice(s * BLK, (s + 1) * BLK)
            rowv = (n * tb + s * BLK + _rows((BLK, BLK))) >= PAD
            _, _, dtp, _, cs, ecs, w, _ = _ssd_decay(dt_ref[rs, :], dtb_ref[...], alog_ref[...], rowv)
            for k, v in enumerate((dtp, cs, ecs, w)):
                o_ref[rs, k * BLK:(k + 1) * BLK] = v

    vec = pl.BlockSpec((1, BLK), lambda n: (0, 0))
    return pl.pallas_call(
        kern, name=name, grid=(t // tb,),
        in_specs=[pl.BlockSpec((tb, BLK), lambda n: (n, dto)), vec, vec],
        out_specs=pl.BlockSpec((tb, 4 * BLK), lambda n: (n, 0)),
        out_shape=jax.ShapeDtypeStruct((t, 4 * BLK), F32),
        compiler_params=_cp("parallel"),
    )(proj, _pad128(dt_bias), _pad128(a_log))


def _load_decay(d_ref):
    dtp, cs, ecs, w = (d_ref[:, k * BLK:(k + 1) * BLK] for k in range(4))
    return dtp, cs, ecs, w, _row_at(ecs, BLK - 1)


def _expand_heads(dtp, ecs, w, dec, e):
    ex = _dot(jnp.concatenate([dtp, ecs, w, jnp.broadcast_to(dec, (8, BLK))], axis=0), e)
    return ex[0:BLK], ex[BLK:2 * BLK], ex[2 * BLK:3 * BLK], jnp.max(ex[3 * BLK:EXP_ROWS], axis=0, keepdims=True)


def _ssd_specs(t, z_off, dt_off, rev):
    nb = t // BLK
    zo, dto = z_off // D_SSM, dt_off // BLK

    def b(n):
        return nb - 1 - n if rev else n

    vec = lambda w: pl.BlockSpec((1, w), lambda n: (0, 0))
    return [
        pl.BlockSpec((BLK, D_SSM), lambda n: (b(n), 0)),
        pl.BlockSpec((BLK, 1024), lambda n: (b(n), 2)),
        pl.BlockSpec((BLK, 1024), lambda n: (b(n), 3)),
        pl.BlockSpec((BLK, D_SSM), lambda n: (b(n), zo)),
        pl.BlockSpec((BLK, BLK), lambda n: (b(n), dto)),
        vec(BLK), vec(BLK), vec(D_SSM), vec(D_SSM),
        pl.BlockSpec((BLK, D_SSM), lambda n: (0, 0)),
        pl.BlockSpec((BLK, 4 * BLK), lambda n: (b(n), 0)),
    ]


def _pad128(v):
    return jnp.zeros((1, BLK), F32).at[0, :v.shape[0]].set(v)


def ssd_fwd(xbc, proj, decay, dt_bias, a_log, d_skip, gate_norm, *, z_off, dt_off, name):
    t = xbc.shape[0]
    nb = t // BLK

    def kern(x_ref, b_ref, c_ref, z_ref, dt_ref, dtb_ref, alog_ref, dsk_ref, gn_ref, e_ref, d_ref,
             yn_ref, ynt_ref, st_ref, p_ref):
        n = pl.program_id(0)

        @pl.when(n == 0)
        def _():
            p_ref[...] = jnp.zeros_like(p_ref)

        bgs = [b_ref[:, g * BLK:(g + 1) * BLK] for g in range(SSD_GROUPS)]
        cgs = [c_ref[:, g * BLK:(g + 1) * BLK] for g in range(SSD_GROUPS)]
        cbs = [_dot_nt(cgs[g], bgs[g]) for g in range(SSD_GROUPS)]
        pgs = [p_ref[g] for g in range(SSD_GROUPS)]
        zs = [_dot(cgs[g], pgs[g]) for g in range(SSD_GROUPS)]
        new_p = []
        dtp, cs, ecs, w, dec = _load_decay(d_ref)
        dtp_c, ecs_c, w_c, dec_c = _expand_heads(dtp, ecs, w, dec, e_ref[...])
        xv = x_ref[...]
        xdt = xv * dtp_c
        wx = w_c * xdt
        cst = cs.T
        tril = _rows((BLK, BLK)) >= _lanes((BLK, BLK))
        low = _lanes((BLK, BLK)) < HEAD_DIM
        for g in range(SSD_GROUPS):
            st_ref[0, g] = pgs[g]
        for g in range(SSD_GROUPS):
            gs = slice(g * GW, (g + 1) * GW)
            ydiag, _ = _ssd_group_fwd(g, xdt, cs, cst, cbs[g], tril, low)
            y = ydiag + zs[g] * ecs_c[:, gs] + dsk_ref[:, gs] * xv[:, gs]
            new_p.append(pgs[g] * dec_c[:, gs] + _dot_tn(bgs[g], wx[:, gs]))
            yz = y * _silu(z_ref[:, gs])
            r = lax.rsqrt(jnp.mean(yz * yz, axis=-1, keepdims=True) + EPS)
            yn = yz * r * gn_ref[:, gs]
            yn_ref[:, gs] = yn.astype(yn_ref.dtype)
            ynt_ref[gs, :] = yn.T.astype(ynt_ref.dtype)
        for g in range(SSD_GROUPS):
            p_ref[g] = new_p[g]

    return pl.pallas_call(
        kern, name=name, grid=(nb,),
        in_specs=_ssd_specs(t, z_off, dt_off, False),
        out_specs=[pl.BlockSpec((BLK, D_SSM), lambda n: (n, 0)),
                   pl.BlockSpec((D_SSM, BLK), lambda n: (0, n)),
                   pl.BlockSpec((1, SSD_GROUPS, BLK, GW), lambda n: (n, 0, 0, 0))],
        out_shape=[jax.ShapeDtypeStruct((t, D_SSM), MXU), jax.ShapeDtypeStruct((D_SSM, t), MXU),
                   jax.ShapeDtypeStruct((nb, SSD_GROUPS, BLK, GW), F32)],
        scratch_shapes=[pltpu.VMEM((SSD_GROUPS, BLK, GW), F32)],
        compiler_params=_cp("arbitrary"),
    )(xbc, xbc, xbc, proj, proj, _pad128(dt_bias), _pad128(a_log),
      jnp.repeat(d_skip, HEAD_DIM).reshape(1, D_SSM), gate_norm.reshape(1, D_SSM), _head_expand(), decay)


def ssd_bwd(xbc, proj, decay, st, dyn, dt_bias, a_log, d_skip, gate_norm, *, z_off, dt_off, name, carry=None):
    t = xbc.shape[0]
    nb = t // BLK

    def kern(x_ref, b_ref, c_ref, z_ref, dt_ref, dtb_ref, alog_ref, dsk_ref, gn_ref, e_ref, d_ref,
             et_ref, st_ref, dyn_ref,
             dxbc_ref, dz_ref, draw_ref, dgn_ref, ddsk_ref, ddtb_ref, dalog_ref,
             dp_ref, tr_ref):
        n = pl.program_id(0)
        blk = nb - 1 - n

        @pl.when(n == 0)
        def _():
            for r_ in (dp_ref, dgn_ref, ddsk_ref, ddtb_ref, dalog_ref):
                r_[...] = jnp.zeros_like(r_)

        bgs = [b_ref[:, g * BLK:(g + 1) * BLK] for g in range(SSD_GROUPS)]
        cgs = [c_ref[:, g * BLK:(g + 1) * BLK] for g in range(SSD_GROUPS)]
        cbs = [_dot_nt(cgs[g], bgs[g]) for g in range(SSD_GROUPS)]
        pgs = [st_ref[0, g] for g in range(SSD_GROUPS)]
        dpns = [dp_ref[g] for g in range(SSD_GROUPS)]
        zs = [_dot(cgs[g], pgs[g]) for g in range(SSD_GROUPS)]
        dwxs = [_dot(bgs[g], dpns[g]) for g in range(SSD_GROUPS)]
        new_dp, dgn_parts = [], []
        valid = ((blk * BLK + _rows((BLK, BLK))) >= PAD) & (_lanes((BLK, BLK)) < SSD_HEADS)
        pre = dt_ref[...] + dtb_ref[...]
        av = -jnp.exp(alog_ref[...])
        dtp, cs, ecs, w, dec = _load_decay(d_ref)
        dtp_c, ecs_c, w_c, dec_c = _expand_heads(dtp, ecs, w, dec, e_ref[...])
        xv = x_ref[...]
        xdt = xv * dtp_c
        wx = w_c * xdt
        cst = cs.T
        tril = _rows((BLK, BLK)) >= _lanes((BLK, BLK))
        lane = _lanes((BLK, BLK))
        rowi = _rows((BLK, BLK))
        low = lane < HEAD_DIM
        dcs = jnp.zeros((BLK, BLK), F32)
        dcst = jnp.zeros((BLK, BLK), F32)
        for g in range(SSD_GROUPS):
            gs = slice(g * GW, (g + 1) * GW)
            bg, cg = bgs[g], cgs[g]
            pg, dpn = pgs[g], dpns[g]
            xg = xv[:, gs]
            ydiag, lm = _ssd_group_fwd(g, xdt, cs, cst, cbs[g], tril, low)
            yoff = zs[g] * ecs_c[:, gs]
            y = ydiag + yoff + dsk_ref[:, gs] * xg
            zz = z_ref[:, gs]
            sz = _silu(zz)
            yz = y * sz
            r = lax.rsqrt(jnp.mean(yz * yz, axis=-1, keepdims=True) + EPS)
            yhat = yz * r
            dynv = dyn_ref[:, gs].astype(F32)
            gy = dynv * gn_ref[:, gs]
            dgn_parts.append(jnp.sum(dynv * yhat, axis=0, keepdims=True))
            dyz = r * (gy - yhat * jnp.mean(gy * yhat, axis=-1, keepdims=True))
            dy = dyz * sz
            dz_ref[:, gs] = (dyz * y * _silu_grad(zz)).astype(dz_ref.dtype)
            tr_ref[EXP_ROWS:RED_ROWS, gs] = jnp.broadcast_to(
                jnp.sum(dy * xg, axis=0, keepdims=True), (8, GW))
            dx = dsk_ref[:, gs] * dy
            dwx = dwxs[g]
            dxdt = w_c[:, gs] * dwx
            tr_ref[0:BLK, gs] = dwx * wx[:, gs]
            dbg = _dot_nt(wx[:, gs], dpn)
            dzo = ecs_c[:, gs] * dy
            tr_ref[BLK:2 * BLK, gs] = dy * yoff
            dcg = _dot_nt(dzo, pg)
            new_dp.append(dec_c[:, gs] * dpn + _dot_tn(cg, dzo))
            tr_ref[3 * BLK:EXP_ROWS, gs] = jnp.broadcast_to(
                jnp.sum(dpn * pg, axis=0, keepdims=True), (8, GW))
            dyh = [jnp.where(low == (k % 2 == 0), dy[:, (k // 2) * BLK:(k // 2 + 1) * BLK], 0.0) for k in range(4)]
            dms = [_dot_nt(dyh[k], xdt[:, g * GW + (k // 2) * BLK:g * GW + (k // 2 + 1) * BLK]) for k in range(4)]
            accs = [_dot_tn(lm[k][1], dyh[k]) for k in range(4)]
            dcb = jnp.zeros((BLK, BLK), F32)
            for k in range(4):
                h = 4 * g + k
                lmat, mmat = lm[k]
                dm = jnp.where(tril, dms[k], 0.0)
                nh = dm * mmat
                dcs = dcs + jnp.where(lane == h, jnp.sum(nh, axis=1, keepdims=True), 0.0)
                dcst = dcst - jnp.where(rowi == h, jnp.sum(nh, axis=0, keepdims=True), 0.0)
                dcb = dcb + dm * lmat
            dxdt = dxdt + jnp.concatenate([accs[0] + accs[1], accs[2] + accs[3]], axis=1)
            dcg = dcg + _dot(dcb, bg)
            dbg = dbg + _dot_tn(dcb, cg)
            tr_ref[2 * BLK:3 * BLK, gs] = dxdt * xg
            dxbc_ref[:, gs] = dx + dxdt * dtp_c[:, gs]
            dxbc_ref[:, D_SSM + g * BLK:D_SSM + (g + 1) * BLK] = dbg
            dxbc_ref[:, D_SSM + 1024 + g * BLK:D_SSM + 1024 + (g + 1) * BLK] = dcg
        red = _dot(tr_ref[...], et_ref[...])
        r1, r2, r3 = red[0:BLK], red[BLK:2 * BLK], red[2 * BLK:3 * BLK]
        for g in range(SSD_GROUPS):
            dp_ref[g] = new_dp[g]
        dgn_ref[...] += jnp.concatenate(dgn_parts, axis=1)
        ddec = jnp.max(red[3 * BLK:EXP_ROWS], axis=0, keepdims=True)
        ddsk_ref[...] += jnp.max(red[EXP_ROWS:RED_ROWS], axis=0, keepdims=True)
        dcs = dcs + dcst.T - r1 + r2
        dcs_last = jnp.sum(r1, axis=0, keepdims=True) + ddec * dec
        dcs = dcs + jnp.where(rowi == BLK - 1, dcs_last, 0.0)
        dda = _cumsum_rev(dcs)
        ddtp = r3 + dda * av
        dalog_ref[...] += jnp.sum(dda * dtp, axis=0, keepdims=True) * av
        draw = jnp.where(valid, ddtp * _sigmoid(pre), 0.0)
        ddtb_ref[...] += jnp.sum(draw, axis=0, keepdims=True)
        draw_ref[...] = draw.astype(draw_ref.dtype)

    vec = lambda w_: pl.BlockSpec((1, w_), lambda n: (0, 0))
    rb = lambda w_: pl.BlockSpec((BLK, w_), lambda n: (nb - 1 - n, 0))
    e = _head_expand()
    res = _call(
        kern, (xbc, xbc, xbc, proj, proj, _pad128(dt_bias), _pad128(a_log),
               jnp.repeat(d_skip, HEAD_DIM).reshape(1, D_SSM), gate_norm.reshape(1, D_SSM), e, decay, e.T, st, dyn),
        name=name, grid=(nb,),
        in_specs=_ssd_specs(t, z_off, dt_off, True)
        + [pl.BlockSpec((D_SSM, BLK), lambda n: (0, 0)),
           pl.BlockSpec((1, SSD_GROUPS, BLK, GW), lambda n: (nb - 1 - n, 0, 0, 0)),
           rb(D_SSM)],
        out_specs=[rb(2 * D_SSM), rb(D_SSM), rb(BLK), vec(D_SSM), vec(BLK), vec(BLK), vec(BLK)],
        out_shape=[jax.ShapeDtypeStruct((t, 2 * D_SSM), F32), jax.ShapeDtypeStruct((t, D_SSM), MXU),
                   jax.ShapeDtypeStruct((t, BLK), MXU), jax.ShapeDtypeStruct((1, D_SSM), F32),
                   jax.ShapeDtypeStruct((1, BLK), F32), jax.ShapeDtypeStruct((1, BLK), F32),
                   jax.ShapeDtypeStruct((1, BLK), F32)],
        scratch_shapes=[pltpu.VMEM((SSD_GROUPS, BLK, GW), F32), pltpu.VMEM((RED_ROWS, D_SSM), F32)],
        sem=("arbitrary",), carry=carry)
    dxbc, dz, draw, dgn, ddsk, ddtb, dalog = res[:7]
    return [dxbc, dz, draw, dgn[0], ddsk[0, :SSD_HEADS], ddtb[0, :SSD_HEADS], dalog[0, :SSD_HEADS]] + res[7:]


def loss_fwd_bwd(h, target, *, name):
    t, d = h.shape
    nb = t // BLK

    def kern(h_ref, t_ref, loss_ref, dh_ref):
        n = pl.program_id(0)
        err = jnp.where(n > 0, h_ref[...] - t_ref[...], 0.0)
        dh_ref[...] = err * (1.0 / d)
        part = (0.5 / d) * jnp.sum(jnp.sum(err * err, axis=1, keepdims=True), axis=0, keepdims=True)

        @pl.when(n == 0)
        def _():
            loss_ref[...] = part

        @pl.when(n > 0)
        def _():
            loss_ref[...] += part

    return pl.pallas_call(
        kern, name=name, grid=(nb,),
        in_specs=[pl.BlockSpec((BLK, d), lambda n: (n, 0)),
                  pl.BlockSpec((BLK, d), lambda n: (jnp.maximum(n - 1, 0), 0))],
        out_specs=[pl.BlockSpec((1, 1), lambda n: (0, 0)), pl.BlockSpec((BLK, d), lambda n: (n, 0))],
        out_shape=[jax.ShapeDtypeStruct((1, 1), F32), jax.ShapeDtypeStruct((t, d), F32)],
        compiler_params=_cp("arbitrary"),
    )(h, target)


def _ew_tile(r, c):
    cap = max(16, (256 * 1024) // c)
    best = None
    for tr in range(16, min(r, cap) + 1, 16):
        if r % tr == 0:
            best = tr
    return best if best is not None else r


def adamw(parts, w, m, v, *, name):
    npart, r, c = parts.shape
    tr = _ew_tile(r, c)

    def kern(p_ref, w_ref, m_ref, v_ref, g_ref, d_ref, m2_ref, v2_ref):
        g = p_ref[0].astype(F32)
        for k in range(1, npart):
            g = g + p_ref[k].astype(F32)
        m2 = ADAM_B1 * m_ref[...] + (1.0 - ADAM_B1) * g
        v2 = ADAM_B2 * v_ref[...] + (1.0 - ADAM_B2) * (g * g)
        m_hat = m2 / (1.0 - ADAM_B1 ** ADAM_STEP)
        v_hat = v2 / (1.0 - ADAM_B2 ** ADAM_STEP)
        g_ref[...] = g
        d_ref[...] = -ADAM_LR * (m_hat / (jnp.sqrt(v_hat) + ADAM_EPS) + ADAM_WD * w_ref[...])
        m2_ref[...] = m2
        v2_ref[...] = v2

    row = pl.BlockSpec((tr, c), lambda i: (i, 0))
    sds = jax.ShapeDtypeStruct((r, c), F32)
    return pl.pallas_call(
        kern, name=name, grid=(r // tr,),
        in_specs=[pl.BlockSpec((npart, tr, c), lambda i: (0, i, 0)), row, row, row],
        out_specs=[row, row, row, row], out_shape=[sds, sds, sds, sds],
        compiler_params=_cp("parallel"),
    )(parts, w, m, v)


def pair_add(p, land, *, name):
    _, r, c = p.shape
    tr = _ew_tile(r, c)
    core = lax.axis_index("c").astype(jnp.int32).reshape(1)

    def kern(c_ref, p_ref, l_ref, o_ref):
        o_ref[...] = (p_ref[...] + l_ref[...]).astype(o_ref.dtype)

    return pl.pallas_call(
        kern, name=name,
        grid_spec=pltpu.PrefetchScalarGridSpec(
            num_scalar_prefetch=1, grid=(4, r // tr),
            in_specs=[pl.BlockSpec((1, tr, c), lambda k, i, c_ref: (2 * k + c_ref[0], i, 0)),
                      pl.BlockSpec((1, tr, c), lambda k, i, c_ref: (k, i, 0))],
            out_specs=pl.BlockSpec((1, tr, c), lambda k, i, c_ref: (k, i, 0))),
        out_shape=jax.ShapeDtypeStruct((4, r, c), BF16),
        compiler_params=_cp("parallel", "parallel"),
    )(core, p, land)


def _me():
    return lax.axis_index("x"), lax.axis_index("y"), lax.axis_index("c")


def all_gather(xs, *, name):
    n = len(xs)

    def body(*refs):
        x_refs, out_refs = refs[:n], refs[n:2 * n]
        send_sems, recv_sems, local_sems = refs[2 * n:]
        mx, my, mc = _me()
        me, sib = (mx, my, mc), (mx, my, 1 - mc)
        chips = [(1 - mx, my), (mx, 1 - my), (1 - mx, 1 - my)]

        def rows(i, px, py, pc):
            return out_refs[i].at[4 * px + 2 * py + pc]

        def copy(i, k, block, to, src=None):
            return pltpu.make_async_remote_copy(
                src_ref=rows(i, *block) if src is None else src, dst_ref=rows(i, *block),
                send_sem=send_sems.at[7 * i + k], recv_sem=recv_sems.at[7 * i + k],
                device_id=to, device_id_type=MESH)

        mine = [pltpu.make_async_copy(x_refs[i], rows(i, *me), local_sems.at[i]) for i in range(n)]
        first = []
        for i in range(n):
            mine[i].start()
            first.append(copy(i, 0, me, sib, src=x_refs[i]))
            first += [copy(i, 1 + j, me, (*chip, mc), src=x_refs[i]) for j, chip in enumerate(chips)]
        for cp in first:
            cp.start()
        passed = []
        for i in range(n):
            for j, chip in enumerate(chips):
                copy(i, 1 + j, (*chip, mc), me).wait_recv()
                passed.append(copy(i, 4 + j, (*chip, mc), sib))
                passed[-1].start()
        for i in range(n):
            copy(i, 0, sib, me).wait_recv()
            for j, chip in enumerate(chips):
                copy(i, 4 + j, (*chip, 1 - mc), me).wait_recv()
        for cp in first + passed:
            cp.wait_send()
        for cp in mine:
            cp.wait()

    return pl.pallas_call(
        body, name=name,
        out_shape=[jax.ShapeDtypeStruct((N_DEV,) + x.shape, x.dtype) for x in xs],
        in_specs=[ANY] * n, out_specs=[ANY] * n,
        scratch_shapes=[pltpu.SemaphoreType.DMA((7 * n,)), pltpu.SemaphoreType.DMA((7 * n,)),
                        pltpu.SemaphoreType.DMA((n,))],
    )(*xs)


def pair_exchange(ps, *, name):
    n = len(ps)

    def body(*refs):
        p_refs, out_refs = refs[:n], refs[n:2 * n]
        send_sems, recv_sems = refs[2 * n:]
        mx, my, mc = _me()
        cps = [pltpu.make_async_remote_copy(
            src_ref=p_refs[i].at[2 * k + (1 - mc)], dst_ref=out_refs[i].at[k],
            send_sem=send_sems.at[4 * i + k], recv_sem=recv_sems.at[4 * i + k],
            device_id=(mx, my, 1 - mc), device_id_type=MESH) for i in range(n) for k in range(4)]
        for cp in cps:
            cp.start()
        for cp in cps:
            cp.wait_recv()
        for cp in cps:
            cp.wait_send()

    return pl.pallas_call(
        body, name=name,
        out_shape=[jax.ShapeDtypeStruct((4,) + p.shape[1:], p.dtype) for p in ps],
        in_specs=[ANY] * n, out_specs=[ANY] * n,
        scratch_shapes=[pltpu.SemaphoreType.DMA((4 * n,)), pltpu.SemaphoreType.DMA((4 * n,))],
    )(*ps)


def chip_exchange(qs, *, name):
    n = len(qs)

    def body(*refs):
        q_refs, out_refs = refs[:n], refs[n:2 * n]
        send_sems, recv_sems, local_sems = refs[2 * n:]
        mx, my, mc = _me()
        mine = 2 * mx + my
        chips = [(1 - mx, my), (mx, 1 - my), (1 - mx, 1 - my)]
        local, sends, recvs = [], [], []
        for i in range(n):
            local.append(pltpu.make_async_copy(q_refs[i].at[mine], out_refs[i].at[mine], local_sems.at[i]))
            for k, (px, py) in enumerate(chips):
                sems = dict(send_sem=send_sems.at[3 * i + k], recv_sem=recv_sems.at[3 * i + k],
                            device_id=(px, py, mc), device_id_type=MESH)
                sends.append(pltpu.make_async_remote_copy(
                    src_ref=q_refs[i].at[2 * px + py], dst_ref=out_refs[i].at[mine], **sems))
                recvs.append(pltpu.make_async_remote_copy(
                    src_ref=q_refs[i].at[mine], dst_ref=out_refs[i].at[2 * px + py], **sems))
        for cp in local + sends:
            cp.start()
        for cp in recvs:
            cp.wait_recv()
        for cp in sends:
            cp.wait_send()
        for cp in local:
            cp.wait()

    return pl.pallas_call(
        body, name=name,
        out_shape=[jax.ShapeDtypeStruct(q.shape, q.dtype) for q in qs],
        in_specs=[ANY] * n, out_specs=[ANY] * n,
        scratch_shapes=[pltpu.SemaphoreType.DMA((3 * n,)), pltpu.SemaphoreType.DMA((3 * n,)),
                        pltpu.SemaphoreType.DMA((n,))],
    )(*qs)


class _Carry:
    def __init__(self, inputs, out_shapes, sems, start, finish):
        self.inputs, self.out_shapes, self.sems = list(inputs), list(out_shapes), list(sems)
        self.start, self.finish = start, finish


def _call(kern, args, *, name, grid, in_specs, out_specs, out_shape, scratch_shapes=(), sem, carry=None):
    in_specs, out_specs, out_shape = list(in_specs), list(out_specs), list(out_shape)
    scratch_shapes = list(scratch_shapes)
    if carry is None:
        return list(pl.pallas_call(
            kern, name=name, grid=grid, in_specs=in_specs, out_specs=out_specs, out_shape=out_shape,
            scratch_shapes=scratch_shapes, compiler_params=_cp(*sem))(*args))
    ni, no, ns = len(in_specs), len(out_specs), len(scratch_shapes)
    ci, co = len(carry.inputs), len(carry.out_shapes)

    def body(*refs):
        o0 = ni + ci
        s0 = o0 + no + co
        ids = [pl.program_id(d) for d in range(len(grid))]
        first = functools.reduce(jnp.logical_and, [i == 0 for i in ids])
        last = functools.reduce(jnp.logical_and, [i == g - 1 for i, g in zip(ids, grid)])
        cin, cout, sems = refs[ni:o0], refs[o0 + no:s0], refs[s0 + ns:]

        @pl.when(first)
        def _():
            carry.start(cin, cout, sems)

        kern(*refs[:ni], *refs[o0:o0 + no], *refs[s0:s0 + ns])

        @pl.when(last)
        def _():
            carry.finish(cin, cout, sems)

    return list(pl.pallas_call(
        body, name=name, grid=grid, in_specs=in_specs + [ANY] * ci, out_specs=out_specs + [ANY] * co,
        out_shape=out_shape + carry.out_shapes, scratch_shapes=scratch_shapes + carry.sems,
        compiler_params=_cp(*(["arbitrary"] * len(grid))))(*args, *carry.inputs))


def merge_carries(cs):
    def split(seq, counts):
        out, off = [], 0
        for k in counts:
            out.append(seq[off:off + k])
            off += k
        return out

    def parts(cin, cout, sems):
        return zip(cs, split(cin, [len(c.inputs) for c in cs]), split(cout, [len(c.out_shapes) for c in cs]),
                   split(sems, [len(c.sems) for c in cs]))

    def start(cin, cout, sems):
        for c, i, o, s in parts(cin, cout, sems):
            c.start(i, o, s)

    def finish(cin, cout, sems):
        for c, i, o, s in parts(cin, cout, sems):
            c.finish(i, o, s)

    return _Carry(sum((c.inputs for c in cs), []), sum((c.out_shapes for c in cs), []),
                  sum((c.sems for c in cs), []), start, finish)


def gather_carry(xs):
    n = len(xs)

    def copies(cin, cout, sems, with_recv=True):
        mx, my, mc = _me()
        me = 4 * mx + 2 * my + mc
        peers = [(mx, my, 1 - mc), (1 - mx, my, mc), (mx, 1 - my, mc), (1 - mx, 1 - my, mc)]
        local, send, recv = [], [], []
        for i in range(n):
            local.append(pltpu.make_async_copy(cin[i], cout[i].at[me], sems[2].at[i]))
            for k, peer in enumerate(peers):
                common = dict(send_sem=sems[0].at[4 * i + k], recv_sem=sems[1].at[4 * i + k],
                              device_id=peer, device_id_type=MESH)
                send.append(pltpu.make_async_remote_copy(src_ref=cin[i], dst_ref=cout[i].at[me], **common))
                if with_recv:
                    recv.append(pltpu.make_async_remote_copy(
                        src_ref=cin[i], dst_ref=cout[i].at[4 * peer[0] + 2 * peer[1] + peer[2]], **common))
        return local, send, recv

    def start(cin, cout, sems):
        local, send, _ = copies(cin, cout, sems, with_recv=False)
        for cp in local + send:
            cp.start()

    def finish(cin, cout, sems):
        local, send, recv = copies(cin, cout, sems)
        for cp in recv:
            cp.wait_recv()
        for cp in send:
            cp.wait_send()
        for cp in local:
            cp.wait()

    return _Carry(xs, [jax.ShapeDtypeStruct((N_DEV,) + x.shape, x.dtype) for x in xs],
                  [pltpu.SemaphoreType.DMA((4 * n,)), pltpu.SemaphoreType.DMA((4 * n,)),
                   pltpu.SemaphoreType.DMA((n,))], start, finish)


def gather_relay(outs, *, name):
    n = len(outs)

    def body(*refs):
        bufs = refs[n:2 * n]
        send_sems, recv_sems = refs[2 * n:]
        mx, my, mc = _me()
        chips = [(1 - mx, my), (mx, 1 - my), (1 - mx, 1 - my)]
        send, recv = [], []
        for i in range(n):
            for j, (px, py) in enumerate(chips):
                common = dict(send_sem=send_sems.at[3 * i + j], recv_sem=recv_sems.at[3 * i + j],
                              device_id=(mx, my, 1 - mc), device_id_type=MESH)
                mine = bufs[i].at[4 * px + 2 * py + mc]
                send.append(pltpu.make_async_remote_copy(src_ref=mine, dst_ref=mine, **common))
                recv.append(pltpu.make_async_remote_copy(
                    src_ref=mine, dst_ref=bufs[i].at[4 * px + 2 * py + (1 - mc)], **common))
        for cp in send:
            cp.start()
        for cp in recv:
            cp.wait_recv()
        for cp in send:
            cp.wait_send()

    return pl.pallas_call(
        body, name=name, out_shape=[jax.ShapeDtypeStruct(o.shape, o.dtype) for o in outs],
        in_specs=[ANY] * n, out_specs=[ANY] * n, input_output_aliases={i: i for i in range(n)},
        scratch_shapes=[pltpu.SemaphoreType.DMA((3 * n,)), pltpu.SemaphoreType.DMA((3 * n,))],
    )(*outs)


def pair_carry(ps):
    n = len(ps)

    def copies(cin, cout, sems):
        mx, my, mc = _me()
        return [pltpu.make_async_remote_copy(
            src_ref=cin[i].at[2 * k + (1 - mc)], dst_ref=cout[i].at[k],
            send_sem=sems[0].at[4 * i + k], recv_sem=sems[1].at[4 * i + k],
            device_id=(mx, my, 1 - mc), device_id_type=MESH) for i in range(n) for k in range(4)]

    def start(cin, cout, sems):
        for cp in copies(cin, cout, sems):
            cp.start()

    def finish(cin, cout, sems):
        cps = copies(cin, cout, sems)
        for cp in cps:
            cp.wait_recv()
        for cp in cps:
            cp.wait_send()

    return _Carry(ps, [jax.ShapeDtypeStruct((4,) + p.shape[1:], p.dtype) for p in ps],
                  [pltpu.SemaphoreType.DMA((4 * n,)), pltpu.SemaphoreType.DMA((4 * n,))], start, finish)


def chip_carry(qs):
    n = len(qs)

    def copies(cin, cout, sems, with_recv=True):
        mx, my, mc = _me()
        mine = 2 * mx + my
        chips = [(1 - mx, my), (mx, 1 - my), (1 - mx, 1 - my)]
        local, send, recv = [], [], []
        for i in range(n):
            local.append(pltpu.make_async_copy(cin[i].at[mine], cout[i].at[mine], sems[2].at[i]))
            for k, (px, py) in enumerate(chips):
                common = dict(send_sem=sems[0].at[3 * i + k], recv_sem=sems[1].at[3 * i + k],
                              device_id=(px, py, mc), device_id_type=MESH)
                send.append(pltpu.make_async_remote_copy(
                    src_ref=cin[i].at[2 * px + py], dst_ref=cout[i].at[mine], **common))
                if with_recv:
                    recv.append(pltpu.make_async_remote_copy(
                        src_ref=cin[i].at[mine], dst_ref=cout[i].at[2 * px + py], **common))
        return local, send, recv

    def start(cin, cout, sems):
        local, send, _ = copies(cin, cout, sems, with_recv=False)
        for cp in local + send:
            cp.start()

    def finish(cin, cout, sems):
        local, send, recv = copies(cin, cout, sems)
        for cp in recv:
            cp.wait_recv()
        for cp in send:
            cp.wait_send()
        for cp in local:
            cp.wait()

    return _Carry(qs, [jax.ShapeDtypeStruct(q.shape, q.dtype) for q in qs],
                  [pltpu.SemaphoreType.DMA((3 * n,)), pltpu.SemaphoreType.DMA((3 * n,)),
                   pltpu.SemaphoreType.DMA((n,))], start, finish)


WEIGHTS = [
    "meta_tokens", "l0_mix_pre_norm", "l0_mix_post_norm", "l0_w_in", "l0_lru_conv_w", "l0_lru_conv_b",
    "l0_lru_w_a", "l0_lru_b_a", "l0_lru_w_x", "l0_lru_b_x", "l0_lru_lambda", "l0_attn_sinks", "l0_w_out",
    "l0_ffn_pre_norm", "l0_ffn_post_norm", "l0_ffn_w_up", "l0_ffn_conv_w", "l0_ffn_conv_b", "l0_ffn_w_down",
    "l1_mix_pre_norm", "l1_mix_post_norm", "l1_w_in", "l1_ssm_conv_w", "l1_ssm_conv_b", "l1_dt_bias",
    "l1_a_log", "l1_d_skip", "l1_gate_norm", "l1_w_out", "l1_ffn_pre_norm", "l1_ffn_post_norm",
    "l1_ffn_w_up", "l1_ffn_conv_w", "l1_ffn_conv_b", "l1_ffn_w_down",
]
INPUTS = (["x"] + WEIGHTS + ["loss_target"] + ["m_" + n for n in WEIGHTS] + ["v_" + n for n in WEIGHTS])

MATS = {"l0_w_in": ("col", (1024, 3328)), "l0_w_out": ("row", (2048, 1024)),
        "l0_ffn_w_up": ("col", (1024, 5632)), "l0_ffn_w_down": ("row", (2816, 1024)),
        "l1_w_in": ("col", (1024, 6176)), "l1_w_out": ("row", (2048, 1024)),
        "l1_ffn_w_up": ("col", (1024, 5632)), "l1_ffn_w_down": ("row", (2816, 1024))}
SMALL_SHARDED = {"meta_tokens": ("col", (16, 1024)), "l0_lru_conv_w": ("col", (4, 1024)),
                 "l0_ffn_conv_w": ("col", (3, 5632)), "l1_ssm_conv_w": ("col", (4, 4096)),
                 "l1_ffn_conv_w": ("col", (3, 5632))}
SHARDED = {**MATS, **SMALL_SHARDED}
REPLICATED = [n for n in WEIGHTS if n not in SHARDED]
SHAPES = {n: ((8, BLK, BLK) if n.endswith(("lru_w_a", "lru_w_x")) else (N_Q_HEADS,) if n.endswith("attn_sinks")
              else (2 * D_FF,) if n.endswith("ffn_conv_b") else (2 * D_SSM,) if n.endswith("ssm_conv_b")
              else (SSD_HEADS,) if n.endswith(("dt_bias", "a_log", "d_skip")) else (D_SSM,) if n.endswith("gate_norm")
              else (D_MODEL,)) for n in REPLICATED}
PACK_W = 1024
SMALL_W = 128


def _shard_shape(name):
    kind, (r, c) = SHARDED[name]
    return (r, c // N_DEV) if kind == "col" else (r // N_DEV, c)


def _rows_of(numel, width):
    return -(-numel // width)


def _to_rows(a, width):
    flat = a.reshape(-1)
    rows = _rows_of(flat.shape[0], width)
    return jnp.pad(flat, (0, rows * width - flat.shape[0])).reshape(rows, width)


def _pack(arrs, width, total_rows):
    slab = jnp.concatenate([_to_rows(a, width) for a in arrs], axis=0)
    return jnp.pad(slab, ((0, total_rows - slab.shape[0]), (0, 0)))


def _unpack(slab, shapes, width):
    out, off = [], 0
    for shp in shapes:
        numel = math.prod(shp)
        rows = _rows_of(numel, width)
        out.append(slab[off:off + rows].reshape(-1)[:numel].reshape(shp))
        off += rows
    return out


def _round_up(n, m):
    return -(-n // m) * m


def _by_dest(name, g):
    kind, (r, c) = SHARDED[name]
    if kind == "col":
        return g.reshape(r, N_DEV, c // N_DEV).transpose(1, 0, 2)
    return g.reshape(N_DEV, r // N_DEV, c)


def _from_shards(name, blocks):
    kind, (r, c) = SHARDED[name]
    return blocks.transpose(1, 0, 2).reshape(r, c) if kind == "col" else blocks.reshape(r, c)


L1_IN_PAD = 6272


def _ffn_fwd(h, a, w, pfx):
    u, ut = rmsnorm_fwd(h, a[pfx + "ffn_pre_norm"], out_dtype=MXU, name=pfx + "ffn_pre", with_t=True)
    up = matmul(u, w[pfx + "ffn_w_up"], name=pfx + "ffn_up")
    act, act_t = dwconv_fwd(up, a[pfx + "ffn_conv_w"], a[pfx + "ffn_conv_b"], mode="geglu", x_off=0,
                            c_out=D_FF, cblk=256, out_dtype=MXU, name=pfx + "ffn_act", with_t=True)
    down = matmul(act, w[pfx + "ffn_w_down"], name=pfx + "ffn_down")
    out = rmsnorm_fwd(down, a[pfx + "ffn_post_norm"], res=h, out_dtype=F32, name=pfx + "ffn_post")
    return out, (h, ut, up, act_t, down)


def _dx_and_pair_stage(names, g, a_list, b, *, name):
    parts = [_by_dest(n, g[n]) for n in names]
    out, from_sibling = matmul_cat(a_list, b, trans_b=True, name=name, carry=pair_carry(parts))
    return out, [pair_add(p, l, name="rs_pair_add_" + n) for n, p, l in zip(names, parts, from_sibling)]


def _ffn_bwd(dh, saved, a, w, pfx, g, carry=None):
    h, ut, up, act_t, down = saved
    dd, g[pfx + "ffn_post_norm"] = rmsnorm_bwd(down, a[pfx + "ffn_post_norm"], dh, out_dtype=MXU,
                                               name=pfx + "ffn_post_bwd")
    dact = matmul(dd, w[pfx + "ffn_w_down"], trans_b=True, name=pfx + "ffn_down_dx")
    g[pfx + "ffn_w_down"] = matmul(act_t, dd, name=pfx + "ffn_down_dw")
    dups, g[pfx + "ffn_conv_w"], g[pfx + "ffn_conv_b"], carried = dwconv_bwd(
        up, a[pfx + "ffn_conv_w"], a[pfx + "ffn_conv_b"], dact, mode="geglu", x_off=0, c_out=D_FF,
        cblk=256, name=pfx + "ffn_act_bwd", carry=carry)
    g[pfx + "ffn_w_up"] = jnp.concatenate(
        [matmul(ut, d, name=pfx + "ffn_up_dw%d" % i) for i, d in enumerate(dups)], axis=1)
    du, q = _dx_and_pair_stage([pfx + "ffn_w_down", pfx + "ffn_w_up"], g, dups, w[pfx + "ffn_w_up"],
                               name=pfx + "ffn_up_dx")
    dh_in, g[pfx + "ffn_pre_norm"] = rmsnorm_bwd(h, a[pfx + "ffn_pre_norm"], du, res=dh, out_dtype=F32,
                                                 name=pfx + "ffn_pre_bwd")
    return dh_in, carried, q


GATHER_EARLY = ["l0_w_out", "l0_ffn_w_up", "l0_ffn_w_down"]
GATHER_LATE = ["l1_w_in", "l1_w_out", "l1_ffn_w_up", "l1_ffn_w_down"]
RS_L1_FFN = ["l1_ffn_w_down", "l1_ffn_w_up"]
RS_L1_MIX = ["l1_w_out", "l1_w_in"]
RS_L0_FFN = ["l0_ffn_w_down", "l0_ffn_w_up"]
RS_LAST = ["l0_w_in", "l0_lru_conv_w", "l0_ffn_conv_w", "l1_ssm_conv_w", "l1_ffn_conv_w"]


REPL_LATE = ["l0_attn_sinks", "l0_mix_pre_norm"]
REPL_EARLY = [n for n in REPLICATED if n not in REPL_LATE]


def _local_step(a, shards):
    x = a["x"][0]
    seq = x.shape[0]
    h0 = jnp.concatenate([jnp.zeros((PAD, D_MODEL), F32), a["meta_tokens"], x], axis=0)
    g, landed = {}, {}

    u0, u0t, w_in0 = rmsnorm_fwd(h0, a["l0_mix_pre_norm"], out_dtype=MXU, name="l0_mix_pre", with_t=True,
                                 carry=gather_carry([shards["l0_w_in"]]))
    w = {"l0_w_in": _from_shards("l0_w_in", gather_relay([w_in0], name="gather_relay_first")[0])}
    proj0 = matmul(u0, w["l0_w_in"], name="l0_in")
    lru = (a["l0_lru_conv_w"], a["l0_lru_conv_b"], a["l0_lru_w_a"], a["l0_lru_b_a"], a["l0_lru_w_x"],
           a["l0_lru_b_x"], a["l0_lru_lambda"])
    ya, ya_t, hl, *early = lru_fwd(proj0, *lru, gate_off=0, xr_off=1024, name="l0_lru",
                                   carry=gather_carry([shards[n] for n in GATHER_EARLY]))
    yb, *late = attn_fwd(proj0, a["l0_attn_sinks"], q_off=2048, k_off=3072, v_off=3200, name="l0_attn",
                         carry=gather_carry([shards[n] for n in GATHER_LATE]))
    relayed = gather_relay(early + late, name="gather_relay")
    w = dict(w, **{n: _from_shards(n, blocks) for n, blocks in zip(GATHER_EARLY + GATHER_LATE, relayed)})
    w["l1_w_in"] = jnp.pad(w["l1_w_in"], ((0, 0), (0, L1_IN_PAD - w["l1_w_in"].shape[1])))
    o0 = matmul_cat([ya, yb], w["l0_w_out"], name="l0_out")
    h1 = rmsnorm_fwd(o0, a["l0_mix_post_norm"], res=h0, out_dtype=F32, name="l0_mix_post")
    h2, ffn0 = _ffn_fwd(h1, a, w, "l0_")

    u2, u2t = rmsnorm_fwd(h2, a["l1_mix_pre_norm"], out_dtype=MXU, name="l1_mix_pre", with_t=True)
    proj1 = matmul(u2, w["l1_w_in"], name="l1_in")
    xbc = dwconv_fwd(proj1, a["l1_ssm_conv_w"], a["l1_ssm_conv_b"], mode="silu", x_off=D_SSM,
                     c_out=2 * D_SSM, cblk=512, out_dtype=F32, name="l1_ssm_conv")
    ssd = (a["l1_dt_bias"], a["l1_a_log"], a["l1_d_skip"], a["l1_gate_norm"])
    decay = ssd_decay(proj1, a["l1_dt_bias"], a["l1_a_log"], dt_off=3 * D_SSM, name="l1_ssd_decay")
    yn, yn_t, st = ssd_fwd(xbc, proj1, decay, *ssd, z_off=0, dt_off=3 * D_SSM, name="l1_ssd")
    o1 = matmul(yn, w["l1_w_out"], name="l1_out")
    h3 = rmsnorm_fwd(o1, a["l1_mix_post_norm"], res=h2, out_dtype=F32, name="l1_mix_post")
    h4, ffn1 = _ffn_fwd(h3, a, w, "l1_")

    loss, dh4 = loss_fwd_bwd(h4, a["loss_target"][0], name="loss")

    dh3, _, q_l1_ffn = _ffn_bwd(dh4, ffn1, a, w, "l1_", g)
    do1, g["l1_mix_post_norm"] = rmsnorm_bwd(o1, a["l1_mix_post_norm"], dh3, out_dtype=MXU,
                                             name="l1_mix_post_bwd")
    dyn = matmul(do1, w["l1_w_out"], trans_b=True, name="l1_out_dx")
    g["l1_w_out"] = matmul(yn_t, do1, name="l1_out_dw")
    (dxbc, dz, draw, g["l1_gate_norm"], g["l1_d_skip"], g["l1_dt_bias"], g["l1_a_log"], *got) = ssd_bwd(
        xbc, proj1, decay, st, dyn, *ssd, z_off=0, dt_off=3 * D_SSM, name="l1_ssd_bwd",
        carry=chip_carry(q_l1_ffn))
    landed.update(zip(RS_L1_FFN, got))
    (dxin,), g["l1_ssm_conv_w"], g["l1_ssm_conv_b"], _ = dwconv_bwd(
        proj1, a["l1_ssm_conv_w"], a["l1_ssm_conv_b"], dxbc, mode="silu", x_off=D_SSM,
        c_out=2 * D_SSM, cblk=512, name="l1_ssm_conv_bwd")
    g["l1_w_in"] = jnp.concatenate(
        [matmul(u2t, dz, name="l1_in_dw_z"), matmul(u2t, dxin, name="l1_in_dw_x"),
         matmul(u2t, draw, name="l1_in_dw_dt")[:, :SSD_HEADS]], axis=1)
    du2, q_l1_mix = _dx_and_pair_stage(RS_L1_MIX, g, [dz, dxin, draw], w["l1_w_in"], name="l1_in_dx")
    dh2, g["l1_mix_pre_norm"] = rmsnorm_bwd(h2, a["l1_mix_pre_norm"], du2, res=dh3, out_dtype=F32,
                                            name="l1_mix_pre_bwd")

    dh1, got, q_l0_ffn = _ffn_bwd(dh2, ffn0, a, w, "l0_", g, carry=chip_carry(q_l1_mix))
    landed.update(zip(RS_L1_MIX, got))
    do0, g["l0_mix_post_norm"] = rmsnorm_bwd(o0, a["l0_mix_post_norm"], dh1, out_dtype=MXU,
                                             name="l0_mix_post_bwd")
    g["l0_w_out"] = jnp.concatenate([matmul(ya_t, do0, name="l0_out_dw_a"),
                                     matmul(yb.T, do0, name="l0_out_dw_b")], axis=0)
    dy, q_out = _dx_and_pair_stage(["l0_w_out"], g, [do0], w["l0_w_out"], name="l0_out_dx")
    (dgate, dxr, g["l0_lru_conv_w"], dcb, g["l0_lru_w_a"], dba, g["l0_lru_w_x"], dbx, dlam, *got) = lru_bwd(
        proj0, hl, dy, *lru, gate_off=0, xr_off=1024, dy_off=0, name="l0_lru_bwd", carry=chip_carry(q_out))
    landed["l0_w_out"] = got[0]
    g["l0_lru_conv_b"], g["l0_lru_b_a"], g["l0_lru_b_x"], g["l0_lru_lambda"] = dcb[0], dba[0], dbx[0], dlam[0]
    dq, dk, dv, g["l0_attn_sinks"], *got = attn_bwd(
        proj0, a["l0_attn_sinks"], dy, q_off=2048, k_off=3072, v_off=3200, dy_off=1024, name="l0_attn_bwd",
        carry=merge_carries([chip_carry(q_l0_ffn), gather_carry([_pack_repl(g, REPL_EARLY)])]))
    landed.update(zip(RS_L0_FFN, got[:2]))
    repl_early = gather_relay(got[2:], name="gather_relay_small_grads")[0]
    dproj0 = [dgate, dxr, dq, dk, dv]
    g["l0_w_in"] = jnp.concatenate(
        [matmul(u0t, d, name="l0_in_dw%d" % i) for i, d in enumerate(dproj0)], axis=1)
    du0, q_last = _dx_and_pair_stage(RS_LAST, g, dproj0, w["l0_w_in"], name="l0_in_dx")
    dh0, g["l0_mix_pre_norm"], *got = rmsnorm_bwd(h0, a["l0_mix_pre_norm"], du0, res=dh1, out_dtype=F32,
                                                  name="l0_mix_pre_bwd", carry=chip_carry(q_last))
    landed.update(zip(RS_LAST, got))
    g["meta_tokens"] = dh0[PAD:BLK]
    meta = _by_dest("meta_tokens", g["meta_tokens"])
    q_meta = pair_add(meta, pair_exchange([meta], name="rs_pair_meta")[0], name="rs_pair_add_meta_tokens")
    landed["meta_tokens"] = chip_exchange([q_meta], name="rs_chip_meta")[0]
    return loss[0, 0], dh0[BLK:].reshape(1, seq, D_MODEL), g, landed, repl_early


def _repl_rows(names):
    return _round_up(sum(_rows_of(math.prod(SHAPES[n]), SMALL_W) for n in names), 16)


def _pack_repl(vals, names):
    return _pack([vals[n] for n in names], SMALL_W, _repl_rows(names))


def kernel(*args):
    a = dict(zip(INPUTS, args))
    first = list(SMALL_SHARDED)
    full = {n: _from_shards(n, blocks) for n, blocks in
            zip(first, all_gather([a[n] for n in first], name="gather_first"))}
    shards = {n: a[n].astype(MXU) for n in MATS}
    loss_part, grad_x, g, landed, repl_early = _local_step({**a, **full}, shards)
    loss = lax.psum(loss_part, ("x", "y", "c"))

    sh_out = {n: adamw(landed[n], a[n], a["m_" + n], a["v_" + n], name="adamw_" + n) for n in SHARDED}

    repl_late = all_gather([_pack_repl(g, REPL_LATE)], name="gather_small_grads")[0]
    rp_out = [{}, {}, {}, {}]
    for names, parts in ((REPL_EARLY, repl_early), (REPL_LATE, repl_late)):
        res = adamw(parts, *[_pack_repl({n: a[p + n] for n in names}, names) for p in ("", "m_", "v_")],
                    name="adamw_replicated_%d" % len(names))
        for k in range(4):
            rp_out[k].update(zip(names, _unpack(res[k], [SHAPES[n] for n in names], SMALL_W)))

    outs = [loss, grad_x]
    for k in range(4):
        outs += [sh_out[n][k] if n in SHARDED else rp_out[k][n] for n in WEIGHTS]
    return tuple(outs)
```

```python
import functools
import math

import jax
import jax.numpy as jnp
import numpy as np
from jax import lax
from jax.experimental import pallas as pl
from jax.experimental.pallas import tpu as pltpu

F32 = jnp.float32
BF16 = jnp.bfloat16
MXU = jnp.bfloat16

D_MODEL = 1024
N_META = 16
BLK = 128
PAD = BLK - N_META
D_RNN = 1024
LRU_C = 8.0
N_Q_HEADS = 16
HEAD_DIM = 64
D_SSM = 2048
SSD_HEADS = 32
SSD_GROUPS = 8
D_FF = 2816
EPS = 1e-6
NEG = -1e30
N_DEV = 8

ADAM_LR = 0.001
ADAM_B1 = 0.9
ADAM_B2 = 0.999
ADAM_EPS = 1e-08
ADAM_WD = 0.01
ADAM_STEP = 10

VMEM_LIMIT = 56 * 1024 * 1024
MESH = pl.DeviceIdType.MESH
ANY = pl.BlockSpec(memory_space=pl.ANY)


def _cp(*sem):
    return pltpu.CompilerParams(dimension_semantics=sem, vmem_limit_bytes=VMEM_LIMIT)


def _pick(n, cands):
    for c in cands:
        if n % c == 0:
            return c
    return n


def _dot(a, b):
    return jnp.dot(a.astype(MXU), b.astype(MXU), preferred_element_type=F32)


def _dot_nt(a, b):
    return lax.dot_general(a.astype(MXU), b.astype(MXU), (((1,), (1,)), ((), ())),
                           preferred_element_type=F32)


def _dot_tn(a, b):
    return jnp.dot(a.T.astype(MXU), b.astype(MXU), preferred_element_type=F32)


def _sigmoid(x):
    return 1.0 / (1.0 + jnp.exp(-x))


def _log1p(x):
    u = 1.0 + x
    return jnp.where(u == 1.0, x, jnp.log(u) * (x / jnp.where(u == 1.0, 1.0, u - 1.0)))


def _expm1(x):
    u = jnp.exp(x)
    um1 = u - 1.0
    lg = jnp.log(jnp.where(u > 0.0, u, 1.0))
    safe = (um1 != 0.0) & (um1 != -1.0)
    return jnp.where(um1 == 0.0, x, jnp.where(um1 == -1.0, -1.0,
                                               um1 * (x / jnp.where(safe, lg, 1.0))))


def _softplus(x):
    return jnp.maximum(x, 0.0) + _log1p(jnp.exp(-jnp.abs(x)))


_GC = math.sqrt(2.0 / math.pi)


def _gelu(x):
    t = jnp.tanh(_GC * (x + 0.044715 * x * x * x))
    return 0.5 * x * (1.0 + t)


def _gelu_grad(x):
    t = jnp.tanh(_GC * (x + 0.044715 * x * x * x))
    return 0.5 * (1.0 + t) + 0.5 * x * (1.0 - t * t) * (_GC * (1.0 + 3.0 * 0.044715 * x * x))


def _silu(x):
    return x * _sigmoid(x)


def _silu_grad(x):
    s = _sigmoid(x)
    return s * (1.0 + x * (1.0 - s))


def _rows(shape):
    return lax.broadcasted_iota(jnp.int32, shape, 0)


def _lanes(shape):
    return lax.broadcasted_iota(jnp.int32, shape, 1)


def _shift_down(x, tail, d):
    if d == 0:
        return x
    n = x.shape[0]
    xr = pltpu.roll(x, d, 0)
    tr = pltpu.roll(tail, d, 0)
    first = jnp.where(_rows(tr.shape) < d, tr, xr[0:8])
    return jnp.concatenate([first, xr[8:n]], axis=0)


def _shift_up(x, head, d):
    if d == 0:
        return x
    n = x.shape[0]
    xr = pltpu.roll(x, n - d, 0)
    hr = pltpu.roll(head, 8 - d, 0)
    last = jnp.where(_rows(hr.shape) >= 8 - d, hr, xr[n - 8:n])
    return jnp.concatenate([xr[0:n - 8], last], axis=0)


def _keep(x, valid, s):
    return jnp.where(valid, x, 0.0) if s == 0 else x


def _row_at(x, i):
    return jnp.sum(jnp.where(_rows(x.shape) == i, x, 0.0), axis=0, keepdims=True)


def _scan_fwd(a, u):
    n = a.shape[0]
    ri = _rows(a.shape)
    d = 1
    while d < n:
        m = ri >= d
        us = jnp.where(m, pltpu.roll(u, d, 0), 0.0)
        as_ = jnp.where(m, pltpu.roll(a, d, 0), 1.0)
        u = u + a * us
        a = a * as_
        d *= 2
    return a, u


def _scan_rev(c, u):
    n = c.shape[0]
    ri = _rows(c.shape)
    d = 1
    while d < n:
        m = ri < n - d
        us = jnp.where(m, pltpu.roll(u, n - d, 0), 0.0)
        cs = jnp.where(m, pltpu.roll(c, n - d, 0), 1.0)
        u = u + c * us
        c = c * cs
        d *= 2
    return c, u


def _cumsum_fwd(x):
    n = x.shape[0]
    ri = _rows(x.shape)
    d = 1
    while d < n:
        x = x + jnp.where(ri >= d, pltpu.roll(x, d, 0), 0.0)
        d *= 2
    return x


def _cumsum_rev(x):
    n = x.shape[0]
    ri = _rows(x.shape)
    d = 1
    while d < n:
        x = x + jnp.where(ri < n - d, pltpu.roll(x, n - d, 0), 0.0)
        d *= 2
    return x


MATMUL_VMEM = 40 * 1024 * 1024


def _matmul_tiles(m, n, k, tk, out_bytes):
    best = None
    for tm in (1664, 1408, 1040, 1024, 832, 640, 512, 384, 256, 128):
        if m % tm:
            continue
        for tn in (2048, 1664, 1408, 1024, 896, 640, 512, 384, 256, 128):
            if n % tn:
                continue
            vmem = 2 * (tm * tk * 2 + tk * tn * 2 + tm * tn * out_bytes) + (tm * tn * 4 if k > tk else 0)
            if vmem > MATMUL_VMEM:
                continue
            traffic = (n // tn) * m * k * 2 + (m // tm) * k * n * 2
            if best is None or traffic < best[0]:
                best = (traffic, tm, tn)
    return (best[1], best[2]) if best else (m, n)


def matmul(a, b, *, trans_b=False, out_dtype=F32, name):
    m, k = a.shape
    n = b.shape[0] if trans_b else b.shape[1]
    tk = k if k <= 2048 else _pick(k, (1664, 1408, 1024, 896, 512, 256, 128))
    nk = k // tk
    tm, tn = _matmul_tiles(m, n, k, tk, jnp.dtype(out_dtype).itemsize)

    def product(a_ref, b_ref):
        return _dot_nt(a_ref[...], b_ref[...]) if trans_b else _dot(a_ref[...], b_ref[...])

    def kern_once(a_ref, b_ref, o_ref):
        o_ref[...] = product(a_ref, b_ref).astype(o_ref.dtype)

    def kern_acc(a_ref, b_ref, o_ref, acc_ref):
        kk = pl.program_id(2)

        @pl.when(kk == 0)
        def _():
            acc_ref[...] = product(a_ref, b_ref)

        @pl.when(kk > 0)
        def _():
            acc_ref[...] += product(a_ref, b_ref)

        @pl.when(kk == nk - 1)
        def _():
            o_ref[...] = acc_ref[...].astype(o_ref.dtype)

    b_spec = (pl.BlockSpec((tn, tk), lambda i, j, kk: (j, kk)) if trans_b
              else pl.BlockSpec((tk, tn), lambda i, j, kk: (kk, j)))
    return pl.pallas_call(
        kern_once if nk == 1 else kern_acc, name=name,
        grid=(m // tm, n // tn, nk),
        in_specs=[pl.BlockSpec((tm, tk), lambda i, j, kk: (i, kk)), b_spec],
        out_specs=pl.BlockSpec((tm, tn), lambda i, j, kk: (i, j)),
        out_shape=jax.ShapeDtypeStruct((m, n), out_dtype),
        scratch_shapes=[] if nk == 1 else [pltpu.VMEM((tm, tn), F32)],
        compiler_params=_cp("parallel", "parallel", "arbitrary"),
    )(a, b)


def matmul_cat(a_list, b, *, trans_b=False, out_dtype=F32, name, carry=None):
    m = a_list[0].shape[0]
    ks = [x.shape[1] for x in a_list]
    ktot = sum(ks)
    n = b.shape[0] if trans_b else b.shape[1]
    tn = _pick(n, (512, 256, 128))
    tm = next((c for c in (1664, 1040, 832, 640, 512, 384, 256, 128)
               if m % c == 0 and c * ktot * 2 <= 8 * 1024 * 1024), m)
    na = len(a_list)

    def kern(*refs):
        b_ref, o_ref = refs[na], refs[na + 1]
        acc, off = None, 0
        for a_ref, k in zip(refs[:na], ks):
            if trans_b:
                part = _dot_nt(a_ref[...], b_ref[:, off:off + k])
            else:
                part = _dot(a_ref[...], b_ref[off:off + k, :])
            acc = part if acc is None else acc + part
            off += k
        o_ref[...] = acc.astype(o_ref.dtype)

    b_spec = (pl.BlockSpec((tn, ktot), lambda i, j: (j, 0)) if trans_b
              else pl.BlockSpec((ktot, tn), lambda i, j: (0, j)))
    res = _call(
        kern, (*a_list, b), name=name, grid=(m // tm, n // tn),
        in_specs=[pl.BlockSpec((tm, k), lambda i, j: (i, 0)) for k in ks] + [b_spec],
        out_specs=[pl.BlockSpec((tm, tn), lambda i, j: (i, j))],
        out_shape=[jax.ShapeDtypeStruct((m, n), out_dtype)],
        sem=("parallel", "parallel"), carry=carry)
    return res[0] if carry is None else (res[0], res[1:])


def _row_tile(t):
    return _pick(t, (832, 640, 512, 384, 256, 128))


def rmsnorm_fwd(x, w, res=None, *, out_dtype, name, with_t=False, carry=None):
    t, d = x.shape
    tr = _conv_tile(t) if with_t else _row_tile(t)

    def kern(*refs):
        x_ref, w_ref = refs[0], refs[1]
        o_ref = refs[-2] if with_t else refs[-1]
        xv = x_ref[...]
        r = lax.rsqrt(jnp.mean(xv * xv, axis=-1, keepdims=True) + EPS)
        y = xv * r * w_ref[...]
        if res is not None:
            y = refs[2][...] + y
        o_ref[...] = y.astype(o_ref.dtype)
        if with_t:
            refs[-1][...] = y.T.astype(o_ref.dtype)

    row = pl.BlockSpec((tr, d), lambda i: (i, 0))
    vec = pl.BlockSpec((1, d), lambda i: (0, 0))
    ins = [x, w.reshape(1, d)] + ([] if res is None else [res])
    specs = [row, vec] + ([] if res is None else [row])
    out_specs, out_shape = [row], [jax.ShapeDtypeStruct((t, d), out_dtype)]
    if with_t:
        out_specs.append(pl.BlockSpec((d, tr), lambda i: (0, i)))
        out_shape.append(jax.ShapeDtypeStruct((d, t), out_dtype))
    res_ = _call(kern, ins, name=name, grid=(t // tr,), in_specs=specs, out_specs=out_specs,
                 out_shape=out_shape, sem=("parallel",), carry=carry)
    return res_[0] if len(res_) == 1 else res_


def rmsnorm_bwd(x, w, dy, res=None, *, out_dtype, name, carry=None):
    t, d = x.shape
    tr = _row_tile(t)

    def kern(*refs):
        if res is None:
            x_ref, w_ref, dy_ref, dx_ref, dw_ref = refs
        else:
            x_ref, w_ref, dy_ref, r_ref, dx_ref, dw_ref = refs
        i = pl.program_id(0)
        xv = x_ref[...]
        dyv = dy_ref[...].astype(F32)
        r = lax.rsqrt(jnp.mean(xv * xv, axis=-1, keepdims=True) + EPS)
        xh = xv * r
        g = dyv * w_ref[...]
        dx = r * (g - xh * jnp.mean(g * xh, axis=-1, keepdims=True))
        if res is not None:
            dx = r_ref[...] + dx
        dx_ref[...] = dx.astype(dx_ref.dtype)
        part = jnp.sum(dyv * xh, axis=0, keepdims=True)

        @pl.when(i == 0)
        def _():
            dw_ref[...] = part

        @pl.when(i > 0)
        def _():
            dw_ref[...] += part

    row = pl.BlockSpec((tr, d), lambda i: (i, 0))
    vec = pl.BlockSpec((1, d), lambda i: (0, 0))
    ins = [x, w.reshape(1, d), dy] + ([] if res is None else [res])
    specs = [row, vec, row] + ([] if res is None else [row])
    return _call(kern, ins, name=name, grid=(t // tr,), in_specs=specs, out_specs=[row, vec],
                 out_shape=[jax.ShapeDtypeStruct((t, d), out_dtype), jax.ShapeDtypeStruct((1, d), F32)],
                 sem=("arbitrary",), carry=carry)


def _conv_tile(t):
    return _pick(t, (640, 384, 256, 128))


def _conv_apply(x, tail, cw, cb, ksz):
    y = cb
    for k in range(ksz):
        y = y + cw[k:k + 1, :] * _shift_down(x, tail, ksz - 1 - k)
    return y


def dwconv_fwd(x, cw, cb, *, mode, x_off, c_out, cblk, out_dtype, name, with_t=False):
    t = x.shape[0]
    ksz = cw.shape[0]
    tb = _conv_tile(t)
    nb, ncb, t8 = t // tb, c_out // cblk, tb // 8
    xo = x_off // cblk
    nin = 2 if mode == "geglu" else 1

    def kern(*refs):
        o_ref = refs[-2] if with_t else refs[-1]
        n = pl.program_id(1)
        for c in range(cblk // BLK):
            ls = slice(c * BLK, (c + 1) * BLK)
            for s in range(tb // BLK):
                rs = slice(s * BLK, (s + 1) * BLK)
                valid = (n * tb + s * BLK + _rows((BLK, BLK))) >= PAD
                hs = []
                for q in range(nin):
                    x_ref, t_ref, w_ref, b_ref = refs[4 * q:4 * q + 4]
                    tail = (jnp.where(n > 0, t_ref[:, ls], 0.0) if s == 0
                            else x_ref[s * BLK - 8:s * BLK, ls])
                    hs.append(_conv_apply(x_ref[rs, ls], tail, w_ref[:, ls], b_ref[:, ls], ksz))
                y = _gelu(hs[0]) * hs[1] if mode == "geglu" else _silu(hs[0])
                y = _keep(y, valid, s)
                o_ref[rs, ls] = y.astype(o_ref.dtype)
                if with_t:
                    refs[-1][ls, rs] = y.T.astype(o_ref.dtype)

    ins, specs = [], []
    for q in range(nin):
        co = xo + q * ncb
        wo = q * ncb
        ins += [x, x, cw, cb.reshape(1, -1)]
        specs += [
            pl.BlockSpec((tb, cblk), lambda j, n, co=co: (n, co + j)),
            pl.BlockSpec((8, cblk), lambda j, n, co=co: (jnp.maximum(n * t8 - 1, 0), co + j)),
            pl.BlockSpec((ksz, cblk), lambda j, n, wo=wo: (0, wo + j)),
            pl.BlockSpec((1, cblk), lambda j, n, wo=wo: (0, wo + j)),
        ]
    out_specs = pl.BlockSpec((tb, cblk), lambda j, n: (n, j))
    out_shape = jax.ShapeDtypeStruct((t, c_out), out_dtype)
    if with_t:
        out_specs = [out_specs, pl.BlockSpec((cblk, tb), lambda j, n: (j, n))]
        out_shape = [out_shape, jax.ShapeDtypeStruct((c_out, t), out_dtype)]
    return pl.pallas_call(
        kern, name=name, grid=(ncb, nb), in_specs=specs, out_specs=out_specs, out_shape=out_shape,
        compiler_params=_cp("parallel", "parallel"),
    )(*ins)


def dwconv_bwd(x, cw, cb, dy, *, mode, x_off, c_out, cblk, name, carry=None):
    t = x.shape[0]
    ksz = cw.shape[0]
    tb = _conv_tile(t)
    nb, ncb, t8 = t // tb, c_out // cblk, tb // 8
    xo = x_off // cblk
    nin = 2 if mode == "geglu" else 1
    ctot = nin * c_out

    def kern(*refs):
        dy_ref = refs[4 * nin]
        outs = refs[4 * nin + 1:4 * nin + 1 + 3 * nin]
        heads = refs[4 * nin + 1 + 3 * nin:]
        n = pl.program_id(1)
        blk = nb - 1 - n

        @pl.when(n == 0)
        def _():
            for q in range(nin):
                heads[q][...] = jnp.zeros_like(heads[q])
                outs[3 * q + 1][...] = jnp.zeros_like(outs[3 * q + 1])
                outs[3 * q + 2][...] = jnp.zeros_like(outs[3 * q + 2])

        for c in range(cblk // BLK):
            ls = slice(c * BLK, (c + 1) * BLK)
            head = [heads[q][:, ls] for q in range(nin)]
            dwa = [[None] * ksz for _ in range(nin)]
            dba = [None] * nin
            for s in reversed(range(tb // BLK)):
                rs = slice(s * BLK, (s + 1) * BLK)
                valid = (blk * tb + s * BLK + _rows((BLK, BLK))) >= PAD
                xs, tails, hs = [], [], []
                for q in range(nin):
                    x_ref, t_ref, w_ref, b_ref = refs[4 * q:4 * q + 4]
                    tail = (jnp.where(blk > 0, t_ref[:, ls], 0.0) if s == 0
                            else x_ref[s * BLK - 8:s * BLK, ls])
                    xs.append(x_ref[rs, ls])
                    tails.append(tail)
                    hs.append(_conv_apply(xs[q], tail, w_ref[:, ls], b_ref[:, ls], ksz))
                dyv = dy_ref[rs, ls].astype(F32)
                if mode == "geglu":
                    dhs = [dyv * hs[1] * _gelu_grad(hs[0]), dyv * _gelu(hs[0])]
                else:
                    dhs = [dyv * _silu_grad(hs[0])]
                for q in range(nin):
                    w_ref = refs[4 * q + 2]
                    dh = _keep(dhs[q], valid, s)
                    dx = jnp.zeros_like(dh)
                    for k in range(ksz):
                        sh = ksz - 1 - k
                        dx = dx + w_ref[k:k + 1, ls] * _shift_up(dh, head[q], sh)
                        part = jnp.sum(dh * _shift_down(xs[q], tails[q], sh), axis=0, keepdims=True)
                        dwa[q][k] = part if dwa[q][k] is None else dwa[q][k] + part
                    outs[3 * q][rs, ls] = _keep(dx, valid, s).astype(outs[3 * q].dtype)
                    part = jnp.sum(dh, axis=0, keepdims=True)
                    dba[q] = part if dba[q] is None else dba[q] + part
                    head[q] = dh[0:8]
            for q in range(nin):
                outs[3 * q + 1][:, ls] += jnp.concatenate(dwa[q], axis=0)
                outs[3 * q + 2][:, ls] += dba[q]
                heads[q][:, ls] = head[q]

    ins, specs, out_specs, out_shape, scratch = [], [], [], [], []
    for q in range(nin):
        co = xo + q * ncb
        wo = q * ncb
        ins += [x, x, cw, cb.reshape(1, -1)]
        specs += [
            pl.BlockSpec((tb, cblk), lambda j, n, co=co: (nb - 1 - n, co + j)),
            pl.BlockSpec((8, cblk), lambda j, n, co=co: (jnp.maximum((nb - 1 - n) * t8 - 1, 0), co + j)),
            pl.BlockSpec((ksz, cblk), lambda j, n, wo=wo: (0, wo + j)),
            pl.BlockSpec((1, cblk), lambda j, n, wo=wo: (0, wo + j)),
        ]
        out_specs += [
            pl.BlockSpec((tb, cblk), lambda j, n: (nb - 1 - n, j)),
            pl.BlockSpec((ksz, cblk), lambda j, n: (0, j)),
            pl.BlockSpec((1, cblk), lambda j, n: (0, j)),
        ]
        out_shape += [jax.ShapeDtypeStruct((t, c_out), MXU),
                      jax.ShapeDtypeStruct((ksz, c_out), F32),
                      jax.ShapeDtypeStruct((1, c_out), F32)]
        scratch.append(pltpu.VMEM((8, cblk), F32))
    ins.append(dy)
    specs.append(pl.BlockSpec((tb, cblk), lambda j, n: (nb - 1 - n, j)))
    res = _call(kern, ins, name=name, grid=(ncb, nb), in_specs=specs, out_specs=out_specs,
                out_shape=out_shape, scratch_shapes=scratch, sem=("parallel", "arbitrary"), carry=carry)
    dxs = [res[3 * q] for q in range(nin)]
    dcw = jnp.concatenate([res[3 * q + 1] for q in range(nin)], axis=1)
    dcb = jnp.concatenate([res[3 * q + 2] for q in range(nin)], axis=1)
    return dxs, dcw, dcb.reshape(ctot), res[3 * nin:]


def _lru_tile(t):
    return _pick(t, (640, 384, 256, 128))


def _lru_gates(xc, wa, ba, wx, bx, sp):
    r = _sigmoid(_dot(xc, wa) + ba)
    i = _sigmoid(_dot(xc, wx) + bx)
    log_a = -LRU_C * r * sp
    a = jnp.exp(log_a)
    mult = jnp.sqrt(-_expm1(2.0 * log_a))
    return r, i, a, mult


def lru_fwd(proj, cw, cb, wa, ba, wx, bx, lam, *, gate_off, xr_off, name, carry=None):
    t = proj.shape[0]
    tb = _lru_tile(t)
    nb, ns, t8 = t // tb, tb // BLK, tb // 8
    go, xo = gate_off // BLK, xr_off // BLK

    def kern(g_ref, x_ref, xt_ref, cw_ref, cb_ref, wa_ref, ba_ref, wx_ref, bx_ref, lam_ref,
             y_ref, yt_ref, h_ref, hc_ref):
        n = pl.program_id(1)

        @pl.when(n == 0)
        def _():
            hc_ref[...] = jnp.zeros_like(hc_ref)

        sp = _softplus(-lam_ref[...])
        hprev = hc_ref[0:1, :]
        scans = []
        for s in range(ns):
            sl = slice(s * BLK, (s + 1) * BLK)
            xv = x_ref[sl, :]
            tail = jnp.where(n > 0, xt_ref[...], 0.0) if s == 0 else x_ref[s * BLK - 8:s * BLK, :]
            valid = (n * tb + s * BLK + _rows((BLK, BLK))) >= PAD
            xc = _keep(_conv_apply(xv, tail, cw_ref[...], cb_ref[...], 4), valid, s)
            _, i, a, mult = _lru_gates(xc, wa_ref[0], ba_ref[...], wx_ref[0], bx_ref[...], sp)
            scans.append(_scan_fwd(a, mult * (i * xc)))
        for s in range(ns):
            sl = slice(s * BLK, (s + 1) * BLK)
            ca, cu = scans[s]
            h = cu + ca * hprev
            hprev = _row_at(h, BLK - 1)
            h_ref[sl, :] = h
            y = _gelu(g_ref[sl, :]) * h
            y_ref[sl, :] = y.astype(y_ref.dtype)
            yt_ref[:, sl] = y.T.astype(yt_ref.dtype)
        hc_ref[...] = jnp.broadcast_to(hprev, hc_ref.shape)

    vec = pl.BlockSpec((1, BLK), lambda j, n: (0, j))
    mat = pl.BlockSpec((1, BLK, BLK), lambda j, n: (j, 0, 0))
    return _call(
        kern, (proj, proj, proj, cw, cb.reshape(1, -1), wa, ba.reshape(1, -1), wx, bx.reshape(1, -1),
               lam.reshape(1, -1)),
        name=name, grid=(D_RNN // BLK, nb),
        in_specs=[
            pl.BlockSpec((tb, BLK), lambda j, n: (n, go + j)),
            pl.BlockSpec((tb, BLK), lambda j, n: (n, xo + j)),
            pl.BlockSpec((8, BLK), lambda j, n: (jnp.maximum(n * t8 - 1, 0), xo + j)),
            pl.BlockSpec((4, BLK), lambda j, n: (0, j)), vec, mat, vec, mat, vec, vec,
        ],
        out_specs=[pl.BlockSpec((tb, BLK), lambda j, n: (n, j)),
                   pl.BlockSpec((BLK, tb), lambda j, n: (j, n)),
                   pl.BlockSpec((tb, BLK), lambda j, n: (n, j))],
        out_shape=[jax.ShapeDtypeStruct((t, D_RNN), MXU), jax.ShapeDtypeStruct((D_RNN, t), MXU),
                   jax.ShapeDtypeStruct((t, D_RNN), F32)],
        scratch_shapes=[pltpu.VMEM((8, BLK), F32)],
        sem=("parallel", "arbitrary"), carry=carry)


def lru_bwd(proj, h, dy, cw, cb, wa, ba, wx, bx, lam, *, gate_off, xr_off, dy_off, name, carry=None):
    t = proj.shape[0]
    tb = _lru_tile(t)
    nb, ns, t8 = t // tb, tb // BLK, tb // 8
    go, xo, do = gate_off // BLK, xr_off // BLK, dy_off // BLK

    def kern(g_ref, x_ref, xt_ref, h_ref, ht_ref, dy_ref, cw_ref, cb_ref, wa_ref, ba_ref,
             wx_ref, bx_ref, lam_ref,
             dg_ref, dx_ref, dcw_ref, dcb_ref, dwa_ref, dba_ref, dwx_ref, dbx_ref, dlam_ref,
             gin_ref, head_ref):
        n = pl.program_id(1)
        blk = nb - 1 - n

        @pl.when(n == 0)
        def _():
            gin_ref[...] = jnp.zeros_like(gin_ref)
            head_ref[...] = jnp.zeros_like(head_ref)
            for r_ in (dcw_ref, dcb_ref, dwa_ref, dba_ref, dwx_ref, dbx_ref, dlam_ref):
                r_[...] = jnp.zeros_like(r_)

        lamv = lam_ref[...]
        sp = _softplus(-lamv)
        dsp_dlam = -_sigmoid(-lamv)
        g_in = gin_ref[0:1, :]
        head = head_ref[...]
        ones8 = jnp.ones((8, BLK), F32)
        wav, wxv = wa_ref[0], wx_ref[0]
        staged = {}
        for s in range(ns):
            sl = slice(s * BLK, (s + 1) * BLK)
            xv = x_ref[sl, :]
            if s == 0:
                tail = jnp.where(blk > 0, xt_ref[...], 0.0)
                htail = jnp.where(blk > 0, ht_ref[...], 0.0)
            else:
                tail = x_ref[s * BLK - 8:s * BLK, :]
                htail = h_ref[s * BLK - 8:s * BLK, :]
            valid = (blk * tb + s * BLK + _rows((BLK, BLK))) >= PAD
            xc = _keep(_conv_apply(xv, tail, cw_ref[...], cb_ref[...], 4), valid, s)
            r, i, a, mult = _lru_gates(xc, wav, ba_ref[...], wxv, bx_ref[...], sp)
            hv = h_ref[sl, :]
            hprev = _shift_down(hv, htail, 1)
            gv = g_ref[sl, :]
            dyv = dy_ref[sl, :].astype(F32)
            dg_ref[sl, :] = (dyv * hv * _gelu_grad(gv)).astype(dg_ref.dtype)
            cc, cu = _scan_rev(_shift_up(a, ones8, 1), dyv * _gelu(gv))
            staged[s] = (xv, tail, valid, xc, r, i, a, mult, hprev, cc, cu)
        for s in reversed(range(ns)):
            sl = slice(s * BLK, (s + 1) * BLK)
            xv, tail, valid, xc, r, i, a, mult, hprev, cc, cu = staged[s]
            gg = cu + cc * g_in
            g_in = _row_at(a * gg, 0)
            da = gg * hprev
            di = gg * mult * xc
            dxc = gg * mult * i
            dmult = gg * i * xc
            dlog_a = da * a - dmult * (a * a) / mult
            dr = dlog_a * (-LRU_C * sp)
            dlam_ref[...] += jnp.sum(dlog_a * (-LRU_C) * r, axis=0, keepdims=True) * dsp_dlam
            dpr = dr * r * (1.0 - r)
            dpi = di * i * (1.0 - i)
            dxc = dxc + _dot_nt(dpr, wav) + _dot_nt(dpi, wxv)
            dxc, dpr, dpi = _keep(dxc, valid, s), _keep(dpr, valid, s), _keep(dpi, valid, s)
            dwa_ref[0] += _dot_tn(xc, dpr)
            dwx_ref[0] += _dot_tn(xc, dpi)
            dba_ref[...] += jnp.sum(dpr, axis=0, keepdims=True)
            dbx_ref[...] += jnp.sum(dpi, axis=0, keepdims=True)
            dx = jnp.zeros_like(dxc)
            dws = []
            for k in range(4):
                dx = dx + cw_ref[k:k + 1, :] * _shift_up(dxc, head, 3 - k)
                dws.append(jnp.sum(dxc * _shift_down(xv, tail, 3 - k), axis=0, keepdims=True))
            dx_ref[sl, :] = _keep(dx, valid, s).astype(dx_ref.dtype)
            dcw_ref[...] += jnp.concatenate(dws, axis=0)
            dcb_ref[...] += jnp.sum(dxc, axis=0, keepdims=True)
            head = dxc[0:8]
        gin_ref[...] = jnp.broadcast_to(g_in, gin_ref.shape)
        head_ref[...] = head

    vec = pl.BlockSpec((1, BLK), lambda j, n: (0, j))
    mat = pl.BlockSpec((1, BLK, BLK), lambda j, n: (j, 0, 0))
    cws = pl.BlockSpec((4, BLK), lambda j, n: (0, j))

    def rb(off):
        return pl.BlockSpec((tb, BLK), lambda j, n: (nb - 1 - n, off + j))

    def tl(off):
        return pl.BlockSpec((8, BLK), lambda j, n: (jnp.maximum((nb - 1 - n) * t8 - 1, 0), off + j))

    return _call(
        kern, (proj, proj, proj, h, h, dy, cw, cb.reshape(1, -1), wa, ba.reshape(1, -1), wx,
               bx.reshape(1, -1), lam.reshape(1, -1)),
        name=name, grid=(D_RNN // BLK, nb),
        in_specs=[rb(go), rb(xo), tl(xo), rb(0), tl(0), rb(do), cws, vec, mat, vec, mat, vec, vec],
        out_specs=[rb(0), rb(0), cws, vec, mat, vec, mat, vec, vec],
        out_shape=[jax.ShapeDtypeStruct((t, D_RNN), MXU), jax.ShapeDtypeStruct((t, D_RNN), MXU),
                   jax.ShapeDtypeStruct((4, D_RNN), F32), jax.ShapeDtypeStruct((1, D_RNN), F32),
                   jax.ShapeDtypeStruct((8, BLK, BLK), F32), jax.ShapeDtypeStruct((1, D_RNN), F32),
                   jax.ShapeDtypeStruct((8, BLK, BLK), F32), jax.ShapeDtypeStruct((1, D_RNN), F32),
                   jax.ShapeDtypeStruct((1, D_RNN), F32)],
        scratch_shapes=[pltpu.VMEM((8, BLK), F32), pltpu.VMEM((8, BLK), F32)],
        sem=("parallel", "arbitrary"), carry=carry)


_SCALE = HEAD_DIM ** -0.5


STK = 4


def _attn_masks(n):
    qi = np.arange(STK * BLK)[:, None] % BLK
    c = np.arange(3 * BLK)[None, :]
    tq = n * BLK + qi - PAD
    s_band = (n - 1) * BLK + c - PAD
    d_band = tq - s_band
    ok_band = (s_band >= N_META) & (d_band >= 0) & (d_band < BLK)
    jm = c - 2 * BLK
    d_meta = tq - (jm - PAD)
    ok_meta = (jm >= PAD) & (d_meta >= 0)
    is_band = c < 2 * BLK
    ok = np.where(is_band, ok_band, ok_meta)
    dist = np.where(is_band, d_band, np.minimum(d_meta, BLK)).astype(np.float32)
    return ok, dist


def _stack_heads(g, e):
    return [8 * g + 2 * i + e for i in range(STK)]


def _attn_bias_table():
    tabs = []
    for n in range(3):
        ok, dist = _attn_masks(n)
        per = []
        for g in range(2):
            for e in range(2):
                slope = np.repeat(np.array([2.0 ** (-8.0 * (h + 1) / N_Q_HEADS) for h in _stack_heads(g, e)],
                                           np.float32), BLK)[:, None]
                per.append(np.where(ok, -(slope * dist), np.float32(NEG)).astype(np.float32))
        tabs.append(np.stack(per))
    return jnp.asarray(np.stack(tabs))


def _stack_sinks(heads, sk):
    return jnp.concatenate(
        [jnp.broadcast_to(jnp.sum(jnp.where(_lanes(sk.shape) == h, sk, 0.0), axis=1, keepdims=True),
                          (BLK, 1)) for h in heads], axis=0)


def _stack_tiles(ref, g, sel):
    return jnp.concatenate(
        [jnp.where(sel, ref[:, (4 * g + i) * BLK:(4 * g + i + 1) * BLK].astype(F32), 0.0)
         for i in range(STK)], axis=0)


def _attn_probs(qk, bias, sink):
    s = qk * _SCALE + bias
    mx = jnp.maximum(jnp.max(s, axis=-1, keepdims=True), sink)
    p = jnp.exp(s - mx)
    es = jnp.exp(sink - mx)
    inv = 1.0 / (jnp.sum(p, axis=-1, keepdims=True) + es)
    return p * inv, es * inv


def _attn_specs(t, q_off, k_off, v_off, rev):
    nb = t // BLK
    qo, ko, vo = q_off // 1024, k_off // BLK, v_off // BLK

    def b(n):
        return nb - 1 - n if rev else n

    return [
        pl.BlockSpec((BLK, 1024), lambda n: (b(n), qo)),
        pl.BlockSpec((BLK, BLK), lambda n: (b(n), ko)),
        pl.BlockSpec((BLK, BLK), lambda n: (b(n), vo)),
        pl.BlockSpec((BLK, BLK), lambda n: (jnp.maximum(b(n) - 1, 0), ko)),
        pl.BlockSpec((BLK, BLK), lambda n: (jnp.maximum(b(n) - 1, 0), vo)),
        pl.BlockSpec((BLK, BLK), lambda n: (0, ko)),
        pl.BlockSpec((BLK, BLK), lambda n: (0, vo)),
        pl.BlockSpec((1, BLK), lambda n: (0, 0)),
        pl.BlockSpec((1, 4, STK * BLK, 3 * BLK), lambda n: (jnp.minimum(b(n), 2), 0, 0, 0)),
    ]


def attn_fwd(proj, sinks, *, q_off, k_off, v_off, name, carry=None):
    t = proj.shape[0]
    nb = t // BLK

    def kern(q_ref, kc_ref, vc_ref, kp_ref, vp_ref, km_ref, vm_ref, sk_ref, tab_ref, o_ref):
        k_all = jnp.concatenate([kp_ref[...], kc_ref[...], km_ref[...]], axis=0)
        v_all = jnp.concatenate([vp_ref[...], vc_ref[...], vm_ref[...]], axis=0)
        k_alt = pltpu.roll(k_all, HEAD_DIM, 1)
        v_alt = pltpu.roll(v_all, HEAD_DIM, 1)
        low = _lanes((BLK, BLK)) < HEAD_DIM
        stacks = [(g, e) for g in range(2) for e in range(2)]
        qk = {(g, e): _dot_nt(_stack_tiles(q_ref, g, low == (e == 0)), k_all if g == e else k_alt)
              for g, e in stacks}
        ps = {(g, e): _attn_probs(qk[g, e], tab_ref[0, 2 * g + e],
                                  _stack_sinks(_stack_heads(g, e), sk_ref[...]))[0] for g, e in stacks}
        outs = {(g, e): _dot(ps[g, e], v_all if g == e else v_alt) for g, e in stacks}
        for hp in range(N_Q_HEADS // 2):
            g, rs = hp // STK, slice((hp % STK) * BLK, (hp % STK + 1) * BLK)
            o_ref[:, hp * BLK:(hp + 1) * BLK] = jnp.where(low, outs[g, 0][rs], outs[g, 1][rs]).astype(o_ref.dtype)

    sk = jnp.zeros((1, BLK), F32).at[0, :N_Q_HEADS].set(sinks)
    return _call(
        kern, (proj, proj, proj, proj, proj, proj, proj, sk, _attn_bias_table()), name=name, grid=(nb,),
        in_specs=_attn_specs(t, q_off, k_off, v_off, False),
        out_specs=[pl.BlockSpec((BLK, 1024), lambda n: (n, 0))],
        out_shape=[jax.ShapeDtypeStruct((t, 1024), MXU)],
        sem=("parallel",), carry=carry)


def attn_bwd(proj, sinks, dy, *, q_off, k_off, v_off, dy_off, name, carry=None):
    t = proj.shape[0]
    nb = t // BLK
    do = dy_off // 1024

    def kern(q_ref, kc_ref, vc_ref, kp_ref, vp_ref, km_ref, vm_ref, sk_ref, tab_ref, do_ref,
             dq_ref, dk_ref, dv_ref, dsk_ref, ck_ref, cv_ref, mk_ref, mv_ref):
        n = pl.program_id(0)
        blk = nb - 1 - n

        @pl.when(n == 0)
        def _():
            for r_ in (ck_ref, cv_ref, mk_ref, mv_ref, dsk_ref):
                r_[...] = jnp.zeros_like(r_)

        k_all = jnp.concatenate([kp_ref[...], kc_ref[...], km_ref[...]], axis=0)
        v_all = jnp.concatenate([vp_ref[...], vc_ref[...], vm_ref[...]], axis=0)
        k_alt = pltpu.roll(k_all, HEAD_DIM, 1)
        v_alt = pltpu.roll(v_all, HEAD_DIM, 1)
        low = _lanes((BLK, BLK)) < HEAD_DIM
        lane1 = _lanes((1, BLK))
        dk_all = jnp.zeros((3 * BLK, BLK), F32)
        dv_all = jnp.zeros((3 * BLK, BLK), F32)
        dsk = jnp.zeros((1, BLK), F32)
        stacks = [(g, e) for g in range(2) for e in range(2)]
        qm = {(g, e): _stack_tiles(q_ref, g, low == (e == 0)) for g, e in stacks}
        dom = {(g, e): _stack_tiles(do_ref, g, low == (e == 0)) for g, e in stacks}
        qk = {(g, e): _dot_nt(qm[g, e], k_all if g == e else k_alt) for g, e in stacks}
        dp = {(g, e): _dot_nt(dom[g, e], v_all if g == e else v_alt) for g, e in stacks}
        ps, dss = {}, {}
        for g, e in stacks:
            heads = _stack_heads(g, e)
            p, psink = _attn_probs(qk[g, e], tab_ref[0, 2 * g + e], _stack_sinks(heads, sk_ref[...]))
            delta = jnp.sum(p * dp[g, e], axis=-1, keepdims=True)
            ps[g, e] = p
            dss[g, e] = p * (dp[g, e] - delta) * _SCALE
            psd = psink * delta
            for i, h in enumerate(heads):
                dsk = dsk + jnp.where(lane1 == h, -jnp.sum(psd[i * BLK:(i + 1) * BLK], axis=0, keepdims=True), 0.0)
        dqs = {(g, e): _dot(dss[g, e], k_all if g == e else k_alt) for g, e in stacks}
        for g, e in stacks:
            dkh = _dot_tn(dss[g, e], qm[g, e])
            dvh = _dot_tn(ps[g, e], dom[g, e])
            if g != e:
                dkh = pltpu.roll(dkh, HEAD_DIM, 1)
                dvh = pltpu.roll(dvh, HEAD_DIM, 1)
            dk_all = dk_all + dkh
            dv_all = dv_all + dvh
        for hp in range(N_Q_HEADS // 2):
            g, rs = hp // STK, slice((hp % STK) * BLK, (hp % STK + 1) * BLK)
            dq_ref[:, hp * BLK:(hp + 1) * BLK] = jnp.where(low, dqs[g, 0][rs], dqs[g, 1][rs]).astype(dq_ref.dtype)
        dsk_ref[...] += dsk
        mk_ref[...] += dk_all[2 * BLK:3 * BLK]
        mv_ref[...] += dv_all[2 * BLK:3 * BLK]
        is0 = blk == 0
        dk_ref[...] = (dk_all[BLK:2 * BLK] + ck_ref[...] + jnp.where(is0, mk_ref[...], 0.0)).astype(dk_ref.dtype)
        dv_ref[...] = (dv_all[BLK:2 * BLK] + cv_ref[...] + jnp.where(is0, mv_ref[...], 0.0)).astype(dv_ref.dtype)
        ck_ref[...] = dk_all[0:BLK]
        cv_ref[...] = dv_all[0:BLK]

    sk = jnp.zeros((1, BLK), F32).at[0, :N_Q_HEADS].set(sinks)
    kv = pl.BlockSpec((BLK, BLK), lambda n: (nb - 1 - n, 0))
    res = _call(
        kern, (proj, proj, proj, proj, proj, proj, proj, sk, _attn_bias_table(), dy), name=name, grid=(nb,),
        in_specs=_attn_specs(t, q_off, k_off, v_off, True)
        + [pl.BlockSpec((BLK, 1024), lambda n: (nb - 1 - n, do))],
        out_specs=[pl.BlockSpec((BLK, 1024), lambda n: (nb - 1 - n, 0)), kv, kv,
                   pl.BlockSpec((1, BLK), lambda n: (0, 0))],
        out_shape=[jax.ShapeDtypeStruct((t, 1024), MXU), jax.ShapeDtypeStruct((t, BLK), MXU),
                   jax.ShapeDtypeStruct((t, BLK), MXU), jax.ShapeDtypeStruct((1, BLK), F32)],
        scratch_shapes=[pltpu.VMEM((BLK, BLK), F32)] * 4,
        sem=("arbitrary",), carry=carry)
    return [res[0], res[1], res[2], res[3][0, :N_Q_HEADS]] + res[4:]


GW = D_SSM // SSD_GROUPS
EXP_ROWS = 3 * BLK + 8
RED_ROWS = EXP_ROWS + 8


def _head_expand():
    ch = jnp.arange(D_SSM) // HEAD_DIM
    return (jnp.arange(BLK)[:, None] == ch[None, :]).astype(BF16)


def _ssd_decay(raw, dtb, alog, rowv):
    valid = rowv & (_lanes((BLK, BLK)) < SSD_HEADS)
    pre = raw + dtb
    dtp = jnp.where(valid, _softplus(pre), 0.0)
    av = -jnp.exp(alog)
    cs = _cumsum_fwd(dtp * av)
    cs_last = _row_at(cs, BLK - 1)
    return valid, pre, dtp, av, cs, jnp.exp(cs), jnp.exp(cs_last - cs), jnp.exp(cs_last)


def _head_col(x, h):
    return jnp.sum(jnp.where(_lanes(x.shape) == h, x, 0.0), axis=1, keepdims=True)


def _ssd_group_fwd(g, xdt, cs, cst, cb, tril, low):
    lm = []
    for k in range(4):
        h = 4 * g + k
        seg = _head_col(cs, h) - _row_at(cst, h)
        lmat = jnp.where(tril, jnp.exp(jnp.minimum(seg, 0.0)), 0.0)
        lm.append((lmat, cb * lmat))
    hv = [_dot(lm[k][1], xdt[:, g * GW + (k // 2) * BLK:g * GW + (k // 2 + 1) * BLK]) for k in range(4)]
    return jnp.concatenate([jnp.where(low, hv[0], hv[1]), jnp.where(low, hv[2], hv[3])], axis=1), lm


def ssd_decay(proj, dt_bias, a_log, *, dt_off, name):
    t = proj.shape[0]
    tb = _conv_tile(t)
    dto = dt_off // BLK

    def kern(dt_ref, dtb_ref, alog_ref, o_ref):
        n = pl.program_id(0)
        for s in range(tb // BLK):
            rs = slice(s * BLK, (s + 1) * BLK)
            rowv = (n * tb + s * BLK + _rows((BLK, BLK))) >= PAD
            _, _, dtp, _, cs, ecs, w, _ = _ssd_decay(dt_ref[rs, :], dtb_ref[...], alog_ref[...], rowv)
            for k, v in enumerate((dtp, cs, ecs, w)):
                o_ref[rs, k * BLK:(k + 1) * BLK] = v

    vec = pl.BlockSpec((1, BLK), lambda n: (0, 0))
    return pl.pallas_call(
        kern, name=name, grid=(t // tb,),
        in_specs=[pl.BlockSpec((tb, BLK), lambda n: (n, dto)), vec, vec],
        out_specs=pl.BlockSpec((tb, 4 * BLK), lambda n: (n, 0)),
        out_shape=jax.ShapeDtypeStruct((t, 4 * BLK), F32),
        compiler_params=_cp("parallel"),
    )(proj, _pad128(dt_bias), _pad128(a_log))


def _load_decay(d_ref):
    dtp, cs, ecs, w = (d_ref[:, k * BLK:(k + 1) * BLK] for k in range(4))
    return dtp, cs, ecs, w, _row_at(ecs, BLK - 1)


def _expand_heads(dtp, ecs, w, dec, e):
    ex = _dot(jnp.concatenate([dtp, ecs, w, jnp.broadcast_to(dec, (8, BLK))], axis=0), e)
    return ex[0:BLK], ex[BLK:2 * BLK], ex[2 * BLK:3 * BLK], jnp.max(ex[3 * BLK:EXP_ROWS], axis=0, keepdims=True)


def _ssd_specs(t, z_off, dt_off, rev):
    nb = t // BLK
    zo, dto = z_off // D_SSM, dt_off // BLK

    def b(n):
        return nb - 1 - n if rev else n

    vec = lambda w: pl.BlockSpec((1, w), lambda n: (0, 0))
    return [
        pl.BlockSpec((BLK, D_SSM), lambda n: (b(n), 0)),
        pl.BlockSpec((BLK, 1024), lambda n: (b(n), 2)),
        pl.BlockSpec((BLK, 1024), lambda n: (b(n), 3)),
        pl.BlockSpec((BLK, D_SSM), lambda n: (b(n), zo)),
        pl.BlockSpec((BLK, BLK), lambda n: (b(n), dto)),
        vec(BLK), vec(BLK), vec(D_SSM), vec(D_SSM),
        pl.BlockSpec((BLK, D_SSM), lambda n: (0, 0)),
        pl.BlockSpec((BLK, 4 * BLK), lambda n: (b(n), 0)),
    ]


def _pad128(v):
    return jnp.zeros((1, BLK), F32).at[0, :v.shape[0]].set(v)


def ssd_fwd(xbc, proj, decay, dt_bias, a_log, d_skip, gate_norm, *, z_off, dt_off, name):
    t = xbc.shape[0]
    nb = t // BLK

    def kern(x_ref, b_ref, c_ref, z_ref, dt_ref, dtb_ref, alog_ref, dsk_ref, gn_ref, e_ref, d_ref,
             yn_ref, ynt_ref, st_ref, p_ref):
        n = pl.program_id(0)

        @pl.when(n == 0)
        def _():
            p_ref[...] = jnp.zeros_like(p_ref)

        bgs = [b_ref[:, g * BLK:(g + 1) * BLK] for g in range(SSD_GROUPS)]
        cgs = [c_ref[:, g * BLK:(g + 1) * BLK] for g in range(SSD_GROUPS)]
        cbs = [_dot_nt(cgs[g], bgs[g]) for g in range(SSD_GROUPS)]
        pgs = [p_ref[g] for g in range(SSD_GROUPS)]
        zs = [_dot(cgs[g], pgs[g]) for g in range(SSD_GROUPS)]
        new_p = []
        dtp, cs, ecs, w, dec = _load_decay(d_ref)
        dtp_c, ecs_c, w_c, dec_c = _expand_heads(dtp, ecs, w, dec, e_ref[...])
        xv = x_ref[...]
        xdt = xv * dtp_c
        wx = w_c * xdt
        cst = cs.T
        tril = _rows((BLK, BLK)) >= _lanes((BLK, BLK))
        low = _lanes((BLK, BLK)) < HEAD_DIM
        for g in range(SSD_GROUPS):
            st_ref[0, g] = pgs[g]
        for g in range(SSD_GROUPS):
            gs = slice(g * GW, (g + 1) * GW)
            ydiag, _ = _ssd_group_fwd(g, xdt, cs, cst, cbs[g], tril, low)
            y = ydiag + zs[g] * ecs_c[:, gs] + dsk_ref[:, gs] * xv[:, gs]
            new_p.append(pgs[g] * dec_c[:, gs] + _dot_tn(bgs[g], wx[:, gs]))
            yz = y * _silu(z_ref[:, gs])
            r = lax.rsqrt(jnp.mean(yz * yz, axis=-1, keepdims=True) + EPS)
            yn = yz * r * gn_ref[:, gs]
            yn_ref[:, gs] = yn.astype(yn_ref.dtype)
            ynt_ref[gs, :] = yn.T.astype(ynt_ref.dtype)
        for g in range(SSD_GROUPS):
            p_ref[g] = new_p[g]

    return pl.pallas_call(
        kern, name=name, grid=(nb,),
        in_specs=_ssd_specs(t, z_off, dt_off, False),
        out_specs=[pl.BlockSpec((BLK, D_SSM), lambda n: (n, 0)),
                   pl.BlockSpec((D_SSM, BLK), lambda n: (0, n)),
                   pl.BlockSpec((1, SSD_GROUPS, BLK, GW), lambda n: (n, 0, 0, 0))],
        out_shape=[jax.ShapeDtypeStruct((t, D_SSM), MXU), jax.ShapeDtypeStruct((D_SSM, t), MXU),
                   jax.ShapeDtypeStruct((nb, SSD_GROUPS, BLK, GW), F32)],
        scratch_shapes=[pltpu.VMEM((SSD_GROUPS, BLK, GW), F32)],
        compiler_params=_cp("arbitrary"),
    )(xbc, xbc, xbc, proj, proj, _pad128(dt_bias), _pad128(a_log),
      jnp.repeat(d_skip, HEAD_DIM).reshape(1, D_SSM), gate_norm.reshape(1, D_SSM), _head_expand(), decay)


def ssd_bwd(xbc, proj, decay, st, dyn, dt_bias, a_log, d_skip, gate_norm, *, z_off, dt_off, name, carry=None):
    t = xbc.shape[0]
    nb = t // BLK

    def kern(x_ref, b_ref, c_ref, z_ref, dt_ref, dtb_ref, alog_ref, dsk_ref, gn_ref, e_ref, d_ref,
             et_ref, st_ref, dyn_ref,
             dxbc_ref, dz_ref, draw_ref, dgn_ref, ddsk_ref, ddtb_ref, dalog_ref,
             dp_ref, tr_ref):
        n = pl.program_id(0)
        blk = nb - 1 - n

        @pl.when(n == 0)
        def _():
            for r_ in (dp_ref, dgn_ref, ddsk_ref, ddtb_ref, dalog_ref):
                r_[...] = jnp.zeros_like(r_)

        bgs = [b_ref[:, g * BLK:(g + 1) * BLK] for g in range(SSD_GROUPS)]
        cgs = [c_ref[:, g * BLK:(g + 1) * BLK] for g in range(SSD_GROUPS)]
        cbs = [_dot_nt(cgs[g], bgs[g]) for g in range(SSD_GROUPS)]
        pgs = [st_ref[0, g] for g in range(SSD_GROUPS)]
        dpns = [dp_ref[g] for g in range(SSD_GROUPS)]
        zs = [_dot(cgs[g], pgs[g]) for g in range(SSD_GROUPS)]
        dwxs = [_dot(bgs[g], dpns[g]) for g in range(SSD_GROUPS)]
        new_dp, dgn_parts = [], []
        valid = ((blk * BLK + _rows((BLK, BLK))) >= PAD) & (_lanes((BLK, BLK)) < SSD_HEADS)
        pre = dt_ref[...] + dtb_ref[...]
        av = -jnp.exp(alog_ref[...])
        dtp, cs, ecs, w, dec = _load_decay(d_ref)
        dtp_c, ecs_c, w_c, dec_c = _expand_heads(dtp, ecs, w, dec, e_ref[...])
        xv = x_ref[...]
        xdt = xv * dtp_c
        wx = w_c * xdt
        cst = cs.T
        tril = _rows((BLK, BLK)) >= _lanes((BLK, BLK))
        lane = _lanes((BLK, BLK))
        rowi = _rows((BLK, BLK))
        low = lane < HEAD_DIM
        dcs = jnp.zeros((BLK, BLK), F32)
        dcst = jnp.zeros((BLK, BLK), F32)
        for g in range(SSD_GROUPS):
            gs = slice(g * GW, (g + 1) * GW)
            bg, cg = bgs[g], cgs[g]
            pg, dpn = pgs[g], dpns[g]
            xg = xv[:, gs]
            ydiag, lm = _ssd_group_fwd(g, xdt, cs, cst, cbs[g], tril, low)
            yoff = zs[g] * ecs_c[:, gs]
            y = ydiag + yoff + dsk_ref[:, gs] * xg
            zz = z_ref[:, gs]
            sz = _silu(zz)
            yz = y * sz
            r = lax.rsqrt(jnp.mean(yz * yz, axis=-1, keepdims=True) + EPS)
            yhat = yz * r
            dynv = dyn_ref[:, gs].astype(F32)
            gy = dynv * gn_ref[:, gs]
            dgn_parts.append(jnp.sum(dynv * yhat, axis=0, keepdims=True))
            dyz = r * (gy - yhat * jnp.mean(gy * yhat, axis=-1, keepdims=True))
            dy = dyz * sz
            dz_ref[:, gs] = (dyz * y * _silu_grad(zz)).astype(dz_ref.dtype)
            tr_ref[EXP_ROWS:RED_ROWS, gs] = jnp.broadcast_to(
                jnp.sum(dy * xg, axis=0, keepdims=True), (8, GW))
            dx = dsk_ref[:, gs] * dy
            dwx = dwxs[g]
            dxdt = w_c[:, gs] * dwx
            tr_ref[0:BLK, gs] = dwx * wx[:, gs]
            dbg = _dot_nt(wx[:, gs], dpn)
            dzo = ecs_c[:, gs] * dy
            tr_ref[BLK:2 * BLK, gs] = dy * yoff
            dcg = _dot_nt(dzo, pg)
            new_dp.append(dec_c[:, gs] * dpn + _dot_tn(cg, dzo))
            tr_ref[3 * BLK:EXP_ROWS, gs] = jnp.broadcast_to(
                jnp.sum(dpn * pg, axis=0, keepdims=True), (8, GW))
            dyh = [jnp.where(low == (k % 2 == 0), dy[:, (k // 2) * BLK:(k // 2 + 1) * BLK], 0.0) for k in range(4)]
            dms = [_dot_nt(dyh[k], xdt[:, g * GW + (k // 2) * BLK:g * GW + (k // 2 + 1) * BLK]) for k in range(4)]
            accs = [_dot_tn(lm[k][1], dyh[k]) for k in range(4)]
            dcb = jnp.zeros((BLK, BLK), F32)
            for k in range(4):
                h = 4 * g + k
                lmat, mmat = lm[k]
                dm = jnp.where(tril, dms[k], 0.0)
                nh = dm * mmat
                dcs = dcs + jnp.where(lane == h, jnp.sum(nh, axis=1, keepdims=True), 0.0)
                dcst = dcst - jnp.where(rowi == h, jnp.sum(nh, axis=0, keepdims=True), 0.0)
                dcb = dcb + dm * lmat
            dxdt = dxdt + jnp.concatenate([accs[0] + accs[1], accs[2] + accs[3]], axis=1)
            dcg = dcg + _dot(dcb, bg)
            dbg = dbg + _dot_tn(dcb, cg)
            tr_ref[2 * BLK:3 * BLK, gs] = dxdt * xg
            dxbc_ref[:, gs] = dx + dxdt * dtp_c[:, gs]
            dxbc_ref[:, D_SSM + g * BLK:D_SSM + (g + 1) * BLK] = dbg
            dxbc_ref[:, D_SSM + 1024 + g * BLK:D_SSM + 1024 + (g + 1) * BLK] = dcg
        red = _dot(tr_ref[...], et_ref[...])
        r1, r2, r3 = red[0:BLK], red[BLK:2 * BLK], red[2 * BLK:3 * BLK]
        for g in range(SSD_GROUPS):
            dp_ref[g] = new_dp[g]
        dgn_ref[...] += jnp.concatenate(dgn_parts, axis=1)
        ddec = jnp.max(red[3 * BLK:EXP_ROWS], axis=0, keepdims=True)
        ddsk_ref[...] += jnp.max(red[EXP_ROWS:RED_ROWS], axis=0, keepdims=True)
        dcs = dcs + dcst.T - r1 + r2
        dcs_last = jnp.sum(r1, axis=0, keepdims=True) + ddec * dec
        dcs = dcs + jnp.where(rowi == BLK - 1, dcs_last, 0.0)
        dda = _cumsum_rev(dcs)
        ddtp = r3 + dda * av
        dalog_ref[...] += jnp.sum(dda * dtp, axis=0, keepdims=True) * av
        draw = jnp.where(valid, ddtp * _sigmoid(pre), 0.0)
        ddtb_ref[...] += jnp.sum(draw, axis=0, keepdims=True)
        draw_ref[...] = draw.astype(draw_ref.dtype)

    vec = lambda w_: pl.BlockSpec((1, w_), lambda n: (0, 0))
    rb = lambda w_: pl.BlockSpec((BLK, w_), lambda n: (nb - 1 - n, 0))
    e = _head_expand()
    res = _call(
        kern, (xbc, xbc, xbc, proj, proj, _pad128(dt_bias), _pad128(a_log),
               jnp.repeat(d_skip, HEAD_DIM).reshape(1, D_SSM), gate_norm.reshape(1, D_SSM), e, decay, e.T, st, dyn),
        name=name, grid=(nb,),
        in_specs=_ssd_specs(t, z_off, dt_off, True)
        + [pl.BlockSpec((D_SSM, BLK), lambda n: (0, 0)),
           pl.BlockSpec((1, SSD_GROUPS, BLK, GW), lambda n: (nb - 1 - n, 0, 0, 0)),
           rb(D_SSM)],
        out_specs=[rb(2 * D_SSM), rb(D_SSM), rb(BLK), vec(D_SSM), vec(BLK), vec(BLK), vec(BLK)],
        out_shape=[jax.ShapeDtypeStruct((t, 2 * D_SSM), F32), jax.ShapeDtypeStruct((t, D_SSM), MXU),
                   jax.ShapeDtypeStruct((t, BLK), MXU), jax.ShapeDtypeStruct((1, D_SSM), F32),
                   jax.ShapeDtypeStruct((1, BLK), F32), jax.ShapeDtypeStruct((1, BLK), F32),
                   jax.ShapeDtypeStruct((1, BLK), F32)],
        scratch_shapes=[pltpu.VMEM((SSD_GROUPS, BLK, GW), F32), pltpu.VMEM((RED_ROWS, D_SSM), F32)],
        sem=("arbitrary",), carry=carry)
    dxbc, dz, draw, dgn, ddsk, ddtb, dalog = res[:7]
    return [dxbc, dz, draw, dgn[0], ddsk[0, :SSD_HEADS], ddtb[0, :SSD_HEADS], dalog[0, :SSD_HEADS]] + res[7:]


def loss_fwd_bwd(h, target, *, name):
    t, d = h.shape
    nb = t // BLK

    def kern(h_ref, t_ref, loss_ref, dh_ref):
        n = pl.program_id(0)
        err = jnp.where(n > 0, h_ref[...] - t_ref[...], 0.0)
        dh_ref[...] = err * (1.0 / d)
        part = (0.5 / d) * jnp.sum(jnp.sum(err * err, axis=1, keepdims=True), axis=0, keepdims=True)

        @pl.when(n == 0)
        def _():
            loss_ref[...] = part

        @pl.when(n > 0)
        def _():
            loss_ref[...] += part

    return pl.pallas_call(
        kern, name=name, grid=(nb,),
        in_specs=[pl.BlockSpec((BLK, d), lambda n: (n, 0)),
                  pl.BlockSpec((BLK, d), lambda n: (jnp.maximum(n - 1, 0), 0))],
        out_specs=[pl.BlockSpec((1, 1), lambda n: (0, 0)), pl.BlockSpec((BLK, d), lambda n: (n, 0))],
        out_shape=[jax.ShapeDtypeStruct((1, 1), F32), jax.ShapeDtypeStruct((t, d), F32)],
        compiler_params=_cp("arbitrary"),
    )(h, target)


def _ew_tile(r, c):
    cap = max(16, (256 * 1024) // c)
    best = None
    for tr in range(16, min(r, cap) + 1, 16):
        if r % tr == 0:
            best = tr
    return best if best is not None else r


def adamw(parts, w, m, v, *, name):
    npart, r, c = parts.shape
    tr = _ew_tile(r, c)

    def kern(p_ref, w_ref, m_ref, v_ref, g_ref, d_ref, m2_ref, v2_ref):
        g = p_ref[0].astype(F32)
        for k in range(1, npart):
            g = g + p_ref[k].astype(F32)
        m2 = ADAM_B1 * m_ref[...] + (1.0 - ADAM_B1) * g
        v2 = ADAM_B2 * v_ref[...] + (1.0 - ADAM_B2) * (g * g)
        m_hat = m2 / (1.0 - ADAM_B1 ** ADAM_STEP)
        v_hat = v2 / (1.0 - ADAM_B2 ** ADAM_STEP)
        g_ref[...] = g
        d_ref[...] = -ADAM_LR * (m_hat / (jnp.sqrt(v_hat) + ADAM_EPS) + ADAM_WD * w_ref[...])
        m2_ref[...] = m2
        v2_ref[...] = v2

    row = pl.BlockSpec((tr, c), lambda i: (i, 0))
    sds = jax.ShapeDtypeStruct((r, c), F32)
    return pl.pallas_call(
        kern, name=name, grid=(r // tr,),
        in_specs=[pl.BlockSpec((npart, tr, c), lambda i: (0, i, 0)), row, row, row],
        out_specs=[row, row, row, row], out_shape=[sds, sds, sds, sds],
        compiler_params=_cp("parallel"),
    )(parts, w, m, v)


def pair_add(p, land, *, name):
    _, r, c = p.shape
    tr = _ew_tile(r, c)
    core = lax.axis_index("c").astype(jnp.int32).reshape(1)

    def kern(c_ref, p_ref, l_ref, o_ref):
        o_ref[...] = (p_ref[...] + l_ref[...]).astype(o_ref.dtype)

    return pl.pallas_call(
        kern, name=name,
        grid_spec=pltpu.PrefetchScalarGridSpec(
            num_scalar_prefetch=1, grid=(4, r // tr),
            in_specs=[pl.BlockSpec((1, tr, c), lambda k, i, c_ref: (2 * k + c_ref[0], i, 0)),
                      pl.BlockSpec((1, tr, c), lambda k, i, c_ref: (k, i, 0))],
            out_specs=pl.BlockSpec((1, tr, c), lambda k, i, c_ref: (k, i, 0))),
        out_shape=jax.ShapeDtypeStruct((4, r, c), BF16),
        compiler_params=_cp("parallel", "parallel"),
    )(core, p, land)


def _me():
    return lax.axis_index("x"), lax.axis_index("y"), lax.axis_index("c")


def all_gather(xs, *, name):
    n = len(xs)

    def body(*refs):
        x_refs, out_refs = refs[:n], refs[n:2 * n]
        send_sems, recv_sems, local_sems = refs[2 * n:]
        mx, my, mc = _me()
        me, sib = (mx, my, mc), (mx, my, 1 - mc)
        chips = [(1 - mx, my), (mx, 1 - my), (1 - mx, 1 - my)]

        def rows(i, px, py, pc):
            return out_refs[i].at[4 * px + 2 * py + pc]

        def copy(i, k, block, to, src=None):
            return pltpu.make_async_remote_copy(
                src_ref=rows(i, *block) if src is None else src, dst_ref=rows(i, *block),
                send_sem=send_sems.at[7 * i + k], recv_sem=recv_sems.at[7 * i + k],
                device_id=to, device_id_type=MESH)

        mine = [pltpu.make_async_copy(x_refs[i], rows(i, *me), local_sems.at[i]) for i in range(n)]
        first = []
        for i in range(n):
            mine[i].start()
            first.append(copy(i, 0, me, sib, src=x_refs[i]))
            first += [copy(i, 1 + j, me, (*chip, mc), src=x_refs[i]) for j, chip in enumerate(chips)]
        for cp in first:
            cp.start()
        passed = []
        for i in range(n):
            for j, chip in enumerate(chips):
                copy(i, 1 + j, (*chip, mc), me).wait_recv()
                passed.append(copy(i, 4 + j, (*chip, mc), sib))
                passed[-1].start()
        for i in range(n):
            copy(i, 0, sib, me).wait_recv()
            for j, chip in enumerate(chips):
                copy(i, 4 + j, (*chip, 1 - mc), me).wait_recv()
        for cp in first + passed:
            cp.wait_send()
        for cp in mine:
            cp.wait()

    return pl.pallas_call(
        body, name=name,
        out_shape=[jax.ShapeDtypeStruct((N_DEV,) + x.shape, x.dtype) for x in xs],
        in_specs=[ANY] * n, out_specs=[ANY] * n,
        scratch_shapes=[pltpu.SemaphoreType.DMA((7 * n,)), pltpu.SemaphoreType.DMA((7 * n,)),
                        pltpu.SemaphoreType.DMA((n,))],
    )(*xs)


def pair_exchange(ps, *, name):
    n = len(ps)

    def body(*refs):
        p_refs, out_refs = refs[:n], refs[n:2 * n]
        send_sems, recv_sems = refs[2 * n:]
        mx, my, mc = _me()
        cps = [pltpu.make_async_remote_copy(
            src_ref=p_refs[i].at[2 * k + (1 - mc)], dst_ref=out_refs[i].at[k],
            send_sem=send_sems.at[4 * i + k], recv_sem=recv_sems.at[4 * i + k],
            device_id=(mx, my, 1 - mc), device_id_type=MESH) for i in range(n) for k in range(4)]
        for cp in cps:
            cp.start()
        for cp in cps:
            cp.wait_recv()
        for cp in cps:
            cp.wait_send()

    return pl.pallas_call(
        body, name=name,
        out_shape=[jax.ShapeDtypeStruct((4,) + p.shape[1:], p.dtype) for p in ps],
        in_specs=[ANY] * n, out_specs=[ANY] * n,
        scratch_shapes=[pltpu.SemaphoreType.DMA((4 * n,)), pltpu.SemaphoreType.DMA((4 * n,))],
    )(*ps)


def chip_exchange(qs, *, name):
    n = len(qs)

    def body(*refs):
        q_refs, out_refs = refs[:n], refs[n:2 * n]
        send_sems, recv_sems, local_sems = refs[2 * n:]
        mx, my, mc = _me()
        mine = 2 * mx + my
        chips = [(1 - mx, my), (mx, 1 - my), (1 - mx, 1 - my)]
        local, sends, recvs = [], [], []
        for i in range(n):
            local.append(pltpu.make_async_copy(q_refs[i].at[mine], out_refs[i].at[mine], local_sems.at[i]))
            for k, (px, py) in enumerate(chips):
                sems = dict(send_sem=send_sems.at[3 * i + k], recv_sem=recv_sems.at[3 * i + k],
                            device_id=(px, py, mc), device_id_type=MESH)
                sends.append(pltpu.make_async_remote_copy(
                    src_ref=q_refs[i].at[2 * px + py], dst_ref=out_refs[i].at[mine], **sems))
                recvs.append(pltpu.make_async_remote_copy(
                    src_ref=q_refs[i].at[mine], dst_ref=out_refs[i].at[2 * px + py], **sems))
        for cp in local + sends:
            cp.start()
        for cp in recvs:
            cp.wait_recv()
        for cp in sends:
            cp.wait_send()
        for cp in local:
            cp.wait()

    return pl.pallas_call(
        body, name=name,
        out_shape=[jax.ShapeDtypeStruct(q.shape, q.dtype) for q in qs],
        in_specs=[ANY] * n, out_specs=[ANY] * n,
        scratch_shapes=[pltpu.SemaphoreType.DMA((3 * n,)), pltpu.SemaphoreType.DMA((3 * n,)),
                        pltpu.SemaphoreType.DMA((n,))],
    )(*qs)


class _Carry:
    def __init__(self, inputs, out_shapes, sems, start, finish):
        self.inputs, self.out_shapes, self.sems = list(inputs), list(out_shapes), list(sems)
        self.start, self.finish = start, finish


def _call(kern, args, *, name, grid, in_specs, out_specs, out_shape, scratch_shapes=(), sem, carry=None):
    in_specs, out_specs, out_shape = list(in_specs), list(out_specs), list(out_shape)
    scratch_shapes = list(scratch_shapes)
    if carry is None:
        return list(pl.pallas_call(
            kern, name=name, grid=grid, in_specs=in_specs, out_specs=out_specs, out_shape=out_shape,
            scratch_shapes=scratch_shapes, compiler_params=_cp(*sem))(*args))
    ni, no, ns = len(in_specs), len(out_specs), len(scratch_shapes)
    ci, co = len(carry.inputs), len(carry.out_shapes)

    def body(*refs):
        o0 = ni + ci
        s0 = o0 + no + co
        ids = [pl.program_id(d) for d in range(len(grid))]
        first = functools.reduce(jnp.logical_and, [i == 0 for i in ids])
        last = functools.reduce(jnp.logical_and, [i == g - 1 for i, g in zip(ids, grid)])
        cin, cout, sems = refs[ni:o0], refs[o0 + no:s0], refs[s0 + ns:]

        @pl.when(first)
        def _():
            carry.start(cin, cout, sems)

        kern(*refs[:ni], *refs[o0:o0 + no], *refs[s0:s0 + ns])

        @pl.when(last)
        def _():
            carry.finish(cin, cout, sems)

    return list(pl.pallas_call(
        body, name=name, grid=grid, in_specs=in_specs + [ANY] * ci, out_specs=out_specs + [ANY] * co,
        out_shape=out_shape + carry.out_shapes, scratch_shapes=scratch_shapes + carry.sems,
        compiler_params=_cp(*(["arbitrary"] * len(grid))))(*args, *carry.inputs))


def merge_carries(cs):
    def split(seq, counts):
        out, off = [], 0
        for k in counts:
            out.append(seq[off:off + k])
            off += k
        return out

    def parts(cin, cout, sems):
        return zip(cs, split(cin, [len(c.inputs) for c in cs]), split(cout, [len(c.out_shapes) for c in cs]),
                   split(sems, [len(c.sems) for c in cs]))

    def start(cin, cout, sems):
        for c, i, o, s in parts(cin, cout, sems):
            c.start(i, o, s)

    def finish(cin, cout, sems):
        for c, i, o, s in parts(cin, cout, sems):
            c.finish(i, o, s)

    return _Carry(sum((c.inputs for c in cs), []), sum((c.out_shapes for c in cs), []),
                  sum((c.sems for c in cs), []), start, finish)


def gather_carry(xs):
    n = len(xs)

    def copies(cin, cout, sems, with_recv=True):
        mx, my, mc = _me()
        me = 4 * mx + 2 * my + mc
        peers = [(mx, my, 1 - mc), (1 - mx, my, mc), (mx, 1 - my, mc), (1 - mx, 1 - my, mc)]
        local, send, recv = [], [], []
        for i in range(n):
            local.append(pltpu.make_async_copy(cin[i], cout[i].at[me], sems[2].at[i]))
            for k, peer in enumerate(peers):
                common = dict(send_sem=sems[0].at[4 * i + k], recv_sem=sems[1].at[4 * i + k],
                              device_id=peer, device_id_type=MESH)
                send.append(pltpu.make_async_remote_copy(src_ref=cin[i], dst_ref=cout[i].at[me], **common))
                if with_recv:
                    recv.append(pltpu.make_async_remote_copy(
                        src_ref=cin[i], dst_ref=cout[i].at[4 * peer[0] + 2 * peer[1] + peer[2]], **common))
        return local, send, recv

    def start(cin, cout, sems):
        local, send, _ = copies(cin, cout, sems, with_recv=False)
        for cp in local + send:
            cp.start()

    def finish(cin, cout, sems):
        local, send, recv = copies(cin, cout, sems)
        for cp in recv:
            cp.wait_recv()
        for cp in send:
            cp.wait_send()
        for cp in local:
            cp.wait()

    return _Carry(xs, [jax.ShapeDtypeStruct((N_DEV,) + x.shape, x.dtype) for x in xs],
                  [pltpu.SemaphoreType.DMA((4 * n,)), pltpu.SemaphoreType.DMA((4 * n,)),
                   pltpu.SemaphoreType.DMA((n,))], start, finish)


def gather_relay(outs, *, name):
    n = len(outs)

    def body(*refs):
        bufs = refs[n:2 * n]
        send_sems, recv_sems = refs[2 * n:]
        mx, my, mc = _me()
        chips = [(1 - mx, my), (mx, 1 - my), (1 - mx, 1 - my)]
        send, recv = [], []
        for i in range(n):
            for j, (px, py) in enumerate(chips):
                common = dict(send_sem=send_sems.at[3 * i + j], recv_sem=recv_sems.at[3 * i + j],
                              device_id=(mx, my, 1 - mc), device_id_type=MESH)
                mine = bufs[i].at[4 * px + 2 * py + mc]
                send.append(pltpu.make_async_remote_copy(src_ref=mine, dst_ref=mine, **common))
                recv.append(pltpu.make_async_remote_copy(
                    src_ref=mine, dst_ref=bufs[i].at[4 * px + 2 * py + (1 - mc)], **common))
        for cp in send:
            cp.start()
        for cp in recv:
            cp.wait_recv()
        for cp in send:
            cp.wait_send()

    return pl.pallas_call(
        body, name=name, out_shape=[jax.ShapeDtypeStruct(o.shape, o.dtype) for o in outs],
        in_specs=[ANY] * n, out_specs=[ANY] * n, input_output_aliases={i: i for i in range(n)},
        scratch_shapes=[pltpu.SemaphoreType.DMA((3 * n,)), pltpu.SemaphoreType.DMA((3 * n,))],
    )(*outs)


def pair_carry(ps):
    n = len(ps)

    def copies(cin, cout, sems):
        mx, my, mc = _me()
        return [pltpu.make_async_remote_copy(
            src_ref=cin[i].at[2 * k + (1 - mc)], dst_ref=cout[i].at[k],
            send_sem=sems[0].at[4 * i + k], recv_sem=sems[1].at[4 * i + k],
            device_id=(mx, my, 1 - mc), device_id_type=MESH) for i in range(n) for k in range(4)]

    def start(cin, cout, sems):
        for cp in copies(cin, cout, sems):
            cp.start()

    def finish(cin, cout, sems):
        cps = copies(cin, cout, sems)
        for cp in cps:
            cp.wait_recv()
        for cp in cps:
            cp.wait_send()

    return _Carry(ps, [jax.ShapeDtypeStruct((4,) + p.shape[1:], p.dtype) for p in ps],
                  [pltpu.SemaphoreType.DMA((4 * n,)), pltpu.SemaphoreType.DMA((4 * n,))], start, finish)


def chip_carry(qs):
    n = len(qs)

    def copies(cin, cout, sems, with_recv=True):
        mx, my, mc = _me()
        mine = 2 * mx + my
        chips = [(1 - mx, my), (mx, 1 - my), (1 - mx, 1 - my)]
        local, send, recv = [], [], []
        for i in range(n):
            local.append(pltpu.make_async_copy(cin[i].at[mine], cout[i].at[mine], sems[2].at[i]))
            for k, (px, py) in enumerate(chips):
                common = dict(send_sem=sems[0].at[3 * i + k], recv_sem=sems[1].at[3 * i + k],
                              device_id=(px, py, mc), device_id_type=MESH)
                send.append(pltpu.make_async_remote_copy(
                    src_ref=cin[i].at[2 * px + py], dst_ref=cout[i].at[mine], **common))
                if with_recv:
                    recv.append(pltpu.make_async_remote_copy(
                        src_ref=cin[i].at[mine], dst_ref=cout[i].at[2 * px + py], **common))
        return local, send, recv

    def start(cin, cout, sems):
        local, send, _ = copies(cin, cout, sems, with_recv=False)
        for cp in local + send:
            cp.start()

    def finish(cin, cout, sems):
        local, send, recv = copies(cin, cout, sems)
        for cp in recv:
            cp.wait_recv()
        for cp in send:
            cp.wait_send()
        for cp in local:
            cp.wait()

    return _Carry(qs, [jax.ShapeDtypeStruct(q.shape, q.dtype) for q in qs],
                  [pltpu.SemaphoreType.DMA((3 * n,)), pltpu.SemaphoreType.DMA((3 * n,)),
                   pltpu.SemaphoreType.DMA((n,))], start, finish)


WEIGHTS = [
    "meta_tokens", "l0_mix_pre_norm", "l0_mix_post_norm", "l0_w_in", "l0_lru_conv_w", "l0_lru_conv_b",
    "l0_lru_w_a", "l0_lru_b_a", "l0_lru_w_x", "l0_lru_b_x", "l0_lru_lambda", "l0_attn_sinks", "l0_w_out",
    "l0_ffn_pre_norm", "l0_ffn_post_norm", "l0_ffn_w_up", "l0_ffn_conv_w", "l0_ffn_conv_b", "l0_ffn_w_down",
    "l1_mix_pre_norm", "l1_mix_post_norm", "l1_w_in", "l1_ssm_conv_w", "l1_ssm_conv_b", "l1_dt_bias",
    "l1_a_log", "l1_d_skip", "l1_gate_norm", "l1_w_out", "l1_ffn_pre_norm", "l1_ffn_post_norm",
    "l1_ffn_w_up", "l1_ffn_conv_w", "l1_ffn_conv_b", "l1_ffn_w_down",
]
INPUTS = (["x"] + WEIGHTS + ["loss_target"] + ["m_" + n for n in WEIGHTS] + ["v_" + n for n in WEIGHTS])

MATS = {"l0_w_in": ("col", (1024, 3328)), "l0_w_out": ("row", (2048, 1024)),
        "l0_ffn_w_up": ("col", (1024, 5632)), "l0_ffn_w_down": ("row", (2816, 1024)),
        "l1_w_in": ("col", (1024, 6176)), "l1_w_out": ("row", (2048, 1024)),
        "l1_ffn_w_up": ("col", (1024, 5632)), "l1_ffn_w_down": ("row", (2816, 1024))}
SMALL_SHARDED = {"meta_tokens": ("col", (16, 1024)), "l0_lru_conv_w": ("col", (4, 1024)),
                 "l0_ffn_conv_w": ("col", (3, 5632)), "l1_ssm_conv_w": ("col", (4, 4096)),
                 "l1_ffn_conv_w": ("col", (3, 5632))}
SHARDED = {**MATS, **SMALL_SHARDED}
REPLICATED = [n for n in WEIGHTS if n not in SHARDED]
SHAPES = {n: ((8, BLK, BLK) if n.endswith(("lru_w_a", "lru_w_x")) else (N_Q_HEADS,) if n.endswith("attn_sinks")
              else (2 * D_FF,) if n.endswith("ffn_conv_b") else (2 * D_SSM,) if n.endswith("ssm_conv_b")
              else (SSD_HEADS,) if n.endswith(("dt_bias", "a_log", "d_skip")) else (D_SSM,) if n.endswith("gate_norm")
              else (D_MODEL,)) for n in REPLICATED}
PACK_W = 1024
SMALL_W = 128


def _shard_shape(name):
    kind, (r, c) = SHARDED[name]
    return (r, c // N_DEV) if kind == "col" else (r // N_DEV, c)


def _rows_of(numel, width):
    return -(-numel // width)


def _to_rows(a, width):
    flat = a.reshape(-1)
    rows = _rows_of(flat.shape[0], width)
    return jnp.pad(flat, (0, rows * width - flat.shape[0])).reshape(rows, width)


def _pack(arrs, width, total_rows):
    slab = jnp.concatenate([_to_rows(a, width) for a in arrs], axis=0)
    return jnp.pad(slab, ((0, total_rows - slab.shape[0]), (0, 0)))


def _unpack(slab, shapes, width):
    out, off = [], 0
    for shp in shapes:
        numel = math.prod(shp)
        rows = _rows_of(numel, width)
        out.append(slab[off:off + rows].reshape(-1)[:numel].reshape(shp))
        off += rows
    return out


def _round_up(n, m):
    return -(-n // m) * m


def _by_dest(name, g):
    kind, (r, c) = SHARDED[name]
    if kind == "col":
        return g.reshape(r, N_DEV, c // N_DEV).transpose(1, 0, 2)
    return g.reshape(N_DEV, r // N_DEV, c)


def _from_shards(name, blocks):
    kind, (r, c) = SHARDED[name]
    return blocks.transpose(1, 0, 2).reshape(r, c) if kind == "col" else blocks.reshape(r, c)


L1_IN_PAD = 6272
FFN_CBLK = 1408
SSM_CBLK = 1024


def _ffn_fwd(h, a, w, pfx):
    u, ut = rmsnorm_fwd(h, a[pfx + "ffn_pre_norm"], out_dtype=MXU, name=pfx + "ffn_pre", with_t=True)
    up = matmul(u, w[pfx + "ffn_w_up"], name=pfx + "ffn_up")
    act, act_t = dwconv_fwd(up, a[pfx + "ffn_conv_w"], a[pfx + "ffn_conv_b"], mode="geglu", x_off=0,
                            c_out=D_FF, cblk=FFN_CBLK, out_dtype=MXU, name=pfx + "ffn_act", with_t=True)
    down = matmul(act, w[pfx + "ffn_w_down"], name=pfx + "ffn_down")
    out = rmsnorm_fwd(down, a[pfx + "ffn_post_norm"], res=h, out_dtype=F32, name=pfx + "ffn_post")
    return out, (h, ut, up, act_t, down)


def _dx_and_pair_stage(names, g, a_list, b, *, name):
    parts = [_by_dest(n, g[n]) for n in names]
    out, from_sibling = matmul_cat(a_list, b, trans_b=True, name=name, carry=pair_carry(parts))
    return out, [pair_add(p, l, name="rs_pair_add_" + n) for n, p, l in zip(names, parts, from_sibling)]


def _ffn_bwd(dh, saved, a, w, pfx, g, carry=None):
    h, ut, up, act_t, down = saved
    dd, g[pfx + "ffn_post_norm"] = rmsnorm_bwd(down, a[pfx + "ffn_post_norm"], dh, out_dtype=MXU,
                                               name=pfx + "ffn_post_bwd")
    dact = matmul(dd, w[pfx + "ffn_w_down"], trans_b=True, name=pfx + "ffn_down_dx")
    g[pfx + "ffn_w_down"] = matmul(act_t, dd, name=pfx + "ffn_down_dw")
    dups, g[pfx + "ffn_conv_w"], g[pfx + "ffn_conv_b"], carried = dwconv_bwd(
        up, a[pfx + "ffn_conv_w"], a[pfx + "ffn_conv_b"], dact, mode="geglu", x_off=0, c_out=D_FF,
        cblk=FFN_CBLK, name=pfx + "ffn_act_bwd", carry=carry)
    g[pfx + "ffn_w_up"] = jnp.concatenate(
        [matmul(ut, d, name=pfx + "ffn_up_dw%d" % i) for i, d in enumerate(dups)], axis=1)
    du, q = _dx_and_pair_stage([pfx + "ffn_w_down", pfx + "ffn_w_up"], g, dups, w[pfx + "ffn_w_up"],
                               name=pfx + "ffn_up_dx")
    dh_in, g[pfx + "ffn_pre_norm"] = rmsnorm_bwd(h, a[pfx + "ffn_pre_norm"], du, res=dh, out_dtype=F32,
                                                 name=pfx + "ffn_pre_bwd")
    return dh_in, carried, q


GATHER_EARLY = ["l0_w_out", "l0_ffn_w_up", "l0_ffn_w_down"]
GATHER_LATE = ["l1_w_in", "l1_w_out", "l1_ffn_w_up", "l1_ffn_w_down"]
RS_L1_FFN = ["l1_ffn_w_down", "l1_ffn_w_up"]
RS_L1_MIX = ["l1_w_out", "l1_w_in"]
RS_L0_FFN = ["l0_ffn_w_down", "l0_ffn_w_up"]
RS_LAST = ["l0_w_in", "l0_lru_conv_w", "l0_ffn_conv_w", "l1_ssm_conv_w", "l1_ffn_conv_w"]


REPL_LATE = ["l0_attn_sinks", "l0_mix_pre_norm"]
REPL_EARLY = [n for n in REPLICATED if n not in REPL_LATE]


def _local_step(a, shards):
    x = a["x"][0]
    seq = x.shape[0]
    h0 = jnp.concatenate([jnp.zeros((PAD, D_MODEL), F32), a["meta_tokens"], x], axis=0)
    g, landed = {}, {}

    u0, u0t, w_in0 = rmsnorm_fwd(h0, a["l0_mix_pre_norm"], out_dtype=MXU, name="l0_mix_pre", with_t=True,
                                 carry=gather_carry([shards["l0_w_in"]]))
    w = {"l0_w_in": _from_shards("l0_w_in", gather_relay([w_in0], name="gather_relay_first")[0])}
    proj0 = matmul(u0, w["l0_w_in"], name="l0_in")
    lru = (a["l0_lru_conv_w"], a["l0_lru_conv_b"], a["l0_lru_w_a"], a["l0_lru_b_a"], a["l0_lru_w_x"],
           a["l0_lru_b_x"], a["l0_lru_lambda"])
    ya, ya_t, hl, *early = lru_fwd(proj0, *lru, gate_off=0, xr_off=1024, name="l0_lru",
                                   carry=gather_carry([shards[n] for n in GATHER_EARLY]))
    yb, *late = attn_fwd(proj0, a["l0_attn_sinks"], q_off=2048, k_off=3072, v_off=3200, name="l0_attn",
                         carry=gather_carry([shards[n] for n in GATHER_LATE]))
    relayed = gather_relay(early + late, name="gather_relay")
    w = dict(w, **{n: _from_shards(n, blocks) for n, blocks in zip(GATHER_EARLY + GATHER_LATE, relayed)})
    w["l1_w_in"] = jnp.pad(w["l1_w_in"], ((0, 0), (0, L1_IN_PAD - w["l1_w_in"].shape[1])))
    o0 = matmul_cat([ya, yb], w["l0_w_out"], name="l0_out")
    h1 = rmsnorm_fwd(o0, a["l0_mix_post_norm"], res=h0, out_dtype=F32, name="l0_mix_post")
    h2, ffn0 = _ffn_fwd(h1, a, w, "l0_")

    u2, u2t = rmsnorm_fwd(h2, a["l1_mix_pre_norm"], out_dtype=MXU, name="l1_mix_pre", with_t=True)
    proj1 = matmul(u2, w["l1_w_in"], name="l1_in")
    xbc = dwconv_fwd(proj1, a["l1_ssm_conv_w"], a["l1_ssm_conv_b"], mode="silu", x_off=D_SSM,
                     c_out=2 * D_SSM, cblk=SSM_CBLK, out_dtype=F32, name="l1_ssm_conv")
    ssd = (a["l1_dt_bias"], a["l1_a_log"], a["l1_d_skip"], a["l1_gate_norm"])
    decay = ssd_decay(proj1, a["l1_dt_bias"], a["l1_a_log"], dt_off=3 * D_SSM, name="l1_ssd_decay")
    yn, yn_t, st = ssd_fwd(xbc, proj1, decay, *ssd, z_off=0, dt_off=3 * D_SSM, name="l1_ssd")
    o1 = matmul(yn, w["l1_w_out"], name="l1_out")
    h3 = rmsnorm_fwd(o1, a["l1_mix_post_norm"], res=h2, out_dtype=F32, name="l1_mix_post")
    h4, ffn1 = _ffn_fwd(h3, a, w, "l1_")

    loss, dh4 = loss_fwd_bwd(h4, a["loss_target"][0], name="loss")

    dh3, _, q_l1_ffn = _ffn_bwd(dh4, ffn1, a, w, "l1_", g)
    do1, g["l1_mix_post_norm"] = rmsnorm_bwd(o1, a["l1_mix_post_norm"], dh3, out_dtype=MXU,
                                             name="l1_mix_post_bwd")
    dyn = matmul(do1, w["l1_w_out"], trans_b=True, name="l1_out_dx")
    g["l1_w_out"] = matmul(yn_t, do1, name="l1_out_dw")
    (dxbc, dz, draw, g["l1_gate_norm"], g["l1_d_skip"], g["l1_dt_bias"], g["l1_a_log"], *got) = ssd_bwd(
        xbc, proj1, decay, st, dyn, *ssd, z_off=0, dt_off=3 * D_SSM, name="l1_ssd_bwd",
        carry=chip_carry(q_l1_ffn))
    landed.update(zip(RS_L1_FFN, got))
    (dxin,), g["l1_ssm_conv_w"], g["l1_ssm_conv_b"], _ = dwconv_bwd(
        proj1, a["l1_ssm_conv_w"], a["l1_ssm_conv_b"], dxbc, mode="silu", x_off=D_SSM,
        c_out=2 * D_SSM, cblk=SSM_CBLK, name="l1_ssm_conv_bwd")
    g["l1_w_in"] = jnp.concatenate(
        [matmul(u2t, dz, name="l1_in_dw_z"), matmul(u2t, dxin, name="l1_in_dw_x"),
         matmul(u2t, draw, name="l1_in_dw_dt")[:, :SSD_HEADS]], axis=1)
    du2, q_l1_mix = _dx_and_pair_stage(RS_L1_MIX, g, [dz, dxin, draw], w["l1_w_in"], name="l1_in_dx")
    dh2, g["l1_mix_pre_norm"] = rmsnorm_bwd(h2, a["l1_mix_pre_norm"], du2, res=dh3, out_dtype=F32,
                                            name="l1_mix_pre_bwd")

    dh1, got, q_l0_ffn = _ffn_bwd(dh2, ffn0, a, w, "l0_", g, carry=chip_carry(q_l1_mix))
    landed.update(zip(RS_L1_MIX, got))
    do0, g["l0_mix_post_norm"] = rmsnorm_bwd(o0, a["l0_mix_post_norm"], dh1, out_dtype=MXU,
                                             name="l0_mix_post_bwd")
    g["l0_w_out"] = jnp.concatenate([matmul(ya_t, do0, name="l0_out_dw_a"),
                                     matmul(yb.T, do0, name="l0_out_dw_b")], axis=0)
    dy, q_out = _dx_and_pair_stage(["l0_w_out"], g, [do0], w["l0_w_out"], name="l0_out_dx")
    (dgate, dxr, g["l0_lru_conv_w"], dcb, g["l0_lru_w_a"], dba, g["l0_lru_w_x"], dbx, dlam, *got) = lru_bwd(
        proj0, hl, dy, *lru, gate_off=0, xr_off=1024, dy_off=0, name="l0_lru_bwd", carry=chip_carry(q_out))
    landed["l0_w_out"] = got[0]
    g["l0_lru_conv_b"], g["l0_lru_b_a"], g["l0_lru_b_x"], g["l0_lru_lambda"] = dcb[0], dba[0], dbx[0], dlam[0]
    dq, dk, dv, g["l0_attn_sinks"], *got = attn_bwd(
        proj0, a["l0_attn_sinks"], dy, q_off=2048, k_off=3072, v_off=3200, dy_off=1024, name="l0_attn_bwd",
        carry=merge_carries([chip_carry(q_l0_ffn), gather_carry([_pack_repl(g, REPL_EARLY)])]))
    landed.update(zip(RS_L0_FFN, got[:2]))
    repl_early = gather_relay(got[2:], name="gather_relay_small_grads")[0]
    dproj0 = [dgate, dxr, dq, dk, dv]
    g["l0_w_in"] = jnp.concatenate(
        [matmul(u0t, d, name="l0_in_dw%d" % i) for i, d in enumerate(dproj0)], axis=1)
    du0, q_last = _dx_and_pair_stage(RS_LAST, g, dproj0, w["l0_w_in"], name="l0_in_dx")
    dh0, g["l0_mix_pre_norm"], *got = rmsnorm_bwd(h0, a["l0_mix_pre_norm"], du0, res=dh1, out_dtype=F32,
                                                  name="l0_mix_pre_bwd", carry=chip_carry(q_last))
    landed.update(zip(RS_LAST, got))
    g["meta_tokens"] = dh0[PAD:BLK]
    meta = _by_dest("meta_tokens", g["meta_tokens"])
    q_meta = pair_add(meta, pair_exchange([meta], name="rs_pair_meta")[0], name="rs_pair_add_meta_tokens")
    landed["meta_tokens"] = chip_exchange([q_meta], name="rs_chip_meta")[0]
    return loss[0, 0], dh0[BLK:].reshape(1, seq, D_MODEL), g, landed, repl_early


def _repl_rows(names):
    return _round_up(sum(_rows_of(math.prod(SHAPES[n]), SMALL_W) for n in names), 16)


def _pack_repl(vals, names):
    return _pack([vals[n] for n in names], SMALL_W, _repl_rows(names))


def kernel(*args):
    a = dict(zip(INPUTS, args))
    first = list(SMALL_SHARDED)
    full = {n: _from_shards(n, blocks) for n, blocks in
            zip(first, all_gather([a[n] for n in first], name="gather_first"))}
    shards = {n: a[n].astype(MXU) for n in MATS}
    loss_part, grad_x, g, landed, repl_early = _local_step({**a, **full}, shards)
    loss = lax.psum(loss_part, ("x", "y", "c"))

    sh_out = {n: adamw(landed[n], a[n], a["m_" + n], a["v_" + n], name="adamw_" + n) for n in SHARDED}

    repl_late = all_gather([_pack_repl(g, REPL_LATE)], name="gather_small_grads")[0]
    rp_out = [{}, {}, {}, {}]
    for names, parts in ((REPL_EARLY, repl_early), (REPL_LATE, repl_late)):
        res = adamw(parts, *[_pack_repl({n: a[p + n] for n in names}, names) for p in ("", "m_", "v_")],
                    name="adamw_replicated_%d" % len(names))
        for k in range(4):
            rp_out[k].update(zip(names, _unpack(res[k], [SHAPES[n] for n in names], SMALL_W)))

    outs = [loss, grad_x]
    for k in range(4):
        outs += [sh_out[n][k] if n in SHARDED else rp_out[k][n] for n in WEIGHTS]
    return tuple(outs)
```

```python
import functools
import math

import jax
import jax.numpy as jnp
import numpy as np
from jax import lax
from jax.experimental import pallas as pl
from jax.experimental.pallas import tpu as pltpu

F32 = jnp.float32
BF16 = jnp.bfloat16
MXU = jnp.bfloat16

D_MODEL = 1024
N_META = 16
BLK = 128
PAD = BLK - N_META
D_RNN = 1024
LRU_C = 8.0
N_Q_HEADS = 16
HEAD_DIM = 64
D_SSM = 2048
SSD_HEADS = 32
SSD_GROUPS = 8
D_FF = 2816
EPS = 1e-6
NEG = -1e30
N_DEV = 8

ADAM_LR = 0.001
ADAM_B1 = 0.9
ADAM_B2 = 0.999
ADAM_EPS = 1e-08
ADAM_WD = 0.01
ADAM_STEP = 10

VMEM_LIMIT = 56 * 1024 * 1024
MESH = pl.DeviceIdType.MESH
ANY = pl.BlockSpec(memory_space=pl.ANY)


def _cp(*sem):
    return pltpu.CompilerParams(dimension_semantics=sem, vmem_limit_bytes=VMEM_LIMIT)


def _pick(n, cands):
    for c in cands:
        if n % c == 0:
            return c
    return n


def _dot(a, b):
    return jnp.dot(a.astype(MXU), b.astype(MXU), preferred_element_type=F32)


def _dot_nt(a, b):
    return lax.dot_general(a.astype(MXU), b.astype(MXU), (((1,), (1,)), ((), ())),
                           preferred_element_type=F32)


def _dot_tn(a, b):
    return jnp.dot(a.T.astype(MXU), b.astype(MXU), preferred_element_type=F32)


def _sigmoid(x):
    return 1.0 / (1.0 + jnp.exp(-x))


def _log1p(x):
    u = 1.0 + x
    return jnp.where(u == 1.0, x, jnp.log(u) * (x / jnp.where(u == 1.0, 1.0, u - 1.0)))


def _expm1(x):
    u = jnp.exp(x)
    um1 = u - 1.0
    lg = jnp.log(jnp.where(u > 0.0, u, 1.0))
    safe = (um1 != 0.0) & (um1 != -1.0)
    return jnp.where(um1 == 0.0, x, jnp.where(um1 == -1.0, -1.0,
                                               um1 * (x / jnp.where(safe, lg, 1.0))))


def _softplus(x):
    return jnp.maximum(x, 0.0) + _log1p(jnp.exp(-jnp.abs(x)))


_GC = math.sqrt(2.0 / math.pi)


def _gelu(x):
    t = jnp.tanh(_GC * (x + 0.044715 * x * x * x))
    return 0.5 * x * (1.0 + t)


def _gelu_grad(x):
    t = jnp.tanh(_GC * (x + 0.044715 * x * x * x))
    return 0.5 * (1.0 + t) + 0.5 * x * (1.0 - t * t) * (_GC * (1.0 + 3.0 * 0.044715 * x * x))


def _silu(x):
    return x * _sigmoid(x)


def _silu_grad(x):
    s = _sigmoid(x)
    return s * (1.0 + x * (1.0 - s))


def _rows(shape):
    return lax.broadcasted_iota(jnp.int32, shape, 0)


def _lanes(shape):
    return lax.broadcasted_iota(jnp.int32, shape, 1)


def _shift_down(x, tail, d):
    if d == 0:
        return x
    n = x.shape[0]
    xr = pltpu.roll(x, d, 0)
    tr = pltpu.roll(tail, d, 0)
    first = jnp.where(_rows(tr.shape) < d, tr, xr[0:8])
    return jnp.concatenate([first, xr[8:n]], axis=0)


def _shift_up(x, head, d):
    if d == 0:
        return x
    n = x.shape[0]
    xr = pltpu.roll(x, n - d, 0)
    hr = pltpu.roll(head, 8 - d, 0)
    last = jnp.where(_rows(hr.shape) >= 8 - d, hr, xr[n - 8:n])
    return jnp.concatenate([xr[0:n - 8], last], axis=0)


def _keep(x, valid, s):
    return jnp.where(valid, x, 0.0) if s == 0 else x


def _row_at(x, i):
    return jnp.sum(jnp.where(_rows(x.shape) == i, x, 0.0), axis=0, keepdims=True)


def _scan_fwd(a, u):
    n = a.shape[0]
    ri = _rows(a.shape)
    d = 1
    while d < n:
        m = ri >= d
        us = jnp.where(m, pltpu.roll(u, d, 0), 0.0)
        as_ = jnp.where(m, pltpu.roll(a, d, 0), 1.0)
        u = u + a * us
        a = a * as_
        d *= 2
    return a, u


def _scan_rev(c, u):
    n = c.shape[0]
    ri = _rows(c.shape)
    d = 1
    while d < n:
        m = ri < n - d
        us = jnp.where(m, pltpu.roll(u, n - d, 0), 0.0)
        cs = jnp.where(m, pltpu.roll(c, n - d, 0), 1.0)
        u = u + c * us
        c = c * cs
        d *= 2
    return c, u


def _cumsum_fwd(x):
    n = x.shape[0]
    ri = _rows(x.shape)
    d = 1
    while d < n:
        x = x + jnp.where(ri >= d, pltpu.roll(x, d, 0), 0.0)
        d *= 2
    return x


def _cumsum_rev(x):
    n = x.shape[0]
    ri = _rows(x.shape)
    d = 1
    while d < n:
        x = x + jnp.where(ri < n - d, pltpu.roll(x, n - d, 0), 0.0)
        d *= 2
    return x


MATMUL_VMEM = 40 * 1024 * 1024


def _matmul_tiles(m, n, k, tk, out_bytes):
    best = None
    for tm in (1664, 1408, 1040, 1024, 832, 640, 512, 384, 256, 128):
        if m % tm:
            continue
        for tn in (2048, 1664, 1408, 1024, 896, 640, 512, 384, 256, 128):
            if n % tn:
                continue
            vmem = 2 * (tm * tk * 2 + tk * tn * 2 + tm * tn * out_bytes) + (tm * tn * 4 if k > tk else 0)
            if vmem > MATMUL_VMEM:
                continue
            traffic = (n // tn) * m * k * 2 + (m // tm) * k * n * 2
            if best is None or traffic < best[0]:
                best = (traffic, tm, tn)
    return (best[1], best[2]) if best else (m, n)


def matmul(a, b, *, trans_b=False, out_dtype=F32, name):
    m, k = a.shape
    n = b.shape[0] if trans_b else b.shape[1]
    tk = k if k <= 2048 else _pick(k, (1664, 1408, 1024, 896, 512, 256, 128))
    nk = k // tk
    tm, tn = _matmul_tiles(m, n, k, tk, jnp.dtype(out_dtype).itemsize)

    def product(a_ref, b_ref):
        return _dot_nt(a_ref[...], b_ref[...]) if trans_b else _dot(a_ref[...], b_ref[...])

    def kern_once(a_ref, b_ref, o_ref):
        o_ref[...] = product(a_ref, b_ref).astype(o_ref.dtype)

    def kern_acc(a_ref, b_ref, o_ref, acc_ref):
        kk = pl.program_id(2)

        @pl.when(kk == 0)
        def _():
            acc_ref[...] = product(a_ref, b_ref)

        @pl.when(kk > 0)
        def _():
            acc_ref[...] += product(a_ref, b_ref)

        @pl.when(kk == nk - 1)
        def _():
            o_ref[...] = acc_ref[...].astype(o_ref.dtype)

    b_spec = (pl.BlockSpec((tn, tk), lambda i, j, kk: (j, kk)) if trans_b
              else pl.BlockSpec((tk, tn), lambda i, j, kk: (kk, j)))
    return pl.pallas_call(
        kern_once if nk == 1 else kern_acc, name=name,
        grid=(m // tm, n // tn, nk),
        in_specs=[pl.BlockSpec((tm, tk), lambda i, j, kk: (i, kk)), b_spec],
        out_specs=pl.BlockSpec((tm, tn), lambda i, j, kk: (i, j)),
        out_shape=jax.ShapeDtypeStruct((m, n), out_dtype),
        scratch_shapes=[] if nk == 1 else [pltpu.VMEM((tm, tn), F32)],
        compiler_params=_cp("parallel", "parallel", "arbitrary"),
    )(a, b)


def matmul_cat(a_list, b, *, trans_b=False, out_dtype=F32, name, carry=None):
    m = a_list[0].shape[0]
    ks = [x.shape[1] for x in a_list]
    ktot = sum(ks)
    n = b.shape[0] if trans_b else b.shape[1]
    tn = _pick(n, (512, 256, 128))
    tm = next((c for c in (1664, 1040, 832, 640, 512, 384, 256, 128)
               if m % c == 0 and c * ktot * 2 <= 8 * 1024 * 1024), m)
    na = len(a_list)

    def kern(*refs):
        b_ref, o_ref = refs[na], refs[na + 1]
        acc, off = None, 0
        for a_ref, k in zip(refs[:na], ks):
            if trans_b:
                part = _dot_nt(a_ref[...], b_ref[:, off:off + k])
            else:
                part = _dot(a_ref[...], b_ref[off:off + k, :])
            acc = part if acc is None else acc + part
            off += k
        o_ref[...] = acc.astype(o_ref.dtype)

    b_spec = (pl.BlockSpec((tn, ktot), lambda i, j: (j, 0)) if trans_b
              else pl.BlockSpec((ktot, tn), lambda i, j: (0, j)))
    res = _call(
        kern, (*a_list, b), name=name, grid=(m // tm, n // tn),
        in_specs=[pl.BlockSpec((tm, k), lambda i, j: (i, 0)) for k in ks] + [b_spec],
        out_specs=[pl.BlockSpec((tm, tn), lambda i, j: (i, j))],
        out_shape=[jax.ShapeDtypeStruct((m, n), out_dtype)],
        sem=("parallel", "parallel"), carry=carry)
    return res[0] if carry is None else (res[0], res[1:])


def _row_tile(t):
    return _pick(t, (832, 640, 512, 384, 256, 128))


def rmsnorm_fwd(x, w, res=None, *, out_dtype, name, with_t=False, carry=None):
    t, d = x.shape
    tr = _conv_tile(t) if with_t else _row_tile(t)

    def kern(*refs):
        x_ref, w_ref = refs[0], refs[1]
        o_ref = refs[-2] if with_t else refs[-1]
        xv = x_ref[...]
        r = lax.rsqrt(jnp.mean(xv * xv, axis=-1, keepdims=True) + EPS)
        y = xv * r * w_ref[...]
        if res is not None:
            y = refs[2][...] + y
        o_ref[...] = y.astype(o_ref.dtype)
        if with_t:
            refs[-1][...] = y.T.astype(o_ref.dtype)

    row = pl.BlockSpec((tr, d), lambda i: (i, 0))
    vec = pl.BlockSpec((1, d), lambda i: (0, 0))
    ins = [x, w.reshape(1, d)] + ([] if res is None else [res])
    specs = [row, vec] + ([] if res is None else [row])
    out_specs, out_shape = [row], [jax.ShapeDtypeStruct((t, d), out_dtype)]
    if with_t:
        out_specs.append(pl.BlockSpec((d, tr), lambda i: (0, i)))
        out_shape.append(jax.ShapeDtypeStruct((d, t), out_dtype))
    res_ = _call(kern, ins, name=name, grid=(t // tr,), in_specs=specs, out_specs=out_specs,
                 out_shape=out_shape, sem=("parallel",), carry=carry)
    return res_[0] if len(res_) == 1 else res_


def rmsnorm_bwd(x, w, dy, res=None, *, out_dtype, name, carry=None):
    t, d = x.shape
    tr = _row_tile(t)

    def kern(*refs):
        if res is None:
            x_ref, w_ref, dy_ref, dx_ref, dw_ref = refs
        else:
            x_ref, w_ref, dy_ref, r_ref, dx_ref, dw_ref = refs
        i = pl.program_id(0)
        xv = x_ref[...]
        dyv = dy_ref[...].astype(F32)
        r = lax.rsqrt(jnp.mean(xv * xv, axis=-1, keepdims=True) + EPS)
        xh = xv * r
        g = dyv * w_ref[...]
        dx = r * (g - xh * jnp.mean(g * xh, axis=-1, keepdims=True))
        if res is not None:
            dx = r_ref[...] + dx
        dx_ref[...] = dx.astype(dx_ref.dtype)
        part = jnp.sum(dyv * xh, axis=0, keepdims=True)

        @pl.when(i == 0)
        def _():
            dw_ref[...] = part

        @pl.when(i > 0)
        def _():
            dw_ref[...] += part

    row = pl.BlockSpec((tr, d), lambda i: (i, 0))
    vec = pl.BlockSpec((1, d), lambda i: (0, 0))
    ins = [x, w.reshape(1, d), dy] + ([] if res is None else [res])
    specs = [row, vec, row] + ([] if res is None else [row])
    return _call(kern, ins, name=name, grid=(t // tr,), in_specs=specs, out_specs=[row, vec],
                 out_shape=[jax.ShapeDtypeStruct((t, d), out_dtype), jax.ShapeDtypeStruct((1, d), F32)],
                 sem=("arbitrary",), carry=carry)


def _conv_tile(t):
    return _pick(t, (640, 384, 256, 128))


def _conv_apply(x, tail, cw, cb, ksz):
    y = cb
    for k in range(ksz):
        y = y + cw[k:k + 1, :] * _shift_down(x, tail, ksz - 1 - k)
    return y


def dwconv_fwd(x, cw, cb, *, mode, x_off, c_out, cblk, out_dtype, name, with_t=False):
    t = x.shape[0]
    ksz = cw.shape[0]
    tb = _conv_tile(t)
    nb, ncb, t8 = t // tb, c_out // cblk, tb // 8
    xo = x_off // cblk
    nin = 2 if mode == "geglu" else 1

    def kern(*refs):
        o_ref = refs[-2] if with_t else refs[-1]
        n = pl.program_id(1)
        for c in range(cblk // BLK):
            ls = slice(c * BLK, (c + 1) * BLK)
            for s in range(tb // BLK):
                rs = slice(s * BLK, (s + 1) * BLK)
                valid = (n * tb + s * BLK + _rows((BLK, BLK))) >= PAD
                hs = []
                for q in range(nin):
                    x_ref, t_ref, w_ref, b_ref = refs[4 * q:4 * q + 4]
                    tail = (jnp.where(n > 0, t_ref[:, ls], 0.0) if s == 0
                            else x_ref[s * BLK - 8:s * BLK, ls])
                    hs.append(_conv_apply(x_ref[rs, ls], tail, w_ref[:, ls], b_ref[:, ls], ksz))
                y = _gelu(hs[0]) * hs[1] if mode == "geglu" else _silu(hs[0])
                y = _keep(y, valid, s)
                o_ref[rs, ls] = y.astype(o_ref.dtype)
                if with_t:
                    refs[-1][ls, rs] = y.T.astype(o_ref.dtype)

    ins, specs = [], []
    for q in range(nin):
        co = xo + q * ncb
        wo = q * ncb
        ins += [x, x, cw, cb.reshape(1, -1)]
        specs += [
            pl.BlockSpec((tb, cblk), lambda j, n, co=co: (n, co + j)),
            pl.BlockSpec((8, cblk), lambda j, n, co=co: (jnp.maximum(n * t8 - 1, 0), co + j)),
            pl.BlockSpec((ksz, cblk), lambda j, n, wo=wo: (0, wo + j)),
            pl.BlockSpec((1, cblk), lambda j, n, wo=wo: (0, wo + j)),
        ]
    out_specs = pl.BlockSpec((tb, cblk), lambda j, n: (n, j))
    out_shape = jax.ShapeDtypeStruct((t, c_out), out_dtype)
    if with_t:
        out_specs = [out_specs, pl.BlockSpec((cblk, tb), lambda j, n: (j, n))]
        out_shape = [out_shape, jax.ShapeDtypeStruct((c_out, t), out_dtype)]
    return pl.pallas_call(
        kern, name=name, grid=(ncb, nb), in_specs=specs, out_specs=out_specs, out_shape=out_shape,
        compiler_params=_cp("parallel", "parallel"),
    )(*ins)


def dwconv_bwd(x, cw, cb, dy, *, mode, x_off, c_out, cblk, name, carry=None):
    t = x.shape[0]
    ksz = cw.shape[0]
    tb = _conv_tile(t)
    nb, ncb, t8 = t // tb, c_out // cblk, tb // 8
    xo = x_off // cblk
    nin = 2 if mode == "geglu" else 1
    ctot = nin * c_out

    def kern(*refs):
        dy_ref = refs[4 * nin]
        outs = refs[4 * nin + 1:4 * nin + 1 + 3 * nin]
        heads = refs[4 * nin + 1 + 3 * nin:]
        n = pl.program_id(1)
        blk = nb - 1 - n

        @pl.when(n == 0)
        def _():
            for q in range(nin):
                heads[q][...] = jnp.zeros_like(heads[q])
                outs[3 * q + 1][...] = jnp.zeros_like(outs[3 * q + 1])
                outs[3 * q + 2][...] = jnp.zeros_like(outs[3 * q + 2])

        for c in range(cblk // BLK):
            ls = slice(c * BLK, (c + 1) * BLK)
            head = [heads[q][:, ls] for q in range(nin)]
            dwa = [[None] * ksz for _ in range(nin)]
            dba = [None] * nin
            for s in reversed(range(tb // BLK)):
                rs = slice(s * BLK, (s + 1) * BLK)
                valid = (blk * tb + s * BLK + _rows((BLK, BLK))) >= PAD
                xs, tails, hs = [], [], []
                for q in range(nin):
                    x_ref, t_ref, w_ref, b_ref = refs[4 * q:4 * q + 4]
                    tail = (jnp.where(blk > 0, t_ref[:, ls], 0.0) if s == 0
                            else x_ref[s * BLK - 8:s * BLK, ls])
                    xs.append(x_ref[rs, ls])
                    tails.append(tail)
                    hs.append(_conv_apply(xs[q], tail, w_ref[:, ls], b_ref[:, ls], ksz))
                dyv = dy_ref[rs, ls].astype(F32)
                if mode == "geglu":
                    dhs = [dyv * hs[1] * _gelu_grad(hs[0]), dyv * _gelu(hs[0])]
                else:
                    dhs = [dyv * _silu_grad(hs[0])]
                for q in range(nin):
                    w_ref = refs[4 * q + 2]
                    dh = _keep(dhs[q], valid, s)
                    dx = jnp.zeros_like(dh)
                    for k in range(ksz):
                        sh = ksz - 1 - k
                        dx = dx + w_ref[k:k + 1, ls] * _shift_up(dh, head[q], sh)
                        part = jnp.sum(dh * _shift_down(xs[q], tails[q], sh), axis=0, keepdims=True)
                        dwa[q][k] = part if dwa[q][k] is None else dwa[q][k] + part
                    outs[3 * q][rs, ls] = _keep(dx, valid, s).astype(outs[3 * q].dtype)
                    part = jnp.sum(dh, axis=0, keepdims=True)
                    dba[q] = part if dba[q] is None else dba[q] + part
                    head[q] = dh[0:8]
            for q in range(nin):
                outs[3 * q + 1][:, ls] += jnp.concatenate(dwa[q], axis=0)
                outs[3 * q + 2][:, ls] += dba[q]
                heads[q][:, ls] = head[q]

    ins, specs, out_specs, out_shape, scratch = [], [], [], [], []
    for q in range(nin):
        co = xo + q * ncb
        wo = q * ncb
        ins += [x, x, cw, cb.reshape(1, -1)]
        specs += [
            pl.BlockSpec((tb, cblk), lambda j, n, co=co: (nb - 1 - n, co + j)),
            pl.BlockSpec((8, cblk), lambda j, n, co=co: (jnp.maximum((nb - 1 - n) * t8 - 1, 0), co + j)),
            pl.BlockSpec((ksz, cblk), lambda j, n, wo=wo: (0, wo + j)),
            pl.BlockSpec((1, cblk), lambda j, n, wo=wo: (0, wo + j)),
        ]
        out_specs += [
            pl.BlockSpec((tb, cblk), lambda j, n: (nb - 1 - n, j)),
            pl.BlockSpec((ksz, cblk), lambda j, n: (0, j)),
            pl.BlockSpec((1, cblk), lambda j, n: (0, j)),
        ]
        out_shape += [jax.ShapeDtypeStruct((t, c_out), MXU),
                      jax.ShapeDtypeStruct((ksz, c_out), F32),
                      jax.ShapeDtypeStruct((1, c_out), F32)]
        scratch.append(pltpu.VMEM((8, cblk), F32))
    ins.append(dy)
    specs.append(pl.BlockSpec((tb, cblk), lambda j, n: (nb - 1 - n, j)))
    res = _call(kern, ins, name=name, grid=(ncb, nb), in_specs=specs, out_specs=out_specs,
                out_shape=out_shape, scratch_shapes=scratch, sem=("parallel", "arbitrary"), carry=carry)
    dxs = [res[3 * q] for q in range(nin)]
    dcw = jnp.concatenate([res[3 * q + 1] for q in range(nin)], axis=1)
    dcb = jnp.concatenate([res[3 * q + 2] for q in range(nin)], axis=1)
    return dxs, dcw, dcb.reshape(ctot), res[3 * nin:]


def _lru_tile(t):
    return _pick(t, (640, 384, 256, 128))


def _lru_gates(xc, wa, ba, wx, bx, sp):
    r = _sigmoid(_dot(xc, wa) + ba)
    i = _sigmoid(_dot(xc, wx) + bx)
    log_a = -LRU_C * r * sp
    a = jnp.exp(log_a)
    mult = jnp.sqrt(-_expm1(2.0 * log_a))
    return r, i, a, mult


def lru_fwd(proj, cw, cb, wa, ba, wx, bx, lam, *, gate_off, xr_off, name, carry=None):
    t = proj.shape[0]
    tb = _lru_tile(t)
    nb, ns, t8 = t // tb, tb // BLK, tb // 8
    go, xo = gate_off // BLK, xr_off // BLK

    def kern(g_ref, x_ref, xt_ref, cw_ref, cb_ref, wa_ref, ba_ref, wx_ref, bx_ref, lam_ref,
             y_ref, yt_ref, h_ref, hc_ref):
        n = pl.program_id(1)

        @pl.when(n == 0)
        def _():
            hc_ref[...] = jnp.zeros_like(hc_ref)

        sp = _softplus(-lam_ref[...])
        hprev = hc_ref[0:1, :]
        scans = []
        for s in range(ns):
            sl = slice(s * BLK, (s + 1) * BLK)
            xv = x_ref[sl, :]
            tail = jnp.where(n > 0, xt_ref[...], 0.0) if s == 0 else x_ref[s * BLK - 8:s * BLK, :]
            valid = (n * tb + s * BLK + _rows((BLK, BLK))) >= PAD
            xc = _keep(_conv_apply(xv, tail, cw_ref[...], cb_ref[...], 4), valid, s)
            _, i, a, mult = _lru_gates(xc, wa_ref[0], ba_ref[...], wx_ref[0], bx_ref[...], sp)
            scans.append(_scan_fwd(a, mult * (i * xc)))
        for s in range(ns):
            sl = slice(s * BLK, (s + 1) * BLK)
            ca, cu = scans[s]
            h = cu + ca * hprev
            hprev = _row_at(h, BLK - 1)
            h_ref[sl, :] = h
            y = _gelu(g_ref[sl, :]) * h
            y_ref[sl, :] = y.astype(y_ref.dtype)
            yt_ref[:, sl] = y.T.astype(yt_ref.dtype)
        hc_ref[...] = jnp.broadcast_to(hprev, hc_ref.shape)

    vec = pl.BlockSpec((1, BLK), lambda j, n: (0, j))
    mat = pl.BlockSpec((1, BLK, BLK), lambda j, n: (j, 0, 0))
    return _call(
        kern, (proj, proj, proj, cw, cb.reshape(1, -1), wa, ba.reshape(1, -1), wx, bx.reshape(1, -1),
               lam.reshape(1, -1)),
        name=name, grid=(D_RNN // BLK, nb),
        in_specs=[
            pl.BlockSpec((tb, BLK), lambda j, n: (n, go + j)),
            pl.BlockSpec((tb, BLK), lambda j, n: (n, xo + j)),
            pl.BlockSpec((8, BLK), lambda j, n: (jnp.maximum(n * t8 - 1, 0), xo + j)),
            pl.BlockSpec((4, BLK), lambda j, n: (0, j)), vec, mat, vec, mat, vec, vec,
        ],
        out_specs=[pl.BlockSpec((tb, BLK), lambda j, n: (n, j)),
                   pl.BlockSpec((BLK, tb), lambda j, n: (j, n)),
                   pl.BlockSpec((tb, BLK), lambda j, n: (n, j))],
        out_shape=[jax.ShapeDtypeStruct((t, D_RNN), MXU), jax.ShapeDtypeStruct((D_RNN, t), MXU),
                   jax.ShapeDtypeStruct((t, D_RNN), F32)],
        scratch_shapes=[pltpu.VMEM((8, BLK), F32)],
        sem=("parallel", "arbitrary"), carry=carry)


def lru_bwd(proj, h, dy, cw, cb, wa, ba, wx, bx, lam, *, gate_off, xr_off, dy_off, name, carry=None):
    t = proj.shape[0]
    tb = _lru_tile(t)
    nb, ns, t8 = t // tb, tb // BLK, tb // 8
    go, xo, do = gate_off // BLK, xr_off // BLK, dy_off // BLK

    def kern(g_ref, x_ref, xt_ref, h_ref, ht_ref, dy_ref, cw_ref, cb_ref, wa_ref, ba_ref,
             wx_ref, bx_ref, lam_ref,
             dg_ref, dx_ref, dcw_ref, dcb_ref, dwa_ref, dba_ref, dwx_ref, dbx_ref, dlam_ref,
             gin_ref, head_ref):
        n = pl.program_id(1)
        blk = nb - 1 - n

        @pl.when(n == 0)
        def _():
            gin_ref[...] = jnp.zeros_like(gin_ref)
            head_ref[...] = jnp.zeros_like(head_ref)
            for r_ in (dcw_ref, dcb_ref, dwa_ref, dba_ref, dwx_ref, dbx_ref, dlam_ref):
                r_[...] = jnp.zeros_like(r_)

        lamv = lam_ref[...]
        sp = _softplus(-lamv)
        dsp_dlam = -_sigmoid(-lamv)
        g_in = gin_ref[0:1, :]
        head = head_ref[...]
        ones8 = jnp.ones((8, BLK), F32)
        wav, wxv = wa_ref[0], wx_ref[0]
        staged = {}
        for s in range(ns):
            sl = slice(s * BLK, (s + 1) * BLK)
            xv = x_ref[sl, :]
            if s == 0:
                tail = jnp.where(blk > 0, xt_ref[...], 0.0)
                htail = jnp.where(blk > 0, ht_ref[...], 0.0)
            else:
                tail = x_ref[s * BLK - 8:s * BLK, :]
                htail = h_ref[s * BLK - 8:s * BLK, :]
            valid = (blk * tb + s * BLK + _rows((BLK, BLK))) >= PAD
            xc = _keep(_conv_apply(xv, tail, cw_ref[...], cb_ref[...], 4), valid, s)
            r, i, a, mult = _lru_gates(xc, wav, ba_ref[...], wxv, bx_ref[...], sp)
            hv = h_ref[sl, :]
            hprev = _shift_down(hv, htail, 1)
            gv = g_ref[sl, :]
            dyv = dy_ref[sl, :].astype(F32)
            dg_ref[sl, :] = (dyv * hv * _gelu_grad(gv)).astype(dg_ref.dtype)
            cc, cu = _scan_rev(_shift_up(a, ones8, 1), dyv * _gelu(gv))
            staged[s] = (xv, tail, valid, xc, r, i, a, mult, hprev, cc, cu)
        for s in reversed(range(ns)):
            sl = slice(s * BLK, (s + 1) * BLK)
            xv, tail, valid, xc, r, i, a, mult, hprev, cc, cu = staged[s]
            gg = cu + cc * g_in
            g_in = _row_at(a * gg, 0)
            da = gg * hprev
            di = gg * mult * xc
            dxc = gg * mult * i
            dmult = gg * i * xc
            dlog_a = da * a - dmult * (a * a) / mult
            dr = dlog_a * (-LRU_C * sp)
            dlam_ref[...] += jnp.sum(dlog_a * (-LRU_C) * r, axis=0, keepdims=True) * dsp_dlam
            dpr = dr * r * (1.0 - r)
            dpi = di * i * (1.0 - i)
            dxc = dxc + _dot_nt(dpr, wav) + _dot_nt(dpi, wxv)
            dxc, dpr, dpi = _keep(dxc, valid, s), _keep(dpr, valid, s), _keep(dpi, valid, s)
            dwa_ref[0] += _dot_tn(xc, dpr)
            dwx_ref[0] += _dot_tn(xc, dpi)
            dba_ref[...] += jnp.sum(dpr, axis=0, keepdims=True)
            dbx_ref[...] += jnp.sum(dpi, axis=0, keepdims=True)
            dx = jnp.zeros_like(dxc)
            dws = []
            for k in range(4):
                dx = dx + cw_ref[k:k + 1, :] * _shift_up(dxc, head, 3 - k)
                dws.append(jnp.sum(dxc * _shift_down(xv, tail, 3 - k), axis=0, keepdims=True))
            dx_ref[sl, :] = _keep(dx, valid, s).astype(dx_ref.dtype)
            dcw_ref[...] += jnp.concatenate(dws, axis=0)
            dcb_ref[...] += jnp.sum(dxc, axis=0, keepdims=True)
            head = dxc[0:8]
        gin_ref[...] = jnp.broadcast_to(g_in, gin_ref.shape)
        head_ref[...] = head

    vec = pl.BlockSpec((1, BLK), lambda j, n: (0, j))
    mat = pl.BlockSpec((1, BLK, BLK), lambda j, n: (j, 0, 0))
    cws = pl.BlockSpec((4, BLK), lambda j, n: (0, j))

    def rb(off):
        return pl.BlockSpec((tb, BLK), lambda j, n: (nb - 1 - n, off + j))

    def tl(off):
        return pl.BlockSpec((8, BLK), lambda j, n: (jnp.maximum((nb - 1 - n) * t8 - 1, 0), off + j))

    return _call(
        kern, (proj, proj, proj, h, h, dy, cw, cb.reshape(1, -1), wa, ba.reshape(1, -1), wx,
               bx.reshape(1, -1), lam.reshape(1, -1)),
        name=name, grid=(D_RNN // BLK, nb),
        in_specs=[rb(go), rb(xo), tl(xo), rb(0), tl(0), rb(do), cws, vec, mat, vec, mat, vec, vec],
        out_specs=[rb(0), rb(0), cws, vec, mat, vec, mat, vec, vec],
        out_shape=[jax.ShapeDtypeStruct((t, D_RNN), MXU), jax.ShapeDtypeStruct((t, D_RNN), MXU),
                   jax.ShapeDtypeStruct((4, D_RNN), F32), jax.ShapeDtypeStruct((1, D_RNN), F32),
                   jax.ShapeDtypeStruct((8, BLK, BLK), F32), jax.ShapeDtypeStruct((1, D_RNN), F32),
                   jax.ShapeDtypeStruct((8, BLK, BLK), F32), jax.ShapeDtypeStruct((1, D_RNN), F32),
                   jax.ShapeDtypeStruct((1, D_RNN), F32)],
        scratch_shapes=[pltpu.VMEM((8, BLK), F32), pltpu.VMEM((8, BLK), F32)],
        sem=("parallel", "arbitrary"), carry=carry)


_SCALE = HEAD_DIM ** -0.5


STK = 4


def _attn_masks(n):
    qi = np.arange(STK * BLK)[:, None] % BLK
    c = np.arange(3 * BLK)[None, :]
    tq = n * BLK + qi - PAD
    s_band = (n - 1) * BLK + c - PAD
    d_band = tq - s_band
    ok_band = (s_band >= N_META) & (d_band >= 0) & (d_band < BLK)
    jm = c - 2 * BLK
    d_meta = tq - (jm - PAD)
    ok_meta = (jm >= PAD) & (d_meta >= 0)
    is_band = c < 2 * BLK
    ok = np.where(is_band, ok_band, ok_meta)
    dist = np.where(is_band, d_band, np.minimum(d_meta, BLK)).astype(np.float32)
    return ok, dist


def _stack_heads(g, e):
    return [8 * g + 2 * i + e for i in range(STK)]


def _attn_bias_table():
    tabs = []
    for n in range(3):
        ok, dist = _attn_masks(n)
        per = []
        for g in range(2):
            for e in range(2):
                slope = np.repeat(np.array([2.0 ** (-8.0 * (h + 1) / N_Q_HEADS) for h in _stack_heads(g, e)],
                                           np.float32), BLK)[:, None]
                per.append(np.where(ok, -(slope * dist), np.float32(NEG)).astype(np.float32))
        tabs.append(np.stack(per))
    return jnp.asarray(np.stack(tabs))


def _stack_sinks(heads, sk):
    return jnp.concatenate(
        [jnp.broadcast_to(jnp.sum(jnp.where(_lanes(sk.shape) == h, sk, 0.0), axis=1, keepdims=True),
                          (BLK, 1)) for h in heads], axis=0)


def _stack_tiles(ref, g, sel):
    return jnp.concatenate(
        [jnp.where(sel, ref[:, (4 * g + i) * BLK:(4 * g + i + 1) * BLK].astype(F32), 0.0)
         for i in range(STK)], axis=0)


def _attn_probs(qk, bias, sink):
    s = qk * _SCALE + bias
    mx = jnp.maximum(jnp.max(s, axis=-1, keepdims=True), sink)
    p = jnp.exp(s - mx)
    es = jnp.exp(sink - mx)
    inv = 1.0 / (jnp.sum(p, axis=-1, keepdims=True) + es)
    return p * inv, es * inv


def _attn_specs(t, q_off, k_off, v_off, rev):
    nb = t // BLK
    qo, ko, vo = q_off // 1024, k_off // BLK, v_off // BLK

    def b(n):
        return nb - 1 - n if rev else n

    return [
        pl.BlockSpec((BLK, 1024), lambda n: (b(n), qo)),
        pl.BlockSpec((BLK, BLK), lambda n: (b(n), ko)),
        pl.BlockSpec((BLK, BLK), lambda n: (b(n), vo)),
        pl.BlockSpec((BLK, BLK), lambda n: (jnp.maximum(b(n) - 1, 0), ko)),
        pl.BlockSpec((BLK, BLK), lambda n: (jnp.maximum(b(n) - 1, 0), vo)),
        pl.BlockSpec((BLK, BLK), lambda n: (0, ko)),
        pl.BlockSpec((BLK, BLK), lambda n: (0, vo)),
        pl.BlockSpec((1, BLK), lambda n: (0, 0)),
        pl.BlockSpec((1, 4, STK * BLK, 3 * BLK), lambda n: (jnp.minimum(b(n), 2), 0, 0, 0)),
    ]


def attn_fwd(proj, sinks, *, q_off, k_off, v_off, name, carry=None):
    t = proj.shape[0]
    nb = t // BLK

    def kern(q_ref, kc_ref, vc_ref, kp_ref, vp_ref, km_ref, vm_ref, sk_ref, tab_ref, o_ref):
        k_all = jnp.concatenate([kp_ref[...], kc_ref[...], km_ref[...]], axis=0)
        v_all = jnp.concatenate([vp_ref[...], vc_ref[...], vm_ref[...]], axis=0)
        k_alt = pltpu.roll(k_all, HEAD_DIM, 1)
        v_alt = pltpu.roll(v_all, HEAD_DIM, 1)
        low = _lanes((BLK, BLK)) < HEAD_DIM
        stacks = [(g, e) for g in range(2) for e in range(2)]
        qk = {(g, e): _dot_nt(_stack_tiles(q_ref, g, low == (e == 0)), k_all if g == e else k_alt)
              for g, e in stacks}
        ps = {(g, e): _attn_probs(qk[g, e], tab_ref[0, 2 * g + e],
                                  _stack_sinks(_stack_heads(g, e), sk_ref[...]))[0] for g, e in stacks}
        outs = {(g, e): _dot(ps[g, e], v_all if g == e else v_alt) for g, e in stacks}
        for hp in range(N_Q_HEADS // 2):
            g, rs = hp // STK, slice((hp % STK) * BLK, (hp % STK + 1) * BLK)
            o_ref[:, hp * BLK:(hp + 1) * BLK] = jnp.where(low, outs[g, 0][rs], outs[g, 1][rs]).astype(o_ref.dtype)

    sk = jnp.zeros((1, BLK), F32).at[0, :N_Q_HEADS].set(sinks)
    return _call(
        kern, (proj, proj, proj, proj, proj, proj, proj, sk, _attn_bias_table()), name=name, grid=(nb,),
        in_specs=_attn_specs(t, q_off, k_off, v_off, False),
        out_specs=[pl.BlockSpec((BLK, 1024), lambda n: (n, 0))],
        out_shape=[jax.ShapeDtypeStruct((t, 1024), MXU)],
        sem=("parallel",), carry=carry)


def attn_bwd(proj, sinks, dy, *, q_off, k_off, v_off, dy_off, name, carry=None):
    t = proj.shape[0]
    nb = t // BLK
    do = dy_off // 1024

    def kern(q_ref, kc_ref, vc_ref, kp_ref, vp_ref, km_ref, vm_ref, sk_ref, tab_ref, do_ref,
             dq_ref, dk_ref, dv_ref, dsk_ref, ck_ref, cv_ref, mk_ref, mv_ref):
        n = pl.program_id(0)
        blk = nb - 1 - n

        @pl.when(n == 0)
        def _():
            for r_ in (ck_ref, cv_ref, mk_ref, mv_ref, dsk_ref):
                r_[...] = jnp.zeros_like(r_)

        k_all = jnp.concatenate([kp_ref[...], kc_ref[...], km_ref[...]], axis=0)
        v_all = jnp.concatenate([vp_ref[...], vc_ref[...], vm_ref[...]], axis=0)
        k_alt = pltpu.roll(k_all, HEAD_DIM, 1)
        v_alt = pltpu.roll(v_all, HEAD_DIM, 1)
        low = _lanes((BLK, BLK)) < HEAD_DIM
        lane1 = _lanes((1, BLK))
        dk_all = jnp.zeros((3 * BLK, BLK), F32)
        dv_all = jnp.zeros((3 * BLK, BLK), F32)
        dsk = jnp.zeros((1, BLK), F32)
        stacks = [(g, e) for g in range(2) for e in range(2)]
        qm = {(g, e): _stack_tiles(q_ref, g, low == (e == 0)) for g, e in stacks}
        dom = {(g, e): _stack_tiles(do_ref, g, low == (e == 0)) for g, e in stacks}
        qk = {(g, e): _dot_nt(qm[g, e], k_all if g == e else k_alt) for g, e in stacks}
        dp = {(g, e): _dot_nt(dom[g, e], v_all if g == e else v_alt) for g, e in stacks}
        ps, dss = {}, {}
        for g, e in stacks:
            heads = _stack_heads(g, e)
            p, psink = _attn_probs(qk[g, e], tab_ref[0, 2 * g + e], _stack_sinks(heads, sk_ref[...]))
            delta = jnp.sum(p * dp[g, e], axis=-1, keepdims=True)
            ps[g, e] = p
            dss[g, e] = p * (dp[g, e] - delta) * _SCALE
            psd = psink * delta
            for i, h in enumerate(heads):
                dsk = dsk + jnp.where(lane1 == h, -jnp.sum(psd[i * BLK:(i + 1) * BLK], axis=0, keepdims=True), 0.0)
        dqs = {(g, e): _dot(dss[g, e], k_all if g == e else k_alt) for g, e in stacks}
        for g, e in stacks:
            dkh = _dot_tn(dss[g, e], qm[g, e])
            dvh = _dot_tn(ps[g, e], dom[g, e])
            if g != e:
                dkh = pltpu.roll(dkh, HEAD_DIM, 1)
                dvh = pltpu.roll(dvh, HEAD_DIM, 1)
            dk_all = dk_all + dkh
            dv_all = dv_all + dvh
        for hp in range(N_Q_HEADS // 2):
            g, rs = hp // STK, slice((hp % STK) * BLK, (hp % STK + 1) * BLK)
            dq_ref[:, hp * BLK:(hp + 1) * BLK] = jnp.where(low, dqs[g, 0][rs], dqs[g, 1][rs]).astype(dq_ref.dtype)
        dsk_ref[...] += dsk
        mk_ref[...] += dk_all[2 * BLK:3 * BLK]
        mv_ref[...] += dv_all[2 * BLK:3 * BLK]
        is0 = blk == 0
        dk_ref[...] = (dk_all[BLK:2 * BLK] + ck_ref[...] + jnp.where(is0, mk_ref[...], 0.0)).astype(dk_ref.dtype)
        dv_ref[...] = (dv_all[BLK:2 * BLK] + cv_ref[...] + jnp.where(is0, mv_ref[...], 0.0)).astype(dv_ref.dtype)
        ck_ref[...] = dk_all[0:BLK]
        cv_ref[...] = dv_all[0:BLK]

    sk = jnp.zeros((1, BLK), F32).at[0, :N_Q_HEADS].set(sinks)
    kv = pl.BlockSpec((BLK, BLK), lambda n: (nb - 1 - n, 0))
    res = _call(
        kern, (proj, proj, proj, proj, proj, proj, proj, sk, _attn_bias_table(), dy), name=name, grid=(nb,),
        in_specs=_attn_specs(t, q_off, k_off, v_off, True)
        + [pl.BlockSpec((BLK, 1024), lambda n: (nb - 1 - n, do))],
        out_specs=[pl.BlockSpec((BLK, 1024), lambda n: (nb - 1 - n, 0)), kv, kv,
                   pl.BlockSpec((1, BLK), lambda n: (0, 0))],
        out_shape=[jax.ShapeDtypeStruct((t, 1024), MXU), jax.ShapeDtypeStruct((t, BLK), MXU),
                   jax.ShapeDtypeStruct((t, BLK), MXU), jax.ShapeDtypeStruct((1, BLK), F32)],
        scratch_shapes=[pltpu.VMEM((BLK, BLK), F32)] * 4,
        sem=("arbitrary",), carry=carry)
    return [res[0], res[1], res[2], res[3][0, :N_Q_HEADS]] + res[4:]


GW = D_SSM // SSD_GROUPS
EXP_ROWS = 3 * BLK + 8
RED_ROWS = EXP_ROWS + 8


def _head_expand():
    ch = jnp.arange(D_SSM) // HEAD_DIM
    return (jnp.arange(BLK)[:, None] == ch[None, :]).astype(BF16)


def _ssd_decay(raw, dtb, alog, rowv):
    valid = rowv & (_lanes((BLK, BLK)) < SSD_HEADS)
    pre = raw + dtb
    dtp = jnp.where(valid, _softplus(pre), 0.0)
    av = -jnp.exp(alog)
    cs = _cumsum_fwd(dtp * av)
    cs_last = _row_at(cs, BLK - 1)
    return valid, pre, dtp, av, cs, jnp.exp(cs), jnp.exp(cs_last - cs), jnp.exp(cs_last)


def _head_col(x, h):
    return jnp.sum(jnp.where(_lanes(x.shape) == h, x, 0.0), axis=1, keepdims=True)


def _ssd_group_fwd(g, xdt, cs, cst, cb, tril, low):
    lm = []
    for k in range(4):
        h = 4 * g + k
        seg = _head_col(cs, h) - _row_at(cst, h)
        lmat = jnp.where(tril, jnp.exp(jnp.minimum(seg, 0.0)), 0.0)
        lm.append((lmat, cb * lmat))
    hv = [_dot(lm[k][1], xdt[:, g * GW + (k // 2) * BLK:g * GW + (k // 2 + 1) * BLK]) for k in range(4)]
    return jnp.concatenate([jnp.where(low, hv[0], hv[1]), jnp.where(low, hv[2], hv[3])], axis=1), lm


def ssd_decay(proj, dt_bias, a_log, *, dt_off, name):
    t = proj.shape[0]
    tb = _conv_tile(t)
    dto = dt_off // BLK

    def kern(dt_ref, dtb_ref, alog_ref, o_ref):
        n = pl.program_id(0)
        for s in range(tb // BLK):
            rs = slice(s * BLK, (s + 1) * BLK)
            rowv = (n * tb + s * BLK + _rows((BLK, BLK))) >= PAD
            _, _, dtp, _, cs, ecs, w, _ = _ssd_decay(dt_ref[rs, :], dtb_ref[...], alog_ref[...], rowv)
            for k, v in enumerate((dtp, cs, ecs, w)):
                o_ref[rs, k * BLK:(k + 1) * BLK] = v

    vec = pl.BlockSpec((1, BLK), lambda n: (0, 0))
    return pl.pallas_call(
        kern, name=name, grid=(t // tb,),
        in_specs=[pl.BlockSpec((tb, BLK), lambda n: (n, dto)), vec, vec],
        out_specs=pl.BlockSpec((tb, 4 * BLK), lambda n: (n, 0)),
        out_shape=jax.ShapeDtypeStruct((t, 4 * BLK), F32),
        compiler_params=_cp("parallel"),
    )(proj, _pad128(dt_bias), _pad128(a_log))


def _load_decay(d_ref):
    dtp, cs, ecs, w = (d_ref[:, k * BLK:(k + 1) * BLK] for k in range(4))
    return dtp, cs, ecs, w, _row_at(ecs, BLK - 1)


def _expand_heads(dtp, ecs, w, dec, e):
    ex = _dot(jnp.concatenate([dtp, ecs, w, jnp.broadcast_to(dec, (8, BLK))], axis=0), e)
    return ex[0:BLK], ex[BLK:2 * BLK], ex[2 * BLK:3 * BLK], jnp.max(ex[3 * BLK:EXP_ROWS], axis=0, keepdims=True)


def _ssd_specs(t, z_off, dt_off, rev):
    nb = t // BLK
    zo, dto = z_off // D_SSM, dt_off // BLK

    def b(n):
        return nb - 1 - n if rev else n

    vec = lambda w: pl.BlockSpec((1, w), lambda n: (0, 0))
    return [
        pl.BlockSpec((BLK, D_SSM), lambda n: (b(n), 0)),
        pl.BlockSpec((BLK, 1024), lambda n: (b(n), 2)),
        pl.BlockSpec((BLK, 1024), lambda n: (b(n), 3)),
        pl.BlockSpec((BLK, D_SSM), lambda n: (b(n), zo)),
        pl.BlockSpec((BLK, BLK), lambda n: (b(n), dto)),
        vec(BLK), vec(BLK), vec(D_SSM), vec(D_SSM),
        pl.BlockSpec((BLK, D_SSM), lambda n: (0, 0)),
        pl.BlockSpec((BLK, 4 * BLK), lambda n: (b(n), 0)),
    ]


def _pad128(v):
    return jnp.zeros((1, BLK), F32).at[0, :v.shape[0]].set(v)


def ssd_fwd(xbc, proj, decay, dt_bias, a_log, d_skip, gate_norm, *, z_off, dt_off, name):
    t = xbc.shape[0]
    nb = t // BLK

    def kern(x_ref, b_ref, c_ref, z_ref, dt_ref, dtb_ref, alog_ref, dsk_ref, gn_ref, e_ref, d_ref,
             yn_ref, ynt_ref, st_ref, p_ref):
        n = pl.program_id(0)

        @pl.when(n == 0)
        def _():
            p_ref[...] = jnp.zeros_like(p_ref)

        bgs = [b_ref[:, g * BLK:(g + 1) * BLK] for g in range(SSD_GROUPS)]
        cgs = [c_ref[:, g * BLK:(g + 1) * BLK] for g in range(SSD_GROUPS)]
        cbs = [_dot_nt(cgs[g], bgs[g]) for g in range(SSD_GROUPS)]
        pgs = [p_ref[g] for g in range(SSD_GROUPS)]
        zs = [_dot(cgs[g], pgs[g]) for g in range(SSD_GROUPS)]
        new_p = []
        dtp, cs, ecs, w, dec = _load_decay(d_ref)
        dtp_c, ecs_c, w_c, dec_c = _expand_heads(dtp, ecs, w, dec, e_ref[...])
        xv = x_ref[...]
        xdt = xv * dtp_c
        wx = w_c * xdt
        cst = cs.T
        tril = _rows((BLK, BLK)) >= _lanes((BLK, BLK))
        low = _lanes((BLK, BLK)) < HEAD_DIM
        for g in range(SSD_GROUPS):
            st_ref[0, g] = pgs[g]
        for g in range(SSD_GROUPS):
            gs = slice(g * GW, (g + 1) * GW)
            ydiag, _ = _ssd_group_fwd(g, xdt, cs, cst, cbs[g], tril, low)
            y = ydiag + zs[g] * ecs_c[:, gs] + dsk_ref[:, gs] * xv[:, gs]
            new_p.append(pgs[g] * dec_c[:, gs] + _dot_tn(bgs[g], wx[:, gs]))
            yz = y * _silu(z_ref[:, gs])
            r = lax.rsqrt(jnp.mean(yz * yz, axis=-1, keepdims=True) + EPS)
            yn = yz * r * gn_ref[:, gs]
            yn_ref[:, gs] = yn.astype(yn_ref.dtype)
            ynt_ref[gs, :] = yn.T.astype(ynt_ref.dtype)
        for g in range(SSD_GROUPS):
            p_ref[g] = new_p[g]

    return pl.pallas_call(
        kern, name=name, grid=(nb,),
        in_specs=_ssd_specs(t, z_off, dt_off, False),
        out_specs=[pl.BlockSpec((BLK, D_SSM), lambda n: (n, 0)),
                   pl.BlockSpec((D_SSM, BLK), lambda n: (0, n)),
                   pl.BlockSpec((1, SSD_GROUPS, BLK, GW), lambda n: (n, 0, 0, 0))],
        out_shape=[jax.ShapeDtypeStruct((t, D_SSM), MXU), jax.ShapeDtypeStruct((D_SSM, t), MXU),
                   jax.ShapeDtypeStruct((nb, SSD_GROUPS, BLK, GW), F32)],
        scratch_shapes=[pltpu.VMEM((SSD_GROUPS, BLK, GW), F32)],
        compiler_params=_cp("arbitrary"),
    )(xbc, xbc, xbc, proj, proj, _pad128(dt_bias), _pad128(a_log),
      jnp.repeat(d_skip, HEAD_DIM).reshape(1, D_SSM), gate_norm.reshape(1, D_SSM), _head_expand(), decay)


def ssd_bwd(xbc, proj, decay, st, dyn, dt_bias, a_log, d_skip, gate_norm, *, z_off, dt_off, name, carry=None):
    t = xbc.shape[0]
    nb = t // BLK

    def kern(x_ref, b_ref, c_ref, z_ref, dt_ref, dtb_ref, alog_ref, dsk_ref, gn_ref, e_ref, d_ref,
             et_ref, st_ref, dyn_ref,
             dxbc_ref, dz_ref, draw_ref, dgn_ref, ddsk_ref, ddtb_ref, dalog_ref,
             dp_ref, tr_ref):
        n = pl.program_id(0)
        blk = nb - 1 - n

        @pl.when(n == 0)
        def _():
            for r_ in (dp_ref, dgn_ref, ddsk_ref, ddtb_ref, dalog_ref):
                r_[...] = jnp.zeros_like(r_)

        bgs = [b_ref[:, g * BLK:(g + 1) * BLK] for g in range(SSD_GROUPS)]
        cgs = [c_ref[:, g * BLK:(g + 1) * BLK] for g in range(SSD_GROUPS)]
        cbs = [_dot_nt(cgs[g], bgs[g]) for g in range(SSD_GROUPS)]
        pgs = [st_ref[0, g] for g in range(SSD_GROUPS)]
        dpns = [dp_ref[g] for g in range(SSD_GROUPS)]
        zs = [_dot(cgs[g], pgs[g]) for g in range(SSD_GROUPS)]
        dwxs = [_dot(bgs[g], dpns[g]) for g in range(SSD_GROUPS)]
        new_dp, dgn_parts = [], []
        valid = ((blk * BLK + _rows((BLK, BLK))) >= PAD) & (_lanes((BLK, BLK)) < SSD_HEADS)
        pre = dt_ref[...] + dtb_ref[...]
        av = -jnp.exp(alog_ref[...])
        dtp, cs, ecs, w, dec = _load_decay(d_ref)
        dtp_c, ecs_c, w_c, dec_c = _expand_heads(dtp, ecs, w, dec, e_ref[...])
        xv = x_ref[...]
        xdt = xv * dtp_c
        wx = w_c * xdt
        cst = cs.T
        tril = _rows((BLK, BLK)) >= _lanes((BLK, BLK))
        lane = _lanes((BLK, BLK))
        rowi = _rows((BLK, BLK))
        low = lane < HEAD_DIM
        dcs = jnp.zeros((BLK, BLK), F32)
        dcst = jnp.zeros((BLK, BLK), F32)
        for g in range(SSD_GROUPS):
            gs = slice(g * GW, (g + 1) * GW)
            bg, cg = bgs[g], cgs[g]
            pg, dpn = pgs[g], dpns[g]
            xg = xv[:, gs]
            ydiag, lm = _ssd_group_fwd(g, xdt, cs, cst, cbs[g], tril, low)
            yoff = zs[g] * ecs_c[:, gs]
            y = ydiag + yoff + dsk_ref[:, gs] * xg
            zz = z_ref[:, gs]
            sz = _silu(zz)
            yz = y * sz
            r = lax.rsqrt(jnp.mean(yz * yz, axis=-1, keepdims=True) + EPS)
            yhat = yz * r
            dynv = dyn_ref[:, gs].astype(F32)
            gy = dynv * gn_ref[:, gs]
            dgn_parts.append(jnp.sum(dynv * yhat, axis=0, keepdims=True))
            dyz = r * (gy - yhat * jnp.mean(gy * yhat, axis=-1, keepdims=True))
            dy = dyz * sz
            dz_ref[:, gs] = (dyz * y * _silu_grad(zz)).astype(dz_ref.dtype)
            tr_ref[EXP_ROWS:RED_ROWS, gs] = jnp.broadcast_to(
                jnp.sum(dy * xg, axis=0, keepdims=True), (8, GW))
            dx = dsk_ref[:, gs] * dy
            dwx = dwxs[g]
            dxdt = w_c[:, gs] * dwx
            tr_ref[0:BLK, gs] = dwx * wx[:, gs]
            dbg = _dot_nt(wx[:, gs], dpn)
            dzo = ecs_c[:, gs] * dy
            tr_ref[BLK:2 * BLK, gs] = dy * yoff
            dcg = _dot_nt(dzo, pg)
            new_dp.append(dec_c[:, gs] * dpn + _dot_tn(cg, dzo))
            tr_ref[3 * BLK:EXP_ROWS, gs] = jnp.broadcast_to(
                jnp.sum(dpn * pg, axis=0, keepdims=True), (8, GW))
            dyh = [jnp.where(low == (k % 2 == 0), dy[:, (k // 2) * BLK:(k // 2 + 1) * BLK], 0.0) for k in range(4)]
            dms = [_dot_nt(dyh[k], xdt[:, g * GW + (k // 2) * BLK:g * GW + (k // 2 + 1) * BLK]) for k in range(4)]
            accs = [_dot_tn(lm[k][1], dyh[k]) for k in range(4)]
            dcb = jnp.zeros((BLK, BLK), F32)
            for k in range(4):
                h = 4 * g + k
                lmat, mmat = lm[k]
                dm = jnp.where(tril, dms[k], 0.0)
                nh = dm * mmat
                dcs = dcs + jnp.where(lane == h, jnp.sum(nh, axis=1, keepdims=True), 0.0)
                dcst = dcst - jnp.where(rowi == h, jnp.sum(nh, axis=0, keepdims=True), 0.0)
                dcb = dcb + dm * lmat
            dxdt = dxdt + jnp.concatenate([accs[0] + accs[1], accs[2] + accs[3]], axis=1)
            dcg = dcg + _dot(dcb, bg)
            dbg = dbg + _dot_tn(dcb, cg)
            tr_ref[2 * BLK:3 * BLK, gs] = dxdt * xg
            dxbc_ref[:, gs] = dx + dxdt * dtp_c[:, gs]
            dxbc_ref[:, D_SSM + g * BLK:D_SSM + (g + 1) * BLK] = dbg
            dxbc_ref[:, D_SSM + 1024 + g * BLK:D_SSM + 1024 + (g + 1) * BLK] = dcg
        red = _dot(tr_ref[...], et_ref[...])
        r1, r2, r3 = red[0:BLK], red[BLK:2 * BLK], red[2 * BLK:3 * BLK]
        for g in range(SSD_GROUPS):
            dp_ref[g] = new_dp[g]
        dgn_ref[...] += jnp.concatenate(dgn_parts, axis=1)
        ddec = jnp.max(red[3 * BLK:EXP_ROWS], axis=0, keepdims=True)
        ddsk_ref[...] += jnp.max(red[EXP_ROWS:RED_ROWS], axis=0, keepdims=True)
        dcs = dcs + dcst.T - r1 + r2
        dcs_last = jnp.sum(r1, axis=0, keepdims=True) + ddec * dec
        dcs = dcs + jnp.where(rowi == BLK - 1, dcs_last, 0.0)
        dda = _cumsum_rev(dcs)
        ddtp = r3 + dda * av
        dalog_ref[...] += jnp.sum(dda * dtp, axis=0, keepdims=True) * av
        draw = jnp.where(valid, ddtp * _sigmoid(pre), 0.0)
        ddtb_ref[...] += jnp.sum(draw, axis=0, keepdims=True)
        draw_ref[...] = draw.astype(draw_ref.dtype)

    vec = lambda w_: pl.BlockSpec((1, w_), lambda n: (0, 0))
    rb = lambda w_: pl.BlockSpec((BLK, w_), lambda n: (nb - 1 - n, 0))
    e = _head_expand()
    res = _call(
        kern, (xbc, xbc, xbc, proj, proj, _pad128(dt_bias), _pad128(a_log),
               jnp.repeat(d_skip, HEAD_DIM).reshape(1, D_SSM), gate_norm.reshape(1, D_SSM), e, decay, e.T, st, dyn),
        name=name, grid=(nb,),
        in_specs=_ssd_specs(t, z_off, dt_off, True)
        + [pl.BlockSpec((D_SSM, BLK), lambda n: (0, 0)),
           pl.BlockSpec((1, SSD_GROUPS, BLK, GW), lambda n: (nb - 1 - n, 0, 0, 0)),
           rb(D_SSM)],
        out_specs=[rb(2 * D_SSM), rb(D_SSM), rb(BLK), vec(D_SSM), vec(BLK), vec(BLK), vec(BLK)],
        out_shape=[jax.ShapeDtypeStruct((t, 2 * D_SSM), F32), jax.ShapeDtypeStruct((t, D_SSM), MXU),
                   jax.ShapeDtypeStruct((t, BLK), MXU), jax.ShapeDtypeStruct((1, D_SSM), F32),
                   jax.ShapeDtypeStruct((1, BLK), F32), jax.ShapeDtypeStruct((1, BLK), F32),
                   jax.ShapeDtypeStruct((1, BLK), F32)],
        scratch_shapes=[pltpu.VMEM((SSD_GROUPS, BLK, GW), F32), pltpu.VMEM((RED_ROWS, D_SSM), F32)],
        sem=("arbitrary",), carry=carry)
    dxbc, dz, draw, dgn, ddsk, ddtb, dalog = res[:7]
    return [dxbc, dz, draw, dgn[0], ddsk[0, :SSD_HEADS], ddtb[0, :SSD_HEADS], dalog[0, :SSD_HEADS]] + res[7:]


def loss_fwd_bwd(h, target, *, name):
    t, d = h.shape
    tb = _conv_tile(t)
    ns = tb // BLK

    def kern(h_ref, *rest):
        t_refs, (loss_ref, dh_ref) = rest[:ns], rest[ns:]
        n = pl.program_id(0)
        part = jnp.zeros((1, 1), F32)
        for s in range(ns):
            rs = slice(s * BLK, (s + 1) * BLK)
            err = h_ref[rs, :] - t_refs[s][...]
            if s == 0:
                err = jnp.where(n > 0, err, 0.0)
            dh_ref[rs, :] = err * (1.0 / d)
            part = part + (0.5 / d) * jnp.sum(jnp.sum(err * err, axis=1, keepdims=True), axis=0, keepdims=True)

        @pl.when(n == 0)
        def _():
            loss_ref[...] = part

        @pl.when(n > 0)
        def _():
            loss_ref[...] += part

    return pl.pallas_call(
        kern, name=name, grid=(t // tb,),
        in_specs=[pl.BlockSpec((tb, d), lambda n: (n, 0))]
        + [pl.BlockSpec((BLK, d), lambda n, s=s: (jnp.maximum(n * ns + s - 1, 0), 0)) for s in range(ns)],
        out_specs=[pl.BlockSpec((1, 1), lambda n: (0, 0)), pl.BlockSpec((tb, d), lambda n: (n, 0))],
        out_shape=[jax.ShapeDtypeStruct((1, 1), F32), jax.ShapeDtypeStruct((t, d), F32)],
        compiler_params=_cp("arbitrary"),
    )(h, *([target] * ns))


def _ew_tile(r, c):
    cap = max(16, (256 * 1024) // c)
    best = None
    for tr in range(16, min(r, cap) + 1, 16):
        if r % tr == 0:
            best = tr
    return best if best is not None else r


def adamw(parts, w, m, v, *, name):
    npart, r, c = parts.shape
    tr = _ew_tile(r, c)

    def kern(p_ref, w_ref, m_ref, v_ref, g_ref, d_ref, m2_ref, v2_ref):
        g = p_ref[0].astype(F32)
        for k in range(1, npart):
            g = g + p_ref[k].astype(F32)
        m2 = ADAM_B1 * m_ref[...] + (1.0 - ADAM_B1) * g
        v2 = ADAM_B2 * v_ref[...] + (1.0 - ADAM_B2) * (g * g)
        m_hat = m2 / (1.0 - ADAM_B1 ** ADAM_STEP)
        v_hat = v2 / (1.0 - ADAM_B2 ** ADAM_STEP)
        g_ref[...] = g
        d_ref[...] = -ADAM_LR * (m_hat / (jnp.sqrt(v_hat) + ADAM_EPS) + ADAM_WD * w_ref[...])
        m2_ref[...] = m2
        v2_ref[...] = v2

    row = pl.BlockSpec((tr, c), lambda i: (i, 0))
    sds = jax.ShapeDtypeStruct((r, c), F32)
    return pl.pallas_call(
        kern, name=name, grid=(r // tr,),
        in_specs=[pl.BlockSpec((npart, tr, c), lambda i: (0, i, 0)), row, row, row],
        out_specs=[row, row, row, row], out_shape=[sds, sds, sds, sds],
        compiler_params=_cp("parallel"),
    )(parts, w, m, v)


def pair_add(p, land, *, name):
    _, r, c = p.shape
    tr = _ew_tile(r, c)
    core = lax.axis_index("c").astype(jnp.int32).reshape(1)

    def kern(c_ref, p_ref, l_ref, o_ref):
        o_ref[...] = (p_ref[...] + l_ref[...]).astype(o_ref.dtype)

    return pl.pallas_call(
        kern, name=name,
        grid_spec=pltpu.PrefetchScalarGridSpec(
            num_scalar_prefetch=1, grid=(4, r // tr),
            in_specs=[pl.BlockSpec((1, tr, c), lambda k, i, c_ref: (2 * k + c_ref[0], i, 0)),
                      pl.BlockSpec((1, tr, c), lambda k, i, c_ref: (k, i, 0))],
            out_specs=pl.BlockSpec((1, tr, c), lambda k, i, c_ref: (k, i, 0))),
        out_shape=jax.ShapeDtypeStruct((4, r, c), BF16),
        compiler_params=_cp("parallel", "parallel"),
    )(core, p, land)


def _me():
    return lax.axis_index("x"), lax.axis_index("y"), lax.axis_index("c")


def all_gather(xs, *, name):
    n = len(xs)

    def body(*refs):
        x_refs, out_refs = refs[:n], refs[n:2 * n]
        send_sems, recv_sems, local_sems = refs[2 * n:]
        mx, my, mc = _me()
        me, sib = (mx, my, mc), (mx, my, 1 - mc)
        chips = [(1 - mx, my), (mx, 1 - my), (1 - mx, 1 - my)]

        def rows(i, px, py, pc):
            return out_refs[i].at[4 * px + 2 * py + pc]

        def copy(i, k, block, to, src=None):
            return pltpu.make_async_remote_copy(
                src_ref=rows(i, *block) if src is None else src, dst_ref=rows(i, *block),
                send_sem=send_sems.at[7 * i + k], recv_sem=recv_sems.at[7 * i + k],
                device_id=to, device_id_type=MESH)

        mine = [pltpu.make_async_copy(x_refs[i], rows(i, *me), local_sems.at[i]) for i in range(n)]
        first = []
        for i in range(n):
            mine[i].start()
            first.append(copy(i, 0, me, sib, src=x_refs[i]))
            first += [copy(i, 1 + j, me, (*chip, mc), src=x_refs[i]) for j, chip in enumerate(chips)]
        for cp in first:
            cp.start()
        passed = []
        for i in range(n):
            for j, chip in enumerate(chips):
                copy(i, 1 + j, (*chip, mc), me).wait_recv()
                passed.append(copy(i, 4 + j, (*chip, mc), sib))
                passed[-1].start()
        for i in range(n):
            copy(i, 0, sib, me).wait_recv()
            for j, chip in enumerate(chips):
                copy(i, 4 + j, (*chip, 1 - mc), me).wait_recv()
        for cp in first + passed:
            cp.wait_send()
        for cp in mine:
            cp.wait()

    return pl.pallas_call(
        body, name=name,
        out_shape=[jax.ShapeDtypeStruct((N_DEV,) + x.shape, x.dtype) for x in xs],
        in_specs=[ANY] * n, out_specs=[ANY] * n,
        scratch_shapes=[pltpu.SemaphoreType.DMA((7 * n,)), pltpu.SemaphoreType.DMA((7 * n,)),
                        pltpu.SemaphoreType.DMA((n,))],
    )(*xs)


def pair_exchange(ps, *, name):
    n = len(ps)

    def body(*refs):
        p_refs, out_refs = refs[:n], refs[n:2 * n]
        send_sems, recv_sems = refs[2 * n:]
        mx, my, mc = _me()
        cps = [pltpu.make_async_remote_copy(
            src_ref=p_refs[i].at[2 * k + (1 - mc)], dst_ref=out_refs[i].at[k],
            send_sem=send_sems.at[4 * i + k], recv_sem=recv_sems.at[4 * i + k],
            device_id=(mx, my, 1 - mc), device_id_type=MESH) for i in range(n) for k in range(4)]
        for cp in cps:
            cp.start()
        for cp in cps:
            cp.wait_recv()
        for cp in cps:
            cp.wait_send()

    return pl.pallas_call(
        body, name=name,
        out_shape=[jax.ShapeDtypeStruct((4,) + p.shape[1:], p.dtype) for p in ps],
        in_specs=[ANY] * n, out_specs=[ANY] * n,
        scratch_shapes=[pltpu.SemaphoreType.DMA((4 * n,)), pltpu.SemaphoreType.DMA((4 * n,))],
    )(*ps)


def chip_exchange(qs, *, name):
    n = len(qs)

    def body(*refs):
        q_refs, out_refs = refs[:n], refs[n:2 * n]
        send_sems, recv_sems, local_sems = refs[2 * n:]
        mx, my, mc = _me()
        mine = 2 * mx + my
        chips = [(1 - mx, my), (mx, 1 - my), (1 - mx, 1 - my)]
        local, sends, recvs = [], [], []
        for i in range(n):
            local.append(pltpu.make_async_copy(q_refs[i].at[mine], out_refs[i].at[mine], local_sems.at[i]))
            for k, (px, py) in enumerate(chips):
                sems = dict(send_sem=send_sems.at[3 * i + k], recv_sem=recv_sems.at[3 * i + k],
                            device_id=(px, py, mc), device_id_type=MESH)
                sends.append(pltpu.make_async_remote_copy(
                    src_ref=q_refs[i].at[2 * px + py], dst_ref=out_refs[i].at[mine], **sems))
                recvs.append(pltpu.make_async_remote_copy(
                    src_ref=q_refs[i].at[mine], dst_ref=out_refs[i].at[2 * px + py], **sems))
        for cp in local + sends:
            cp.start()
        for cp in recvs:
            cp.wait_recv()
        for cp in sends:
            cp.wait_send()
        for cp in local:
            cp.wait()

    return pl.pallas_call(
        body, name=name,
        out_shape=[jax.ShapeDtypeStruct(q.shape, q.dtype) for q in qs],
        in_specs=[ANY] * n, out_specs=[ANY] * n,
        scratch_shapes=[pltpu.SemaphoreType.DMA((3 * n,)), pltpu.SemaphoreType.DMA((3 * n,)),
                        pltpu.SemaphoreType.DMA((n,))],
    )(*qs)


class _Carry:
    def __init__(self, inputs, out_shapes, sems, start, finish):
        self.inputs, self.out_shapes, self.sems = list(inputs), list(out_shapes), list(sems)
        self.start, self.finish = start, finish


def _call(kern, args, *, name, grid, in_specs, out_specs, out_shape, scratch_shapes=(), sem, carry=None):
    in_specs, out_specs, out_shape = list(in_specs), list(out_specs), list(out_shape)
    scratch_shapes = list(scratch_shapes)
    if carry is None:
        return list(pl.pallas_call(
            kern, name=name, grid=grid, in_specs=in_specs, out_specs=out_specs, out_shape=out_shape,
            scratch_shapes=scratch_shapes, compiler_params=_cp(*sem))(*args))
    ni, no, ns = len(in_specs), len(out_specs), len(scratch_shapes)
    ci, co = len(carry.inputs), len(carry.out_shapes)

    def body(*refs):
        o0 = ni + ci
        s0 = o0 + no + co
        ids = [pl.program_id(d) for d in range(len(grid))]
        first = functools.reduce(jnp.logical_and, [i == 0 for i in ids])
        last = functools.reduce(jnp.logical_and, [i == g - 1 for i, g in zip(ids, grid)])
        cin, cout, sems = refs[ni:o0], refs[o0 + no:s0], refs[s0 + ns:]

        @pl.when(first)
        def _():
            carry.start(cin, cout, sems)

        kern(*refs[:ni], *refs[o0:o0 + no], *refs[s0:s0 + ns])

        @pl.when(last)
        def _():
            carry.finish(cin, cout, sems)

    return list(pl.pallas_call(
        body, name=name, grid=grid, in_specs=in_specs + [ANY] * ci, out_specs=out_specs + [ANY] * co,
        out_shape=out_shape + carry.out_shapes, scratch_shapes=scratch_shapes + carry.sems,
        compiler_params=_cp(*(["arbitrary"] * len(grid))))(*args, *carry.inputs))


def merge_carries(cs):
    def split(seq, counts):
        out, off = [], 0
        for k in counts:
            out.append(seq[off:off + k])
            off += k
        return out

    def parts(cin, cout, sems):
        return zip(cs, split(cin, [len(c.inputs) for c in cs]), split(cout, [len(c.out_shapes) for c in cs]),
                   split(sems, [len(c.sems) for c in cs]))

    def start(cin, cout, sems):
        for c, i, o, s in parts(cin, cout, sems):
            c.start(i, o, s)

    def finish(cin, cout, sems):
        for c, i, o, s in parts(cin, cout, sems):
            c.finish(i, o, s)

    return _Carry(sum((c.inputs for c in cs), []), sum((c.out_shapes for c in cs), []),
                  sum((c.sems for c in cs), []), start, finish)


def gather_carry(xs):
    n = len(xs)

    def copies(cin, cout, sems, with_recv=True):
        mx, my, mc = _me()
        me = 4 * mx + 2 * my + mc
        peers = [(mx, my, 1 - mc), (1 - mx, my, mc), (mx, 1 - my, mc), (1 - mx, 1 - my, mc)]
        local, send, recv = [], [], []
        for i in range(n):
            local.append(pltpu.make_async_copy(cin[i], cout[i].at[me], sems[2].at[i]))
            for k, peer in enumerate(peers):
                common = dict(send_sem=sems[0].at[4 * i + k], recv_sem=sems[1].at[4 * i + k],
                              device_id=peer, device_id_type=MESH)
                send.append(pltpu.make_async_remote_copy(src_ref=cin[i], dst_ref=cout[i].at[me], **common))
                if with_recv:
                    recv.append(pltpu.make_async_remote_copy(
                        src_ref=cin[i], dst_ref=cout[i].at[4 * peer[0] + 2 * peer[1] + peer[2]], **common))
        return local, send, recv

    def start(cin, cout, sems):
        local, send, _ = copies(cin, cout, sems, with_recv=False)
        for cp in local + send:
            cp.start()

    def finish(cin, cout, sems):
        local, send, recv = copies(cin, cout, sems)
        for cp in recv:
            cp.wait_recv()
        for cp in send:
            cp.wait_send()
        for cp in local:
            cp.wait()

    return _Carry(xs, [jax.ShapeDtypeStruct((N_DEV,) + x.shape, x.dtype) for x in xs],
                  [pltpu.SemaphoreType.DMA((4 * n,)), pltpu.SemaphoreType.DMA((4 * n,)),
                   pltpu.SemaphoreType.DMA((n,))], start, finish)


def gather_relay(outs, *, name):
    n = len(outs)

    def body(*refs):
        bufs = refs[n:2 * n]
        send_sems, recv_sems = refs[2 * n:]
        mx, my, mc = _me()
        chips = [(1 - mx, my), (mx, 1 - my), (1 - mx, 1 - my)]
        send, recv = [], []
        for i in range(n):
            for j, (px, py) in enumerate(chips):
                common = dict(send_sem=send_sems.at[3 * i + j], recv_sem=recv_sems.at[3 * i + j],
                              device_id=(mx, my, 1 - mc), device_id_type=MESH)
                mine = bufs[i].at[4 * px + 2 * py + mc]
                send.append(pltpu.make_async_remote_copy(src_ref=mine, dst_ref=mine, **common))
                recv.append(pltpu.make_async_remote_copy(
                    src_ref=mine, dst_ref=bufs[i].at[4 * px + 2 * py + (1 - mc)], **common))
        for cp in send:
            cp.start()
        for cp in recv:
            cp.wait_recv()
        for cp in send:
            cp.wait_send()

    return pl.pallas_call(
        body, name=name, out_shape=[jax.ShapeDtypeStruct(o.shape, o.dtype) for o in outs],
        in_specs=[ANY] * n, out_specs=[ANY] * n, input_output_aliases={i: i for i in range(n)},
        scratch_shapes=[pltpu.SemaphoreType.DMA((3 * n,)), pltpu.SemaphoreType.DMA((3 * n,))],
    )(*outs)


def pair_carry(ps):
    n = len(ps)

    def copies(cin, cout, sems):
        mx, my, mc = _me()
        return [pltpu.make_async_remote_copy(
            src_ref=cin[i].at[2 * k + (1 - mc)], dst_ref=cout[i].at[k],
            send_sem=sems[0].at[4 * i + k], recv_sem=sems[1].at[4 * i + k],
            device_id=(mx, my, 1 - mc), device_id_type=MESH) for i in range(n) for k in range(4)]

    def start(cin, cout, sems):
        for cp in copies(cin, cout, sems):
            cp.start()

    def finish(cin, cout, sems):
        cps = copies(cin, cout, sems)
        for cp in cps:
            cp.wait_recv()
        for cp in cps:
            cp.wait_send()

    return _Carry(ps, [jax.ShapeDtypeStruct((4,) + p.shape[1:], p.dtype) for p in ps],
                  [pltpu.SemaphoreType.DMA((4 * n,)), pltpu.SemaphoreType.DMA((4 * n,))], start, finish)


def chip_carry(qs):
    n = len(qs)

    def copies(cin, cout, sems, with_recv=True):
        mx, my, mc = _me()
        mine = 2 * mx + my
        chips = [(1 - mx, my), (mx, 1 - my), (1 - mx, 1 - my)]
        local, send, recv = [], [], []
        for i in range(n):
            local.append(pltpu.make_async_copy(cin[i].at[mine], cout[i].at[mine], sems[2].at[i]))
            for k, (px, py) in enumerate(chips):
                common = dict(send_sem=sems[0].at[3 * i + k], recv_sem=sems[1].at[3 * i + k],
                              device_id=(px, py, mc), device_id_type=MESH)
                send.append(pltpu.make_async_remote_copy(
                    src_ref=cin[i].at[2 * px + py], dst_ref=cout[i].at[mine], **common))
                if with_recv:
                    recv.append(pltpu.make_async_remote_copy(
                        src_ref=cin[i].at[mine], dst_ref=cout[i].at[2 * px + py], **common))
        return local, send, recv

    def start(cin, cout, sems):
        local, send, _ = copies(cin, cout, sems, with_recv=False)
        for cp in local + send:
            cp.start()

    def finish(cin, cout, sems):
        local, send, recv = copies(cin, cout, sems)
        for cp in recv:
            cp.wait_recv()
        for cp in send:
            cp.wait_send()
        for cp in local:
            cp.wait()

    return _Carry(qs, [jax.ShapeDtypeStruct(q.shape, q.dtype) for q in qs],
                  [pltpu.SemaphoreType.DMA((3 * n,)), pltpu.SemaphoreType.DMA((3 * n,)),
                   pltpu.SemaphoreType.DMA((n,))], start, finish)


WEIGHTS = [
    "meta_tokens", "l0_mix_pre_norm", "l0_mix_post_norm", "l0_w_in", "l0_lru_conv_w", "l0_lru_conv_b",
    "l0_lru_w_a", "l0_lru_b_a", "l0_lru_w_x", "l0_lru_b_x", "l0_lru_lambda", "l0_attn_sinks", "l0_w_out",
    "l0_ffn_pre_norm", "l0_ffn_post_norm", "l0_ffn_w_up", "l0_ffn_conv_w", "l0_ffn_conv_b", "l0_ffn_w_down",
    "l1_mix_pre_norm", "l1_mix_post_norm", "l1_w_in", "l1_ssm_conv_w", "l1_ssm_conv_b", "l1_dt_bias",
    "l1_a_log", "l1_d_skip", "l1_gate_norm", "l1_w_out", "l1_ffn_pre_norm", "l1_ffn_post_norm",
    "l1_ffn_w_up", "l1_ffn_conv_w", "l1_ffn_conv_b", "l1_ffn_w_down",
]
INPUTS = (["x"] + WEIGHTS + ["loss_target"] + ["m_" + n for n in WEIGHTS] + ["v_" + n for n in WEIGHTS])

MATS = {"l0_w_in": ("col", (1024, 3328)), "l0_w_out": ("row", (2048, 1024)),
        "l0_ffn_w_up": ("col", (1024, 5632)), "l0_ffn_w_down": ("row", (2816, 1024)),
        "l1_w_in": ("col", (1024, 6176)), "l1_w_out": ("row", (2048, 1024)),
        "l1_ffn_w_up": ("col", (1024, 5632)), "l1_ffn_w_down": ("row", (2816, 1024))}
SMALL_SHARDED = {"meta_tokens": ("col", (16, 1024)), "l0_lru_conv_w": ("col", (4, 1024)),
                 "l0_ffn_conv_w": ("col", (3, 5632)), "l1_ssm_conv_w": ("col", (4, 4096)),
                 "l1_ffn_conv_w": ("col", (3, 5632))}
SHARDED = {**MATS, **SMALL_SHARDED}
REPLICATED = [n for n in WEIGHTS if n not in SHARDED]
SHAPES = {n: ((8, BLK, BLK) if n.endswith(("lru_w_a", "lru_w_x")) else (N_Q_HEADS,) if n.endswith("attn_sinks")
              else (2 * D_FF,) if n.endswith("ffn_conv_b") else (2 * D_SSM,) if n.endswith("ssm_conv_b")
              else (SSD_HEADS,) if n.endswith(("dt_bias", "a_log", "d_skip")) else (D_SSM,) if n.endswith("gate_norm")
              else (D_MODEL,)) for n in REPLICATED}
PACK_W = 1024
SMALL_W = 128


def _shard_shape(name):
    kind, (r, c) = SHARDED[name]
    return (r, c // N_DEV) if kind == "col" else (r // N_DEV, c)


def _rows_of(numel, width):
    return -(-numel // width)


def _to_rows(a, width):
    flat = a.reshape(-1)
    rows = _rows_of(flat.shape[0], width)
    return jnp.pad(flat, (0, rows * width - flat.shape[0])).reshape(rows, width)


def _pack(arrs, width, total_rows):
    slab = jnp.concatenate([_to_rows(a, width) for a in arrs], axis=0)
    return jnp.pad(slab, ((0, total_rows - slab.shape[0]), (0, 0)))


def _unpack(slab, shapes, width):
    out, off = [], 0
    for shp in shapes:
        numel = math.prod(shp)
        rows = _rows_of(numel, width)
        out.append(slab[off:off + rows].reshape(-1)[:numel].reshape(shp))
        off += rows
    return out


def _round_up(n, m):
    return -(-n // m) * m


def _by_dest(name, g):
    kind, (r, c) = SHARDED[name]
    if kind == "col":
        return g.reshape(r, N_DEV, c // N_DEV).transpose(1, 0, 2)
    return g.reshape(N_DEV, r // N_DEV, c)


def _from_shards(name, blocks):
    kind, (r, c) = SHARDED[name]
    return blocks.transpose(1, 0, 2).reshape(r, c) if kind == "col" else blocks.reshape(r, c)


L1_IN_PAD = 6272
FFN_CBLK = 1408
SSM_CBLK = 1024


def _ffn_fwd(h, a, w, pfx):
    u, ut = rmsnorm_fwd(h, a[pfx + "ffn_pre_norm"], out_dtype=MXU, name=pfx + "ffn_pre", with_t=True)
    up = matmul(u, w[pfx + "ffn_w_up"], name=pfx + "ffn_up")
    act, act_t = dwconv_fwd(up, a[pfx + "ffn_conv_w"], a[pfx + "ffn_conv_b"], mode="geglu", x_off=0,
                            c_out=D_FF, cblk=FFN_CBLK, out_dtype=MXU, name=pfx + "ffn_act", with_t=True)
    down = matmul(act, w[pfx + "ffn_w_down"], name=pfx + "ffn_down")
    out = rmsnorm_fwd(down, a[pfx + "ffn_post_norm"], res=h, out_dtype=F32, name=pfx + "ffn_post")
    return out, (h, ut, up, act_t, down)


def _dx_and_pair_stage(names, g, a_list, b, *, name):
    parts = [_by_dest(n, g[n]) for n in names]
    out, from_sibling = matmul_cat(a_list, b, trans_b=True, name=name, carry=pair_carry(parts))
    return out, [pair_add(p, l, name="rs_pair_add_" + n) for n, p, l in zip(names, parts, from_sibling)]


def _ffn_bwd(dh, saved, a, w, pfx, g, carry=None):
    h, ut, up, act_t, down = saved
    dd, g[pfx + "ffn_post_norm"] = rmsnorm_bwd(down, a[pfx + "ffn_post_norm"], dh, out_dtype=MXU,
                                               name=pfx + "ffn_post_bwd")
    dact = matmul(dd, w[pfx + "ffn_w_down"], trans_b=True, name=pfx + "ffn_down_dx")
    g[pfx + "ffn_w_down"] = matmul(act_t, dd, name=pfx + "ffn_down_dw")
    dups, g[pfx + "ffn_conv_w"], g[pfx + "ffn_conv_b"], carried = dwconv_bwd(
        up, a[pfx + "ffn_conv_w"], a[pfx + "ffn_conv_b"], dact, mode="geglu", x_off=0, c_out=D_FF,
        cblk=FFN_CBLK, name=pfx + "ffn_act_bwd", carry=carry)
    g[pfx + "ffn_w_up"] = jnp.concatenate(
        [matmul(ut, d, name=pfx + "ffn_up_dw%d" % i) for i, d in enumerate(dups)], axis=1)
    du, q = _dx_and_pair_stage([pfx + "ffn_w_down", pfx + "ffn_w_up"], g, dups, w[pfx + "ffn_w_up"],
                               name=pfx + "ffn_up_dx")
    dh_in, g[pfx + "ffn_pre_norm"] = rmsnorm_bwd(h, a[pfx + "ffn_pre_norm"], du, res=dh, out_dtype=F32,
                                                 name=pfx + "ffn_pre_bwd")
    return dh_in, carried, q


GATHER_EARLY = ["l0_w_out", "l0_ffn_w_up", "l0_ffn_w_down"]
GATHER_LATE = ["l1_w_in", "l1_w_out", "l1_ffn_w_up", "l1_ffn_w_down"]
RS_L1_FFN = ["l1_ffn_w_down", "l1_ffn_w_up"]
RS_L1_MIX = ["l1_w_out", "l1_w_in"]
RS_L0_FFN = ["l0_ffn_w_down", "l0_ffn_w_up"]
RS_LAST = ["l0_w_in", "l0_lru_conv_w", "l0_ffn_conv_w", "l1_ssm_conv_w", "l1_ffn_conv_w"]


REPL_LATE = ["l0_attn_sinks", "l0_mix_pre_norm"]
REPL_EARLY = [n for n in REPLICATED if n not in REPL_LATE]


def _local_step(a, shards):
    x = a["x"][0]
    seq = x.shape[0]
    h0 = jnp.concatenate([jnp.zeros((PAD, D_MODEL), F32), a["meta_tokens"], x], axis=0)
    g, landed = {}, {}

    u0, u0t, w_in0 = rmsnorm_fwd(h0, a["l0_mix_pre_norm"], out_dtype=MXU, name="l0_mix_pre", with_t=True,
                                 carry=gather_carry([shards["l0_w_in"]]))
    w = {"l0_w_in": _from_shards("l0_w_in", gather_relay([w_in0], name="gather_relay_first")[0])}
    proj0 = matmul(u0, w["l0_w_in"], name="l0_in")
    lru = (a["l0_lru_conv_w"], a["l0_lru_conv_b"], a["l0_lru_w_a"], a["l0_lru_b_a"], a["l0_lru_w_x"],
           a["l0_lru_b_x"], a["l0_lru_lambda"])
    ya, ya_t, hl, *early = lru_fwd(proj0, *lru, gate_off=0, xr_off=1024, name="l0_lru",
                                   carry=gather_carry([shards[n] for n in GATHER_EARLY]))
    yb, *late = attn_fwd(proj0, a["l0_attn_sinks"], q_off=2048, k_off=3072, v_off=3200, name="l0_attn",
                         carry=gather_carry([shards[n] for n in GATHER_LATE]))
    relayed = gather_relay(early + late, name="gather_relay")
    w = dict(w, **{n: _from_shards(n, blocks) for n, blocks in zip(GATHER_EARLY + GATHER_LATE, relayed)})
    w["l1_w_in"] = jnp.pad(w["l1_w_in"], ((0, 0), (0, L1_IN_PAD - w["l1_w_in"].shape[1])))
    o0 = matmul_cat([ya, yb], w["l0_w_out"], name="l0_out")
    h1 = rmsnorm_fwd(o0, a["l0_mix_post_norm"], res=h0, out_dtype=F32, name="l0_mix_post")
    h2, ffn0 = _ffn_fwd(h1, a, w, "l0_")

    u2, u2t = rmsnorm_fwd(h2, a["l1_mix_pre_norm"], out_dtype=MXU, name="l1_mix_pre", with_t=True)
    proj1 = matmul(u2, w["l1_w_in"], name="l1_in")
    xbc = dwconv_fwd(proj1, a["l1_ssm_conv_w"], a["l1_ssm_conv_b"], mode="silu", x_off=D_SSM,
                     c_out=2 * D_SSM, cblk=SSM_CBLK, out_dtype=F32, name="l1_ssm_conv")
    ssd = (a["l1_dt_bias"], a["l1_a_log"], a["l1_d_skip"], a["l1_gate_norm"])
    decay = ssd_decay(proj1, a["l1_dt_bias"], a["l1_a_log"], dt_off=3 * D_SSM, name="l1_ssd_decay")
    yn, yn_t, st = ssd_fwd(xbc, proj1, decay, *ssd, z_off=0, dt_off=3 * D_SSM, name="l1_ssd")
    o1 = matmul(yn, w["l1_w_out"], name="l1_out")
    h3 = rmsnorm_fwd(o1, a["l1_mix_post_norm"], res=h2, out_dtype=F32, name="l1_mix_post")
    h4, ffn1 = _ffn_fwd(h3, a, w, "l1_")

    loss, dh4 = loss_fwd_bwd(h4, a["loss_target"][0], name="loss")

    dh3, _, q_l1_ffn = _ffn_bwd(dh4, ffn1, a, w, "l1_", g)
    do1, g["l1_mix_post_norm"] = rmsnorm_bwd(o1, a["l1_mix_post_norm"], dh3, out_dtype=MXU,
                                             name="l1_mix_post_bwd")
    dyn = matmul(do1, w["l1_w_out"], trans_b=True, name="l1_out_dx")
    g["l1_w_out"] = matmul(yn_t, do1, name="l1_out_dw")
    (dxbc, dz, draw, g["l1_gate_norm"], g["l1_d_skip"], g["l1_dt_bias"], g["l1_a_log"], *got) = ssd_bwd(
        xbc, proj1, decay, st, dyn, *ssd, z_off=0, dt_off=3 * D_SSM, name="l1_ssd_bwd",
        carry=chip_carry(q_l1_ffn))
    landed.update(zip(RS_L1_FFN, got))
    (dxin,), g["l1_ssm_conv_w"], g["l1_ssm_conv_b"], _ = dwconv_bwd(
        proj1, a["l1_ssm_conv_w"], a["l1_ssm_conv_b"], dxbc, mode="silu", x_off=D_SSM,
        c_out=2 * D_SSM, cblk=SSM_CBLK, name="l1_ssm_conv_bwd")
    g["l1_w_in"] = jnp.concatenate(
        [matmul(u2t, dz, name="l1_in_dw_z"), matmul(u2t, dxin, name="l1_in_dw_x"),
         matmul(u2t, draw, name="l1_in_dw_dt")[:, :SSD_HEADS]], axis=1)
    du2, q_l1_mix = _dx_and_pair_stage(RS_L1_MIX, g, [dz, dxin, draw], w["l1_w_in"], name="l1_in_dx")
    dh2, g["l1_mix_pre_norm"] = rmsnorm_bwd(h2, a["l1_mix_pre_norm"], du2, res=dh3, out_dtype=F32,
                                            name="l1_mix_pre_bwd")

    dh1, got, q_l0_ffn = _ffn_bwd(dh2, ffn0, a, w, "l0_", g, carry=chip_carry(q_l1_mix))
    landed.update(zip(RS_L1_MIX, got))
    do0, g["l0_mix_post_norm"] = rmsnorm_bwd(o0, a["l0_mix_post_norm"], dh1, out_dtype=MXU,
                                             name="l0_mix_post_bwd")
    g["l0_w_out"] = jnp.concatenate([matmul(ya_t, do0, name="l0_out_dw_a"),
                                     matmul(yb.T, do0, name="l0_out_dw_b")], axis=0)
    dy, q_out = _dx_and_pair_stage(["l0_w_out"], g, [do0], w["l0_w_out"], name="l0_out_dx")
    (dgate, dxr, g["l0_lru_conv_w"], dcb, g["l0_lru_w_a"], dba, g["l0_lru_w_x"], dbx, dlam, *got) = lru_bwd(
        proj0, hl, dy, *lru, gate_off=0, xr_off=1024, dy_off=0, name="l0_lru_bwd", carry=chip_carry(q_out))
    landed["l0_w_out"] = got[0]
    g["l0_lru_conv_b"], g["l0_lru_b_a"], g["l0_lru_b_x"], g["l0_lru_lambda"] = dcb[0], dba[0], dbx[0], dlam[0]
    dq, dk, dv, g["l0_attn_sinks"], *got = attn_bwd(
        proj0, a["l0_attn_sinks"], dy, q_off=2048, k_off=3072, v_off=3200, dy_off=1024, name="l0_attn_bwd",
        carry=merge_carries([chip_carry(q_l0_ffn), gather_carry([_pack_repl(g, REPL_EARLY)])]))
    landed.update(zip(RS_L0_FFN, got[:2]))
    repl_early = gather_relay(got[2:], name="gather_relay_small_grads")[0]
    dproj0 = [dgate, dxr, dq, dk, dv]
    g["l0_w_in"] = jnp.concatenate(
        [matmul(u0t, d, name="l0_in_dw%d" % i) for i, d in enumerate(dproj0)], axis=1)
    du0, q_last = _dx_and_pair_stage(RS_LAST, g, dproj0, w["l0_w_in"], name="l0_in_dx")
    dh0, g["l0_mix_pre_norm"], *got = rmsnorm_bwd(h0, a["l0_mix_pre_norm"], du0, res=dh1, out_dtype=F32,
                                                  name="l0_mix_pre_bwd", carry=chip_carry(q_last))
    landed.update(zip(RS_LAST, got))
    g["meta_tokens"] = dh0[PAD:BLK]
    meta = _by_dest("meta_tokens", g["meta_tokens"])
    q_meta = pair_add(meta, pair_exchange([meta], name="rs_pair_meta")[0], name="rs_pair_add_meta_tokens")
    landed["meta_tokens"] = chip_exchange([q_meta], name="rs_chip_meta")[0]
    return loss[0, 0], dh0[BLK:].reshape(1, seq, D_MODEL), g, landed, repl_early


def _repl_rows(names):
    return _round_up(sum(_rows_of(math.prod(SHAPES[n]), SMALL_W) for n in names), 16)


def _pack_repl(vals, names):
    return _pack([vals[n] for n in names], SMALL_W, _repl_rows(names))


def kernel(*args):
    a = dict(zip(INPUTS, args))
    first = list(SMALL_SHARDED)
    full = {n: _from_shards(n, blocks) for n, blocks in
            zip(first, all_gather([a[n] for n in first], name="gather_first"))}
    shards = {n: a[n].astype(MXU) for n in MATS}
    loss_part, grad_x, g, landed, repl_early = _local_step({**a, **full}, shards)
    loss = lax.psum(loss_part, ("x", "y", "c"))

    sh_out = {n: adamw(landed[n], a[n], a["m_" + n], a["v_" + n], name="adamw_" + n) for n in SHARDED}

    repl_late = all_gather([_pack_repl(g, REPL_LATE)], name="gather_small_grads")[0]
    rp_out = [{}, {}, {}, {}]
    for names, parts in ((REPL_EARLY, repl_early), (REPL_LATE, repl_late)):
        res = adamw(parts, *[_pack_repl({n: a[p + n] for n in names}, names) for p in ("", "m_", "v_")],
                    name="adamw_replicated_%d" % len(names))
        for k in range(4):
            rp_out[k].update(zip(names, _unpack(res[k], [SHAPES[n] for n in names], SMALL_W)))

    outs = [loss, grad_x]
    for k in range(4):
        outs += [sh_out[n][k] if n in SHARDED else rp_out[k][n] for n in WEIGHTS]
    return tuple(outs)
```

```python
import functools
import math

import jax
import jax.numpy as jnp
import numpy as np
from jax import lax
from jax.experimental import pallas as pl
from jax.experimental.pallas import tpu as pltpu

F32 = jnp.float32
BF16 = jnp.bfloat16
MXU = jnp.bfloat16

D_MODEL = 1024
N_META = 16
BLK = 128
PAD = BLK - N_META
D_RNN = 1024
LRU_C = 8.0
N_Q_HEADS = 16
HEAD_DIM = 64
D_SSM = 2048
SSD_HEADS = 32
SSD_GROUPS = 8
D_FF = 2816
EPS = 1e-6
NEG = -1e30
N_DEV = 8

ADAM_LR = 0.001
ADAM_B1 = 0.9
ADAM_B2 = 0.999
ADAM_EPS = 1e-08
ADAM_WD = 0.01
ADAM_STEP = 10

VMEM_LIMIT = 56 * 1024 * 1024
MESH = pl.DeviceIdType.MESH
ANY = pl.BlockSpec(memory_space=pl.ANY)


def _cp(*sem):
    return pltpu.CompilerParams(dimension_semantics=sem, vmem_limit_bytes=VMEM_LIMIT)


def _pick(n, cands):
    for c in cands:
        if n % c == 0:
            return c
    return n


def _dot(a, b):
    return jnp.dot(a.astype(MXU), b.astype(MXU), preferred_element_type=F32)


def _dot_nt(a, b):
    return lax.dot_general(a.astype(MXU), b.astype(MXU), (((1,), (1,)), ((), ())),
                           preferred_element_type=F32)


def _dot_tn(a, b):
    return jnp.dot(a.T.astype(MXU), b.astype(MXU), preferred_element_type=F32)


def _sigmoid(x):
    return 1.0 / (1.0 + jnp.exp(-x))


def _log1p(x):
    u = 1.0 + x
    return jnp.where(u == 1.0, x, jnp.log(u) * (x / jnp.where(u == 1.0, 1.0, u - 1.0)))


def _expm1(x):
    u = jnp.exp(x)
    um1 = u - 1.0
    lg = jnp.log(jnp.where(u > 0.0, u, 1.0))
    safe = (um1 != 0.0) & (um1 != -1.0)
    return jnp.where(um1 == 0.0, x, jnp.where(um1 == -1.0, -1.0,
                                               um1 * (x / jnp.where(safe, lg, 1.0))))


def _softplus(x):
    return jnp.maximum(x, 0.0) + _log1p(jnp.exp(-jnp.abs(x)))


_GC = math.sqrt(2.0 / math.pi)


def _gelu(x):
    t = jnp.tanh(_GC * (x + 0.044715 * x * x * x))
    return 0.5 * x * (1.0 + t)


def _gelu_grad(x):
    t = jnp.tanh(_GC * (x + 0.044715 * x * x * x))
    return 0.5 * (1.0 + t) + 0.5 * x * (1.0 - t * t) * (_GC * (1.0 + 3.0 * 0.044715 * x * x))


def _silu(x):
    return x * _sigmoid(x)


def _silu_grad(x):
    s = _sigmoid(x)
    return s * (1.0 + x * (1.0 - s))


def _rows(shape):
    return lax.broadcasted_iota(jnp.int32, shape, 0)


def _lanes(shape):
    return lax.broadcasted_iota(jnp.int32, shape, 1)


def _shift_down(x, tail, d):
    if d == 0:
        return x
    n = x.shape[0]
    xr = pltpu.roll(x, d, 0)
    tr = pltpu.roll(tail, d, 0)
    first = jnp.where(_rows(tr.shape) < d, tr, xr[0:8])
    return jnp.concatenate([first, xr[8:n]], axis=0)


def _shift_up(x, head, d):
    if d == 0:
        return x
    n = x.shape[0]
    xr = pltpu.roll(x, n - d, 0)
    hr = pltpu.roll(head, 8 - d, 0)
    last = jnp.where(_rows(hr.shape) >= 8 - d, hr, xr[n - 8:n])
    return jnp.concatenate([xr[0:n - 8], last], axis=0)


def _keep(x, valid, s):
    return jnp.where(valid, x, 0.0) if s == 0 else x


def _row_at(x, i):
    return jnp.sum(jnp.where(_rows(x.shape) == i, x, 0.0), axis=0, keepdims=True)


def _scan_fwd(a, u):
    n = a.shape[0]
    ri = _rows(a.shape)
    d = 1
    while d < n:
        m = ri >= d
        us = jnp.where(m, pltpu.roll(u, d, 0), 0.0)
        as_ = jnp.where(m, pltpu.roll(a, d, 0), 1.0)
        u = u + a * us
        a = a * as_
        d *= 2
    return a, u


def _scan_rev(c, u):
    n = c.shape[0]
    ri = _rows(c.shape)
    d = 1
    while d < n:
        m = ri < n - d
        us = jnp.where(m, pltpu.roll(u, n - d, 0), 0.0)
        cs = jnp.where(m, pltpu.roll(c, n - d, 0), 1.0)
        u = u + c * us
        c = c * cs
        d *= 2
    return c, u


def _cumsum_fwd(x):
    n = x.shape[0]
    ri = _rows(x.shape)
    d = 1
    while d < n:
        x = x + jnp.where(ri >= d, pltpu.roll(x, d, 0), 0.0)
        d *= 2
    return x


def _cumsum_rev(x):
    n = x.shape[0]
    ri = _rows(x.shape)
    d = 1
    while d < n:
        x = x + jnp.where(ri < n - d, pltpu.roll(x, n - d, 0), 0.0)
        d *= 2
    return x


MATMUL_VMEM = 40 * 1024 * 1024


def _matmul_tiles(m, n, k, tk, out_bytes):
    best = None
    for tm in (1664, 1408, 1040, 1024, 832, 640, 512, 384, 256, 128):
        if m % tm:
            continue
        for tn in (2048, 1664, 1408, 1024, 896, 640, 512, 384, 256, 128):
            if n % tn:
                continue
            vmem = 2 * (tm * tk * 2 + tk * tn * 2 + tm * tn * out_bytes) + (tm * tn * 4 if k > tk else 0)
            if vmem > MATMUL_VMEM:
                continue
            traffic = (n // tn) * m * k * 2 + (m // tm) * k * n * 2
            if best is None or traffic < best[0]:
                best = (traffic, tm, tn)
    return (best[1], best[2]) if best else (m, n)


def matmul(a, b, *, trans_b=False, out_dtype=F32, name):
    m, k = a.shape
    n = b.shape[0] if trans_b else b.shape[1]
    tk = k if k <= 2048 else _pick(k, (1664, 1408, 1024, 896, 512, 256, 128))
    nk = k // tk
    tm, tn = _matmul_tiles(m, n, k, tk, jnp.dtype(out_dtype).itemsize)

    def product(a_ref, b_ref):
        return _dot_nt(a_ref[...], b_ref[...]) if trans_b else _dot(a_ref[...], b_ref[...])

    def kern_once(a_ref, b_ref, o_ref):
        o_ref[...] = product(a_ref, b_ref).astype(o_ref.dtype)

    def kern_acc(a_ref, b_ref, o_ref, acc_ref):
        kk = pl.program_id(2)

        @pl.when(kk == 0)
        def _():
            acc_ref[...] = product(a_ref, b_ref)

        @pl.when(kk > 0)
        def _():
            acc_ref[...] += product(a_ref, b_ref)

        @pl.when(kk == nk - 1)
        def _():
            o_ref[...] = acc_ref[...].astype(o_ref.dtype)

    b_spec = (pl.BlockSpec((tn, tk), lambda i, j, kk: (j, kk)) if trans_b
              else pl.BlockSpec((tk, tn), lambda i, j, kk: (kk, j)))
    return pl.pallas_call(
        kern_once if nk == 1 else kern_acc, name=name,
        grid=(m // tm, n // tn, nk),
        in_specs=[pl.BlockSpec((tm, tk), lambda i, j, kk: (i, kk)), b_spec],
        out_specs=pl.BlockSpec((tm, tn), lambda i, j, kk: (i, j)),
        out_shape=jax.ShapeDtypeStruct((m, n), out_dtype),
        scratch_shapes=[] if nk == 1 else [pltpu.VMEM((tm, tn), F32)],
        compiler_params=_cp("parallel", "parallel", "arbitrary"),
    )(a, b)


def matmul_cat(a_list, b, *, trans_b=False, out_dtype=F32, name, carry=None):
    m = a_list[0].shape[0]
    ks = [x.shape[1] for x in a_list]
    ktot = sum(ks)
    n = b.shape[0] if trans_b else b.shape[1]
    tn = _pick(n, (512, 256, 128))
    tm = next((c for c in (1664, 1040, 832, 640, 512, 384, 256, 128)
               if m % c == 0 and c * ktot * 2 <= 8 * 1024 * 1024), m)
    na = len(a_list)

    def kern(*refs):
        b_ref, o_ref = refs[na], refs[na + 1]
        acc, off = None, 0
        for a_ref, k in zip(refs[:na], ks):
            if trans_b:
                part = _dot_nt(a_ref[...], b_ref[:, off:off + k])
            else:
                part = _dot(a_ref[...], b_ref[off:off + k, :])
            acc = part if acc is None else acc + part
            off += k
        o_ref[...] = acc.astype(o_ref.dtype)

    b_spec = (pl.BlockSpec((tn, ktot), lambda i, j: (j, 0)) if trans_b
              else pl.BlockSpec((ktot, tn), lambda i, j: (0, j)))
    res = _call(
        kern, (*a_list, b), name=name, grid=(m // tm, n // tn),
        in_specs=[pl.BlockSpec((tm, k), lambda i, j: (i, 0)) for k in ks] + [b_spec],
        out_specs=[pl.BlockSpec((tm, tn), lambda i, j: (i, j))],
        out_shape=[jax.ShapeDtypeStruct((m, n), out_dtype)],
        sem=("parallel", "parallel"), carry=carry)
    return res[0] if carry is None else (res[0], res[1:])


def _row_tile(t):
    return _pick(t, (832, 640, 512, 384, 256, 128))


def rmsnorm_fwd(x, w, res=None, *, out_dtype, name, with_t=False, carry=None):
    t, d = x.shape
    tr = _conv_tile(t) if with_t else _row_tile(t)

    def kern(*refs):
        x_ref, w_ref = refs[0], refs[1]
        o_ref = refs[-2] if with_t else refs[-1]
        xv = x_ref[...]
        r = lax.rsqrt(jnp.mean(xv * xv, axis=-1, keepdims=True) + EPS)
        y = xv * r * w_ref[...]
        if res is not None:
            y = refs[2][...] + y
        o_ref[...] = y.astype(o_ref.dtype)
        if with_t:
            refs[-1][...] = y.T.astype(o_ref.dtype)

    row = pl.BlockSpec((tr, d), lambda i: (i, 0))
    vec = pl.BlockSpec((1, d), lambda i: (0, 0))
    ins = [x, w.reshape(1, d)] + ([] if res is None else [res])
    specs = [row, vec] + ([] if res is None else [row])
    out_specs, out_shape = [row], [jax.ShapeDtypeStruct((t, d), out_dtype)]
    if with_t:
        out_specs.append(pl.BlockSpec((d, tr), lambda i: (0, i)))
        out_shape.append(jax.ShapeDtypeStruct((d, t), out_dtype))
    res_ = _call(kern, ins, name=name, grid=(t // tr,), in_specs=specs, out_specs=out_specs,
                 out_shape=out_shape, sem=("parallel",), carry=carry)
    return res_[0] if len(res_) == 1 else res_


def rmsnorm_bwd(x, w, dy, res=None, *, out_dtype, name, carry=None):
    t, d = x.shape
    tr = _row_tile(t)

    def kern(*refs):
        if res is None:
            x_ref, w_ref, dy_ref, dx_ref, dw_ref = refs
        else:
            x_ref, w_ref, dy_ref, r_ref, dx_ref, dw_ref = refs
        i = pl.program_id(0)
        xv = x_ref[...]
        dyv = dy_ref[...].astype(F32)
        r = lax.rsqrt(jnp.mean(xv * xv, axis=-1, keepdims=True) + EPS)
        xh = xv * r
        g = dyv * w_ref[...]
        dx = r * (g - xh * jnp.mean(g * xh, axis=-1, keepdims=True))
        if res is not None:
            dx = r_ref[...] + dx
        dx_ref[...] = dx.astype(dx_ref.dtype)
        part = jnp.sum(dyv * xh, axis=0, keepdims=True)

        @pl.when(i == 0)
        def _():
            dw_ref[...] = part

        @pl.when(i > 0)
        def _():
            dw_ref[...] += part

    row = pl.BlockSpec((tr, d), lambda i: (i, 0))
    vec = pl.BlockSpec((1, d), lambda i: (0, 0))
    ins = [x, w.reshape(1, d), dy] + ([] if res is None else [res])
    specs = [row, vec, row] + ([] if res is None else [row])
    return _call(kern, ins, name=name, grid=(t // tr,), in_specs=specs, out_specs=[row, vec],
                 out_shape=[jax.ShapeDtypeStruct((t, d), out_dtype), jax.ShapeDtypeStruct((1, d), F32)],
                 sem=("arbitrary",), carry=carry)


def _conv_tile(t):
    return _pick(t, (640, 384, 256, 128))


def _conv_apply(x, tail, cw, cb, ksz):
    y = cb
    for k in range(ksz):
        y = y + cw[k:k + 1, :] * _shift_down(x, tail, ksz - 1 - k)
    return y


def dwconv_fwd(x, cw, cb, *, mode, x_off, c_out, cblk, out_dtype, name, with_t=False):
    t = x.shape[0]
    ksz = cw.shape[0]
    tb = _conv_tile(t)
    nb, ncb, t8 = t // tb, c_out // cblk, tb // 8
    xo = x_off // cblk
    nin = 2 if mode == "geglu" else 1

    def kern(*refs):
        o_ref = refs[-2] if with_t else refs[-1]
        n = pl.program_id(1)
        for c in range(cblk // BLK):
            ls = slice(c * BLK, (c + 1) * BLK)
            for s in range(tb // BLK):
                rs = slice(s * BLK, (s + 1) * BLK)
                valid = (n * tb + s * BLK + _rows((BLK, BLK))) >= PAD
                hs = []
                for q in range(nin):
                    x_ref, t_ref, w_ref, b_ref = refs[4 * q:4 * q + 4]
                    tail = (jnp.where(n > 0, t_ref[:, ls], 0.0) if s == 0
                            else x_ref[s * BLK - 8:s * BLK, ls])
                    hs.append(_conv_apply(x_ref[rs, ls], tail, w_ref[:, ls], b_ref[:, ls], ksz))
                y = _gelu(hs[0]) * hs[1] if mode == "geglu" else _silu(hs[0])
                y = _keep(y, valid, s)
                o_ref[rs, ls] = y.astype(o_ref.dtype)
                if with_t:
                    refs[-1][ls, rs] = y.T.astype(o_ref.dtype)

    ins, specs = [], []
    for q in range(nin):
        co = xo + q * ncb
        wo = q * ncb
        ins += [x, x, cw, cb.reshape(1, -1)]
        specs += [
            pl.BlockSpec((tb, cblk), lambda j, n, co=co: (n, co + j)),
            pl.BlockSpec((8, cblk), lambda j, n, co=co: (jnp.maximum(n * t8 - 1, 0), co + j)),
            pl.BlockSpec((ksz, cblk), lambda j, n, wo=wo: (0, wo + j)),
            pl.BlockSpec((1, cblk), lambda j, n, wo=wo: (0, wo + j)),
        ]
    out_specs = pl.BlockSpec((tb, cblk), lambda j, n: (n, j))
    out_shape = jax.ShapeDtypeStruct((t, c_out), out_dtype)
    if with_t:
        out_specs = [out_specs, pl.BlockSpec((cblk, tb), lambda j, n: (j, n))]
        out_shape = [out_shape, jax.ShapeDtypeStruct((c_out, t), out_dtype)]
    return pl.pallas_call(
        kern, name=name, grid=(ncb, nb), in_specs=specs, out_specs=out_specs, out_shape=out_shape,
        compiler_params=_cp("parallel", "parallel"),
    )(*ins)


def dwconv_bwd(x, cw, cb, dy, *, mode, x_off, c_out, cblk, name, carry=None):
    t = x.shape[0]
    ksz = cw.shape[0]
    tb = _conv_tile(t)
    nb, ncb, t8 = t // tb, c_out // cblk, tb // 8
    xo = x_off // cblk
    nin = 2 if mode == "geglu" else 1
    ctot = nin * c_out

    def kern(*refs):
        dy_ref = refs[4 * nin]
        outs = refs[4 * nin + 1:4 * nin + 1 + 3 * nin]
        heads = refs[4 * nin + 1 + 3 * nin:]
        n = pl.program_id(1)
        blk = nb - 1 - n

        @pl.when(n == 0)
        def _():
            for q in range(nin):
                heads[q][...] = jnp.zeros_like(heads[q])
                outs[3 * q + 1][...] = jnp.zeros_like(outs[3 * q + 1])
                outs[3 * q + 2][...] = jnp.zeros_like(outs[3 * q + 2])

        for c in range(cblk // BLK):
            ls = slice(c * BLK, (c + 1) * BLK)
            head = [heads[q][:, ls] for q in range(nin)]
            dwa = [[None] * ksz for _ in range(nin)]
            dba = [None] * nin
            for s in reversed(range(tb // BLK)):
                rs = slice(s * BLK, (s + 1) * BLK)
                valid = (blk * tb + s * BLK + _rows((BLK, BLK))) >= PAD
                xs, tails, hs = [], [], []
                for q in range(nin):
                    x_ref, t_ref, w_ref, b_ref = refs[4 * q:4 * q + 4]
                    tail = (jnp.where(blk > 0, t_ref[:, ls], 0.0) if s == 0
                            else x_ref[s * BLK - 8:s * BLK, ls])
                    xs.append(x_ref[rs, ls])
                    tails.append(tail)
                    hs.append(_conv_apply(xs[q], tail, w_ref[:, ls], b_ref[:, ls], ksz))
                dyv = dy_ref[rs, ls].astype(F32)
                if mode == "geglu":
                    dhs = [dyv * hs[1] * _gelu_grad(hs[0]), dyv * _gelu(hs[0])]
                else:
                    dhs = [dyv * _silu_grad(hs[0])]
                for q in range(nin):
                    w_ref = refs[4 * q + 2]
                    dh = _keep(dhs[q], valid, s)
                    dx = jnp.zeros_like(dh)
                    for k in range(ksz):
                        sh = ksz - 1 - k
                        dx = dx + w_ref[k:k + 1, ls] * _shift_up(dh, head[q], sh)
                        part = jnp.sum(dh * _shift_down(xs[q], tails[q], sh), axis=0, keepdims=True)
                        dwa[q][k] = part if dwa[q][k] is None else dwa[q][k] + part
                    outs[3 * q][rs, ls] = _keep(dx, valid, s).astype(outs[3 * q].dtype)
                    part = jnp.sum(dh, axis=0, keepdims=True)
                    dba[q] = part if dba[q] is None else dba[q] + part
                    head[q] = dh[0:8]
            for q in range(nin):
                outs[3 * q + 1][:, ls] += jnp.concatenate(dwa[q], axis=0)
                outs[3 * q + 2][:, ls] += dba[q]
                heads[q][:, ls] = head[q]

    ins, specs, out_specs, out_shape, scratch = [], [], [], [], []
    for q in range(nin):
        co = xo + q * ncb
        wo = q * ncb
        ins += [x, x, cw, cb.reshape(1, -1)]
        specs += [
            pl.BlockSpec((tb, cblk), lambda j, n, co=co: (nb - 1 - n, co + j)),
            pl.BlockSpec((8, cblk), lambda j, n, co=co: (jnp.maximum((nb - 1 - n) * t8 - 1, 0), co + j)),
            pl.BlockSpec((ksz, cblk), lambda j, n, wo=wo: (0, wo + j)),
            pl.BlockSpec((1, cblk), lambda j, n, wo=wo: (0, wo + j)),
        ]
        out_specs += [
            pl.BlockSpec((tb, cblk), lambda j, n: (nb - 1 - n, j)),
            pl.BlockSpec((ksz, cblk), lambda j, n: (0, j)),
            pl.BlockSpec((1, cblk), lambda j, n: (0, j)),
        ]
        out_shape += [jax.ShapeDtypeStruct((t, c_out), MXU),
                      jax.ShapeDtypeStruct((ksz, c_out), F32),
                      jax.ShapeDtypeStruct((1, c_out), F32)]
        scratch.append(pltpu.VMEM((8, cblk), F32))
    ins.append(dy)
    specs.append(pl.BlockSpec((tb, cblk), lambda j, n: (nb - 1 - n, j)))
    res = _call(kern, ins, name=name, grid=(ncb, nb), in_specs=specs, out_specs=out_specs,
                out_shape=out_shape, scratch_shapes=scratch, sem=("parallel", "arbitrary"), carry=carry)
    dxs = [res[3 * q] for q in range(nin)]
    dcw = jnp.concatenate([res[3 * q + 1] for q in range(nin)], axis=1)
    dcb = jnp.concatenate([res[3 * q + 2] for q in range(nin)], axis=1)
    return dxs, dcw, dcb.reshape(ctot), res[3 * nin:]


def _lru_tile(t):
    return _pick(t, (1664, 640, 384, 256, 128))


def _lru_gates(xc, wa, ba, wx, bx, sp):
    r = _sigmoid(_dot(xc, wa) + ba)
    i = _sigmoid(_dot(xc, wx) + bx)
    log_a = -LRU_C * r * sp
    a = jnp.exp(log_a)
    mult = jnp.sqrt(-_expm1(2.0 * log_a))
    return r, i, a, mult


def lru_fwd(proj, cw, cb, wa, ba, wx, bx, lam, *, gate_off, xr_off, name, carry=None):
    t = proj.shape[0]
    tb = _lru_tile(t)
    nb, ns, t8 = t // tb, tb // BLK, tb // 8
    go, xo = gate_off // BLK, xr_off // BLK

    def kern(g_ref, x_ref, xt_ref, cw_ref, cb_ref, wa_ref, ba_ref, wx_ref, bx_ref, lam_ref,
             y_ref, yt_ref, h_ref, hc_ref):
        n = pl.program_id(1)

        @pl.when(n == 0)
        def _():
            hc_ref[...] = jnp.zeros_like(hc_ref)

        sp = _softplus(-lam_ref[...])
        hprev = hc_ref[0:1, :]
        scans = []
        for s in range(ns):
            sl = slice(s * BLK, (s + 1) * BLK)
            xv = x_ref[sl, :]
            tail = jnp.where(n > 0, xt_ref[...], 0.0) if s == 0 else x_ref[s * BLK - 8:s * BLK, :]
            valid = (n * tb + s * BLK + _rows((BLK, BLK))) >= PAD
            xc = _keep(_conv_apply(xv, tail, cw_ref[...], cb_ref[...], 4), valid, s)
            _, i, a, mult = _lru_gates(xc, wa_ref[0], ba_ref[...], wx_ref[0], bx_ref[...], sp)
            scans.append(_scan_fwd(a, mult * (i * xc)))
        for s in range(ns):
            sl = slice(s * BLK, (s + 1) * BLK)
            ca, cu = scans[s]
            h = cu + ca * hprev
            hprev = _row_at(h, BLK - 1)
            h_ref[sl, :] = h
            y = _gelu(g_ref[sl, :]) * h
            y_ref[sl, :] = y.astype(y_ref.dtype)
            yt_ref[:, sl] = y.T.astype(yt_ref.dtype)
        hc_ref[...] = jnp.broadcast_to(hprev, hc_ref.shape)

    vec = pl.BlockSpec((1, BLK), lambda j, n: (0, j))
    mat = pl.BlockSpec((1, BLK, BLK), lambda j, n: (j, 0, 0))
    return _call(
        kern, (proj, proj, proj, cw, cb.reshape(1, -1), wa, ba.reshape(1, -1), wx, bx.reshape(1, -1),
               lam.reshape(1, -1)),
        name=name, grid=(D_RNN // BLK, nb),
        in_specs=[
            pl.BlockSpec((tb, BLK), lambda j, n: (n, go + j)),
            pl.BlockSpec((tb, BLK), lambda j, n: (n, xo + j)),
            pl.BlockSpec((8, BLK), lambda j, n: (jnp.maximum(n * t8 - 1, 0), xo + j)),
            pl.BlockSpec((4, BLK), lambda j, n: (0, j)), vec, mat, vec, mat, vec, vec,
        ],
        out_specs=[pl.BlockSpec((tb, BLK), lambda j, n: (n, j)),
                   pl.BlockSpec((BLK, tb), lambda j, n: (j, n)),
                   pl.BlockSpec((tb, BLK), lambda j, n: (n, j))],
        out_shape=[jax.ShapeDtypeStruct((t, D_RNN), MXU), jax.ShapeDtypeStruct((D_RNN, t), MXU),
                   jax.ShapeDtypeStruct((t, D_RNN), F32)],
        scratch_shapes=[pltpu.VMEM((8, BLK), F32)],
        sem=("parallel", "arbitrary"), carry=carry)


def lru_bwd(proj, h, dy, cw, cb, wa, ba, wx, bx, lam, *, gate_off, xr_off, dy_off, name, carry=None):
    t = proj.shape[0]
    tb = _lru_tile(t)
    nb, ns, t8 = t // tb, tb // BLK, tb // 8
    go, xo, do = gate_off // BLK, xr_off // BLK, dy_off // BLK

    def kern(g_ref, x_ref, xt_ref, h_ref, ht_ref, dy_ref, cw_ref, cb_ref, wa_ref, ba_ref,
             wx_ref, bx_ref, lam_ref,
             dg_ref, dx_ref, dcw_ref, dcb_ref, dwa_ref, dba_ref, dwx_ref, dbx_ref, dlam_ref,
             gin_ref, head_ref):
        n = pl.program_id(1)
        blk = nb - 1 - n

        @pl.when(n == 0)
        def _():
            gin_ref[...] = jnp.zeros_like(gin_ref)
            head_ref[...] = jnp.zeros_like(head_ref)
            for r_ in (dcw_ref, dcb_ref, dwa_ref, dba_ref, dwx_ref, dbx_ref, dlam_ref):
                r_[...] = jnp.zeros_like(r_)

        lamv = lam_ref[...]
        sp = _softplus(-lamv)
        dsp_dlam = -_sigmoid(-lamv)
        g_in = gin_ref[0:1, :]
        head = head_ref[...]
        ones8 = jnp.ones((8, BLK), F32)
        wav, wxv = wa_ref[0], wx_ref[0]
        staged = {}
        for s in range(ns):
            sl = slice(s * BLK, (s + 1) * BLK)
            xv = x_ref[sl, :]
            if s == 0:
                tail = jnp.where(blk > 0, xt_ref[...], 0.0)
                htail = jnp.where(blk > 0, ht_ref[...], 0.0)
            else:
                tail = x_ref[s * BLK - 8:s * BLK, :]
                htail = h_ref[s * BLK - 8:s * BLK, :]
            valid = (blk * tb + s * BLK + _rows((BLK, BLK))) >= PAD
            xc = _keep(_conv_apply(xv, tail, cw_ref[...], cb_ref[...], 4), valid, s)
            r, i, a, mult = _lru_gates(xc, wav, ba_ref[...], wxv, bx_ref[...], sp)
            hv = h_ref[sl, :]
            hprev = _shift_down(hv, htail, 1)
            gv = g_ref[sl, :]
            dyv = dy_ref[sl, :].astype(F32)
            dg_ref[sl, :] = (dyv * hv * _gelu_grad(gv)).astype(dg_ref.dtype)
            cc, cu = _scan_rev(_shift_up(a, ones8, 1), dyv * _gelu(gv))
            staged[s] = (xv, tail, valid, xc, r, i, a, mult, hprev, cc, cu)
        for s in reversed(range(ns)):
            sl = slice(s * BLK, (s + 1) * BLK)
            xv, tail, valid, xc, r, i, a, mult, hprev, cc, cu = staged[s]
            gg = cu + cc * g_in
            g_in = _row_at(a * gg, 0)
            da = gg * hprev
            di = gg * mult * xc
            dxc = gg * mult * i
            dmult = gg * i * xc
            dlog_a = da * a - dmult * (a * a) / mult
            dr = dlog_a * (-LRU_C * sp)
            dlam_ref[...] += jnp.sum(dlog_a * (-LRU_C) * r, axis=0, keepdims=True) * dsp_dlam
            dpr = dr * r * (1.0 - r)
            dpi = di * i * (1.0 - i)
            dxc = dxc + _dot_nt(dpr, wav) + _dot_nt(dpi, wxv)
            dxc, dpr, dpi = _keep(dxc, valid, s), _keep(dpr, valid, s), _keep(dpi, valid, s)
            dwa_ref[0] += _dot_tn(xc, dpr)
            dwx_ref[0] += _dot_tn(xc, dpi)
            dba_ref[...] += jnp.sum(dpr, axis=0, keepdims=True)
            dbx_ref[...] += jnp.sum(dpi, axis=0, keepdims=True)
            dx = jnp.zeros_like(dxc)
            dws = []
            for k in range(4):
                dx = dx + cw_ref[k:k + 1, :] * _shift_up(dxc, head, 3 - k)
                dws.append(jnp.sum(dxc * _shift_down(xv, tail, 3 - k), axis=0, keepdims=True))
            dx_ref[sl, :] = _keep(dx, valid, s).astype(dx_ref.dtype)
            dcw_ref[...] += jnp.concatenate(dws, axis=0)
            dcb_ref[...] += jnp.sum(dxc, axis=0, keepdims=True)
            head = dxc[0:8]
        gin_ref[...] = jnp.broadcast_to(g_in, gin_ref.shape)
        head_ref[...] = head

    vec = pl.BlockSpec((1, BLK), lambda j, n: (0, j))
    mat = pl.BlockSpec((1, BLK, BLK), lambda j, n: (j, 0, 0))
    cws = pl.BlockSpec((4, BLK), lambda j, n: (0, j))

    def rb(off):
        return pl.BlockSpec((tb, BLK), lambda j, n: (nb - 1 - n, off + j))

    def tl(off):
        return pl.BlockSpec((8, BLK), lambda j, n: (jnp.maximum((nb - 1 - n) * t8 - 1, 0), off + j))

    return _call(
        kern, (proj, proj, proj, h, h, dy, cw, cb.reshape(1, -1), wa, ba.reshape(1, -1), wx,
               bx.reshape(1, -1), lam.reshape(1, -1)),
        name=name, grid=(D_RNN // BLK, nb),
        in_specs=[rb(go), rb(xo), tl(xo), rb(0), tl(0), rb(do), cws, vec, mat, vec, mat, vec, vec],
        out_specs=[rb(0), rb(0), cws, vec, mat, vec, mat, vec, vec],
        out_shape=[jax.ShapeDtypeStruct((t, D_RNN), MXU), jax.ShapeDtypeStruct((t, D_RNN), MXU),
                   jax.ShapeDtypeStruct((4, D_RNN), F32), jax.ShapeDtypeStruct((1, D_RNN), F32),
                   jax.ShapeDtypeStruct((8, BLK, BLK), F32), jax.ShapeDtypeStruct((1, D_RNN), F32),
                   jax.ShapeDtypeStruct((8, BLK, BLK), F32), jax.ShapeDtypeStruct((1, D_RNN), F32),
                   jax.ShapeDtypeStruct((1, D_RNN), F32)],
        scratch_shapes=[pltpu.VMEM((8, BLK), F32), pltpu.VMEM((8, BLK), F32)],
        sem=("parallel", "arbitrary"), carry=carry)


_SCALE = HEAD_DIM ** -0.5


STK = 4


def _attn_masks(n):
    qi = np.arange(STK * BLK)[:, None] % BLK
    c = np.arange(3 * BLK)[None, :]
    tq = n * BLK + qi - PAD
    s_band = (n - 1) * BLK + c - PAD
    d_band = tq - s_band
    ok_band = (s_band >= N_META) & (d_band >= 0) & (d_band < BLK)
    jm = c - 2 * BLK
    d_meta = tq - (jm - PAD)
    ok_meta = (jm >= PAD) & (d_meta >= 0)
    is_band = c < 2 * BLK
    ok = np.where(is_band, ok_band, ok_meta)
    dist = np.where(is_band, d_band, np.minimum(d_meta, BLK)).astype(np.float32)
    return ok, dist


def _stack_heads(g, e):
    return [8 * g + 2 * i + e for i in range(STK)]


def _attn_bias_table():
    tabs = []
    for n in range(3):
        ok, dist = _attn_masks(n)
        per = []
        for g in range(2):
            for e in range(2):
                slope = np.repeat(np.array([2.0 ** (-8.0 * (h + 1) / N_Q_HEADS) for h in _stack_heads(g, e)],
                                           np.float32), BLK)[:, None]
                per.append(np.where(ok, -(slope * dist), np.float32(NEG)).astype(np.float32))
        tabs.append(np.stack(per))
    return jnp.asarray(np.stack(tabs))


def _stack_sinks(heads, sk):
    return jnp.concatenate(
        [jnp.broadcast_to(jnp.sum(jnp.where(_lanes(sk.shape) == h, sk, 0.0), axis=1, keepdims=True),
                          (BLK, 1)) for h in heads], axis=0)


def _stack_tiles(ref, g, sel):
    return jnp.concatenate(
        [jnp.where(sel, ref[:, (4 * g + i) * BLK:(4 * g + i + 1) * BLK].astype(F32), 0.0)
         for i in range(STK)], axis=0)


def _attn_probs(qk, bias, sink):
    s = qk * _SCALE + bias
    mx = jnp.maximum(jnp.max(s, axis=-1, keepdims=True), sink)
    p = jnp.exp(s - mx)
    es = jnp.exp(sink - mx)
    inv = 1.0 / (jnp.sum(p, axis=-1, keepdims=True) + es)
    return p * inv, es * inv


def _attn_specs(t, q_off, k_off, v_off, rev):
    nb = t // BLK
    qo, ko, vo = q_off // 1024, k_off // BLK, v_off // BLK

    def b(n):
        return nb - 1 - n if rev else n

    return [
        pl.BlockSpec((BLK, 1024), lambda n: (b(n), qo)),
        pl.BlockSpec((BLK, BLK), lambda n: (b(n), ko)),
        pl.BlockSpec((BLK, BLK), lambda n: (b(n), vo)),
        pl.BlockSpec((BLK, BLK), lambda n: (jnp.maximum(b(n) - 1, 0), ko)),
        pl.BlockSpec((BLK, BLK), lambda n: (jnp.maximum(b(n) - 1, 0), vo)),
        pl.BlockSpec((BLK, BLK), lambda n: (0, ko)),
        pl.BlockSpec((BLK, BLK), lambda n: (0, vo)),
        pl.BlockSpec((1, BLK), lambda n: (0, 0)),
        pl.BlockSpec((1, 4, STK * BLK, 3 * BLK), lambda n: (jnp.minimum(b(n), 2), 0, 0, 0)),
    ]


def attn_fwd(proj, sinks, *, q_off, k_off, v_off, name, carry=None):
    t = proj.shape[0]
    nb = t // BLK

    def kern(q_ref, kc_ref, vc_ref, kp_ref, vp_ref, km_ref, vm_ref, sk_ref, tab_ref, o_ref):
        k_all = jnp.concatenate([kp_ref[...], kc_ref[...], km_ref[...]], axis=0)
        v_all = jnp.concatenate([vp_ref[...], vc_ref[...], vm_ref[...]], axis=0)
        k_alt = pltpu.roll(k_all, HEAD_DIM, 1)
        v_alt = pltpu.roll(v_all, HEAD_DIM, 1)
        low = _lanes((BLK, BLK)) < HEAD_DIM
        stacks = [(g, e) for g in range(2) for e in range(2)]
        qk = {(g, e): _dot_nt(_stack_tiles(q_ref, g, low == (e == 0)), k_all if g == e else k_alt)
              for g, e in stacks}
        ps = {(g, e): _attn_probs(qk[g, e], tab_ref[0, 2 * g + e],
                                  _stack_sinks(_stack_heads(g, e), sk_ref[...]))[0] for g, e in stacks}
        outs = {(g, e): _dot(ps[g, e], v_all if g == e else v_alt) for g, e in stacks}
        for hp in range(N_Q_HEADS // 2):
            g, rs = hp // STK, slice((hp % STK) * BLK, (hp % STK + 1) * BLK)
            o_ref[:, hp * BLK:(hp + 1) * BLK] = jnp.where(low, outs[g, 0][rs], outs[g, 1][rs]).astype(o_ref.dtype)

    sk = jnp.zeros((1, BLK), F32).at[0, :N_Q_HEADS].set(sinks)
    return _call(
        kern, (proj, proj, proj, proj, proj, proj, proj, sk, _attn_bias_table()), name=name, grid=(nb,),
        in_specs=_attn_specs(t, q_off, k_off, v_off, False),
        out_specs=[pl.BlockSpec((BLK, 1024), lambda n: (n, 0))],
        out_shape=[jax.ShapeDtypeStruct((t, 1024), MXU)],
        sem=("parallel",), carry=carry)


def attn_bwd(proj, sinks, dy, *, q_off, k_off, v_off, dy_off, name, carry=None):
    t = proj.shape[0]
    nb = t // BLK
    do = dy_off // 1024

    def kern(q_ref, kc_ref, vc_ref, kp_ref, vp_ref, km_ref, vm_ref, sk_ref, tab_ref, do_ref,
             dq_ref, dk_ref, dv_ref, dsk_ref, ck_ref, cv_ref, mk_ref, mv_ref):
        n = pl.program_id(0)
        blk = nb - 1 - n

        @pl.when(n == 0)
        def _():
            for r_ in (ck_ref, cv_ref, mk_ref, mv_ref, dsk_ref):
                r_[...] = jnp.zeros_like(r_)

        k_all = jnp.concatenate([kp_ref[...], kc_ref[...], km_ref[...]], axis=0)
        v_all = jnp.concatenate([vp_ref[...], vc_ref[...], vm_ref[...]], axis=0)
        k_alt = pltpu.roll(k_all, HEAD_DIM, 1)
        v_alt = pltpu.roll(v_all, HEAD_DIM, 1)
        low = _lanes((BLK, BLK)) < HEAD_DIM
        lane1 = _lanes((1, BLK))
        dk_all = jnp.zeros((3 * BLK, BLK), F32)
        dv_all = jnp.zeros((3 * BLK, BLK), F32)
        dsk = jnp.zeros((1, BLK), F32)
        stacks = [(g, e) for g in range(2) for e in range(2)]
        qm = {(g, e): _stack_tiles(q_ref, g, low == (e == 0)) for g, e in stacks}
        dom = {(g, e): _stack_tiles(do_ref, g, low == (e == 0)) for g, e in stacks}
        qk = {(g, e): _dot_nt(qm[g, e], k_all if g == e else k_alt) for g, e in stacks}
        dp = {(g, e): _dot_nt(dom[g, e], v_all if g == e else v_alt) for g, e in stacks}
        ps, dss = {}, {}
        for g, e in stacks:
            heads = _stack_heads(g, e)
            p, psink = _attn_probs(qk[g, e], tab_ref[0, 2 * g + e], _stack_sinks(heads, sk_ref[...]))
            delta = jnp.sum(p * dp[g, e], axis=-1, keepdims=True)
            ps[g, e] = p
            dss[g, e] = p * (dp[g, e] - delta) * _SCALE
            psd = psink * delta
            for i, h in enumerate(heads):
                dsk = dsk + jnp.where(lane1 == h, -jnp.sum(psd[i * BLK:(i + 1) * BLK], axis=0, keepdims=True), 0.0)
        dqs = {(g, e): _dot(dss[g, e], k_all if g == e else k_alt) for g, e in stacks}
        for g, e in stacks:
            dkh = _dot_tn(dss[g, e], qm[g, e])
            dvh = _dot_tn(ps[g, e], dom[g, e])
            if g != e:
                dkh = pltpu.roll(dkh, HEAD_DIM, 1)
                dvh = pltpu.roll(dvh, HEAD_DIM, 1)
            dk_all = dk_all + dkh
            dv_all = dv_all + dvh
        for hp in range(N_Q_HEADS // 2):
            g, rs = hp // STK, slice((hp % STK) * BLK, (hp % STK + 1) * BLK)
            dq_ref[:, hp * BLK:(hp + 1) * BLK] = jnp.where(low, dqs[g, 0][rs], dqs[g, 1][rs]).astype(dq_ref.dtype)
        dsk_ref[...] += dsk
        mk_ref[...] += dk_all[2 * BLK:3 * BLK]
        mv_ref[...] += dv_all[2 * BLK:3 * BLK]
        is0 = blk == 0
        dk_ref[...] = (dk_all[BLK:2 * BLK] + ck_ref[...] + jnp.where(is0, mk_ref[...], 0.0)).astype(dk_ref.dtype)
        dv_ref[...] = (dv_all[BLK:2 * BLK] + cv_ref[...] + jnp.where(is0, mv_ref[...], 0.0)).astype(dv_ref.dtype)
        ck_ref[...] = dk_all[0:BLK]
        cv_ref[...] = dv_all[0:BLK]

    sk = jnp.zeros((1, BLK), F32).at[0, :N_Q_HEADS].set(sinks)
    kv = pl.BlockSpec((BLK, BLK), lambda n: (nb - 1 - n, 0))
    res = _call(
        kern, (proj, proj, proj, proj, proj, proj, proj, sk, _attn_bias_table(), dy), name=name, grid=(nb,),
        in_specs=_attn_specs(t, q_off, k_off, v_off, True)
        + [pl.BlockSpec((BLK, 1024), lambda n: (nb - 1 - n, do))],
        out_specs=[pl.BlockSpec((BLK, 1024), lambda n: (nb - 1 - n, 0)), kv, kv,
                   pl.BlockSpec((1, BLK), lambda n: (0, 0))],
        out_shape=[jax.ShapeDtypeStruct((t, 1024), MXU), jax.ShapeDtypeStruct((t, BLK), MXU),
                   jax.ShapeDtypeStruct((t, BLK), MXU), jax.ShapeDtypeStruct((1, BLK), F32)],
        scratch_shapes=[pltpu.VMEM((BLK, BLK), F32)] * 4,
        sem=("arbitrary",), carry=carry)
    return [res[0], res[1], res[2], res[3][0, :N_Q_HEADS]] + res[4:]


GW = D_SSM // SSD_GROUPS
EXP_ROWS = 3 * BLK + 8
RED_ROWS = EXP_ROWS + 8


def _head_expand():
    ch = jnp.arange(D_SSM) // HEAD_DIM
    return (jnp.arange(BLK)[:, None] == ch[None, :]).astype(BF16)


def _ssd_decay(raw, dtb, alog, rowv):
    valid = rowv & (_lanes((BLK, BLK)) < SSD_HEADS)
    pre = raw + dtb
    dtp = jnp.where(valid, _softplus(pre), 0.0)
    av = -jnp.exp(alog)
    cs = _cumsum_fwd(dtp * av)
    cs_last = _row_at(cs, BLK - 1)
    return valid, pre, dtp, av, cs, jnp.exp(cs), jnp.exp(cs_last - cs), jnp.exp(cs_last)


def _head_col(x, h):
    return jnp.sum(jnp.where(_lanes(x.shape) == h, x, 0.0), axis=1, keepdims=True)


def _ssd_group_fwd(g, xdt, cs, cst, cb, tril, low):
    lm = []
    for k in range(4):
        h = 4 * g + k
        seg = _head_col(cs, h) - _row_at(cst, h)
        lmat = jnp.where(tril, jnp.exp(jnp.minimum(seg, 0.0)), 0.0)
        lm.append((lmat, cb * lmat))
    hv = [_dot(lm[k][1], xdt[:, g * GW + (k // 2) * BLK:g * GW + (k // 2 + 1) * BLK]) for k in range(4)]
    return jnp.concatenate([jnp.where(low, hv[0], hv[1]), jnp.where(low, hv[2], hv[3])], axis=1), lm


def ssd_decay(proj, dt_bias, a_log, *, dt_off, name):
    t = proj.shape[0]
    tb = _conv_tile(t)
    dto = dt_off // BLK

    def kern(dt_ref, dtb_ref, alog_ref, o_ref):
        n = pl.program_id(0)
        for s in range(tb // BLK):
            rs = slice(s * BLK, (s + 1) * BLK)
            rowv = (n * tb + s * BLK + _rows((BLK, BLK))) >= PAD
            _, _, dtp, _, cs, ecs, w, _ = _ssd_decay(dt_ref[rs, :], dtb_ref[...], alog_ref[...], rowv)
            for k, v in enumerate((dtp, cs, ecs, w)):
                o_ref[rs, k * BLK:(k + 1) * BLK] = v

    vec = pl.BlockSpec((1, BLK), lambda n: (0, 0))
    return pl.pallas_call(
        kern, name=name, grid=(t // tb,),
        in_specs=[pl.BlockSpec((tb, BLK), lambda n: (n, dto)), vec, vec],
        out_specs=pl.BlockSpec((tb, 4 * BLK), lambda n: (n, 0)),
        out_shape=jax.ShapeDtypeStruct((t, 4 * BLK), F32),
        compiler_params=_cp("parallel"),
    )(proj, _pad128(dt_bias), _pad128(a_log))


def _load_decay(d_ref):
    dtp, cs, ecs, w = (d_ref[:, k * BLK:(k + 1) * BLK] for k in range(4))
    return dtp, cs, ecs, w, _row_at(ecs, BLK - 1)


def _expand_heads(dtp, ecs, w, dec, e):
    ex = _dot(jnp.concatenate([dtp, ecs, w, jnp.broadcast_to(dec, (8, BLK))], axis=0), e)
    return ex[0:BLK], ex[BLK:2 * BLK], ex[2 * BLK:3 * BLK], jnp.max(ex[3 * BLK:EXP_ROWS], axis=0, keepdims=True)


def _ssd_specs(t, z_off, dt_off, rev):
    nb = t // BLK
    zo, dto = z_off // D_SSM, dt_off // BLK

    def b(n):
        return nb - 1 - n if rev else n

    vec = lambda w: pl.BlockSpec((1, w), lambda n: (0, 0))
    return [
        pl.BlockSpec((BLK, D_SSM), lambda n: (b(n), 0)),
        pl.BlockSpec((BLK, 1024), lambda n: (b(n), 2)),
        pl.BlockSpec((BLK, 1024), lambda n: (b(n), 3)),
        pl.BlockSpec((BLK, D_SSM), lambda n: (b(n), zo)),
        pl.BlockSpec((BLK, BLK), lambda n: (b(n), dto)),
        vec(BLK), vec(BLK), vec(D_SSM), vec(D_SSM),
        pl.BlockSpec((BLK, D_SSM), lambda n: (0, 0)),
        pl.BlockSpec((BLK, 4 * BLK), lambda n: (b(n), 0)),
    ]


def _pad128(v):
    return jnp.zeros((1, BLK), F32).at[0, :v.shape[0]].set(v)


def ssd_fwd(xbc, proj, decay, dt_bias, a_log, d_skip, gate_norm, *, z_off, dt_off, name):
    t = xbc.shape[0]
    nb = t // BLK

    def kern(x_ref, b_ref, c_ref, z_ref, dt_ref, dtb_ref, alog_ref, dsk_ref, gn_ref, e_ref, d_ref,
             yn_ref, ynt_ref, st_ref, p_ref):
        n = pl.program_id(0)

        @pl.when(n == 0)
        def _():
            p_ref[...] = jnp.zeros_like(p_ref)

        bgs = [b_ref[:, g * BLK:(g + 1) * BLK] for g in range(SSD_GROUPS)]
        cgs = [c_ref[:, g * BLK:(g + 1) * BLK] for g in range(SSD_GROUPS)]
        cbs = [_dot_nt(cgs[g], bgs[g]) for g in range(SSD_GROUPS)]
        pgs = [p_ref[g] for g in range(SSD_GROUPS)]
        zs = [_dot(cgs[g], pgs[g]) for g in range(SSD_GROUPS)]
        new_p = []
        dtp, cs, ecs, w, dec = _load_decay(d_ref)
        dtp_c, ecs_c, w_c, dec_c = _expand_heads(dtp, ecs, w, dec, e_ref[...])
        xv = x_ref[...]
        xdt = xv * dtp_c
        wx = w_c * xdt
        cst = cs.T
        tril = _rows((BLK, BLK)) >= _lanes((BLK, BLK))
        low = _lanes((BLK, BLK)) < HEAD_DIM
        for g in range(SSD_GROUPS):
            st_ref[0, g] = pgs[g]
        for g in range(SSD_GROUPS):
            gs = slice(g * GW, (g + 1) * GW)
            ydiag, _ = _ssd_group_fwd(g, xdt, cs, cst, cbs[g], tril, low)
            y = ydiag + zs[g] * ecs_c[:, gs] + dsk_ref[:, gs] * xv[:, gs]
            new_p.append(pgs[g] * dec_c[:, gs] + _dot_tn(bgs[g], wx[:, gs]))
            yz = y * _silu(z_ref[:, gs])
            r = lax.rsqrt(jnp.mean(yz * yz, axis=-1, keepdims=True) + EPS)
            yn = yz * r * gn_ref[:, gs]
            yn_ref[:, gs] = yn.astype(yn_ref.dtype)
            ynt_ref[gs, :] = yn.T.astype(ynt_ref.dtype)
        for g in range(SSD_GROUPS):
            p_ref[g] = new_p[g]

    return pl.pallas_call(
        kern, name=name, grid=(nb,),
        in_specs=_ssd_specs(t, z_off, dt_off, False),
        out_specs=[pl.BlockSpec((BLK, D_SSM), lambda n: (n, 0)),
                   pl.BlockSpec((D_SSM, BLK), lambda n: (0, n)),
                   pl.BlockSpec((1, SSD_GROUPS, BLK, GW), lambda n: (n, 0, 0, 0))],
        out_shape=[jax.ShapeDtypeStruct((t, D_SSM), MXU), jax.ShapeDtypeStruct((D_SSM, t), MXU),
                   jax.ShapeDtypeStruct((nb, SSD_GROUPS, BLK, GW), F32)],
        scratch_shapes=[pltpu.VMEM((SSD_GROUPS, BLK, GW), F32)],
        compiler_params=_cp("arbitrary"),
    )(xbc, xbc, xbc, proj, proj, _pad128(dt_bias), _pad128(a_log),
      jnp.repeat(d_skip, HEAD_DIM).reshape(1, D_SSM), gate_norm.reshape(1, D_SSM), _head_expand(), decay)


def ssd_bwd(xbc, proj, decay, st, dyn, dt_bias, a_log, d_skip, gate_norm, *, z_off, dt_off, name, carry=None):
    t = xbc.shape[0]
    nb = t // BLK

    def kern(x_ref, b_ref, c_ref, z_ref, dt_ref, dtb_ref, alog_ref, dsk_ref, gn_ref, e_ref, d_ref,
             et_ref, st_ref, dyn_ref,
             dxbc_ref, dz_ref, draw_ref, dgn_ref, ddsk_ref, ddtb_ref, dalog_ref,
             dp_ref, tr_ref):
        n = pl.program_id(0)
        blk = nb - 1 - n

        @pl.when(n == 0)
        def _():
            for r_ in (dp_ref, dgn_ref, ddsk_ref, ddtb_ref, dalog_ref):
                r_[...] = jnp.zeros_like(r_)

        bgs = [b_ref[:, g * BLK:(g + 1) * BLK] for g in range(SSD_GROUPS)]
        cgs = [c_ref[:, g * BLK:(g + 1) * BLK] for g in range(SSD_GROUPS)]
        cbs = [_dot_nt(cgs[g], bgs[g]) for g in range(SSD_GROUPS)]
        pgs = [st_ref[0, g] for g in range(SSD_GROUPS)]
        dpns = [dp_ref[g] for g in range(SSD_GROUPS)]
        zs = [_dot(cgs[g], pgs[g]) for g in range(SSD_GROUPS)]
        dwxs = [_dot(bgs[g], dpns[g]) for g in range(SSD_GROUPS)]
        new_dp, dgn_parts = [], []
        valid = ((blk * BLK + _rows((BLK, BLK))) >= PAD) & (_lanes((BLK, BLK)) < SSD_HEADS)
        pre = dt_ref[...] + dtb_ref[...]
        av = -jnp.exp(alog_ref[...])
        dtp, cs, ecs, w, dec = _load_decay(d_ref)
        dtp_c, ecs_c, w_c, dec_c = _expand_heads(dtp, ecs, w, dec, e_ref[...])
        xv = x_ref[...]
        xdt = xv * dtp_c
        wx = w_c * xdt
        cst = cs.T
        tril = _rows((BLK, BLK)) >= _lanes((BLK, BLK))
        lane = _lanes((BLK, BLK))
        rowi = _rows((BLK, BLK))
        low = lane < HEAD_DIM
        dcs = jnp.zeros((BLK, BLK), F32)
        dcst = jnp.zeros((BLK, BLK), F32)
        for g in range(SSD_GROUPS):
            gs = slice(g * GW, (g + 1) * GW)
            bg, cg = bgs[g], cgs[g]
            pg, dpn = pgs[g], dpns[g]
            xg = xv[:, gs]
            ydiag, lm = _ssd_group_fwd(g, xdt, cs, cst, cbs[g], tril, low)
            yoff = zs[g] * ecs_c[:, gs]
            y = ydiag + yoff + dsk_ref[:, gs] * xg
            zz = z_ref[:, gs]
            sz = _silu(zz)
            yz = y * sz
            r = lax.rsqrt(jnp.mean(yz * yz, axis=-1, keepdims=True) + EPS)
            yhat = yz * r
            dynv = dyn_ref[:, gs].astype(F32)
            gy = dynv * gn_ref[:, gs]
            dgn_parts.append(jnp.sum(dynv * yhat, axis=0, keepdims=True))
            dyz = r * (gy - yhat * jnp.mean(gy * yhat, axis=-1, keepdims=True))
            dy = dyz * sz
            dz_ref[:, gs] = (dyz * y * _silu_grad(zz)).astype(dz_ref.dtype)
            tr_ref[EXP_ROWS:RED_ROWS, gs] = jnp.broadcast_to(
                jnp.sum(dy * xg, axis=0, keepdims=True), (8, GW))
            dx = dsk_ref[:, gs] * dy
            dwx = dwxs[g]
            dxdt = w_c[:, gs] * dwx
            tr_ref[0:BLK, gs] = dwx * wx[:, gs]
            dbg = _dot_nt(wx[:, gs], dpn)
            dzo = ecs_c[:, gs] * dy
            tr_ref[BLK:2 * BLK, gs] = dy * yoff
            dcg = _dot_nt(dzo, pg)
            new_dp.append(dec_c[:, gs] * dpn + _dot_tn(cg, dzo))
            tr_ref[3 * BLK:EXP_ROWS, gs] = jnp.broadcast_to(
                jnp.sum(dpn * pg, axis=0, keepdims=True), (8, GW))
            dyh = [jnp.where(low == (k % 2 == 0), dy[:, (k // 2) * BLK:(k // 2 + 1) * BLK], 0.0) for k in range(4)]
            dms = [_dot_nt(dyh[k], xdt[:, g * GW + (k // 2) * BLK:g * GW + (k // 2 + 1) * BLK]) for k in range(4)]
            accs = [_dot_tn(lm[k][1], dyh[k]) for k in range(4)]
            dcb = jnp.zeros((BLK, BLK), F32)
            for k in range(4):
                h = 4 * g + k
                lmat, mmat = lm[k]
                dm = jnp.where(tril, dms[k], 0.0)
                nh = dm * mmat
                dcs = dcs + jnp.where(lane == h, jnp.sum(nh, axis=1, keepdims=True), 0.0)
                dcst = dcst - jnp.where(rowi == h, jnp.sum(nh, axis=0, keepdims=True), 0.0)
                dcb = dcb + dm * lmat
            dxdt = dxdt + jnp.concatenate([accs[0] + accs[1], accs[2] + accs[3]], axis=1)
            dcg = dcg + _dot(dcb, bg)
            dbg = dbg + _dot_tn(dcb, cg)
            tr_ref[2 * BLK:3 * BLK, gs] = dxdt * xg
            dxbc_ref[:, gs] = dx + dxdt * dtp_c[:, gs]
            dxbc_ref[:, D_SSM + g * BLK:D_SSM + (g + 1) * BLK] = dbg
            dxbc_ref[:, D_SSM + 1024 + g * BLK:D_SSM + 1024 + (g + 1) * BLK] = dcg
        red = _dot(tr_ref[...], et_ref[...])
        r1, r2, r3 = red[0:BLK], red[BLK:2 * BLK], red[2 * BLK:3 * BLK]
        for g in range(SSD_GROUPS):
            dp_ref[g] = new_dp[g]
        dgn_ref[...] += jnp.concatenate(dgn_parts, axis=1)
        ddec = jnp.max(red[3 * BLK:EXP_ROWS], axis=0, keepdims=True)
        ddsk_ref[...] += jnp.max(red[EXP_ROWS:RED_ROWS], axis=0, keepdims=True)
        dcs = dcs + dcst.T - r1 + r2
        dcs_last = jnp.sum(r1, axis=0, keepdims=True) + ddec * dec
        dcs = dcs + jnp.where(rowi == BLK - 1, dcs_last, 0.0)
        dda = _cumsum_rev(dcs)
        ddtp = r3 + dda * av
        dalog_ref[...] += jnp.sum(dda * dtp, axis=0, keepdims=True) * av
        draw = jnp.where(valid, ddtp * _sigmoid(pre), 0.0)
        ddtb_ref[...] += jnp.sum(draw, axis=0, keepdims=True)
        draw_ref[...] = draw.astype(draw_ref.dtype)

    vec = lambda w_: pl.BlockSpec((1, w_), lambda n: (0, 0))
    rb = lambda w_: pl.BlockSpec((BLK, w_), lambda n: (nb - 1 - n, 0))
    e = _head_expand()
    res = _call(
        kern, (xbc, xbc, xbc, proj, proj, _pad128(dt_bias), _pad128(a_log),
               jnp.repeat(d_skip, HEAD_DIM).reshape(1, D_SSM), gate_norm.reshape(1, D_SSM), e, decay, e.T, st, dyn),
        name=name, grid=(nb,),
        in_specs=_ssd_specs(t, z_off, dt_off, True)
        + [pl.BlockSpec((D_SSM, BLK), lambda n: (0, 0)),
           pl.BlockSpec((1, SSD_GROUPS, BLK, GW), lambda n: (nb - 1 - n, 0, 0, 0)),
           rb(D_SSM)],
        out_specs=[rb(2 * D_SSM), rb(D_SSM), rb(BLK), vec(D_SSM), vec(BLK), vec(BLK), vec(BLK)],
        out_shape=[jax.ShapeDtypeStruct((t, 2 * D_SSM), F32), jax.ShapeDtypeStruct((t, D_SSM), MXU),
                   jax.ShapeDtypeStruct((t, BLK), MXU), jax.ShapeDtypeStruct((1, D_SSM), F32),
                   jax.ShapeDtypeStruct((1, BLK), F32), jax.ShapeDtypeStruct((1, BLK), F32),
                   jax.ShapeDtypeStruct((1, BLK), F32)],
        scratch_shapes=[pltpu.VMEM((SSD_GROUPS, BLK, GW), F32), pltpu.VMEM((RED_ROWS, D_SSM), F32)],
        sem=("arbitrary",), carry=carry)
    dxbc, dz, draw, dgn, ddsk, ddtb, dalog = res[:7]
    return [dxbc, dz, draw, dgn[0], ddsk[0, :SSD_HEADS], ddtb[0, :SSD_HEADS], dalog[0, :SSD_HEADS]] + res[7:]


def loss_fwd_bwd(h, target, *, name):
    t, d = h.shape
    tb = _conv_tile(t)
    ns = tb // BLK

    def kern(h_ref, *rest):
        t_refs, (loss_ref, dh_ref) = rest[:ns], rest[ns:]
        n = pl.program_id(0)
        part = jnp.zeros((1, 1), F32)
        for s in range(ns):
            rs = slice(s * BLK, (s + 1) * BLK)
            err = h_ref[rs, :] - t_refs[s][...]
            if s == 0:
                err = jnp.where(n > 0, err, 0.0)
            dh_ref[rs, :] = err * (1.0 / d)
            part = part + (0.5 / d) * jnp.sum(jnp.sum(err * err, axis=1, keepdims=True), axis=0, keepdims=True)

        @pl.when(n == 0)
        def _():
            loss_ref[...] = part

        @pl.when(n > 0)
        def _():
            loss_ref[...] += part

    return pl.pallas_call(
        kern, name=name, grid=(t // tb,),
        in_specs=[pl.BlockSpec((tb, d), lambda n: (n, 0))]
        + [pl.BlockSpec((BLK, d), lambda n, s=s: (jnp.maximum(n * ns + s - 1, 0), 0)) for s in range(ns)],
        out_specs=[pl.BlockSpec((1, 1), lambda n: (0, 0)), pl.BlockSpec((tb, d), lambda n: (n, 0))],
        out_shape=[jax.ShapeDtypeStruct((1, 1), F32), jax.ShapeDtypeStruct((t, d), F32)],
        compiler_params=_cp("arbitrary"),
    )(h, *([target] * ns))


def _ew_tile(r, c):
    cap = max(16, (256 * 1024) // c)
    best = None
    for tr in range(16, min(r, cap) + 1, 16):
        if r % tr == 0:
            best = tr
    return best if best is not None else r


def adamw(parts, w, m, v, *, name):
    npart, r, c = parts.shape
    tr = _ew_tile(r, c)

    def kern(p_ref, w_ref, m_ref, v_ref, g_ref, d_ref, m2_ref, v2_ref):
        g = p_ref[0].astype(F32)
        for k in range(1, npart):
            g = g + p_ref[k].astype(F32)
        m2 = ADAM_B1 * m_ref[...] + (1.0 - ADAM_B1) * g
        v2 = ADAM_B2 * v_ref[...] + (1.0 - ADAM_B2) * (g * g)
        m_hat = m2 / (1.0 - ADAM_B1 ** ADAM_STEP)
        v_hat = v2 / (1.0 - ADAM_B2 ** ADAM_STEP)
        g_ref[...] = g
        d_ref[...] = -ADAM_LR * (m_hat / (jnp.sqrt(v_hat) + ADAM_EPS) + ADAM_WD * w_ref[...])
        m2_ref[...] = m2
        v2_ref[...] = v2

    row = pl.BlockSpec((tr, c), lambda i: (i, 0))
    sds = jax.ShapeDtypeStruct((r, c), F32)
    return pl.pallas_call(
        kern, name=name, grid=(r // tr,),
        in_specs=[pl.BlockSpec((npart, tr, c), lambda i: (0, i, 0)), row, row, row],
        out_specs=[row, row, row, row], out_shape=[sds, sds, sds, sds],
        compiler_params=_cp("parallel"),
    )(parts, w, m, v)


def pair_add(p, land, *, name):
    _, r, c = p.shape
    tr = _ew_tile(r, c)
    core = lax.axis_index("c").astype(jnp.int32).reshape(1)

    def kern(c_ref, p_ref, l_ref, o_ref):
        o_ref[...] = (p_ref[...] + l_ref[...]).astype(o_ref.dtype)

    return pl.pallas_call(
        kern, name=name,
        grid_spec=pltpu.PrefetchScalarGridSpec(
            num_scalar_prefetch=1, grid=(4, r // tr),
            in_specs=[pl.BlockSpec((1, tr, c), lambda k, i, c_ref: (2 * k + c_ref[0], i, 0)),
                      pl.BlockSpec((1, tr, c), lambda k, i, c_ref: (k, i, 0))],
            out_specs=pl.BlockSpec((1, tr, c), lambda k, i, c_ref: (k, i, 0))),
        out_shape=jax.ShapeDtypeStruct((4, r, c), BF16),
        compiler_params=_cp("parallel", "parallel"),
    )(core, p, land)


def _me():
    return lax.axis_index("x"), lax.axis_index("y"), lax.axis_index("c")


def all_gather(xs, *, name):
    n = len(xs)

    def body(*refs):
        x_refs, out_refs = refs[:n], refs[n:2 * n]
        send_sems, recv_sems, local_sems = refs[2 * n:]
        mx, my, mc = _me()
        me, sib = (mx, my, mc), (mx, my, 1 - mc)
        chips = [(1 - mx, my), (mx, 1 - my), (1 - mx, 1 - my)]

        def rows(i, px, py, pc):
            return out_refs[i].at[4 * px + 2 * py + pc]

        def copy(i, k, block, to, src=None):
            return pltpu.make_async_remote_copy(
                src_ref=rows(i, *block) if src is None else src, dst_ref=rows(i, *block),
                send_sem=send_sems.at[7 * i + k], recv_sem=recv_sems.at[7 * i + k],
                device_id=to, device_id_type=MESH)

        mine = [pltpu.make_async_copy(x_refs[i], rows(i, *me), local_sems.at[i]) for i in range(n)]
        first = []
        for i in range(n):
            mine[i].start()
            first.append(copy(i, 0, me, sib, src=x_refs[i]))
            first += [copy(i, 1 + j, me, (*chip, mc), src=x_refs[i]) for j, chip in enumerate(chips)]
        for cp in first:
            cp.start()
        passed = []
        for i in range(n):
            for j, chip in enumerate(chips):
                copy(i, 1 + j, (*chip, mc), me).wait_recv()
                passed.append(copy(i, 4 + j, (*chip, mc), sib))
                passed[-1].start()
        for i in range(n):
            copy(i, 0, sib, me).wait_recv()
            for j, chip in enumerate(chips):
                copy(i, 4 + j, (*chip, 1 - mc), me).wait_recv()
        for cp in first + passed:
            cp.wait_send()
        for cp in mine:
            cp.wait()

    return pl.pallas_call(
        body, name=name,
        out_shape=[jax.ShapeDtypeStruct((N_DEV,) + x.shape, x.dtype) for x in xs],
        in_specs=[ANY] * n, out_specs=[ANY] * n,
        scratch_shapes=[pltpu.SemaphoreType.DMA((7 * n,)), pltpu.SemaphoreType.DMA((7 * n,)),
                        pltpu.SemaphoreType.DMA((n,))],
    )(*xs)


def pair_exchange(ps, *, name):
    n = len(ps)

    def body(*refs):
        p_refs, out_refs = refs[:n], refs[n:2 * n]
        send_sems, recv_sems = refs[2 * n:]
        mx, my, mc = _me()
        cps = [pltpu.make_async_remote_copy(
            src_ref=p_refs[i].at[2 * k + (1 - mc)], dst_ref=out_refs[i].at[k],
            send_sem=send_sems.at[4 * i + k], recv_sem=recv_sems.at[4 * i + k],
            device_id=(mx, my, 1 - mc), device_id_type=MESH) for i in range(n) for k in range(4)]
        for cp in cps:
            cp.start()
        for cp in cps:
            cp.wait_recv()
        for cp in cps:
            cp.wait_send()

    return pl.pallas_call(
        body, name=name,
        out_shape=[jax.ShapeDtypeStruct((4,) + p.shape[1:], p.dtype) for p in ps],
        in_specs=[ANY] * n, out_specs=[ANY] * n,
        scratch_shapes=[pltpu.SemaphoreType.DMA((4 * n,)), pltpu.SemaphoreType.DMA((4 * n,))],
    )(*ps)


def chip_exchange(qs, *, name):
    n = len(qs)

    def body(*refs):
        q_refs, out_refs = refs[:n], refs[n:2 * n]
        send_sems, recv_sems, local_sems = refs[2 * n:]
        mx, my, mc = _me()
        mine = 2 * mx + my
        chips = [(1 - mx, my), (mx, 1 - my), (1 - mx, 1 - my)]
        local, sends, recvs = [], [], []
        for i in range(n):
            local.append(pltpu.make_async_copy(q_refs[i].at[mine], out_refs[i].at[mine], local_sems.at[i]))
            for k, (px, py) in enumerate(chips):
                sems = dict(send_sem=send_sems.at[3 * i + k], recv_sem=recv_sems.at[3 * i + k],
                            device_id=(px, py, mc), device_id_type=MESH)
                sends.append(pltpu.make_async_remote_copy(
                    src_ref=q_refs[i].at[2 * px + py], dst_ref=out_refs[i].at[mine], **sems))
                recvs.append(pltpu.make_async_remote_copy(
                    src_ref=q_refs[i].at[mine], dst_ref=out_refs[i].at[2 * px + py], **sems))
        for cp in local + sends:
            cp.start()
        for cp in recvs:
            cp.wait_recv()
        for cp in sends:
            cp.wait_send()
        for cp in local:
            cp.wait()

    return pl.pallas_call(
        body, name=name,
        out_shape=[jax.ShapeDtypeStruct(q.shape, q.dtype) for q in qs],
        in_specs=[ANY] * n, out_specs=[ANY] * n,
        scratch_shapes=[pltpu.SemaphoreType.DMA((3 * n,)), pltpu.SemaphoreType.DMA((3 * n,)),
                        pltpu.SemaphoreType.DMA((n,))],
    )(*qs)


class _Carry:
    def __init__(self, inputs, out_shapes, sems, start, finish):
        self.inputs, self.out_shapes, self.sems = list(inputs), list(out_shapes), list(sems)
        self.start, self.finish = start, finish


def _call(kern, args, *, name, grid, in_specs, out_specs, out_shape, scratch_shapes=(), sem, carry=None):
    in_specs, out_specs, out_shape = list(in_specs), list(out_specs), list(out_shape)
    scratch_shapes = list(scratch_shapes)
    if carry is None:
        return list(pl.pallas_call(
            kern, name=name, grid=grid, in_specs=in_specs, out_specs=out_specs, out_shape=out_shape,
            scratch_shapes=scratch_shapes, compiler_params=_cp(*sem))(*args))
    ni, no, ns = len(in_specs), len(out_specs), len(scratch_shapes)
    ci, co = len(carry.inputs), len(carry.out_shapes)

    def body(*refs):
        o0 = ni + ci
        s0 = o0 + no + co
        ids = [pl.program_id(d) for d in range(len(grid))]
        first = functools.reduce(jnp.logical_and, [i == 0 for i in ids])
        last = functools.reduce(jnp.logical_and, [i == g - 1 for i, g in zip(ids, grid)])
        cin, cout, sems = refs[ni:o0], refs[o0 + no:s0], refs[s0 + ns:]

        @pl.when(first)
        def _():
            carry.start(cin, cout, sems)

        kern(*refs[:ni], *refs[o0:o0 + no], *refs[s0:s0 + ns])

        @pl.when(last)
        def _():
            carry.finish(cin, cout, sems)

    return list(pl.pallas_call(
        body, name=name, grid=grid, in_specs=in_specs + [ANY] * ci, out_specs=out_specs + [ANY] * co,
        out_shape=out_shape + carry.out_shapes, scratch_shapes=scratch_shapes + carry.sems,
        compiler_params=_cp(*(["arbitrary"] * len(grid))))(*args, *carry.inputs))


def merge_carries(cs):
    def split(seq, counts):
        out, off = [], 0
        for k in counts:
            out.append(seq[off:off + k])
            off += k
        return out

    def parts(cin, cout, sems):
        return zip(cs, split(cin, [len(c.inputs) for c in cs]), split(cout, [len(c.out_shapes) for c in cs]),
                   split(sems, [len(c.sems) for c in cs]))

    def start(cin, cout, sems):
        for c, i, o, s in parts(cin, cout, sems):
            c.start(i, o, s)

    def finish(cin, cout, sems):
        for c, i, o, s in parts(cin, cout, sems):
            c.finish(i, o, s)

    return _Carry(sum((c.inputs for c in cs), []), sum((c.out_shapes for c in cs), []),
                  sum((c.sems for c in cs), []), start, finish)


def gather_carry(xs):
    n = len(xs)

    def copies(cin, cout, sems, with_recv=True):
        mx, my, mc = _me()
        me = 4 * mx + 2 * my + mc
        peers = [(mx, my, 1 - mc), (1 - mx, my, mc), (mx, 1 - my, mc), (1 - mx, 1 - my, mc)]
        local, send, recv = [], [], []
        for i in range(n):
            local.append(pltpu.make_async_copy(cin[i], cout[i].at[me], sems[2].at[i]))
            for k, peer in enumerate(peers):
                common = dict(send_sem=sems[0].at[4 * i + k], recv_sem=sems[1].at[4 * i + k],
                              device_id=peer, device_id_type=MESH)
                send.append(pltpu.make_async_remote_copy(src_ref=cin[i], dst_ref=cout[i].at[me], **common))
                if with_recv:
                    recv.append(pltpu.make_async_remote_copy(
                        src_ref=cin[i], dst_ref=cout[i].at[4 * peer[0] + 2 * peer[1] + peer[2]], **common))
        return local, send, recv

    def start(cin, cout, sems):
        local, send, _ = copies(cin, cout, sems, with_recv=False)
        for cp in local + send:
            cp.start()

    def finish(cin, cout, sems):
        local, send, recv = copies(cin, cout, sems)
        for cp in recv:
            cp.wait_recv()
        for cp in send:
            cp.wait_send()
        for cp in local:
            cp.wait()

    return _Carry(xs, [jax.ShapeDtypeStruct((N_DEV,) + x.shape, x.dtype) for x in xs],
                  [pltpu.SemaphoreType.DMA((4 * n,)), pltpu.SemaphoreType.DMA((4 * n,)),
                   pltpu.SemaphoreType.DMA((n,))], start, finish)


def gather_relay(outs, *, name):
    n = len(outs)

    def body(*refs):
        bufs = refs[n:2 * n]
        send_sems, recv_sems = refs[2 * n:]
        mx, my, mc = _me()
        chips = [(1 - mx, my), (mx, 1 - my), (1 - mx, 1 - my)]
        send, recv = [], []
        for i in range(n):
            for j, (px, py) in enumerate(chips):
                common = dict(send_sem=send_sems.at[3 * i + j], recv_sem=recv_sems.at[3 * i + j],
                              device_id=(mx, my, 1 - mc), device_id_type=MESH)
                mine = bufs[i].at[4 * px + 2 * py + mc]
                send.append(pltpu.make_async_remote_copy(src_ref=mine, dst_ref=mine, **common))
                recv.append(pltpu.make_async_remote_copy(
                    src_ref=mine, dst_ref=bufs[i].at[4 * px + 2 * py + (1 - mc)], **common))
        for cp in send:
            cp.start()
        for cp in recv:
            cp.wait_recv()
        for cp in send:
            cp.wait_send()

    return pl.pallas_call(
        body, name=name, out_shape=[jax.ShapeDtypeStruct(o.shape, o.dtype) for o in outs],
        in_specs=[ANY] * n, out_specs=[ANY] * n, input_output_aliases={i: i for i in range(n)},
        scratch_shapes=[pltpu.SemaphoreType.DMA((3 * n,)), pltpu.SemaphoreType.DMA((3 * n,))],
    )(*outs)


def pair_carry(ps):
    n = len(ps)

    def copies(cin, cout, sems):
        mx, my, mc = _me()
        return [pltpu.make_async_remote_copy(
            src_ref=cin[i].at[2 * k + (1 - mc)], dst_ref=cout[i].at[k],
            send_sem=sems[0].at[4 * i + k], recv_sem=sems[1].at[4 * i + k],
            device_id=(mx, my, 1 - mc), device_id_type=MESH) for i in range(n) for k in range(4)]

    def start(cin, cout, sems):
        for cp in copies(cin, cout, sems):
            cp.start()

    def finish(cin, cout, sems):
        cps = copies(cin, cout, sems)
        for cp in cps:
            cp.wait_recv()
        for cp in cps:
            cp.wait_send()

    return _Carry(ps, [jax.ShapeDtypeStruct((4,) + p.shape[1:], p.dtype) for p in ps],
                  [pltpu.SemaphoreType.DMA((4 * n,)), pltpu.SemaphoreType.DMA((4 * n,))], start, finish)


def chip_carry(qs):
    n = len(qs)

    def copies(cin, cout, sems, with_recv=True):
        mx, my, mc = _me()
        mine = 2 * mx + my
        chips = [(1 - mx, my), (mx, 1 - my), (1 - mx, 1 - my)]
        local, send, recv = [], [], []
        for i in range(n):
            local.append(pltpu.make_async_copy(cin[i].at[mine], cout[i].at[mine], sems[2].at[i]))
            for k, (px, py) in enumerate(chips):
                common = dict(send_sem=sems[0].at[3 * i + k], recv_sem=sems[1].at[3 * i + k],
                              device_id=(px, py, mc), device_id_type=MESH)
                send.append(pltpu.make_async_remote_copy(
                    src_ref=cin[i].at[2 * px + py], dst_ref=cout[i].at[mine], **common))
                if with_recv:
                    recv.append(pltpu.make_async_remote_copy(
                        src_ref=cin[i].at[mine], dst_ref=cout[i].at[2 * px + py], **common))
        return local, send, recv

    def start(cin, cout, sems):
        local, send, _ = copies(cin, cout, sems, with_recv=False)
        for cp in local + send:
            cp.start()

    def finish(cin, cout, sems):
        local, send, recv = copies(cin, cout, sems)
        for cp in recv:
            cp.wait_recv()
        for cp in send:
            cp.wait_send()
        for cp in local:
            cp.wait()

    return _Carry(qs, [jax.ShapeDtypeStruct(q.shape, q.dtype) for q in qs],
                  [pltpu.SemaphoreType.DMA((3 * n,)), pltpu.SemaphoreType.DMA((3 * n,)),
                   pltpu.SemaphoreType.DMA((n,))], start, finish)


WEIGHTS = [
    "meta_tokens", "l0_mix_pre_norm", "l0_mix_post_norm", "l0_w_in", "l0_lru_conv_w", "l0_lru_conv_b",
    "l0_lru_w_a", "l0_lru_b_a", "l0_lru_w_x", "l0_lru_b_x", "l0_lru_lambda", "l0_attn_sinks", "l0_w_out",
    "l0_ffn_pre_norm", "l0_ffn_post_norm", "l0_ffn_w_up", "l0_ffn_conv_w", "l0_ffn_conv_b", "l0_ffn_w_down",
    "l1_mix_pre_norm", "l1_mix_post_norm", "l1_w_in", "l1_ssm_conv_w", "l1_ssm_conv_b", "l1_dt_bias",
    "l1_a_log", "l1_d_skip", "l1_gate_norm", "l1_w_out", "l1_ffn_pre_norm", "l1_ffn_post_norm",
    "l1_ffn_w_up", "l1_ffn_conv_w", "l1_ffn_conv_b", "l1_ffn_w_down",
]
INPUTS = (["x"] + WEIGHTS + ["loss_target"] + ["m_" + n for n in WEIGHTS] + ["v_" + n for n in WEIGHTS])

MATS = {"l0_w_in": ("col", (1024, 3328)), "l0_w_out": ("row", (2048, 1024)),
        "l0_ffn_w_up": ("col", (1024, 5632)), "l0_ffn_w_down": ("row", (2816, 1024)),
        "l1_w_in": ("col", (1024, 6176)), "l1_w_out": ("row", (2048, 1024)),
        "l1_ffn_w_up": ("col", (1024, 5632)), "l1_ffn_w_down": ("row", (2816, 1024))}
SMALL_SHARDED = {"meta_tokens": ("col", (16, 1024)), "l0_lru_conv_w": ("col", (4, 1024)),
                 "l0_ffn_conv_w": ("col", (3, 5632)), "l1_ssm_conv_w": ("col", (4, 4096)),
                 "l1_ffn_conv_w": ("col", (3, 5632))}
SHARDED = {**MATS, **SMALL_SHARDED}
REPLICATED = [n for n in WEIGHTS if n not in SHARDED]
SHAPES = {n: ((8, BLK, BLK) if n.endswith(("lru_w_a", "lru_w_x")) else (N_Q_HEADS,) if n.endswith("attn_sinks")
              else (2 * D_FF,) if n.endswith("ffn_conv_b") else (2 * D_SSM,) if n.endswith("ssm_conv_b")
              else (SSD_HEADS,) if n.endswith(("dt_bias", "a_log", "d_skip")) else (D_SSM,) if n.endswith("gate_norm")
              else (D_MODEL,)) for n in REPLICATED}
PACK_W = 1024
SMALL_W = 128


def _shard_shape(name):
    kind, (r, c) = SHARDED[name]
    return (r, c // N_DEV) if kind == "col" else (r // N_DEV, c)


def _rows_of(numel, width):
    return -(-numel // width)


def _to_rows(a, width):
    flat = a.reshape(-1)
    rows = _rows_of(flat.shape[0], width)
    return jnp.pad(flat, (0, rows * width - flat.shape[0])).reshape(rows, width)


def _pack(arrs, width, total_rows):
    slab = jnp.concatenate([_to_rows(a, width) for a in arrs], axis=0)
    return jnp.pad(slab, ((0, total_rows - slab.shape[0]), (0, 0)))


def _unpack(slab, shapes, width):
    out, off = [], 0
    for shp in shapes:
        numel = math.prod(shp)
        rows = _rows_of(numel, width)
        out.append(slab[off:off + rows].reshape(-1)[:numel].reshape(shp))
        off += rows
    return out


def _round_up(n, m):
    return -(-n // m) * m


def _by_dest(name, g):
    kind, (r, c) = SHARDED[name]
    if kind == "col":
        return g.reshape(r, N_DEV, c // N_DEV).transpose(1, 0, 2)
    return g.reshape(N_DEV, r // N_DEV, c)


def _from_shards(name, blocks):
    kind, (r, c) = SHARDED[name]
    return blocks.transpose(1, 0, 2).reshape(r, c) if kind == "col" else blocks.reshape(r, c)


L1_IN_PAD = 6272
FFN_CBLK = 1408
SSM_CBLK = 1024


def _ffn_fwd(h, a, w, pfx):
    u, ut = rmsnorm_fwd(h, a[pfx + "ffn_pre_norm"], out_dtype=MXU, name=pfx + "ffn_pre", with_t=True)
    up = matmul(u, w[pfx + "ffn_w_up"], name=pfx + "ffn_up")
    act, act_t = dwconv_fwd(up, a[pfx + "ffn_conv_w"], a[pfx + "ffn_conv_b"], mode="geglu", x_off=0,
                            c_out=D_FF, cblk=FFN_CBLK, out_dtype=MXU, name=pfx + "ffn_act", with_t=True)
    down = matmul(act, w[pfx + "ffn_w_down"], name=pfx + "ffn_down")
    out = rmsnorm_fwd(down, a[pfx + "ffn_post_norm"], res=h, out_dtype=F32, name=pfx + "ffn_post")
    return out, (h, ut, up, act_t, down)


def _dx_and_pair_stage(names, g, a_list, b, *, name):
    parts = [_by_dest(n, g[n]) for n in names]
    out, from_sibling = matmul_cat(a_list, b, trans_b=True, name=name, carry=pair_carry(parts))
    return out, [pair_add(p, l, name="rs_pair_add_" + n) for n, p, l in zip(names, parts, from_sibling)]


def _ffn_bwd(dh, saved, a, w, pfx, g, carry=None):
    h, ut, up, act_t, down = saved
    dd, g[pfx + "ffn_post_norm"] = rmsnorm_bwd(down, a[pfx + "ffn_post_norm"], dh, out_dtype=MXU,
                                               name=pfx + "ffn_post_bwd")
    dact = matmul(dd, w[pfx + "ffn_w_down"], trans_b=True, name=pfx + "ffn_down_dx")
    g[pfx + "ffn_w_down"] = matmul(act_t, dd, name=pfx + "ffn_down_dw")
    dups, g[pfx + "ffn_conv_w"], g[pfx + "ffn_conv_b"], carried = dwconv_bwd(
        up, a[pfx + "ffn_conv_w"], a[pfx + "ffn_conv_b"], dact, mode="geglu", x_off=0, c_out=D_FF,
        cblk=FFN_CBLK, name=pfx + "ffn_act_bwd", carry=carry)
    g[pfx + "ffn_w_up"] = jnp.concatenate(
        [matmul(ut, d, name=pfx + "ffn_up_dw%d" % i) for i, d in enumerate(dups)], axis=1)
    du, q = _dx_and_pair_stage([pfx + "ffn_w_down", pfx + "ffn_w_up"], g, dups, w[pfx + "ffn_w_up"],
                               name=pfx + "ffn_up_dx")
    dh_in, g[pfx + "ffn_pre_norm"] = rmsnorm_bwd(h, a[pfx + "ffn_pre_norm"], du, res=dh, out_dtype=F32,
                                                 name=pfx + "ffn_pre_bwd")
    return dh_in, carried, q


GATHER_EARLY = ["l0_w_out", "l0_ffn_w_up", "l0_ffn_w_down"]
GATHER_LATE = ["l1_w_in", "l1_w_out", "l1_ffn_w_up", "l1_ffn_w_down"]
RS_L1_FFN = ["l1_ffn_w_down", "l1_ffn_w_up"]
RS_L1_MIX = ["l1_w_out", "l1_w_in"]
RS_L0_FFN = ["l0_ffn_w_down", "l0_ffn_w_up"]
RS_LAST = ["l0_w_in", "l0_lru_conv_w", "l0_ffn_conv_w", "l1_ssm_conv_w", "l1_ffn_conv_w"]


REPL_LATE = ["l0_attn_sinks", "l0_mix_pre_norm"]
REPL_EARLY = [n for n in REPLICATED if n not in REPL_LATE]


def _local_step(a, shards):
    x = a["x"][0]
    seq = x.shape[0]
    h0 = jnp.concatenate([jnp.zeros((PAD, D_MODEL), F32), a["meta_tokens"], x], axis=0)
    g, landed = {}, {}

    u0, u0t, w_in0 = rmsnorm_fwd(h0, a["l0_mix_pre_norm"], out_dtype=MXU, name="l0_mix_pre", with_t=True,
                                 carry=gather_carry([shards["l0_w_in"]]))
    w = {"l0_w_in": _from_shards("l0_w_in", gather_relay([w_in0], name="gather_relay_first")[0])}
    proj0 = matmul(u0, w["l0_w_in"], name="l0_in")
    lru = (a["l0_lru_conv_w"], a["l0_lru_conv_b"], a["l0_lru_w_a"], a["l0_lru_b_a"], a["l0_lru_w_x"],
           a["l0_lru_b_x"], a["l0_lru_lambda"])
    ya, ya_t, hl, *early = lru_fwd(proj0, *lru, gate_off=0, xr_off=1024, name="l0_lru",
                                   carry=gather_carry([shards[n] for n in GATHER_EARLY]))
    yb, *late = attn_fwd(proj0, a["l0_attn_sinks"], q_off=2048, k_off=3072, v_off=3200, name="l0_attn",
                         carry=gather_carry([shards[n] for n in GATHER_LATE]))
    relayed = gather_relay(early + late, name="gather_relay")
    w = dict(w, **{n: _from_shards(n, blocks) for n, blocks in zip(GATHER_EARLY + GATHER_LATE, relayed)})
    w["l1_w_in"] = jnp.pad(w["l1_w_in"], ((0, 0), (0, L1_IN_PAD - w["l1_w_in"].shape[1])))
    o0 = matmul_cat([ya, yb], w["l0_w_out"], name="l0_out")
    h1 = rmsnorm_fwd(o0, a["l0_mix_post_norm"], res=h0, out_dtype=F32, name="l0_mix_post")
    h2, ffn0 = _ffn_fwd(h1, a, w, "l0_")

    u2, u2t = rmsnorm_fwd(h2, a["l1_mix_pre_norm"], out_dtype=MXU, name="l1_mix_pre", with_t=True)
    proj1 = matmul(u2, w["l1_w_in"], name="l1_in")
    xbc = dwconv_fwd(proj1, a["l1_ssm_conv_w"], a["l1_ssm_conv_b"], mode="silu", x_off=D_SSM,
                     c_out=2 * D_SSM, cblk=SSM_CBLK, out_dtype=F32, name="l1_ssm_conv")
    ssd = (a["l1_dt_bias"], a["l1_a_log"], a["l1_d_skip"], a["l1_gate_norm"])
    decay = ssd_decay(proj1, a["l1_dt_bias"], a["l1_a_log"], dt_off=3 * D_SSM, name="l1_ssd_decay")
    yn, yn_t, st = ssd_fwd(xbc, proj1, decay, *ssd, z_off=0, dt_off=3 * D_SSM, name="l1_ssd")
    o1 = matmul(yn, w["l1_w_out"], name="l1_out")
    h3 = rmsnorm_fwd(o1, a["l1_mix_post_norm"], res=h2, out_dtype=F32, name="l1_mix_post")
    h4, ffn1 = _ffn_fwd(h3, a, w, "l1_")

    loss, dh4 = loss_fwd_bwd(h4, a["loss_target"][0], name="loss")

    dh3, _, q_l1_ffn = _ffn_bwd(dh4, ffn1, a, w, "l1_", g)
    do1, g["l1_mix_post_norm"] = rmsnorm_bwd(o1, a["l1_mix_post_norm"], dh3, out_dtype=MXU,
                                             name="l1_mix_post_bwd")
    dyn = matmul(do1, w["l1_w_out"], trans_b=True, name="l1_out_dx")
    g["l1_w_out"] = matmul(yn_t, do1, name="l1_out_dw")
    (dxbc, dz, draw, g["l1_gate_norm"], g["l1_d_skip"], g["l1_dt_bias"], g["l1_a_log"], *got) = ssd_bwd(
        xbc, proj1, decay, st, dyn, *ssd, z_off=0, dt_off=3 * D_SSM, name="l1_ssd_bwd",
        carry=chip_carry(q_l1_ffn))
    landed.update(zip(RS_L1_FFN, got))
    (dxin,), g["l1_ssm_conv_w"], g["l1_ssm_conv_b"], _ = dwconv_bwd(
        proj1, a["l1_ssm_conv_w"], a["l1_ssm_conv_b"], dxbc, mode="silu", x_off=D_SSM,
        c_out=2 * D_SSM, cblk=SSM_CBLK, name="l1_ssm_conv_bwd")
    g["l1_w_in"] = jnp.concatenate(
        [matmul(u2t, dz, name="l1_in_dw_z"), matmul(u2t, dxin, name="l1_in_dw_x"),
         matmul(u2t, draw, name="l1_in_dw_dt")[:, :SSD_HEADS]], axis=1)
    du2, q_l1_mix = _dx_and_pair_stage(RS_L1_MIX, g, [dz, dxin, draw], w["l1_w_in"], name="l1_in_dx")
    dh2, g["l1_mix_pre_norm"] = rmsnorm_bwd(h2, a["l1_mix_pre_norm"], du2, res=dh3, out_dtype=F32,
                                            name="l1_mix_pre_bwd")

    dh1, got, q_l0_ffn = _ffn_bwd(dh2, ffn0, a, w, "l0_", g, carry=chip_carry(q_l1_mix))
    landed.update(zip(RS_L1_MIX, got))
    do0, g["l0_mix_post_norm"] = rmsnorm_bwd(o0, a["l0_mix_post_norm"], dh1, out_dtype=MXU,
                                             name="l0_mix_post_bwd")
    g["l0_w_out"] = jnp.concatenate([matmul(ya_t, do0, name="l0_out_dw_a"),
                                     matmul(yb.T, do0, name="l0_out_dw_b")], axis=0)
    dy, q_out = _dx_and_pair_stage(["l0_w_out"], g, [do0], w["l0_w_out"], name="l0_out_dx")
    (dgate, dxr, g["l0_lru_conv_w"], dcb, g["l0_lru_w_a"], dba, g["l0_lru_w_x"], dbx, dlam, *got) = lru_bwd(
        proj0, hl, dy, *lru, gate_off=0, xr_off=1024, dy_off=0, name="l0_lru_bwd", carry=chip_carry(q_out))
    landed["l0_w_out"] = got[0]
    g["l0_lru_conv_b"], g["l0_lru_b_a"], g["l0_lru_b_x"], g["l0_lru_lambda"] = dcb[0], dba[0], dbx[0], dlam[0]
    dq, dk, dv, g["l0_attn_sinks"], *got = attn_bwd(
        proj0, a["l0_attn_sinks"], dy, q_off=2048, k_off=3072, v_off=3200, dy_off=1024, name="l0_attn_bwd",
        carry=merge_carries([chip_carry(q_l0_ffn), gather_carry([_pack_repl(g, REPL_EARLY)])]))
    landed.update(zip(RS_L0_FFN, got[:2]))
    repl_early = gather_relay(got[2:], name="gather_relay_small_grads")[0]
    dproj0 = [dgate, dxr, dq, dk, dv]
    g["l0_w_in"] = jnp.concatenate(
        [matmul(u0t, d, name="l0_in_dw%d" % i) for i, d in enumerate(dproj0)], axis=1)
    du0, q_last = _dx_and_pair_stage(RS_LAST, g, dproj0, w["l0_w_in"], name="l0_in_dx")
    dh0, g["l0_mix_pre_norm"], *got = rmsnorm_bwd(h0, a["l0_mix_pre_norm"], du0, res=dh1, out_dtype=F32,
                                                  name="l0_mix_pre_bwd", carry=chip_carry(q_last))
    landed.update(zip(RS_LAST, got))
    g["meta_tokens"] = dh0[PAD:BLK]
    meta = _by_dest("meta_tokens", g["meta_tokens"])
    q_meta = pair_add(meta, pair_exchange([meta], name="rs_pair_meta")[0], name="rs_pair_add_meta_tokens")
    landed["meta_tokens"] = chip_exchange([q_meta], name="rs_chip_meta")[0]
    return loss[0, 0], dh0[BLK:].reshape(1, seq, D_MODEL), g, landed, repl_early


def _repl_rows(names):
    return _round_up(sum(_rows_of(math.prod(SHAPES[n]), SMALL_W) for n in names), 16)


def _pack_repl(vals, names):
    return _pack([vals[n] for n in names], SMALL_W, _repl_rows(names))


def kernel(*args):
    a = dict(zip(INPUTS, args))
    first = list(SMALL_SHARDED)
    full = {n: _from_shards(n, blocks) for n, blocks in
            zip(first, all_gather([a[n] for n in first], name="gather_first"))}
    shards = {n: a[n].astype(MXU) for n in MATS}
    loss_part, grad_x, g, landed, repl_early = _local_step({**a, **full}, shards)
    loss = lax.psum(loss_part, ("x", "y", "c"))

    sh_out = {n: adamw(landed[n], a[n], a["m_" + n], a["v_" + n], name="adamw_" + n) for n in SHARDED}

    repl_late = all_gather([_pack_repl(g, REPL_LATE)], name="gather_small_grads")[0]
    rp_out = [{}, {}, {}, {}]
    for names, parts in ((REPL_EARLY, repl_early), (REPL_LATE, repl_late)):
        res = adamw(parts, *[_pack_repl({n: a[p + n] for n in names}, names) for p in ("", "m_", "v_")],
                    name="adamw_replicated_%d" % len(names))
        for k in range(4):
            rp_out[k].update(zip(names, _unpack(res[k], [SHAPES[n] for n in names], SMALL_W)))

    outs = [loss, grad_x]
    for k in range(4):
        outs += [sh_out[n][k] if n in SHARDED else rp_out[k][n] for n in WEIGHTS]
    return tuple(outs)
```
